```python
import jax, jax.numpy as jnp
from jax import lax
import numpy as np

D_MODEL = 2048
BATCH = 8
SEQ = 4096
DEPTH = 2

HEAD_DIM = 64
A_HEADS = 8
A_WIDTH = A_HEADS * HEAD_DIM
CHUNK = 128
B_WIDTH = 768
CONV_WIDTH = 3
DILATION_PATTERNS = ((128, 1), (512, 4), (2048, 16))
C_HEADS_PER_PATTERN = 4
C_HEADS = C_HEADS_PER_PATTERN * len(DILATION_PATTERNS)
C_WIDTH = C_HEADS * HEAD_DIM
D_MIX = A_WIDTH + B_WIDTH + C_WIDTH
PROJ_SIZES = (A_WIDTH, A_WIDTH, B_WIDTH, B_WIDTH, B_WIDTH, C_WIDTH, C_WIDTH, C_WIDTH)
PROJ_SPLITS = tuple(int(s) for s in np.cumsum(PROJ_SIZES)[:-1])
D_IN_PROJ = sum(PROJ_SIZES)
D_FF = 4 * D_MODEL
EPS = 1e-6

kernel_name = 'hymba_style_sgu_shortconv_dilated_attn'


def rms_norm(x, g):
    x32 = x.astype(jnp.float32)
    y = x32 * lax.rsqrt(jnp.mean(x32 * x32, axis=-1, keepdims=True) + EPS)
    return (y * g.astype(jnp.float32)).astype(x.dtype)


def spatial_gating(u, v, w_s, b_s):
    bsz, s = u.shape[:2]
    n_chunks = s // CHUNK
    vc = v.reshape(bsz, n_chunks, CHUNK, A_HEADS, HEAD_DIM)
    w_causal = jnp.tril(w_s)
    mixed = jnp.einsum('hij,bcjhd->bcihd', w_causal, vc) + b_s.T[None, None, :, :, None]
    return u * mixed.reshape(bsz, s, A_WIDTH)


def short_gated_conv(b_gate, c_gate, xb, w_conv):
    z = c_gate * xb
    zp = jnp.pad(z, ((0, 0), (CONV_WIDTH - 1, 0), (0, 0)))
    s = z.shape[1]
    conv = sum(w_conv[i] * zp[:, i:i + s] for i in range(CONV_WIDTH))
    return b_gate * conv


def head_rms_norm(x, g):
    x32 = x.astype(jnp.float32)
    return x32 * lax.rsqrt(jnp.mean(x32 * x32, axis=-1, keepdims=True) + EPS) * g.astype(jnp.float32)


def dilated_window_attention(q, k, v, window, dilation):
    bsz, s, h, d = q.shape
    length = s // dilation
    blk = window // dilation
    nb = -(-length // blk)
    lp = nb * blk

    def to_sub(t):
        t = t.reshape(bsz, length, dilation, h, d).transpose(0, 2, 1, 3, 4)
        t = jnp.pad(t, ((0, 0), (0, 0), (0, lp - length), (0, 0), (0, 0)))
        return t.reshape(bsz, dilation, nb, blk, h, d)

    qb = to_sub(q)
    kb = to_sub(k)
    vb = to_sub(v.astype(jnp.float32))
    pad_prev = ((0, 0), (0, 0), (1, 0), (0, 0), (0, 0), (0, 0))
    kcat = jnp.concatenate([jnp.pad(kb, pad_prev)[:, :, :-1], kb], axis=3)
    vcat = jnp.concatenate([jnp.pad(vb, pad_prev)[:, :, :-1], vb], axis=3)

    scores = jnp.einsum('brnqhd,brnkhd->brnhqk', qb, kcat) * (HEAD_DIM ** -0.5)
    qi = jnp.arange(blk)[:, None]
    kj = jnp.arange(2 * blk)[None, :]
    band = (kj >= qi) & (kj <= qi + blk)
    has_prev = jnp.arange(nb)[:, None, None] > 0
    mask = band[None] & (has_prev | (kj >= blk)[None])
    scores = jnp.where(mask[None, None, :, None], scores, -jnp.inf)
    m = jnp.max(scores, axis=-1, keepdims=True)
    e = jnp.exp(scores - m)
    den = jnp.sum(e, axis=-1, keepdims=True)
    o = jnp.einsum('brnhqk,brnkhd->brnhqd', e, vcat) / den
    lse = (m + jnp.log(den))[..., 0]

    o = o.transpose(0, 1, 2, 4, 3, 5).reshape(bsz, dilation, lp, h, d)[:, :, :length]
    o = o.transpose(0, 2, 1, 3, 4).reshape(bsz, s, h, d)
    lse = lse.transpose(0, 1, 2, 4, 3).reshape(bsz, dilation, lp, h)[:, :, :length]
    lse = lse.transpose(0, 2, 1, 3).reshape(bsz, s, h)
    return o, lse


def dilated_mixture(q, k, v, q_g, k_g):
    bsz, s = q.shape[:2]
    q = head_rms_norm(q.reshape(bsz, s, C_HEADS, HEAD_DIM), q_g)
    k = head_rms_norm(k.reshape(bsz, s, C_HEADS, HEAD_DIM), k_g)
    v = v.reshape(bsz, s, C_HEADS, HEAD_DIM)
    outs, lses = [], []
    for g, (window, dilation) in enumerate(DILATION_PATTERNS):
        sl = slice(g * C_HEADS_PER_PATTERN, (g + 1) * C_HEADS_PER_PATTERN)
        o, lse = dilated_window_attention(q[:, :, sl], k[:, :, sl], v[:, :, sl], window, dilation)
        outs.append(o)
        lses.append(lse)
    alpha = jax.nn.softmax(jnp.stack(lses, axis=0), axis=0)
    y = jnp.stack(outs, axis=0) * alpha[..., None]
    return y.transpose(1, 2, 0, 3, 4).reshape(bsz, s, C_WIDTH).astype(v.dtype)


def _fwd_setup_inputs(seed: int = 0) -> dict:
    key = jax.random.key(seed)
    ks = jax.random.split(key, 12)
    n = jax.random.normal
    f32 = jnp.float32
    return {
        'x': n(ks[0], (BATCH, SEQ, D_MODEL), f32),
        'attn_norm': 1.0 + 0.02 * n(ks[1], (DEPTH, D_MODEL), f32),
        'w_in': n(ks[2], (DEPTH, D_MODEL, D_IN_PROJ), f32) * D_MODEL ** -0.5,
        'sgu_w': n(ks[3], (DEPTH, A_HEADS, CHUNK, CHUNK), f32) * CHUNK ** -0.5,
        'sgu_b': 1.0 + 0.1 * n(ks[4], (DEPTH, A_HEADS, CHUNK), f32),
        'conv_w': n(ks[5], (DEPTH, CONV_WIDTH, B_WIDTH), f32) * CONV_WIDTH ** -0.5,
        'q_norm': 1.0 + 0.02 * n(ks[6], (DEPTH, HEAD_DIM), f32),
        'k_norm': 1.0 + 0.02 * n(ks[7], (DEPTH, HEAD_DIM), f32),
        'w_out': n(ks[8], (DEPTH, D_MIX, D_MODEL), f32) * D_MIX ** -0.5,
        'mlp_norm': 1.0 + 0.02 * n(ks[9], (DEPTH, D_MODEL), f32),
        'w_mlp_in': n(ks[10], (DEPTH, D_MODEL, D_FF), f32) * D_MODEL ** -0.5,
        'w_mlp_out': n(ks[11], (DEPTH, D_FF, D_MODEL), f32) * D_FF ** -0.5,
    }


def _fwd_reference(x, attn_norm, w_in, sgu_w, sgu_b, conv_w, q_norm, k_norm, w_out,
              mlp_norm, w_mlp_in, w_mlp_out):
    for l in range(DEPTH):
        h = rms_norm(x, attn_norm[l])
        p = h @ w_in[l]
        a_u, a_v, b_b, b_c, b_x, q, k, v = jnp.split(p, PROJ_SPLITS, axis=-1)
        y_a = spatial_gating(a_u, a_v, sgu_w[l], sgu_b[l])
        y_b = short_gated_conv(b_b, b_c, b_x, conv_w[l])
        y_c = dilated_mixture(q, k, v, q_norm[l], k_norm[l])
        x = x + jnp.concatenate([y_a, y_b, y_c], axis=-1) @ w_out[l]
        h = rms_norm(x, mlp_norm[l])
        x = x + jnp.square(jax.nn.relu(h @ w_mlp_in[l])) @ w_mlp_out[l]
    return x


import jax as _jax
import jax.numpy as _jnp

TWIN_FORMAT = 'train_step'
FWD_PARAMS = ['x', 'attn_norm', 'w_in', 'sgu_w', 'sgu_b', 'conv_w', 'q_norm', 'k_norm', 'w_out', 'mlp_norm', 'w_mlp_in', 'w_mlp_out']
TWIN_WEIGHTS = ['attn_norm', 'w_in', 'sgu_w', 'sgu_b', 'conv_w', 'q_norm', 'k_norm', 'w_out', 'mlp_norm', 'w_mlp_in', 'w_mlp_out']
TWIN_DIFF_INPUT = 'x'
TWIN_INPUTS = ['x', 'attn_norm', 'w_in', 'sgu_w', 'sgu_b', 'conv_w', 'q_norm', 'k_norm', 'w_out', 'mlp_norm', 'w_mlp_in', 'w_mlp_out', 'loss_target', 'm_attn_norm', 'm_w_in', 'm_sgu_w', 'm_sgu_b', 'm_conv_w', 'm_q_norm', 'm_k_norm', 'm_w_out', 'm_mlp_norm', 'm_w_mlp_in', 'm_w_mlp_out', 'v_attn_norm', 'v_w_in', 'v_sgu_w', 'v_sgu_b', 'v_conv_w', 'v_q_norm', 'v_k_norm', 'v_w_out', 'v_mlp_norm', 'v_w_mlp_in', 'v_w_mlp_out']
TWIN_OUTPUTS = ['loss', 'grad_x', 'grad_attn_norm', 'grad_w_in', 'grad_sgu_w', 'grad_sgu_b', 'grad_conv_w', 'grad_q_norm', 'grad_k_norm', 'grad_w_out', 'grad_mlp_norm', 'grad_w_mlp_in', 'grad_w_mlp_out', 'delta_attn_norm', 'delta_w_in', 'delta_sgu_w', 'delta_sgu_b', 'delta_conv_w', 'delta_q_norm', 'delta_k_norm', 'delta_w_out', 'delta_mlp_norm', 'delta_w_mlp_in', 'delta_w_mlp_out', 'new_m_attn_norm', 'new_m_w_in', 'new_m_sgu_w', 'new_m_sgu_b', 'new_m_conv_w', 'new_m_q_norm', 'new_m_k_norm', 'new_m_w_out', 'new_m_mlp_norm', 'new_m_w_mlp_in', 'new_m_w_mlp_out', 'new_v_attn_norm', 'new_v_w_in', 'new_v_sgu_w', 'new_v_sgu_b', 'new_v_conv_w', 'new_v_q_norm', 'new_v_k_norm', 'new_v_w_out', 'new_v_mlp_norm', 'new_v_w_mlp_in', 'new_v_w_mlp_out']
TWIN_LEAF_KINDS = {'loss': 'loss', 'grad_x': 'grad_x', 'grad_attn_norm': 'grad_w', 'grad_w_in': 'grad_w', 'grad_sgu_w': 'grad_w', 'grad_sgu_b': 'grad_w', 'grad_conv_w': 'grad_w', 'grad_q_norm': 'grad_w', 'grad_k_norm': 'grad_w', 'grad_w_out': 'grad_w', 'grad_mlp_norm': 'grad_w', 'grad_w_mlp_in': 'grad_w', 'grad_w_mlp_out': 'grad_w', 'delta_attn_norm': 'delta_w', 'delta_w_in': 'delta_w', 'delta_sgu_w': 'delta_w', 'delta_sgu_b': 'delta_w', 'delta_conv_w': 'delta_w', 'delta_q_norm': 'delta_w', 'delta_k_norm': 'delta_w', 'delta_w_out': 'delta_w', 'delta_mlp_norm': 'delta_w', 'delta_w_mlp_in': 'delta_w', 'delta_w_mlp_out': 'delta_w', 'new_m_attn_norm': 'new_m', 'new_m_w_in': 'new_m', 'new_m_sgu_w': 'new_m', 'new_m_sgu_b': 'new_m', 'new_m_conv_w': 'new_m', 'new_m_q_norm': 'new_m', 'new_m_k_norm': 'new_m', 'new_m_w_out': 'new_m', 'new_m_mlp_norm': 'new_m', 'new_m_w_mlp_in': 'new_m', 'new_m_w_mlp_out': 'new_m', 'new_v_attn_norm': 'new_v', 'new_v_w_in': 'new_v', 'new_v_sgu_w': 'new_v', 'new_v_sgu_b': 'new_v', 'new_v_conv_w': 'new_v', 'new_v_q_norm': 'new_v', 'new_v_k_norm': 'new_v', 'new_v_w_out': 'new_v', 'new_v_mlp_norm': 'new_v', 'new_v_w_mlp_in': 'new_v', 'new_v_w_mlp_out': 'new_v'}


def _forward(args):
    return _fwd_reference(*[args[k] for k in FWD_PARAMS])


def _output_shape():
    def fwd():
        inp = _fwd_setup_inputs(0)
        return _fwd_reference(*[inp[k] for k in FWD_PARAMS])
    out = _jax.eval_shape(fwd)
    return out.shape, out.dtype

N_MICROBATCH = 1
ADAM_LR = 0.001
ADAM_B1 = 0.9
ADAM_B2 = 0.999
ADAM_EPS = 1e-08
ADAM_WD = 0.01
ADAM_STEP = 10
PER_EXAMPLE_BATCH_AXIS = {'x': 0, 'loss_target': 0}
SHARED_INPUTS = []
_WEIGHT_DTYPES = {'attn_norm': _jnp.float32, 'w_in': _jnp.float32, 'sgu_w': _jnp.float32, 'sgu_b': _jnp.float32, 'conv_w': _jnp.float32, 'q_norm': _jnp.float32, 'k_norm': _jnp.float32, 'w_out': _jnp.float32, 'mlp_norm': _jnp.float32, 'w_mlp_in': _jnp.float32, 'w_mlp_out': _jnp.float32}
MOMENT_SCALE = {'attn_norm': 2.617974e+01, 'w_in': 1.209233e+00, 'sgu_w': 5.103888e-01, 'sgu_b': 7.869251e+00, 'conv_w': 9.199815e+00, 'q_norm': 4.158016e-01, 'k_norm': 4.153924e-01, 'w_out': 1.615313e+00, 'mlp_norm': 4.854935e+01, 'w_mlp_in': 1.634956e+00, 'w_mlp_out': 7.368847e+00}


def _to_microbatches(a, axis):
    t = _jnp.moveaxis(a, axis, 0)
    t = t.reshape((N_MICROBATCH, t.shape[0] // N_MICROBATCH) + t.shape[1:])
    return _jnp.moveaxis(t, 1, axis + 1)


def setup_inputs(seed: int = 0) -> dict:
    inp = _fwd_setup_inputs(seed)
    key = _jax.random.fold_in(_jax.random.key(seed), 7919)
    shape, _ = _output_shape()
    out = dict(inp)
    out["loss_target"] = _jax.random.normal(_jax.random.fold_in(key, 0), shape, _jnp.float32)
    for i, name in enumerate(TWIN_WEIGHTS):
        w = inp[name].astype(_jnp.float32)
        if MOMENT_SCALE is None:
            s = _jnp.sqrt(_jnp.mean(_jnp.square(w)) + 1e-30)
        else:
            s = MOMENT_SCALE[name]
        km, kv = _jax.random.split(_jax.random.fold_in(key, i + 1))
        out[name] = w
        out["m_" + name] = s * _jax.random.normal(km, w.shape, _jnp.float32)
        out["v_" + name] = (s * s) * _jax.random.uniform(kv, w.shape, _jnp.float32, 0.5, 1.5)
    if N_MICROBATCH > 1:
        for name, axis in PER_EXAMPLE_BATCH_AXIS.items():
            out[name] = _to_microbatches(out[name], axis)
    return {'x': out['x'], 'attn_norm': out['attn_norm'], 'w_in': out['w_in'], 'sgu_w': out['sgu_w'], 'sgu_b': out['sgu_b'], 'conv_w': out['conv_w'], 'q_norm': out['q_norm'], 'k_norm': out['k_norm'], 'w_out': out['w_out'], 'mlp_norm': out['mlp_norm'], 'w_mlp_in': out['w_mlp_in'], 'w_mlp_out': out['w_mlp_out'], 'loss_target': out['loss_target'], 'm_attn_norm': out['m_attn_norm'], 'm_w_in': out['m_w_in'], 'm_sgu_w': out['m_sgu_w'], 'm_sgu_b': out['m_sgu_b'], 'm_conv_w': out['m_conv_w'], 'm_q_norm': out['m_q_norm'], 'm_k_norm': out['m_k_norm'], 'm_w_out': out['m_w_out'], 'm_mlp_norm': out['m_mlp_norm'], 'm_w_mlp_in': out['m_w_mlp_in'], 'm_w_mlp_out': out['m_w_mlp_out'], 'v_attn_norm': out['v_attn_norm'], 'v_w_in': out['v_w_in'], 'v_sgu_w': out['v_sgu_w'], 'v_sgu_b': out['v_sgu_b'], 'v_conv_w': out['v_conv_w'], 'v_q_norm': out['v_q_norm'], 'v_k_norm': out['v_k_norm'], 'v_w_out': out['v_w_out'], 'v_mlp_norm': out['v_mlp_norm'], 'v_w_mlp_in': out['v_w_mlp_in'], 'v_w_mlp_out': out['v_w_mlp_out']}


def _loss(weights, diff, rest, loss_target):
    with _jax.named_scope("forward"):
        args = {**rest, TWIN_DIFF_INPUT: diff, **{k: w.astype(_WEIGHT_DTYPES[k]) for k, w in weights.items()}}
        y = _forward(args)
    with _jax.named_scope("loss_head"):
        err = _jnp.square(y.astype(_jnp.float32) - loss_target)
        return 0.5 * _jnp.sum(_jnp.mean(err, axis=-1)) if err.ndim else 0.5 * err


def _adamw(w, g, m, v):
    m = ADAM_B1 * m + (1.0 - ADAM_B1) * g
    v = ADAM_B2 * v + (1.0 - ADAM_B2) * _jnp.square(g)
    m_hat = m / (1.0 - ADAM_B1 ** ADAM_STEP)
    v_hat = v / (1.0 - ADAM_B2 ** ADAM_STEP)
    delta = -ADAM_LR * (m_hat / (_jnp.sqrt(v_hat) + ADAM_EPS) + ADAM_WD * w)
    return delta, m, v


def reference(x, attn_norm, w_in, sgu_w, sgu_b, conv_w, q_norm, k_norm, w_out, mlp_norm, w_mlp_in, w_mlp_out, loss_target, m_attn_norm, m_w_in, m_sgu_w, m_sgu_b, m_conv_w, m_q_norm, m_k_norm, m_w_out, m_mlp_norm, m_w_mlp_in, m_w_mlp_out, v_attn_norm, v_w_in, v_sgu_w, v_sgu_b, v_conv_w, v_q_norm, v_k_norm, v_w_out, v_mlp_norm, v_w_mlp_in, v_w_mlp_out):
    given = dict(x=x, attn_norm=attn_norm, w_in=w_in, sgu_w=sgu_w, sgu_b=sgu_b, conv_w=conv_w, q_norm=q_norm, k_norm=k_norm, w_out=w_out, mlp_norm=mlp_norm, w_mlp_in=w_mlp_in, w_mlp_out=w_mlp_out, loss_target=loss_target, m_attn_norm=m_attn_norm, m_w_in=m_w_in, m_sgu_w=m_sgu_w, m_sgu_b=m_sgu_b, m_conv_w=m_conv_w, m_q_norm=m_q_norm, m_k_norm=m_k_norm, m_w_out=m_w_out, m_mlp_norm=m_mlp_norm, m_w_mlp_in=m_w_mlp_in, m_w_mlp_out=m_w_mlp_out, v_attn_norm=v_attn_norm, v_w_in=v_w_in, v_sgu_w=v_sgu_w, v_sgu_b=v_sgu_b, v_conv_w=v_conv_w, v_q_norm=v_q_norm, v_k_norm=v_k_norm, v_w_out=v_w_out, v_mlp_norm=v_mlp_norm, v_w_mlp_in=v_w_mlp_in, v_w_mlp_out=v_w_mlp_out)
    weights = {n: given[n] for n in TWIN_WEIGHTS}
    shared = {n: given[n] for n in SHARED_INPUTS}
    per_example = {n: given[n] for n in ['x']}
    grad_fn = _jax.value_and_grad(_loss, argnums=(0, 1))

    def one_microbatch(ex, loss_target):
        ex = dict(ex)
        diff = ex.pop(TWIN_DIFF_INPUT)
        return grad_fn(weights, diff, {**shared, **ex}, loss_target)

    if N_MICROBATCH == 1:
        loss, (grad_w, grad_x) = one_microbatch(per_example, given["loss_target"])
    else:
        def body(carry, xs):
            loss_sum, grad_sum = carry
            l_k, (gw_k, gx_k) = one_microbatch(xs[0], xs[1])
            with _jax.named_scope("update"):
                return (loss_sum + l_k, _jax.tree.map(_jnp.add, grad_sum, gw_k)), gx_k

        init = (_jnp.zeros((), _jnp.float32), _jax.tree.map(_jnp.zeros_like, weights))
        (loss, grad_w), grad_x = _jax.lax.scan(body, init, (per_example, given["loss_target"]))
    with _jax.named_scope("update"):
        delta_w, new_m, new_v = {}, {}, {}
        for n in TWIN_WEIGHTS:
            delta_w[n], new_m[n], new_v[n] = _adamw(weights[n], grad_w[n], given["m_" + n], given["v_" + n])
    return (loss, grad_x, *[grad_w[n] for n in TWIN_WEIGHTS], *[delta_w[n] for n in TWIN_WEIGHTS],
            *[new_m[n] for n in TWIN_WEIGHTS], *[new_v[n] for n in TWIN_WEIGHTS])
```

```python
import jax
import jax.numpy as jnp
from jax import lax
from jax.experimental import pallas as pl
from jax.experimental.pallas import tpu as pltpu

N_DEV = 8
HEAD_DIM = 64
CHUNK = 128
ATT_BLK = 128
DILATIONS = (1, 4, 16)
CONV_WIDTH = 3
EPS = 1e-6
ADAM_LR = 0.001
ADAM_B1 = 0.9
ADAM_B2 = 0.999
ADAM_EPS = 1e-08
ADAM_WD = 0.01
ADAM_STEP = 10
MXU_DTYPE = jnp.bfloat16
F32 = jnp.float32
LANE = 128
VMEM_LIMIT_BYTES = 56 * 1024 * 1024
NEG_INF = float("-inf")


def _cparams(*sem):
    return pltpu.CompilerParams(dimension_semantics=sem, vmem_limit_bytes=VMEM_LIMIT_BYTES)


def _sds(shape, dtype):
    return jax.ShapeDtypeStruct(shape, dtype)


def _fit(n, tile):
    while n % tile:
        tile //= 2
    return tile


def _exchange(srcs, scatter, name):
    n = len(srcs)
    out_shape = [_sds(s.shape if scatter else (N_DEV,) + s.shape, s.dtype) for s in srcs]

    def body(*refs):
        src, out = refs[:n], refs[n:2 * n]
        send_sems, recv_sems, local_sems = refs[2 * n:]
        x, y, c = lax.axis_index("x"), lax.axis_index("y"), lax.axis_index("c")
        me = 4 * x + 2 * y + c
        copies = []
        for t in range(n):
            mine = src[t].at[me] if scatter else src[t]
            local = pltpu.make_async_copy(mine, out[t].at[me], local_sems.at[t])
            local.start()
            copies.append(local)
            for k in range(1, N_DEV):
                px = (1 - x) if (k & 4) else x
                py = (1 - y) if (k & 2) else y
                pc = (1 - c) if (k & 1) else c
                peer = 4 * px + 2 * py + pc
                s = src[t].at[peer] if scatter else src[t]
                cp = pltpu.make_async_remote_copy(
                    src_ref=s, dst_ref=out[t].at[me],
                    send_sem=send_sems.at[t * (N_DEV - 1) + k - 1],
                    recv_sem=recv_sems.at[t * (N_DEV - 1) + k - 1],
                    device_id=(px, py, pc), device_id_type=pl.DeviceIdType.MESH)
                cp.start()
                copies.append(cp)
        for cp in copies:
            cp.wait()

    hbm = pl.BlockSpec(memory_space=pltpu.HBM)
    return pl.pallas_call(
        body, name=name, out_shape=out_shape,
        in_specs=[hbm] * n, out_specs=[hbm] * n,
        scratch_shapes=[pltpu.SemaphoreType.DMA((n * (N_DEV - 1),)),
                        pltpu.SemaphoreType.DMA((n * (N_DEV - 1),)),
                        pltpu.SemaphoreType.DMA((n,))],
    )(*srcs)


def _rmsnorm_fwd(x, g, name, tr=512):
    T, D = x.shape

    def body(x_ref, g_ref, h_ref, r_ref):
        xv = x_ref[...]
        r = lax.rsqrt(jnp.mean(xv * xv, axis=-1, keepdims=True) + EPS)
        h_ref[...] = (xv * r * g_ref[...]).astype(h_ref.dtype)
        r_ref[...] = r

    return pl.pallas_call(
        body, name=name, grid=(T // tr,),
        in_specs=[pl.BlockSpec((tr, D), lambda i: (i, 0)), pl.BlockSpec((1, D), lambda i: (0, 0))],
        out_specs=[pl.BlockSpec((tr, D), lambda i: (i, 0)), pl.BlockSpec((tr, 1), lambda i: (i, 0))],
        out_shape=[_sds((T, D), MXU_DTYPE), _sds((T, 1), F32)],
        compiler_params=_cparams("parallel"),
    )(x, g)


def _rmsnorm_bwd(dh, x, g, r, dres, name, tr=256):
    T, D = x.shape

    def body(dh_ref, x_ref, g_ref, r_ref, dres_ref, dx_ref, dxb_ref, dg_ref):
        @pl.when(pl.program_id(0) == 0)
        def _():
            dg_ref[...] = jnp.zeros_like(dg_ref)

        dh_v, xv, rv = dh_ref[...], x_ref[...], r_ref[...]
        gdy = dh_v * g_ref[...]
        mean_xg = jnp.mean(xv * gdy, axis=-1, keepdims=True)
        dx = dres_ref[...] + rv * gdy - xv * (rv * rv * rv) * mean_xg
        dx_ref[...] = dx
        dxb_ref[...] = dx.astype(dxb_ref.dtype)
        dg_ref[...] += jnp.sum(dh_v * xv * rv, axis=0, keepdims=True)

    row = lambda i: (i, 0)
    return pl.pallas_call(
        body, name=name, grid=(T // tr,),
        in_specs=[pl.BlockSpec((tr, D), row), pl.BlockSpec((tr, D), row), pl.BlockSpec((1, D), lambda i: (0, 0)),
                  pl.BlockSpec((tr, 1), row), pl.BlockSpec((tr, D), row)],
        out_specs=[pl.BlockSpec((tr, D), row), pl.BlockSpec((tr, D), row), pl.BlockSpec((1, D), lambda i: (0, 0))],
        out_shape=[_sds((T, D), F32), _sds((T, D), MXU_DTYPE), _sds((1, D), F32)],
        compiler_params=_cparams("arbitrary"),
    )(dh, x, g, r, dres)


def _mm_nn(a, b, name, out_dtype=F32, residual=None, relu2=False, tm=512, tn=512, tk=None):
    M, K = a.shape
    grouped = b.ndim == 3
    N = b.shape[0] * b.shape[2] if grouped else b.shape[1]
    tm, tn = _fit(M, tm), _fit(b.shape[2] if grouped else N, tn)
    tk = K if tk is None else tk
    nk = K // tk
    assert K % tk == 0
    if grouped:
        per = b.shape[2] // tn
        b_spec = pl.BlockSpec((None, tk, tn), lambda i, j, k: (j // per, k, j % per))
    else:
        b_spec = pl.BlockSpec((tk, tn), lambda i, j, k: (k, j))
    n_out = 2 if relu2 else 1

    def body(*refs):
        a_ref, b_ref = refs[0], refs[1]
        r_ref = refs[2] if residual is not None else None
        o = 3 if residual is not None else 2
        outs = refs[o:o + n_out]
        acc_ref = refs[o + n_out] if nk > 1 else None

        def finish(acc):
            if r_ref is not None:
                acc = acc + r_ref[...]
            outs[0][...] = acc.astype(outs[0].dtype)
            if relu2:
                rl = jnp.maximum(acc, 0.0)
                outs[1][...] = (rl * rl).astype(outs[1].dtype)

        prod = jnp.dot(a_ref[...], b_ref[...], preferred_element_type=F32)
        if nk == 1:
            finish(prod)
        else:
            k = pl.program_id(2)

            @pl.when(k == 0)
            def _():
                acc_ref[...] = prod

            @pl.when(k > 0)
            def _():
                acc_ref[...] += prod

            @pl.when(k == nk - 1)
            def _():
                finish(acc_ref[...])

    out_blk = pl.BlockSpec((tm, tn), lambda i, j, k: (i, j))
    in_specs = [pl.BlockSpec((tm, tk), lambda i, j, k: (i, k)), b_spec]
    args = [a, b]
    if residual is not None:
        in_specs.append(out_blk)
        args.append(residual)
    out_shape = [_sds((M, N), out_dtype)]
    if relu2:
        out_shape.append(_sds((M, N), MXU_DTYPE))
    outs = pl.pallas_call(
        body, name=name, grid=(M // tm, N // tn, nk),
        in_specs=in_specs, out_specs=[out_blk] * n_out, out_shape=out_shape,
        scratch_shapes=[pltpu.VMEM((tm, tn), F32)] if nk > 1 else [],
        compiler_params=_cparams("parallel", "parallel", "arbitrary"),
    )(*args)
    return outs if relu2 else outs[0]


def _mm_nt(a, b, name, out_dtype=F32, relu2_pre=None, tm=512, tn=512, tk=None):
    M, K = a.shape
    grouped = b.ndim == 3
    N = b.shape[1] if grouped else b.shape[0]
    tm, tn = _fit(M, tm), _fit(N, tn)
    tk = K if tk is None else tk
    nk = K // tk
    assert K % tk == 0
    if grouped:
        per = b.shape[2] // tk
        assert b.shape[2] % tk == 0
        b_spec = pl.BlockSpec((None, tn, tk), lambda i, j, k: (k // per, j, k % per))
    else:
        b_spec = pl.BlockSpec((tn, tk), lambda i, j, k: (j, k))

    def body(*refs):
        a_ref, b_ref = refs[0], refs[1]
        p_ref = refs[2] if relu2_pre is not None else None
        o = 3 if relu2_pre is not None else 2
        out_ref = refs[o]
        acc_ref = refs[o + 1] if nk > 1 else None

        def finish(acc):
            if p_ref is not None:
                acc = acc * (2.0 * jnp.maximum(p_ref[...], 0.0))
            out_ref[...] = acc.astype(out_ref.dtype)

        prod = lax.dot_general(a_ref[...], b_ref[...], (((1,), (1,)), ((), ())), preferred_element_type=F32)
        if nk == 1:
            finish(prod)
        else:
            k = pl.program_id(2)

            @pl.when(k == 0)
            def _():
                acc_ref[...] = prod

            @pl.when(k > 0)
            def _():
                acc_ref[...] += prod

            @pl.when(k == nk - 1)
            def _():
                finish(acc_ref[...])

    out_blk = pl.BlockSpec((tm, tn), lambda i, j, k: (i, j))
    in_specs = [pl.BlockSpec((tm, tk), lambda i, j, k: (i, k)), b_spec]
    args = [a, b]
    if relu2_pre is not None:
        in_specs.append(out_blk)
        args.append(relu2_pre)
    return pl.pallas_call(
        body, name=name, grid=(M // tm, N // tn, nk),
        in_specs=in_specs, out_specs=out_blk, out_shape=_sds((M, N), out_dtype),
        scratch_shapes=[pltpu.VMEM((tm, tn), F32)] if nk > 1 else [],
        compiler_params=_cparams("parallel", "parallel", "arbitrary"),
    )(*args)


def _mm_tn(a, b, name, groups=None, tm=512, tn=512, tk=1024):
    T, M = a.shape
    N = b.shape[1]
    tm, tn, tk = _fit(M, tm), _fit(N if groups is None else N // groups, tn), _fit(T, tk)
    nk = T // tk

    def body(a_ref, b_ref, out_ref, acc_ref):
        k = pl.program_id(2)
        prod = lax.dot_general(a_ref[...], b_ref[...], (((0,), (0,)), ((), ())), preferred_element_type=F32)

        @pl.when(k == 0)
        def _():
            acc_ref[...] = prod

        @pl.when(k > 0)
        def _():
            acc_ref[...] += prod

        @pl.when(k == nk - 1)
        def _():
            out_ref[...] = acc_ref[...].astype(out_ref.dtype)

    if groups is None:
        out_spec = pl.BlockSpec((tm, tn), lambda i, j, k: (i, j))
        out_shape = _sds((M, N), MXU_DTYPE)
    else:
        per = N // groups // tn
        out_spec = pl.BlockSpec((None, tm, tn), lambda i, j, k: (j // per, i, j % per))
        out_shape = _sds((groups, M, N // groups), MXU_DTYPE)
    return pl.pallas_call(
        body, name=name, grid=(M // tm, N // tn, nk),
        in_specs=[pl.BlockSpec((tk, tm), lambda i, j, k: (k, i)), pl.BlockSpec((tk, tn), lambda i, j, k: (k, j))],
        out_specs=out_spec, out_shape=out_shape,
        scratch_shapes=[pltpu.VMEM((tm, tn), F32)],
        compiler_params=_cparams("parallel", "parallel", "arbitrary"),
    )(a, b)


def _loss_and_grad(y, target, name, tr=512):
    T, D = y.shape

    def body(y_ref, t_ref, loss_ref, dx_ref, dxb_ref):
        @pl.when(pl.program_id(0) == 0)
        def _():
            loss_ref[...] = jnp.zeros_like(loss_ref)

        err = y_ref[...] - t_ref[...]
        loss_ref[...] += 0.5 * jnp.sum(jnp.mean(err * err, axis=-1, keepdims=True), axis=0, keepdims=True)
        dx = err * (1.0 / D)
        dx_ref[...] = dx
        dxb_ref[...] = dx.astype(dxb_ref.dtype)

    row = lambda i: (i, 0)
    return pl.pallas_call(
        body, name=name, grid=(T // tr,),
        in_specs=[pl.BlockSpec((tr, D), row), pl.BlockSpec((tr, D), row)],
        out_specs=[pl.BlockSpec((8, LANE), lambda i: (0, 0)), pl.BlockSpec((tr, D), row), pl.BlockSpec((tr, D), row)],
        out_shape=[_sds((8, LANE), F32), _sds((T, D), F32), _sds((T, D), MXU_DTYPE)],
        compiler_params=_cparams("arbitrary"),
    )(y, target)


def _sgu_mixed(v, w_ref, b_ref, col_head, n_heads):
    mixed = b_ref[...]
    for h in range(n_heads):
        full = jnp.dot(w_ref[h], v, preferred_element_type=F32)
        mixed = mixed + jnp.where(col_head == h, full, 0.0)
    return mixed


def _sgu_fwd(p, w_tril, bmat, name):
    T = p.shape[0]
    H = w_tril.shape[0]
    AW = H * HEAD_DIM

    def body(u_ref, v_ref, w_ref, b_ref, y_ref):
        col_head = lax.broadcasted_iota(jnp.int32, (CHUNK, AW), 1) // HEAD_DIM
        mixed = _sgu_mixed(v_ref[...].astype(MXU_DTYPE), w_ref, b_ref, col_head, H)
        y_ref[...] = (u_ref[...] * mixed).astype(y_ref.dtype)

    const3 = lambda c: (0, 0, 0)
    return pl.pallas_call(
        body, name=name, grid=(T // CHUNK,),
        in_specs=[pl.BlockSpec((CHUNK, AW), lambda c: (c, 0)), pl.BlockSpec((CHUNK, AW), lambda c: (c, 1)),
                  pl.BlockSpec((H, CHUNK, CHUNK), const3), pl.BlockSpec((CHUNK, AW), lambda c: (0, 0))],
        out_specs=pl.BlockSpec((CHUNK, AW), lambda c: (c, 0)),
        out_shape=_sds((T, AW), MXU_DTYPE),
        compiler_params=_cparams("parallel"),
    )(p, p, w_tril, bmat)


def _sgu_bwd(dymix, p, w_tril, w_tril_t, bmat, name):
    T = p.shape[0]
    H = w_tril.shape[0]
    AW = H * HEAD_DIM

    def body(dy_ref, u_ref, v_ref, w_ref, wt_ref, b_ref, du_ref, dv_ref, dw_ref, db_ref):
        @pl.when(pl.program_id(0) == 0)
        def _():
            dw_ref[...] = jnp.zeros_like(dw_ref)
            db_ref[...] = jnp.zeros_like(db_ref)

        col_head = lax.broadcasted_iota(jnp.int32, (CHUNK, AW), 1) // HEAD_DIM
        v = v_ref[...].astype(MXU_DTYPE)
        dy = dy_ref[...]
        mixed = _sgu_mixed(v, w_ref, b_ref, col_head, H)
        du_ref[...] = (dy * mixed).astype(du_ref.dtype)
        dm = dy * u_ref[...]
        db_ref[...] += dm
        dm_c = dm.astype(MXU_DTYPE)
        dv = jnp.zeros((CHUNK, AW), F32)
        for h in range(H):
            sel = col_head == h
            dv = dv + jnp.where(sel, jnp.dot(wt_ref[h], dm_c, preferred_element_type=F32), 0.0)
            dm_h = jnp.where(sel, dm, 0.0).astype(MXU_DTYPE)
            dw_ref[h] += lax.dot_general(dm_h, v, (((1,), (1,)), ((), ())), preferred_element_type=F32)
        dv_ref[...] = dv.astype(dv_ref.dtype)

    const3 = lambda c: (0, 0, 0)
    blk = pl.BlockSpec((CHUNK, AW), lambda c: (c, 0))
    return pl.pallas_call(
        body, name=name, grid=(T // CHUNK,),
        in_specs=[blk, blk, pl.BlockSpec((CHUNK, AW), lambda c: (c, 1)),
                  pl.BlockSpec((H, CHUNK, CHUNK), const3), pl.BlockSpec((H, CHUNK, CHUNK), const3),
                  pl.BlockSpec((CHUNK, AW), lambda c: (0, 0))],
        out_specs=[blk, blk, pl.BlockSpec((H, CHUNK, CHUNK), const3), pl.BlockSpec((CHUNK, AW), lambda c: (0, 0))],
        out_shape=[_sds((T, AW), MXU_DTYPE), _sds((T, AW), MXU_DTYPE), _sds((H, CHUNK, CHUNK), F32), _sds((CHUNK, AW), F32)],
        compiler_params=_cparams("arbitrary"),
    )(dymix, p, p, w_tril, w_tril_t, bmat)


def _shift_down(z, s, row):
    return jnp.where(row >= s, pltpu.roll(z, s, 0), 0.0)


def _shift_up(z, s, row, T):
    return jnp.where(row < T - s, pltpu.roll(z, T - s, 0), 0.0)


def _conv_fwd(p, w_conv, AW, name):
    T = p.shape[0]
    BW = w_conv.shape[1]
    nb = BW // LANE
    b0 = 2 * AW // LANE

    def body(b_ref, c_ref, x_ref, w_ref, y_ref):
        row = lax.broadcasted_iota(jnp.int32, (T, LANE), 0)
        z = c_ref[...] * x_ref[...]
        w0, w1, w2 = w_ref[0:1, :], w_ref[1:2, :], w_ref[2:3, :]
        conv = w2 * z + w1 * _shift_down(z, 1, row) + w0 * _shift_down(z, 2, row)
        y_ref[...] = (b_ref[...] * conv).astype(y_ref.dtype)

    return pl.pallas_call(
        body, name=name, grid=(nb,),
        in_specs=[pl.BlockSpec((T, LANE), lambda j: (0, b0 + j)), pl.BlockSpec((T, LANE), lambda j: (0, b0 + nb + j)),
                  pl.BlockSpec((T, LANE), lambda j: (0, b0 + 2 * nb + j)), pl.BlockSpec((CONV_WIDTH, LANE), lambda j: (0, j))],
        out_specs=pl.BlockSpec((T, LANE), lambda j: (0, j)),
        out_shape=_sds((T, BW), MXU_DTYPE),
        compiler_params=_cparams("parallel"),
    )(p, p, p, w_conv)


def _conv_bwd(dymix, p, w_conv, AW, name):
    T = p.shape[0]
    BW = w_conv.shape[1]
    nb = BW // LANE
    b0 = 2 * AW // LANE
    y0 = AW // LANE

    def body(dy_ref, b_ref, c_ref, x_ref, w_ref, db_ref, dc_ref, dxb_ref, dw_ref):
        row = lax.broadcasted_iota(jnp.int32, (T, LANE), 0)
        cv, xv, dy = c_ref[...], x_ref[...], dy_ref[...]
        w0, w1, w2 = w_ref[0:1, :], w_ref[1:2, :], w_ref[2:3, :]
        z = cv * xv
        z1 = _shift_down(z, 1, row)
        z2 = _shift_down(z, 2, row)
        conv = w2 * z + w1 * z1 + w0 * z2
        db_ref[...] = (dy * conv).astype(db_ref.dtype)
        dconv = dy * b_ref[...]
        dz = w2 * dconv + w1 * _shift_up(dconv, 1, row, T) + w0 * _shift_up(dconv, 2, row, T)
        dc_ref[...] = (dz * xv).astype(dc_ref.dtype)
        dxb_ref[...] = (dz * cv).astype(dxb_ref.dtype)
        dw_ref[0:1, :] = jnp.sum(dconv * z2, axis=0, keepdims=True)
        dw_ref[1:2, :] = jnp.sum(dconv * z1, axis=0, keepdims=True)
        dw_ref[2:3, :] = jnp.sum(dconv * z, axis=0, keepdims=True)

    col = lambda j: (0, j)
    return pl.pallas_call(
        body, name=name, grid=(nb,),
        in_specs=[pl.BlockSpec((T, LANE), lambda j: (0, y0 + j)),
                  pl.BlockSpec((T, LANE), lambda j: (0, b0 + j)), pl.BlockSpec((T, LANE), lambda j: (0, b0 + nb + j)),
                  pl.BlockSpec((T, LANE), lambda j: (0, b0 + 2 * nb + j)), pl.BlockSpec((CONV_WIDTH, LANE), col)],
        out_specs=[pl.BlockSpec((T, LANE), col)] * 3 + [pl.BlockSpec((CONV_WIDTH, LANE), col)],
        out_shape=[_sds((T, BW), MXU_DTYPE)] * 3 + [_sds((CONV_WIDTH, BW), F32)],
        compiler_params=_cparams("parallel"),
    )(dymix, p, p, p, w_conv)


def _head_sum(x, col_head, n_heads):
    out = jnp.zeros_like(x)
    for h in range(n_heads):
        sel = col_head == h
        out = jnp.where(sel, jnp.sum(jnp.where(sel, x, 0.0), axis=-1, keepdims=True), out)
    return out


def _head_col(x, sel):
    return jnp.max(jnp.where(sel, x, NEG_INF), axis=-1, keepdims=True)


def _head_norm(x, g, col_head, n_heads):
    r = lax.rsqrt(_head_sum(x * x, col_head, n_heads) * (1.0 / HEAD_DIM) + EPS)
    return x * r * g, r


def _head_norm_bwd(dy, x, g, r, col_head, n_heads):
    gdy = dy * g
    mean_xg = _head_sum(x * gdy, col_head, n_heads) * (1.0 / HEAD_DIM)
    return r * gdy - x * (r * r * r) * mean_xg, dy * x * r


def _attn_specs(d, PW, DPu, q0, nb):
    cur = lambda off: pl.BlockSpec((ATT_BLK, PW), lambda r, n: (n, r * DPu + off))
    prev = lambda off: pl.BlockSpec((ATT_BLK, PW), lambda r, n: (jnp.maximum(n - 1, 0), r * DPu + off))
    nxt = lambda off: pl.BlockSpec((ATT_BLK, PW), lambda r, n: (jnp.minimum(n + 1, nb - 1), r * DPu + off))
    return cur, prev, nxt


def _attn_fwd(p, qg, kg, g, d, DP, PW, q_start, name):
    T = p.shape[0]
    L = T // d
    nb = L // ATT_BLK
    HP = PW // HEAD_DIM
    DPu = DP // PW
    CW = 3 * PW
    q0 = q_start // PW + g
    k0, v0 = q0 + CW // PW, q0 + 2 * CW // PW
    scale = HEAD_DIM ** -0.5

    def body(q_ref, kc_ref, kp_ref, vc_ref, vp_ref, qg_ref, kg_ref, o_ref, lse_ref):
        n = pl.program_id(1)
        col_q = lax.broadcasted_iota(jnp.int32, (ATT_BLK, PW), 1) // HEAD_DIM
        col_k = lax.broadcasted_iota(jnp.int32, (2 * ATT_BLK, PW), 1) // HEAD_DIM
        qn, _ = _head_norm(q_ref[...], qg_ref[...], col_q, HP)
        kcat = jnp.concatenate([kp_ref[...], kc_ref[...]], axis=0)
        kn, _ = _head_norm(kcat, kg_ref[...], col_k, HP)
        kn = kn.astype(MXU_DTYPE)
        vcat = jnp.concatenate([vp_ref[...], vc_ref[...]], axis=0).astype(MXU_DTYPE)
        qi = lax.broadcasted_iota(jnp.int32, (ATT_BLK, 2 * ATT_BLK), 0)
        kj = lax.broadcasted_iota(jnp.int32, (ATT_BLK, 2 * ATT_BLK), 1)
        mask = (kj >= qi) & (kj <= qi + ATT_BLK) & ((n > 0) | (kj >= ATT_BLK))
        o = jnp.zeros((ATT_BLK, PW), F32)
        lse = jnp.zeros((ATT_BLK, PW), F32)
        for h in range(HP):
            sel = col_q == h
            qh = jnp.where(sel, qn, 0.0).astype(MXU_DTYPE)
            s = lax.dot_general(qh, kn, (((1,), (1,)), ((), ())), preferred_element_type=F32) * scale
            s = jnp.where(mask, s, NEG_INF)
            m = jnp.max(s, axis=-1, keepdims=True)
            e = jnp.exp(s - m)
            den = jnp.sum(e, axis=-1, keepdims=True)
            oh = jnp.dot(e.astype(MXU_DTYPE), vcat, preferred_element_type=F32) / den
            o = jnp.where(sel, oh, o)
            lse = jnp.where(sel, m + jnp.log(den), lse)
        o_ref[...] = o
        lse_ref[...] = lse

    cur, prev, _ = _attn_specs(d, PW, DPu, q0, nb)
    gain = pl.BlockSpec((1, PW), lambda r, n: (0, 0))
    out_blk = pl.BlockSpec((ATT_BLK, PW), lambda r, n: (n, r))
    pv = p.reshape(L, d * DP)
    o, lse = pl.pallas_call(
        body, name=name, grid=(d, nb),
        in_specs=[cur(q0), cur(k0), prev(k0), cur(v0), prev(v0), gain, gain],
        out_specs=[out_blk, out_blk],
        out_shape=[_sds((L, d * PW), F32), _sds((L, d * PW), F32)],
        compiler_params=_cparams("parallel", "parallel"),
    )(pv, pv, pv, pv, pv, qg, kg)
    return o.reshape(T, PW), lse.reshape(T, PW)


def _attn_bwd(p, qg, kg, o, lse, do, dlse, g, d, DP, PW, q_start, name):
    T = p.shape[0]
    L = T // d
    nb = L // ATT_BLK
    HP = PW // HEAD_DIM
    DPu = DP // PW
    CW = 3 * PW
    q0 = q_start // PW + g
    k0, v0 = q0 + CW // PW, q0 + 2 * CW // PW
    scale = HEAD_DIM ** -0.5
    B = ATT_BLK

    def body(qc_ref, qx_ref, kc_ref, kp_ref, vc_ref, vp_ref, qg_ref, kg_ref,
             oc_ref, ox_ref, lc_ref, lx_ref, doc_ref, dox_ref, dlc_ref, dlx_ref,
             dq_ref, dk_ref, dv_ref, dqg_ref, dkg_ref):
        r_id, n = pl.program_id(0), pl.program_id(1)

        @pl.when((r_id == 0) & (n == 0))
        def _():
            dqg_ref[...] = jnp.zeros_like(dqg_ref)
            dkg_ref[...] = jnp.zeros_like(dkg_ref)

        col1 = lax.broadcasted_iota(jnp.int32, (B, PW), 1) // HEAD_DIM
        col2 = lax.broadcasted_iota(jnp.int32, (2 * B, PW), 1) // HEAD_DIM
        qgv, kgv = qg_ref[...], kg_ref[...]

        q_raw = qc_ref[...]
        qn, q_r = _head_norm(q_raw, qgv, col1, HP)
        kcat_raw = jnp.concatenate([kp_ref[...], kc_ref[...]], axis=0)
        kn_cat, _ = _head_norm(kcat_raw, kgv, col2, HP)
        kn_cat = kn_cat.astype(MXU_DTYPE)
        vcat = jnp.concatenate([vp_ref[...], vc_ref[...]], axis=0).astype(MXU_DTYPE)
        do_c, lse_c, dl_c = doc_ref[...], lc_ref[...], dlc_ref[...]
        delta_c = _head_sum(do_c * oc_ref[...], col1, HP)
        qi = lax.broadcasted_iota(jnp.int32, (B, 2 * B), 0)
        kj = lax.broadcasted_iota(jnp.int32, (B, 2 * B), 1)
        mask1 = (kj >= qi) & (kj <= qi + B) & ((n > 0) | (kj >= B))
        dqn = jnp.zeros((B, PW), F32)
        for h in range(HP):
            sel = col1 == h
            qh = jnp.where(sel, qn, 0.0).astype(MXU_DTYPE)
            s = lax.dot_general(qh, kn_cat, (((1,), (1,)), ((), ())), preferred_element_type=F32) * scale
            pr = jnp.where(mask1, jnp.exp(s - _head_col(lse_c, sel)), 0.0)
            do_h = jnp.where(sel, do_c, 0.0).astype(MXU_DTYPE)
            dp_ = lax.dot_general(do_h, vcat, (((1,), (1,)), ((), ())), preferred_element_type=F32)
            ds = pr * (dp_ - _head_col(delta_c, sel) + _head_col(dl_c, sel)) * scale
            dqn = jnp.where(sel, jnp.dot(ds.astype(MXU_DTYPE), kn_cat, preferred_element_type=F32), dqn)
        dq, dqg_part = _head_norm_bwd(dqn, q_raw, qgv, q_r, col1, HP)
        dq_ref[...] = dq.astype(dq_ref.dtype)
        dqg_ref[0:1, :] += jnp.sum(dqg_part, axis=0, keepdims=True)

        k_raw = kc_ref[...]
        kn, k_r = _head_norm(k_raw, kgv, col1, HP)
        kn_c = kn.astype(MXU_DTYPE)
        v_c = vc_ref[...].astype(MXU_DTYPE)
        qcat_raw = jnp.concatenate([q_raw, qx_ref[...]], axis=0)
        qn_cat, _ = _head_norm(qcat_raw, qgv, col2, HP)
        do_cat = jnp.concatenate([do_c, dox_ref[...]], axis=0)
        o_cat = jnp.concatenate([oc_ref[...], ox_ref[...]], axis=0)
        lse_cat = jnp.concatenate([lse_c, lx_ref[...]], axis=0)
        dl_cat = jnp.concatenate([dl_c, dlx_ref[...]], axis=0)
        delta_cat = _head_sum(do_cat * o_cat, col2, HP)
        i2 = lax.broadcasted_iota(jnp.int32, (2 * B, B), 0)
        j2 = lax.broadcasted_iota(jnp.int32, (2 * B, B), 1)
        mask2 = ((i2 < B) & (j2 <= i2)) | ((i2 >= B) & (j2 >= i2 - B) & (n + 1 < nb))
        dkn = jnp.zeros((B, PW), F32)
        dv = jnp.zeros((B, PW), F32)
        for h in range(HP):
            sel1 = col1 == h
            sel2 = col2 == h
            qh = jnp.where(sel2, qn_cat, 0.0).astype(MXU_DTYPE)
            s = lax.dot_general(qh, kn_c, (((1,), (1,)), ((), ())), preferred_element_type=F32) * scale
            pr = jnp.where(mask2, jnp.exp(s - _head_col(lse_cat, sel2)), 0.0)
            do_h = jnp.where(sel2, do_cat, 0.0).astype(MXU_DTYPE)
            dv_h = lax.dot_general(pr.astype(MXU_DTYPE), do_h, (((0,), (0,)), ((), ())), preferred_element_type=F32)
            dv = jnp.where(sel1, dv_h, dv)
            dp_ = lax.dot_general(do_h, v_c, (((1,), (1,)), ((), ())), preferred_element_type=F32)
            ds = pr * (dp_ - _head_col(delta_cat, sel2) + _head_col(dl_cat, sel2)) * scale
            dk_h = lax.dot_general(ds.astype(MXU_DTYPE), qh, (((0,), (0,)), ((), ())), preferred_element_type=F32)
            dkn = jnp.where(sel1, dk_h, dkn)
        dk, dkg_part = _head_norm_bwd(dkn, k_raw, kgv, k_r, col1, HP)
        dk_ref[...] = dk.astype(dk_ref.dtype)
        dv_ref[...] = dv.astype(dv_ref.dtype)
        dkg_ref[0:1, :] += jnp.sum(dkg_part, axis=0, keepdims=True)

    cur, prev, nxt = _attn_specs(d, PW, DPu, q0, nb)
    gain = pl.BlockSpec((1, PW), lambda r, n: (0, 0))
    blk_c = pl.BlockSpec((B, PW), lambda r, n: (n, r))
    blk_x = pl.BlockSpec((B, PW), lambda r, n: (jnp.minimum(n + 1, nb - 1), r))
    acc = pl.BlockSpec((8, PW), lambda r, n: (0, 0))
    pv = p.reshape(L, d * DP)
    view = lambda a: a.reshape(L, d * PW)
    dq, dk, dv, dqg, dkg = pl.pallas_call(
        body, name=name, grid=(d, nb),
        in_specs=[cur(q0), nxt(q0), cur(k0), prev(k0), cur(v0), prev(v0), gain, gain,
                  blk_c, blk_x, blk_c, blk_x, blk_c, blk_x, blk_c, blk_x],
        out_specs=[blk_c, blk_c, blk_c, acc, acc],
        out_shape=[_sds((L, d * PW), MXU_DTYPE)] * 3 + [_sds((8, PW), F32)] * 2,
        compiler_params=_cparams("arbitrary", "arbitrary"),
    )(pv, pv, pv, pv, pv, pv, qg, kg, view(o), view(o), view(lse), view(lse), view(do), view(do), view(dlse), view(dlse))
    return dq.reshape(T, PW), dk.reshape(T, PW), dv.reshape(T, PW), dqg[0], dkg[0]


def _softmax3(lses):
    mx = jnp.maximum(jnp.maximum(lses[0], lses[1]), lses[2])
    ex = [jnp.exp(l - mx) for l in lses]
    inv = 1.0 / (ex[0] + ex[1] + ex[2])
    return [e * inv for e in ex]


def _mix_fwd(os_, lses, name, tr=512):
    T, PW = os_[0].shape

    def body(o0, o1, o2, l0, l1, l2, y_ref):
        alpha = _softmax3([l0[...], l1[...], l2[...]])
        for g, o_ref in enumerate((o0, o1, o2)):
            y_ref[:, g * PW:(g + 1) * PW] = (o_ref[...] * alpha[g]).astype(y_ref.dtype)

    blk = pl.BlockSpec((tr, PW), lambda i: (i, 0))
    return pl.pallas_call(
        body, name=name, grid=(T // tr,),
        in_specs=[blk] * 6, out_specs=pl.BlockSpec((tr, 3 * PW), lambda i: (i, 0)),
        out_shape=_sds((T, 3 * PW), MXU_DTYPE),
        compiler_params=_cparams("parallel"),
    )(*os_, *lses)


def _mix_bwd(dymix, os_, lses, c_start, name, tr=512):
    T, PW = os_[0].shape
    HP = PW // HEAD_DIM
    c0 = c_start // PW

    def body(d0, d1, d2, o0, o1, o2, l0, l1, l2, do0, do1, do2, dl0, dl1, dl2):
        col_head = lax.broadcasted_iota(jnp.int32, (tr, PW), 1) // HEAD_DIM
        alpha = _softmax3([l0[...], l1[...], l2[...]])
        dys = [d0[...], d1[...], d2[...]]
        dots = [_head_sum(dy * o_ref[...], col_head, HP) for dy, o_ref in zip(dys, (o0, o1, o2))]
        mean_dot = alpha[0] * dots[0] + alpha[1] * dots[1] + alpha[2] * dots[2]
        for g, (do_ref, dl_ref) in enumerate(((do0, dl0), (do1, dl1), (do2, dl2))):
            do_ref[...] = dys[g] * alpha[g]
            dl_ref[...] = alpha[g] * (dots[g] - mean_dot)

    blk = pl.BlockSpec((tr, PW), lambda i: (i, 0))
    dy_specs = [pl.BlockSpec((tr, PW), lambda i, g=g: (i, c0 + g)) for g in range(3)]
    outs = pl.pallas_call(
        body, name=name, grid=(T // tr,),
        in_specs=dy_specs + [blk] * 6, out_specs=[blk] * 6,
        out_shape=[_sds((T, PW), F32)] * 6,
        compiler_params=_cparams("parallel"),
    )(dymix, dymix, dymix, *os_, *lses)
    return outs[:3], outs[3:]


def _adamw_math(w, g, m, v):
    m2 = ADAM_B1 * m + (1.0 - ADAM_B1) * g
    v2 = ADAM_B2 * v + (1.0 - ADAM_B2) * (g * g)
    m_hat = m2 / (1.0 - ADAM_B1 ** ADAM_STEP)
    v_hat = v2 / (1.0 - ADAM_B2 ** ADAM_STEP)
    delta = -ADAM_LR * (m_hat / (jnp.sqrt(v_hat) + ADAM_EPS) + ADAM_WD * w)
    return delta, m2, v2


def _adamw_layer(layer, w, m, v, parts, prev, name, tr=256):
    _, R, C = w.shape
    tr = min(tr, R)

    def body(w_ref, m_ref, v_ref, p_ref, *rest):
        g_ref, d_ref, m2_ref, v2_ref = rest[-4:]
        g = p_ref[0].astype(F32)
        for j in range(1, N_DEV):
            g = g + p_ref[j].astype(F32)
        delta, m2, v2 = _adamw_math(w_ref[...], g, m_ref[...], v_ref[...])
        g_ref[...] = g
        d_ref[...] = delta
        m2_ref[...] = m2
        v2_ref[...] = v2

    lay = pl.BlockSpec((None, tr, C), lambda i: (layer, i, 0))
    in_specs = [lay, lay, lay, pl.BlockSpec((N_DEV, tr, C), lambda i: (0, i, 0))]
    args = [w, m, v, parts]
    aliases = {}
    if prev is not None:
        in_specs += [pl.BlockSpec(memory_space=pl.ANY)] * 4
        args += list(prev)
        aliases = {4 + i: i for i in range(4)}
    return pl.pallas_call(
        body, name=name, grid=(R // tr,),
        in_specs=in_specs, out_specs=[lay] * 4, out_shape=[_sds(w.shape, F32)] * 4,
        input_output_aliases=aliases,
        compiler_params=_cparams("parallel"),
    )(*args)


def _sum_parts(parts, name):
    _, R, C = parts.shape

    def body(p_ref, out_ref):
        g = p_ref[0]
        for j in range(1, N_DEV):
            g = g + p_ref[j]
        out_ref[...] = g

    return pl.pallas_call(
        body, name=name, grid=(1,),
        in_specs=[pl.BlockSpec((N_DEV, R, C), lambda i: (0, 0, 0))], out_specs=pl.BlockSpec((R, C), lambda i: (0, 0)),
        out_shape=_sds((R, C), F32), compiler_params=_cparams("arbitrary"),
    )(parts)


def _adamw_flat(w, g, m, v, name):
    R, C = w.shape

    def body(w_ref, g_ref, m_ref, v_ref, d_ref, m2_ref, v2_ref):
        delta, m2, v2 = _adamw_math(w_ref[...], g_ref[...], m_ref[...], v_ref[...])
        d_ref[...] = delta
        m2_ref[...] = m2
        v2_ref[...] = v2

    blk = pl.BlockSpec((R, C), lambda i: (0, 0))
    return pl.pallas_call(
        body, name=name, grid=(1,), in_specs=[blk] * 4, out_specs=[blk] * 3, out_shape=[_sds((R, C), F32)] * 3,
        compiler_params=_cparams("arbitrary"),
    )(w, g, m, v)


def _pack(arrays, rows_multiple=8):
    flat = []
    for a in arrays:
        a = a.reshape(-1).astype(F32)
        flat.append(jnp.pad(a, (0, (-a.shape[0]) % LANE)))
    flat = jnp.concatenate(flat)
    flat = jnp.pad(flat, (0, (-flat.shape[0]) % (LANE * rows_multiple)))
    return flat.reshape(-1, LANE)


def _unpack(packed, shapes):
    flat = packed.reshape(-1)
    out, off = [], 0
    for s in shapes:
        size = 1
        for dim in s:
            size *= dim
        out.append(flat[off:off + size].reshape(s))
        off += size + (-size) % LANE
    return out


def _layer_fwd(x, wts, dims):
    AW, BW, PW, DP = dims["AW"], dims["BW"], dims["PW"], dims["DP"]
    q_start = 2 * AW + 3 * BW
    h, r1 = _rmsnorm_fwd(x, wts["attn_norm"], "rmsnorm_fwd")
    p = _mm_nn(h, wts["w_in"], "proj_in", tn=dims["tn_in"])
    y_a = _sgu_fwd(p, wts["sgu_tril"], wts["sgu_bmat"], "sgu_fwd")
    y_b = _conv_fwd(p, wts["conv_w"], AW, "conv_fwd")
    os_, lses = [], []
    for g, d in enumerate(DILATIONS):
        o, lse = _attn_fwd(p, wts["q_gain"], wts["k_gain"], g, d, DP, PW, q_start, "attn_fwd_%d" % d)
        os_.append(o)
        lses.append(lse)
    y_c = _mix_fwd(os_, lses, "mix_fwd")
    ymix = jnp.concatenate([y_a, y_b, y_c], axis=1)
    x1 = _mm_nn(ymix, wts["w_out"], "proj_out", residual=x)
    h2, r2 = _rmsnorm_fwd(x1, wts["mlp_norm"], "rmsnorm_fwd")
    a, hid = _mm_nn(h2, wts["w_mlp_in"], "mlp_in", relu2=True)
    x2 = _mm_nn(hid, wts["w_mlp_out"], "mlp_out", residual=x1, tk=dims["tk_ff"])
    saved = dict(x=x, h=h, r1=r1, p=p, os=os_, lses=lses, ymix=ymix, x1=x1, h2=h2, r2=r2, a=a, hid=hid)
    return x2, saved


def _layer_bwd(dx, dxb, wts, saved, dims):
    AW, BW, PW, DP = dims["AW"], dims["BW"], dims["PW"], dims["DP"]
    q_start = 2 * AW + 3 * BW
    da = _mm_nt(dxb, wts["w_mlp_out"], "mlp_out_dgrad", out_dtype=MXU_DTYPE, relu2_pre=saved["a"])
    g_w2 = _mm_tn(saved["hid"], dxb, "mlp_out_wgrad")
    dh2 = _mm_nt(da, wts["w_mlp_in"], "mlp_in_dgrad", tk=dims["tk_ff_g"])
    g_w1 = _mm_tn(saved["h2"], da, "mlp_in_wgrad", groups=N_DEV)
    dx1, dx1b, g_mlp_norm = _rmsnorm_bwd(dh2, saved["x1"], wts["mlp_norm"], saved["r2"], dx, "rmsnorm_bwd")
    dymix = _mm_nt(dx1b, wts["w_out"], "proj_out_dgrad")
    g_wout = _mm_tn(saved["ymix"], dx1b, "proj_out_wgrad")
    p = saved["p"]
    du, dv, g_sgu_w, g_sgu_bmat = _sgu_bwd(dymix, p, wts["sgu_tril"], wts["sgu_tril_t"], wts["sgu_bmat"], "sgu_bwd")
    d_b, d_c, d_xb, g_conv = _conv_bwd(dymix, p, wts["conv_w"], AW, "conv_bwd")
    dos, dlses = _mix_bwd(dymix, saved["os"], saved["lses"], AW + BW, "mix_bwd")
    dqs, dks, dvs = [], [], []
    g_q = g_k = 0.0
    for g, d in enumerate(DILATIONS):
        dq, dk, dvv, dqg, dkg = _attn_bwd(p, wts["q_gain"], wts["k_gain"], saved["os"][g], saved["lses"][g],
                                          dos[g], dlses[g], g, d, DP, PW, q_start, "attn_bwd_%d" % d)
        dqs.append(dq)
        dks.append(dk)
        dvs.append(dvv)
        g_q = g_q + dqg.reshape(-1, HEAD_DIM).sum(0)
        g_k = g_k + dkg.reshape(-1, HEAD_DIM).sum(0)
    dp = jnp.concatenate([du, dv, d_b, d_c, d_xb] + dqs + dks + dvs, axis=1)
    dh = _mm_nt(dp, wts["w_in"], "proj_in_dgrad", tk=dims["tk_in_g"])
    g_win = _mm_tn(saved["h"], dp, "proj_in_wgrad", tn=dims["tn_in"])
    dx0, dx0b, g_attn_norm = _rmsnorm_bwd(dh, saved["x"], wts["attn_norm"], saved["r1"], dx1, "rmsnorm_bwd")
    H = AW // HEAD_DIM
    tril = jnp.tril(jnp.ones((CHUNK, CHUNK), F32))
    small = [g_attn_norm.reshape(-1), g_sgu_w * tril, g_sgu_bmat.reshape(CHUNK, H, HEAD_DIM).sum(-1).T,
             g_conv, g_q, g_k, g_mlp_norm.reshape(-1)]
    D = dx.shape[1]
    big = [g_win.reshape(D, N_DEV, DP // N_DEV).transpose(1, 0, 2),
           g_wout.reshape(N_DEV, -1, D), g_w1, g_w2.reshape(N_DEV, -1, D)]
    return dx0, dx0b, big, small


def kernel(x, attn_norm, w_in, sgu_w, sgu_b, conv_w, q_norm, k_norm, w_out, mlp_norm, w_mlp_in, w_mlp_out, loss_target, m_attn_norm, m_w_in, m_sgu_w, m_sgu_b, m_conv_w, m_q_norm, m_k_norm, m_w_out, m_mlp_norm, m_w_mlp_in, m_w_mlp_out, v_attn_norm, v_w_in, v_sgu_w, v_sgu_b, v_conv_w, v_q_norm, v_k_norm, v_w_out, v_mlp_norm, v_w_mlp_in, v_w_mlp_out):
    n_layers = attn_norm.shape[0]
    T, D = x.shape[1], x.shape[2]
    H = sgu_w.shape[1]
    AW = H * HEAD_DIM
    BW = conv_w.shape[2] * N_DEV
    DP = w_in.shape[2] * N_DEV
    DMIX = w_out.shape[1] * N_DEV
    DFF = w_mlp_in.shape[2] * N_DEV
    PW = (DMIX - AW - BW) // 3
    HP = PW // HEAD_DIM
    pick = lambda n, opts: next(o for o in opts if n % o == 0)
    dims = dict(AW=AW, BW=BW, PW=PW, DP=DP,
                tn_in=pick(DP, (512, 256, 128)), tk_ff=pick(DFF, (2048, 1024, 512)),
                tk_ff_g=pick(DFF // N_DEV, (1024, 512, 256, 128)), tk_in_g=pick(DP, (2816, 1024, 512, 256, 128)))
    me = 4 * lax.axis_index("x") + 2 * lax.axis_index("y") + lax.axis_index("c")

    conv_shard_shape = conv_w.shape
    gathered = _exchange(
        [w[l].astype(MXU_DTYPE) for l in range(n_layers) for w in (w_in, w_out, w_mlp_in, w_mlp_out)] + [_pack([conv_w])],
        scatter=False, name="gather_weights")
    conv_full = jnp.stack([_unpack(gathered[-1][j], [conv_shard_shape])[0] for j in range(N_DEV)], axis=2)
    conv_full = conv_full.reshape(n_layers, CONV_WIDTH, BW)

    tril = jnp.tril(jnp.ones((CHUNK, CHUNK), F32))
    layers = []
    for l in range(n_layers):
        g_in, g_out, g_1, g_2 = gathered[4 * l:4 * l + 4]
        w_tril = sgu_w[l] * tril
        layers.append(dict(
            attn_norm=attn_norm[l][None], mlp_norm=mlp_norm[l][None],
            w_in=g_in.transpose(1, 0, 2).reshape(D, DP), w_out=g_out.reshape(DMIX, D),
            w_mlp_in=g_1, w_mlp_out=g_2.reshape(DFF, D),
            sgu_tril=w_tril.astype(MXU_DTYPE), sgu_tril_t=w_tril.transpose(0, 2, 1).astype(MXU_DTYPE),
            sgu_bmat=jnp.repeat(sgu_b[l].T, HEAD_DIM, axis=1),
            conv_w=conv_full[l],
            q_gain=jnp.tile(q_norm[l], HP)[None], k_gain=jnp.tile(k_norm[l], HP)[None]))

    xs = x[0]
    saved = []
    for l in range(n_layers):
        xs, sv = _layer_fwd(xs, layers[l], dims)
        saved.append(sv)
    loss_blk, dx, dxb = _loss_and_grad(xs, loss_target[0], "loss")
    loss = lax.psum(loss_blk[0, 0], ("x", "y", "c"))

    parts = [None] * n_layers
    small = [None] * n_layers
    for l in reversed(range(n_layers)):
        dx, dxb, big, small[l] = _layer_bwd(dx, dxb, layers[l], saved[l], dims)
        parts[l] = _exchange(big, scatter=True, name="scatter_grads")
    grad_x = dx[None]

    big_w = (w_in, w_out, w_mlp_in, w_mlp_out)
    big_m = (m_w_in, m_w_out, m_w_mlp_in, m_w_mlp_out)
    big_v = (v_w_in, v_w_out, v_w_mlp_in, v_w_mlp_out)
    big_out = []
    for i, nm in enumerate(("w_in", "w_out", "w_mlp_in", "w_mlp_out")):
        res = None
        for l in reversed(range(n_layers)):
            res = _adamw_layer(l, big_w[i], big_m[i], big_v[i], parts[l][i], res, "adamw_" + nm)
        big_out.append(res)

    small_shapes = [s.shape for s in small[0]]
    gathered_small = _exchange([_pack([s for l in range(n_layers) for s in small[l]])], scatter=False, name="gather_small")[0]
    summed = _unpack(_sum_parts(gathered_small, "sum_small"), small_shapes * n_layers)
    ns = len(small_shapes)
    g_small = [jnp.stack([summed[l * ns + i] for l in range(n_layers)]) for i in range(ns)]
    g_attn_norm, g_sgu_w, g_sgu_b, g_conv_full, g_q, g_k, g_mlp_norm = g_small
    cs = conv_w.shape[2]
    g_conv = lax.dynamic_slice_in_dim(g_conv_full, me * cs, cs, axis=2)
    sm_w = (attn_norm, sgu_w, sgu_b, conv_w, q_norm, k_norm, mlp_norm)
    sm_m = (m_attn_norm, m_sgu_w, m_sgu_b, m_conv_w, m_q_norm, m_k_norm, m_mlp_norm)
    sm_v = (v_attn_norm, v_sgu_w, v_sgu_b, v_conv_w, v_q_norm, v_k_norm, v_mlp_norm)
    sm_g = (g_attn_norm, g_sgu_w, g_sgu_b, g_conv, g_q, g_k, g_mlp_norm)
    sm_res = _adamw_flat(_pack(sm_w), _pack(sm_g), _pack(sm_m), _pack(sm_v), "adamw_small")
    shapes = [w.shape for w in sm_w]
    sm_delta, sm_m2, sm_v2 = (_unpack(r, shapes) for r in sm_res)

    def ordered(small_list, big_kind):
        b = [big_out[i][big_kind] for i in range(4)]
        return [small_list[0], b[0], small_list[1], small_list[2], small_list[3], small_list[4], small_list[5],
                b[1], small_list[6], b[2], b[3]]

    return (loss, grad_x, *ordered(list(sm_g), 0), *ordered(sm_delta, 1), *ordered(sm_m2, 2), *ordered(sm_v2, 3))
```

```python
import jax
import jax.numpy as jnp
from jax import lax
from jax.experimental import pallas as pl
from jax.experimental.pallas import tpu as pltpu

N_DEV = 8
HEAD_DIM = 64
CHUNK = 128
ATT_BLK = 128
DILATIONS = (1, 4, 16)
CONV_WIDTH = 3
EPS = 1e-6
ADAM_LR = 0.001
ADAM_B1 = 0.9
ADAM_B2 = 0.999
ADAM_EPS = 1e-08
ADAM_WD = 0.01
ADAM_STEP = 10
MXU_DTYPE = jnp.bfloat16
F32 = jnp.float32
LANE = 128
VMEM_LIMIT_BYTES = 56 * 1024 * 1024
NEG_INF = float("-inf")


def _cparams(*sem):
    return pltpu.CompilerParams(dimension_semantics=sem, vmem_limit_bytes=VMEM_LIMIT_BYTES)


def _sds(shape, dtype):
    return jax.ShapeDtypeStruct(shape, dtype)


def _fit(n, tile):
    for t in range(min(tile, n) // LANE * LANE, 0, -LANE):
        if n % t == 0:
            return t
    return n


def _exchange(srcs, scatter, name):
    n = len(srcs)
    out_shape = [_sds(s.shape if scatter else (N_DEV,) + s.shape, s.dtype) for s in srcs]

    def body(*refs):
        src, out = refs[:n], refs[n:2 * n]
        send_sems, recv_sems, local_sems = refs[2 * n:]
        x, y, c = lax.axis_index("x"), lax.axis_index("y"), lax.axis_index("c")
        me = 4 * x + 2 * y + c
        copies = []
        for t in range(n):
            mine = src[t].at[me] if scatter else src[t]
            local = pltpu.make_async_copy(mine, out[t].at[me], local_sems.at[t])
            local.start()
            copies.append(local)
            for k in range(1, N_DEV):
                px = (1 - x) if (k & 4) else x
                py = (1 - y) if (k & 2) else y
                pc = (1 - c) if (k & 1) else c
                peer = 4 * px + 2 * py + pc
                s = src[t].at[peer] if scatter else src[t]
                cp = pltpu.make_async_remote_copy(
                    src_ref=s, dst_ref=out[t].at[me],
                    send_sem=send_sems.at[t * (N_DEV - 1) + k - 1],
                    recv_sem=recv_sems.at[t * (N_DEV - 1) + k - 1],
                    device_id=(px, py, pc), device_id_type=pl.DeviceIdType.MESH)
                cp.start()
                copies.append(cp)
        for cp in copies:
            cp.wait()

    hbm = pl.BlockSpec(memory_space=pltpu.HBM)
    return pl.pallas_call(
        body, name=name, out_shape=out_shape,
        in_specs=[hbm] * n, out_specs=[hbm] * n,
        scratch_shapes=[pltpu.SemaphoreType.DMA((n * (N_DEV - 1),)),
                        pltpu.SemaphoreType.DMA((n * (N_DEV - 1),)),
                        pltpu.SemaphoreType.DMA((n,))],
    )(*srcs)


def _rmsnorm_fwd(x, g, name, tr=512):
    T, D = x.shape

    def body(x_ref, g_ref, h_ref, r_ref):
        xv = x_ref[...]
        r = lax.rsqrt(jnp.mean(xv * xv, axis=-1, keepdims=True) + EPS)
        h_ref[...] = (xv * r * g_ref[...]).astype(h_ref.dtype)
        r_ref[...] = r

    return pl.pallas_call(
        body, name=name, grid=(T // tr,),
        in_specs=[pl.BlockSpec((tr, D), lambda i: (i, 0)), pl.BlockSpec((1, D), lambda i: (0, 0))],
        out_specs=[pl.BlockSpec((tr, D), lambda i: (i, 0)), pl.BlockSpec((tr, 1), lambda i: (i, 0))],
        out_shape=[_sds((T, D), MXU_DTYPE), _sds((T, 1), F32)],
        compiler_params=_cparams("parallel"),
    )(x, g)


def _rmsnorm_bwd(dh, x, g, r, dres, name, tr=256):
    T, D = x.shape

    def body(dh_ref, x_ref, g_ref, r_ref, dres_ref, dx_ref, dxb_ref, dg_ref):
        @pl.when(pl.program_id(0) == 0)
        def _():
            dg_ref[...] = jnp.zeros_like(dg_ref)

        dh_v, xv, rv = dh_ref[...], x_ref[...], r_ref[...]
        gdy = dh_v * g_ref[...]
        mean_xg = jnp.mean(xv * gdy, axis=-1, keepdims=True)
        dx = dres_ref[...] + rv * gdy - xv * (rv * rv * rv) * mean_xg
        dx_ref[...] = dx
        dxb_ref[...] = dx.astype(dxb_ref.dtype)
        dg_ref[...] += jnp.sum(dh_v * xv * rv, axis=0, keepdims=True)

    row = lambda i: (i, 0)
    return pl.pallas_call(
        body, name=name, grid=(T // tr,),
        in_specs=[pl.BlockSpec((tr, D), row), pl.BlockSpec((tr, D), row), pl.BlockSpec((1, D), lambda i: (0, 0)),
                  pl.BlockSpec((tr, 1), row), pl.BlockSpec((tr, D), row)],
        out_specs=[pl.BlockSpec((tr, D), row), pl.BlockSpec((tr, D), row), pl.BlockSpec((1, D), lambda i: (0, 0))],
        out_shape=[_sds((T, D), F32), _sds((T, D), MXU_DTYPE), _sds((1, D), F32)],
        compiler_params=_cparams("arbitrary"),
    )(dh, x, g, r, dres)


def _mm_nn(a, b, name, out_dtype=F32, residual=None, relu2=False, tm=512, tn=512, tk=None):
    M, K = a.shape
    grouped = b.ndim == 3
    N = b.shape[0] * b.shape[2] if grouped else b.shape[1]
    tm, tn = _fit(M, tm), _fit(b.shape[2] if grouped else N, tn)
    tk = K if tk is None else _fit(K, tk)
    nk = K // tk
    if grouped:
        per = b.shape[2] // tn
        b_spec = pl.BlockSpec((None, tk, tn), lambda i, j, k: (j // per, k, j % per))
    else:
        b_spec = pl.BlockSpec((tk, tn), lambda i, j, k: (k, j))
    n_out = 2 if relu2 else 1

    def body(*refs):
        a_ref, b_ref = refs[0], refs[1]
        r_ref = refs[2] if residual is not None else None
        o = 3 if residual is not None else 2
        outs = refs[o:o + n_out]
        acc_ref = refs[o + n_out] if nk > 1 else None

        def finish(acc):
            if r_ref is not None:
                acc = acc + r_ref[...]
            outs[0][...] = acc.astype(outs[0].dtype)
            if relu2:
                rl = jnp.maximum(acc, 0.0)
                outs[1][...] = (rl * rl).astype(outs[1].dtype)

        prod = jnp.dot(a_ref[...], b_ref[...], preferred_element_type=F32)
        if nk == 1:
            finish(prod)
        else:
            k = pl.program_id(2)

            @pl.when(k == 0)
            def _():
                acc_ref[...] = prod

            @pl.when(k > 0)
            def _():
                acc_ref[...] += prod

            @pl.when(k == nk - 1)
            def _():
                finish(acc_ref[...])

    out_blk = pl.BlockSpec((tm, tn), lambda i, j, k: (i, j))
    in_specs = [pl.BlockSpec((tm, tk), lambda i, j, k: (i, k)), b_spec]
    args = [a, b]
    if residual is not None:
        in_specs.append(out_blk)
        args.append(residual)
    out_shape = [_sds((M, N), out_dtype)]
    if relu2:
        out_shape.append(_sds((M, N), MXU_DTYPE))
    outs = pl.pallas_call(
        body, name=name, grid=(M // tm, N // tn, nk),
        in_specs=in_specs, out_specs=[out_blk] * n_out, out_shape=out_shape,
        scratch_shapes=[pltpu.VMEM((tm, tn), F32)] if nk > 1 else [],
        compiler_params=_cparams("parallel", "parallel", "arbitrary"),
    )(*args)
    return outs if relu2 else outs[0]


def _mm_nt(a, b, name, out_dtype=F32, relu2_pre=None, tm=512, tn=512, tk=None):
    M, K = a.shape
    grouped = b.ndim == 3
    N = b.shape[1] if grouped else b.shape[0]
    tm, tn = _fit(M, tm), _fit(N, tn)
    tk = K if tk is None else _fit(b.shape[2] if grouped else K, tk)
    nk = K // tk
    if grouped:
        per = b.shape[2] // tk
        b_spec = pl.BlockSpec((None, tn, tk), lambda i, j, k: (k // per, j, k % per))
    else:
        b_spec = pl.BlockSpec((tn, tk), lambda i, j, k: (j, k))

    def body(*refs):
        a_ref, b_ref = refs[0], refs[1]
        p_ref = refs[2] if relu2_pre is not None else None
        o = 3 if relu2_pre is not None else 2
        out_ref = refs[o]
        acc_ref = refs[o + 1] if nk > 1 else None

        def finish(acc):
            if p_ref is not None:
                acc = acc * (2.0 * jnp.maximum(p_ref[...], 0.0))
            out_ref[...] = acc.astype(out_ref.dtype)

        prod = lax.dot_general(a_ref[...], b_ref[...], (((1,), (1,)), ((), ())), preferred_element_type=F32)
        if nk == 1:
            finish(prod)
        else:
            k = pl.program_id(2)

            @pl.when(k == 0)
            def _():
                acc_ref[...] = prod

            @pl.when(k > 0)
            def _():
                acc_ref[...] += prod

            @pl.when(k == nk - 1)
            def _():
                finish(acc_ref[...])

    out_blk = pl.BlockSpec((tm, tn), lambda i, j, k: (i, j))
    in_specs = [pl.BlockSpec((tm, tk), lambda i, j, k: (i, k)), b_spec]
    args = [a, b]
    if relu2_pre is not None:
        in_specs.append(out_blk)
        args.append(relu2_pre)
    return pl.pallas_call(
        body, name=name, grid=(M // tm, N // tn, nk),
        in_specs=in_specs, out_specs=out_blk, out_shape=_sds((M, N), out_dtype),
        scratch_shapes=[pltpu.VMEM((tm, tn), F32)] if nk > 1 else [],
        compiler_params=_cparams("parallel", "parallel", "arbitrary"),
    )(*args)


def _mm_tn(a, b, name, groups=None, tm=512, tn=512, tk=1024):
    T, M = a.shape
    N = b.shape[1]
    tm, tn, tk = _fit(M, tm), _fit(N if groups is None else N // groups, tn), _fit(T, tk)
    nk = T // tk

    def body(a_ref, b_ref, out_ref, acc_ref):
        k = pl.program_id(2)
        prod = lax.dot_general(a_ref[...], b_ref[...], (((0,), (0,)), ((), ())), preferred_element_type=F32)

        @pl.when(k == 0)
        def _():
            acc_ref[...] = prod

        @pl.when(k > 0)
        def _():
            acc_ref[...] += prod

        @pl.when(k == nk - 1)
        def _():
            out_ref[...] = acc_ref[...].astype(out_ref.dtype)

    if groups is None:
        out_spec = pl.BlockSpec((tm, tn), lambda i, j, k: (i, j))
        out_shape = _sds((M, N), MXU_DTYPE)
    else:
        per = N // groups // tn
        out_spec = pl.BlockSpec((None, tm, tn), lambda i, j, k: (j // per, i, j % per))
        out_shape = _sds((groups, M, N // groups), MXU_DTYPE)
    return pl.pallas_call(
        body, name=name, grid=(M // tm, N // tn, nk),
        in_specs=[pl.BlockSpec((tk, tm), lambda i, j, k: (k, i)), pl.BlockSpec((tk, tn), lambda i, j, k: (k, j))],
        out_specs=out_spec, out_shape=out_shape,
        scratch_shapes=[pltpu.VMEM((tm, tn), F32)],
        compiler_params=_cparams("parallel", "parallel", "arbitrary"),
    )(a, b)


def _loss_and_grad(y, target, name, tr=512):
    T, D = y.shape

    def body(y_ref, t_ref, loss_ref, dx_ref, dxb_ref):
        @pl.when(pl.program_id(0) == 0)
        def _():
            loss_ref[...] = jnp.zeros_like(loss_ref)

        err = y_ref[...] - t_ref[...]
        loss_ref[...] += 0.5 * jnp.sum(jnp.mean(err * err, axis=-1, keepdims=True), axis=0, keepdims=True)
        dx = err * (1.0 / D)
        dx_ref[...] = dx
        dxb_ref[...] = dx.astype(dxb_ref.dtype)

    row = lambda i: (i, 0)
    return pl.pallas_call(
        body, name=name, grid=(T // tr,),
        in_specs=[pl.BlockSpec((tr, D), row), pl.BlockSpec((tr, D), row)],
        out_specs=[pl.BlockSpec((8, LANE), lambda i: (0, 0)), pl.BlockSpec((tr, D), row), pl.BlockSpec((tr, D), row)],
        out_shape=[_sds((8, LANE), F32), _sds((T, D), F32), _sds((T, D), MXU_DTYPE)],
        compiler_params=_cparams("arbitrary"),
    )(y, target)


def _sgu_mixed(v, w_ref, b_ref, col_head, n_heads):
    mixed = b_ref[...]
    for h in range(n_heads):
        full = jnp.dot(w_ref[h], v, preferred_element_type=F32)
        mixed = mixed + jnp.where(col_head == h, full, 0.0)
    return mixed


def _sgu_fwd(p, w_tril, bmat, name):
    T = p.shape[0]
    H = w_tril.shape[0]
    AW = H * HEAD_DIM

    def body(u_ref, v_ref, w_ref, b_ref, y_ref):
        col_head = lax.broadcasted_iota(jnp.int32, (CHUNK, AW), 1) // HEAD_DIM
        mixed = _sgu_mixed(v_ref[...].astype(MXU_DTYPE), w_ref, b_ref, col_head, H)
        y_ref[...] = (u_ref[...] * mixed).astype(y_ref.dtype)

    const3 = lambda c: (0, 0, 0)
    return pl.pallas_call(
        body, name=name, grid=(T // CHUNK,),
        in_specs=[pl.BlockSpec((CHUNK, AW), lambda c: (c, 0)), pl.BlockSpec((CHUNK, AW), lambda c: (c, 1)),
                  pl.BlockSpec((H, CHUNK, CHUNK), const3), pl.BlockSpec((CHUNK, AW), lambda c: (0, 0))],
        out_specs=pl.BlockSpec((CHUNK, AW), lambda c: (c, 0)),
        out_shape=_sds((T, AW), MXU_DTYPE),
        compiler_params=_cparams("parallel"),
    )(p, p, w_tril, bmat)


def _sgu_bwd(dymix, p, w_tril, w_tril_t, bmat, name):
    T = p.shape[0]
    H = w_tril.shape[0]
    AW = H * HEAD_DIM

    def body(dy_ref, u_ref, v_ref, w_ref, wt_ref, b_ref, du_ref, dv_ref, dw_ref, db_ref):
        @pl.when(pl.program_id(0) == 0)
        def _():
            dw_ref[...] = jnp.zeros_like(dw_ref)
            db_ref[...] = jnp.zeros_like(db_ref)

        col_head = lax.broadcasted_iota(jnp.int32, (CHUNK, AW), 1) // HEAD_DIM
        v = v_ref[...].astype(MXU_DTYPE)
        dy = dy_ref[...]
        mixed = _sgu_mixed(v, w_ref, b_ref, col_head, H)
        du_ref[...] = (dy * mixed).astype(du_ref.dtype)
        dm = dy * u_ref[...]
        db_ref[...] += dm
        dm_c = dm.astype(MXU_DTYPE)
        dv = jnp.zeros((CHUNK, AW), F32)
        for h in range(H):
            sel = col_head == h
            dv = dv + jnp.where(sel, jnp.dot(wt_ref[h], dm_c, preferred_element_type=F32), 0.0)
            dm_h = jnp.where(sel, dm, 0.0).astype(MXU_DTYPE)
            dw_ref[h] += lax.dot_general(dm_h, v, (((1,), (1,)), ((), ())), preferred_element_type=F32)
        dv_ref[...] = dv.astype(dv_ref.dtype)

    const3 = lambda c: (0, 0, 0)
    blk = pl.BlockSpec((CHUNK, AW), lambda c: (c, 0))
    return pl.pallas_call(
        body, name=name, grid=(T // CHUNK,),
        in_specs=[blk, blk, pl.BlockSpec((CHUNK, AW), lambda c: (c, 1)),
                  pl.BlockSpec((H, CHUNK, CHUNK), const3), pl.BlockSpec((H, CHUNK, CHUNK), const3),
                  pl.BlockSpec((CHUNK, AW), lambda c: (0, 0))],
        out_specs=[blk, blk, pl.BlockSpec((H, CHUNK, CHUNK), const3), pl.BlockSpec((CHUNK, AW), lambda c: (0, 0))],
        out_shape=[_sds((T, AW), MXU_DTYPE), _sds((T, AW), MXU_DTYPE), _sds((H, CHUNK, CHUNK), F32), _sds((CHUNK, AW), F32)],
        compiler_params=_cparams("arbitrary"),
    )(dymix, p, p, w_tril, w_tril_t, bmat)


def _shift_down(z, s, row):
    return jnp.where(row >= s, pltpu.roll(z, s, 0), 0.0)


def _shift_up(z, s, row, T):
    return jnp.where(row < T - s, pltpu.roll(z, T - s, 0), 0.0)


def _conv_fwd(p, w_conv, AW, name):
    T = p.shape[0]
    BW = w_conv.shape[1]
    nb = BW // LANE
    b0 = 2 * AW // LANE

    def body(b_ref, c_ref, x_ref, w_ref, y_ref):
        row = lax.broadcasted_iota(jnp.int32, (T, LANE), 0)
        z = c_ref[...] * x_ref[...]
        w0, w1, w2 = w_ref[0:1, :], w_ref[1:2, :], w_ref[2:3, :]
        conv = w2 * z + w1 * _shift_down(z, 1, row) + w0 * _shift_down(z, 2, row)
        y_ref[...] = (b_ref[...] * conv).astype(y_ref.dtype)

    return pl.pallas_call(
        body, name=name, grid=(nb,),
        in_specs=[pl.BlockSpec((T, LANE), lambda j: (0, b0 + j)), pl.BlockSpec((T, LANE), lambda j: (0, b0 + nb + j)),
                  pl.BlockSpec((T, LANE), lambda j: (0, b0 + 2 * nb + j)), pl.BlockSpec((CONV_WIDTH, LANE), lambda j: (0, j))],
        out_specs=pl.BlockSpec((T, LANE), lambda j: (0, j)),
        out_shape=_sds((T, BW), MXU_DTYPE),
        compiler_params=_cparams("parallel"),
    )(p, p, p, w_conv)


def _conv_bwd(dymix, p, w_conv, AW, name):
    T = p.shape[0]
    BW = w_conv.shape[1]
    nb = BW // LANE
    b0 = 2 * AW // LANE
    y0 = AW // LANE

    def body(dy_ref, b_ref, c_ref, x_ref, w_ref, db_ref, dc_ref, dxb_ref, dw_ref):
        row = lax.broadcasted_iota(jnp.int32, (T, LANE), 0)
        cv, xv, dy = c_ref[...], x_ref[...], dy_ref[...]
        w0, w1, w2 = w_ref[0:1, :], w_ref[1:2, :], w_ref[2:3, :]
        z = cv * xv
        z1 = _shift_down(z, 1, row)
        z2 = _shift_down(z, 2, row)
        conv = w2 * z + w1 * z1 + w0 * z2
        db_ref[...] = (dy * conv).astype(db_ref.dtype)
        dconv = dy * b_ref[...]
        dz = w2 * dconv + w1 * _shift_up(dconv, 1, row, T) + w0 * _shift_up(dconv, 2, row, T)
        dc_ref[...] = (dz * xv).astype(dc_ref.dtype)
        dxb_ref[...] = (dz * cv).astype(dxb_ref.dtype)
        dw_ref[0:1, :] = jnp.sum(dconv * z2, axis=0, keepdims=True)
        dw_ref[1:2, :] = jnp.sum(dconv * z1, axis=0, keepdims=True)
        dw_ref[2:3, :] = jnp.sum(dconv * z, axis=0, keepdims=True)

    col = lambda j: (0, j)
    return pl.pallas_call(
        body, name=name, grid=(nb,),
        in_specs=[pl.BlockSpec((T, LANE), lambda j: (0, y0 + j)),
                  pl.BlockSpec((T, LANE), lambda j: (0, b0 + j)), pl.BlockSpec((T, LANE), lambda j: (0, b0 + nb + j)),
                  pl.BlockSpec((T, LANE), lambda j: (0, b0 + 2 * nb + j)), pl.BlockSpec((CONV_WIDTH, LANE), col)],
        out_specs=[pl.BlockSpec((T, LANE), col)] * 3 + [pl.BlockSpec((CONV_WIDTH, LANE), col)],
        out_shape=[_sds((T, BW), MXU_DTYPE)] * 3 + [_sds((CONV_WIDTH, BW), F32)],
        compiler_params=_cparams("parallel"),
    )(dymix, p, p, p, w_conv)


def _head_sum(x, col_head, n_heads):
    out = jnp.zeros_like(x)
    for h in range(n_heads):
        sel = col_head == h
        out = jnp.where(sel, jnp.sum(jnp.where(sel, x, 0.0), axis=-1, keepdims=True), out)
    return out


def _head_col(x, sel):
    return jnp.max(jnp.where(sel, x, NEG_INF), axis=-1, keepdims=True)


def _head_norm(x, g, col_head, n_heads):
    r = lax.rsqrt(_head_sum(x * x, col_head, n_heads) * (1.0 / HEAD_DIM) + EPS)
    return x * r * g, r


def _head_norm_bwd(dy, x, g, r, col_head, n_heads):
    gdy = dy * g
    mean_xg = _head_sum(x * gdy, col_head, n_heads) * (1.0 / HEAD_DIM)
    return r * gdy - x * (r * r * r) * mean_xg, dy * x * r


def _attn_qkv(p, d, PW, q0):
    T, DP = p.shape
    offs = [q0, q0 + 3, q0 + 6]
    if d == 1:
        return [p, p, p], offs, DP // PW
    arrs = [p[:, o * PW:(o + 1) * PW].reshape(T // d, d * PW) for o in offs]
    return arrs, [0, 0, 0], 1


def _attn_specs(PW, U, nb):
    cur = lambda off: pl.BlockSpec((ATT_BLK, PW), lambda r, n: (n, r * U + off))
    prev = lambda off: pl.BlockSpec((ATT_BLK, PW), lambda r, n: (jnp.maximum(n - 1, 0), r * U + off))
    nxt = lambda off: pl.BlockSpec((ATT_BLK, PW), lambda r, n: (jnp.minimum(n + 1, nb - 1), r * U + off))
    return cur, prev, nxt


def _attn_fwd(p, qg, kg, g, d, DP, PW, q_start, name):
    T = p.shape[0]
    L = T // d
    nb = L // ATT_BLK
    HP = PW // HEAD_DIM
    (qa, ka, va), (q0, k0, v0), U = _attn_qkv(p, d, PW, q_start // PW + g)
    scale = HEAD_DIM ** -0.5

    def body(q_ref, kc_ref, kp_ref, vc_ref, vp_ref, qg_ref, kg_ref, o_ref, lse_ref):
        n = pl.program_id(1)
        col_q = lax.broadcasted_iota(jnp.int32, (ATT_BLK, PW), 1) // HEAD_DIM
        col_k = lax.broadcasted_iota(jnp.int32, (2 * ATT_BLK, PW), 1) // HEAD_DIM
        qn, _ = _head_norm(q_ref[...], qg_ref[...], col_q, HP)
        kcat = jnp.concatenate([kp_ref[...], kc_ref[...]], axis=0)
        kn, _ = _head_norm(kcat, kg_ref[...], col_k, HP)
        kn = kn.astype(MXU_DTYPE)
        vcat = jnp.concatenate([vp_ref[...], vc_ref[...]], axis=0).astype(MXU_DTYPE)
        qi = lax.broadcasted_iota(jnp.int32, (ATT_BLK, 2 * ATT_BLK), 0)
        kj = lax.broadcasted_iota(jnp.int32, (ATT_BLK, 2 * ATT_BLK), 1)
        mask = (kj >= qi) & (kj <= qi + ATT_BLK) & ((n > 0) | (kj >= ATT_BLK))
        o = jnp.zeros((ATT_BLK, PW), F32)
        lse = jnp.zeros((ATT_BLK, PW), F32)
        for h in range(HP):
            sel = col_q == h
            qh = jnp.where(sel, qn, 0.0).astype(MXU_DTYPE)
            s = lax.dot_general(qh, kn, (((1,), (1,)), ((), ())), preferred_element_type=F32) * scale
            s = jnp.where(mask, s, NEG_INF)
            m = jnp.max(s, axis=-1, keepdims=True)
            e = jnp.exp(s - m)
            den = jnp.sum(e, axis=-1, keepdims=True)
            oh = jnp.dot(e.astype(MXU_DTYPE), vcat, preferred_element_type=F32) / den
            o = jnp.where(sel, oh, o)
            lse = jnp.where(sel, m + jnp.log(den), lse)
        o_ref[...] = o
        lse_ref[...] = lse

    cur, prev, _ = _attn_specs(PW, U, nb)
    gain = pl.BlockSpec((1, PW), lambda r, n: (0, 0))
    out_blk = pl.BlockSpec((ATT_BLK, PW), lambda r, n: (n, r))
    o, lse = pl.pallas_call(
        body, name=name, grid=(d, nb),
        in_specs=[cur(q0), cur(k0), prev(k0), cur(v0), prev(v0), gain, gain],
        out_specs=[out_blk, out_blk],
        out_shape=[_sds((L, d * PW), F32), _sds((L, d * PW), F32)],
        compiler_params=_cparams("parallel", "parallel"),
    )(qa, ka, ka, va, va, qg, kg)
    return o.reshape(T, PW), lse.reshape(T, PW)


def _attn_bwd(p, qg, kg, o, lse, do, dlse, g, d, DP, PW, q_start, name):
    T = p.shape[0]
    L = T // d
    nb = L // ATT_BLK
    HP = PW // HEAD_DIM
    (qa, ka, va), (q0, k0, v0), U = _attn_qkv(p, d, PW, q_start // PW + g)
    scale = HEAD_DIM ** -0.5
    B = ATT_BLK

    def body(qc_ref, qx_ref, kc_ref, kp_ref, vc_ref, vp_ref, qg_ref, kg_ref,
             oc_ref, ox_ref, lc_ref, lx_ref, doc_ref, dox_ref, dlc_ref, dlx_ref,
             dq_ref, dk_ref, dv_ref, dqg_ref, dkg_ref):
        r_id, n = pl.program_id(0), pl.program_id(1)

        @pl.when((r_id == 0) & (n == 0))
        def _():
            dqg_ref[...] = jnp.zeros_like(dqg_ref)
            dkg_ref[...] = jnp.zeros_like(dkg_ref)

        col1 = lax.broadcasted_iota(jnp.int32, (B, PW), 1) // HEAD_DIM
        col2 = lax.broadcasted_iota(jnp.int32, (2 * B, PW), 1) // HEAD_DIM
        qgv, kgv = qg_ref[...], kg_ref[...]

        q_raw = qc_ref[...]
        qn, q_r = _head_norm(q_raw, qgv, col1, HP)
        kcat_raw = jnp.concatenate([kp_ref[...], kc_ref[...]], axis=0)
        kn_cat, _ = _head_norm(kcat_raw, kgv, col2, HP)
        kn_cat = kn_cat.astype(MXU_DTYPE)
        vcat = jnp.concatenate([vp_ref[...], vc_ref[...]], axis=0).astype(MXU_DTYPE)
        do_c, lse_c, dl_c = doc_ref[...], lc_ref[...], dlc_ref[...]
        delta_c = _head_sum(do_c * oc_ref[...], col1, HP)
        qi = lax.broadcasted_iota(jnp.int32, (B, 2 * B), 0)
        kj = lax.broadcasted_iota(jnp.int32, (B, 2 * B), 1)
        mask1 = (kj >= qi) & (kj <= qi + B) & ((n > 0) | (kj >= B))
        dqn = jnp.zeros((B, PW), F32)
        for h in range(HP):
            sel = col1 == h
            qh = jnp.where(sel, qn, 0.0).astype(MXU_DTYPE)
            s = lax.dot_general(qh, kn_cat, (((1,), (1,)), ((), ())), preferred_element_type=F32) * scale
            pr = jnp.where(mask1, jnp.exp(s - _head_col(lse_c, sel)), 0.0)
            do_h = jnp.where(sel, do_c, 0.0).astype(MXU_DTYPE)
            dp_ = lax.dot_general(do_h, vcat, (((1,), (1,)), ((), ())), preferred_element_type=F32)
            ds = pr * (dp_ - _head_col(delta_c, sel) + _head_col(dl_c, sel)) * scale
            dqn = jnp.where(sel, jnp.dot(ds.astype(MXU_DTYPE), kn_cat, preferred_element_type=F32), dqn)
        dq, dqg_part = _head_norm_bwd(dqn, q_raw, qgv, q_r, col1, HP)
        dq_ref[...] = dq.astype(dq_ref.dtype)
        dqg_ref[0:1, :] += jnp.sum(dqg_part, axis=0, keepdims=True)

        k_raw = kc_ref[...]
        kn, k_r = _head_norm(k_raw, kgv, col1, HP)
        kn_c = kn.astype(MXU_DTYPE)
        v_c = vc_ref[...].astype(MXU_DTYPE)
        qcat_raw = jnp.concatenate([q_raw, qx_ref[...]], axis=0)
        qn_cat, _ = _head_norm(qcat_raw, qgv, col2, HP)
        do_cat = jnp.concatenate([do_c, dox_ref[...]], axis=0)
        o_cat = jnp.concatenate([oc_ref[...], ox_ref[...]], axis=0)
        lse_cat = jnp.concatenate([lse_c, lx_ref[...]], axis=0)
        dl_cat = jnp.concatenate([dl_c, dlx_ref[...]], axis=0)
        delta_cat = _head_sum(do_cat * o_cat, col2, HP)
        i2 = lax.broadcasted_iota(jnp.int32, (2 * B, B), 0)
        j2 = lax.broadcasted_iota(jnp.int32, (2 * B, B), 1)
        mask2 = ((i2 < B) & (j2 <= i2)) | ((i2 >= B) & (j2 >= i2 - B) & (n + 1 < nb))
        dkn = jnp.zeros((B, PW), F32)
        dv = jnp.zeros((B, PW), F32)
        for h in range(HP):
            sel1 = col1 == h
            sel2 = col2 == h
            qh = jnp.where(sel2, qn_cat, 0.0).astype(MXU_DTYPE)
            s = lax.dot_general(qh, kn_c, (((1,), (1,)), ((), ())), preferred_element_type=F32) * scale
            pr = jnp.where(mask2, jnp.exp(s - _head_col(lse_cat, sel2)), 0.0)
            do_h = jnp.where(sel2, do_cat, 0.0).astype(MXU_DTYPE)
            dv_h = lax.dot_general(pr.astype(MXU_DTYPE), do_h, (((0,), (0,)), ((), ())), preferred_element_type=F32)
            dv = jnp.where(sel1, dv_h, dv)
            dp_ = lax.dot_general(do_h, v_c, (((1,), (1,)), ((), ())), preferred_element_type=F32)
            ds = pr * (dp_ - _head_col(delta_cat, sel2) + _head_col(dl_cat, sel2)) * scale
            dk_h = lax.dot_general(ds.astype(MXU_DTYPE), qh, (((0,), (0,)), ((), ())), preferred_element_type=F32)
            dkn = jnp.where(sel1, dk_h, dkn)
        dk, dkg_part = _head_norm_bwd(dkn, k_raw, kgv, k_r, col1, HP)
        dk_ref[...] = dk.astype(dk_ref.dtype)
        dv_ref[...] = dv.astype(dv_ref.dtype)
        dkg_ref[0:1, :] += jnp.sum(dkg_part, axis=0, keepdims=True)

    cur, prev, nxt = _attn_specs(PW, U, nb)
    gain = pl.BlockSpec((1, PW), lambda r, n: (0, 0))
    blk_c = pl.BlockSpec((B, PW), lambda r, n: (n, r))
    blk_x = pl.BlockSpec((B, PW), lambda r, n: (jnp.minimum(n + 1, nb - 1), r))
    acc = pl.BlockSpec((8, PW), lambda r, n: (0, 0))
    view = lambda a: a.reshape(L, d * PW)
    o, lse, do, dlse = view(o), view(lse), view(do), view(dlse)
    dq, dk, dv, dqg, dkg = pl.pallas_call(
        body, name=name, grid=(d, nb),
        in_specs=[cur(q0), nxt(q0), cur(k0), prev(k0), cur(v0), prev(v0), gain, gain,
                  blk_c, blk_x, blk_c, blk_x, blk_c, blk_x, blk_c, blk_x],
        out_specs=[blk_c, blk_c, blk_c, acc, acc],
        out_shape=[_sds((L, d * PW), MXU_DTYPE)] * 3 + [_sds((8, PW), F32)] * 2,
        compiler_params=_cparams("arbitrary", "arbitrary"),
    )(qa, qa, ka, ka, va, va, qg, kg, o, o, lse, lse, do, do, dlse, dlse)
    return dq.reshape(T, PW), dk.reshape(T, PW), dv.reshape(T, PW), dqg[0], dkg[0]


def _softmax3(lses):
    mx = jnp.maximum(jnp.maximum(lses[0], lses[1]), lses[2])
    ex = [jnp.exp(l - mx) for l in lses]
    inv = 1.0 / (ex[0] + ex[1] + ex[2])
    return [e * inv for e in ex]


def _mix_fwd(os_, lses, name, tr=512):
    T, PW = os_[0].shape

    def body(o0, o1, o2, l0, l1, l2, y_ref):
        alpha = _softmax3([l0[...], l1[...], l2[...]])
        for g, o_ref in enumerate((o0, o1, o2)):
            y_ref[:, g * PW:(g + 1) * PW] = (o_ref[...] * alpha[g]).astype(y_ref.dtype)

    blk = pl.BlockSpec((tr, PW), lambda i: (i, 0))
    return pl.pallas_call(
        body, name=name, grid=(T // tr,),
        in_specs=[blk] * 6, out_specs=pl.BlockSpec((tr, 3 * PW), lambda i: (i, 0)),
        out_shape=_sds((T, 3 * PW), MXU_DTYPE),
        compiler_params=_cparams("parallel"),
    )(*os_, *lses)


def _mix_bwd(dymix, os_, lses, c_start, name, tr=512):
    T, PW = os_[0].shape
    HP = PW // HEAD_DIM
    c0 = c_start // PW

    def body(d0, d1, d2, o0, o1, o2, l0, l1, l2, do0, do1, do2, dl0, dl1, dl2):
        col_head = lax.broadcasted_iota(jnp.int32, (tr, PW), 1) // HEAD_DIM
        alpha = _softmax3([l0[...], l1[...], l2[...]])
        dys = [d0[...], d1[...], d2[...]]
        dots = [_head_sum(dy * o_ref[...], col_head, HP) for dy, o_ref in zip(dys, (o0, o1, o2))]
        mean_dot = alpha[0] * dots[0] + alpha[1] * dots[1] + alpha[2] * dots[2]
        for g, (do_ref, dl_ref) in enumerate(((do0, dl0), (do1, dl1), (do2, dl2))):
            do_ref[...] = dys[g] * alpha[g]
            dl_ref[...] = alpha[g] * (dots[g] - mean_dot)

    blk = pl.BlockSpec((tr, PW), lambda i: (i, 0))
    dy_specs = [pl.BlockSpec((tr, PW), lambda i, g=g: (i, c0 + g)) for g in range(3)]
    outs = pl.pallas_call(
        body, name=name, grid=(T // tr,),
        in_specs=dy_specs + [blk] * 6, out_specs=[blk] * 6,
        out_shape=[_sds((T, PW), F32)] * 6,
        compiler_params=_cparams("parallel"),
    )(dymix, dymix, dymix, *os_, *lses)
    return outs[:3], outs[3:]


def _adamw_math(w, g, m, v):
    m2 = ADAM_B1 * m + (1.0 - ADAM_B1) * g
    v2 = ADAM_B2 * v + (1.0 - ADAM_B2) * (g * g)
    m_hat = m2 / (1.0 - ADAM_B1 ** ADAM_STEP)
    v_hat = v2 / (1.0 - ADAM_B2 ** ADAM_STEP)
    delta = -ADAM_LR * (m_hat / (jnp.sqrt(v_hat) + ADAM_EPS) + ADAM_WD * w)
    return delta, m2, v2


def _adamw_layer(layer, w, m, v, parts, prev, name, tr=256):
    _, R, C = w.shape
    tr = min(tr, R)

    def body(w_ref, m_ref, v_ref, p_ref, *rest):
        g_ref, d_ref, m2_ref, v2_ref = rest[-4:]
        g = p_ref[0].astype(F32)
        for j in range(1, N_DEV):
            g = g + p_ref[j].astype(F32)
        delta, m2, v2 = _adamw_math(w_ref[...], g, m_ref[...], v_ref[...])
        g_ref[...] = g
        d_ref[...] = delta
        m2_ref[...] = m2
        v2_ref[...] = v2

    lay = pl.BlockSpec((None, tr, C), lambda i: (layer, i, 0))
    in_specs = [lay, lay, lay, pl.BlockSpec((N_DEV, tr, C), lambda i: (0, i, 0))]
    args = [w, m, v, parts]
    aliases = {}
    if prev is not None:
        in_specs += [pl.BlockSpec(memory_space=pl.ANY)] * 4
        args += list(prev)
        aliases = {4 + i: i for i in range(4)}
    return pl.pallas_call(
        body, name=name, grid=(R // tr,),
        in_specs=in_specs, out_specs=[lay] * 4, out_shape=[_sds(w.shape, F32)] * 4,
        input_output_aliases=aliases,
        compiler_params=_cparams("parallel"),
    )(*args)


def _sum_parts(parts, name):
    _, R, C = parts.shape

    def body(p_ref, out_ref):
        g = p_ref[0]
        for j in range(1, N_DEV):
            g = g + p_ref[j]
        out_ref[...] = g

    return pl.pallas_call(
        body, name=name, grid=(1,),
        in_specs=[pl.BlockSpec((N_DEV, R, C), lambda i: (0, 0, 0))], out_specs=pl.BlockSpec((R, C), lambda i: (0, 0)),
        out_shape=_sds((R, C), F32), compiler_params=_cparams("arbitrary"),
    )(parts)


def _adamw_flat(w, g, m, v, name):
    R, C = w.shape

    def body(w_ref, g_ref, m_ref, v_ref, d_ref, m2_ref, v2_ref):
        delta, m2, v2 = _adamw_math(w_ref[...], g_ref[...], m_ref[...], v_ref[...])
        d_ref[...] = delta
        m2_ref[...] = m2
        v2_ref[...] = v2

    blk = pl.BlockSpec((R, C), lambda i: (0, 0))
    return pl.pallas_call(
        body, name=name, grid=(1,), in_specs=[blk] * 4, out_specs=[blk] * 3, out_shape=[_sds((R, C), F32)] * 3,
        compiler_params=_cparams("arbitrary"),
    )(w, g, m, v)


def _pack(arrays, rows_multiple=8):
    flat = []
    for a in arrays:
        a = a.reshape(-1).astype(F32)
        flat.append(jnp.pad(a, (0, (-a.shape[0]) % LANE)))
    flat = jnp.concatenate(flat)
    flat = jnp.pad(flat, (0, (-flat.shape[0]) % (LANE * rows_multiple)))
    return flat.reshape(-1, LANE)


def _unpack(packed, shapes):
    flat = packed.reshape(-1)
    out, off = [], 0
    for s in shapes:
        size = 1
        for dim in s:
            size *= dim
        out.append(flat[off:off + size].reshape(s))
        off += size + (-size) % LANE
    return out


def _layer_fwd(x, wts, dims):
    AW, BW, PW, DP = dims["AW"], dims["BW"], dims["PW"], dims["DP"]
    q_start = 2 * AW + 3 * BW
    h, r1 = _rmsnorm_fwd(x, wts["attn_norm"], "rmsnorm_fwd")
    p = _mm_nn(h, wts["w_in"], "proj_in", tm=1024, tn=512)
    y_a = _sgu_fwd(p, wts["sgu_tril"], wts["sgu_bmat"], "sgu_fwd")
    y_b = _conv_fwd(p, wts["conv_w"], AW, "conv_fwd")
    os_, lses = [], []
    for g, d in enumerate(DILATIONS):
        o, lse = _attn_fwd(p, wts["q_gain"], wts["k_gain"], g, d, DP, PW, q_start, "attn_fwd_%d" % d)
        os_.append(o)
        lses.append(lse)
    y_c = _mix_fwd(os_, lses, "mix_fwd")
    ymix = jnp.concatenate([y_a, y_b, y_c], axis=1)
    x1 = _mm_nn(ymix, wts["w_out"], "proj_out", residual=x, tm=1024, tn=1024)
    h2, r2 = _rmsnorm_fwd(x1, wts["mlp_norm"], "rmsnorm_fwd")
    a, hid = _mm_nn(h2, wts["w_mlp_in"], "mlp_in", relu2=True, tm=1024, tn=1024)
    x2 = _mm_nn(hid, wts["w_mlp_out"], "mlp_out", residual=x1, tm=1024, tn=1024, tk=2048)
    saved = dict(x=x, h=h, r1=r1, p=p, os=os_, lses=lses, ymix=ymix, x1=x1, h2=h2, r2=r2, a=a, hid=hid)
    return x2, saved


def _layer_bwd(dx, dxb, wts, saved, dims):
    AW, BW, PW, DP = dims["AW"], dims["BW"], dims["PW"], dims["DP"]
    q_start = 2 * AW + 3 * BW
    da = _mm_nt(dxb, wts["w_mlp_out"], "mlp_out_dgrad", out_dtype=MXU_DTYPE, relu2_pre=saved["a"], tm=1024, tn=1024)
    g_w2 = _mm_tn(saved["hid"], dxb, "mlp_out_wgrad", tm=1024, tn=2048, tk=1024)
    dh2 = _mm_nt(da, wts["w_mlp_in"], "mlp_in_dgrad", tm=1024, tn=2048, tk=1024)
    g_w1 = _mm_tn(saved["h2"], da, "mlp_in_wgrad", groups=N_DEV, tm=2048, tn=1024, tk=1024)
    dx1, dx1b, g_mlp_norm = _rmsnorm_bwd(dh2, saved["x1"], wts["mlp_norm"], saved["r2"], dx, "rmsnorm_bwd")
    dymix = _mm_nt(dx1b, wts["w_out"], "proj_out_dgrad", tm=1024, tn=1024)
    g_wout = _mm_tn(saved["ymix"], dx1b, "proj_out_wgrad", tm=1024, tn=2048, tk=1024)
    p = saved["p"]
    du, dv, g_sgu_w, g_sgu_bmat = _sgu_bwd(dymix, p, wts["sgu_tril"], wts["sgu_tril_t"], wts["sgu_bmat"], "sgu_bwd")
    d_b, d_c, d_xb, g_conv = _conv_bwd(dymix, p, wts["conv_w"], AW, "conv_bwd")
    dos, dlses = _mix_bwd(dymix, saved["os"], saved["lses"], AW + BW, "mix_bwd")
    dqs, dks, dvs = [], [], []
    g_q = g_k = 0.0
    for g, d in enumerate(DILATIONS):
        dq, dk, dvv, dqg, dkg = _attn_bwd(p, wts["q_gain"], wts["k_gain"], saved["os"][g], saved["lses"][g],
                                          dos[g], dlses[g], g, d, DP, PW, q_start, "attn_bwd_%d" % d)
        dqs.append(dq)
        dks.append(dk)
        dvs.append(dvv)
        g_q = g_q + dqg.reshape(-1, HEAD_DIM).sum(0)
        g_k = g_k + dkg.reshape(-1, HEAD_DIM).sum(0)
    dp = jnp.concatenate([du, dv, d_b, d_c, d_xb] + dqs + dks + dvs, axis=1)
    dh = _mm_nt(dp, wts["w_in"], "proj_in_dgrad", tm=1024, tn=2048, tk=1408)
    g_win = _mm_tn(saved["h"], dp, "proj_in_wgrad", tm=2048, tn=1408, tk=1024)
    dx0, dx0b, g_attn_norm = _rmsnorm_bwd(dh, saved["x"], wts["attn_norm"], saved["r1"], dx1, "rmsnorm_bwd")
    H = AW // HEAD_DIM
    tril = jnp.tril(jnp.ones((CHUNK, CHUNK), F32))
    small = [g_attn_norm.reshape(-1), g_sgu_w * tril, g_sgu_bmat.reshape(CHUNK, H, HEAD_DIM).sum(-1).T,
             g_conv, g_q, g_k, g_mlp_norm.reshape(-1)]
    D = dx.shape[1]
    big = [g_win.reshape(D, N_DEV, DP // N_DEV).transpose(1, 0, 2),
           g_wout.reshape(N_DEV, -1, D), g_w1, g_w2.reshape(N_DEV, -1, D)]
    return dx0, dx0b, big, small


def kernel(x, attn_norm, w_in, sgu_w, sgu_b, conv_w, q_norm, k_norm, w_out, mlp_norm, w_mlp_in, w_mlp_out, loss_target, m_attn_norm, m_w_in, m_sgu_w, m_sgu_b, m_conv_w, m_q_norm, m_k_norm, m_w_out, m_mlp_norm, m_w_mlp_in, m_w_mlp_out, v_attn_norm, v_w_in, v_sgu_w, v_sgu_b, v_conv_w, v_q_norm, v_k_norm, v_w_out, v_mlp_norm, v_w_mlp_in, v_w_mlp_out):
    n_layers = attn_norm.shape[0]
    T, D = x.shape[1], x.shape[2]
    H = sgu_w.shape[1]
    AW = H * HEAD_DIM
    BW = conv_w.shape[2] * N_DEV
    DP = w_in.shape[2] * N_DEV
    DMIX = w_out.shape[1] * N_DEV
    DFF = w_mlp_in.shape[2] * N_DEV
    PW = (DMIX - AW - BW) // 3
    HP = PW // HEAD_DIM
    dims = dict(AW=AW, BW=BW, PW=PW, DP=DP)
    me = 4 * lax.axis_index("x") + 2 * lax.axis_index("y") + lax.axis_index("c")

    conv_shard_shape = conv_w.shape
    gathered = _exchange(
        [w[l].astype(MXU_DTYPE) for l in range(n_layers) for w in (w_in, w_out, w_mlp_in, w_mlp_out)] + [_pack([conv_w])],
        scatter=False, name="gather_weights")
    conv_full = jnp.stack([_unpack(gathered[-1][j], [conv_shard_shape])[0] for j in range(N_DEV)], axis=2)
    conv_full = conv_full.reshape(n_layers, CONV_WIDTH, BW)

    tril = jnp.tril(jnp.ones((CHUNK, CHUNK), F32))
    layers = []
    for l in range(n_layers):
        g_in, g_out, g_1, g_2 = gathered[4 * l:4 * l + 4]
        w_tril = sgu_w[l] * tril
        layers.append(dict(
            attn_norm=attn_norm[l][None], mlp_norm=mlp_norm[l][None],
            w_in=g_in.transpose(1, 0, 2).reshape(D, DP), w_out=g_out.reshape(DMIX, D),
            w_mlp_in=g_1, w_mlp_out=g_2.reshape(DFF, D),
            sgu_tril=w_tril.astype(MXU_DTYPE), sgu_tril_t=w_tril.transpose(0, 2, 1).astype(MXU_DTYPE),
            sgu_bmat=jnp.repeat(sgu_b[l].T, HEAD_DIM, axis=1),
            conv_w=conv_full[l],
            q_gain=jnp.tile(q_norm[l], HP)[None], k_gain=jnp.tile(k_norm[l], HP)[None]))

    xs = x[0]
    saved = []
    for l in range(n_layers):
        xs, sv = _layer_fwd(xs, layers[l], dims)
        saved.append(sv)
    loss_blk, dx, dxb = _loss_and_grad(xs, loss_target[0], "loss")
    loss = lax.psum(loss_blk[0, 0], ("x", "y", "c"))

    parts = [None] * n_layers
    small = [None] * n_layers
    for l in reversed(range(n_layers)):
        dx, dxb, big, small[l] = _layer_bwd(dx, dxb, layers[l], saved[l], dims)
        parts[l] = _exchange(big, scatter=True, name="scatter_grads")
    grad_x = dx[None]

    big_w = (w_in, w_out, w_mlp_in, w_mlp_out)
    big_m = (m_w_in, m_w_out, m_w_mlp_in, m_w_mlp_out)
    big_v = (v_w_in, v_w_out, v_w_mlp_in, v_w_mlp_out)
    big_out = []
    for i, nm in enumerate(("w_in", "w_out", "w_mlp_in", "w_mlp_out")):
        res = None
        for l in reversed(range(n_layers)):
            res = _adamw_layer(l, big_w[i], big_m[i], big_v[i], parts[l][i], res, "adamw_" + nm)
        big_out.append(res)

    small_shapes = [s.shape for s in small[0]]
    gathered_small = _exchange([_pack([s for l in range(n_layers) for s in small[l]])], scatter=False, name="gather_small")[0]
    summed = _unpack(_sum_parts(gathered_small, "sum_small"), small_shapes * n_layers)
    ns = len(small_shapes)
    g_small = [jnp.stack([summed[l * ns + i] for l in range(n_layers)]) for i in range(ns)]
    g_attn_norm, g_sgu_w, g_sgu_b, g_conv_full, g_q, g_k, g_mlp_norm = g_small
    cs = conv_w.shape[2]
    g_conv = lax.dynamic_slice_in_dim(g_conv_full, me * cs, cs, axis=2)
    sm_w = (attn_norm, sgu_w, sgu_b, conv_w, q_norm, k_norm, mlp_norm)
    sm_m = (m_attn_norm, m_sgu_w, m_sgu_b, m_conv_w, m_q_norm, m_k_norm, m_mlp_norm)
    sm_v = (v_attn_norm, v_sgu_w, v_sgu_b, v_conv_w, v_q_norm, v_k_norm, v_mlp_norm)
    sm_g = (g_attn_norm, g_sgu_w, g_sgu_b, g_conv, g_q, g_k, g_mlp_norm)
    sm_res = _adamw_flat(_pack(sm_w), _pack(sm_g), _pack(sm_m), _pack(sm_v), "adamw_small")
    shapes = [w.shape for w in sm_w]
    sm_delta, sm_m2, sm_v2 = (_unpack(r, shapes) for r in sm_res)

    def ordered(small_list, big_kind):
        b = [big_out[i][big_kind] for i in range(4)]
        return [small_list[0], b[0], small_list[1], small_list[2], small_list[3], small_list[4], small_list[5],
                b[1], small_list[6], b[2], b[3]]

    return (loss, grad_x, *ordered(list(sm_g), 0), *ordered(sm_delta, 1), *ordered(sm_m2, 2), *ordered(sm_v2, 3))
```

```python
import jax
import jax.numpy as jnp
from jax import lax
from jax.experimental import pallas as pl
from jax.experimental.pallas import tpu as pltpu

N_DEV = 8
HEAD_DIM = 64
CHUNK = 128
ATT_BLK = 128
DILATIONS = (1, 4, 16)
CONV_WIDTH = 3
EPS = 1e-6
ADAM_LR = 0.001
ADAM_B1 = 0.9
ADAM_B2 = 0.999
ADAM_EPS = 1e-08
ADAM_WD = 0.01
ADAM_STEP = 10
MXU_DTYPE = jnp.bfloat16
F32 = jnp.float32
LANE = 128
VMEM_LIMIT_BYTES = 56 * 1024 * 1024
NEG_INF = float("-inf")


def _cparams(*sem):
    return pltpu.CompilerParams(dimension_semantics=sem, vmem_limit_bytes=VMEM_LIMIT_BYTES)


def _sds(shape, dtype):
    return jax.ShapeDtypeStruct(shape, dtype)


def _fit(n, tile):
    for t in range(min(tile, n) // LANE * LANE, 0, -LANE):
        if n % t == 0:
            return t
    return n


_HBM = pl.BlockSpec(memory_space=pltpu.HBM)
_SEM = pl.BlockSpec(memory_space=pltpu.SEMAPHORE)
_DATAFLOW = pltpu.SideEffectType.DATAFLOW_SIDE_EFFECTING
N_PEER = N_DEV - 1


def _mesh_pos():
    x, y, c = lax.axis_index("x"), lax.axis_index("y"), lax.axis_index("c")
    return x, y, c, 4 * x + 2 * y + c


def _remote_copies(src, land, send_sems, recv_sems, scatter):
    x, y, c, me = _mesh_pos()
    copies = []
    for k in range(1, N_DEV):
        px = (1 - x) if (k & 4) else x
        py = (1 - y) if (k & 2) else y
        pc = (1 - c) if (k & 1) else c
        copies.append(pltpu.make_async_remote_copy(
            src_ref=src.at[4 * px + 2 * py + pc] if scatter else src,
            dst_ref=land.at[k - 1] if scatter else land.at[me],
            send_sem=send_sems.at[k - 1], recv_sem=recv_sems.at[k - 1],
            device_id=(px, py, pc), device_id_type=pl.DeviceIdType.MESH))
    return copies


def _place_own(srcs, name):
    n = len(srcs)

    def body(*refs):
        src, land, sems = refs[:n], refs[n:2 * n], refs[2 * n]
        me = _mesh_pos()[3]
        copies = [pltpu.make_async_copy(src[t], land[t].at[me], sems.at[t]) for t in range(n)]
        for cp in copies:
            cp.start()
        for cp in copies:
            cp.wait()

    return pl.pallas_call(
        body, name=name, out_shape=[_sds((N_DEV,) + s.shape, s.dtype) for s in srcs],
        in_specs=[_HBM] * n, out_specs=[_HBM] * n, scratch_shapes=[pltpu.SemaphoreType.DMA((n,))],
    )(*srcs)


def _exchange_start(srcs, lands, scatter, name):
    n = len(srcs)

    def body(*refs):
        src, land = refs[:n], refs[n:2 * n]
        send, recv = refs[2 * n:3 * n], refs[3 * n:4 * n]
        token = refs[6 * n]
        for t in range(n):
            for cp in _remote_copies(src[t], land[t], send[t], recv[t], scatter):
                cp.start()
        token[...] = jnp.zeros_like(token)

    thru = [pltpu.HBM(a.shape, a.dtype) for a in list(srcs) + list(lands)]
    outs = pl.pallas_call(
        body, name=name,
        out_shape=[pltpu.SemaphoreType.DMA((N_PEER,))] * (2 * n) + thru + [_sds((8, LANE), F32)],
        in_specs=[_HBM] * (2 * n),
        out_specs=[_SEM] * (2 * n) + [_HBM] * (2 * n) + [pl.BlockSpec(memory_space=pltpu.VMEM)],
        input_output_aliases={i: 2 * n + i for i in range(2 * n)},
        compiler_params=pltpu.CompilerParams(has_side_effects=_DATAFLOW),
    )(*[pltpu.with_memory_space_constraint(a, pltpu.HBM) for a in list(srcs) + list(lands)])
    flights = [(outs[t], outs[n + t], outs[2 * n + t], outs[3 * n + t]) for t in range(n)]
    return flights, outs[4 * n]


def _exchange_wait(flight, after, scatter, name):
    send, recv, src, land = flight

    def body(src_ref, land_ref, send_ref, recv_ref, after_ref, src_out, land_out):
        for cp in _remote_copies(src_ref, land_ref, send_ref, recv_ref, scatter):
            cp.wait_send()
            cp.wait_recv()

    return pl.pallas_call(
        body, name=name, out_shape=(pltpu.HBM(src.shape, src.dtype), pltpu.HBM(land.shape, land.dtype)),
        in_specs=(_HBM, _HBM, _SEM, _SEM, pl.BlockSpec(memory_space=pl.ANY)), out_specs=(_HBM, _HBM),
        input_output_aliases={0: 0, 1: 1},
        compiler_params=pltpu.CompilerParams(has_side_effects=_DATAFLOW),
    )(src, land, send, recv, after)


def _order_after(token, *values):
    out = lax.optimization_barrier((token,) + tuple(values))
    return out[1] if len(values) == 1 else out[1:]


def _rmsnorm_fwd(x, g, name, tr=512):
    T, D = x.shape

    def body(x_ref, g_ref, h_ref, r_ref):
        xv = x_ref[...]
        r = lax.rsqrt(jnp.mean(xv * xv, axis=-1, keepdims=True) + EPS)
        h_ref[...] = (xv * r * g_ref[...]).astype(h_ref.dtype)
        r_ref[...] = r

    return pl.pallas_call(
        body, name=name, grid=(T // tr,),
        in_specs=[pl.BlockSpec((tr, D), lambda i: (i, 0)), pl.BlockSpec((1, D), lambda i: (0, 0))],
        out_specs=[pl.BlockSpec((tr, D), lambda i: (i, 0)), pl.BlockSpec((tr, 1), lambda i: (i, 0))],
        out_shape=[_sds((T, D), MXU_DTYPE), _sds((T, 1), F32)],
        compiler_params=_cparams("parallel"),
    )(x, g)


def _rmsnorm_bwd(dh, x, g, r, dres, name, tr=256):
    T, D = x.shape

    def body(dh_ref, x_ref, g_ref, r_ref, dres_ref, dx_ref, dxb_ref, dg_ref):
        @pl.when(pl.program_id(0) == 0)
        def _():
            dg_ref[...] = jnp.zeros_like(dg_ref)

        dh_v, xv, rv = dh_ref[...], x_ref[...], r_ref[...]
        gdy = dh_v * g_ref[...]
        mean_xg = jnp.mean(xv * gdy, axis=-1, keepdims=True)
        dx = dres_ref[...] + rv * gdy - xv * (rv * rv * rv) * mean_xg
        dx_ref[...] = dx
        dxb_ref[...] = dx.astype(dxb_ref.dtype)
        dg_ref[...] += jnp.sum(dh_v * xv * rv, axis=0, keepdims=True)

    row = lambda i: (i, 0)
    return pl.pallas_call(
        body, name=name, grid=(T // tr,),
        in_specs=[pl.BlockSpec((tr, D), row), pl.BlockSpec((tr, D), row), pl.BlockSpec((1, D), lambda i: (0, 0)),
                  pl.BlockSpec((tr, 1), row), pl.BlockSpec((tr, D), row)],
        out_specs=[pl.BlockSpec((tr, D), row), pl.BlockSpec((tr, D), row), pl.BlockSpec((1, D), lambda i: (0, 0))],
        out_shape=[_sds((T, D), F32), _sds((T, D), MXU_DTYPE), _sds((1, D), F32)],
        compiler_params=_cparams("arbitrary"),
    )(dh, x, g, r, dres)


def _mm_nn(a, b, name, out_dtype=F32, residual=None, relu2=False, tm=512, tn=512, tk=None):
    M, K = a.shape
    grouped = b.ndim == 3
    N = b.shape[0] * b.shape[2] if grouped else b.shape[1]
    tm, tn = _fit(M, tm), _fit(b.shape[2] if grouped else N, tn)
    tk = K if tk is None else _fit(K, tk)
    nk = K // tk
    if grouped:
        per = b.shape[2] // tn
        b_spec = pl.BlockSpec((None, tk, tn), lambda i, j, k: (j // per, k, j % per))
    else:
        b_spec = pl.BlockSpec((tk, tn), lambda i, j, k: (k, j))
    n_out = 2 if relu2 else 1

    def body(*refs):
        a_ref, b_ref = refs[0], refs[1]
        r_ref = refs[2] if residual is not None else None
        o = 3 if residual is not None else 2
        outs = refs[o:o + n_out]
        acc_ref = refs[o + n_out] if nk > 1 else None

        def finish(acc):
            if r_ref is not None:
                acc = acc + r_ref[...]
            outs[0][...] = acc.astype(outs[0].dtype)
            if relu2:
                rl = jnp.maximum(acc, 0.0)
                outs[1][...] = (rl * rl).astype(outs[1].dtype)

        prod = jnp.dot(a_ref[...], b_ref[...], preferred_element_type=F32)
        if nk == 1:
            finish(prod)
        else:
            k = pl.program_id(2)

            @pl.when(k == 0)
            def _():
                acc_ref[...] = prod

            @pl.when(k > 0)
            def _():
                acc_ref[...] += prod

            @pl.when(k == nk - 1)
            def _():
                finish(acc_ref[...])

    out_blk = pl.BlockSpec((tm, tn), lambda i, j, k: (i, j))
    in_specs = [pl.BlockSpec((tm, tk), lambda i, j, k: (i, k)), b_spec]
    args = [a, b]
    if residual is not None:
        in_specs.append(out_blk)
        args.append(residual)
    out_shape = [_sds((M, N), out_dtype)]
    if relu2:
        out_shape.append(_sds((M, N), MXU_DTYPE))
    outs = pl.pallas_call(
        body, name=name, grid=(M // tm, N // tn, nk),
        in_specs=in_specs, out_specs=[out_blk] * n_out, out_shape=out_shape,
        scratch_shapes=[pltpu.VMEM((tm, tn), F32)] if nk > 1 else [],
        compiler_params=_cparams("parallel", "parallel", "arbitrary"),
    )(*args)
    return outs if relu2 else outs[0]


def _mm_nt(a, b, name, out_dtype=F32, relu2_pre=None, tm=512, tn=512, tk=None):
    M, K = a.shape
    grouped = b.ndim == 3
    N = b.shape[1] if grouped else b.shape[0]
    tm, tn = _fit(M, tm), _fit(N, tn)
    tk = K if tk is None else _fit(b.shape[2] if grouped else K, tk)
    nk = K // tk
    if grouped:
        per = b.shape[2] // tk
        b_spec = pl.BlockSpec((None, tn, tk), lambda i, j, k: (k // per, j, k % per))
    else:
        b_spec = pl.BlockSpec((tn, tk), lambda i, j, k: (j, k))

    def body(*refs):
        a_ref, b_ref = refs[0], refs[1]
        p_ref = refs[2] if relu2_pre is not None else None
        o = 3 if relu2_pre is not None else 2
        out_ref = refs[o]
        acc_ref = refs[o + 1] if nk > 1 else None

        def finish(acc):
            if p_ref is not None:
                acc = acc * (2.0 * jnp.maximum(p_ref[...], 0.0))
            out_ref[...] = acc.astype(out_ref.dtype)

        prod = lax.dot_general(a_ref[...], b_ref[...], (((1,), (1,)), ((), ())), preferred_element_type=F32)
        if nk == 1:
            finish(prod)
        else:
            k = pl.program_id(2)

            @pl.when(k == 0)
            def _():
                acc_ref[...] = prod

            @pl.when(k > 0)
            def _():
                acc_ref[...] += prod

            @pl.when(k == nk - 1)
            def _():
                finish(acc_ref[...])

    out_blk = pl.BlockSpec((tm, tn), lambda i, j, k: (i, j))
    in_specs = [pl.BlockSpec((tm, tk), lambda i, j, k: (i, k)), b_spec]
    args = [a, b]
    if relu2_pre is not None:
        in_specs.append(out_blk)
        args.append(relu2_pre)
    return pl.pallas_call(
        body, name=name, grid=(M // tm, N // tn, nk),
        in_specs=in_specs, out_specs=out_blk, out_shape=_sds((M, N), out_dtype),
        scratch_shapes=[pltpu.VMEM((tm, tn), F32)] if nk > 1 else [],
        compiler_params=_cparams("parallel", "parallel", "arbitrary"),
    )(*args)


def _mm_tn(a, b, name, groups=None, tm=512, tn=512, tk=1024):
    T, M = a.shape
    N = b.shape[1]
    tm, tn, tk = _fit(M, tm), _fit(N if groups is None else N // groups, tn), _fit(T, tk)
    nk = T // tk

    def body(a_ref, b_ref, out_ref, acc_ref):
        k = pl.program_id(2)
        prod = lax.dot_general(a_ref[...], b_ref[...], (((0,), (0,)), ((), ())), preferred_element_type=F32)

        @pl.when(k == 0)
        def _():
            acc_ref[...] = prod

        @pl.when(k > 0)
        def _():
            acc_ref[...] += prod

        @pl.when(k == nk - 1)
        def _():
            out_ref[...] = acc_ref[...].astype(out_ref.dtype)

    if groups is None:
        out_spec = pl.BlockSpec((tm, tn), lambda i, j, k: (i, j))
        out_shape = _sds((M, N), MXU_DTYPE)
    else:
        per = N // groups // tn
        out_spec = pl.BlockSpec((None, tm, tn), lambda i, j, k: (j // per, i, j % per))
        out_shape = _sds((groups, M, N // groups), MXU_DTYPE)
    return pl.pallas_call(
        body, name=name, grid=(M // tm, N // tn, nk),
        in_specs=[pl.BlockSpec((tk, tm), lambda i, j, k: (k, i)), pl.BlockSpec((tk, tn), lambda i, j, k: (k, j))],
        out_specs=out_spec, out_shape=out_shape,
        scratch_shapes=[pltpu.VMEM((tm, tn), F32)],
        compiler_params=_cparams("parallel", "parallel", "arbitrary"),
    )(a, b)


def _loss_and_grad(y, target, name, tr=512):
    T, D = y.shape

    def body(y_ref, t_ref, loss_ref, dx_ref, dxb_ref):
        @pl.when(pl.program_id(0) == 0)
        def _():
            loss_ref[...] = jnp.zeros_like(loss_ref)

        err = y_ref[...] - t_ref[...]
        loss_ref[...] += 0.5 * jnp.sum(jnp.mean(err * err, axis=-1, keepdims=True), axis=0, keepdims=True)
        dx = err * (1.0 / D)
        dx_ref[...] = dx
        dxb_ref[...] = dx.astype(dxb_ref.dtype)

    row = lambda i: (i, 0)
    return pl.pallas_call(
        body, name=name, grid=(T // tr,),
        in_specs=[pl.BlockSpec((tr, D), row), pl.BlockSpec((tr, D), row)],
        out_specs=[pl.BlockSpec((8, LANE), lambda i: (0, 0)), pl.BlockSpec((tr, D), row), pl.BlockSpec((tr, D), row)],
        out_shape=[_sds((8, LANE), F32), _sds((T, D), F32), _sds((T, D), MXU_DTYPE)],
        compiler_params=_cparams("arbitrary"),
    )(y, target)


def _sgu_mixed(v, w_ref, b_ref, col_head, n_heads):
    mixed = b_ref[...]
    for h in range(n_heads):
        full = jnp.dot(w_ref[h], v, preferred_element_type=F32)
        mixed = mixed + jnp.where(col_head == h, full, 0.0)
    return mixed


def _sgu_fwd(p, w_tril, bmat, name):
    T = p.shape[0]
    H = w_tril.shape[0]
    AW = H * HEAD_DIM

    def body(u_ref, v_ref, w_ref, b_ref, y_ref):
        col_head = lax.broadcasted_iota(jnp.int32, (CHUNK, AW), 1) // HEAD_DIM
        mixed = _sgu_mixed(v_ref[...].astype(MXU_DTYPE), w_ref, b_ref, col_head, H)
        y_ref[...] = (u_ref[...] * mixed).astype(y_ref.dtype)

    const3 = lambda c: (0, 0, 0)
    return pl.pallas_call(
        body, name=name, grid=(T // CHUNK,),
        in_specs=[pl.BlockSpec((CHUNK, AW), lambda c: (c, 0)), pl.BlockSpec((CHUNK, AW), lambda c: (c, 1)),
                  pl.BlockSpec((H, CHUNK, CHUNK), const3), pl.BlockSpec((CHUNK, AW), lambda c: (0, 0))],
        out_specs=pl.BlockSpec((CHUNK, AW), lambda c: (c, 0)),
        out_shape=_sds((T, AW), MXU_DTYPE),
        compiler_params=_cparams("parallel"),
    )(p, p, w_tril, bmat)


def _sgu_bwd(dymix, p, w_tril, w_tril_t, bmat, name):
    T = p.shape[0]
    H = w_tril.shape[0]
    AW = H * HEAD_DIM

    def body(dy_ref, u_ref, v_ref, w_ref, wt_ref, b_ref, du_ref, dv_ref, dw_ref, db_ref):
        @pl.when(pl.program_id(0) == 0)
        def _():
            dw_ref[...] = jnp.zeros_like(dw_ref)
            db_ref[...] = jnp.zeros_like(db_ref)

        col_head = lax.broadcasted_iota(jnp.int32, (CHUNK, AW), 1) // HEAD_DIM
        v = v_ref[...].astype(MXU_DTYPE)
        dy = dy_ref[...]
        mixed = _sgu_mixed(v, w_ref, b_ref, col_head, H)
        du_ref[...] = (dy * mixed).astype(du_ref.dtype)
        dm = dy * u_ref[...]
        db_ref[...] += dm
        dm_c = dm.astype(MXU_DTYPE)
        dv = jnp.zeros((CHUNK, AW), F32)
        for h in range(H):
            sel = col_head == h
            dv = dv + jnp.where(sel, jnp.dot(wt_ref[h], dm_c, preferred_element_type=F32), 0.0)
            dm_h = jnp.where(sel, dm, 0.0).astype(MXU_DTYPE)
            dw_ref[h] += lax.dot_general(dm_h, v, (((1,), (1,)), ((), ())), preferred_element_type=F32)
        dv_ref[...] = dv.astype(dv_ref.dtype)

    const3 = lambda c: (0, 0, 0)
    blk = pl.BlockSpec((CHUNK, AW), lambda c: (c, 0))
    return pl.pallas_call(
        body, name=name, grid=(T // CHUNK,),
        in_specs=[blk, blk, pl.BlockSpec((CHUNK, AW), lambda c: (c, 1)),
                  pl.BlockSpec((H, CHUNK, CHUNK), const3), pl.BlockSpec((H, CHUNK, CHUNK), const3),
                  pl.BlockSpec((CHUNK, AW), lambda c: (0, 0))],
        out_specs=[blk, blk, pl.BlockSpec((H, CHUNK, CHUNK), const3), pl.BlockSpec((CHUNK, AW), lambda c: (0, 0))],
        out_shape=[_sds((T, AW), MXU_DTYPE), _sds((T, AW), MXU_DTYPE), _sds((H, CHUNK, CHUNK), F32), _sds((CHUNK, AW), F32)],
        compiler_params=_cparams("arbitrary"),
    )(dymix, p, p, w_tril, w_tril_t, bmat)


def _shift_down(z, s, row):
    return jnp.where(row >= s, pltpu.roll(z, s, 0), 0.0)


def _shift_up(z, s, row, T):
    return jnp.where(row < T - s, pltpu.roll(z, T - s, 0), 0.0)


def _conv_fwd(p, w_conv, AW, name):
    T = p.shape[0]
    BW = w_conv.shape[1]
    nb = BW // LANE
    b0 = 2 * AW // LANE

    def body(b_ref, c_ref, x_ref, w_ref, y_ref):
        row = lax.broadcasted_iota(jnp.int32, (T, LANE), 0)
        z = c_ref[...] * x_ref[...]
        w0, w1, w2 = w_ref[0:1, :], w_ref[1:2, :], w_ref[2:3, :]
        conv = w2 * z + w1 * _shift_down(z, 1, row) + w0 * _shift_down(z, 2, row)
        y_ref[...] = (b_ref[...] * conv).astype(y_ref.dtype)

    return pl.pallas_call(
        body, name=name, grid=(nb,),
        in_specs=[pl.BlockSpec((T, LANE), lambda j: (0, b0 + j)), pl.BlockSpec((T, LANE), lambda j: (0, b0 + nb + j)),
                  pl.BlockSpec((T, LANE), lambda j: (0, b0 + 2 * nb + j)), pl.BlockSpec((CONV_WIDTH, LANE), lambda j: (0, j))],
        out_specs=pl.BlockSpec((T, LANE), lambda j: (0, j)),
        out_shape=_sds((T, BW), MXU_DTYPE),
        compiler_params=_cparams("parallel"),
    )(p, p, p, w_conv)


def _conv_bwd(dymix, p, w_conv, AW, name):
    T = p.shape[0]
    BW = w_conv.shape[1]
    nb = BW // LANE
    b0 = 2 * AW // LANE
    y0 = AW // LANE

    def body(dy_ref, b_ref, c_ref, x_ref, w_ref, db_ref, dc_ref, dxb_ref, dw_ref):
        row = lax.broadcasted_iota(jnp.int32, (T, LANE), 0)
        cv, xv, dy = c_ref[...], x_ref[...], dy_ref[...]
        w0, w1, w2 = w_ref[0:1, :], w_ref[1:2, :], w_ref[2:3, :]
        z = cv * xv
        z1 = _shift_down(z, 1, row)
        z2 = _shift_down(z, 2, row)
        conv = w2 * z + w1 * z1 + w0 * z2
        db_ref[...] = (dy * conv).astype(db_ref.dtype)
        dconv = dy * b_ref[...]
        dz = w2 * dconv + w1 * _shift_up(dconv, 1, row, T) + w0 * _shift_up(dconv, 2, row, T)
        dc_ref[...] = (dz * xv).astype(dc_ref.dtype)
        dxb_ref[...] = (dz * cv).astype(dxb_ref.dtype)
        dw_ref[0:1, :] = jnp.sum(dconv * z2, axis=0, keepdims=True)
        dw_ref[1:2, :] = jnp.sum(dconv * z1, axis=0, keepdims=True)
        dw_ref[2:3, :] = jnp.sum(dconv * z, axis=0, keepdims=True)

    col = lambda j: (0, j)
    return pl.pallas_call(
        body, name=name, grid=(nb,),
        in_specs=[pl.BlockSpec((T, LANE), lambda j: (0, y0 + j)),
                  pl.BlockSpec((T, LANE), lambda j: (0, b0 + j)), pl.BlockSpec((T, LANE), lambda j: (0, b0 + nb + j)),
                  pl.BlockSpec((T, LANE), lambda j: (0, b0 + 2 * nb + j)), pl.BlockSpec((CONV_WIDTH, LANE), col)],
        out_specs=[pl.BlockSpec((T, LANE), col)] * 3 + [pl.BlockSpec((CONV_WIDTH, LANE), col)],
        out_shape=[_sds((T, BW), MXU_DTYPE)] * 3 + [_sds((CONV_WIDTH, BW), F32)],
        compiler_params=_cparams("parallel"),
    )(dymix, p, p, p, w_conv)


def _head_sum(x, col_head, n_heads):
    out = jnp.zeros_like(x)
    for h in range(n_heads):
        sel = col_head == h
        out = jnp.where(sel, jnp.sum(jnp.where(sel, x, 0.0), axis=-1, keepdims=True), out)
    return out


def _head_col(x, sel):
    return jnp.max(jnp.where(sel, x, NEG_INF), axis=-1, keepdims=True)


def _head_norm(x, g, col_head, n_heads):
    r = lax.rsqrt(_head_sum(x * x, col_head, n_heads) * (1.0 / HEAD_DIM) + EPS)
    return x * r * g, r


def _head_norm_bwd(dy, x, g, r, col_head, n_heads):
    gdy = dy * g
    mean_xg = _head_sum(x * gdy, col_head, n_heads) * (1.0 / HEAD_DIM)
    return r * gdy - x * (r * r * r) * mean_xg, dy * x * r


def _attn_qkv(p, d, PW, q0):
    T, DP = p.shape
    offs = [q0, q0 + 3, q0 + 6]
    if d == 1:
        return [p, p, p], offs, DP // PW
    arrs = [p[:, o * PW:(o + 1) * PW].reshape(T // d, d * PW) for o in offs]
    return arrs, [0, 0, 0], 1


def _attn_specs(PW, U, nb):
    cur = lambda off: pl.BlockSpec((ATT_BLK, PW), lambda r, n: (n, r * U + off))
    prev = lambda off: pl.BlockSpec((ATT_BLK, PW), lambda r, n: (jnp.maximum(n - 1, 0), r * U + off))
    nxt = lambda off: pl.BlockSpec((ATT_BLK, PW), lambda r, n: (jnp.minimum(n + 1, nb - 1), r * U + off))
    return cur, prev, nxt


def _attn_fwd(p, qg, kg, g, d, DP, PW, q_start, name):
    T = p.shape[0]
    L = T // d
    nb = L // ATT_BLK
    HP = PW // HEAD_DIM
    (qa, ka, va), (q0, k0, v0), U = _attn_qkv(p, d, PW, q_start // PW + g)
    scale = HEAD_DIM ** -0.5

    def body(q_ref, kc_ref, kp_ref, vc_ref, vp_ref, qg_ref, kg_ref, o_ref, lse_ref):
        n = pl.program_id(1)
        col_q = lax.broadcasted_iota(jnp.int32, (ATT_BLK, PW), 1) // HEAD_DIM
        col_k = lax.broadcasted_iota(jnp.int32, (2 * ATT_BLK, PW), 1) // HEAD_DIM
        qn, _ = _head_norm(q_ref[...], qg_ref[...], col_q, HP)
        kcat = jnp.concatenate([kp_ref[...], kc_ref[...]], axis=0)
        kn, _ = _head_norm(kcat, kg_ref[...], col_k, HP)
        kn = kn.astype(MXU_DTYPE)
        vcat = jnp.concatenate([vp_ref[...], vc_ref[...]], axis=0).astype(MXU_DTYPE)
        qi = lax.broadcasted_iota(jnp.int32, (ATT_BLK, 2 * ATT_BLK), 0)
        kj = lax.broadcasted_iota(jnp.int32, (ATT_BLK, 2 * ATT_BLK), 1)
        mask = (kj >= qi) & (kj <= qi + ATT_BLK) & ((n > 0) | (kj >= ATT_BLK))
        o = jnp.zeros((ATT_BLK, PW), F32)
        lse = jnp.zeros((ATT_BLK, PW), F32)
        for h in range(HP):
            sel = col_q == h
            qh = jnp.where(sel, qn, 0.0).astype(MXU_DTYPE)
            s = lax.dot_general(qh, kn, (((1,), (1,)), ((), ())), preferred_element_type=F32) * scale
            s = jnp.where(mask, s, NEG_INF)
            m = jnp.max(s, axis=-1, keepdims=True)
            e = jnp.exp(s - m)
            den = jnp.sum(e, axis=-1, keepdims=True)
            oh = jnp.dot(e.astype(MXU_DTYPE), vcat, preferred_element_type=F32) / den
            o = jnp.where(sel, oh, o)
            lse = jnp.where(sel, m + jnp.log(den), lse)
        o_ref[...] = o
        lse_ref[...] = lse

    cur, prev, _ = _attn_specs(PW, U, nb)
    gain = pl.BlockSpec((1, PW), lambda r, n: (0, 0))
    out_blk = pl.BlockSpec((ATT_BLK, PW), lambda r, n: (n, r))
    o, lse = pl.pallas_call(
        body, name=name, grid=(d, nb),
        in_specs=[cur(q0), cur(k0), prev(k0), cur(v0), prev(v0), gain, gain],
        out_specs=[out_blk, out_blk],
        out_shape=[_sds((L, d * PW), F32), _sds((L, d * PW), F32)],
        compiler_params=_cparams("parallel", "parallel"),
    )(qa, ka, ka, va, va, qg, kg)
    return o.reshape(T, PW), lse.reshape(T, PW)


def _attn_bwd(p, qg, kg, o, lse, do, dlse, g, d, DP, PW, q_start, name):
    T = p.shape[0]
    L = T // d
    nb = L // ATT_BLK
    HP = PW // HEAD_DIM
    (qa, ka, va), (q0, k0, v0), U = _attn_qkv(p, d, PW, q_start // PW + g)
    scale = HEAD_DIM ** -0.5
    B = ATT_BLK

    def body(qc_ref, qx_ref, kc_ref, kp_ref, vc_ref, vp_ref, qg_ref, kg_ref,
             oc_ref, ox_ref, lc_ref, lx_ref, doc_ref, dox_ref, dlc_ref, dlx_ref,
             dq_ref, dk_ref, dv_ref, dqg_ref, dkg_ref):
        r_id, n = pl.program_id(0), pl.program_id(1)

        @pl.when((r_id == 0) & (n == 0))
        def _():
            dqg_ref[...] = jnp.zeros_like(dqg_ref)
            dkg_ref[...] = jnp.zeros_like(dkg_ref)

        col1 = lax.broadcasted_iota(jnp.int32, (B, PW), 1) // HEAD_DIM
        col2 = lax.broadcasted_iota(jnp.int32, (2 * B, PW), 1) // HEAD_DIM
        qgv, kgv = qg_ref[...], kg_ref[...]

        q_raw = qc_ref[...]
        qn, q_r = _head_norm(q_raw, qgv, col1, HP)
        kcat_raw = jnp.concatenate([kp_ref[...], kc_ref[...]], axis=0)
        kn_cat, _ = _head_norm(kcat_raw, kgv, col2, HP)
        kn_cat = kn_cat.astype(MXU_DTYPE)
        vcat = jnp.concatenate([vp_ref[...], vc_ref[...]], axis=0).astype(MXU_DTYPE)
        do_c, lse_c, dl_c = doc_ref[...], lc_ref[...], dlc_ref[...]
        delta_c = _head_sum(do_c * oc_ref[...], col1, HP)
        qi = lax.broadcasted_iota(jnp.int32, (B, 2 * B), 0)
        kj = lax.broadcasted_iota(jnp.int32, (B, 2 * B), 1)
        mask1 = (kj >= qi) & (kj <= qi + B) & ((n > 0) | (kj >= B))
        dqn = jnp.zeros((B, PW), F32)
        for h in range(HP):
            sel = col1 == h
            qh = jnp.where(sel, qn, 0.0).astype(MXU_DTYPE)
            s = lax.dot_general(qh, kn_cat, (((1,), (1,)), ((), ())), preferred_element_type=F32) * scale
            pr = jnp.where(mask1, jnp.exp(s - _head_col(lse_c, sel)), 0.0)
            do_h = jnp.where(sel, do_c, 0.0).astype(MXU_DTYPE)
            dp_ = lax.dot_general(do_h, vcat, (((1,), (1,)), ((), ())), preferred_element_type=F32)
            ds = pr * (dp_ - _head_col(delta_c, sel) + _head_col(dl_c, sel)) * scale
            dqn = jnp.where(sel, jnp.dot(ds.astype(MXU_DTYPE), kn_cat, preferred_element_type=F32), dqn)
        dq, dqg_part = _head_norm_bwd(dqn, q_raw, qgv, q_r, col1, HP)
        dq_ref[...] = dq.astype(dq_ref.dtype)
        dqg_ref[0:1, :] += jnp.sum(dqg_part, axis=0, keepdims=True)

        k_raw = kc_ref[...]
        kn, k_r = _head_norm(k_raw, kgv, col1, HP)
        kn_c = kn.astype(MXU_DTYPE)
        v_c = vc_ref[...].astype(MXU_DTYPE)
        qcat_raw = jnp.concatenate([q_raw, qx_ref[...]], axis=0)
        qn_cat, _ = _head_norm(qcat_raw, qgv, col2, HP)
        do_cat = jnp.concatenate([do_c, dox_ref[...]], axis=0)
        o_cat = jnp.concatenate([oc_ref[...], ox_ref[...]], axis=0)
        lse_cat = jnp.concatenate([lse_c, lx_ref[...]], axis=0)
        dl_cat = jnp.concatenate([dl_c, dlx_ref[...]], axis=0)
        delta_cat = _head_sum(do_cat * o_cat, col2, HP)
        i2 = lax.broadcasted_iota(jnp.int32, (2 * B, B), 0)
        j2 = lax.broadcasted_iota(jnp.int32, (2 * B, B), 1)
        mask2 = ((i2 < B) & (j2 <= i2)) | ((i2 >= B) & (j2 >= i2 - B) & (n + 1 < nb))
        dkn = jnp.zeros((B, PW), F32)
        dv = jnp.zeros((B, PW), F32)
        for h in range(HP):
            sel1 = col1 == h
            sel2 = col2 == h
            qh = jnp.where(sel2, qn_cat, 0.0).astype(MXU_DTYPE)
            s = lax.dot_general(qh, kn_c, (((1,), (1,)), ((), ())), preferred_element_type=F32) * scale
            pr = jnp.where(mask2, jnp.exp(s - _head_col(lse_cat, sel2)), 0.0)
            do_h = jnp.where(sel2, do_cat, 0.0).astype(MXU_DTYPE)
            dv_h = lax.dot_general(pr.astype(MXU_DTYPE), do_h, (((0,), (0,)), ((), ())), preferred_element_type=F32)
            dv = jnp.where(sel1, dv_h, dv)
            dp_ = lax.dot_general(do_h, v_c, (((1,), (1,)), ((), ())), preferred_element_type=F32)
            ds = pr * (dp_ - _head_col(delta_cat, sel2) + _head_col(dl_cat, sel2)) * scale
            dk_h = lax.dot_general(ds.astype(MXU_DTYPE), qh, (((0,), (0,)), ((), ())), preferred_element_type=F32)
            dkn = jnp.where(sel1, dk_h, dkn)
        dk, dkg_part = _head_norm_bwd(dkn, k_raw, kgv, k_r, col1, HP)
        dk_ref[...] = dk.astype(dk_ref.dtype)
        dv_ref[...] = dv.astype(dv_ref.dtype)
        dkg_ref[0:1, :] += jnp.sum(dkg_part, axis=0, keepdims=True)

    cur, prev, nxt = _attn_specs(PW, U, nb)
    gain = pl.BlockSpec((1, PW), lambda r, n: (0, 0))
    blk_c = pl.BlockSpec((B, PW), lambda r, n: (n, r))
    blk_x = pl.BlockSpec((B, PW), lambda r, n: (jnp.minimum(n + 1, nb - 1), r))
    acc = pl.BlockSpec((8, PW), lambda r, n: (0, 0))
    view = lambda a: a.reshape(L, d * PW)
    o, lse, do, dlse = view(o), view(lse), view(do), view(dlse)
    dq, dk, dv, dqg, dkg = pl.pallas_call(
        body, name=name, grid=(d, nb),
        in_specs=[cur(q0), nxt(q0), cur(k0), prev(k0), cur(v0), prev(v0), gain, gain,
                  blk_c, blk_x, blk_c, blk_x, blk_c, blk_x, blk_c, blk_x],
        out_specs=[blk_c, blk_c, blk_c, acc, acc],
        out_shape=[_sds((L, d * PW), MXU_DTYPE)] * 3 + [_sds((8, PW), F32)] * 2,
        compiler_params=_cparams("arbitrary", "arbitrary"),
    )(qa, qa, ka, ka, va, va, qg, kg, o, o, lse, lse, do, do, dlse, dlse)
    return dq.reshape(T, PW), dk.reshape(T, PW), dv.reshape(T, PW), dqg[0], dkg[0]


def _softmax3(lses):
    mx = jnp.maximum(jnp.maximum(lses[0], lses[1]), lses[2])
    ex = [jnp.exp(l - mx) for l in lses]
    inv = 1.0 / (ex[0] + ex[1] + ex[2])
    return [e * inv for e in ex]


def _mix_fwd(os_, lses, name, tr=512):
    T, PW = os_[0].shape

    def body(o0, o1, o2, l0, l1, l2, y_ref):
        alpha = _softmax3([l0[...], l1[...], l2[...]])
        for g, o_ref in enumerate((o0, o1, o2)):
            y_ref[:, g * PW:(g + 1) * PW] = (o_ref[...] * alpha[g]).astype(y_ref.dtype)

    blk = pl.BlockSpec((tr, PW), lambda i: (i, 0))
    return pl.pallas_call(
        body, name=name, grid=(T // tr,),
        in_specs=[blk] * 6, out_specs=pl.BlockSpec((tr, 3 * PW), lambda i: (i, 0)),
        out_shape=_sds((T, 3 * PW), MXU_DTYPE),
        compiler_params=_cparams("parallel"),
    )(*os_, *lses)


def _mix_bwd(dymix, os_, lses, c_start, name, tr=512):
    T, PW = os_[0].shape
    HP = PW // HEAD_DIM
    c0 = c_start // PW

    def body(d0, d1, d2, o0, o1, o2, l0, l1, l2, do0, do1, do2, dl0, dl1, dl2):
        col_head = lax.broadcasted_iota(jnp.int32, (tr, PW), 1) // HEAD_DIM
        alpha = _softmax3([l0[...], l1[...], l2[...]])
        dys = [d0[...], d1[...], d2[...]]
        dots = [_head_sum(dy * o_ref[...], col_head, HP) for dy, o_ref in zip(dys, (o0, o1, o2))]
        mean_dot = alpha[0] * dots[0] + alpha[1] * dots[1] + alpha[2] * dots[2]
        for g, (do_ref, dl_ref) in enumerate(((do0, dl0), (do1, dl1), (do2, dl2))):
            do_ref[...] = dys[g] * alpha[g]
            dl_ref[...] = alpha[g] * (dots[g] - mean_dot)

    blk = pl.BlockSpec((tr, PW), lambda i: (i, 0))
    dy_specs = [pl.BlockSpec((tr, PW), lambda i, g=g: (i, c0 + g)) for g in range(3)]
    outs = pl.pallas_call(
        body, name=name, grid=(T // tr,),
        in_specs=dy_specs + [blk] * 6, out_specs=[blk] * 6,
        out_shape=[_sds((T, PW), F32)] * 6,
        compiler_params=_cparams("parallel"),
    )(dymix, dymix, dymix, *os_, *lses)
    return outs[:3], outs[3:]


def _adamw_math(w, g, m, v):
    m2 = ADAM_B1 * m + (1.0 - ADAM_B1) * g
    v2 = ADAM_B2 * v + (1.0 - ADAM_B2) * (g * g)
    m_hat = m2 / (1.0 - ADAM_B1 ** ADAM_STEP)
    v_hat = v2 / (1.0 - ADAM_B2 ** ADAM_STEP)
    delta = -ADAM_LR * (m_hat / (jnp.sqrt(v_hat) + ADAM_EPS) + ADAM_WD * w)
    return delta, m2, v2


def _adamw_layer(layer, w, m, v, own, landed, me, prev, name, tr=256):
    _, R, C = w.shape
    tr = min(tr, R)

    def body(me_ref, w_ref, m_ref, v_ref, own_ref, land_ref, *rest):
        g_ref, d_ref, m2_ref, v2_ref = rest[-4:]
        g = own_ref[...].astype(F32)
        for j in range(N_PEER):
            g = g + land_ref[j].astype(F32)
        delta, m2, v2 = _adamw_math(w_ref[...], g, m_ref[...], v_ref[...])
        g_ref[...] = g
        d_ref[...] = delta
        m2_ref[...] = m2
        v2_ref[...] = v2

    lay = pl.BlockSpec((None, tr, C), lambda i, me_ref: (layer, i, 0))
    in_specs = [lay, lay, lay, pl.BlockSpec((None, tr, C), lambda i, me_ref: (me_ref[0], i, 0)),
                pl.BlockSpec((N_PEER, tr, C), lambda i, me_ref: (0, i, 0))]
    args = [me, w, m, v, own, landed]
    aliases = {}
    if prev is not None:
        in_specs += [pl.BlockSpec(memory_space=pl.ANY)] * 4
        args += list(prev)
        aliases = {6 + i: i for i in range(4)}
    return pl.pallas_call(
        body, name=name,
        grid_spec=pltpu.PrefetchScalarGridSpec(num_scalar_prefetch=1, grid=(R // tr,), in_specs=in_specs, out_specs=[lay] * 4),
        out_shape=[_sds(w.shape, F32)] * 4,
        input_output_aliases=aliases,
        compiler_params=_cparams("parallel"),
    )(*args)


def _sum_parts(parts, name):
    _, R, C = parts.shape

    def body(p_ref, out_ref):
        g = p_ref[0]
        for j in range(1, N_DEV):
            g = g + p_ref[j]
        out_ref[...] = g

    return pl.pallas_call(
        body, name=name, grid=(1,),
        in_specs=[pl.BlockSpec((N_DEV, R, C), lambda i: (0, 0, 0))], out_specs=pl.BlockSpec((R, C), lambda i: (0, 0)),
        out_shape=_sds((R, C), F32), compiler_params=_cparams("arbitrary"),
    )(parts)


def _adamw_flat(w, g, m, v, name):
    R, C = w.shape

    def body(w_ref, g_ref, m_ref, v_ref, d_ref, m2_ref, v2_ref):
        delta, m2, v2 = _adamw_math(w_ref[...], g_ref[...], m_ref[...], v_ref[...])
        d_ref[...] = delta
        m2_ref[...] = m2
        v2_ref[...] = v2

    blk = pl.BlockSpec((R, C), lambda i: (0, 0))
    return pl.pallas_call(
        body, name=name, grid=(1,), in_specs=[blk] * 4, out_specs=[blk] * 3, out_shape=[_sds((R, C), F32)] * 3,
        compiler_params=_cparams("arbitrary"),
    )(w, g, m, v)


def _pack(arrays, rows_multiple=8):
    flat = []
    for a in arrays:
        a = a.reshape(-1).astype(F32)
        flat.append(jnp.pad(a, (0, (-a.shape[0]) % LANE)))
    flat = jnp.concatenate(flat)
    flat = jnp.pad(flat, (0, (-flat.shape[0]) % (LANE * rows_multiple)))
    return flat.reshape(-1, LANE)


def _unpack(packed, shapes):
    flat = packed.reshape(-1)
    out, off = [], 0
    for s in shapes:
        size = 1
        for dim in s:
            size *= dim
        out.append(flat[off:off + size].reshape(s))
        off += size + (-size) % LANE
    return out


def _layer_fwd(x, wts, getw, dims):
    AW, BW, PW, DP = dims["AW"], dims["BW"], dims["PW"], dims["DP"]
    q_start = 2 * AW + 3 * BW
    h, r1 = _rmsnorm_fwd(x, wts["attn_norm"], "rmsnorm_fwd")
    p = _mm_nn(h, getw("w_in", h), "proj_in", tm=1024, tn=512)
    y_a = _sgu_fwd(p, wts["sgu_tril"], wts["sgu_bmat"], "sgu_fwd")
    y_b = _conv_fwd(p, getw("conv_w", y_a), AW, "conv_fwd")
    os_, lses = [], []
    for g, d in enumerate(DILATIONS):
        o, lse = _attn_fwd(p, wts["q_gain"], wts["k_gain"], g, d, DP, PW, q_start, "attn_fwd_%d" % d)
        os_.append(o)
        lses.append(lse)
    y_c = _mix_fwd(os_, lses, "mix_fwd")
    ymix = jnp.concatenate([y_a, y_b, y_c], axis=1)
    x1 = _mm_nn(ymix, getw("w_out", ymix), "proj_out", residual=x, tm=1024, tn=1024)
    h2, r2 = _rmsnorm_fwd(x1, wts["mlp_norm"], "rmsnorm_fwd")
    a, hid = _mm_nn(h2, getw("w_mlp_in", h2), "mlp_in", relu2=True, tm=1024, tn=1024)
    x2 = _mm_nn(hid, getw("w_mlp_out", hid), "mlp_out", residual=x1, tm=1024, tn=1024, tk=2048)
    saved = dict(x=x, h=h, r1=r1, p=p, os=os_, lses=lses, ymix=ymix, x1=x1, h2=h2, r2=r2, a=a, hid=hid)
    return x2, saved


def _layer_bwd(dx, dxb, wts, getw, scatter, saved, dims):
    AW, BW, PW, DP = dims["AW"], dims["BW"], dims["PW"], dims["DP"]
    q_start = 2 * AW + 3 * BW
    D = dx.shape[1]
    da = _mm_nt(dxb, getw("w_mlp_out", None), "mlp_out_dgrad", out_dtype=MXU_DTYPE, relu2_pre=saved["a"], tm=1024, tn=1024)
    g_w2 = _mm_tn(saved["hid"], dxb, "mlp_out_wgrad", tm=1024, tn=2048, tk=1024)
    da = scatter("w_mlp_out", g_w2.reshape(N_DEV, -1, D), da)
    dh2 = _mm_nt(da, getw("w_mlp_in", None), "mlp_in_dgrad", tm=1024, tn=2048, tk=1024)
    g_w1 = _mm_tn(saved["h2"], da, "mlp_in_wgrad", groups=N_DEV, tm=2048, tn=1024, tk=1024)
    dh2 = scatter("w_mlp_in", g_w1, dh2)
    dx1, dx1b, g_mlp_norm = _rmsnorm_bwd(dh2, saved["x1"], wts["mlp_norm"], saved["r2"], dx, "rmsnorm_bwd")
    dymix = _mm_nt(dx1b, getw("w_out", None), "proj_out_dgrad", tm=1024, tn=1024)
    g_wout = _mm_tn(saved["ymix"], dx1b, "proj_out_wgrad", tm=1024, tn=2048, tk=1024)
    dymix = scatter("w_out", g_wout.reshape(N_DEV, -1, D), dymix)
    p = saved["p"]
    du, dv, g_sgu_w, g_sgu_bmat = _sgu_bwd(dymix, p, wts["sgu_tril"], wts["sgu_tril_t"], wts["sgu_bmat"], "sgu_bwd")
    d_b, d_c, d_xb, g_conv = _conv_bwd(dymix, p, getw("conv_w", None), AW, "conv_bwd")
    dos, dlses = _mix_bwd(dymix, saved["os"], saved["lses"], AW + BW, "mix_bwd")
    dqs, dks, dvs = [], [], []
    g_q = g_k = 0.0
    for g, d in enumerate(DILATIONS):
        dq, dk, dvv, dqg, dkg = _attn_bwd(p, wts["q_gain"], wts["k_gain"], saved["os"][g], saved["lses"][g],
                                          dos[g], dlses[g], g, d, DP, PW, q_start, "attn_bwd_%d" % d)
        dqs.append(dq)
        dks.append(dk)
        dvs.append(dvv)
        g_q = g_q + dqg.reshape(-1, HEAD_DIM).sum(0)
        g_k = g_k + dkg.reshape(-1, HEAD_DIM).sum(0)
    dp = jnp.concatenate([du, dv, d_b, d_c, d_xb] + dqs + dks + dvs, axis=1)
    dh = _mm_nt(dp, getw("w_in", None), "proj_in_dgrad", tm=1024, tn=2048, tk=1408)
    g_win = _mm_tn(saved["h"], dp, "proj_in_wgrad", tm=2048, tn=1408, tk=1024)
    dh = scatter("w_in", g_win.reshape(D, N_DEV, DP // N_DEV).transpose(1, 0, 2), dh)
    dx0, dx0b, g_attn_norm = _rmsnorm_bwd(dh, saved["x"], wts["attn_norm"], saved["r1"], dx1, "rmsnorm_bwd")
    H = AW // HEAD_DIM
    tril = jnp.tril(jnp.ones((CHUNK, CHUNK), F32))
    small = [g_attn_norm.reshape(-1), g_sgu_w * tril, g_sgu_bmat.reshape(CHUNK, H, HEAD_DIM).sum(-1).T,
             g_conv, g_q, g_k, g_mlp_norm.reshape(-1)]
    return dx0, dx0b, small


def kernel(x, attn_norm, w_in, sgu_w, sgu_b, conv_w, q_norm, k_norm, w_out, mlp_norm, w_mlp_in, w_mlp_out, loss_target, m_attn_norm, m_w_in, m_sgu_w, m_sgu_b, m_conv_w, m_q_norm, m_k_norm, m_w_out, m_mlp_norm, m_w_mlp_in, m_w_mlp_out, v_attn_norm, v_w_in, v_sgu_w, v_sgu_b, v_conv_w, v_q_norm, v_k_norm, v_w_out, v_mlp_norm, v_w_mlp_in, v_w_mlp_out):
    n_layers = attn_norm.shape[0]
    T, D = x.shape[1], x.shape[2]
    H = sgu_w.shape[1]
    AW = H * HEAD_DIM
    BW = conv_w.shape[2] * N_DEV
    DP = w_in.shape[2] * N_DEV
    DMIX = w_out.shape[1] * N_DEV
    DFF = w_mlp_in.shape[2] * N_DEV
    PW = (DMIX - AW - BW) // 3
    HP = PW // HEAD_DIM
    dims = dict(AW=AW, BW=BW, PW=PW, DP=DP)
    me = 4 * lax.axis_index("x") + 2 * lax.axis_index("y") + lax.axis_index("c")

    big_names = ("w_in", "w_out", "w_mlp_in", "w_mlp_out")
    big_w = dict(zip(big_names, (w_in, w_out, w_mlp_in, w_mlp_out)))
    big_m = dict(zip(big_names, (m_w_in, m_w_out, m_w_mlp_in, m_w_mlp_out)))
    big_v = dict(zip(big_names, (v_w_in, v_w_out, v_w_mlp_in, v_w_mlp_out)))

    keys = []
    for l in range(n_layers):
        keys += [(l, nm) for nm in big_names]
    keys.insert(1, (0, "conv_w"))
    srcs = [_pack([conv_w]) if nm == "conv_w" else big_w[nm][l].astype(MXU_DTYPE) for l, nm in keys]
    flights, token = _exchange_start(srcs, _place_own(srcs, "place_weights"), scatter=False, name="gather_start")
    in_flight = dict(zip(keys, flights))
    relayout = dict(
        w_in=lambda g: g.transpose(1, 0, 2).reshape(D, DP), w_out=lambda g: g.reshape(DMIX, D),
        w_mlp_in=lambda g: g, w_mlp_out=lambda g: g.reshape(DFF, D),
        conv_w=lambda g: jnp.stack([_unpack(g[j], [conv_w.shape])[0] for j in range(N_DEV)], axis=2).reshape(
            n_layers, CONV_WIDTH, BW))
    gathered = {}

    def weight_getter(l):
        def getw(nm, after):
            key = (0, nm) if nm == "conv_w" else (l, nm)
            if key not in gathered:
                _, land = _exchange_wait(in_flight[key], after, scatter=False, name="gather_wait_%d_%s" % key)
                gathered[key] = relayout[nm](land)
            return gathered[key][l] if nm == "conv_w" else gathered[key]
        return getw

    tril = jnp.tril(jnp.ones((CHUNK, CHUNK), F32))
    layers = []
    for l in range(n_layers):
        w_tril = sgu_w[l] * tril
        layers.append(dict(
            attn_norm=attn_norm[l][None], mlp_norm=mlp_norm[l][None],
            sgu_tril=w_tril.astype(MXU_DTYPE), sgu_tril_t=w_tril.transpose(0, 2, 1).astype(MXU_DTYPE),
            sgu_bmat=jnp.repeat(sgu_b[l].T, HEAD_DIM, axis=1),
            q_gain=jnp.tile(q_norm[l], HP)[None], k_gain=jnp.tile(k_norm[l], HP)[None]))

    xs = _order_after(token, x[0])
    saved = []
    for l in range(n_layers):
        xs, sv = _layer_fwd(xs, layers[l], weight_getter(l), dims)
        saved.append(sv)
    loss_blk, dx, dxb = _loss_and_grad(xs, loss_target[0], "loss")
    loss = lax.psum(loss_blk[0, 0], ("x", "y", "c"))

    scattering = {}

    def scatter_starter(l):
        def scatter(nm, partials, value):
            land = lax.empty((N_PEER,) + partials.shape[1:], partials.dtype)
            fl, tok = _exchange_start([partials], [land], scatter=True, name="scatter_start_%d_%s" % (l, nm))
            scattering[(l, nm)] = fl[0]
            return _order_after(tok, value)
        return scatter

    small = [None] * n_layers
    for l in reversed(range(n_layers)):
        dx, dxb, small[l] = _layer_bwd(dx, dxb, layers[l], weight_getter(l), scatter_starter(l), saved[l], dims)

    small_shapes = [s.shape for s in small[0]]
    small_src = [_pack([s for l in range(n_layers) for s in small[l]])]
    small_flights, token = _exchange_start(small_src, _place_own(small_src, "place_small"), scatter=False, name="small_start")
    dx = _order_after(token, dx)
    grad_x = dx[None]

    me1 = me.astype(jnp.int32).reshape(1)
    res = {nm: None for nm in big_names}
    after = dx
    for l in reversed(range(n_layers)):
        for nm in reversed(big_names):
            own, landed = _exchange_wait(scattering[(l, nm)], after, scatter=True, name="scatter_wait_%d_%s" % (l, nm))
            res[nm] = _adamw_layer(l, big_w[nm], big_m[nm], big_v[nm], own, landed, me1, res[nm], "adamw_" + nm)
            after = res[nm][0]
    big_out = [res[nm] for nm in big_names]

    _, gathered_small = _exchange_wait(small_flights[0], after, scatter=False, name="small_wait")
    summed = _unpack(_sum_parts(gathered_small, "sum_small"), small_shapes * n_layers)
    ns = len(small_shapes)
    g_small = [jnp.stack([summed[l * ns + i] for l in range(n_layers)]) for i in range(ns)]
    g_attn_norm, g_sgu_w, g_sgu_b, g_conv_full, g_q, g_k, g_mlp_norm = g_small
    cs = conv_w.shape[2]
    g_conv = lax.dynamic_slice_in_dim(g_conv_full, me * cs, cs, axis=2)
    sm_w = (attn_norm, sgu_w, sgu_b, conv_w, q_norm, k_norm, mlp_norm)
    sm_m = (m_attn_norm, m_sgu_w, m_sgu_b, m_conv_w, m_q_norm, m_k_norm, m_mlp_norm)
    sm_v = (v_attn_norm, v_sgu_w, v_sgu_b, v_conv_w, v_q_norm, v_k_norm, v_mlp_norm)
    sm_g = (g_attn_norm, g_sgu_w, g_sgu_b, g_conv, g_q, g_k, g_mlp_norm)
    sm_res = _adamw_flat(_pack(sm_w), _pack(sm_g), _pack(sm_m), _pack(sm_v), "adamw_small")
    shapes = [w.shape for w in sm_w]
    sm_delta, sm_m2, sm_v2 = (_unpack(r, shapes) for r in sm_res)

    def ordered(small_list, big_kind):
        b = [big_out[i][big_kind] for i in range(4)]
        return [small_list[0], b[0], small_list[1], small_list[2], small_list[3], small_list[4], small_list[5],
                b[1], small_list[6], b[2], b[3]]

    return (loss, grad_x, *ordered(list(sm_g), 0), *ordered(sm_delta, 1), *ordered(sm_m2, 2), *ordered(sm_v2, 3))
```

```python
import jax
import jax.numpy as jnp
from jax import lax
from jax.experimental import pallas as pl
from jax.experimental.pallas import tpu as pltpu

N_DEV = 8
HEAD_DIM = 64
CHUNK = 128
ATT_BLK = 128
DILATIONS = (1, 4, 16)
CONV_WIDTH = 3
EPS = 1e-6
ADAM_LR = 0.001
ADAM_B1 = 0.9
ADAM_B2 = 0.999
ADAM_EPS = 1e-08
ADAM_WD = 0.01
ADAM_STEP = 10
MXU_DTYPE = jnp.bfloat16
F32 = jnp.float32
LANE = 128
VMEM_LIMIT_BYTES = 56 * 1024 * 1024
NEG_INF = float("-inf")


def _cparams(*sem):
    return pltpu.CompilerParams(dimension_semantics=sem, vmem_limit_bytes=VMEM_LIMIT_BYTES)


def _sds(shape, dtype):
    return jax.ShapeDtypeStruct(shape, dtype)


def _fit(n, tile):
    for t in range(min(tile, n) // LANE * LANE, 0, -LANE):
        if n % t == 0:
            return t
    return n


_HBM = pl.BlockSpec(memory_space=pltpu.HBM)
_SEM = pl.BlockSpec(memory_space=pltpu.SEMAPHORE)
_DATAFLOW = pltpu.SideEffectType.DATAFLOW_SIDE_EFFECTING
N_PEER = N_DEV - 1


def _mesh_pos():
    x, y, c = lax.axis_index("x"), lax.axis_index("y"), lax.axis_index("c")
    return x, y, c, 4 * x + 2 * y + c


def _remote_copies(src, land, send_sems, recv_sems, scatter):
    x, y, c, me = _mesh_pos()
    copies = []
    for k in range(1, N_DEV):
        px = (1 - x) if (k & 4) else x
        py = (1 - y) if (k & 2) else y
        pc = (1 - c) if (k & 1) else c
        copies.append(pltpu.make_async_remote_copy(
            src_ref=src.at[4 * px + 2 * py + pc] if scatter else src,
            dst_ref=land.at[k - 1] if scatter else land.at[me],
            send_sem=send_sems.at[k - 1], recv_sem=recv_sems.at[k - 1],
            device_id=(px, py, pc), device_id_type=pl.DeviceIdType.MESH))
    return copies


def _own_in_place(src, me):
    land = lax.empty((N_DEV,) + src.shape, src.dtype)
    return lax.dynamic_update_slice(land, src[None], (me,) + (0,) * src.ndim)


def _exchange_start(srcs, lands, scatter, name):
    n = len(srcs)

    def body(*refs):
        src, land = refs[:n], refs[n:2 * n]
        send, recv = refs[2 * n:3 * n], refs[3 * n:4 * n]
        token = refs[6 * n]
        for t in range(n):
            for cp in _remote_copies(src[t], land[t], send[t], recv[t], scatter):
                cp.start()
        token[...] = jnp.zeros_like(token)

    thru = [pltpu.HBM(a.shape, a.dtype) for a in list(srcs) + list(lands)]
    outs = pl.pallas_call(
        body, name=name,
        out_shape=[pltpu.SemaphoreType.DMA((N_PEER,))] * (2 * n) + thru + [_sds((8, LANE), F32)],
        in_specs=[_HBM] * (2 * n),
        out_specs=[_SEM] * (2 * n) + [_HBM] * (2 * n) + [pl.BlockSpec(memory_space=pltpu.VMEM)],
        input_output_aliases={i: 2 * n + i for i in range(2 * n)},
        compiler_params=pltpu.CompilerParams(has_side_effects=_DATAFLOW),
    )(*[pltpu.with_memory_space_constraint(a, pltpu.HBM) for a in list(srcs) + list(lands)])
    flights = [(outs[t], outs[n + t], outs[2 * n + t], outs[3 * n + t]) for t in range(n)]
    return flights, outs[4 * n]


def _exchange_wait(flight, after, scatter, name):
    send, recv, src, land = flight

    def body(src_ref, land_ref, send_ref, recv_ref, after_ref, src_out, land_out):
        for cp in _remote_copies(src_ref, land_ref, send_ref, recv_ref, scatter):
            cp.wait_send()
            cp.wait_recv()

    return pl.pallas_call(
        body, name=name, out_shape=(pltpu.HBM(src.shape, src.dtype), pltpu.HBM(land.shape, land.dtype)),
        in_specs=(_HBM, _HBM, _SEM, _SEM, pl.BlockSpec(memory_space=pl.ANY)), out_specs=(_HBM, _HBM),
        input_output_aliases={0: 0, 1: 1},
        compiler_params=pltpu.CompilerParams(has_side_effects=_DATAFLOW),
    )(src, land, send, recv, after)


def _rmsnorm_fwd(x, g, name, dep=None, tr=512):
    T, D = x.shape

    def body(x_ref, g_ref, *rest):
        h_ref, r_ref = rest[-2:]
        xv = x_ref[...]
        r = lax.rsqrt(jnp.mean(xv * xv, axis=-1, keepdims=True) + EPS)
        h_ref[...] = (xv * r * g_ref[...]).astype(h_ref.dtype)
        r_ref[...] = r

    in_specs = [pl.BlockSpec((tr, D), lambda i: (i, 0)), pl.BlockSpec((1, D), lambda i: (0, 0))]
    args = [x, g]
    if dep is not None:
        in_specs.append(pl.BlockSpec(dep.shape, lambda i: (0, 0)))
        args.append(dep)
    return pl.pallas_call(
        body, name=name, grid=(T // tr,),
        in_specs=in_specs,
        out_specs=[pl.BlockSpec((tr, D), lambda i: (i, 0)), pl.BlockSpec((tr, 1), lambda i: (i, 0))],
        out_shape=[_sds((T, D), MXU_DTYPE), _sds((T, 1), F32)],
        compiler_params=_cparams("parallel"),
    )(*args)


def _rmsnorm_bwd(dh, x, g, r, dres, name, tr=256):
    T, D = x.shape

    def body(dh_ref, x_ref, g_ref, r_ref, dres_ref, dx_ref, dxb_ref, dg_ref):
        @pl.when(pl.program_id(0) == 0)
        def _():
            dg_ref[...] = jnp.zeros_like(dg_ref)

        dh_v, xv, rv = dh_ref[...], x_ref[...], r_ref[...]
        gdy = dh_v * g_ref[...]
        mean_xg = jnp.mean(xv * gdy, axis=-1, keepdims=True)
        dx = dres_ref[...] + rv * gdy - xv * (rv * rv * rv) * mean_xg
        dx_ref[...] = dx
        dxb_ref[...] = dx.astype(dxb_ref.dtype)
        dg_ref[...] += jnp.sum(dh_v * xv * rv, axis=0, keepdims=True)

    row = lambda i: (i, 0)
    return pl.pallas_call(
        body, name=name, grid=(T // tr,),
        in_specs=[pl.BlockSpec((tr, D), row), pl.BlockSpec((tr, D), row), pl.BlockSpec((1, D), lambda i: (0, 0)),
                  pl.BlockSpec((tr, 1), row), pl.BlockSpec((tr, D), row)],
        out_specs=[pl.BlockSpec((tr, D), row), pl.BlockSpec((tr, D), row), pl.BlockSpec((1, D), lambda i: (0, 0))],
        out_shape=[_sds((T, D), F32), _sds((T, D), MXU_DTYPE), _sds((1, D), F32)],
        compiler_params=_cparams("arbitrary"),
    )(dh, x, g, r, dres)


def _mm_nn(a, b, name, out_dtype=F32, residual=None, relu2=False, tm=512, tn=512, tk=None):
    M, K = a.shape
    grouped = b.ndim == 3
    N = b.shape[0] * b.shape[2] if grouped else b.shape[1]
    tm, tn = _fit(M, tm), _fit(b.shape[2] if grouped else N, tn)
    tk = K if tk is None else _fit(K, tk)
    nk = K // tk
    if grouped:
        per = b.shape[2] // tn
        b_spec = pl.BlockSpec((None, tk, tn), lambda i, j, k: (j // per, k, j % per))
    else:
        b_spec = pl.BlockSpec((tk, tn), lambda i, j, k: (k, j))
    n_out = 2 if relu2 else 1

    def body(*refs):
        a_ref, b_ref = refs[0], refs[1]
        r_ref = refs[2] if residual is not None else None
        o = 3 if residual is not None else 2
        outs = refs[o:o + n_out]
        acc_ref = refs[o + n_out] if nk > 1 else None

        def finish(acc):
            if r_ref is not None:
                acc = acc + r_ref[...]
            outs[0][...] = acc.astype(outs[0].dtype)
            if relu2:
                rl = jnp.maximum(acc, 0.0)
                outs[1][...] = (rl * rl).astype(outs[1].dtype)

        prod = jnp.dot(a_ref[...], b_ref[...], preferred_element_type=F32)
        if nk == 1:
            finish(prod)
        else:
            k = pl.program_id(2)

            @pl.when(k == 0)
            def _():
                acc_ref[...] = prod

            @pl.when(k > 0)
            def _():
                acc_ref[...] += prod

            @pl.when(k == nk - 1)
            def _():
                finish(acc_ref[...])

    out_blk = pl.BlockSpec((tm, tn), lambda i, j, k: (i, j))
    in_specs = [pl.BlockSpec((tm, tk), lambda i, j, k: (i, k)), b_spec]
    args = [a, b]
    if residual is not None:
        in_specs.append(out_blk)
        args.append(residual)
    out_shape = [_sds((M, N), out_dtype)]
    if relu2:
        out_shape.append(_sds((M, N), MXU_DTYPE))
    outs = pl.pallas_call(
        body, name=name, grid=(M // tm, N // tn, nk),
        in_specs=in_specs, out_specs=[out_blk] * n_out, out_shape=out_shape,
        scratch_shapes=[pltpu.VMEM((tm, tn), F32)] if nk > 1 else [],
        compiler_params=_cparams("parallel", "parallel", "arbitrary"),
    )(*args)
    return outs if relu2 else outs[0]


def _mm_nt(a, b, name, out_dtype=F32, relu2_pre=None, dep=None, tm=512, tn=512, tk=None):
    M, K = a.shape
    grouped = b.ndim == 3
    N = b.shape[1] if grouped else b.shape[0]
    tm, tn = _fit(M, tm), _fit(N, tn)
    tk = K if tk is None else _fit(b.shape[2] if grouped else K, tk)
    nk = K // tk
    if grouped:
        per = b.shape[2] // tk
        b_spec = pl.BlockSpec((None, tn, tk), lambda i, j, k: (k // per, j, k % per))
    else:
        b_spec = pl.BlockSpec((tn, tk), lambda i, j, k: (j, k))

    def body(*refs):
        a_ref, b_ref = refs[0], refs[1]
        p_ref = refs[2] if relu2_pre is not None else None
        o = 2 + (relu2_pre is not None) + (dep is not None)
        out_ref = refs[o]
        acc_ref = refs[o + 1] if nk > 1 else None

        def finish(acc):
            if p_ref is not None:
                acc = acc * (2.0 * jnp.maximum(p_ref[...], 0.0))
            out_ref[...] = acc.astype(out_ref.dtype)

        prod = lax.dot_general(a_ref[...], b_ref[...], (((1,), (1,)), ((), ())), preferred_element_type=F32)
        if nk == 1:
            finish(prod)
        else:
            k = pl.program_id(2)

            @pl.when(k == 0)
            def _():
                acc_ref[...] = prod

            @pl.when(k > 0)
            def _():
                acc_ref[...] += prod

            @pl.when(k == nk - 1)
            def _():
                finish(acc_ref[...])

    out_blk = pl.BlockSpec((tm, tn), lambda i, j, k: (i, j))
    in_specs = [pl.BlockSpec((tm, tk), lambda i, j, k: (i, k)), b_spec]
    args = [a, b]
    if relu2_pre is not None:
        in_specs.append(out_blk)
        args.append(relu2_pre)
    if dep is not None:
        in_specs.append(pl.BlockSpec(dep.shape, lambda i, j, k: (0, 0)))
        args.append(dep)
    return pl.pallas_call(
        body, name=name, grid=(M // tm, N // tn, nk),
        in_specs=in_specs, out_specs=out_blk, out_shape=_sds((M, N), out_dtype),
        scratch_shapes=[pltpu.VMEM((tm, tn), F32)] if nk > 1 else [],
        compiler_params=_cparams("parallel", "parallel", "arbitrary"),
    )(*args)


def _mm_tn(a, b, name, groups=None, tm=512, tn=512, tk=1024):
    T, M = a.shape
    N = b.shape[1]
    tm, tn, tk = _fit(M, tm), _fit(N if groups is None else N // groups, tn), _fit(T, tk)
    nk = T // tk

    def body(a_ref, b_ref, out_ref, acc_ref):
        k = pl.program_id(2)
        prod = lax.dot_general(a_ref[...], b_ref[...], (((0,), (0,)), ((), ())), preferred_element_type=F32)

        @pl.when(k == 0)
        def _():
            acc_ref[...] = prod

        @pl.when(k > 0)
        def _():
            acc_ref[...] += prod

        @pl.when(k == nk - 1)
        def _():
            out_ref[...] = acc_ref[...].astype(out_ref.dtype)

    if groups is None:
        out_spec = pl.BlockSpec((tm, tn), lambda i, j, k: (i, j))
        out_shape = _sds((M, N), MXU_DTYPE)
    else:
        per = N // groups // tn
        out_spec = pl.BlockSpec((None, tm, tn), lambda i, j, k: (j // per, i, j % per))
        out_shape = _sds((groups, M, N // groups), MXU_DTYPE)
    return pl.pallas_call(
        body, name=name, grid=(M // tm, N // tn, nk),
        in_specs=[pl.BlockSpec((tk, tm), lambda i, j, k: (k, i)), pl.BlockSpec((tk, tn), lambda i, j, k: (k, j))],
        out_specs=out_spec, out_shape=out_shape,
        scratch_shapes=[pltpu.VMEM((tm, tn), F32)],
        compiler_params=_cparams("parallel", "parallel", "arbitrary"),
    )(a, b)


def _loss_and_grad(y, target, name, tr=512):
    T, D = y.shape

    def body(y_ref, t_ref, loss_ref, dx_ref, dxb_ref):
        @pl.when(pl.program_id(0) == 0)
        def _():
            loss_ref[...] = jnp.zeros_like(loss_ref)

        err = y_ref[...] - t_ref[...]
        loss_ref[...] += 0.5 * jnp.sum(jnp.mean(err * err, axis=-1, keepdims=True), axis=0, keepdims=True)
        dx = err * (1.0 / D)
        dx_ref[...] = dx
        dxb_ref[...] = dx.astype(dxb_ref.dtype)

    row = lambda i: (i, 0)
    return pl.pallas_call(
        body, name=name, grid=(T // tr,),
        in_specs=[pl.BlockSpec((tr, D), row), pl.BlockSpec((tr, D), row)],
        out_specs=[pl.BlockSpec((8, LANE), lambda i: (0, 0)), pl.BlockSpec((tr, D), row), pl.BlockSpec((tr, D), row)],
        out_shape=[_sds((8, LANE), F32), _sds((T, D), F32), _sds((T, D), MXU_DTYPE)],
        compiler_params=_cparams("arbitrary"),
    )(y, target)


def _sgu_mixed(v, w_ref, b_ref, col_head, n_heads):
    mixed = b_ref[...]
    for h in range(n_heads):
        full = jnp.dot(w_ref[h], v, preferred_element_type=F32)
        mixed = mixed + jnp.where(col_head == h, full, 0.0)
    return mixed


def _sgu_fwd(p, w_tril, bmat, name):
    T = p.shape[0]
    H = w_tril.shape[0]
    AW = H * HEAD_DIM

    def body(u_ref, v_ref, w_ref, b_ref, y_ref):
        col_head = lax.broadcasted_iota(jnp.int32, (CHUNK, AW), 1) // HEAD_DIM
        mixed = _sgu_mixed(v_ref[...].astype(MXU_DTYPE), w_ref, b_ref, col_head, H)
        y_ref[...] = (u_ref[...] * mixed).astype(y_ref.dtype)

    const3 = lambda c: (0, 0, 0)
    return pl.pallas_call(
        body, name=name, grid=(T // CHUNK,),
        in_specs=[pl.BlockSpec((CHUNK, AW), lambda c: (c, 0)), pl.BlockSpec((CHUNK, AW), lambda c: (c, 1)),
                  pl.BlockSpec((H, CHUNK, CHUNK), const3), pl.BlockSpec((CHUNK, AW), lambda c: (0, 0))],
        out_specs=pl.BlockSpec((CHUNK, AW), lambda c: (c, 0)),
        out_shape=_sds((T, AW), MXU_DTYPE),
        compiler_params=_cparams("parallel"),
    )(p, p, w_tril, bmat)


def _sgu_bwd(dymix, p, w_tril, w_tril_t, bmat, name):
    T = p.shape[0]
    H = w_tril.shape[0]
    AW = H * HEAD_DIM

    def body(dy_ref, u_ref, v_ref, w_ref, wt_ref, b_ref, du_ref, dv_ref, dw_ref, db_ref):
        @pl.when(pl.program_id(0) == 0)
        def _():
            dw_ref[...] = jnp.zeros_like(dw_ref)
            db_ref[...] = jnp.zeros_like(db_ref)

        col_head = lax.broadcasted_iota(jnp.int32, (CHUNK, AW), 1) // HEAD_DIM
        v = v_ref[...].astype(MXU_DTYPE)
        dy = dy_ref[...]
        mixed = _sgu_mixed(v, w_ref, b_ref, col_head, H)
        du_ref[...] = (dy * mixed).astype(du_ref.dtype)
        dm = dy * u_ref[...]
        db_ref[...] += dm
        dm_c = dm.astype(MXU_DTYPE)
        dv = jnp.zeros((CHUNK, AW), F32)
        for h in range(H):
            sel = col_head == h
            dv = dv + jnp.where(sel, jnp.dot(wt_ref[h], dm_c, preferred_element_type=F32), 0.0)
            dm_h = jnp.where(sel, dm, 0.0).astype(MXU_DTYPE)
            dw_ref[h] += lax.dot_general(dm_h, v, (((1,), (1,)), ((), ())), preferred_element_type=F32)
        dv_ref[...] = dv.astype(dv_ref.dtype)

    const3 = lambda c: (0, 0, 0)
    blk = pl.BlockSpec((CHUNK, AW), lambda c: (c, 0))
    return pl.pallas_call(
        body, name=name, grid=(T // CHUNK,),
        in_specs=[blk, blk, pl.BlockSpec((CHUNK, AW), lambda c: (c, 1)),
                  pl.BlockSpec((H, CHUNK, CHUNK), const3), pl.BlockSpec((H, CHUNK, CHUNK), const3),
                  pl.BlockSpec((CHUNK, AW), lambda c: (0, 0))],
        out_specs=[blk, blk, pl.BlockSpec((H, CHUNK, CHUNK), const3), pl.BlockSpec((CHUNK, AW), lambda c: (0, 0))],
        out_shape=[_sds((T, AW), MXU_DTYPE), _sds((T, AW), MXU_DTYPE), _sds((H, CHUNK, CHUNK), F32), _sds((CHUNK, AW), F32)],
        compiler_params=_cparams("arbitrary"),
    )(dymix, p, p, w_tril, w_tril_t, bmat)


def _shift_down(z, s, row):
    return jnp.where(row >= s, pltpu.roll(z, s, 0), 0.0)


def _shift_up(z, s, row, T):
    return jnp.where(row < T - s, pltpu.roll(z, T - s, 0), 0.0)


def _conv_fwd(p, w_conv, AW, name):
    T = p.shape[0]
    BW = w_conv.shape[1]
    nb = BW // LANE
    b0 = 2 * AW // LANE

    def body(b_ref, c_ref, x_ref, w_ref, y_ref):
        row = lax.broadcasted_iota(jnp.int32, (T, LANE), 0)
        z = c_ref[...] * x_ref[...]
        w0, w1, w2 = w_ref[0:1, :], w_ref[1:2, :], w_ref[2:3, :]
        conv = w2 * z + w1 * _shift_down(z, 1, row) + w0 * _shift_down(z, 2, row)
        y_ref[...] = (b_ref[...] * conv).astype(y_ref.dtype)

    return pl.pallas_call(
        body, name=name, grid=(nb,),
        in_specs=[pl.BlockSpec((T, LANE), lambda j: (0, b0 + j)), pl.BlockSpec((T, LANE), lambda j: (0, b0 + nb + j)),
                  pl.BlockSpec((T, LANE), lambda j: (0, b0 + 2 * nb + j)), pl.BlockSpec((CONV_WIDTH, LANE), lambda j: (0, j))],
        out_specs=pl.BlockSpec((T, LANE), lambda j: (0, j)),
        out_shape=_sds((T, BW), MXU_DTYPE),
        compiler_params=_cparams("parallel"),
    )(p, p, p, w_conv)


def _conv_bwd(dymix, p, w_conv, AW, name):
    T = p.shape[0]
    BW = w_conv.shape[1]
    nb = BW // LANE
    b0 = 2 * AW // LANE
    y0 = AW // LANE

    def body(dy_ref, b_ref, c_ref, x_ref, w_ref, db_ref, dc_ref, dxb_ref, dw_ref):
        row = lax.broadcasted_iota(jnp.int32, (T, LANE), 0)
        cv, xv, dy = c_ref[...], x_ref[...], dy_ref[...]
        w0, w1, w2 = w_ref[0:1, :], w_ref[1:2, :], w_ref[2:3, :]
        z = cv * xv
        z1 = _shift_down(z, 1, row)
        z2 = _shift_down(z, 2, row)
        conv = w2 * z + w1 * z1 + w0 * z2
        db_ref[...] = (dy * conv).astype(db_ref.dtype)
        dconv = dy * b_ref[...]
        dz = w2 * dconv + w1 * _shift_up(dconv, 1, row, T) + w0 * _shift_up(dconv, 2, row, T)
        dc_ref[...] = (dz * xv).astype(dc_ref.dtype)
        dxb_ref[...] = (dz * cv).astype(dxb_ref.dtype)
        dw_ref[0:1, :] = jnp.sum(dconv * z2, axis=0, keepdims=True)
        dw_ref[1:2, :] = jnp.sum(dconv * z1, axis=0, keepdims=True)
        dw_ref[2:3, :] = jnp.sum(dconv * z, axis=0, keepdims=True)

    col = lambda j: (0, j)
    return pl.pallas_call(
        body, name=name, grid=(nb,),
        in_specs=[pl.BlockSpec((T, LANE), lambda j: (0, y0 + j)),
                  pl.BlockSpec((T, LANE), lambda j: (0, b0 + j)), pl.BlockSpec((T, LANE), lambda j: (0, b0 + nb + j)),
                  pl.BlockSpec((T, LANE), lambda j: (0, b0 + 2 * nb + j)), pl.BlockSpec((CONV_WIDTH, LANE), col)],
        out_specs=[pl.BlockSpec((T, LANE), col)] * 3 + [pl.BlockSpec((CONV_WIDTH, LANE), col)],
        out_shape=[_sds((T, BW), MXU_DTYPE)] * 3 + [_sds((CONV_WIDTH, BW), F32)],
        compiler_params=_cparams("parallel"),
    )(dymix, p, p, p, w_conv)


def _head_sum(x, col_head, n_heads):
    out = jnp.zeros_like(x)
    for h in range(n_heads):
        sel = col_head == h
        out = jnp.where(sel, jnp.sum(jnp.where(sel, x, 0.0), axis=-1, keepdims=True), out)
    return out


def _head_col(x, sel):
    return jnp.max(jnp.where(sel, x, NEG_INF), axis=-1, keepdims=True)


def _head_norm(x, g, col_head, n_heads):
    r = lax.rsqrt(_head_sum(x * x, col_head, n_heads) * (1.0 / HEAD_DIM) + EPS)
    return x * r * g, r


def _head_norm_bwd(dy, x, g, r, col_head, n_heads):
    gdy = dy * g
    mean_xg = _head_sum(x * gdy, col_head, n_heads) * (1.0 / HEAD_DIM)
    return r * gdy - x * (r * r * r) * mean_xg, dy * x * r


def _attn_qkv(p, d, PW, q0):
    T, DP = p.shape
    offs = [q0, q0 + 3, q0 + 6]
    if d == 1:
        return [p, p, p], offs, DP // PW
    arrs = [p[:, o * PW:(o + 1) * PW].reshape(T // d, d * PW) for o in offs]
    return arrs, [0, 0, 0], 1


def _attn_specs(PW, U, nb):
    cur = lambda off: pl.BlockSpec((ATT_BLK, PW), lambda r, n: (n, r * U + off))
    prev = lambda off: pl.BlockSpec((ATT_BLK, PW), lambda r, n: (jnp.maximum(n - 1, 0), r * U + off))
    nxt = lambda off: pl.BlockSpec((ATT_BLK, PW), lambda r, n: (jnp.minimum(n + 1, nb - 1), r * U + off))
    return cur, prev, nxt


def _attn_fwd(p, qg, kg, g, d, DP, PW, q_start, name):
    T = p.shape[0]
    L = T // d
    nb = L // ATT_BLK
    HP = PW // HEAD_DIM
    (qa, ka, va), (q0, k0, v0), U = _attn_qkv(p, d, PW, q_start // PW + g)
    scale = HEAD_DIM ** -0.5

    def body(q_ref, kc_ref, kp_ref, vc_ref, vp_ref, qg_ref, kg_ref, o_ref, lse_ref):
        n = pl.program_id(1)
        col_q = lax.broadcasted_iota(jnp.int32, (ATT_BLK, PW), 1) // HEAD_DIM
        col_k = lax.broadcasted_iota(jnp.int32, (2 * ATT_BLK, PW), 1) // HEAD_DIM
        qn, _ = _head_norm(q_ref[...], qg_ref[...], col_q, HP)
        kcat = jnp.concatenate([kp_ref[...], kc_ref[...]], axis=0)
        kn, _ = _head_norm(kcat, kg_ref[...], col_k, HP)
        kn = kn.astype(MXU_DTYPE)
        vcat = jnp.concatenate([vp_ref[...], vc_ref[...]], axis=0).astype(MXU_DTYPE)
        qi = lax.broadcasted_iota(jnp.int32, (ATT_BLK, 2 * ATT_BLK), 0)
        kj = lax.broadcasted_iota(jnp.int32, (ATT_BLK, 2 * ATT_BLK), 1)
        mask = (kj >= qi) & (kj <= qi + ATT_BLK) & ((n > 0) | (kj >= ATT_BLK))
        o = jnp.zeros((ATT_BLK, PW), F32)
        lse = jnp.zeros((ATT_BLK, PW), F32)
        for h in range(HP):
            sel = col_q == h
            qh = jnp.where(sel, qn, 0.0).astype(MXU_DTYPE)
            s = lax.dot_general(qh, kn, (((1,), (1,)), ((), ())), preferred_element_type=F32) * scale
            s = jnp.where(mask, s, NEG_INF)
            m = jnp.max(s, axis=-1, keepdims=True)
            e = jnp.exp(s - m)
            den = jnp.sum(e, axis=-1, keepdims=True)
            oh = jnp.dot(e.astype(MXU_DTYPE), vcat, preferred_element_type=F32) / den
            o = jnp.where(sel, oh, o)
            lse = jnp.where(sel, m + jnp.log(den), lse)
        o_ref[...] = o
        lse_ref[...] = lse

    cur, prev, _ = _attn_specs(PW, U, nb)
    gain = pl.BlockSpec((1, PW), lambda r, n: (0, 0))
    out_blk = pl.BlockSpec((ATT_BLK, PW), lambda r, n: (n, r))
    o, lse = pl.pallas_call(
        body, name=name, grid=(d, nb),
        in_specs=[cur(q0), cur(k0), prev(k0), cur(v0), prev(v0), gain, gain],
        out_specs=[out_blk, out_blk],
        out_shape=[_sds((L, d * PW), F32), _sds((L, d * PW), F32)],
        compiler_params=_cparams("parallel", "parallel"),
    )(qa, ka, ka, va, va, qg, kg)
    return o.reshape(T, PW), lse.reshape(T, PW)


def _attn_bwd(p, qg, kg, o, lse, do, dlse, g, d, DP, PW, q_start, name):
    T = p.shape[0]
    L = T // d
    nb = L // ATT_BLK
    HP = PW // HEAD_DIM
    (qa, ka, va), (q0, k0, v0), U = _attn_qkv(p, d, PW, q_start // PW + g)
    scale = HEAD_DIM ** -0.5
    B = ATT_BLK

    def body(qc_ref, qx_ref, kc_ref, kp_ref, vc_ref, vp_ref, qg_ref, kg_ref,
             oc_ref, ox_ref, lc_ref, lx_ref, doc_ref, dox_ref, dlc_ref, dlx_ref,
             dq_ref, dk_ref, dv_ref, dqg_ref, dkg_ref):
        r_id, n = pl.program_id(0), pl.program_id(1)

        @pl.when((r_id == 0) & (n == 0))
        def _():
            dqg_ref[...] = jnp.zeros_like(dqg_ref)
            dkg_ref[...] = jnp.zeros_like(dkg_ref)

        col1 = lax.broadcasted_iota(jnp.int32, (B, PW), 1) // HEAD_DIM
        col2 = lax.broadcasted_iota(jnp.int32, (2 * B, PW), 1) // HEAD_DIM
        qgv, kgv = qg_ref[...], kg_ref[...]

        q_raw = qc_ref[...]
        qn, q_r = _head_norm(q_raw, qgv, col1, HP)
        kcat_raw = jnp.concatenate([kp_ref[...], kc_ref[...]], axis=0)
        kn_cat, _ = _head_norm(kcat_raw, kgv, col2, HP)
        kn_cat = kn_cat.astype(MXU_DTYPE)
        vcat = jnp.concatenate([vp_ref[...], vc_ref[...]], axis=0).astype(MXU_DTYPE)
        do_c, lse_c, dl_c = doc_ref[...], lc_ref[...], dlc_ref[...]
        delta_c = _head_sum(do_c * oc_ref[...], col1, HP)
        qi = lax.broadcasted_iota(jnp.int32, (B, 2 * B), 0)
        kj = lax.broadcasted_iota(jnp.int32, (B, 2 * B), 1)
        mask1 = (kj >= qi) & (kj <= qi + B) & ((n > 0) | (kj >= B))
        dqn = jnp.zeros((B, PW), F32)
        for h in range(HP):
            sel = col1 == h
            qh = jnp.where(sel, qn, 0.0).astype(MXU_DTYPE)
            s = lax.dot_general(qh, kn_cat, (((1,), (1,)), ((), ())), preferred_element_type=F32) * scale
            pr = jnp.where(mask1, jnp.exp(s - _head_col(lse_c, sel)), 0.0)
            do_h = jnp.where(sel, do_c, 0.0).astype(MXU_DTYPE)
            dp_ = lax.dot_general(do_h, vcat, (((1,), (1,)), ((), ())), preferred_element_type=F32)
            ds = pr * (dp_ - _head_col(delta_c, sel) + _head_col(dl_c, sel)) * scale
            dqn = jnp.where(sel, jnp.dot(ds.astype(MXU_DTYPE), kn_cat, preferred_element_type=F32), dqn)
        dq, dqg_part = _head_norm_bwd(dqn, q_raw, qgv, q_r, col1, HP)
        dq_ref[...] = dq.astype(dq_ref.dtype)
        dqg_ref[0:1, :] += jnp.sum(dqg_part, axis=0, keepdims=True)

        k_raw = kc_ref[...]
        kn, k_r = _head_norm(k_raw, kgv, col1, HP)
        kn_c = kn.astype(MXU_DTYPE)
        v_c = vc_ref[...].astype(MXU_DTYPE)
        qcat_raw = jnp.concatenate([q_raw, qx_ref[...]], axis=0)
        qn_cat, _ = _head_norm(qcat_raw, qgv, col2, HP)
        do_cat = jnp.concatenate([do_c, dox_ref[...]], axis=0)
        o_cat = jnp.concatenate([oc_ref[...], ox_ref[...]], axis=0)
        lse_cat = jnp.concatenate([lse_c, lx_ref[...]], axis=0)
        dl_cat = jnp.concatenate([dl_c, dlx_ref[...]], axis=0)
        delta_cat = _head_sum(do_cat * o_cat, col2, HP)
        i2 = lax.broadcasted_iota(jnp.int32, (2 * B, B), 0)
        j2 = lax.broadcasted_iota(jnp.int32, (2 * B, B), 1)
        mask2 = ((i2 < B) & (j2 <= i2)) | ((i2 >= B) & (j2 >= i2 - B) & (n + 1 < nb))
        dkn = jnp.zeros((B, PW), F32)
        dv = jnp.zeros((B, PW), F32)
        for h in range(HP):
            sel1 = col1 == h
            sel2 = col2 == h
            qh = jnp.where(sel2, qn_cat, 0.0).astype(MXU_DTYPE)
            s = lax.dot_general(qh, kn_c, (((1,), (1,)), ((), ())), preferred_element_type=F32) * scale
            pr = jnp.where(mask2, jnp.exp(s - _head_col(lse_cat, sel2)), 0.0)
            do_h = jnp.where(sel2, do_cat, 0.0).astype(MXU_DTYPE)
            dv_h = lax.dot_general(pr.astype(MXU_DTYPE), do_h, (((0,), (0,)), ((), ())), preferred_element_type=F32)
            dv = jnp.where(sel1, dv_h, dv)
            dp_ = lax.dot_general(do_h, v_c, (((1,), (1,)), ((), ())), preferred_element_type=F32)
            ds = pr * (dp_ - _head_col(delta_cat, sel2) + _head_col(dl_cat, sel2)) * scale
            dk_h = lax.dot_general(ds.astype(MXU_DTYPE), qh, (((0,), (0,)), ((), ())), preferred_element_type=F32)
            dkn = jnp.where(sel1, dk_h, dkn)
        dk, dkg_part = _head_norm_bwd(dkn, k_raw, kgv, k_r, col1, HP)
        dk_ref[...] = dk.astype(dk_ref.dtype)
        dv_ref[...] = dv.astype(dv_ref.dtype)
        dkg_ref[0:1, :] += jnp.sum(dkg_part, axis=0, keepdims=True)

    cur, prev, nxt = _attn_specs(PW, U, nb)
    gain = pl.BlockSpec((1, PW), lambda r, n: (0, 0))
    blk_c = pl.BlockSpec((B, PW), lambda r, n: (n, r))
    blk_x = pl.BlockSpec((B, PW), lambda r, n: (jnp.minimum(n + 1, nb - 1), r))
    acc = pl.BlockSpec((8, PW), lambda r, n: (0, 0))
    view = lambda a: a.reshape(L, d * PW)
    o, lse, do, dlse = view(o), view(lse), view(do), view(dlse)
    dq, dk, dv, dqg, dkg = pl.pallas_call(
        body, name=name, grid=(d, nb),
        in_specs=[cur(q0), nxt(q0), cur(k0), prev(k0), cur(v0), prev(v0), gain, gain,
                  blk_c, blk_x, blk_c, blk_x, blk_c, blk_x, blk_c, blk_x],
        out_specs=[blk_c, blk_c, blk_c, acc, acc],
        out_shape=[_sds((L, d * PW), MXU_DTYPE)] * 3 + [_sds((8, PW), F32)] * 2,
        compiler_params=_cparams("arbitrary", "arbitrary"),
    )(qa, qa, ka, ka, va, va, qg, kg, o, o, lse, lse, do, do, dlse, dlse)
    return dq.reshape(T, PW), dk.reshape(T, PW), dv.reshape(T, PW), dqg[0], dkg[0]


def _softmax3(lses):
    mx = jnp.maximum(jnp.maximum(lses[0], lses[1]), lses[2])
    ex = [jnp.exp(l - mx) for l in lses]
    inv = 1.0 / (ex[0] + ex[1] + ex[2])
    return [e * inv for e in ex]


def _mix_fwd(os_, lses, name, tr=512):
    T, PW = os_[0].shape

    def body(o0, o1, o2, l0, l1, l2, y_ref):
        alpha = _softmax3([l0[...], l1[...], l2[...]])
        for g, o_ref in enumerate((o0, o1, o2)):
            y_ref[:, g * PW:(g + 1) * PW] = (o_ref[...] * alpha[g]).astype(y_ref.dtype)

    blk = pl.BlockSpec((tr, PW), lambda i: (i, 0))
    return pl.pallas_call(
        body, name=name, grid=(T // tr,),
        in_specs=[blk] * 6, out_specs=pl.BlockSpec((tr, 3 * PW), lambda i: (i, 0)),
        out_shape=_sds((T, 3 * PW), MXU_DTYPE),
        compiler_params=_cparams("parallel"),
    )(*os_, *lses)


def _mix_bwd(dymix, os_, lses, c_start, name, tr=512):
    T, PW = os_[0].shape
    HP = PW // HEAD_DIM
    c0 = c_start // PW

    def body(d0, d1, d2, o0, o1, o2, l0, l1, l2, do0, do1, do2, dl0, dl1, dl2):
        col_head = lax.broadcasted_iota(jnp.int32, (tr, PW), 1) // HEAD_DIM
        alpha = _softmax3([l0[...], l1[...], l2[...]])
        dys = [d0[...], d1[...], d2[...]]
        dots = [_head_sum(dy * o_ref[...], col_head, HP) for dy, o_ref in zip(dys, (o0, o1, o2))]
        mean_dot = alpha[0] * dots[0] + alpha[1] * dots[1] + alpha[2] * dots[2]
        for g, (do_ref, dl_ref) in enumerate(((do0, dl0), (do1, dl1), (do2, dl2))):
            do_ref[...] = dys[g] * alpha[g]
            dl_ref[...] = alpha[g] * (dots[g] - mean_dot)

    blk = pl.BlockSpec((tr, PW), lambda i: (i, 0))
    dy_specs = [pl.BlockSpec((tr, PW), lambda i, g=g: (i, c0 + g)) for g in range(3)]
    outs = pl.pallas_call(
        body, name=name, grid=(T // tr,),
        in_specs=dy_specs + [blk] * 6, out_specs=[blk] * 6,
        out_shape=[_sds((T, PW), F32)] * 6,
        compiler_params=_cparams("parallel"),
    )(dymix, dymix, dymix, *os_, *lses)
    return outs[:3], outs[3:]


def _adamw_math(w, g, m, v):
    m2 = ADAM_B1 * m + (1.0 - ADAM_B1) * g
    v2 = ADAM_B2 * v + (1.0 - ADAM_B2) * (g * g)
    m_hat = m2 / (1.0 - ADAM_B1 ** ADAM_STEP)
    v_hat = v2 / (1.0 - ADAM_B2 ** ADAM_STEP)
    delta = -ADAM_LR * (m_hat / (jnp.sqrt(v_hat) + ADAM_EPS) + ADAM_WD * w)
    return delta, m2, v2


def _adamw_layer(layer, w, m, v, own, landed, me, prev, name, tr=256):
    _, R, C = w.shape
    tr = min(tr, R)

    def body(me_ref, w_ref, m_ref, v_ref, own_ref, land_ref, *rest):
        g_ref, d_ref, m2_ref, v2_ref = rest[-4:]
        g = own_ref[...].astype(F32)
        for j in range(N_PEER):
            g = g + land_ref[j].astype(F32)
        delta, m2, v2 = _adamw_math(w_ref[...], g, m_ref[...], v_ref[...])
        g_ref[...] = g
        d_ref[...] = delta
        m2_ref[...] = m2
        v2_ref[...] = v2

    lay = pl.BlockSpec((None, tr, C), lambda i, me_ref: (layer, i, 0))
    in_specs = [lay, lay, lay, pl.BlockSpec((None, tr, C), lambda i, me_ref: (me_ref[0], i, 0)),
                pl.BlockSpec((N_PEER, tr, C), lambda i, me_ref: (0, i, 0))]
    args = [me, w, m, v, own, landed]
    aliases = {}
    if prev is not None:
        in_specs += [pl.BlockSpec(memory_space=pl.ANY)] * 4
        args += list(prev)
        aliases = {6 + i: i for i in range(4)}
    return pl.pallas_call(
        body, name=name,
        grid_spec=pltpu.PrefetchScalarGridSpec(num_scalar_prefetch=1, grid=(R // tr,), in_specs=in_specs, out_specs=[lay] * 4),
        out_shape=[_sds(w.shape, F32)] * 4,
        input_output_aliases=aliases,
        compiler_params=_cparams("parallel"),
    )(*args)


def _sum_parts(parts, name):
    _, R, C = parts.shape

    def body(p_ref, out_ref):
        g = p_ref[0]
        for j in range(1, N_DEV):
            g = g + p_ref[j]
        out_ref[...] = g

    return pl.pallas_call(
        body, name=name, grid=(1,),
        in_specs=[pl.BlockSpec((N_DEV, R, C), lambda i: (0, 0, 0))], out_specs=pl.BlockSpec((R, C), lambda i: (0, 0)),
        out_shape=_sds((R, C), F32), compiler_params=_cparams("arbitrary"),
    )(parts)


def _adamw_flat(w, g, m, v, name):
    R, C = w.shape

    def body(w_ref, g_ref, m_ref, v_ref, d_ref, m2_ref, v2_ref):
        delta, m2, v2 = _adamw_math(w_ref[...], g_ref[...], m_ref[...], v_ref[...])
        d_ref[...] = delta
        m2_ref[...] = m2
        v2_ref[...] = v2

    blk = pl.BlockSpec((R, C), lambda i: (0, 0))
    return pl.pallas_call(
        body, name=name, grid=(1,), in_specs=[blk] * 4, out_specs=[blk] * 3, out_shape=[_sds((R, C), F32)] * 3,
        compiler_params=_cparams("arbitrary"),
    )(w, g, m, v)


def _pack(arrays, rows_multiple=8):
    flat = []
    for a in arrays:
        a = a.reshape(-1).astype(F32)
        flat.append(jnp.pad(a, (0, (-a.shape[0]) % LANE)))
    flat = jnp.concatenate(flat)
    flat = jnp.pad(flat, (0, (-flat.shape[0]) % (LANE * rows_multiple)))
    return flat.reshape(-1, LANE)


def _unpack(packed, shapes):
    flat = packed.reshape(-1)
    out, off = [], 0
    for s in shapes:
        size = 1
        for dim in s:
            size *= dim
        out.append(flat[off:off + size].reshape(s))
        off += size + (-size) % LANE
    return out


def _layer_fwd(x, wts, getw, dims, dep=None):
    AW, BW, PW, DP = dims["AW"], dims["BW"], dims["PW"], dims["DP"]
    q_start = 2 * AW + 3 * BW
    h, r1 = _rmsnorm_fwd(x, wts["attn_norm"], "rmsnorm_fwd", dep=dep)
    p = _mm_nn(h, getw("w_in", h), "proj_in", tm=1024, tn=512)
    y_a = _sgu_fwd(p, wts["sgu_tril"], wts["sgu_bmat"], "sgu_fwd")
    y_b = _conv_fwd(p, getw("conv_w", y_a), AW, "conv_fwd")
    os_, lses = [], []
    for g, d in enumerate(DILATIONS):
        o, lse = _attn_fwd(p, wts["q_gain"], wts["k_gain"], g, d, DP, PW, q_start, "attn_fwd_%d" % d)
        os_.append(o)
        lses.append(lse)
    y_c = _mix_fwd(os_, lses, "mix_fwd")
    ymix = jnp.concatenate([y_a, y_b, y_c], axis=1)
    x1 = _mm_nn(ymix, getw("w_out", ymix), "proj_out", residual=x, tm=1024, tn=1024)
    h2, r2 = _rmsnorm_fwd(x1, wts["mlp_norm"], "rmsnorm_fwd")
    a, hid = _mm_nn(h2, getw("w_mlp_in", h2), "mlp_in", relu2=True, tm=1024, tn=1024)
    x2 = _mm_nn(hid, getw("w_mlp_out", hid), "mlp_out", residual=x1, tm=1024, tn=1024, tk=2048)
    saved = dict(x=x, h=h, r1=r1, p=p, os=os_, lses=lses, ymix=ymix, x1=x1, h2=h2, r2=r2, a=a, hid=hid)
    return x2, saved


def _layer_bwd(dx, dxb, wts, getw, scatter, saved, dims):
    AW, BW, PW, DP = dims["AW"], dims["BW"], dims["PW"], dims["DP"]
    q_start = 2 * AW + 3 * BW
    D = dx.shape[1]
    g_w2 = _mm_tn(saved["hid"], dxb, "mlp_out_wgrad", tm=1024, tn=2048, tk=1024)
    token = scatter("w_mlp_out", g_w2.reshape(N_DEV, -1, D))
    da = _mm_nt(dxb, getw("w_mlp_out", None), "mlp_out_dgrad", out_dtype=MXU_DTYPE, relu2_pre=saved["a"], dep=token,
                tm=1024, tn=1024)
    g_w1 = _mm_tn(saved["h2"], da, "mlp_in_wgrad", groups=N_DEV, tm=2048, tn=1024, tk=1024)
    token = scatter("w_mlp_in", g_w1)
    dh2 = _mm_nt(da, getw("w_mlp_in", None), "mlp_in_dgrad", dep=token, tm=1024, tn=2048, tk=1024)
    dx1, dx1b, g_mlp_norm = _rmsnorm_bwd(dh2, saved["x1"], wts["mlp_norm"], saved["r2"], dx, "rmsnorm_bwd")
    g_wout = _mm_tn(saved["ymix"], dx1b, "proj_out_wgrad", tm=1024, tn=2048, tk=1024)
    token = scatter("w_out", g_wout.reshape(N_DEV, -1, D))
    dymix = _mm_nt(dx1b, getw("w_out", None), "proj_out_dgrad", dep=token, tm=1024, tn=1024)
    p = saved["p"]
    du, dv, g_sgu_w, g_sgu_bmat = _sgu_bwd(dymix, p, wts["sgu_tril"], wts["sgu_tril_t"], wts["sgu_bmat"], "sgu_bwd")
    d_b, d_c, d_xb, g_conv = _conv_bwd(dymix, p, getw("conv_w", None), AW, "conv_bwd")
    dos, dlses = _mix_bwd(dymix, saved["os"], saved["lses"], AW + BW, "mix_bwd")
    dqs, dks, dvs = [], [], []
    g_q = g_k = 0.0
    for g, d in enumerate(DILATIONS):
        dq, dk, dvv, dqg, dkg = _attn_bwd(p, wts["q_gain"], wts["k_gain"], saved["os"][g], saved["lses"][g],
                                          dos[g], dlses[g], g, d, DP, PW, q_start, "attn_bwd_%d" % d)
        dqs.append(dq)
        dks.append(dk)
        dvs.append(dvv)
        g_q = g_q + dqg.reshape(-1, HEAD_DIM).sum(0)
        g_k = g_k + dkg.reshape(-1, HEAD_DIM).sum(0)
    dp = jnp.concatenate([du, dv, d_b, d_c, d_xb] + dqs + dks + dvs, axis=1)
    g_win = _mm_tn(saved["h"], dp, "proj_in_wgrad", tm=2048, tn=1408, tk=1024)
    token = scatter("w_in", g_win.reshape(D, N_DEV, DP // N_DEV).transpose(1, 0, 2))
    dh = _mm_nt(dp, getw("w_in", None), "proj_in_dgrad", dep=token, tm=1024, tn=2048, tk=1408)
    dx0, dx0b, g_attn_norm = _rmsnorm_bwd(dh, saved["x"], wts["attn_norm"], saved["r1"], dx1, "rmsnorm_bwd")
    H = AW // HEAD_DIM
    tril = jnp.tril(jnp.ones((CHUNK, CHUNK), F32))
    small = [g_attn_norm.reshape(-1), g_sgu_w * tril, g_sgu_bmat.reshape(CHUNK, H, HEAD_DIM).sum(-1).T,
             g_conv, g_q, g_k, g_mlp_norm.reshape(-1)]
    return dx0, dx0b, small


def kernel(x, attn_norm, w_in, sgu_w, sgu_b, conv_w, q_norm, k_norm, w_out, mlp_norm, w_mlp_in, w_mlp_out, loss_target, m_attn_norm, m_w_in, m_sgu_w, m_sgu_b, m_conv_w, m_q_norm, m_k_norm, m_w_out, m_mlp_norm, m_w_mlp_in, m_w_mlp_out, v_attn_norm, v_w_in, v_sgu_w, v_sgu_b, v_conv_w, v_q_norm, v_k_norm, v_w_out, v_mlp_norm, v_w_mlp_in, v_w_mlp_out):
    n_layers = attn_norm.shape[0]
    T, D = x.shape[1], x.shape[2]
    H = sgu_w.shape[1]
    AW = H * HEAD_DIM
    BW = conv_w.shape[2] * N_DEV
    DP = w_in.shape[2] * N_DEV
    DMIX = w_out.shape[1] * N_DEV
    DFF = w_mlp_in.shape[2] * N_DEV
    PW = (DMIX - AW - BW) // 3
    HP = PW // HEAD_DIM
    dims = dict(AW=AW, BW=BW, PW=PW, DP=DP)
    me = 4 * lax.axis_index("x") + 2 * lax.axis_index("y") + lax.axis_index("c")

    big_names = ("w_in", "w_out", "w_mlp_in", "w_mlp_out")
    big_w = dict(zip(big_names, (w_in, w_out, w_mlp_in, w_mlp_out)))
    big_m = dict(zip(big_names, (m_w_in, m_w_out, m_w_mlp_in, m_w_mlp_out)))
    big_v = dict(zip(big_names, (v_w_in, v_w_out, v_w_mlp_in, v_w_mlp_out)))

    keys = []
    for l in range(n_layers):
        keys += [(l, nm) for nm in big_names]
    keys.insert(1, (0, "conv_w"))
    srcs = [_pack([conv_w]) if nm == "conv_w" else big_w[nm][l].astype(MXU_DTYPE) for l, nm in keys]
    flights, gather_token = _exchange_start(srcs, [_own_in_place(s, me) for s in srcs], scatter=False, name="gather_start")
    in_flight = dict(zip(keys, flights))
    relayout = dict(
        w_in=lambda g: g.transpose(1, 0, 2).reshape(D, DP), w_out=lambda g: g.reshape(DMIX, D),
        w_mlp_in=lambda g: g, w_mlp_out=lambda g: g.reshape(DFF, D),
        conv_w=lambda g: jnp.stack([_unpack(g[j], [conv_w.shape])[0] for j in range(N_DEV)], axis=2).reshape(
            n_layers, CONV_WIDTH, BW))
    gathered = {}

    def weight_getter(l):
        def getw(nm, after):
            key = (0, nm) if nm == "conv_w" else (l, nm)
            if key not in gathered:
                _, land = _exchange_wait(in_flight[key], after, scatter=False, name="gather_wait_%d_%s" % key)
                gathered[key] = relayout[nm](land)
            return gathered[key][l] if nm == "conv_w" else gathered[key]
        return getw

    tril = jnp.tril(jnp.ones((CHUNK, CHUNK), F32))
    layers = []
    for l in range(n_layers):
        w_tril = sgu_w[l] * tril
        layers.append(dict(
            attn_norm=attn_norm[l][None], mlp_norm=mlp_norm[l][None],
            sgu_tril=w_tril.astype(MXU_DTYPE), sgu_tril_t=w_tril.transpose(0, 2, 1).astype(MXU_DTYPE),
            sgu_bmat=jnp.repeat(sgu_b[l].T, HEAD_DIM, axis=1),
            q_gain=jnp.tile(q_norm[l], HP)[None], k_gain=jnp.tile(k_norm[l], HP)[None]))

    xs = x[0]
    saved = []
    for l in range(n_layers):
        xs, sv = _layer_fwd(xs, layers[l], weight_getter(l), dims, dep=gather_token if l == 0 else None)
        saved.append(sv)
    loss_blk, dx, dxb = _loss_and_grad(xs, loss_target[0], "loss")
    loss = lax.psum(loss_blk[0, 0], ("x", "y", "c"))

    scattering = {}

    def scatter_starter(l):
        def scatter(nm, partials):
            land = lax.empty((N_PEER,) + partials.shape[1:], partials.dtype)
            fl, tok = _exchange_start([partials], [land], scatter=True, name="scatter_start_%d_%s" % (l, nm))
            scattering[(l, nm)] = fl[0]
            return tok
        return scatter

    small = [None] * n_layers
    for l in reversed(range(n_layers)):
        dx, dxb, small[l] = _layer_bwd(dx, dxb, layers[l], weight_getter(l), scatter_starter(l), saved[l], dims)

    small_shapes = [s.shape for s in small[0]]
    small_src = [_pack([s for l in range(n_layers) for s in small[l]])]
    small_flights, small_token = _exchange_start(small_src, [_own_in_place(s, me) for s in small_src], scatter=False,
                                                 name="small_start")
    grad_x = dx[None]

    me1 = me.astype(jnp.int32).reshape(1)
    res = {nm: None for nm in big_names}
    after = small_token
    for l in reversed(range(n_layers)):
        for nm in reversed(big_names):
            own, landed = _exchange_wait(scattering[(l, nm)], after, scatter=True, name="scatter_wait_%d_%s" % (l, nm))
            res[nm] = _adamw_layer(l, big_w[nm], big_m[nm], big_v[nm], own, landed, me1, res[nm], "adamw_" + nm)
            after = res[nm][0]
    big_out = [res[nm] for nm in big_names]

    _, gathered_small = _exchange_wait(small_flights[0], after, scatter=False, name="small_wait")
    summed = _unpack(_sum_parts(gathered_small, "sum_small"), small_shapes * n_layers)
    ns = len(small_shapes)
    g_small = [jnp.stack([summed[l * ns + i] for l in range(n_layers)]) for i in range(ns)]
    g_attn_norm, g_sgu_w, g_sgu_b, g_conv_full, g_q, g_k, g_mlp_norm = g_small
    cs = conv_w.shape[2]
    g_conv = lax.dynamic_slice_in_dim(g_conv_full, me * cs, cs, axis=2)
    sm_w = (attn_norm, sgu_w, sgu_b, conv_w, q_norm, k_norm, mlp_norm)
    sm_m = (m_attn_norm, m_sgu_w, m_sgu_b, m_conv_w, m_q_norm, m_k_norm, m_mlp_norm)
    sm_v = (v_attn_norm, v_sgu_w, v_sgu_b, v_conv_w, v_q_norm, v_k_norm, v_mlp_norm)
    sm_g = (g_attn_norm, g_sgu_w, g_sgu_b, g_conv, g_q, g_k, g_mlp_norm)
    sm_res = _adamw_flat(_pack(sm_w), _pack(sm_g), _pack(sm_m), _pack(sm_v), "adamw_small")
    shapes = [w.shape for w in sm_w]
    sm_delta, sm_m2, sm_v2 = (_unpack(r, shapes) for r in sm_res)

    def ordered(small_list, big_kind):
        b = [big_out[i][big_kind] for i in range(4)]
        return [small_list[0], b[0], small_list[1], small_list[2], small_list[3], small_list[4], small_list[5],
                b[1], small_list[6], b[2], b[3]]

    return (loss, grad_x, *ordered(list(sm_g), 0), *ordered(sm_delta, 1), *ordered(sm_m2, 2), *ordered(sm_v2, 3))
```

```python
import jax
import jax.numpy as jnp
from jax import lax
from jax.experimental import pallas as pl
from jax.experimental.pallas import tpu as pltpu

N_DEV = 8
HEAD_DIM = 64
CHUNK = 128
ATT_BLK = 128
DILATIONS = (1, 4, 16)
CONV_WIDTH = 3
EPS = 1e-6
ADAM_LR = 0.001
ADAM_B1 = 0.9
ADAM_B2 = 0.999
ADAM_EPS = 1e-08
ADAM_WD = 0.01
ADAM_STEP = 10
MXU_DTYPE = jnp.bfloat16
F32 = jnp.float32
LANE = 128
VMEM_LIMIT_BYTES = 56 * 1024 * 1024
NEG_INF = float("-inf")


def _cparams(*sem):
    return pltpu.CompilerParams(dimension_semantics=sem, vmem_limit_bytes=VMEM_LIMIT_BYTES)


def _sds(shape, dtype):
    return jax.ShapeDtypeStruct(shape, dtype)


def _fit(n, tile):
    for t in range(min(tile, n) // LANE * LANE, 0, -LANE):
        if n % t == 0:
            return t
    return n


_HBM = pl.BlockSpec(memory_space=pltpu.HBM)
_SEM = pl.BlockSpec(memory_space=pltpu.SEMAPHORE)
_DATAFLOW = pltpu.SideEffectType.DATAFLOW_SIDE_EFFECTING
N_PEER = N_DEV - 1


def _mesh_pos():
    x, y, c = lax.axis_index("x"), lax.axis_index("y"), lax.axis_index("c")
    return x, y, c, 4 * x + 2 * y + c


def _remote_copies(src, land, send_sems, recv_sems, scatter):
    x, y, c, me = _mesh_pos()
    copies = []
    for k in range(1, N_DEV):
        px = (1 - x) if (k & 4) else x
        py = (1 - y) if (k & 2) else y
        pc = (1 - c) if (k & 1) else c
        copies.append(pltpu.make_async_remote_copy(
            src_ref=src.at[4 * px + 2 * py + pc] if scatter else src,
            dst_ref=land.at[k - 1] if scatter else land.at[me],
            send_sem=send_sems.at[k - 1], recv_sem=recv_sems.at[k - 1],
            device_id=(px, py, pc), device_id_type=pl.DeviceIdType.MESH))
    return copies


def _own_in_place(src, me):
    land = lax.empty((N_DEV,) + src.shape, src.dtype)
    return lax.dynamic_update_slice(land, src[None], (me,) + (0,) * src.ndim)


def _exchange_start(srcs, lands, scatter, name):
    n = len(srcs)

    def body(*refs):
        src, land = refs[:n], refs[n:2 * n]
        send, recv = refs[2 * n:3 * n], refs[3 * n:4 * n]
        token = refs[6 * n]
        for t in range(n):
            for cp in _remote_copies(src[t], land[t], send[t], recv[t], scatter):
                cp.start()
        token[...] = jnp.zeros_like(token)

    thru = [pltpu.HBM(a.shape, a.dtype) for a in list(srcs) + list(lands)]
    outs = pl.pallas_call(
        body, name=name,
        out_shape=[pltpu.SemaphoreType.DMA((N_PEER,))] * (2 * n) + thru + [_sds((8, LANE), F32)],
        in_specs=[_HBM] * (2 * n),
        out_specs=[_SEM] * (2 * n) + [_HBM] * (2 * n) + [pl.BlockSpec(memory_space=pltpu.VMEM)],
        input_output_aliases={i: 2 * n + i for i in range(2 * n)},
        compiler_params=pltpu.CompilerParams(has_side_effects=_DATAFLOW),
    )(*[pltpu.with_memory_space_constraint(a, pltpu.HBM) for a in list(srcs) + list(lands)])
    flights = [(outs[t], outs[n + t], outs[2 * n + t], outs[3 * n + t]) for t in range(n)]
    return flights, outs[4 * n]


def _exchange_wait(flight, after, scatter, name):
    send, recv, src, land = flight

    def body(src_ref, land_ref, send_ref, recv_ref, after_ref, src_out, land_out):
        for cp in _remote_copies(src_ref, land_ref, send_ref, recv_ref, scatter):
            cp.wait_send()
            cp.wait_recv()

    return pl.pallas_call(
        body, name=name, out_shape=(pltpu.HBM(src.shape, src.dtype), pltpu.HBM(land.shape, land.dtype)),
        in_specs=(_HBM, _HBM, _SEM, _SEM, pl.BlockSpec(memory_space=pl.ANY)), out_specs=(_HBM, _HBM),
        input_output_aliases={0: 0, 1: 1},
        compiler_params=pltpu.CompilerParams(has_side_effects=_DATAFLOW),
    )(src, land, send, recv, after)


def _rmsnorm_fwd(x, g, name, dep=None, tr=512):
    T, D = x.shape

    def body(x_ref, g_ref, *rest):
        h_ref, r_ref = rest[-2:]
        xv = x_ref[...]
        r = lax.rsqrt(jnp.mean(xv * xv, axis=-1, keepdims=True) + EPS)
        h_ref[...] = (xv * r * g_ref[...]).astype(h_ref.dtype)
        r_ref[...] = r

    in_specs = [pl.BlockSpec((tr, D), lambda i: (i, 0)), pl.BlockSpec((1, D), lambda i: (0, 0))]
    args = [x, g]
    if dep is not None:
        in_specs.append(pl.BlockSpec(dep.shape, lambda i: (0, 0)))
        args.append(dep)
    return pl.pallas_call(
        body, name=name, grid=(T // tr,),
        in_specs=in_specs,
        out_specs=[pl.BlockSpec((tr, D), lambda i: (i, 0)), pl.BlockSpec((tr, 1), lambda i: (i, 0))],
        out_shape=[_sds((T, D), MXU_DTYPE), _sds((T, 1), F32)],
        compiler_params=_cparams("parallel"),
    )(*args)


def _rmsnorm_bwd(dh, x, g, r, dres, name, tr=256):
    T, D = x.shape

    def body(dh_ref, x_ref, g_ref, r_ref, dres_ref, dx_ref, dxb_ref, dg_ref):
        @pl.when(pl.program_id(0) == 0)
        def _():
            dg_ref[...] = jnp.zeros_like(dg_ref)

        dh_v, xv, rv = dh_ref[...], x_ref[...], r_ref[...]
        gdy = dh_v * g_ref[...]
        mean_xg = jnp.mean(xv * gdy, axis=-1, keepdims=True)
        dx = dres_ref[...] + rv * gdy - xv * (rv * rv * rv) * mean_xg
        dx_ref[...] = dx
        dxb_ref[...] = dx.astype(dxb_ref.dtype)
        dg_ref[...] += jnp.sum(dh_v * xv * rv, axis=0, keepdims=True)

    row = lambda i: (i, 0)
    return pl.pallas_call(
        body, name=name, grid=(T // tr,),
        in_specs=[pl.BlockSpec((tr, D), row), pl.BlockSpec((tr, D), row), pl.BlockSpec((1, D), lambda i: (0, 0)),
                  pl.BlockSpec((tr, 1), row), pl.BlockSpec((tr, D), row)],
        out_specs=[pl.BlockSpec((tr, D), row), pl.BlockSpec((tr, D), row), pl.BlockSpec((1, D), lambda i: (0, 0))],
        out_shape=[_sds((T, D), F32), _sds((T, D), MXU_DTYPE), _sds((1, D), F32)],
        compiler_params=_cparams("arbitrary"),
    )(dh, x, g, r, dres)


def _mm_nn(a, b, name, out_dtype=F32, residual=None, relu2=False, tm=512, tn=512, tk=None):
    M, K = a.shape
    grouped = b.ndim == 3
    N = b.shape[0] * b.shape[2] if grouped else b.shape[1]
    tm, tn = _fit(M, tm), _fit(b.shape[2] if grouped else N, tn)
    tk = K if tk is None else _fit(K, tk)
    nk = K // tk
    if grouped:
        per = b.shape[2] // tn
        b_spec = pl.BlockSpec((None, tk, tn), lambda i, j, k: (j // per, k, j % per))
    else:
        b_spec = pl.BlockSpec((tk, tn), lambda i, j, k: (k, j))
    n_out = 2 if relu2 else 1

    def body(*refs):
        a_ref, b_ref = refs[0], refs[1]
        r_ref = refs[2] if residual is not None else None
        o = 3 if residual is not None else 2
        outs = refs[o:o + n_out]
        acc_ref = refs[o + n_out] if nk > 1 else None

        def finish(acc):
            if r_ref is not None:
                acc = acc + r_ref[...]
            outs[0][...] = acc.astype(outs[0].dtype)
            if relu2:
                rl = jnp.maximum(acc, 0.0)
                outs[1][...] = (rl * rl).astype(outs[1].dtype)

        prod = jnp.dot(a_ref[...], b_ref[...], preferred_element_type=F32)
        if nk == 1:
            finish(prod)
        else:
            k = pl.program_id(2)

            @pl.when(k == 0)
            def _():
                acc_ref[...] = prod

            @pl.when(k > 0)
            def _():
                acc_ref[...] += prod

            @pl.when(k == nk - 1)
            def _():
                finish(acc_ref[...])

    out_blk = pl.BlockSpec((tm, tn), lambda i, j, k: (i, j))
    in_specs = [pl.BlockSpec((tm, tk), lambda i, j, k: (i, k)), b_spec]
    args = [a, b]
    if residual is not None:
        in_specs.append(out_blk)
        args.append(residual)
    out_shape = [_sds((M, N), out_dtype)]
    if relu2:
        out_shape.append(_sds((M, N), MXU_DTYPE))
    outs = pl.pallas_call(
        body, name=name, grid=(M // tm, N // tn, nk),
        in_specs=in_specs, out_specs=[out_blk] * n_out, out_shape=out_shape,
        scratch_shapes=[pltpu.VMEM((tm, tn), F32)] if nk > 1 else [],
        compiler_params=_cparams("parallel", "parallel", "arbitrary"),
    )(*args)
    return outs if relu2 else outs[0]


def _mm_nt(a, b, name, out_dtype=F32, relu2_pre=None, dep=None, tm=512, tn=512, tk=None):
    M, K = a.shape
    grouped = b.ndim == 3
    N = b.shape[1] if grouped else b.shape[0]
    tm, tn = _fit(M, tm), _fit(N, tn)
    tk = K if tk is None else _fit(b.shape[2] if grouped else K, tk)
    nk = K // tk
    if grouped:
        per = b.shape[2] // tk
        b_spec = pl.BlockSpec((None, tn, tk), lambda i, j, k: (k // per, j, k % per))
    else:
        b_spec = pl.BlockSpec((tn, tk), lambda i, j, k: (j, k))

    def body(*refs):
        a_ref, b_ref = refs[0], refs[1]
        p_ref = refs[2] if relu2_pre is not None else None
        o = 2 + (relu2_pre is not None) + (dep is not None)
        out_ref = refs[o]
        acc_ref = refs[o + 1] if nk > 1 else None

        def finish(acc):
            if p_ref is not None:
                acc = acc * (2.0 * jnp.maximum(p_ref[...], 0.0))
            out_ref[...] = acc.astype(out_ref.dtype)

        prod = lax.dot_general(a_ref[...], b_ref[...], (((1,), (1,)), ((), ())), preferred_element_type=F32)
        if nk == 1:
            finish(prod)
        else:
            k = pl.program_id(2)

            @pl.when(k == 0)
            def _():
                acc_ref[...] = prod

            @pl.when(k > 0)
            def _():
                acc_ref[...] += prod

            @pl.when(k == nk - 1)
            def _():
                finish(acc_ref[...])

    out_blk = pl.BlockSpec((tm, tn), lambda i, j, k: (i, j))
    in_specs = [pl.BlockSpec((tm, tk), lambda i, j, k: (i, k)), b_spec]
    args = [a, b]
    if relu2_pre is not None:
        in_specs.append(out_blk)
        args.append(relu2_pre)
    if dep is not None:
        in_specs.append(pl.BlockSpec(dep.shape, lambda i, j, k: (0, 0)))
        args.append(dep)
    return pl.pallas_call(
        body, name=name, grid=(M // tm, N // tn, nk),
        in_specs=in_specs, out_specs=out_blk, out_shape=_sds((M, N), out_dtype),
        scratch_shapes=[pltpu.VMEM((tm, tn), F32)] if nk > 1 else [],
        compiler_params=_cparams("parallel", "parallel", "arbitrary"),
    )(*args)


def _mm_tn(a, b, name, groups=None, tm=512, tn=512, tk=1024):
    T, M = a.shape
    N = b.shape[1]
    tm, tn, tk = _fit(M, tm), _fit(N if groups is None else N // groups, tn), _fit(T, tk)
    nk = T // tk

    def body(a_ref, b_ref, out_ref, acc_ref):
        k = pl.program_id(2)
        prod = lax.dot_general(a_ref[...], b_ref[...], (((0,), (0,)), ((), ())), preferred_element_type=F32)

        @pl.when(k == 0)
        def _():
            acc_ref[...] = prod

        @pl.when(k > 0)
        def _():
            acc_ref[...] += prod

        @pl.when(k == nk - 1)
        def _():
            out_ref[...] = acc_ref[...].astype(out_ref.dtype)

    if groups is None:
        out_spec = pl.BlockSpec((tm, tn), lambda i, j, k: (i, j))
        out_shape = _sds((M, N), MXU_DTYPE)
    else:
        per = N // groups // tn
        out_spec = pl.BlockSpec((None, tm, tn), lambda i, j, k: (j // per, i, j % per))
        out_shape = _sds((groups, M, N // groups), MXU_DTYPE)
    return pl.pallas_call(
        body, name=name, grid=(M // tm, N // tn, nk),
        in_specs=[pl.BlockSpec((tk, tm), lambda i, j, k: (k, i)), pl.BlockSpec((tk, tn), lambda i, j, k: (k, j))],
        out_specs=out_spec, out_shape=out_shape,
        scratch_shapes=[pltpu.VMEM((tm, tn), F32)],
        compiler_params=_cparams("parallel", "parallel", "arbitrary"),
    )(a, b)


def _loss_and_grad(y, target, name, tr=512):
    T, D = y.shape

    def body(y_ref, t_ref, loss_ref, dx_ref, dxb_ref):
        @pl.when(pl.program_id(0) == 0)
        def _():
            loss_ref[...] = jnp.zeros_like(loss_ref)

        err = y_ref[...] - t_ref[...]
        loss_ref[...] += 0.5 * jnp.sum(jnp.mean(err * err, axis=-1, keepdims=True), axis=0, keepdims=True)
        dx = err * (1.0 / D)
        dx_ref[...] = dx
        dxb_ref[...] = dx.astype(dxb_ref.dtype)

    row = lambda i: (i, 0)
    return pl.pallas_call(
        body, name=name, grid=(T // tr,),
        in_specs=[pl.BlockSpec((tr, D), row), pl.BlockSpec((tr, D), row)],
        out_specs=[pl.BlockSpec((8, LANE), lambda i: (0, 0)), pl.BlockSpec((tr, D), row), pl.BlockSpec((tr, D), row)],
        out_shape=[_sds((8, LANE), F32), _sds((T, D), F32), _sds((T, D), MXU_DTYPE)],
        compiler_params=_cparams("arbitrary"),
    )(y, target)


def _sgu_mixed(v, w_ref, b_ref, col_head, n_heads):
    mixed = b_ref[...]
    for h in range(n_heads):
        full = jnp.dot(w_ref[h], v, preferred_element_type=F32)
        mixed = mixed + jnp.where(col_head == h, full, 0.0)
    return mixed


def _sgu_fwd(p, w_tril, bmat, name):
    T = p.shape[0]
    H = w_tril.shape[0]
    AW = H * HEAD_DIM

    def body(u_ref, v_ref, w_ref, b_ref, y_ref):
        col_head = lax.broadcasted_iota(jnp.int32, (CHUNK, AW), 1) // HEAD_DIM
        mixed = _sgu_mixed(v_ref[...].astype(MXU_DTYPE), w_ref, b_ref, col_head, H)
        y_ref[...] = (u_ref[...] * mixed).astype(y_ref.dtype)

    const3 = lambda c: (0, 0, 0)
    return pl.pallas_call(
        body, name=name, grid=(T // CHUNK,),
        in_specs=[pl.BlockSpec((CHUNK, AW), lambda c: (c, 0)), pl.BlockSpec((CHUNK, AW), lambda c: (c, 1)),
                  pl.BlockSpec((H, CHUNK, CHUNK), const3), pl.BlockSpec((CHUNK, AW), lambda c: (0, 0))],
        out_specs=pl.BlockSpec((CHUNK, AW), lambda c: (c, 0)),
        out_shape=_sds((T, AW), MXU_DTYPE),
        compiler_params=_cparams("parallel"),
    )(p, p, w_tril, bmat)


def _sgu_bwd(dymix, p, w_tril, w_tril_t, bmat, name):
    T = p.shape[0]
    H = w_tril.shape[0]
    AW = H * HEAD_DIM

    def body(dy_ref, u_ref, v_ref, w_ref, wt_ref, b_ref, du_ref, dv_ref, dw_ref, db_ref):
        @pl.when(pl.program_id(0) == 0)
        def _():
            dw_ref[...] = jnp.zeros_like(dw_ref)
            db_ref[...] = jnp.zeros_like(db_ref)

        col_head = lax.broadcasted_iota(jnp.int32, (CHUNK, AW), 1) // HEAD_DIM
        v = v_ref[...].astype(MXU_DTYPE)
        dy = dy_ref[...]
        mixed = _sgu_mixed(v, w_ref, b_ref, col_head, H)
        du_ref[...] = (dy * mixed).astype(du_ref.dtype)
        dm = dy * u_ref[...]
        db_ref[...] += dm
        dm_c = dm.astype(MXU_DTYPE)
        dv = jnp.zeros((CHUNK, AW), F32)
        for h in range(H):
            sel = col_head == h
            dv = dv + jnp.where(sel, jnp.dot(wt_ref[h], dm_c, preferred_element_type=F32), 0.0)
            dm_h = jnp.where(sel, dm, 0.0).astype(MXU_DTYPE)
            dw_ref[h] += lax.dot_general(dm_h, v, (((1,), (1,)), ((), ())), preferred_element_type=F32)
        dv_ref[...] = dv.astype(dv_ref.dtype)

    const3 = lambda c: (0, 0, 0)
    blk = pl.BlockSpec((CHUNK, AW), lambda c: (c, 0))
    return pl.pallas_call(
        body, name=name, grid=(T // CHUNK,),
        in_specs=[blk, blk, pl.BlockSpec((CHUNK, AW), lambda c: (c, 1)),
                  pl.BlockSpec((H, CHUNK, CHUNK), const3), pl.BlockSpec((H, CHUNK, CHUNK), const3),
                  pl.BlockSpec((CHUNK, AW), lambda c: (0, 0))],
        out_specs=[blk, blk, pl.BlockSpec((H, CHUNK, CHUNK), const3), pl.BlockSpec((CHUNK, AW), lambda c: (0, 0))],
        out_shape=[_sds((T, AW), MXU_DTYPE), _sds((T, AW), MXU_DTYPE), _sds((H, CHUNK, CHUNK), F32), _sds((CHUNK, AW), F32)],
        compiler_params=_cparams("arbitrary"),
    )(dymix, p, p, w_tril, w_tril_t, bmat)


def _shift_down(z, s, row):
    return jnp.where(row >= s, pltpu.roll(z, s, 0), 0.0)


def _shift_up(z, s, row, T):
    return jnp.where(row < T - s, pltpu.roll(z, T - s, 0), 0.0)


def _conv_fwd(p, w_conv, AW, name):
    T = p.shape[0]
    BW = w_conv.shape[1]
    nb = BW // LANE
    b0 = 2 * AW // LANE

    def body(b_ref, c_ref, x_ref, w_ref, y_ref):
        row = lax.broadcasted_iota(jnp.int32, (T, LANE), 0)
        z = c_ref[...] * x_ref[...]
        w0, w1, w2 = w_ref[0:1, :], w_ref[1:2, :], w_ref[2:3, :]
        conv = w2 * z + w1 * _shift_down(z, 1, row) + w0 * _shift_down(z, 2, row)
        y_ref[...] = (b_ref[...] * conv).astype(y_ref.dtype)

    return pl.pallas_call(
        body, name=name, grid=(nb,),
        in_specs=[pl.BlockSpec((T, LANE), lambda j: (0, b0 + j)), pl.BlockSpec((T, LANE), lambda j: (0, b0 + nb + j)),
                  pl.BlockSpec((T, LANE), lambda j: (0, b0 + 2 * nb + j)), pl.BlockSpec((CONV_WIDTH, LANE), lambda j: (0, j))],
        out_specs=pl.BlockSpec((T, LANE), lambda j: (0, j)),
        out_shape=_sds((T, BW), MXU_DTYPE),
        compiler_params=_cparams("parallel"),
    )(p, p, p, w_conv)


def _conv_bwd(dymix, p, w_conv, AW, name):
    T = p.shape[0]
    BW = w_conv.shape[1]
    nb = BW // LANE
    b0 = 2 * AW // LANE
    y0 = AW // LANE

    def body(dy_ref, b_ref, c_ref, x_ref, w_ref, db_ref, dc_ref, dxb_ref, dw_ref):
        row = lax.broadcasted_iota(jnp.int32, (T, LANE), 0)
        cv, xv, dy = c_ref[...], x_ref[...], dy_ref[...]
        w0, w1, w2 = w_ref[0:1, :], w_ref[1:2, :], w_ref[2:3, :]
        z = cv * xv
        z1 = _shift_down(z, 1, row)
        z2 = _shift_down(z, 2, row)
        conv = w2 * z + w1 * z1 + w0 * z2
        db_ref[...] = (dy * conv).astype(db_ref.dtype)
        dconv = dy * b_ref[...]
        dz = w2 * dconv + w1 * _shift_up(dconv, 1, row, T) + w0 * _shift_up(dconv, 2, row, T)
        dc_ref[...] = (dz * xv).astype(dc_ref.dtype)
        dxb_ref[...] = (dz * cv).astype(dxb_ref.dtype)
        dw_ref[0:1, :] = jnp.sum(dconv * z2, axis=0, keepdims=True)
        dw_ref[1:2, :] = jnp.sum(dconv * z1, axis=0, keepdims=True)
        dw_ref[2:3, :] = jnp.sum(dconv * z, axis=0, keepdims=True)

    col = lambda j: (0, j)
    return pl.pallas_call(
        body, name=name, grid=(nb,),
        in_specs=[pl.BlockSpec((T, LANE), lambda j: (0, y0 + j)),
                  pl.BlockSpec((T, LANE), lambda j: (0, b0 + j)), pl.BlockSpec((T, LANE), lambda j: (0, b0 + nb + j)),
                  pl.BlockSpec((T, LANE), lambda j: (0, b0 + 2 * nb + j)), pl.BlockSpec((CONV_WIDTH, LANE), col)],
        out_specs=[pl.BlockSpec((T, LANE), col)] * 3 + [pl.BlockSpec((CONV_WIDTH, LANE), col)],
        out_shape=[_sds((T, BW), MXU_DTYPE)] * 3 + [_sds((CONV_WIDTH, BW), F32)],
        compiler_params=_cparams("parallel"),
    )(dymix, p, p, p, w_conv)


def _head_sum(x, col_head, n_heads):
    out = jnp.zeros_like(x)
    for h in range(n_heads):
        sel = col_head == h
        out = jnp.where(sel, jnp.sum(jnp.where(sel, x, 0.0), axis=-1, keepdims=True), out)
    return out


def _head_col(x, sel):
    return jnp.max(jnp.where(sel, x, NEG_INF), axis=-1, keepdims=True)


def _head_norm(x, g, col_head, n_heads):
    r = lax.rsqrt(_head_sum(x * x, col_head, n_heads) * (1.0 / HEAD_DIM) + EPS)
    return x * r * g, r


def _head_norm_bwd(dy, x, g, r, col_head, n_heads):
    gdy = dy * g
    mean_xg = _head_sum(x * gdy, col_head, n_heads) * (1.0 / HEAD_DIM)
    return r * gdy - x * (r * r * r) * mean_xg, dy * x * r


ATT_SPAN_MIN = 512
HEADS_PER_LANES = LANE // HEAD_DIM


def _attn_geometry(T, d):
    m = max(1, ATT_SPAN_MIN // (ATT_BLK * d))
    return m, ATT_BLK * d * m, ATT_BLK * d, T // (ATT_BLK * d)


def _rows(ref, start, d):
    return ref[pl.ds(start, ATT_BLK, stride=d), :] if d > 1 else ref[pl.ds(start, ATT_BLK), :]


def _set_rows(ref, start, d, value):
    if d > 1:
        ref[pl.ds(start, ATT_BLK, stride=d), :] = value
    else:
        ref[pl.ds(start, ATT_BLK), :] = value


def _for_each_block(d, m, task):
    for j in range(m):
        if d == 1:
            task(0, j)
        else:
            lax.fori_loop(0, d, lambda r, carry, j=j: (task(r, j), carry)[1], 0)


def _head_slices():
    return [slice(h * HEAD_DIM, (h + 1) * HEAD_DIM) for h in range(HEADS_PER_LANES)]


def _attn_fwd(p, qg, kg, g, d, DP, PW, q_start, name):
    T = p.shape[0]
    B, W = ATT_BLK, LANE
    m, span, group, _ = _attn_geometry(T, d)
    c_q = (q_start + g * PW) // W
    c_k, c_v = c_q + 3 * PW // W, c_q + 6 * PW // W
    scale = HEAD_DIM ** -0.5

    def body(q_ref, k_ref, v_ref, kp_ref, vp_ref, qg_ref, kg_ref, o_ref, lse_ref):
        n = pl.program_id(1)
        col1 = lax.broadcasted_iota(jnp.int32, (B, W), 1) // HEAD_DIM
        col2 = lax.broadcasted_iota(jnp.int32, (2 * B, W), 1) // HEAD_DIM
        qi = lax.broadcasted_iota(jnp.int32, (B, 2 * B), 0)
        kj = lax.broadcasted_iota(jnp.int32, (B, 2 * B), 1)
        band = (kj >= qi) & (kj <= qi + B)
        qgv, kgv = qg_ref[...], kg_ref[...]

        def task(r, j):
            cur = j * group + r
            if j == 0:
                kp, vp = _rows(kp_ref, r, d), _rows(vp_ref, r, d)
            else:
                kp, vp = _rows(k_ref, cur - group, d), _rows(v_ref, cur - group, d)
            mask = band & ((n * m + j > 0) | (kj >= B))
            qn, _ = _head_norm(_rows(q_ref, cur, d), qgv, col1, HEADS_PER_LANES)
            kn, _ = _head_norm(jnp.concatenate([kp, _rows(k_ref, cur, d)], axis=0), kgv, col2, HEADS_PER_LANES)
            qn, kn = qn.astype(MXU_DTYPE), kn.astype(MXU_DTYPE)
            vcat = jnp.concatenate([vp, _rows(v_ref, cur, d)], axis=0).astype(MXU_DTYPE)
            o_parts, lse_parts = [], []
            for sl in _head_slices():
                s = lax.dot_general(qn[:, sl], kn[:, sl], (((1,), (1,)), ((), ())), preferred_element_type=F32) * scale
                s = jnp.where(mask, s, NEG_INF)
                mx = jnp.max(s, axis=-1, keepdims=True)
                e = jnp.exp(s - mx)
                den = jnp.sum(e, axis=-1, keepdims=True)
                o_parts.append(jnp.dot(e.astype(MXU_DTYPE), vcat[:, sl], preferred_element_type=F32) / den)
                lse_parts.append(jnp.broadcast_to(mx + jnp.log(den), (B, HEAD_DIM)))
            _set_rows(o_ref, cur, d, jnp.concatenate(o_parts, axis=1))
            _set_rows(lse_ref, cur, d, jnp.concatenate(lse_parts, axis=1))

        _for_each_block(d, m, task)

    main = lambda c0: pl.BlockSpec((span, W), lambda hp, n: (n, c0 + hp))
    prev = lambda c0: pl.BlockSpec((group, W), lambda hp, n: (jnp.maximum(n * m - 1, 0), c0 + hp))
    gain = pl.BlockSpec((1, W), lambda hp, n: (0, 0))
    out_blk = pl.BlockSpec((span, W), lambda hp, n: (n, hp))
    return pl.pallas_call(
        body, name=name, grid=(PW // W, T // span),
        in_specs=[main(c_q), main(c_k), main(c_v), prev(c_k), prev(c_v), gain, gain],
        out_specs=[out_blk, out_blk],
        out_shape=[_sds((T, PW), F32), _sds((T, PW), F32)],
        compiler_params=_cparams("parallel", "parallel"),
    )(p, p, p, p, p, qg, kg)


def _attn_bwd(p, qg, kg, lse, do, corr, g, d, PW, q_start, name):
    T = p.shape[0]
    B, W = ATT_BLK, LANE
    m, span, group, n_blocks = _attn_geometry(T, d)
    c_q = (q_start + g * PW) // W
    c_k, c_v = c_q + 3 * PW // W, c_q + 6 * PW // W
    scale = HEAD_DIM ** -0.5
    n_heads = HEADS_PER_LANES
    nt = (((1,), (1,)), ((), ()))
    tn = (((0,), (0,)), ((), ()))

    def body(q_ref, k_ref, v_ref, do_ref, l_ref, c_ref, kp_ref, vp_ref, qx_ref, dox_ref, lx_ref, cx_ref, qg_ref, kg_ref,
             dq_ref, dk_ref, dv_ref, dqg_ref, dkg_ref):
        hp, n = pl.program_id(0), pl.program_id(1)

        @pl.when((hp == 0) & (n == 0))
        def _():
            dqg_ref[...] = jnp.zeros_like(dqg_ref)
            dkg_ref[...] = jnp.zeros_like(dkg_ref)

        col = lax.broadcasted_iota(jnp.int32, (B, W), 1) // HEAD_DIM
        qgv, kgv = qg_ref[...], kg_ref[...]
        i1 = lax.broadcasted_iota(jnp.int32, (B, B), 0)
        j1 = lax.broadcasted_iota(jnp.int32, (B, B), 1)
        i2 = lax.broadcasted_iota(jnp.int32, (2 * B, B), 0)
        j2 = lax.broadcasted_iota(jnp.int32, (2 * B, B), 1)

        def task(r, j):
            cur = j * group + r
            blk = n * m + j
            q_c, k_c, v_c = _rows(q_ref, cur, d), _rows(k_ref, cur, d), _rows(v_ref, cur, d)
            do_c, l_c, c_c = _rows(do_ref, cur, d), _rows(l_ref, cur, d), _rows(c_ref, cur, d)
            if j == 0:
                k_p, v_p = _rows(kp_ref, r, d), _rows(vp_ref, r, d)
            else:
                k_p, v_p = _rows(k_ref, cur - group, d), _rows(v_ref, cur - group, d)
            if j == m - 1:
                nxt = [_rows(ref, r, d) for ref in (qx_ref, dox_ref, lx_ref, cx_ref)]
            else:
                nxt = [_rows(ref, cur + group, d) for ref in (q_ref, do_ref, l_ref, c_ref)]
            q_x, do_x, l_x, c_x = nxt
            qn_c, q_r = _head_norm(q_c, qgv, col, n_heads)
            kn_c, k_r = _head_norm(k_c, kgv, col, n_heads)
            kn_p, _ = _head_norm(k_p, kgv, col, n_heads)
            qn_x, _ = _head_norm(q_x, qgv, col, n_heads)
            kn_c, kn_p, v_c, v_p = (a.astype(MXU_DTYPE) for a in (kn_c, kn_p, v_c, v_p))
            qn_c = qn_c.astype(MXU_DTYPE)
            qn_cat = jnp.concatenate([qn_c, qn_x.astype(MXU_DTYPE)], axis=0)
            do_cb = do_c.astype(MXU_DTYPE)
            do_cat = jnp.concatenate([do_cb, do_x.astype(MXU_DTYPE)], axis=0)
            l_cat = jnp.concatenate([l_c, l_x], axis=0)
            c_cat = jnp.concatenate([c_c, c_x], axis=0)
            mask_p = (j1 >= i1) & (blk > 0)
            mask_c = ((i2 < B) & (j2 <= i2)) | ((i2 >= B) & (j2 >= i2 - B) & (blk + 1 < n_blocks))
            dqn, dkn, dv = [], [], []
            for h, sl in enumerate(_head_slices()):
                lane = slice(h * HEAD_DIM, h * HEAD_DIM + 1)
                s_p = lax.dot_general(qn_c[:, sl], kn_p[:, sl], nt, preferred_element_type=F32) * scale
                pr_p = jnp.where(mask_p, jnp.exp(s_p - l_c[:, lane]), 0.0)
                dp_p = lax.dot_general(do_cb[:, sl], v_p[:, sl], nt, preferred_element_type=F32)
                ds_p = (pr_p * (dp_p + c_c[:, lane]) * scale).astype(MXU_DTYPE)
                s_c = lax.dot_general(qn_cat[:, sl], kn_c[:, sl], nt, preferred_element_type=F32) * scale
                pr_c = jnp.where(mask_c, jnp.exp(s_c - l_cat[:, lane]), 0.0)
                dp_c = lax.dot_general(do_cat[:, sl], v_c[:, sl], nt, preferred_element_type=F32)
                ds_c = (pr_c * (dp_c + c_cat[:, lane]) * scale).astype(MXU_DTYPE)
                dqn.append(jnp.dot(ds_p, kn_p[:, sl], preferred_element_type=F32)
                           + jnp.dot(ds_c[:B], kn_c[:, sl], preferred_element_type=F32))
                dkn.append(lax.dot_general(ds_c, qn_cat[:, sl], tn, preferred_element_type=F32))
                dv.append(lax.dot_general(pr_c.astype(MXU_DTYPE), do_cat[:, sl], tn, preferred_element_type=F32))
            dq, dqg_part = _head_norm_bwd(jnp.concatenate(dqn, axis=1), q_c, qgv, q_r, col, n_heads)
            dk, dkg_part = _head_norm_bwd(jnp.concatenate(dkn, axis=1), k_c, kgv, k_r, col, n_heads)
            _set_rows(dq_ref, cur, d, dq)
            _set_rows(dk_ref, cur, d, dk)
            _set_rows(dv_ref, cur, d, jnp.concatenate(dv, axis=1))
            dqg_ref[0:1, :] += jnp.sum(dqg_part, axis=0, keepdims=True)
            dkg_ref[0:1, :] += jnp.sum(dkg_part, axis=0, keepdims=True)

        _for_each_block(d, m, task)

    n_spans = T // span
    main = lambda c0: pl.BlockSpec((span, W), lambda hp, n: (n, c0 + hp))
    prev = lambda c0: pl.BlockSpec((group, W), lambda hp, n: (jnp.maximum(n * m - 1, 0), c0 + hp))
    nxt = lambda c0: pl.BlockSpec((group, W), lambda hp, n: (jnp.minimum((n + 1) * m, n_blocks - 1), c0 + hp))
    gain = pl.BlockSpec((1, W), lambda hp, n: (0, 0))
    acc = pl.BlockSpec((8, W), lambda hp, n: (0, 0))
    own = pl.BlockSpec((span, W), lambda hp, n: (n, hp))
    dq, dk, dv, dqg, dkg = pl.pallas_call(
        body, name=name, grid=(PW // W, n_spans),
        in_specs=[main(c_q), main(c_k), main(c_v), main(0), main(0), main(0), prev(c_k), prev(c_v),
                  nxt(c_q), nxt(0), nxt(0), nxt(0), gain, gain],
        out_specs=[own, own, own, acc, acc],
        out_shape=[_sds((T, PW), F32)] * 3 + [_sds((8, W), F32)] * 2,
        compiler_params=_cparams("arbitrary", "arbitrary"),
    )(p, p, p, do, lse, corr, p, p, p, do, lse, corr, qg, kg)
    return dq, dk, dv, dqg[0], dkg[0]


def _softmax3(lses):
    mx = jnp.maximum(jnp.maximum(lses[0], lses[1]), lses[2])
    ex = [jnp.exp(l - mx) for l in lses]
    inv = 1.0 / (ex[0] + ex[1] + ex[2])
    return [e * inv for e in ex]


def _mix_fwd(os_, lses, name, tr=512):
    T, PW = os_[0].shape

    def body(o0, o1, o2, l0, l1, l2, y_ref):
        alpha = _softmax3([l0[...], l1[...], l2[...]])
        for g, o_ref in enumerate((o0, o1, o2)):
            y_ref[:, g * PW:(g + 1) * PW] = (o_ref[...] * alpha[g]).astype(y_ref.dtype)

    blk = pl.BlockSpec((tr, PW), lambda i: (i, 0))
    return pl.pallas_call(
        body, name=name, grid=(T // tr,),
        in_specs=[blk] * 6, out_specs=pl.BlockSpec((tr, 3 * PW), lambda i: (i, 0)),
        out_shape=_sds((T, 3 * PW), MXU_DTYPE),
        compiler_params=_cparams("parallel"),
    )(*os_, *lses)


def _mix_bwd(dymix, os_, lses, c_start, name, tr=512):
    T, PW = os_[0].shape
    HP = PW // HEAD_DIM
    c0 = c_start // PW

    def body(d0, d1, d2, o0, o1, o2, l0, l1, l2, do0, do1, do2, dl0, dl1, dl2):
        col_head = lax.broadcasted_iota(jnp.int32, (tr, PW), 1) // HEAD_DIM
        alpha = _softmax3([l0[...], l1[...], l2[...]])
        dys = [d0[...], d1[...], d2[...]]
        dots = [_head_sum(dy * o_ref[...], col_head, HP) for dy, o_ref in zip(dys, (o0, o1, o2))]
        mean_dot = alpha[0] * dots[0] + alpha[1] * dots[1] + alpha[2] * dots[2]
        for g, (do_ref, dl_ref) in enumerate(((do0, dl0), (do1, dl1), (do2, dl2))):
            do_ref[...] = dys[g] * alpha[g]
            dl_ref[...] = -alpha[g] * mean_dot

    blk = pl.BlockSpec((tr, PW), lambda i: (i, 0))
    dy_specs = [pl.BlockSpec((tr, PW), lambda i, g=g: (i, c0 + g)) for g in range(3)]
    outs = pl.pallas_call(
        body, name=name, grid=(T // tr,),
        in_specs=dy_specs + [blk] * 6, out_specs=[blk] * 6,
        out_shape=[_sds((T, PW), F32)] * 6,
        compiler_params=_cparams("parallel"),
    )(dymix, dymix, dymix, *os_, *lses)
    return outs[:3], outs[3:]


def _adamw_math(w, g, m, v):
    m2 = ADAM_B1 * m + (1.0 - ADAM_B1) * g
    v2 = ADAM_B2 * v + (1.0 - ADAM_B2) * (g * g)
    m_hat = m2 / (1.0 - ADAM_B1 ** ADAM_STEP)
    v_hat = v2 / (1.0 - ADAM_B2 ** ADAM_STEP)
    delta = -ADAM_LR * (m_hat / (jnp.sqrt(v_hat) + ADAM_EPS) + ADAM_WD * w)
    return delta, m2, v2


def _adamw_layer(layer, w, m, v, own, landed, me, prev, name, tr=256):
    _, R, C = w.shape
    tr = min(tr, R)

    def body(me_ref, w_ref, m_ref, v_ref, own_ref, land_ref, *rest):
        g_ref, d_ref, m2_ref, v2_ref = rest[-4:]
        g = own_ref[...].astype(F32)
        for j in range(N_PEER):
            g = g + land_ref[j].astype(F32)
        delta, m2, v2 = _adamw_math(w_ref[...], g, m_ref[...], v_ref[...])
        g_ref[...] = g
        d_ref[...] = delta
        m2_ref[...] = m2
        v2_ref[...] = v2

    lay = pl.BlockSpec((None, tr, C), lambda i, me_ref: (layer, i, 0))
    in_specs = [lay, lay, lay, pl.BlockSpec((None, tr, C), lambda i, me_ref: (me_ref[0], i, 0)),
                pl.BlockSpec((N_PEER, tr, C), lambda i, me_ref: (0, i, 0))]
    args = [me, w, m, v, own, landed]
    aliases = {}
    if prev is not None:
        in_specs += [pl.BlockSpec(memory_space=pl.ANY)] * 4
        args += list(prev)
        aliases = {6 + i: i for i in range(4)}
    return pl.pallas_call(
        body, name=name,
        grid_spec=pltpu.PrefetchScalarGridSpec(num_scalar_prefetch=1, grid=(R // tr,), in_specs=in_specs, out_specs=[lay] * 4),
        out_shape=[_sds(w.shape, F32)] * 4,
        input_output_aliases=aliases,
        compiler_params=_cparams("parallel"),
    )(*args)


def _sum_parts(parts, name):
    _, R, C = parts.shape

    def body(p_ref, out_ref):
        g = p_ref[0]
        for j in range(1, N_DEV):
            g = g + p_ref[j]
        out_ref[...] = g

    return pl.pallas_call(
        body, name=name, grid=(1,),
        in_specs=[pl.BlockSpec((N_DEV, R, C), lambda i: (0, 0, 0))], out_specs=pl.BlockSpec((R, C), lambda i: (0, 0)),
        out_shape=_sds((R, C), F32), compiler_params=_cparams("arbitrary"),
    )(parts)


def _adamw_flat(w, g, m, v, name):
    R, C = w.shape

    def body(w_ref, g_ref, m_ref, v_ref, d_ref, m2_ref, v2_ref):
        delta, m2, v2 = _adamw_math(w_ref[...], g_ref[...], m_ref[...], v_ref[...])
        d_ref[...] = delta
        m2_ref[...] = m2
        v2_ref[...] = v2

    blk = pl.BlockSpec((R, C), lambda i: (0, 0))
    return pl.pallas_call(
        body, name=name, grid=(1,), in_specs=[blk] * 4, out_specs=[blk] * 3, out_shape=[_sds((R, C), F32)] * 3,
        compiler_params=_cparams("arbitrary"),
    )(w, g, m, v)


def _pack(arrays, rows_multiple=8):
    flat = []
    for a in arrays:
        a = a.reshape(-1).astype(F32)
        flat.append(jnp.pad(a, (0, (-a.shape[0]) % LANE)))
    flat = jnp.concatenate(flat)
    flat = jnp.pad(flat, (0, (-flat.shape[0]) % (LANE * rows_multiple)))
    return flat.reshape(-1, LANE)


def _unpack(packed, shapes):
    flat = packed.reshape(-1)
    out, off = [], 0
    for s in shapes:
        size = 1
        for dim in s:
            size *= dim
        out.append(flat[off:off + size].reshape(s))
        off += size + (-size) % LANE
    return out


def _layer_fwd(x, wts, getw, dims, dep=None):
    AW, BW, PW, DP = dims["AW"], dims["BW"], dims["PW"], dims["DP"]
    q_start = 2 * AW + 3 * BW
    h, r1 = _rmsnorm_fwd(x, wts["attn_norm"], "rmsnorm_fwd", dep=dep)
    p = _mm_nn(h, getw("w_in", h), "proj_in", tm=1024, tn=512)
    y_a = _sgu_fwd(p, wts["sgu_tril"], wts["sgu_bmat"], "sgu_fwd")
    y_b = _conv_fwd(p, getw("conv_w", y_a), AW, "conv_fwd")
    os_, lses = [], []
    for g, d in enumerate(DILATIONS):
        o, lse = _attn_fwd(p, wts["q_gain"], wts["k_gain"], g, d, DP, PW, q_start, "attn_fwd_%d" % d)
        os_.append(o)
        lses.append(lse)
    y_c = _mix_fwd(os_, lses, "mix_fwd")
    ymix = jnp.concatenate([y_a, y_b, y_c], axis=1)
    x1 = _mm_nn(ymix, getw("w_out", ymix), "proj_out", residual=x, tm=1024, tn=1024)
    h2, r2 = _rmsnorm_fwd(x1, wts["mlp_norm"], "rmsnorm_fwd")
    a, hid = _mm_nn(h2, getw("w_mlp_in", h2), "mlp_in", relu2=True, tm=1024, tn=1024)
    x2 = _mm_nn(hid, getw("w_mlp_out", hid), "mlp_out", residual=x1, tm=1024, tn=1024, tk=2048)
    saved = dict(x=x, h=h, r1=r1, p=p, os=os_, lses=lses, ymix=ymix, x1=x1, h2=h2, r2=r2, a=a, hid=hid)
    return x2, saved


def _layer_bwd(dx, dxb, wts, getw, scatter, saved, dims):
    AW, BW, PW, DP = dims["AW"], dims["BW"], dims["PW"], dims["DP"]
    q_start = 2 * AW + 3 * BW
    D = dx.shape[1]
    g_w2 = _mm_tn(saved["hid"], dxb, "mlp_out_wgrad", tm=1024, tn=2048, tk=1024)
    token = scatter("w_mlp_out", g_w2.reshape(N_DEV, -1, D))
    da = _mm_nt(dxb, getw("w_mlp_out", None), "mlp_out_dgrad", out_dtype=MXU_DTYPE, relu2_pre=saved["a"], dep=token,
                tm=1024, tn=1024)
    g_w1 = _mm_tn(saved["h2"], da, "mlp_in_wgrad", groups=N_DEV, tm=2048, tn=1024, tk=1024)
    token = scatter("w_mlp_in", g_w1)
    dh2 = _mm_nt(da, getw("w_mlp_in", None), "mlp_in_dgrad", dep=token, tm=1024, tn=2048, tk=1024)
    dx1, dx1b, g_mlp_norm = _rmsnorm_bwd(dh2, saved["x1"], wts["mlp_norm"], saved["r2"], dx, "rmsnorm_bwd")
    g_wout = _mm_tn(saved["ymix"], dx1b, "proj_out_wgrad", tm=1024, tn=2048, tk=1024)
    token = scatter("w_out", g_wout.reshape(N_DEV, -1, D))
    dymix = _mm_nt(dx1b, getw("w_out", None), "proj_out_dgrad", dep=token, tm=1024, tn=1024)
    p = saved["p"]
    du, dv, g_sgu_w, g_sgu_bmat = _sgu_bwd(dymix, p, wts["sgu_tril"], wts["sgu_tril_t"], wts["sgu_bmat"], "sgu_bwd")
    d_b, d_c, d_xb, g_conv = _conv_bwd(dymix, p, getw("conv_w", None), AW, "conv_bwd")
    dos, corrs = _mix_bwd(dymix, saved["os"], saved["lses"], AW + BW, "mix_bwd")
    dqs, dks, dvs = [], [], []
    g_q = g_k = 0.0
    for g, d in enumerate(DILATIONS):
        dq, dk, dvv, dqg, dkg = _attn_bwd(p, wts["q_gain"], wts["k_gain"], saved["lses"][g], dos[g], corrs[g],
                                          g, d, PW, q_start, "attn_bwd_%d" % d)
        dqs.append(dq.astype(MXU_DTYPE))
        dks.append(dk.astype(MXU_DTYPE))
        dvs.append(dvv.astype(MXU_DTYPE))
        g_q = g_q + dqg.reshape(-1, HEAD_DIM).sum(0)
        g_k = g_k + dkg.reshape(-1, HEAD_DIM).sum(0)
    dp = jnp.concatenate([du, dv, d_b, d_c, d_xb] + dqs + dks + dvs, axis=1)
    g_win = _mm_tn(saved["h"], dp, "proj_in_wgrad", tm=2048, tn=1408, tk=1024)
    token = scatter("w_in", g_win.reshape(D, N_DEV, DP // N_DEV).transpose(1, 0, 2))
    dh = _mm_nt(dp, getw("w_in", None), "proj_in_dgrad", dep=token, tm=1024, tn=2048, tk=1408)
    dx0, dx0b, g_attn_norm = _rmsnorm_bwd(dh, saved["x"], wts["attn_norm"], saved["r1"], dx1, "rmsnorm_bwd")
    H = AW // HEAD_DIM
    tril = jnp.tril(jnp.ones((CHUNK, CHUNK), F32))
    small = [g_attn_norm.reshape(-1), g_sgu_w * tril, g_sgu_bmat.reshape(CHUNK, H, HEAD_DIM).sum(-1).T,
             g_conv, g_q, g_k, g_mlp_norm.reshape(-1)]
    return dx0, dx0b, small


def kernel(x, attn_norm, w_in, sgu_w, sgu_b, conv_w, q_norm, k_norm, w_out, mlp_norm, w_mlp_in, w_mlp_out, loss_target, m_attn_norm, m_w_in, m_sgu_w, m_sgu_b, m_conv_w, m_q_norm, m_k_norm, m_w_out, m_mlp_norm, m_w_mlp_in, m_w_mlp_out, v_attn_norm, v_w_in, v_sgu_w, v_sgu_b, v_conv_w, v_q_norm, v_k_norm, v_w_out, v_mlp_norm, v_w_mlp_in, v_w_mlp_out):
    n_layers = attn_norm.shape[0]
    T, D = x.shape[1], x.shape[2]
    H = sgu_w.shape[1]
    AW = H * HEAD_DIM
    BW = conv_w.shape[2] * N_DEV
    DP = w_in.shape[2] * N_DEV
    DMIX = w_out.shape[1] * N_DEV
    DFF = w_mlp_in.shape[2] * N_DEV
    PW = (DMIX - AW - BW) // 3
    HP = PW // HEAD_DIM
    dims = dict(AW=AW, BW=BW, PW=PW, DP=DP)
    me = 4 * lax.axis_index("x") + 2 * lax.axis_index("y") + lax.axis_index("c")

    big_names = ("w_in", "w_out", "w_mlp_in", "w_mlp_out")
    big_w = dict(zip(big_names, (w_in, w_out, w_mlp_in, w_mlp_out)))
    big_m = dict(zip(big_names, (m_w_in, m_w_out, m_w_mlp_in, m_w_mlp_out)))
    big_v = dict(zip(big_names, (v_w_in, v_w_out, v_w_mlp_in, v_w_mlp_out)))

    keys = []
    for l in range(n_layers):
        keys += [(l, nm) for nm in big_names]
    keys.insert(1, (0, "conv_w"))
    srcs = [_pack([conv_w]) if nm == "conv_w" else big_w[nm][l].astype(MXU_DTYPE) for l, nm in keys]
    flights, gather_token = _exchange_start(srcs, [_own_in_place(s, me) for s in srcs], scatter=False, name="gather_start")
    in_flight = dict(zip(keys, flights))
    relayout = dict(
        w_in=lambda g: g.transpose(1, 0, 2).reshape(D, DP), w_out=lambda g: g.reshape(DMIX, D),
        w_mlp_in=lambda g: g, w_mlp_out=lambda g: g.reshape(DFF, D),
        conv_w=lambda g: jnp.stack([_unpack(g[j], [conv_w.shape])[0] for j in range(N_DEV)], axis=2).reshape(
            n_layers, CONV_WIDTH, BW))
    gathered = {}

    def weight_getter(l):
        def getw(nm, after):
            key = (0, nm) if nm == "conv_w" else (l, nm)
            if key not in gathered:
                _, land = _exchange_wait(in_flight[key], after, scatter=False, name="gather_wait_%d_%s" % key)
                gathered[key] = relayout[nm](land)
            return gathered[key][l] if nm == "conv_w" else gathered[key]
        return getw

    tril = jnp.tril(jnp.ones((CHUNK, CHUNK), F32))
    layers = []
    for l in range(n_layers):
        w_tril = sgu_w[l] * tril
        layers.append(dict(
            attn_norm=attn_norm[l][None], mlp_norm=mlp_norm[l][None],
            sgu_tril=w_tril.astype(MXU_DTYPE), sgu_tril_t=w_tril.transpose(0, 2, 1).astype(MXU_DTYPE),
            sgu_bmat=jnp.repeat(sgu_b[l].T, HEAD_DIM, axis=1),
            q_gain=jnp.tile(q_norm[l], HEADS_PER_LANES)[None], k_gain=jnp.tile(k_norm[l], HEADS_PER_LANES)[None]))

    xs = x[0]
    saved = []
    for l in range(n_layers):
        xs, sv = _layer_fwd(xs, layers[l], weight_getter(l), dims, dep=gather_token if l == 0 else None)
        saved.append(sv)
    loss_blk, dx, dxb = _loss_and_grad(xs, loss_target[0], "loss")
    loss = lax.psum(loss_blk[0, 0], ("x", "y", "c"))

    scattering = {}

    def scatter_starter(l):
        def scatter(nm, partials):
            land = lax.empty((N_PEER,) + partials.shape[1:], partials.dtype)
            fl, tok = _exchange_start([partials], [land], scatter=True, name="scatter_start_%d_%s" % (l, nm))
            scattering[(l, nm)] = fl[0]
            return tok
        return scatter

    small = [None] * n_layers
    for l in reversed(range(n_layers)):
        dx, dxb, small[l] = _layer_bwd(dx, dxb, layers[l], weight_getter(l), scatter_starter(l), saved[l], dims)

    small_shapes = [s.shape for s in small[0]]
    small_src = [_pack([s for l in range(n_layers) for s in small[l]])]
    small_flights, small_token = _exchange_start(small_src, [_own_in_place(s, me) for s in small_src], scatter=False,
                                                 name="small_start")
    grad_x = dx[None]

    me1 = me.astype(jnp.int32).reshape(1)
    res = {nm: None for nm in big_names}
    after = small_token
    for l in reversed(range(n_layers)):
        for nm in reversed(big_names):
            own, landed = _exchange_wait(scattering[(l, nm)], after, scatter=True, name="scatter_wait_%d_%s" % (l, nm))
            res[nm] = _adamw_layer(l, big_w[nm], big_m[nm], big_v[nm], own, landed, me1, res[nm], "adamw_" + nm)
            after = res[nm][0]
    big_out = [res[nm] for nm in big_names]

    _, gathered_small = _exchange_wait(small_flights[0], after, scatter=False, name="small_wait")
    summed = _unpack(_sum_parts(gathered_small, "sum_small"), small_shapes * n_layers)
    ns = len(small_shapes)
    g_small = [jnp.stack([summed[l * ns + i] for l in range(n_layers)]) for i in range(ns)]
    g_attn_norm, g_sgu_w, g_sgu_b, g_conv_full, g_q, g_k, g_mlp_norm = g_small
    cs = conv_w.shape[2]
    g_conv = lax.dynamic_slice_in_dim(g_conv_full, me * cs, cs, axis=2)
    sm_w = (attn_norm, sgu_w, sgu_b, conv_w, q_norm, k_norm, mlp_norm)
    sm_m = (m_attn_norm, m_sgu_w, m_sgu_b, m_conv_w, m_q_norm, m_k_norm, m_mlp_norm)
    sm_v = (v_attn_norm, v_sgu_w, v_sgu_b, v_conv_w, v_q_norm, v_k_norm, v_mlp_norm)
    sm_g = (g_attn_norm, g_sgu_w, g_sgu_b, g_conv, g_q, g_k, g_mlp_norm)
    sm_res = _adamw_flat(_pack(sm_w), _pack(sm_g), _pack(sm_m), _pack(sm_v), "adamw_small")
    shapes = [w.shape for w in sm_w]
    sm_delta, sm_m2, sm_v2 = (_unpack(r, shapes) for r in sm_res)

    def ordered(small_list, big_kind):
        b = [big_out[i][big_kind] for i in range(4)]
        return [small_list[0], b[0], small_list[1], small_list[2], small_list[3], small_list[4], small_list[5],
                b[1], small_list[6], b[2], b[3]]

    return (loss, grad_x, *ordered(list(sm_g), 0), *ordered(sm_delta, 1), *ordered(sm_m2, 2), *ordered(sm_v2, 3))
```

```python
import jax
import jax.numpy as jnp
from jax import lax
from jax.experimental import pallas as pl
from jax.experimental.pallas import tpu as pltpu

N_DEV = 8
HEAD_DIM = 64
CHUNK = 128
ATT_BLK = 128
DILATIONS = (1, 4, 16)
CONV_WIDTH = 3
EPS = 1e-6
ADAM_LR = 0.001
ADAM_B1 = 0.9
ADAM_B2 = 0.999
ADAM_EPS = 1e-08
ADAM_WD = 0.01
ADAM_STEP = 10
MXU_DTYPE = jnp.bfloat16
F32 = jnp.float32
LANE = 128
VMEM_LIMIT_BYTES = 56 * 1024 * 1024
NEG_INF = float("-inf")


def _cparams(*sem):
    return pltpu.CompilerParams(dimension_semantics=sem, vmem_limit_bytes=VMEM_LIMIT_BYTES)


def _sds(shape, dtype):
    return jax.ShapeDtypeStruct(shape, dtype)


def _fit(n, tile):
    for t in range(min(tile, n) // LANE * LANE, 0, -LANE):
        if n % t == 0:
            return t
    return n


_HBM = pl.BlockSpec(memory_space=pltpu.HBM)
_SEM = pl.BlockSpec(memory_space=pltpu.SEMAPHORE)
_DATAFLOW = pltpu.SideEffectType.DATAFLOW_SIDE_EFFECTING
N_PEER = N_DEV - 1


def _mesh_pos():
    x, y, c = lax.axis_index("x"), lax.axis_index("y"), lax.axis_index("c")
    return x, y, c, 4 * x + 2 * y + c


OTHER_CHIPS = (4, 2, 6)
EXCHANGE_PEERS = dict(
    scatter=tuple(range(1, N_DEV)),
    gather_all=tuple(range(1, N_DEV)),
    gather=(1,) + OTHER_CHIPS,
    forward=OTHER_CHIPS)


def _remote_copies(src, land, send_sems, recv_sems, mode):
    x, y, c, me = _mesh_pos()
    copies = []
    for i, k in enumerate(EXCHANGE_PEERS[mode]):
        px = (1 - x) if (k & 4) else x
        py = (1 - y) if (k & 2) else y
        pc = (1 - c) if (k & 1) else c
        if mode == "scatter":
            src_ref, dst_ref, dev = src.at[4 * px + 2 * py + pc], land.at[i], (px, py, pc)
        elif mode == "forward":
            slot = 4 * px + 2 * py + c
            src_ref, dst_ref, dev = land.at[slot], land.at[slot], (x, y, 1 - c)
        else:
            src_ref, dst_ref, dev = src, land.at[me], (px, py, pc)
        copies.append(pltpu.make_async_remote_copy(
            src_ref=src_ref, dst_ref=dst_ref, send_sem=send_sems.at[i], recv_sem=recv_sems.at[i],
            device_id=dev, device_id_type=pl.DeviceIdType.MESH))
    return copies


def _own_in_place(src, me):
    land = lax.empty((N_DEV,) + src.shape, src.dtype)
    return lax.dynamic_update_slice(land, src[None], (me,) + (0,) * src.ndim)


def _exchange_start(srcs, lands, mode, name):
    n = len(lands)
    has_src = srcs is not None
    arrays = (list(srcs) if has_src else []) + list(lands)
    n_arr = len(arrays)
    n_copies = len(EXCHANGE_PEERS[mode])

    def body(*refs):
        src = refs[:n] if has_src else [None] * n
        land = refs[n_arr - n:n_arr]
        send, recv = refs[n_arr:n_arr + n], refs[n_arr + n:n_arr + 2 * n]
        token = refs[2 * n_arr + 2 * n]
        for t in range(n):
            for cp in _remote_copies(src[t], land[t], send[t], recv[t], mode):
                cp.start()
        token[...] = jnp.zeros_like(token)

    outs = pl.pallas_call(
        body, name=name,
        out_shape=([pltpu.SemaphoreType.DMA((n_copies,))] * (2 * n) + [pltpu.HBM(a.shape, a.dtype) for a in arrays]
                   + [_sds((8, LANE), F32)]),
        in_specs=[_HBM] * n_arr,
        out_specs=[_SEM] * (2 * n) + [_HBM] * n_arr + [pl.BlockSpec(memory_space=pltpu.VMEM)],
        input_output_aliases={i: 2 * n + i for i in range(n_arr)},
        compiler_params=pltpu.CompilerParams(has_side_effects=_DATAFLOW),
    )(*[pltpu.with_memory_space_constraint(a, pltpu.HBM) for a in arrays])
    thru = outs[2 * n:2 * n + n_arr]
    flights = [(outs[t], outs[n + t], thru[t] if has_src else None, thru[n_arr - n + t]) for t in range(n)]
    return flights, outs[2 * n + n_arr]


def _exchange_wait(flight, after, mode, name):
    send, recv, src, land = flight
    arrays = [land] if src is None else [src, land]
    n_arr = len(arrays)

    def body(*refs):
        src_ref = refs[0] if n_arr == 2 else None
        land_ref, send_ref, recv_ref = refs[n_arr - 1], refs[n_arr], refs[n_arr + 1]
        for cp in _remote_copies(src_ref, land_ref, send_ref, recv_ref, mode):
            cp.wait_send()
            cp.wait_recv()

    outs = pl.pallas_call(
        body, name=name, out_shape=[pltpu.HBM(a.shape, a.dtype) for a in arrays],
        in_specs=[_HBM] * n_arr + [_SEM, _SEM, pl.BlockSpec(memory_space=pl.ANY)], out_specs=[_HBM] * n_arr,
        input_output_aliases={i: i for i in range(n_arr)},
        compiler_params=pltpu.CompilerParams(has_side_effects=_DATAFLOW),
    )(*arrays, send, recv, after)
    return (None, outs[0]) if src is None else (outs[0], outs[1])


def _rmsnorm_fwd(x, g, name, dep=None, tr=512):
    T, D = x.shape

    def body(x_ref, g_ref, *rest):
        h_ref, r_ref = rest[-2:]
        xv = x_ref[...]
        r = lax.rsqrt(jnp.mean(xv * xv, axis=-1, keepdims=True) + EPS)
        h_ref[...] = (xv * r * g_ref[...]).astype(h_ref.dtype)
        r_ref[...] = r

    in_specs = [pl.BlockSpec((tr, D), lambda i: (i, 0)), pl.BlockSpec((1, D), lambda i: (0, 0))]
    args = [x, g]
    if dep is not None:
        in_specs.append(pl.BlockSpec(dep.shape, lambda i: (0, 0)))
        args.append(dep)
    return pl.pallas_call(
        body, name=name, grid=(T // tr,),
        in_specs=in_specs,
        out_specs=[pl.BlockSpec((tr, D), lambda i: (i, 0)), pl.BlockSpec((tr, 1), lambda i: (i, 0))],
        out_shape=[_sds((T, D), MXU_DTYPE), _sds((T, 1), F32)],
        compiler_params=_cparams("parallel"),
    )(*args)


def _rmsnorm_bwd(dh, x, g, r, dres, name, tr=256):
    T, D = x.shape

    def body(dh_ref, x_ref, g_ref, r_ref, dres_ref, dx_ref, dxb_ref, dg_ref):
        @pl.when(pl.program_id(0) == 0)
        def _():
            dg_ref[...] = jnp.zeros_like(dg_ref)

        dh_v, xv, rv = dh_ref[...], x_ref[...], r_ref[...]
        gdy = dh_v * g_ref[...]
        mean_xg = jnp.mean(xv * gdy, axis=-1, keepdims=True)
        dx = dres_ref[...] + rv * gdy - xv * (rv * rv * rv) * mean_xg
        dx_ref[...] = dx
        dxb_ref[...] = dx.astype(dxb_ref.dtype)
        dg_ref[...] += jnp.sum(dh_v * xv * rv, axis=0, keepdims=True)

    row = lambda i: (i, 0)
    return pl.pallas_call(
        body, name=name, grid=(T // tr,),
        in_specs=[pl.BlockSpec((tr, D), row), pl.BlockSpec((tr, D), row), pl.BlockSpec((1, D), lambda i: (0, 0)),
                  pl.BlockSpec((tr, 1), row), pl.BlockSpec((tr, D), row)],
        out_specs=[pl.BlockSpec((tr, D), row), pl.BlockSpec((tr, D), row), pl.BlockSpec((1, D), lambda i: (0, 0))],
        out_shape=[_sds((T, D), F32), _sds((T, D), MXU_DTYPE), _sds((1, D), F32)],
        compiler_params=_cparams("arbitrary"),
    )(dh, x, g, r, dres)


def _mm_nn(a, b, name, out_dtype=F32, residual=None, relu2=False, tm=512, tn=512, tk=None):
    M, K = a.shape
    grouped = b.ndim == 3
    N = b.shape[0] * b.shape[2] if grouped else b.shape[1]
    tm, tn = _fit(M, tm), _fit(b.shape[2] if grouped else N, tn)
    tk = K if tk is None else _fit(K, tk)
    nk = K // tk
    if grouped:
        per = b.shape[2] // tn
        b_spec = pl.BlockSpec((None, tk, tn), lambda i, j, k: (j // per, k, j % per))
    else:
        b_spec = pl.BlockSpec((tk, tn), lambda i, j, k: (k, j))
    n_out = 2 if relu2 else 1

    def body(*refs):
        a_ref, b_ref = refs[0], refs[1]
        r_ref = refs[2] if residual is not None else None
        o = 3 if residual is not None else 2
        outs = refs[o:o + n_out]
        acc_ref = refs[o + n_out] if nk > 1 else None

        def finish(acc):
            if r_ref is not None:
                acc = acc + r_ref[...]
            outs[0][...] = acc.astype(outs[0].dtype)
            if relu2:
                rl = jnp.maximum(acc, 0.0)
                outs[1][...] = (rl * rl).astype(outs[1].dtype)

        prod = jnp.dot(a_ref[...], b_ref[...], preferred_element_type=F32)
        if nk == 1:
            finish(prod)
        else:
            k = pl.program_id(2)

            @pl.when(k == 0)
            def _():
                acc_ref[...] = prod

            @pl.when(k > 0)
            def _():
                acc_ref[...] += prod

            @pl.when(k == nk - 1)
            def _():
                finish(acc_ref[...])

    out_blk = pl.BlockSpec((tm, tn), lambda i, j, k: (i, j))
    in_specs = [pl.BlockSpec((tm, tk), lambda i, j, k: (i, k)), b_spec]
    args = [a, b]
    if residual is not None:
        in_specs.append(out_blk)
        args.append(residual)
    out_shape = [_sds((M, N), out_dtype)]
    if relu2:
        out_shape.append(_sds((M, N), MXU_DTYPE))
    outs = pl.pallas_call(
        body, name=name, grid=(M // tm, N // tn, nk),
        in_specs=in_specs, out_specs=[out_blk] * n_out, out_shape=out_shape,
        scratch_shapes=[pltpu.VMEM((tm, tn), F32)] if nk > 1 else [],
        compiler_params=_cparams("parallel", "parallel", "arbitrary"),
    )(*args)
    return outs if relu2 else outs[0]


def _mm_nt(a, b, name, out_dtype=F32, relu2_pre=None, dep=None, tm=512, tn=512, tk=None):
    M, K = a.shape
    grouped = b.ndim == 3
    N = b.shape[1] if grouped else b.shape[0]
    tm, tn = _fit(M, tm), _fit(N, tn)
    tk = K if tk is None else _fit(b.shape[2] if grouped else K, tk)
    nk = K // tk
    if grouped:
        per = b.shape[2] // tk
        b_spec = pl.BlockSpec((None, tn, tk), lambda i, j, k: (k // per, j, k % per))
    else:
        b_spec = pl.BlockSpec((tn, tk), lambda i, j, k: (j, k))

    def body(*refs):
        a_ref, b_ref = refs[0], refs[1]
        p_ref = refs[2] if relu2_pre is not None else None
        o = 2 + (relu2_pre is not None) + (dep is not None)
        out_ref = refs[o]
        acc_ref = refs[o + 1] if nk > 1 else None

        def finish(acc):
            if p_ref is not None:
                acc = acc * (2.0 * jnp.maximum(p_ref[...], 0.0))
            out_ref[...] = acc.astype(out_ref.dtype)

        prod = lax.dot_general(a_ref[...], b_ref[...], (((1,), (1,)), ((), ())), preferred_element_type=F32)
        if nk == 1:
            finish(prod)
        else:
            k = pl.program_id(2)

            @pl.when(k == 0)
            def _():
                acc_ref[...] = prod

            @pl.when(k > 0)
            def _():
                acc_ref[...] += prod

            @pl.when(k == nk - 1)
            def _():
                finish(acc_ref[...])

    out_blk = pl.BlockSpec((tm, tn), lambda i, j, k: (i, j))
    in_specs = [pl.BlockSpec((tm, tk), lambda i, j, k: (i, k)), b_spec]
    args = [a, b]
    if relu2_pre is not None:
        in_specs.append(out_blk)
        args.append(relu2_pre)
    if dep is not None:
        in_specs.append(pl.BlockSpec(dep.shape, lambda i, j, k: (0, 0)))
        args.append(dep)
    return pl.pallas_call(
        body, name=name, grid=(M // tm, N // tn, nk),
        in_specs=in_specs, out_specs=out_blk, out_shape=_sds((M, N), out_dtype),
        scratch_shapes=[pltpu.VMEM((tm, tn), F32)] if nk > 1 else [],
        compiler_params=_cparams("parallel", "parallel", "arbitrary"),
    )(*args)


def _mm_tn(a, b, name, groups=None, tm=512, tn=512, tk=1024):
    T, M = a.shape
    N = b.shape[1]
    tm, tn, tk = _fit(M, tm), _fit(N if groups is None else N // groups, tn), _fit(T, tk)
    nk = T // tk

    def body(a_ref, b_ref, out_ref, acc_ref):
        k = pl.program_id(2)
        prod = lax.dot_general(a_ref[...], b_ref[...], (((0,), (0,)), ((), ())), preferred_element_type=F32)

        @pl.when(k == 0)
        def _():
            acc_ref[...] = prod

        @pl.when(k > 0)
        def _():
            acc_ref[...] += prod

        @pl.when(k == nk - 1)
        def _():
            out_ref[...] = acc_ref[...].astype(out_ref.dtype)

    if groups is None:
        out_spec = pl.BlockSpec((tm, tn), lambda i, j, k: (i, j))
        out_shape = _sds((M, N), MXU_DTYPE)
    else:
        per = N // groups // tn
        out_spec = pl.BlockSpec((None, tm, tn), lambda i, j, k: (j // per, i, j % per))
        out_shape = _sds((groups, M, N // groups), MXU_DTYPE)
    return pl.pallas_call(
        body, name=name, grid=(M // tm, N // tn, nk),
        in_specs=[pl.BlockSpec((tk, tm), lambda i, j, k: (k, i)), pl.BlockSpec((tk, tn), lambda i, j, k: (k, j))],
        out_specs=out_spec, out_shape=out_shape,
        scratch_shapes=[pltpu.VMEM((tm, tn), F32)],
        compiler_params=_cparams("parallel", "parallel", "arbitrary"),
    )(a, b)


def _loss_and_grad(y, target, name, tr=512):
    T, D = y.shape

    def body(y_ref, t_ref, loss_ref, dx_ref, dxb_ref):
        @pl.when(pl.program_id(0) == 0)
        def _():
            loss_ref[...] = jnp.zeros_like(loss_ref)

        err = y_ref[...] - t_ref[...]
        loss_ref[...] += 0.5 * jnp.sum(jnp.mean(err * err, axis=-1, keepdims=True), axis=0, keepdims=True)
        dx = err * (1.0 / D)
        dx_ref[...] = dx
        dxb_ref[...] = dx.astype(dxb_ref.dtype)

    row = lambda i: (i, 0)
    return pl.pallas_call(
        body, name=name, grid=(T // tr,),
        in_specs=[pl.BlockSpec((tr, D), row), pl.BlockSpec((tr, D), row)],
        out_specs=[pl.BlockSpec((8, LANE), lambda i: (0, 0)), pl.BlockSpec((tr, D), row), pl.BlockSpec((tr, D), row)],
        out_shape=[_sds((8, LANE), F32), _sds((T, D), F32), _sds((T, D), MXU_DTYPE)],
        compiler_params=_cparams("arbitrary"),
    )(y, target)


def _sgu_mixed(v, w_ref, b_ref, col_head, n_heads):
    mixed = b_ref[...]
    for h in range(n_heads):
        full = jnp.dot(w_ref[h], v, preferred_element_type=F32)
        mixed = mixed + jnp.where(col_head == h, full, 0.0)
    return mixed


def _sgu_fwd(p, w_tril, bmat, name):
    T = p.shape[0]
    H = w_tril.shape[0]
    AW = H * HEAD_DIM

    def body(u_ref, v_ref, w_ref, b_ref, y_ref):
        col_head = lax.broadcasted_iota(jnp.int32, (CHUNK, AW), 1) // HEAD_DIM
        mixed = _sgu_mixed(v_ref[...].astype(MXU_DTYPE), w_ref, b_ref, col_head, H)
        y_ref[...] = (u_ref[...] * mixed).astype(y_ref.dtype)

    const3 = lambda c: (0, 0, 0)
    return pl.pallas_call(
        body, name=name, grid=(T // CHUNK,),
        in_specs=[pl.BlockSpec((CHUNK, AW), lambda c: (c, 0)), pl.BlockSpec((CHUNK, AW), lambda c: (c, 1)),
                  pl.BlockSpec((H, CHUNK, CHUNK), const3), pl.BlockSpec((CHUNK, AW), lambda c: (0, 0))],
        out_specs=pl.BlockSpec((CHUNK, AW), lambda c: (c, 0)),
        out_shape=_sds((T, AW), MXU_DTYPE),
        compiler_params=_cparams("parallel"),
    )(p, p, w_tril, bmat)


def _sgu_bwd(dymix, p, w_tril, w_tril_t, bmat, name):
    T = p.shape[0]
    H = w_tril.shape[0]
    AW = H * HEAD_DIM

    def body(dy_ref, u_ref, v_ref, w_ref, wt_ref, b_ref, du_ref, dv_ref, dw_ref, db_ref):
        @pl.when(pl.program_id(0) == 0)
        def _():
            dw_ref[...] = jnp.zeros_like(dw_ref)
            db_ref[...] = jnp.zeros_like(db_ref)

        col_head = lax.broadcasted_iota(jnp.int32, (CHUNK, AW), 1) // HEAD_DIM
        v = v_ref[...].astype(MXU_DTYPE)
        dy = dy_ref[...]
        mixed = _sgu_mixed(v, w_ref, b_ref, col_head, H)
        du_ref[...] = (dy * mixed).astype(du_ref.dtype)
        dm = dy * u_ref[...]
        db_ref[...] += dm
        dm_c = dm.astype(MXU_DTYPE)
        dv = jnp.zeros((CHUNK, AW), F32)
        for h in range(H):
            sel = col_head == h
            dv = dv + jnp.where(sel, jnp.dot(wt_ref[h], dm_c, preferred_element_type=F32), 0.0)
            dm_h = jnp.where(sel, dm, 0.0).astype(MXU_DTYPE)
            dw_ref[h] += lax.dot_general(dm_h, v, (((1,), (1,)), ((), ())), preferred_element_type=F32)
        dv_ref[...] = dv.astype(dv_ref.dtype)

    const3 = lambda c: (0, 0, 0)
    blk = pl.BlockSpec((CHUNK, AW), lambda c: (c, 0))
    return pl.pallas_call(
        body, name=name, grid=(T // CHUNK,),
        in_specs=[blk, blk, pl.BlockSpec((CHUNK, AW), lambda c: (c, 1)),
                  pl.BlockSpec((H, CHUNK, CHUNK), const3), pl.BlockSpec((H, CHUNK, CHUNK), const3),
                  pl.BlockSpec((CHUNK, AW), lambda c: (0, 0))],
        out_specs=[blk, blk, pl.BlockSpec((H, CHUNK, CHUNK), const3), pl.BlockSpec((CHUNK, AW), lambda c: (0, 0))],
        out_shape=[_sds((T, AW), MXU_DTYPE), _sds((T, AW), MXU_DTYPE), _sds((H, CHUNK, CHUNK), F32), _sds((CHUNK, AW), F32)],
        compiler_params=_cparams("arbitrary"),
    )(dymix, p, p, w_tril, w_tril_t, bmat)


def _shift_down(z, s, row):
    return jnp.where(row >= s, pltpu.roll(z, s, 0), 0.0)


def _shift_up(z, s, row, T):
    return jnp.where(row < T - s, pltpu.roll(z, T - s, 0), 0.0)


def _conv_fwd(p, w_conv, AW, name):
    T = p.shape[0]
    BW = w_conv.shape[1]
    nb = BW // LANE
    b0 = 2 * AW // LANE

    def body(b_ref, c_ref, x_ref, w_ref, y_ref):
        row = lax.broadcasted_iota(jnp.int32, (T, LANE), 0)
        z = c_ref[...] * x_ref[...]
        w0, w1, w2 = w_ref[0:1, :], w_ref[1:2, :], w_ref[2:3, :]
        conv = w2 * z + w1 * _shift_down(z, 1, row) + w0 * _shift_down(z, 2, row)
        y_ref[...] = (b_ref[...] * conv).astype(y_ref.dtype)

    return pl.pallas_call(
        body, name=name, grid=(nb,),
        in_specs=[pl.BlockSpec((T, LANE), lambda j: (0, b0 + j)), pl.BlockSpec((T, LANE), lambda j: (0, b0 + nb + j)),
                  pl.BlockSpec((T, LANE), lambda j: (0, b0 + 2 * nb + j)), pl.BlockSpec((CONV_WIDTH, LANE), lambda j: (0, j))],
        out_specs=pl.BlockSpec((T, LANE), lambda j: (0, j)),
        out_shape=_sds((T, BW), MXU_DTYPE),
        compiler_params=_cparams("parallel"),
    )(p, p, p, w_conv)


def _conv_bwd(dymix, p, w_conv, AW, name):
    T = p.shape[0]
    BW = w_conv.shape[1]
    nb = BW // LANE
    b0 = 2 * AW // LANE
    y0 = AW // LANE

    def body(dy_ref, b_ref, c_ref, x_ref, w_ref, db_ref, dc_ref, dxb_ref, dw_ref):
        row = lax.broadcasted_iota(jnp.int32, (T, LANE), 0)
        cv, xv, dy = c_ref[...], x_ref[...], dy_ref[...]
        w0, w1, w2 = w_ref[0:1, :], w_ref[1:2, :], w_ref[2:3, :]
        z = cv * xv
        z1 = _shift_down(z, 1, row)
        z2 = _shift_down(z, 2, row)
        conv = w2 * z + w1 * z1 + w0 * z2
        db_ref[...] = (dy * conv).astype(db_ref.dtype)
        dconv = dy * b_ref[...]
        dz = w2 * dconv + w1 * _shift_up(dconv, 1, row, T) + w0 * _shift_up(dconv, 2, row, T)
        dc_ref[...] = (dz * xv).astype(dc_ref.dtype)
        dxb_ref[...] = (dz * cv).astype(dxb_ref.dtype)
        dw_ref[0:1, :] = jnp.sum(dconv * z2, axis=0, keepdims=True)
        dw_ref[1:2, :] = jnp.sum(dconv * z1, axis=0, keepdims=True)
        dw_ref[2:3, :] = jnp.sum(dconv * z, axis=0, keepdims=True)

    col = lambda j: (0, j)
    return pl.pallas_call(
        body, name=name, grid=(nb,),
        in_specs=[pl.BlockSpec((T, LANE), lambda j: (0, y0 + j)),
                  pl.BlockSpec((T, LANE), lambda j: (0, b0 + j)), pl.BlockSpec((T, LANE), lambda j: (0, b0 + nb + j)),
                  pl.BlockSpec((T, LANE), lambda j: (0, b0 + 2 * nb + j)), pl.BlockSpec((CONV_WIDTH, LANE), col)],
        out_specs=[pl.BlockSpec((T, LANE), col)] * 3 + [pl.BlockSpec((CONV_WIDTH, LANE), col)],
        out_shape=[_sds((T, BW), MXU_DTYPE)] * 3 + [_sds((CONV_WIDTH, BW), F32)],
        compiler_params=_cparams("parallel"),
    )(dymix, p, p, p, w_conv)


def _head_sum(x, col_head, n_heads):
    out = jnp.zeros_like(x)
    for h in range(n_heads):
        sel = col_head == h
        out = jnp.where(sel, jnp.sum(jnp.where(sel, x, 0.0), axis=-1, keepdims=True), out)
    return out


def _head_col(x, sel):
    return jnp.max(jnp.where(sel, x, NEG_INF), axis=-1, keepdims=True)


def _head_norm(x, g, col_head, n_heads):
    r = lax.rsqrt(_head_sum(x * x, col_head, n_heads) * (1.0 / HEAD_DIM) + EPS)
    return x * r * g, r


def _head_norm_bwd(dy, x, g, r, col_head, n_heads):
    gdy = dy * g
    mean_xg = _head_sum(x * gdy, col_head, n_heads) * (1.0 / HEAD_DIM)
    return r * gdy - x * (r * r * r) * mean_xg, dy * x * r


ATT_SPAN_MIN = 512
HEADS_PER_LANES = LANE // HEAD_DIM


def _attn_geometry(T, d):
    m = max(1, ATT_SPAN_MIN // (ATT_BLK * d))
    return m, ATT_BLK * d * m, ATT_BLK * d, T // (ATT_BLK * d)


def _rows(ref, start, d):
    return ref[pl.ds(start, ATT_BLK, stride=d), :] if d > 1 else ref[pl.ds(start, ATT_BLK), :]


def _set_rows(ref, start, d, value):
    if d > 1:
        ref[pl.ds(start, ATT_BLK, stride=d), :] = value
    else:
        ref[pl.ds(start, ATT_BLK), :] = value


def _for_each_block(d, m, task):
    for j in range(m):
        if d == 1:
            task(0, j)
        else:
            lax.fori_loop(0, d, lambda r, carry, j=j: (task(r, j), carry)[1], 0)


def _head_slices():
    return [slice(h * HEAD_DIM, (h + 1) * HEAD_DIM) for h in range(HEADS_PER_LANES)]


def _attn_fwd(p, qg, kg, g, d, DP, PW, q_start, name):
    T = p.shape[0]
    B, W = ATT_BLK, LANE
    m, span, group, _ = _attn_geometry(T, d)
    c_q = (q_start + g * PW) // W
    c_k, c_v = c_q + 3 * PW // W, c_q + 6 * PW // W
    scale = HEAD_DIM ** -0.5

    def body(q_ref, k_ref, v_ref, kp_ref, vp_ref, qg_ref, kg_ref, o_ref, lse_ref):
        n = pl.program_id(1)
        col1 = lax.broadcasted_iota(jnp.int32, (B, W), 1) // HEAD_DIM
        col2 = lax.broadcasted_iota(jnp.int32, (2 * B, W), 1) // HEAD_DIM
        qi = lax.broadcasted_iota(jnp.int32, (B, 2 * B), 0)
        kj = lax.broadcasted_iota(jnp.int32, (B, 2 * B), 1)
        band = (kj >= qi) & (kj <= qi + B)
        qgv, kgv = qg_ref[...], kg_ref[...]

        def task(r, j):
            cur = j * group + r
            if j == 0:
                kp, vp = _rows(kp_ref, r, d), _rows(vp_ref, r, d)
            else:
                kp, vp = _rows(k_ref, cur - group, d), _rows(v_ref, cur - group, d)
            mask = band & ((n * m + j > 0) | (kj >= B))
            qn, _ = _head_norm(_rows(q_ref, cur, d), qgv, col1, HEADS_PER_LANES)
            kn, _ = _head_norm(jnp.concatenate([kp, _rows(k_ref, cur, d)], axis=0), kgv, col2, HEADS_PER_LANES)
            qn, kn = qn.astype(MXU_DTYPE), kn.astype(MXU_DTYPE)
            vcat = jnp.concatenate([vp, _rows(v_ref, cur, d)], axis=0).astype(MXU_DTYPE)
            o_parts, lse_parts = [], []
            for sl in _head_slices():
                s = lax.dot_general(qn[:, sl], kn[:, sl], (((1,), (1,)), ((), ())), preferred_element_type=F32) * scale
                s = jnp.where(mask, s, NEG_INF)
                mx = jnp.max(s, axis=-1, keepdims=True)
                e = jnp.exp(s - mx)
                den = jnp.sum(e, axis=-1, keepdims=True)
                o_parts.append(jnp.dot(e.astype(MXU_DTYPE), vcat[:, sl], preferred_element_type=F32) / den)
                lse_parts.append(jnp.broadcast_to(mx + jnp.log(den), (B, HEAD_DIM)))
            _set_rows(o_ref, cur, d, jnp.concatenate(o_parts, axis=1))
            _set_rows(lse_ref, cur, d, jnp.concatenate(lse_parts, axis=1))

        _for_each_block(d, m, task)

    main = lambda c0: pl.BlockSpec((span, W), lambda hp, n: (n, c0 + hp))
    prev = lambda c0: pl.BlockSpec((group, W), lambda hp, n: (jnp.maximum(n * m - 1, 0), c0 + hp))
    gain = pl.BlockSpec((1, W), lambda hp, n: (0, 0))
    out_blk = pl.BlockSpec((span, W), lambda hp, n: (n, hp))
    return pl.pallas_call(
        body, name=name, grid=(PW // W, T // span),
        in_specs=[main(c_q), main(c_k), main(c_v), prev(c_k), prev(c_v), gain, gain],
        out_specs=[out_blk, out_blk],
        out_shape=[_sds((T, PW), F32), _sds((T, PW), F32)],
        compiler_params=_cparams("parallel", "parallel"),
    )(p, p, p, p, p, qg, kg)


def _attn_bwd(p, qg, kg, lse, do, corr, g, d, PW, q_start, name):
    T = p.shape[0]
    B, W = ATT_BLK, LANE
    m, span, group, n_blocks = _attn_geometry(T, d)
    c_q = (q_start + g * PW) // W
    c_k, c_v = c_q + 3 * PW // W, c_q + 6 * PW // W
    scale = HEAD_DIM ** -0.5
    n_heads = HEADS_PER_LANES
    nt = (((1,), (1,)), ((), ()))
    tn = (((0,), (0,)), ((), ()))

    def body(q_ref, k_ref, v_ref, do_ref, l_ref, c_ref, kp_ref, vp_ref, qx_ref, dox_ref, lx_ref, cx_ref, qg_ref, kg_ref,
             dq_ref, dk_ref, dv_ref, dqg_ref, dkg_ref):
        hp, n = pl.program_id(0), pl.program_id(1)

        @pl.when((hp == 0) & (n == 0))
        def _():
            dqg_ref[...] = jnp.zeros_like(dqg_ref)
            dkg_ref[...] = jnp.zeros_like(dkg_ref)

        col = lax.broadcasted_iota(jnp.int32, (B, W), 1) // HEAD_DIM
        qgv, kgv = qg_ref[...], kg_ref[...]
        i1 = lax.broadcasted_iota(jnp.int32, (B, B), 0)
        j1 = lax.broadcasted_iota(jnp.int32, (B, B), 1)
        i2 = lax.broadcasted_iota(jnp.int32, (2 * B, B), 0)
        j2 = lax.broadcasted_iota(jnp.int32, (2 * B, B), 1)

        def task(r, j):
            cur = j * group + r
            blk = n * m + j
            q_c, k_c, v_c = _rows(q_ref, cur, d), _rows(k_ref, cur, d), _rows(v_ref, cur, d)
            do_c, l_c, c_c = _rows(do_ref, cur, d), _rows(l_ref, cur, d), _rows(c_ref, cur, d)
            if j == 0:
                k_p, v_p = _rows(kp_ref, r, d), _rows(vp_ref, r, d)
            else:
                k_p, v_p = _rows(k_ref, cur - group, d), _rows(v_ref, cur - group, d)
            if j == m - 1:
                nxt = [_rows(ref, r, d) for ref in (qx_ref, dox_ref, lx_ref, cx_ref)]
            else:
                nxt = [_rows(ref, cur + group, d) for ref in (q_ref, do_ref, l_ref, c_ref)]
            q_x, do_x, l_x, c_x = nxt
            qn_c, q_r = _head_norm(q_c, qgv, col, n_heads)
            kn_c, k_r = _head_norm(k_c, kgv, col, n_heads)
            kn_p, _ = _head_norm(k_p, kgv, col, n_heads)
            qn_x, _ = _head_norm(q_x, qgv, col, n_heads)
            kn_c, kn_p, v_c, v_p = (a.astype(MXU_DTYPE) for a in (kn_c, kn_p, v_c, v_p))
            qn_c = qn_c.astype(MXU_DTYPE)
            qn_cat = jnp.concatenate([qn_c, qn_x.astype(MXU_DTYPE)], axis=0)
            do_cb = do_c.astype(MXU_DTYPE)
            do_cat = jnp.concatenate([do_cb, do_x.astype(MXU_DTYPE)], axis=0)
            l_cat = jnp.concatenate([l_c, l_x], axis=0)
            c_cat = jnp.concatenate([c_c, c_x], axis=0)
            mask_p = (j1 >= i1) & (blk > 0)
            mask_c = ((i2 < B) & (j2 <= i2)) | ((i2 >= B) & (j2 >= i2 - B) & (blk + 1 < n_blocks))
            dqn, dkn, dv = [], [], []
            for h, sl in enumerate(_head_slices()):
                lane = slice(h * HEAD_DIM, h * HEAD_DIM + 1)
                s_p = lax.dot_general(qn_c[:, sl], kn_p[:, sl], nt, preferred_element_type=F32) * scale
                pr_p = jnp.where(mask_p, jnp.exp(s_p - l_c[:, lane]), 0.0)
                dp_p = lax.dot_general(do_cb[:, sl], v_p[:, sl], nt, preferred_element_type=F32)
                ds_p = (pr_p * (dp_p + c_c[:, lane]) * scale).astype(MXU_DTYPE)
                s_c = lax.dot_general(qn_cat[:, sl], kn_c[:, sl], nt, preferred_element_type=F32) * scale
                pr_c = jnp.where(mask_c, jnp.exp(s_c - l_cat[:, lane]), 0.0)
                dp_c = lax.dot_general(do_cat[:, sl], v_c[:, sl], nt, preferred_element_type=F32)
                ds_c = (pr_c * (dp_c + c_cat[:, lane]) * scale).astype(MXU_DTYPE)
                dqn.append(jnp.dot(ds_p, kn_p[:, sl], preferred_element_type=F32)
                           + jnp.dot(ds_c[:B], kn_c[:, sl], preferred_element_type=F32))
                dkn.append(lax.dot_general(ds_c, qn_cat[:, sl], tn, preferred_element_type=F32))
                dv.append(lax.dot_general(pr_c.astype(MXU_DTYPE), do_cat[:, sl], tn, preferred_element_type=F32))
            dq, dqg_part = _head_norm_bwd(jnp.concatenate(dqn, axis=1), q_c, qgv, q_r, col, n_heads)
            dk, dkg_part = _head_norm_bwd(jnp.concatenate(dkn, axis=1), k_c, kgv, k_r, col, n_heads)
            _set_rows(dq_ref, cur, d, dq)
            _set_rows(dk_ref, cur, d, dk)
            _set_rows(dv_ref, cur, d, jnp.concatenate(dv, axis=1))
            dqg_ref[0:1, :] += jnp.sum(dqg_part, axis=0, keepdims=True)
            dkg_ref[0:1, :] += jnp.sum(dkg_part, axis=0, keepdims=True)

        _for_each_block(d, m, task)

    n_spans = T // span
    main = lambda c0: pl.BlockSpec((span, W), lambda hp, n: (n, c0 + hp))
    prev = lambda c0: pl.BlockSpec((group, W), lambda hp, n: (jnp.maximum(n * m - 1, 0), c0 + hp))
    nxt = lambda c0: pl.BlockSpec((group, W), lambda hp, n: (jnp.minimum((n + 1) * m, n_blocks - 1), c0 + hp))
    gain = pl.BlockSpec((1, W), lambda hp, n: (0, 0))
    acc = pl.BlockSpec((8, W), lambda hp, n: (0, 0))
    own = pl.BlockSpec((span, W), lambda hp, n: (n, hp))
    dq, dk, dv, dqg, dkg = pl.pallas_call(
        body, name=name, grid=(PW // W, n_spans),
        in_specs=[main(c_q), main(c_k), main(c_v), main(0), main(0), main(0), prev(c_k), prev(c_v),
                  nxt(c_q), nxt(0), nxt(0), nxt(0), gain, gain],
        out_specs=[own, own, own, acc, acc],
        out_shape=[_sds((T, PW), F32)] * 3 + [_sds((8, W), F32)] * 2,
        compiler_params=_cparams("arbitrary", "arbitrary"),
    )(p, p, p, do, lse, corr, p, p, p, do, lse, corr, qg, kg)
    return dq, dk, dv, dqg[0], dkg[0]


def _softmax3(lses):
    mx = jnp.maximum(jnp.maximum(lses[0], lses[1]), lses[2])
    ex = [jnp.exp(l - mx) for l in lses]
    inv = 1.0 / (ex[0] + ex[1] + ex[2])
    return [e * inv for e in ex]


def _mix_fwd(os_, lses, name, tr=512):
    T, PW = os_[0].shape

    def body(o0, o1, o2, l0, l1, l2, y_ref):
        alpha = _softmax3([l0[...], l1[...], l2[...]])
        for g, o_ref in enumerate((o0, o1, o2)):
            y_ref[:, g * PW:(g + 1) * PW] = (o_ref[...] * alpha[g]).astype(y_ref.dtype)

    blk = pl.BlockSpec((tr, PW), lambda i: (i, 0))
    return pl.pallas_call(
        body, name=name, grid=(T // tr,),
        in_specs=[blk] * 6, out_specs=pl.BlockSpec((tr, 3 * PW), lambda i: (i, 0)),
        out_shape=_sds((T, 3 * PW), MXU_DTYPE),
        compiler_params=_cparams("parallel"),
    )(*os_, *lses)


def _mix_bwd(dymix, os_, lses, c_start, name, tr=512):
    T, PW = os_[0].shape
    HP = PW // HEAD_DIM
    c0 = c_start // PW

    def body(d0, d1, d2, o0, o1, o2, l0, l1, l2, do0, do1, do2, dl0, dl1, dl2):
        col_head = lax.broadcasted_iota(jnp.int32, (tr, PW), 1) // HEAD_DIM
        alpha = _softmax3([l0[...], l1[...], l2[...]])
        dys = [d0[...], d1[...], d2[...]]
        dots = [_head_sum(dy * o_ref[...], col_head, HP) for dy, o_ref in zip(dys, (o0, o1, o2))]
        mean_dot = alpha[0] * dots[0] + alpha[1] * dots[1] + alpha[2] * dots[2]
        for g, (do_ref, dl_ref) in enumerate(((do0, dl0), (do1, dl1), (do2, dl2))):
            do_ref[...] = dys[g] * alpha[g]
            dl_ref[...] = -alpha[g] * mean_dot

    blk = pl.BlockSpec((tr, PW), lambda i: (i, 0))
    dy_specs = [pl.BlockSpec((tr, PW), lambda i, g=g: (i, c0 + g)) for g in range(3)]
    outs = pl.pallas_call(
        body, name=name, grid=(T // tr,),
        in_specs=dy_specs + [blk] * 6, out_specs=[blk] * 6,
        out_shape=[_sds((T, PW), F32)] * 6,
        compiler_params=_cparams("parallel"),
    )(dymix, dymix, dymix, *os_, *lses)
    return outs[:3], outs[3:]


def _adamw_math(w, g, m, v):
    m2 = ADAM_B1 * m + (1.0 - ADAM_B1) * g
    v2 = ADAM_B2 * v + (1.0 - ADAM_B2) * (g * g)
    m_hat = m2 / (1.0 - ADAM_B1 ** ADAM_STEP)
    v_hat = v2 / (1.0 - ADAM_B2 ** ADAM_STEP)
    delta = -ADAM_LR * (m_hat / (jnp.sqrt(v_hat) + ADAM_EPS) + ADAM_WD * w)
    return delta, m2, v2


def _adamw_layer(layer, w, m, v, own, landed, me, prev, name, tr=256):
    _, R, C = w.shape
    tr = min(tr, R)

    def body(me_ref, w_ref, m_ref, v_ref, own_ref, land_ref, *rest):
        g_ref, d_ref, m2_ref, v2_ref = rest[-4:]
        g = own_ref[...].astype(F32)
        for j in range(N_PEER):
            g = g + land_ref[j].astype(F32)
        delta, m2, v2 = _adamw_math(w_ref[...], g, m_ref[...], v_ref[...])
        g_ref[...] = g
        d_ref[...] = delta
        m2_ref[...] = m2
        v2_ref[...] = v2

    lay = pl.BlockSpec((None, tr, C), lambda i, me_ref: (layer, i, 0))
    in_specs = [lay, lay, lay, pl.BlockSpec((None, tr, C), lambda i, me_ref: (me_ref[0], i, 0)),
                pl.BlockSpec((N_PEER, tr, C), lambda i, me_ref: (0, i, 0))]
    args = [me, w, m, v, own, landed]
    aliases = {}
    if prev is not None:
        in_specs += [pl.BlockSpec(memory_space=pl.ANY)] * 4
        args += list(prev)
        aliases = {6 + i: i for i in range(4)}
    return pl.pallas_call(
        body, name=name,
        grid_spec=pltpu.PrefetchScalarGridSpec(num_scalar_prefetch=1, grid=(R // tr,), in_specs=in_specs, out_specs=[lay] * 4),
        out_shape=[_sds(w.shape, F32)] * 4,
        input_output_aliases=aliases,
        compiler_params=_cparams("parallel"),
    )(*args)


def _sum_parts(parts, name):
    _, R, C = parts.shape

    def body(p_ref, out_ref):
        g = p_ref[0]
        for j in range(1, N_DEV):
            g = g + p_ref[j]
        out_ref[...] = g

    return pl.pallas_call(
        body, name=name, grid=(1,),
        in_specs=[pl.BlockSpec((N_DEV, R, C), lambda i: (0, 0, 0))], out_specs=pl.BlockSpec((R, C), lambda i: (0, 0)),
        out_shape=_sds((R, C), F32), compiler_params=_cparams("arbitrary"),
    )(parts)


def _adamw_flat(w, g, m, v, name):
    R, C = w.shape

    def body(w_ref, g_ref, m_ref, v_ref, d_ref, m2_ref, v2_ref):
        delta, m2, v2 = _adamw_math(w_ref[...], g_ref[...], m_ref[...], v_ref[...])
        d_ref[...] = delta
        m2_ref[...] = m2
        v2_ref[...] = v2

    blk = pl.BlockSpec((R, C), lambda i: (0, 0))
    return pl.pallas_call(
        body, name=name, grid=(1,), in_specs=[blk] * 4, out_specs=[blk] * 3, out_shape=[_sds((R, C), F32)] * 3,
        compiler_params=_cparams("arbitrary"),
    )(w, g, m, v)


def _pack(arrays, rows_multiple=8):
    flat = []
    for a in arrays:
        a = a.reshape(-1).astype(F32)
        flat.append(jnp.pad(a, (0, (-a.shape[0]) % LANE)))
    flat = jnp.concatenate(flat)
    flat = jnp.pad(flat, (0, (-flat.shape[0]) % (LANE * rows_multiple)))
    return flat.reshape(-1, LANE)


def _unpack(packed, shapes):
    flat = packed.reshape(-1)
    out, off = [], 0
    for s in shapes:
        size = 1
        for dim in s:
            size *= dim
        out.append(flat[off:off + size].reshape(s))
        off += size + (-size) % LANE
    return out


def _layer_fwd(x, wts, getw, dims, dep=None):
    AW, BW, PW, DP = dims["AW"], dims["BW"], dims["PW"], dims["DP"]
    q_start = 2 * AW + 3 * BW
    h, r1 = _rmsnorm_fwd(x, wts["attn_norm"], "rmsnorm_fwd", dep=dep)
    p = _mm_nn(h, getw("w_in", h), "proj_in", tm=1024, tn=512)
    y_a = _sgu_fwd(p, wts["sgu_tril"], wts["sgu_bmat"], "sgu_fwd")
    y_b = _conv_fwd(p, getw("conv_w", y_a), AW, "conv_fwd")
    os_, lses = [], []
    for g, d in enumerate(DILATIONS):
        o, lse = _attn_fwd(p, wts["q_gain"], wts["k_gain"], g, d, DP, PW, q_start, "attn_fwd_%d" % d)
        os_.append(o)
        lses.append(lse)
    y_c = _mix_fwd(os_, lses, "mix_fwd")
    ymix = jnp.concatenate([y_a, y_b, y_c], axis=1)
    x1 = _mm_nn(ymix, getw("w_out", ymix), "proj_out", residual=x, tm=1024, tn=1024)
    h2, r2 = _rmsnorm_fwd(x1, wts["mlp_norm"], "rmsnorm_fwd")
    a, hid = _mm_nn(h2, getw("w_mlp_in", h2), "mlp_in", relu2=True, tm=1024, tn=1024)
    x2 = _mm_nn(hid, getw("w_mlp_out", hid), "mlp_out", residual=x1, tm=1024, tn=1024, tk=2048)
    saved = dict(x=x, h=h, r1=r1, p=p, os=os_, lses=lses, ymix=ymix, x1=x1, h2=h2, r2=r2, a=a, hid=hid)
    return x2, saved


def _layer_bwd(dx, dxb, wts, getw, scatter, saved, dims):
    AW, BW, PW, DP = dims["AW"], dims["BW"], dims["PW"], dims["DP"]
    q_start = 2 * AW + 3 * BW
    D = dx.shape[1]
    g_w2 = _mm_tn(saved["hid"], dxb, "mlp_out_wgrad", tm=1024, tn=2048, tk=1024)
    token = scatter("w_mlp_out", g_w2.reshape(N_DEV, -1, D))
    da = _mm_nt(dxb, getw("w_mlp_out", None), "mlp_out_dgrad", out_dtype=MXU_DTYPE, relu2_pre=saved["a"], dep=token,
                tm=1024, tn=1024)
    g_w1 = _mm_tn(saved["h2"], da, "mlp_in_wgrad", groups=N_DEV, tm=2048, tn=1024, tk=1024)
    token = scatter("w_mlp_in", g_w1)
    dh2 = _mm_nt(da, getw("w_mlp_in", None), "mlp_in_dgrad", dep=token, tm=1024, tn=2048, tk=1024)
    dx1, dx1b, g_mlp_norm = _rmsnorm_bwd(dh2, saved["x1"], wts["mlp_norm"], saved["r2"], dx, "rmsnorm_bwd")
    g_wout = _mm_tn(saved["ymix"], dx1b, "proj_out_wgrad", tm=1024, tn=2048, tk=1024)
    token = scatter("w_out", g_wout.reshape(N_DEV, -1, D))
    dymix = _mm_nt(dx1b, getw("w_out", None), "proj_out_dgrad", dep=token, tm=1024, tn=1024)
    p = saved["p"]
    du, dv, g_sgu_w, g_sgu_bmat = _sgu_bwd(dymix, p, wts["sgu_tril"], wts["sgu_tril_t"], wts["sgu_bmat"], "sgu_bwd")
    d_b, d_c, d_xb, g_conv = _conv_bwd(dymix, p, getw("conv_w", None), AW, "conv_bwd")
    dos, corrs = _mix_bwd(dymix, saved["os"], saved["lses"], AW + BW, "mix_bwd")
    dqs, dks, dvs = [], [], []
    g_q = g_k = 0.0
    for g, d in enumerate(DILATIONS):
        dq, dk, dvv, dqg, dkg = _attn_bwd(p, wts["q_gain"], wts["k_gain"], saved["lses"][g], dos[g], corrs[g],
                                          g, d, PW, q_start, "attn_bwd_%d" % d)
        dqs.append(dq.astype(MXU_DTYPE))
        dks.append(dk.astype(MXU_DTYPE))
        dvs.append(dvv.astype(MXU_DTYPE))
        g_q = g_q + dqg.reshape(-1, HEAD_DIM).sum(0)
        g_k = g_k + dkg.reshape(-1, HEAD_DIM).sum(0)
    dp = jnp.concatenate([du, dv, d_b, d_c, d_xb] + dqs + dks + dvs, axis=1)
    g_win = _mm_tn(saved["h"], dp, "proj_in_wgrad", tm=2048, tn=1408, tk=1024)
    token = scatter("w_in", g_win.reshape(D, N_DEV, DP // N_DEV).transpose(1, 0, 2))
    dh = _mm_nt(dp, getw("w_in", None), "proj_in_dgrad", dep=token, tm=1024, tn=2048, tk=1408)
    dx0, dx0b, g_attn_norm = _rmsnorm_bwd(dh, saved["x"], wts["attn_norm"], saved["r1"], dx1, "rmsnorm_bwd")
    H = AW // HEAD_DIM
    tril = jnp.tril(jnp.ones((CHUNK, CHUNK), F32))
    small = [g_attn_norm.reshape(-1), g_sgu_w * tril, g_sgu_bmat.reshape(CHUNK, H, HEAD_DIM).sum(-1).T,
             g_conv, g_q, g_k, g_mlp_norm.reshape(-1)]
    return dx0, dx0b, small


def kernel(x, attn_norm, w_in, sgu_w, sgu_b, conv_w, q_norm, k_norm, w_out, mlp_norm, w_mlp_in, w_mlp_out, loss_target, m_attn_norm, m_w_in, m_sgu_w, m_sgu_b, m_conv_w, m_q_norm, m_k_norm, m_w_out, m_mlp_norm, m_w_mlp_in, m_w_mlp_out, v_attn_norm, v_w_in, v_sgu_w, v_sgu_b, v_conv_w, v_q_norm, v_k_norm, v_w_out, v_mlp_norm, v_w_mlp_in, v_w_mlp_out):
    n_layers = attn_norm.shape[0]
    T, D = x.shape[1], x.shape[2]
    H = sgu_w.shape[1]
    AW = H * HEAD_DIM
    BW = conv_w.shape[2] * N_DEV
    DP = w_in.shape[2] * N_DEV
    DMIX = w_out.shape[1] * N_DEV
    DFF = w_mlp_in.shape[2] * N_DEV
    PW = (DMIX - AW - BW) // 3
    HP = PW // HEAD_DIM
    dims = dict(AW=AW, BW=BW, PW=PW, DP=DP)
    me = 4 * lax.axis_index("x") + 2 * lax.axis_index("y") + lax.axis_index("c")

    big_names = ("w_in", "w_out", "w_mlp_in", "w_mlp_out")
    big_w = dict(zip(big_names, (w_in, w_out, w_mlp_in, w_mlp_out)))
    big_m = dict(zip(big_names, (m_w_in, m_w_out, m_w_mlp_in, m_w_mlp_out)))
    big_v = dict(zip(big_names, (v_w_in, v_w_out, v_w_mlp_in, v_w_mlp_out)))

    keys = []
    for l in range(n_layers):
        keys += [(l, nm) for nm in big_names]
    keys.insert(1, (0, "conv_w"))
    srcs = [_pack([conv_w]) if nm == "conv_w" else big_w[nm][l].astype(MXU_DTYPE) for l, nm in keys]
    flights, gather_token = _exchange_start(srcs, [_own_in_place(s, me) for s in srcs], "gather", name="gather_start")
    arriving = dict(zip(keys, flights))
    forwarding = {}
    relayout = dict(
        w_in=lambda g: g.transpose(1, 0, 2).reshape(D, DP), w_out=lambda g: g.reshape(DMIX, D),
        w_mlp_in=lambda g: g, w_mlp_out=lambda g: g.reshape(DFF, D),
        conv_w=lambda g: jnp.stack([_unpack(g[j], [conv_w.shape])[0] for j in range(N_DEV)], axis=2).reshape(
            n_layers, CONV_WIDTH, BW))
    gathered = {}

    def forward(key, after):
        _, land = _exchange_wait(arriving[key], after, "gather", name="gather_arrive_%d_%s" % key)
        fl, token = _exchange_start(None, [land], "forward", name="gather_forward_%d_%s" % key)
        forwarding[key] = fl[0]
        return token

    def weight_getter(l):
        def getw(nm, after):
            key = (0, nm) if nm == "conv_w" else (l, nm)
            if key not in gathered:
                ahead = keys[keys.index(key):][:2]
                for k in ahead:
                    if k not in forwarding:
                        after = forward(k, after)
                _, land = _exchange_wait(forwarding[key], after, "forward", name="gather_wait_%d_%s" % key)
                gathered[key] = relayout[nm](land)
            return gathered[key][l] if nm == "conv_w" else gathered[key]
        return getw

    tril = jnp.tril(jnp.ones((CHUNK, CHUNK), F32))
    layers = []
    for l in range(n_layers):
        w_tril = sgu_w[l] * tril
        layers.append(dict(
            attn_norm=attn_norm[l][None], mlp_norm=mlp_norm[l][None],
            sgu_tril=w_tril.astype(MXU_DTYPE), sgu_tril_t=w_tril.transpose(0, 2, 1).astype(MXU_DTYPE),
            sgu_bmat=jnp.repeat(sgu_b[l].T, HEAD_DIM, axis=1),
            q_gain=jnp.tile(q_norm[l], HEADS_PER_LANES)[None], k_gain=jnp.tile(k_norm[l], HEADS_PER_LANES)[None]))

    xs = x[0]
    saved = []
    for l in range(n_layers):
        xs, sv = _layer_fwd(xs, layers[l], weight_getter(l), dims, dep=gather_token if l == 0 else None)
        saved.append(sv)
    loss_blk, dx, dxb = _loss_and_grad(xs, loss_target[0], "loss")
    loss = lax.psum(loss_blk[0, 0], ("x", "y", "c"))

    scattering = {}

    def scatter_starter(l):
        def scatter(nm, partials):
            land = lax.empty((N_PEER,) + partials.shape[1:], partials.dtype)
            fl, tok = _exchange_start([partials], [land], "scatter", name="scatter_start_%d_%s" % (l, nm))
            scattering[(l, nm)] = fl[0]
            return tok
        return scatter

    small = [None] * n_layers
    for l in reversed(range(n_layers)):
        dx, dxb, small[l] = _layer_bwd(dx, dxb, layers[l], weight_getter(l), scatter_starter(l), saved[l], dims)

    small_shapes = [s.shape for s in small[0]]
    small_src = [_pack([s for l in range(n_layers) for s in small[l]])]
    small_flights, small_token = _exchange_start(small_src, [_own_in_place(s, me) for s in small_src], "gather_all",
                                                 name="small_start")
    grad_x = dx[None]

    me1 = me.astype(jnp.int32).reshape(1)
    res = {nm: None for nm in big_names}
    after = small_token
    for l in reversed(range(n_layers)):
        for nm in reversed(big_names):
            own, landed = _exchange_wait(scattering[(l, nm)], after, "scatter", name="scatter_wait_%d_%s" % (l, nm))
            res[nm] = _adamw_layer(l, big_w[nm], big_m[nm], big_v[nm], own, landed, me1, res[nm], "adamw_" + nm)
            after = res[nm][0]
    big_out = [res[nm] for nm in big_names]

    _, gathered_small = _exchange_wait(small_flights[0], after, "gather_all", name="small_wait")
    summed = _unpack(_sum_parts(gathered_small, "sum_small"), small_shapes * n_layers)
    ns = len(small_shapes)
    g_small = [jnp.stack([summed[l * ns + i] for l in range(n_layers)]) for i in range(ns)]
    g_attn_norm, g_sgu_w, g_sgu_b, g_conv_full, g_q, g_k, g_mlp_norm = g_small
    cs = conv_w.shape[2]
    g_conv = lax.dynamic_slice_in_dim(g_conv_full, me * cs, cs, axis=2)
    sm_w = (attn_norm, sgu_w, sgu_b, conv_w, q_norm, k_norm, mlp_norm)
    sm_m = (m_attn_norm, m_sgu_w, m_sgu_b, m_conv_w, m_q_norm, m_k_norm, m_mlp_norm)
    sm_v = (v_attn_norm, v_sgu_w, v_sgu_b, v_conv_w, v_q_norm, v_k_norm, v_mlp_norm)
    sm_g = (g_attn_norm, g_sgu_w, g_sgu_b, g_conv, g_q, g_k, g_mlp_norm)
    sm_res = _adamw_flat(_pack(sm_w), _pack(sm_g), _pack(sm_m), _pack(sm_v), "adamw_small")
    shapes = [w.shape for w in sm_w]
    sm_delta, sm_m2, sm_v2 = (_unpack(r, shapes) for r in sm_res)

    def ordered(small_list, big_kind):
        b = [big_out[i][big_kind] for i in range(4)]
        return [small_list[0], b[0], small_list[1], small_list[2], small_list[3], small_list[4], small_list[5],
                b[1], small_list[6], b[2], b[3]]

    return (loss, grad_x, *ordered(list(sm_g), 0), *ordered(sm_delta, 1), *ordered(sm_m2, 2), *ordered(sm_v2, 3))
```

```python
import jax
import jax.numpy as jnp
from jax import lax
from jax.experimental import pallas as pl
from jax.experimental.pallas import tpu as pltpu

N_DEV = 8
HEAD_DIM = 64
CHUNK = 128
ATT_BLK = 128
DILATIONS = (1, 4, 16)
CONV_WIDTH = 3
EPS = 1e-6
ADAM_LR = 0.001
ADAM_B1 = 0.9
ADAM_B2 = 0.999
ADAM_EPS = 1e-08
ADAM_WD = 0.01
ADAM_STEP = 10
MXU_DTYPE = jnp.bfloat16
F32 = jnp.float32
LANE = 128
VMEM_LIMIT_BYTES = 56 * 1024 * 1024
NEG_INF = float("-inf")


def _cparams(*sem):
    return pltpu.CompilerParams(dimension_semantics=sem, vmem_limit_bytes=VMEM_LIMIT_BYTES)


def _sds(shape, dtype):
    return jax.ShapeDtypeStruct(shape, dtype)


def _fit(n, tile):
    for t in range(min(tile, n) // LANE * LANE, 0, -LANE):
        if n % t == 0:
            return t
    return n


_HBM = pl.BlockSpec(memory_space=pltpu.HBM)
_SEM = pl.BlockSpec(memory_space=pltpu.SEMAPHORE)
_DATAFLOW = pltpu.SideEffectType.DATAFLOW_SIDE_EFFECTING
N_PEER = N_DEV - 1


def _mesh_pos():
    x, y, c = lax.axis_index("x"), lax.axis_index("y"), lax.axis_index("c")
    return x, y, c, 4 * x + 2 * y + c


OTHER_CHIPS = (4, 2, 6)
EXCHANGE_PEERS = dict(
    scatter=tuple(range(1, N_DEV)),
    gather_all=tuple(range(1, N_DEV)),
    gather=(1,) + OTHER_CHIPS,
    forward=OTHER_CHIPS)


def _remote_copies(src, land, send_sems, recv_sems, mode):
    x, y, c, me = _mesh_pos()
    copies = []
    for i, k in enumerate(EXCHANGE_PEERS[mode]):
        px = (1 - x) if (k & 4) else x
        py = (1 - y) if (k & 2) else y
        pc = (1 - c) if (k & 1) else c
        if mode == "scatter":
            src_ref, dst_ref, dev = src.at[4 * px + 2 * py + pc], land.at[i], (px, py, pc)
        elif mode == "forward":
            slot = 4 * px + 2 * py + c
            src_ref, dst_ref, dev = land.at[slot], land.at[slot], (x, y, 1 - c)
        else:
            src_ref, dst_ref, dev = src, land.at[me], (px, py, pc)
        copies.append(pltpu.make_async_remote_copy(
            src_ref=src_ref, dst_ref=dst_ref, send_sem=send_sems.at[i], recv_sem=recv_sems.at[i],
            device_id=dev, device_id_type=pl.DeviceIdType.MESH))
    return copies


def _own_in_place(src, me):
    land = lax.empty((N_DEV,) + src.shape, src.dtype)
    return lax.dynamic_update_slice(land, src[None], (me,) + (0,) * src.ndim)


def _exchange_start(srcs, lands, mode, name):
    n = len(lands)
    has_src = srcs is not None
    arrays = (list(srcs) if has_src else []) + list(lands)
    n_arr = len(arrays)
    n_copies = len(EXCHANGE_PEERS[mode])

    def body(*refs):
        src = refs[:n] if has_src else [None] * n
        land = refs[n_arr - n:n_arr]
        send, recv = refs[n_arr:n_arr + n], refs[n_arr + n:n_arr + 2 * n]
        token = refs[2 * n_arr + 2 * n]
        for t in range(n):
            for cp in _remote_copies(src[t], land[t], send[t], recv[t], mode):
                cp.start()
        token[...] = jnp.zeros_like(token)

    outs = pl.pallas_call(
        body, name=name,
        out_shape=([pltpu.SemaphoreType.DMA((n_copies,))] * (2 * n) + [pltpu.HBM(a.shape, a.dtype) for a in arrays]
                   + [_sds((8, LANE), F32)]),
        in_specs=[_HBM] * n_arr,
        out_specs=[_SEM] * (2 * n) + [_HBM] * n_arr + [pl.BlockSpec(memory_space=pltpu.VMEM)],
        input_output_aliases={i: 2 * n + i for i in range(n_arr)},
        compiler_params=pltpu.CompilerParams(has_side_effects=_DATAFLOW),
    )(*[pltpu.with_memory_space_constraint(a, pltpu.HBM) for a in arrays])
    thru = outs[2 * n:2 * n + n_arr]
    flights = [(outs[t], outs[n + t], thru[t] if has_src else None, thru[n_arr - n + t]) for t in range(n)]
    return flights, outs[2 * n + n_arr]


def _exchange_wait(flight, after, mode, name):
    send, recv, src, land = flight
    arrays = [land] if src is None else [src, land]
    n_arr = len(arrays)

    def body(*refs):
        src_ref = refs[0] if n_arr == 2 else None
        land_ref, send_ref, recv_ref = refs[n_arr - 1], refs[n_arr], refs[n_arr + 1]
        for cp in _remote_copies(src_ref, land_ref, send_ref, recv_ref, mode):
            cp.wait_send()
            cp.wait_recv()

    outs = pl.pallas_call(
        body, name=name, out_shape=[pltpu.HBM(a.shape, a.dtype) for a in arrays],
        in_specs=[_HBM] * n_arr + [_SEM, _SEM, pl.BlockSpec(memory_space=pl.ANY)], out_specs=[_HBM] * n_arr,
        input_output_aliases={i: i for i in range(n_arr)},
        compiler_params=pltpu.CompilerParams(has_side_effects=_DATAFLOW),
    )(*arrays, send, recv, after)
    return (None, outs[0]) if src is None else (outs[0], outs[1])


def _rmsnorm_fwd(x, g, name, dep=None, tr=512):
    T, D = x.shape

    def body(x_ref, g_ref, *rest):
        h_ref, r_ref = rest[-2:]
        xv = x_ref[...]
        r = lax.rsqrt(jnp.mean(xv * xv, axis=-1, keepdims=True) + EPS)
        h_ref[...] = (xv * r * g_ref[...]).astype(h_ref.dtype)
        r_ref[...] = r

    in_specs = [pl.BlockSpec((tr, D), lambda i: (i, 0)), pl.BlockSpec((1, D), lambda i: (0, 0))]
    args = [x, g]
    if dep is not None:
        in_specs.append(pl.BlockSpec(dep.shape, lambda i: (0, 0)))
        args.append(dep)
    return pl.pallas_call(
        body, name=name, grid=(T // tr,),
        in_specs=in_specs,
        out_specs=[pl.BlockSpec((tr, D), lambda i: (i, 0)), pl.BlockSpec((tr, 1), lambda i: (i, 0))],
        out_shape=[_sds((T, D), MXU_DTYPE), _sds((T, 1), F32)],
        compiler_params=_cparams("parallel"),
    )(*args)


def _rmsnorm_bwd(dh, x, g, r, dres, name, tr=256):
    T, D = x.shape

    def body(dh_ref, x_ref, g_ref, r_ref, dres_ref, dx_ref, dxb_ref, dg_ref):
        @pl.when(pl.program_id(0) == 0)
        def _():
            dg_ref[...] = jnp.zeros_like(dg_ref)

        dh_v, xv, rv = dh_ref[...], x_ref[...], r_ref[...]
        gdy = dh_v * g_ref[...]
        mean_xg = jnp.mean(xv * gdy, axis=-1, keepdims=True)
        dx = dres_ref[...] + rv * gdy - xv * (rv * rv * rv) * mean_xg
        dx_ref[...] = dx
        dxb_ref[...] = dx.astype(dxb_ref.dtype)
        dg_ref[...] += jnp.sum(dh_v * xv * rv, axis=0, keepdims=True)

    row = lambda i: (i, 0)
    return pl.pallas_call(
        body, name=name, grid=(T // tr,),
        in_specs=[pl.BlockSpec((tr, D), row), pl.BlockSpec((tr, D), row), pl.BlockSpec((1, D), lambda i: (0, 0)),
                  pl.BlockSpec((tr, 1), row), pl.BlockSpec((tr, D), row)],
        out_specs=[pl.BlockSpec((tr, D), row), pl.BlockSpec((tr, D), row), pl.BlockSpec((1, D), lambda i: (0, 0))],
        out_shape=[_sds((T, D), F32), _sds((T, D), MXU_DTYPE), _sds((1, D), F32)],
        compiler_params=_cparams("arbitrary"),
    )(dh, x, g, r, dres)


def _mm_nn(a, b, name, out_dtype=F32, residual=None, relu2=False, a_single=False, tm=512, tn=512, tk=None):
    M, K = a.shape
    grouped = b.ndim == 3
    N = b.shape[0] * b.shape[2] if grouped else b.shape[1]
    tm, tn = _fit(M, tm), _fit(b.shape[2] if grouped else N, tn)
    tk = K if tk is None else _fit(K, tk)
    nk = K // tk
    if grouped:
        per = b.shape[2] // tn
        b_spec = pl.BlockSpec((None, tk, tn), lambda i, j, k: (j // per, k, j % per))
    else:
        b_spec = pl.BlockSpec((tk, tn), lambda i, j, k: (k, j))
    n_out = 2 if relu2 else 1

    def body(*refs):
        a_ref, b_ref = refs[0], refs[1]
        r_ref = refs[2] if residual is not None else None
        o = 3 if residual is not None else 2
        outs = refs[o:o + n_out]
        acc_ref = refs[o + n_out] if nk > 1 else None

        def finish(acc):
            if r_ref is not None:
                acc = acc + r_ref[...]
            outs[0][...] = acc.astype(outs[0].dtype)
            if relu2:
                rl = jnp.maximum(acc, 0.0)
                outs[1][...] = (rl * rl).astype(outs[1].dtype)

        prod = jnp.dot(a_ref[...], b_ref[...], preferred_element_type=F32)
        if nk == 1:
            finish(prod)
        else:
            k = pl.program_id(2)

            @pl.when(k == 0)
            def _():
                acc_ref[...] = prod

            @pl.when(k > 0)
            def _():
                acc_ref[...] += prod

            @pl.when(k == nk - 1)
            def _():
                finish(acc_ref[...])

    out_blk = pl.BlockSpec((tm, tn), lambda i, j, k: (i, j))
    in_specs = [pl.BlockSpec((tm, tk), lambda i, j, k: (i, k), pipeline_mode=pl.Buffered(1 if a_single else 2)), b_spec]
    args = [a, b]
    if residual is not None:
        in_specs.append(out_blk)
        args.append(residual)
    out_shape = [_sds((M, N), out_dtype)]
    if relu2:
        out_shape.append(_sds((M, N), MXU_DTYPE))
    outs = pl.pallas_call(
        body, name=name, grid=(M // tm, N // tn, nk),
        in_specs=in_specs, out_specs=[out_blk] * n_out, out_shape=out_shape,
        scratch_shapes=[pltpu.VMEM((tm, tn), F32)] if nk > 1 else [],
        compiler_params=_cparams("parallel", "parallel", "arbitrary"),
    )(*args)
    return outs if relu2 else outs[0]


def _mm_nt(a, b, name, out_dtype=F32, relu2_pre=None, dep=None, a_single=False, tm=512, tn=512, tk=None):
    M, K = a.shape
    grouped = b.ndim == 3
    N = b.shape[1] if grouped else b.shape[0]
    tm, tn = _fit(M, tm), _fit(N, tn)
    tk = K if tk is None else _fit(b.shape[2] if grouped else K, tk)
    nk = K // tk
    if grouped:
        per = b.shape[2] // tk
        b_spec = pl.BlockSpec((None, tn, tk), lambda i, j, k: (k // per, j, k % per))
    else:
        b_spec = pl.BlockSpec((tn, tk), lambda i, j, k: (j, k))

    def body(*refs):
        a_ref, b_ref = refs[0], refs[1]
        p_ref = refs[2] if relu2_pre is not None else None
        o = 2 + (relu2_pre is not None) + (dep is not None)
        out_ref = refs[o]
        acc_ref = refs[o + 1] if nk > 1 else None

        def finish(acc):
            if p_ref is not None:
                acc = acc * (2.0 * jnp.maximum(p_ref[...], 0.0))
            out_ref[...] = acc.astype(out_ref.dtype)

        prod = lax.dot_general(a_ref[...], b_ref[...], (((1,), (1,)), ((), ())), preferred_element_type=F32)
        if nk == 1:
            finish(prod)
        else:
            k = pl.program_id(2)

            @pl.when(k == 0)
            def _():
                acc_ref[...] = prod

            @pl.when(k > 0)
            def _():
                acc_ref[...] += prod

            @pl.when(k == nk - 1)
            def _():
                finish(acc_ref[...])

    out_blk = pl.BlockSpec((tm, tn), lambda i, j, k: (i, j))
    in_specs = [pl.BlockSpec((tm, tk), lambda i, j, k: (i, k), pipeline_mode=pl.Buffered(1 if a_single else 2)), b_spec]
    args = [a, b]
    if relu2_pre is not None:
        in_specs.append(out_blk)
        args.append(relu2_pre)
    if dep is not None:
        in_specs.append(pl.BlockSpec(dep.shape, lambda i, j, k: (0, 0)))
        args.append(dep)
    return pl.pallas_call(
        body, name=name, grid=(M // tm, N // tn, nk),
        in_specs=in_specs, out_specs=out_blk, out_shape=_sds((M, N), out_dtype),
        scratch_shapes=[pltpu.VMEM((tm, tn), F32)] if nk > 1 else [],
        compiler_params=_cparams("parallel", "parallel", "arbitrary"),
    )(*args)


def _wgrad_wide_a(a, b, name, tm=512):
    T, M = a.shape
    N = b.shape[1]
    tm = _fit(M, tm)

    def body(a_ref, b_ref, out_ref):
        out_ref[...] = lax.dot_general(a_ref[...], b_ref[...], (((0,), (0,)), ((), ())),
                                       preferred_element_type=F32).astype(out_ref.dtype)

    return pl.pallas_call(
        body, name=name, grid=(M // tm,),
        in_specs=[pl.BlockSpec((T, tm), lambda i: (0, i)),
                  pl.BlockSpec((T, N), lambda i: (0, 0), pipeline_mode=pl.Buffered(1))],
        out_specs=pl.BlockSpec((tm, N), lambda i: (i, 0)), out_shape=_sds((M, N), MXU_DTYPE),
        compiler_params=_cparams("parallel"),
    )(a, b)


def _wgrad_wide_b(a, b, name, groups=None, tn=512, t_chunk=512):
    T, M = a.shape
    N = b.shape[1]
    tn = _fit(N if groups is None else N // groups, tn)
    t_chunk = _fit(T, t_chunk)

    def body(a_ref, b_ref, out_ref, at_ref):
        @pl.when(pl.program_id(0) == 0)
        def _():
            for c in range(0, T, t_chunk):
                at_ref[:, c:c + t_chunk] = a_ref[c:c + t_chunk, :].T

        out_ref[...] = jnp.dot(at_ref[...], b_ref[...], preferred_element_type=F32).astype(out_ref.dtype)

    if groups is None:
        out_spec = pl.BlockSpec((M, tn), lambda j: (0, j))
        out_shape = _sds((M, N), MXU_DTYPE)
    else:
        per = N // groups // tn
        out_spec = pl.BlockSpec((None, M, tn), lambda j: (j // per, 0, j % per))
        out_shape = _sds((groups, M, N // groups), MXU_DTYPE)
    return pl.pallas_call(
        body, name=name, grid=(N // tn,),
        in_specs=[pl.BlockSpec((T, M), lambda j: (0, 0), pipeline_mode=pl.Buffered(1)),
                  pl.BlockSpec((T, tn), lambda j: (0, j))],
        out_specs=out_spec, out_shape=out_shape,
        scratch_shapes=[pltpu.VMEM((M, T), MXU_DTYPE)],
        compiler_params=_cparams("arbitrary"),
    )(a, b)


def _loss_and_grad(y, target, name, tr=512):
    T, D = y.shape

    def body(y_ref, t_ref, loss_ref, dx_ref, dxb_ref):
        @pl.when(pl.program_id(0) == 0)
        def _():
            loss_ref[...] = jnp.zeros_like(loss_ref)

        err = y_ref[...] - t_ref[...]
        loss_ref[...] += 0.5 * jnp.sum(jnp.mean(err * err, axis=-1, keepdims=True), axis=0, keepdims=True)
        dx = err * (1.0 / D)
        dx_ref[...] = dx
        dxb_ref[...] = dx.astype(dxb_ref.dtype)

    row = lambda i: (i, 0)
    return pl.pallas_call(
        body, name=name, grid=(T // tr,),
        in_specs=[pl.BlockSpec((tr, D), row), pl.BlockSpec((tr, D), row)],
        out_specs=[pl.BlockSpec((8, LANE), lambda i: (0, 0)), pl.BlockSpec((tr, D), row), pl.BlockSpec((tr, D), row)],
        out_shape=[_sds((8, LANE), F32), _sds((T, D), F32), _sds((T, D), MXU_DTYPE)],
        compiler_params=_cparams("arbitrary"),
    )(y, target)


def _sgu_mixed(v, w_ref, b_ref, col_head, n_heads):
    mixed = b_ref[...]
    for h in range(n_heads):
        full = jnp.dot(w_ref[h], v, preferred_element_type=F32)
        mixed = mixed + jnp.where(col_head == h, full, 0.0)
    return mixed


def _sgu_fwd(p, w_tril, bmat, name):
    T = p.shape[0]
    H = w_tril.shape[0]
    AW = H * HEAD_DIM

    def body(u_ref, v_ref, w_ref, b_ref, y_ref):
        col_head = lax.broadcasted_iota(jnp.int32, (CHUNK, AW), 1) // HEAD_DIM
        mixed = _sgu_mixed(v_ref[...].astype(MXU_DTYPE), w_ref, b_ref, col_head, H)
        y_ref[...] = (u_ref[...] * mixed).astype(y_ref.dtype)

    const3 = lambda c: (0, 0, 0)
    return pl.pallas_call(
        body, name=name, grid=(T // CHUNK,),
        in_specs=[pl.BlockSpec((CHUNK, AW), lambda c: (c, 0)), pl.BlockSpec((CHUNK, AW), lambda c: (c, 1)),
                  pl.BlockSpec((H, CHUNK, CHUNK), const3), pl.BlockSpec((CHUNK, AW), lambda c: (0, 0))],
        out_specs=pl.BlockSpec((CHUNK, AW), lambda c: (c, 0)),
        out_shape=_sds((T, AW), MXU_DTYPE),
        compiler_params=_cparams("parallel"),
    )(p, p, w_tril, bmat)


def _sgu_bwd(dymix, p, w_tril, w_tril_t, bmat, name):
    T = p.shape[0]
    H = w_tril.shape[0]
    AW = H * HEAD_DIM

    def body(dy_ref, u_ref, v_ref, w_ref, wt_ref, b_ref, du_ref, dv_ref, dw_ref, db_ref):
        @pl.when(pl.program_id(0) == 0)
        def _():
            dw_ref[...] = jnp.zeros_like(dw_ref)
            db_ref[...] = jnp.zeros_like(db_ref)

        col_head = lax.broadcasted_iota(jnp.int32, (CHUNK, AW), 1) // HEAD_DIM
        v = v_ref[...].astype(MXU_DTYPE)
        dy = dy_ref[...]
        mixed = _sgu_mixed(v, w_ref, b_ref, col_head, H)
        du_ref[...] = (dy * mixed).astype(du_ref.dtype)
        dm = dy * u_ref[...]
        db_ref[...] += dm
        dm_c = dm.astype(MXU_DTYPE)
        dv = jnp.zeros((CHUNK, AW), F32)
        for h in range(H):
            sel = col_head == h
            dv = dv + jnp.where(sel, jnp.dot(wt_ref[h], dm_c, preferred_element_type=F32), 0.0)
            dm_h = jnp.where(sel, dm, 0.0).astype(MXU_DTYPE)
            dw_ref[h] += lax.dot_general(dm_h, v, (((1,), (1,)), ((), ())), preferred_element_type=F32)
        dv_ref[...] = dv.astype(dv_ref.dtype)

    const3 = lambda c: (0, 0, 0)
    blk = pl.BlockSpec((CHUNK, AW), lambda c: (c, 0))
    return pl.pallas_call(
        body, name=name, grid=(T // CHUNK,),
        in_specs=[blk, blk, pl.BlockSpec((CHUNK, AW), lambda c: (c, 1)),
                  pl.BlockSpec((H, CHUNK, CHUNK), const3), pl.BlockSpec((H, CHUNK, CHUNK), const3),
                  pl.BlockSpec((CHUNK, AW), lambda c: (0, 0))],
        out_specs=[blk, blk, pl.BlockSpec((H, CHUNK, CHUNK), const3), pl.BlockSpec((CHUNK, AW), lambda c: (0, 0))],
        out_shape=[_sds((T, AW), MXU_DTYPE), _sds((T, AW), MXU_DTYPE), _sds((H, CHUNK, CHUNK), F32), _sds((CHUNK, AW), F32)],
        compiler_params=_cparams("arbitrary"),
    )(dymix, p, p, w_tril, w_tril_t, bmat)


def _shift_down(z, s, row):
    return jnp.where(row >= s, pltpu.roll(z, s, 0), 0.0)


def _shift_up(z, s, row, T):
    return jnp.where(row < T - s, pltpu.roll(z, T - s, 0), 0.0)


def _conv_fwd(p, w_conv, AW, name):
    T = p.shape[0]
    BW = w_conv.shape[1]
    nb = BW // LANE
    b0 = 2 * AW // LANE

    def body(b_ref, c_ref, x_ref, w_ref, y_ref):
        row = lax.broadcasted_iota(jnp.int32, (T, LANE), 0)
        z = c_ref[...] * x_ref[...]
        w0, w1, w2 = w_ref[0:1, :], w_ref[1:2, :], w_ref[2:3, :]
        conv = w2 * z + w1 * _shift_down(z, 1, row) + w0 * _shift_down(z, 2, row)
        y_ref[...] = (b_ref[...] * conv).astype(y_ref.dtype)

    return pl.pallas_call(
        body, name=name, grid=(nb,),
        in_specs=[pl.BlockSpec((T, LANE), lambda j: (0, b0 + j)), pl.BlockSpec((T, LANE), lambda j: (0, b0 + nb + j)),
                  pl.BlockSpec((T, LANE), lambda j: (0, b0 + 2 * nb + j)), pl.BlockSpec((CONV_WIDTH, LANE), lambda j: (0, j))],
        out_specs=pl.BlockSpec((T, LANE), lambda j: (0, j)),
        out_shape=_sds((T, BW), MXU_DTYPE),
        compiler_params=_cparams("parallel"),
    )(p, p, p, w_conv)


def _conv_bwd(dymix, p, w_conv, AW, name):
    T = p.shape[0]
    BW = w_conv.shape[1]
    nb = BW // LANE
    b0 = 2 * AW // LANE
    y0 = AW // LANE

    def body(dy_ref, b_ref, c_ref, x_ref, w_ref, db_ref, dc_ref, dxb_ref, dw_ref):
        row = lax.broadcasted_iota(jnp.int32, (T, LANE), 0)
        cv, xv, dy = c_ref[...], x_ref[...], dy_ref[...]
        w0, w1, w2 = w_ref[0:1, :], w_ref[1:2, :], w_ref[2:3, :]
        z = cv * xv
        z1 = _shift_down(z, 1, row)
        z2 = _shift_down(z, 2, row)
        conv = w2 * z + w1 * z1 + w0 * z2
        db_ref[...] = (dy * conv).astype(db_ref.dtype)
        dconv = dy * b_ref[...]
        dz = w2 * dconv + w1 * _shift_up(dconv, 1, row, T) + w0 * _shift_up(dconv, 2, row, T)
        dc_ref[...] = (dz * xv).astype(dc_ref.dtype)
        dxb_ref[...] = (dz * cv).astype(dxb_ref.dtype)
        dw_ref[0:1, :] = jnp.sum(dconv * z2, axis=0, keepdims=True)
        dw_ref[1:2, :] = jnp.sum(dconv * z1, axis=0, keepdims=True)
        dw_ref[2:3, :] = jnp.sum(dconv * z, axis=0, keepdims=True)

    col = lambda j: (0, j)
    return pl.pallas_call(
        body, name=name, grid=(nb,),
        in_specs=[pl.BlockSpec((T, LANE), lambda j: (0, y0 + j)),
                  pl.BlockSpec((T, LANE), lambda j: (0, b0 + j)), pl.BlockSpec((T, LANE), lambda j: (0, b0 + nb + j)),
                  pl.BlockSpec((T, LANE), lambda j: (0, b0 + 2 * nb + j)), pl.BlockSpec((CONV_WIDTH, LANE), col)],
        out_specs=[pl.BlockSpec((T, LANE), col)] * 3 + [pl.BlockSpec((CONV_WIDTH, LANE), col)],
        out_shape=[_sds((T, BW), MXU_DTYPE)] * 3 + [_sds((CONV_WIDTH, BW), F32)],
        compiler_params=_cparams("parallel"),
    )(dymix, p, p, p, w_conv)


def _head_sum(x, col_head, n_heads):
    out = jnp.zeros_like(x)
    for h in range(n_heads):
        sel = col_head == h
        out = jnp.where(sel, jnp.sum(jnp.where(sel, x, 0.0), axis=-1, keepdims=True), out)
    return out


def _same_head(width):
    assert width == 2 * HEAD_DIM
    return lax.broadcasted_iota(jnp.int32, (1, width), 1) < HEAD_DIM


def _head_sum2(x, first):
    s0 = jnp.sum(jnp.where(first, x, 0.0), axis=-1, keepdims=True)
    s1 = jnp.sum(jnp.where(first, 0.0, x), axis=-1, keepdims=True)
    return jnp.where(first, s0, s1)


def _head_norm(x, g, first):
    r = lax.rsqrt(_head_sum2(x * x, first) * (1.0 / HEAD_DIM) + EPS)
    return x * r * g, r


def _head_norm_bwd(dy, x, g, r, first):
    gdy = dy * g
    mean_xg = _head_sum2(x * gdy, first) * (1.0 / HEAD_DIM)
    return r * gdy - x * (r * r * r) * mean_xg, dy * x * r


ATT_SPAN_MIN = 512
ATT_FWD_UNROLL = 4
ATT_BWD_UNROLL = 2
HEADS_PER_LANES = LANE // HEAD_DIM


def _attn_geometry(T, d):
    m = max(1, ATT_SPAN_MIN // (ATT_BLK * d))
    return m, ATT_BLK * d * m, ATT_BLK * d, T // (ATT_BLK * d)


def _rows(ref, start, d):
    return ref[pl.ds(start, ATT_BLK, stride=d), :] if d > 1 else ref[pl.ds(start, ATT_BLK), :]


def _set_rows(ref, start, d, value):
    if d > 1:
        ref[pl.ds(start, ATT_BLK, stride=d), :] = value
    else:
        ref[pl.ds(start, ATT_BLK), :] = value


def _for_each_block(d, m, task, unroll):
    for j in range(m):
        if d == 1:
            task(0, j)
        else:
            lax.fori_loop(0, d, lambda r, carry, j=j: (task(r, j), carry)[1], 0, unroll=min(unroll, d))


def _head_slices():
    return [slice(h * HEAD_DIM, (h + 1) * HEAD_DIM) for h in range(HEADS_PER_LANES)]


def _attn_fwd(p, qg, kg, g, d, DP, PW, q_start, name):
    T = p.shape[0]
    B, W = ATT_BLK, LANE
    m, span, group, _ = _attn_geometry(T, d)
    c_q = (q_start + g * PW) // W
    c_k, c_v = c_q + 3 * PW // W, c_q + 6 * PW // W
    scale = HEAD_DIM ** -0.5

    def body(q_ref, k_ref, v_ref, kp_ref, vp_ref, qg_ref, kg_ref, o_ref, lse_ref):
        n = pl.program_id(1)
        same = _same_head(W)
        qi = lax.broadcasted_iota(jnp.int32, (B, 2 * B), 0)
        kj = lax.broadcasted_iota(jnp.int32, (B, 2 * B), 1)
        band = (kj >= qi) & (kj <= qi + B)
        qgv, kgv = qg_ref[...], kg_ref[...]

        def task(r, j):
            cur = j * group + r
            if j == 0:
                kp, vp = _rows(kp_ref, r, d), _rows(vp_ref, r, d)
            else:
                kp, vp = _rows(k_ref, cur - group, d), _rows(v_ref, cur - group, d)
            mask = band & ((n * m + j > 0) | (kj >= B))
            qn, _ = _head_norm(_rows(q_ref, cur, d), qgv, same)
            kn, _ = _head_norm(jnp.concatenate([kp, _rows(k_ref, cur, d)], axis=0), kgv, same)
            qn, kn = qn.astype(MXU_DTYPE), kn.astype(MXU_DTYPE)
            vcat = jnp.concatenate([vp, _rows(v_ref, cur, d)], axis=0).astype(MXU_DTYPE)
            o_parts, lse_parts = [], []
            for sl in _head_slices():
                s = lax.dot_general(qn[:, sl], kn[:, sl], (((1,), (1,)), ((), ())), preferred_element_type=F32) * scale
                s = jnp.where(mask, s, NEG_INF)
                mx = jnp.max(s, axis=-1, keepdims=True)
                e = jnp.exp(s - mx)
                den = jnp.sum(e, axis=-1, keepdims=True)
                o_parts.append(jnp.dot(e.astype(MXU_DTYPE), vcat[:, sl], preferred_element_type=F32) / den)
                lse_parts.append(jnp.broadcast_to(mx + jnp.log(den), (B, HEAD_DIM)))
            _set_rows(o_ref, cur, d, jnp.concatenate(o_parts, axis=1))
            _set_rows(lse_ref, cur, d, jnp.concatenate(lse_parts, axis=1))

        _for_each_block(d, m, task, ATT_FWD_UNROLL)

    main = lambda c0: pl.BlockSpec((span, W), lambda hp, n: (n, c0 + hp))
    prev = lambda c0: pl.BlockSpec((group, W), lambda hp, n: (jnp.maximum(n * m - 1, 0), c0 + hp))
    gain = pl.BlockSpec((1, W), lambda hp, n: (0, 0))
    out_blk = pl.BlockSpec((span, W), lambda hp, n: (n, hp))
    return pl.pallas_call(
        body, name=name, grid=(PW // W, T // span),
        in_specs=[main(c_q), main(c_k), main(c_v), prev(c_k), prev(c_v), gain, gain],
        out_specs=[out_blk, out_blk],
        out_shape=[_sds((T, PW), F32), _sds((T, PW), F32)],
        compiler_params=_cparams("parallel", "parallel"),
    )(p, p, p, p, p, qg, kg)


def _attn_bwd(p, qg, kg, lse, do, corr, g, d, PW, q_start, name):
    T = p.shape[0]
    B, W = ATT_BLK, LANE
    m, span, group, n_blocks = _attn_geometry(T, d)
    c_q = (q_start + g * PW) // W
    c_k, c_v = c_q + 3 * PW // W, c_q + 6 * PW // W
    scale = HEAD_DIM ** -0.5
    nt = (((1,), (1,)), ((), ()))
    tn = (((0,), (0,)), ((), ()))

    def body(q_ref, k_ref, v_ref, do_ref, l_ref, c_ref, kp_ref, vp_ref, qx_ref, dox_ref, lx_ref, cx_ref, qg_ref, kg_ref,
             dq_ref, dk_ref, dv_ref, dqg_ref, dkg_ref):
        hp, n = pl.program_id(0), pl.program_id(1)

        @pl.when((hp == 0) & (n == 0))
        def _():
            dqg_ref[...] = jnp.zeros_like(dqg_ref)
            dkg_ref[...] = jnp.zeros_like(dkg_ref)

        same = _same_head(W)
        qgv, kgv = qg_ref[...], kg_ref[...]
        i1 = lax.broadcasted_iota(jnp.int32, (B, B), 0)
        j1 = lax.broadcasted_iota(jnp.int32, (B, B), 1)
        i2 = lax.broadcasted_iota(jnp.int32, (2 * B, B), 0)
        j2 = lax.broadcasted_iota(jnp.int32, (2 * B, B), 1)

        def task(r, j):
            cur = j * group + r
            blk = n * m + j
            q_c, k_c, v_c = _rows(q_ref, cur, d), _rows(k_ref, cur, d), _rows(v_ref, cur, d)
            do_c, l_c, c_c = _rows(do_ref, cur, d), _rows(l_ref, cur, d), _rows(c_ref, cur, d)
            if j == 0:
                k_p, v_p = _rows(kp_ref, r, d), _rows(vp_ref, r, d)
            else:
                k_p, v_p = _rows(k_ref, cur - group, d), _rows(v_ref, cur - group, d)
            if j == m - 1:
                nxt = [_rows(ref, r, d) for ref in (qx_ref, dox_ref, lx_ref, cx_ref)]
            else:
                nxt = [_rows(ref, cur + group, d) for ref in (q_ref, do_ref, l_ref, c_ref)]
            q_x, do_x, l_x, c_x = nxt
            qn_c, q_r = _head_norm(q_c, qgv, same)
            kn_c, k_r = _head_norm(k_c, kgv, same)
            kn_p, _ = _head_norm(k_p, kgv, same)
            qn_x, _ = _head_norm(q_x, qgv, same)
            kn_c, kn_p, v_c, v_p = (a.astype(MXU_DTYPE) for a in (kn_c, kn_p, v_c, v_p))
            qn_c = qn_c.astype(MXU_DTYPE)
            qn_cat = jnp.concatenate([qn_c, qn_x.astype(MXU_DTYPE)], axis=0)
            do_cb = do_c.astype(MXU_DTYPE)
            do_cat = jnp.concatenate([do_cb, do_x.astype(MXU_DTYPE)], axis=0)
            l_cat = jnp.concatenate([l_c, l_x], axis=0)
            c_cat = jnp.concatenate([c_c, c_x], axis=0)
            mask_p = (j1 >= i1) & (blk > 0)
            mask_c = ((i2 < B) & (j2 <= i2)) | ((i2 >= B) & (j2 >= i2 - B) & (blk + 1 < n_blocks))
            dqn, dkn, dv = [], [], []
            for h, sl in enumerate(_head_slices()):
                lane = slice(h * HEAD_DIM, h * HEAD_DIM + 1)
                s_p = lax.dot_general(qn_c[:, sl], kn_p[:, sl], nt, preferred_element_type=F32) * scale
                pr_p = jnp.where(mask_p, jnp.exp(s_p - l_c[:, lane]), 0.0)
                dp_p = lax.dot_general(do_cb[:, sl], v_p[:, sl], nt, preferred_element_type=F32)
                ds_p = (pr_p * (dp_p + c_c[:, lane]) * scale).astype(MXU_DTYPE)
                s_c = lax.dot_general(qn_cat[:, sl], kn_c[:, sl], nt, preferred_element_type=F32) * scale
                pr_c = jnp.where(mask_c, jnp.exp(s_c - l_cat[:, lane]), 0.0)
                dp_c = lax.dot_general(do_cat[:, sl], v_c[:, sl], nt, preferred_element_type=F32)
                ds_c = (pr_c * (dp_c + c_cat[:, lane]) * scale).astype(MXU_DTYPE)
                dqn.append(jnp.dot(ds_p, kn_p[:, sl], preferred_element_type=F32)
                           + jnp.dot(ds_c[:B], kn_c[:, sl], preferred_element_type=F32))
                dkn.append(lax.dot_general(ds_c, qn_cat[:, sl], tn, preferred_element_type=F32))
                dv.append(lax.dot_general(pr_c.astype(MXU_DTYPE), do_cat[:, sl], tn, preferred_element_type=F32))
            dq, dqg_part = _head_norm_bwd(jnp.concatenate(dqn, axis=1), q_c, qgv, q_r, same)
            dk, dkg_part = _head_norm_bwd(jnp.concatenate(dkn, axis=1), k_c, kgv, k_r, same)
            _set_rows(dq_ref, cur, d, dq)
            _set_rows(dk_ref, cur, d, dk)
            _set_rows(dv_ref, cur, d, jnp.concatenate(dv, axis=1))
            dqg_ref[0:1, :] += jnp.sum(dqg_part, axis=0, keepdims=True)
            dkg_ref[0:1, :] += jnp.sum(dkg_part, axis=0, keepdims=True)

        _for_each_block(d, m, task, ATT_BWD_UNROLL)

    n_spans = T // span
    main = lambda c0: pl.BlockSpec((span, W), lambda hp, n: (n, c0 + hp))
    prev = lambda c0: pl.BlockSpec((group, W), lambda hp, n: (jnp.maximum(n * m - 1, 0), c0 + hp))
    nxt = lambda c0: pl.BlockSpec((group, W), lambda hp, n: (jnp.minimum((n + 1) * m, n_blocks - 1), c0 + hp))
    gain = pl.BlockSpec((1, W), lambda hp, n: (0, 0))
    acc = pl.BlockSpec((8, W), lambda hp, n: (0, 0))
    own = pl.BlockSpec((span, W), lambda hp, n: (n, hp))
    dq, dk, dv, dqg, dkg = pl.pallas_call(
        body, name=name, grid=(PW // W, n_spans),
        in_specs=[main(c_q), main(c_k), main(c_v), main(0), main(0), main(0), prev(c_k), prev(c_v),
                  nxt(c_q), nxt(0), nxt(0), nxt(0), gain, gain],
        out_specs=[own, own, own, acc, acc],
        out_shape=[_sds((T, PW), F32)] * 3 + [_sds((8, W), F32)] * 2,
        compiler_params=_cparams("arbitrary", "arbitrary"),
    )(p, p, p, do, lse, corr, p, p, p, do, lse, corr, qg, kg)
    return dq, dk, dv, dqg[0], dkg[0]


def _softmax3(lses):
    mx = jnp.maximum(jnp.maximum(lses[0], lses[1]), lses[2])
    ex = [jnp.exp(l - mx) for l in lses]
    inv = 1.0 / (ex[0] + ex[1] + ex[2])
    return [e * inv for e in ex]


def _mix_fwd(os_, lses, name, tr=512):
    T, PW = os_[0].shape

    def body(o0, o1, o2, l0, l1, l2, y_ref):
        alpha = _softmax3([l0[...], l1[...], l2[...]])
        for g, o_ref in enumerate((o0, o1, o2)):
            y_ref[:, g * PW:(g + 1) * PW] = (o_ref[...] * alpha[g]).astype(y_ref.dtype)

    blk = pl.BlockSpec((tr, PW), lambda i: (i, 0))
    return pl.pallas_call(
        body, name=name, grid=(T // tr,),
        in_specs=[blk] * 6, out_specs=pl.BlockSpec((tr, 3 * PW), lambda i: (i, 0)),
        out_shape=_sds((T, 3 * PW), MXU_DTYPE),
        compiler_params=_cparams("parallel"),
    )(*os_, *lses)


def _mix_bwd(dymix, os_, lses, c_start, name, tr=512):
    T, PW = os_[0].shape
    HP = PW // HEAD_DIM
    c0 = c_start // PW

    def body(d0, d1, d2, o0, o1, o2, l0, l1, l2, do0, do1, do2, dl0, dl1, dl2):
        col_head = lax.broadcasted_iota(jnp.int32, (tr, PW), 1) // HEAD_DIM
        alpha = _softmax3([l0[...], l1[...], l2[...]])
        dys = [d0[...], d1[...], d2[...]]
        dots = [_head_sum(dy * o_ref[...], col_head, HP) for dy, o_ref in zip(dys, (o0, o1, o2))]
        mean_dot = alpha[0] * dots[0] + alpha[1] * dots[1] + alpha[2] * dots[2]
        for g, (do_ref, dl_ref) in enumerate(((do0, dl0), (do1, dl1), (do2, dl2))):
            do_ref[...] = dys[g] * alpha[g]
            dl_ref[...] = -alpha[g] * mean_dot

    blk = pl.BlockSpec((tr, PW), lambda i: (i, 0))
    dy_specs = [pl.BlockSpec((tr, PW), lambda i, g=g: (i, c0 + g)) for g in range(3)]
    outs = pl.pallas_call(
        body, name=name, grid=(T // tr,),
        in_specs=dy_specs + [blk] * 6, out_specs=[blk] * 6,
        out_shape=[_sds((T, PW), F32)] * 6,
        compiler_params=_cparams("parallel"),
    )(dymix, dymix, dymix, *os_, *lses)
    return outs[:3], outs[3:]


def _adamw_math(w, g, m, v):
    m2 = ADAM_B1 * m + (1.0 - ADAM_B1) * g
    v2 = ADAM_B2 * v + (1.0 - ADAM_B2) * (g * g)
    m_hat = m2 / (1.0 - ADAM_B1 ** ADAM_STEP)
    v_hat = v2 / (1.0 - ADAM_B2 ** ADAM_STEP)
    delta = -ADAM_LR * (m_hat / (jnp.sqrt(v_hat) + ADAM_EPS) + ADAM_WD * w)
    return delta, m2, v2


def _adamw_layer(layer, w, m, v, own, landed, me, prev, name, tr=256):
    _, R, C = w.shape
    tr = min(tr, R)

    def body(me_ref, w_ref, m_ref, v_ref, own_ref, land_ref, *rest):
        g_ref, d_ref, m2_ref, v2_ref = rest[-4:]
        g = own_ref[...].astype(F32)
        for j in range(N_PEER):
            g = g + land_ref[j].astype(F32)
        delta, m2, v2 = _adamw_math(w_ref[...], g, m_ref[...], v_ref[...])
        g_ref[...] = g
        d_ref[...] = delta
        m2_ref[...] = m2
        v2_ref[...] = v2

    lay = pl.BlockSpec((None, tr, C), lambda i, me_ref: (layer, i, 0))
    in_specs = [lay, lay, lay, pl.BlockSpec((None, tr, C), lambda i, me_ref: (me_ref[0], i, 0)),
                pl.BlockSpec((N_PEER, tr, C), lambda i, me_ref: (0, i, 0))]
    args = [me, w, m, v, own, landed]
    aliases = {}
    if prev is not None:
        in_specs += [pl.BlockSpec(memory_space=pl.ANY)] * 4
        args += list(prev)
        aliases = {6 + i: i for i in range(4)}
    return pl.pallas_call(
        body, name=name,
        grid_spec=pltpu.PrefetchScalarGridSpec(num_scalar_prefetch=1, grid=(R // tr,), in_specs=in_specs, out_specs=[lay] * 4),
        out_shape=[_sds(w.shape, F32)] * 4,
        input_output_aliases=aliases,
        compiler_params=_cparams("parallel"),
    )(*args)


def _sum_parts(parts, name):
    _, R, C = parts.shape

    def body(p_ref, out_ref):
        g = p_ref[0]
        for j in range(1, N_DEV):
            g = g + p_ref[j]
        out_ref[...] = g

    return pl.pallas_call(
        body, name=name, grid=(1,),
        in_specs=[pl.BlockSpec((N_DEV, R, C), lambda i: (0, 0, 0))], out_specs=pl.BlockSpec((R, C), lambda i: (0, 0)),
        out_shape=_sds((R, C), F32), compiler_params=_cparams("arbitrary"),
    )(parts)


def _adamw_flat(w, g, m, v, name):
    R, C = w.shape

    def body(w_ref, g_ref, m_ref, v_ref, d_ref, m2_ref, v2_ref):
        delta, m2, v2 = _adamw_math(w_ref[...], g_ref[...], m_ref[...], v_ref[...])
        d_ref[...] = delta
        m2_ref[...] = m2
        v2_ref[...] = v2

    blk = pl.BlockSpec((R, C), lambda i: (0, 0))
    return pl.pallas_call(
        body, name=name, grid=(1,), in_specs=[blk] * 4, out_specs=[blk] * 3, out_shape=[_sds((R, C), F32)] * 3,
        compiler_params=_cparams("arbitrary"),
    )(w, g, m, v)


def _pack(arrays, rows_multiple=8):
    flat = []
    for a in arrays:
        a = a.reshape(-1).astype(F32)
        flat.append(jnp.pad(a, (0, (-a.shape[0]) % LANE)))
    flat = jnp.concatenate(flat)
    flat = jnp.pad(flat, (0, (-flat.shape[0]) % (LANE * rows_multiple)))
    return flat.reshape(-1, LANE)


def _unpack(packed, shapes):
    flat = packed.reshape(-1)
    out, off = [], 0
    for s in shapes:
        size = 1
        for dim in s:
            size *= dim
        out.append(flat[off:off + size].reshape(s))
        off += size + (-size) % LANE
    return out


def _layer_fwd(x, wts, getw, dims, dep=None):
    AW, BW, PW, DP = dims["AW"], dims["BW"], dims["PW"], dims["DP"]
    q_start = 2 * AW + 3 * BW
    h, r1 = _rmsnorm_fwd(x, wts["attn_norm"], "rmsnorm_fwd", dep=dep)
    p = _mm_nn(h, getw("w_in", h), "proj_in", tm=1024, tn=512)
    y_a = _sgu_fwd(p, wts["sgu_tril"], wts["sgu_bmat"], "sgu_fwd")
    y_b = _conv_fwd(p, getw("conv_w", y_a), AW, "conv_fwd")
    os_, lses = [], []
    for g, d in enumerate(DILATIONS):
        o, lse = _attn_fwd(p, wts["q_gain"], wts["k_gain"], g, d, DP, PW, q_start, "attn_fwd_%d" % d)
        os_.append(o)
        lses.append(lse)
    y_c = _mix_fwd(os_, lses, "mix_fwd")
    ymix = jnp.concatenate([y_a, y_b, y_c], axis=1)
    x1 = _mm_nn(ymix, getw("w_out", ymix), "proj_out", residual=x, tm=1024, tn=1024)
    h2, r2 = _rmsnorm_fwd(x1, wts["mlp_norm"], "rmsnorm_fwd")
    a, hid = _mm_nn(h2, getw("w_mlp_in", h2), "mlp_in", relu2=True, tm=1024, tn=1024)
    x2 = _mm_nn(hid, getw("w_mlp_out", hid), "mlp_out", residual=x1, a_single=True, tm=1024, tn=512)
    saved = dict(x=x, h=h, r1=r1, p=p, os=os_, lses=lses, ymix=ymix, x1=x1, h2=h2, r2=r2, a=a, hid=hid)
    return x2, saved


def _layer_bwd(dx, dxb, wts, getw, scatter, saved, dims):
    AW, BW, PW, DP = dims["AW"], dims["BW"], dims["PW"], dims["DP"]
    q_start = 2 * AW + 3 * BW
    D = dx.shape[1]
    g_w2 = _wgrad_wide_a(saved["hid"], dxb, "mlp_out_wgrad")
    token = scatter("w_mlp_out", g_w2.reshape(N_DEV, -1, D))
    da = _mm_nt(dxb, getw("w_mlp_out", None), "mlp_out_dgrad", out_dtype=MXU_DTYPE, relu2_pre=saved["a"], dep=token,
                tm=1024, tn=1024)
    g_w1 = _wgrad_wide_b(saved["h2"], da, "mlp_in_wgrad", groups=N_DEV)
    token = scatter("w_mlp_in", g_w1)
    dh2 = _mm_nt(da, getw("w_mlp_in", None), "mlp_in_dgrad", dep=token, tm=1024, tn=2048, tk=1024)
    dx1, dx1b, g_mlp_norm = _rmsnorm_bwd(dh2, saved["x1"], wts["mlp_norm"], saved["r2"], dx, "rmsnorm_bwd")
    g_wout = _wgrad_wide_b(saved["ymix"], dx1b, "proj_out_wgrad")
    token = scatter("w_out", g_wout.reshape(N_DEV, -1, D))
    dymix = _mm_nt(dx1b, getw("w_out", None), "proj_out_dgrad", dep=token, tm=1024, tn=1024)
    p = saved["p"]
    du, dv, g_sgu_w, g_sgu_bmat = _sgu_bwd(dymix, p, wts["sgu_tril"], wts["sgu_tril_t"], wts["sgu_bmat"], "sgu_bwd")
    d_b, d_c, d_xb, g_conv = _conv_bwd(dymix, p, getw("conv_w", None), AW, "conv_bwd")
    dos, corrs = _mix_bwd(dymix, saved["os"], saved["lses"], AW + BW, "mix_bwd")
    dqs, dks, dvs = [], [], []
    g_q = g_k = 0.0
    for g, d in enumerate(DILATIONS):
        dq, dk, dvv, dqg, dkg = _attn_bwd(p, wts["q_gain"], wts["k_gain"], saved["lses"][g], dos[g], corrs[g],
                                          g, d, PW, q_start, "attn_bwd_%d" % d)
        dqs.append(dq.astype(MXU_DTYPE))
        dks.append(dk.astype(MXU_DTYPE))
        dvs.append(dvv.astype(MXU_DTYPE))
        g_q = g_q + dqg.reshape(-1, HEAD_DIM).sum(0)
        g_k = g_k + dkg.reshape(-1, HEAD_DIM).sum(0)
    dp = jnp.concatenate([du, dv, d_b, d_c, d_xb] + dqs + dks + dvs, axis=1)
    g_win = _wgrad_wide_b(saved["h"], dp, "proj_in_wgrad")
    token = scatter("w_in", g_win.reshape(D, N_DEV, DP // N_DEV).transpose(1, 0, 2))
    dh = _mm_nt(dp, getw("w_in", None), "proj_in_dgrad", dep=token, a_single=True, tm=1024, tn=512)
    dx0, dx0b, g_attn_norm = _rmsnorm_bwd(dh, saved["x"], wts["attn_norm"], saved["r1"], dx1, "rmsnorm_bwd")
    H = AW // HEAD_DIM
    tril = jnp.tril(jnp.ones((CHUNK, CHUNK), F32))
    small = [g_attn_norm.reshape(-1), g_sgu_w * tril, g_sgu_bmat.reshape(CHUNK, H, HEAD_DIM).sum(-1).T,
             g_conv, g_q, g_k, g_mlp_norm.reshape(-1)]
    return dx0, dx0b, small


def kernel(x, attn_norm, w_in, sgu_w, sgu_b, conv_w, q_norm, k_norm, w_out, mlp_norm, w_mlp_in, w_mlp_out, loss_target, m_attn_norm, m_w_in, m_sgu_w, m_sgu_b, m_conv_w, m_q_norm, m_k_norm, m_w_out, m_mlp_norm, m_w_mlp_in, m_w_mlp_out, v_attn_norm, v_w_in, v_sgu_w, v_sgu_b, v_conv_w, v_q_norm, v_k_norm, v_w_out, v_mlp_norm, v_w_mlp_in, v_w_mlp_out):
    n_layers = attn_norm.shape[0]
    T, D = x.shape[1], x.shape[2]
    H = sgu_w.shape[1]
    AW = H * HEAD_DIM
    BW = conv_w.shape[2] * N_DEV
    DP = w_in.shape[2] * N_DEV
    DMIX = w_out.shape[1] * N_DEV
    DFF = w_mlp_in.shape[2] * N_DEV
    PW = (DMIX - AW - BW) // 3
    HP = PW // HEAD_DIM
    dims = dict(AW=AW, BW=BW, PW=PW, DP=DP)
    me = 4 * lax.axis_index("x") + 2 * lax.axis_index("y") + lax.axis_index("c")

    big_names = ("w_in", "w_out", "w_mlp_in", "w_mlp_out")
    big_w = dict(zip(big_names, (w_in, w_out, w_mlp_in, w_mlp_out)))
    big_m = dict(zip(big_names, (m_w_in, m_w_out, m_w_mlp_in, m_w_mlp_out)))
    big_v = dict(zip(big_names, (v_w_in, v_w_out, v_w_mlp_in, v_w_mlp_out)))

    keys = []
    for l in range(n_layers):
        keys += [(l, nm) for nm in big_names]
    keys.insert(1, (0, "conv_w"))
    srcs = [_pack([conv_w]) if nm == "conv_w" else big_w[nm][l].astype(MXU_DTYPE) for l, nm in keys]
    flights, gather_token = _exchange_start(srcs, [_own_in_place(s, me) for s in srcs], "gather", name="gather_start")
    arriving = dict(zip(keys, flights))
    forwarding = {}
    relayout = dict(
        w_in=lambda g: g.transpose(1, 0, 2).reshape(D, DP), w_out=lambda g: g.reshape(DMIX, D),
        w_mlp_in=lambda g: g, w_mlp_out=lambda g: g.reshape(DFF, D),
        conv_w=lambda g: jnp.stack([_unpack(g[j], [conv_w.shape])[0] for j in range(N_DEV)], axis=2).reshape(
            n_layers, CONV_WIDTH, BW))
    gathered = {}

    def forward(key, after):
        _, land = _exchange_wait(arriving[key], after, "gather", name="gather_arrive_%d_%s" % key)
        fl, token = _exchange_start(None, [land], "forward", name="gather_forward_%d_%s" % key)
        forwarding[key] = fl[0]
        return token

    def weight_getter(l):
        def getw(nm, after):
            key = (0, nm) if nm == "conv_w" else (l, nm)
            if key not in gathered:
                ahead = keys[keys.index(key):][:2]
                for k in ahead:
                    if k not in forwarding:
                        after = forward(k, after)
                _, land = _exchange_wait(forwarding[key], after, "forward", name="gather_wait_%d_%s" % key)
                gathered[key] = relayout[nm](land)
            return gathered[key][l] if nm == "conv_w" else gathered[key]
        return getw

    tril = jnp.tril(jnp.ones((CHUNK, CHUNK), F32))
    layers = []
    for l in range(n_layers):
        w_tril = sgu_w[l] * tril
        layers.append(dict(
            attn_norm=attn_norm[l][None], mlp_norm=mlp_norm[l][None],
            sgu_tril=w_tril.astype(MXU_DTYPE), sgu_tril_t=w_tril.transpose(0, 2, 1).astype(MXU_DTYPE),
            sgu_bmat=jnp.repeat(sgu_b[l].T, HEAD_DIM, axis=1),
            q_gain=jnp.tile(q_norm[l], HEADS_PER_LANES)[None], k_gain=jnp.tile(k_norm[l], HEADS_PER_LANES)[None]))

    xs = x[0]
    saved = []
    for l in range(n_layers):
        xs, sv = _layer_fwd(xs, layers[l], weight_getter(l), dims, dep=gather_token if l == 0 else None)
        saved.append(sv)
    loss_blk, dx, dxb = _loss_and_grad(xs, loss_target[0], "loss")
    loss = lax.psum(loss_blk[0, 0], ("x", "y", "c"))

    scattering = {}

    def scatter_starter(l):
        def scatter(nm, partials):
            land = lax.empty((N_PEER,) + partials.shape[1:], partials.dtype)
            fl, tok = _exchange_start([partials], [land], "scatter", name="scatter_start_%d_%s" % (l, nm))
            scattering[(l, nm)] = fl[0]
            return tok
        return scatter

    small = [None] * n_layers
    for l in reversed(range(n_layers)):
        dx, dxb, small[l] = _layer_bwd(dx, dxb, layers[l], weight_getter(l), scatter_starter(l), saved[l], dims)

    small_shapes = [s.shape for s in small[0]]
    small_src = [_pack([s for l in range(n_layers) for s in small[l]])]
    small_flights, small_token = _exchange_start(small_src, [_own_in_place(s, me) for s in small_src], "gather_all",
                                                 name="small_start")
    grad_x = dx[None]

    me1 = me.astype(jnp.int32).reshape(1)
    res = {nm: None for nm in big_names}
    after = small_token
    for l in reversed(range(n_layers)):
        for nm in reversed(big_names):
            own, landed = _exchange_wait(scattering[(l, nm)], after, "scatter", name="scatter_wait_%d_%s" % (l, nm))
            res[nm] = _adamw_layer(l, big_w[nm], big_m[nm], big_v[nm], own, landed, me1, res[nm], "adamw_" + nm)
            after = res[nm][0]
    big_out = [res[nm] for nm in big_names]

    _, gathered_small = _exchange_wait(small_flights[0], after, "gather_all", name="small_wait")
    summed = _unpack(_sum_parts(gathered_small, "sum_small"), small_shapes * n_layers)
    ns = len(small_shapes)
    g_small = [jnp.stack([summed[l * ns + i] for l in range(n_layers)]) for i in range(ns)]
    g_attn_norm, g_sgu_w, g_sgu_b, g_conv_full, g_q, g_k, g_mlp_norm = g_small
    cs = conv_w.shape[2]
    g_conv = lax.dynamic_slice_in_dim(g_conv_full, me * cs, cs, axis=2)
    sm_w = (attn_norm, sgu_w, sgu_b, conv_w, q_norm, k_norm, mlp_norm)
    sm_m = (m_attn_norm, m_sgu_w, m_sgu_b, m_conv_w, m_q_norm, m_k_norm, m_mlp_norm)
    sm_v = (v_attn_norm, v_sgu_w, v_sgu_b, v_conv_w, v_q_norm, v_k_norm, v_mlp_norm)
    sm_g = (g_attn_norm, g_sgu_w, g_sgu_b, g_conv, g_q, g_k, g_mlp_norm)
    sm_res = _adamw_flat(_pack(sm_w), _pack(sm_g), _pack(sm_m), _pack(sm_v), "adamw_small")
    shapes = [w.shape for w in sm_w]
    sm_delta, sm_m2, sm_v2 = (_unpack(r, shapes) for r in sm_res)

    def ordered(small_list, big_kind):
        b = [big_out[i][big_kind] for i in range(4)]
        return [small_list[0], b[0], small_list[1], small_list[2], small_list[3], small_list[4], small_list[5],
                b[1], small_list[6], b[2], b[3]]

    return (loss, grad_x, *ordered(list(sm_g), 0), *ordered(sm_delta, 1), *ordered(sm_m2, 2), *ordered(sm_v2, 3))
```

```python
import jax
import jax.numpy as jnp
from jax import lax
from jax.experimental import pallas as pl
from jax.experimental.pallas import tpu as pltpu

N_DEV = 8
HEAD_DIM = 64
CHUNK = 128
ATT_BLK = 128
DILATIONS = (1, 4, 16)
CONV_WIDTH = 3
EPS = 1e-6
ADAM_LR = 0.001
ADAM_B1 = 0.9
ADAM_B2 = 0.999
ADAM_EPS = 1e-08
ADAM_WD = 0.01
ADAM_STEP = 10
MXU_DTYPE = jnp.bfloat16
F32 = jnp.float32
LANE = 128
VMEM_LIMIT_BYTES = 56 * 1024 * 1024
NEG_INF = float("-inf")


def _cparams(*sem):
    return pltpu.CompilerParams(dimension_semantics=sem, vmem_limit_bytes=VMEM_LIMIT_BYTES)


def _sds(shape, dtype):
    return jax.ShapeDtypeStruct(shape, dtype)


def _fit(n, tile):
    for t in range(min(tile, n) // LANE * LANE, 0, -LANE):
        if n % t == 0:
            return t
    return n


_HBM = pl.BlockSpec(memory_space=pltpu.HBM)
_SEM = pl.BlockSpec(memory_space=pltpu.SEMAPHORE)
_DATAFLOW = pltpu.SideEffectType.DATAFLOW_SIDE_EFFECTING
N_PEER = N_DEV - 1


def _mesh_pos():
    x, y, c = lax.axis_index("x"), lax.axis_index("y"), lax.axis_index("c")
    return x, y, c, 4 * x + 2 * y + c


OTHER_CHIPS = (4, 2, 6)
EXCHANGE_PEERS = dict(
    scatter=tuple(range(1, N_DEV)),
    gather_all=tuple(range(1, N_DEV)),
    gather=(1,) + OTHER_CHIPS,
    forward=OTHER_CHIPS)


def _remote_copies(src, land, send_sems, recv_sems, mode):
    x, y, c, me = _mesh_pos()
    copies = []
    for i, k in enumerate(EXCHANGE_PEERS[mode]):
        px = (1 - x) if (k & 4) else x
        py = (1 - y) if (k & 2) else y
        pc = (1 - c) if (k & 1) else c
        if mode == "scatter":
            src_ref, dst_ref, dev = src.at[4 * px + 2 * py + pc], land.at[i], (px, py, pc)
        elif mode == "forward":
            slot = 4 * px + 2 * py + c
            src_ref, dst_ref, dev = land.at[slot], land.at[slot], (x, y, 1 - c)
        else:
            src_ref, dst_ref, dev = src, land.at[me], (px, py, pc)
        copies.append(pltpu.make_async_remote_copy(
            src_ref=src_ref, dst_ref=dst_ref, send_sem=send_sems.at[i], recv_sem=recv_sems.at[i],
            device_id=dev, device_id_type=pl.DeviceIdType.MESH))
    return copies


def _own_in_place(src, me):
    land = lax.empty((N_DEV,) + src.shape, src.dtype)
    return lax.dynamic_update_slice(land, src[None], (me,) + (0,) * src.ndim)


def _exchange_start(srcs, lands, mode, name):
    n = len(lands)
    has_src = srcs is not None
    arrays = (list(srcs) if has_src else []) + list(lands)
    n_arr = len(arrays)
    n_copies = len(EXCHANGE_PEERS[mode])

    def body(*refs):
        src = refs[:n] if has_src else [None] * n
        land = refs[n_arr - n:n_arr]
        send, recv = refs[n_arr:n_arr + n], refs[n_arr + n:n_arr + 2 * n]
        token = refs[2 * n_arr + 2 * n]
        for t in range(n):
            for cp in _remote_copies(src[t], land[t], send[t], recv[t], mode):
                cp.start()
        token[...] = jnp.zeros_like(token)

    outs = pl.pallas_call(
        body, name=name,
        out_shape=([pltpu.SemaphoreType.DMA((n_copies,))] * (2 * n) + [pltpu.HBM(a.shape, a.dtype) for a in arrays]
                   + [_sds((8, LANE), F32)]),
        in_specs=[_HBM] * n_arr,
        out_specs=[_SEM] * (2 * n) + [_HBM] * n_arr + [pl.BlockSpec(memory_space=pltpu.VMEM)],
        input_output_aliases={i: 2 * n + i for i in range(n_arr)},
        compiler_params=pltpu.CompilerParams(has_side_effects=_DATAFLOW),
    )(*[pltpu.with_memory_space_constraint(a, pltpu.HBM) for a in arrays])
    thru = outs[2 * n:2 * n + n_arr]
    flights = [(outs[t], outs[n + t], thru[t] if has_src else None, thru[n_arr - n + t]) for t in range(n)]
    return flights, outs[2 * n + n_arr]


def _exchange_wait(flight, after, mode, name):
    send, recv, src, land = flight
    arrays = [land] if src is None else [src, land]
    n_arr = len(arrays)

    def body(*refs):
        src_ref = refs[0] if n_arr == 2 else None
        land_ref, send_ref, recv_ref = refs[n_arr - 1], refs[n_arr], refs[n_arr + 1]
        for cp in _remote_copies(src_ref, land_ref, send_ref, recv_ref, mode):
            cp.wait_send()
            cp.wait_recv()

    outs = pl.pallas_call(
        body, name=name, out_shape=[pltpu.HBM(a.shape, a.dtype) for a in arrays],
        in_specs=[_HBM] * n_arr + [_SEM, _SEM, pl.BlockSpec(memory_space=pl.ANY)], out_specs=[_HBM] * n_arr,
        input_output_aliases={i: i for i in range(n_arr)},
        compiler_params=pltpu.CompilerParams(has_side_effects=_DATAFLOW),
    )(*arrays, send, recv, after)
    return (None, outs[0]) if src is None else (outs[0], outs[1])


def _rmsnorm_fwd(x, g, name, dep=None, tr=512):
    T, D = x.shape

    def body(x_ref, g_ref, *rest):
        h_ref, r_ref = rest[-2:]
        xv = x_ref[...]
        r = lax.rsqrt(jnp.mean(xv * xv, axis=-1, keepdims=True) + EPS)
        h_ref[...] = (xv * r * g_ref[...]).astype(h_ref.dtype)
        r_ref[...] = r

    in_specs = [pl.BlockSpec((tr, D), lambda i: (i, 0)), pl.BlockSpec((1, D), lambda i: (0, 0))]
    args = [x, g]
    if dep is not None:
        in_specs.append(pl.BlockSpec(dep.shape, lambda i: (0, 0)))
        args.append(dep)
    return pl.pallas_call(
        body, name=name, grid=(T // tr,),
        in_specs=in_specs,
        out_specs=[pl.BlockSpec((tr, D), lambda i: (i, 0)), pl.BlockSpec((tr, 1), lambda i: (i, 0))],
        out_shape=[_sds((T, D), MXU_DTYPE), _sds((T, 1), F32)],
        compiler_params=_cparams("parallel"),
    )(*args)


def _rmsnorm_bwd(dh, x, g, r, dres, name, tr=256):
    T, D = x.shape

    def body(dh_ref, x_ref, g_ref, r_ref, dres_ref, dx_ref, dxb_ref, dg_ref):
        @pl.when(pl.program_id(0) == 0)
        def _():
            dg_ref[...] = jnp.zeros_like(dg_ref)

        dh_v, xv, rv = dh_ref[...], x_ref[...], r_ref[...]
        gdy = dh_v * g_ref[...]
        mean_xg = jnp.mean(xv * gdy, axis=-1, keepdims=True)
        dx = dres_ref[...] + rv * gdy - xv * (rv * rv * rv) * mean_xg
        dx_ref[...] = dx
        dxb_ref[...] = dx.astype(dxb_ref.dtype)
        dg_ref[...] += jnp.sum(dh_v * xv * rv, axis=0, keepdims=True)

    row = lambda i: (i, 0)
    return pl.pallas_call(
        body, name=name, grid=(T // tr,),
        in_specs=[pl.BlockSpec((tr, D), row), pl.BlockSpec((tr, D), row), pl.BlockSpec((1, D), lambda i: (0, 0)),
                  pl.BlockSpec((tr, 1), row), pl.BlockSpec((tr, D), row)],
        out_specs=[pl.BlockSpec((tr, D), row), pl.BlockSpec((tr, D), row), pl.BlockSpec((1, D), lambda i: (0, 0))],
        out_shape=[_sds((T, D), F32), _sds((T, D), MXU_DTYPE), _sds((1, D), F32)],
        compiler_params=_cparams("arbitrary"),
    )(dh, x, g, r, dres)


def _mm_nn(a, b, name, out_dtype=F32, residual=None, relu2=False, columns_first=False, tm=512, tn=512):
    M, K = a.shape
    grouped = b.ndim == 3
    N = b.shape[0] * b.shape[2] if grouped else b.shape[1]
    tm, tn = _fit(M, tm), _fit(b.shape[2] if grouped else N, tn)
    tile = (lambda j, i: (i, j)) if columns_first else (lambda i, j: (i, j))
    b_mode = pl.Buffered(1 if columns_first else 2)
    if grouped:
        per = b.shape[2] // tn
        b_spec = pl.BlockSpec((None, K, tn), lambda *g: (tile(*g)[1] // per, 0, tile(*g)[1] % per), pipeline_mode=b_mode)
    else:
        b_spec = pl.BlockSpec((K, tn), lambda *g: (0, tile(*g)[1]), pipeline_mode=b_mode)
    n_out = 2 if relu2 else 1

    def body(*refs):
        a_ref, b_ref = refs[0], refs[1]
        r_ref = refs[2] if residual is not None else None
        outs = refs[(3 if residual is not None else 2):]
        acc = jnp.dot(a_ref[...], b_ref[...], preferred_element_type=F32)
        if r_ref is not None:
            acc = acc + r_ref[...]
        outs[0][...] = acc.astype(outs[0].dtype)
        if relu2:
            rl = jnp.maximum(acc, 0.0)
            outs[1][...] = (rl * rl).astype(outs[1].dtype)

    out_blk = pl.BlockSpec((tm, tn), lambda *g: tile(*g))
    in_specs = [pl.BlockSpec((tm, K), lambda *g: (tile(*g)[0], 0)), b_spec]
    args = [a, b]
    if residual is not None:
        in_specs.append(out_blk)
        args.append(residual)
    out_shape = [_sds((M, N), out_dtype)]
    if relu2:
        out_shape.append(_sds((M, N), MXU_DTYPE))
    outs = pl.pallas_call(
        body, name=name, grid=(N // tn, M // tm) if columns_first else (M // tm, N // tn),
        in_specs=in_specs, out_specs=[out_blk] * n_out, out_shape=out_shape,
        compiler_params=_cparams("parallel", "parallel"),
    )(*args)
    return outs if relu2 else outs[0]


def _mm_nt(a, b, name, out_dtype=F32, relu2_pre=None, dep=None, a_single=False, tm=512, tn=512):
    M, K = a.shape
    grouped = b.ndim == 3
    N = b.shape[1] if grouped else b.shape[0]
    tm, tn = _fit(M, tm), _fit(N, tn)
    nt = (((1,), (1,)), ((), ()))
    if grouped:
        G, _, Kg = b.shape
        b_spec = pl.BlockSpec((G, tn, Kg), lambda i, j: (0, j, 0))
    else:
        b_spec = pl.BlockSpec((tn, K), lambda i, j: (j, 0))

    def body(*refs):
        a_ref, b_ref = refs[0], refs[1]
        p_ref = refs[2] if relu2_pre is not None else None
        out_ref = refs[2 + (relu2_pre is not None) + (dep is not None)]
        if grouped:
            acc = lax.dot_general(a_ref[:, 0:Kg], b_ref[0], nt, preferred_element_type=F32)
            for g in range(1, G):
                acc += lax.dot_general(a_ref[:, g * Kg:(g + 1) * Kg], b_ref[g], nt, preferred_element_type=F32)
        else:
            acc = lax.dot_general(a_ref[...], b_ref[...], nt, preferred_element_type=F32)
        if p_ref is not None:
            acc = acc * (2.0 * jnp.maximum(p_ref[...].astype(F32), 0.0))
        out_ref[...] = acc.astype(out_ref.dtype)

    out_blk = pl.BlockSpec((tm, tn), lambda i, j: (i, j))
    in_specs = [pl.BlockSpec((tm, K), lambda i, j: (i, 0), pipeline_mode=pl.Buffered(1 if a_single else 2)), b_spec]
    args = [a, b]
    if relu2_pre is not None:
        in_specs.append(out_blk)
        args.append(relu2_pre)
    if dep is not None:
        in_specs.append(pl.BlockSpec(dep.shape, lambda i, j: (0, 0)))
        args.append(dep)
    return pl.pallas_call(
        body, name=name, grid=(M // tm, N // tn),
        in_specs=in_specs, out_specs=out_blk, out_shape=_sds((M, N), out_dtype),
        compiler_params=_cparams("parallel", "parallel"),
    )(*args)


def _wgrad_wide_a(a, b, name, tm=512):
    T, M = a.shape
    N = b.shape[1]
    tm = _fit(M, tm)

    def body(a_ref, b_ref, out_ref):
        out_ref[...] = lax.dot_general(a_ref[...], b_ref[...], (((0,), (0,)), ((), ())),
                                       preferred_element_type=F32).astype(out_ref.dtype)

    return pl.pallas_call(
        body, name=name, grid=(M // tm,),
        in_specs=[pl.BlockSpec((T, tm), lambda i: (0, i)),
                  pl.BlockSpec((T, N), lambda i: (0, 0), pipeline_mode=pl.Buffered(1))],
        out_specs=pl.BlockSpec((tm, N), lambda i: (i, 0)), out_shape=_sds((M, N), MXU_DTYPE),
        compiler_params=_cparams("parallel"),
    )(a, b)


def _wgrad_wide_b(a, b, name, groups=None, tn=512, t_chunk=512):
    T, M = a.shape
    N = b.shape[1]
    tn = _fit(N if groups is None else N // groups, tn)
    t_chunk = _fit(T, t_chunk)

    def body(a_ref, b_ref, out_ref, at_ref):
        @pl.when(pl.program_id(0) == 0)
        def _():
            for c in range(0, T, t_chunk):
                at_ref[:, c:c + t_chunk] = a_ref[c:c + t_chunk, :].T

        out_ref[...] = jnp.dot(at_ref[...], b_ref[...], preferred_element_type=F32).astype(out_ref.dtype)

    if groups is None:
        out_spec = pl.BlockSpec((M, tn), lambda j: (0, j))
        out_shape = _sds((M, N), MXU_DTYPE)
    else:
        per = N // groups // tn
        out_spec = pl.BlockSpec((None, M, tn), lambda j: (j // per, 0, j % per))
        out_shape = _sds((groups, M, N // groups), MXU_DTYPE)
    return pl.pallas_call(
        body, name=name, grid=(N // tn,),
        in_specs=[pl.BlockSpec((T, M), lambda j: (0, 0), pipeline_mode=pl.Buffered(1)),
                  pl.BlockSpec((T, tn), lambda j: (0, j))],
        out_specs=out_spec, out_shape=out_shape,
        scratch_shapes=[pltpu.VMEM((M, T), MXU_DTYPE)],
        compiler_params=_cparams("arbitrary"),
    )(a, b)


def _loss_and_grad(y, target, name, tr=512):
    T, D = y.shape

    def body(y_ref, t_ref, loss_ref, dx_ref, dxb_ref):
        @pl.when(pl.program_id(0) == 0)
        def _():
            loss_ref[...] = jnp.zeros_like(loss_ref)

        err = y_ref[...] - t_ref[...]
        loss_ref[...] += 0.5 * jnp.sum(jnp.mean(err * err, axis=-1, keepdims=True), axis=0, keepdims=True)
        dx = err * (1.0 / D)
        dx_ref[...] = dx
        dxb_ref[...] = dx.astype(dxb_ref.dtype)

    row = lambda i: (i, 0)
    return pl.pallas_call(
        body, name=name, grid=(T // tr,),
        in_specs=[pl.BlockSpec((tr, D), row), pl.BlockSpec((tr, D), row)],
        out_specs=[pl.BlockSpec((8, LANE), lambda i: (0, 0)), pl.BlockSpec((tr, D), row), pl.BlockSpec((tr, D), row)],
        out_shape=[_sds((8, LANE), F32), _sds((T, D), F32), _sds((T, D), MXU_DTYPE)],
        compiler_params=_cparams("arbitrary"),
    )(y, target)


def _sgu_mixed(v, w_ref, b_ref, col_head, n_heads):
    mixed = b_ref[...]
    for h in range(n_heads):
        full = jnp.dot(w_ref[h], v, preferred_element_type=F32)
        mixed = mixed + jnp.where(col_head == h, full, 0.0)
    return mixed


def _sgu_fwd(p, w_tril, bmat, name):
    T = p.shape[0]
    H = w_tril.shape[0]
    AW = H * HEAD_DIM

    def body(u_ref, v_ref, w_ref, b_ref, y_ref):
        col_head = lax.broadcasted_iota(jnp.int32, (CHUNK, AW), 1) // HEAD_DIM
        mixed = _sgu_mixed(v_ref[...].astype(MXU_DTYPE), w_ref, b_ref, col_head, H)
        y_ref[...] = (u_ref[...] * mixed).astype(y_ref.dtype)

    const3 = lambda c: (0, 0, 0)
    return pl.pallas_call(
        body, name=name, grid=(T // CHUNK,),
        in_specs=[pl.BlockSpec((CHUNK, AW), lambda c: (c, 0)), pl.BlockSpec((CHUNK, AW), lambda c: (c, 1)),
                  pl.BlockSpec((H, CHUNK, CHUNK), const3), pl.BlockSpec((CHUNK, AW), lambda c: (0, 0))],
        out_specs=pl.BlockSpec((CHUNK, AW), lambda c: (c, 0)),
        out_shape=_sds((T, AW), MXU_DTYPE),
        compiler_params=_cparams("parallel"),
    )(p, p, w_tril, bmat)


def _sgu_bwd(dymix, p, w_tril, w_tril_t, bmat, name):
    T = p.shape[0]
    H = w_tril.shape[0]
    AW = H * HEAD_DIM

    def body(dy_ref, u_ref, v_ref, w_ref, wt_ref, b_ref, du_ref, dv_ref, dw_ref, db_ref):
        @pl.when(pl.program_id(0) == 0)
        def _():
            dw_ref[...] = jnp.zeros_like(dw_ref)
            db_ref[...] = jnp.zeros_like(db_ref)

        col_head = lax.broadcasted_iota(jnp.int32, (CHUNK, AW), 1) // HEAD_DIM
        v = v_ref[...].astype(MXU_DTYPE)
        dy = dy_ref[...]
        mixed = _sgu_mixed(v, w_ref, b_ref, col_head, H)
        du_ref[...] = (dy * mixed).astype(du_ref.dtype)
        dm = dy * u_ref[...]
        db_ref[...] += dm
        dm_c = dm.astype(MXU_DTYPE)
        dv = jnp.zeros((CHUNK, AW), F32)
        for h in range(H):
            sel = col_head == h
            dv = dv + jnp.where(sel, jnp.dot(wt_ref[h], dm_c, preferred_element_type=F32), 0.0)
            dm_h = jnp.where(sel, dm, 0.0).astype(MXU_DTYPE)
            dw_ref[h] += lax.dot_general(dm_h, v, (((1,), (1,)), ((), ())), preferred_element_type=F32)
        dv_ref[...] = dv.astype(dv_ref.dtype)

    const3 = lambda c: (0, 0, 0)
    blk = pl.BlockSpec((CHUNK, AW), lambda c: (c, 0))
    return pl.pallas_call(
        body, name=name, grid=(T // CHUNK,),
        in_specs=[blk, blk, pl.BlockSpec((CHUNK, AW), lambda c: (c, 1)),
                  pl.BlockSpec((H, CHUNK, CHUNK), const3), pl.BlockSpec((H, CHUNK, CHUNK), const3),
                  pl.BlockSpec((CHUNK, AW), lambda c: (0, 0))],
        out_specs=[blk, blk, pl.BlockSpec((H, CHUNK, CHUNK), const3), pl.BlockSpec((CHUNK, AW), lambda c: (0, 0))],
        out_shape=[_sds((T, AW), MXU_DTYPE), _sds((T, AW), MXU_DTYPE), _sds((H, CHUNK, CHUNK), F32), _sds((CHUNK, AW), F32)],
        compiler_params=_cparams("arbitrary"),
    )(dymix, p, p, w_tril, w_tril_t, bmat)


def _shift_down(z, s, row):
    return jnp.where(row >= s, pltpu.roll(z, s, 0), 0.0)


def _shift_up(z, s, row, T):
    return jnp.where(row < T - s, pltpu.roll(z, T - s, 0), 0.0)


def _conv_fwd(p, w_conv, AW, name):
    T = p.shape[0]
    BW = w_conv.shape[1]
    nb = BW // LANE
    b0 = 2 * AW // LANE

    def body(b_ref, c_ref, x_ref, w_ref, y_ref):
        row = lax.broadcasted_iota(jnp.int32, (T, LANE), 0)
        z = c_ref[...] * x_ref[...]
        w0, w1, w2 = w_ref[0:1, :], w_ref[1:2, :], w_ref[2:3, :]
        conv = w2 * z + w1 * _shift_down(z, 1, row) + w0 * _shift_down(z, 2, row)
        y_ref[...] = (b_ref[...] * conv).astype(y_ref.dtype)

    return pl.pallas_call(
        body, name=name, grid=(nb,),
        in_specs=[pl.BlockSpec((T, LANE), lambda j: (0, b0 + j)), pl.BlockSpec((T, LANE), lambda j: (0, b0 + nb + j)),
                  pl.BlockSpec((T, LANE), lambda j: (0, b0 + 2 * nb + j)), pl.BlockSpec((CONV_WIDTH, LANE), lambda j: (0, j))],
        out_specs=pl.BlockSpec((T, LANE), lambda j: (0, j)),
        out_shape=_sds((T, BW), MXU_DTYPE),
        compiler_params=_cparams("parallel"),
    )(p, p, p, w_conv)


def _conv_bwd(dymix, p, w_conv, AW, name):
    T = p.shape[0]
    BW = w_conv.shape[1]
    nb = BW // LANE
    b0 = 2 * AW // LANE
    y0 = AW // LANE

    def body(dy_ref, b_ref, c_ref, x_ref, w_ref, db_ref, dc_ref, dxb_ref, dw_ref):
        row = lax.broadcasted_iota(jnp.int32, (T, LANE), 0)
        cv, xv, dy = c_ref[...], x_ref[...], dy_ref[...]
        w0, w1, w2 = w_ref[0:1, :], w_ref[1:2, :], w_ref[2:3, :]
        z = cv * xv
        z1 = _shift_down(z, 1, row)
        z2 = _shift_down(z, 2, row)
        conv = w2 * z + w1 * z1 + w0 * z2
        db_ref[...] = (dy * conv).astype(db_ref.dtype)
        dconv = dy * b_ref[...]
        dz = w2 * dconv + w1 * _shift_up(dconv, 1, row, T) + w0 * _shift_up(dconv, 2, row, T)
        dc_ref[...] = (dz * xv).astype(dc_ref.dtype)
        dxb_ref[...] = (dz * cv).astype(dxb_ref.dtype)
        dw_ref[0:1, :] = jnp.sum(dconv * z2, axis=0, keepdims=True)
        dw_ref[1:2, :] = jnp.sum(dconv * z1, axis=0, keepdims=True)
        dw_ref[2:3, :] = jnp.sum(dconv * z, axis=0, keepdims=True)

    col = lambda j: (0, j)
    return pl.pallas_call(
        body, name=name, grid=(nb,),
        in_specs=[pl.BlockSpec((T, LANE), lambda j: (0, y0 + j)),
                  pl.BlockSpec((T, LANE), lambda j: (0, b0 + j)), pl.BlockSpec((T, LANE), lambda j: (0, b0 + nb + j)),
                  pl.BlockSpec((T, LANE), lambda j: (0, b0 + 2 * nb + j)), pl.BlockSpec((CONV_WIDTH, LANE), col)],
        out_specs=[pl.BlockSpec((T, LANE), col)] * 3 + [pl.BlockSpec((CONV_WIDTH, LANE), col)],
        out_shape=[_sds((T, BW), MXU_DTYPE)] * 3 + [_sds((CONV_WIDTH, BW), F32)],
        compiler_params=_cparams("parallel"),
    )(dymix, p, p, p, w_conv)


def _head_sum(x, col_head, n_heads):
    out = jnp.zeros_like(x)
    for h in range(n_heads):
        sel = col_head == h
        out = jnp.where(sel, jnp.sum(jnp.where(sel, x, 0.0), axis=-1, keepdims=True), out)
    return out


def _same_head(width):
    assert width == 2 * HEAD_DIM
    return lax.broadcasted_iota(jnp.int32, (1, width), 1) < HEAD_DIM


def _head_sum2(x, first):
    s0 = jnp.sum(jnp.where(first, x, 0.0), axis=-1, keepdims=True)
    s1 = jnp.sum(jnp.where(first, 0.0, x), axis=-1, keepdims=True)
    return jnp.where(first, s0, s1)


def _head_norm(x, g, first):
    r = lax.rsqrt(_head_sum2(x * x, first) * (1.0 / HEAD_DIM) + EPS)
    return x * r * g, r


def _head_norm_bwd(dy, x, g, r, first):
    gdy = dy * g
    mean_xg = _head_sum2(x * gdy, first) * (1.0 / HEAD_DIM)
    return r * gdy - x * (r * r * r) * mean_xg, dy * x * r


ATT_SPAN_MIN = 512
ATT_FWD_UNROLL = 4
ATT_BWD_UNROLL = 2
HEADS_PER_LANES = LANE // HEAD_DIM


def _attn_geometry(T, d):
    m = max(1, ATT_SPAN_MIN // (ATT_BLK * d))
    return m, ATT_BLK * d * m, ATT_BLK * d, T // (ATT_BLK * d)


def _rows(ref, start, d):
    return ref[pl.ds(start, ATT_BLK, stride=d), :] if d > 1 else ref[pl.ds(start, ATT_BLK), :]


def _set_rows(ref, start, d, value):
    if d > 1:
        ref[pl.ds(start, ATT_BLK, stride=d), :] = value
    else:
        ref[pl.ds(start, ATT_BLK), :] = value


def _for_each_block(d, m, task, unroll):
    for j in range(m):
        if d == 1:
            task(0, j)
        else:
            lax.fori_loop(0, d, lambda r, carry, j=j: (task(r, j), carry)[1], 0, unroll=min(unroll, d))


def _head_slices():
    return [slice(h * HEAD_DIM, (h + 1) * HEAD_DIM) for h in range(HEADS_PER_LANES)]


def _attn_fwd(p, qg, kg, g, d, DP, PW, q_start, name):
    T = p.shape[0]
    B, W = ATT_BLK, LANE
    m, span, group, _ = _attn_geometry(T, d)
    c_q = (q_start + g * PW) // W
    c_k, c_v = c_q + 3 * PW // W, c_q + 6 * PW // W
    scale = HEAD_DIM ** -0.5

    def body(q_ref, k_ref, v_ref, kp_ref, vp_ref, qg_ref, kg_ref, o_ref, lse_ref):
        n = pl.program_id(1)
        same = _same_head(W)
        qi = lax.broadcasted_iota(jnp.int32, (B, 2 * B), 0)
        kj = lax.broadcasted_iota(jnp.int32, (B, 2 * B), 1)
        band = (kj >= qi) & (kj <= qi + B)
        qgv, kgv = qg_ref[...], kg_ref[...]

        def task(r, j):
            cur = j * group + r
            if j == 0:
                kp, vp = _rows(kp_ref, r, d), _rows(vp_ref, r, d)
            else:
                kp, vp = _rows(k_ref, cur - group, d), _rows(v_ref, cur - group, d)
            mask = band & ((n * m + j > 0) | (kj >= B))
            qn, _ = _head_norm(_rows(q_ref, cur, d), qgv, same)
            kn, _ = _head_norm(jnp.concatenate([kp, _rows(k_ref, cur, d)], axis=0), kgv, same)
            qn, kn = qn.astype(MXU_DTYPE), kn.astype(MXU_DTYPE)
            vcat = jnp.concatenate([vp, _rows(v_ref, cur, d)], axis=0).astype(MXU_DTYPE)
            o_parts, lse_parts = [], []
            for sl in _head_slices():
                s = lax.dot_general(qn[:, sl], kn[:, sl], (((1,), (1,)), ((), ())), preferred_element_type=F32) * scale
                s = jnp.where(mask, s, NEG_INF)
                mx = jnp.max(s, axis=-1, keepdims=True)
                e = jnp.exp(s - mx)
                den = jnp.sum(e, axis=-1, keepdims=True)
                o_parts.append(jnp.dot(e.astype(MXU_DTYPE), vcat[:, sl], preferred_element_type=F32) / den)
                lse_parts.append(jnp.broadcast_to(mx + jnp.log(den), (B, HEAD_DIM)))
            _set_rows(o_ref, cur, d, jnp.concatenate(o_parts, axis=1))
            _set_rows(lse_ref, cur, d, jnp.concatenate(lse_parts, axis=1))

        _for_each_block(d, m, task, ATT_FWD_UNROLL)

    main = lambda c0: pl.BlockSpec((span, W), lambda hp, n: (n, c0 + hp))
    prev = lambda c0: pl.BlockSpec((group, W), lambda hp, n: (jnp.maximum(n * m - 1, 0), c0 + hp))
    gain = pl.BlockSpec((1, W), lambda hp, n: (0, 0))
    out_blk = pl.BlockSpec((span, W), lambda hp, n: (n, hp))
    return pl.pallas_call(
        body, name=name, grid=(PW // W, T // span),
        in_specs=[main(c_q), main(c_k), main(c_v), prev(c_k), prev(c_v), gain, gain],
        out_specs=[out_blk, out_blk],
        out_shape=[_sds((T, PW), F32), _sds((T, PW), F32)],
        compiler_params=_cparams("parallel", "parallel"),
    )(p, p, p, p, p, qg, kg)


def _attn_bwd(p, qg, kg, lse, do, corr, g, d, PW, q_start, name):
    T = p.shape[0]
    B, W = ATT_BLK, LANE
    m, span, group, n_blocks = _attn_geometry(T, d)
    c_q = (q_start + g * PW) // W
    c_k, c_v = c_q + 3 * PW // W, c_q + 6 * PW // W
    scale = HEAD_DIM ** -0.5
    nt = (((1,), (1,)), ((), ()))
    tn = (((0,), (0,)), ((), ()))

    def body(q_ref, k_ref, v_ref, do_ref, l_ref, c_ref, kp_ref, vp_ref, qx_ref, dox_ref, lx_ref, cx_ref, qg_ref, kg_ref,
             dq_ref, dk_ref, dv_ref, dqg_ref, dkg_ref):
        hp, n = pl.program_id(0), pl.program_id(1)

        @pl.when((hp == 0) & (n == 0))
        def _():
            dqg_ref[...] = jnp.zeros_like(dqg_ref)
            dkg_ref[...] = jnp.zeros_like(dkg_ref)

        same = _same_head(W)
        qgv, kgv = qg_ref[...], kg_ref[...]
        i1 = lax.broadcasted_iota(jnp.int32, (B, B), 0)
        j1 = lax.broadcasted_iota(jnp.int32, (B, B), 1)
        i2 = lax.broadcasted_iota(jnp.int32, (2 * B, B), 0)
        j2 = lax.broadcasted_iota(jnp.int32, (2 * B, B), 1)

        def task(r, j):
            cur = j * group + r
            blk = n * m + j
            q_c, k_c, v_c = _rows(q_ref, cur, d), _rows(k_ref, cur, d), _rows(v_ref, cur, d)
            do_c, l_c, c_c = _rows(do_ref, cur, d), _rows(l_ref, cur, d), _rows(c_ref, cur, d)
            if j == 0:
                k_p, v_p = _rows(kp_ref, r, d), _rows(vp_ref, r, d)
            else:
                k_p, v_p = _rows(k_ref, cur - group, d), _rows(v_ref, cur - group, d)
            if j == m - 1:
                nxt = [_rows(ref, r, d) for ref in (qx_ref, dox_ref, lx_ref, cx_ref)]
            else:
                nxt = [_rows(ref, cur + group, d) for ref in (q_ref, do_ref, l_ref, c_ref)]
            q_x, do_x, l_x, c_x = nxt
            qn_c, q_r = _head_norm(q_c, qgv, same)
            kn_c, k_r = _head_norm(k_c, kgv, same)
            kn_p, _ = _head_norm(k_p, kgv, same)
            qn_x, _ = _head_norm(q_x, qgv, same)
            kn_c, kn_p, v_c, v_p = (a.astype(MXU_DTYPE) for a in (kn_c, kn_p, v_c, v_p))
            qn_c = qn_c.astype(MXU_DTYPE)
            qn_cat = jnp.concatenate([qn_c, qn_x.astype(MXU_DTYPE)], axis=0)
            do_cb = do_c.astype(MXU_DTYPE)
            do_cat = jnp.concatenate([do_cb, do_x.astype(MXU_DTYPE)], axis=0)
            l_cat = jnp.concatenate([l_c, l_x], axis=0)
            c_cat = jnp.concatenate([c_c, c_x], axis=0)
            mask_p = (j1 >= i1) & (blk > 0)
            mask_c = ((i2 < B) & (j2 <= i2)) | ((i2 >= B) & (j2 >= i2 - B) & (blk + 1 < n_blocks))
            dqn, dkn, dv = [], [], []
            for h, sl in enumerate(_head_slices()):
                lane = slice(h * HEAD_DIM, h * HEAD_DIM + 1)
                s_p = lax.dot_general(qn_c[:, sl], kn_p[:, sl], nt, preferred_element_type=F32) * scale
                pr_p = jnp.where(mask_p, jnp.exp(s_p - l_c[:, lane]), 0.0)
                dp_p = lax.dot_general(do_cb[:, sl], v_p[:, sl], nt, preferred_element_type=F32)
                ds_p = (pr_p * (dp_p + c_c[:, lane]) * scale).astype(MXU_DTYPE)
                s_c = lax.dot_general(qn_cat[:, sl], kn_c[:, sl], nt, preferred_element_type=F32) * scale
                pr_c = jnp.where(mask_c, jnp.exp(s_c - l_cat[:, lane]), 0.0)
                dp_c = lax.dot_general(do_cat[:, sl], v_c[:, sl], nt, preferred_element_type=F32)
                ds_c = (pr_c * (dp_c + c_cat[:, lane]) * scale).astype(MXU_DTYPE)
                dqn.append(jnp.dot(ds_p, kn_p[:, sl], preferred_element_type=F32)
                           + jnp.dot(ds_c[:B], kn_c[:, sl], preferred_element_type=F32))
                dkn.append(lax.dot_general(ds_c, qn_cat[:, sl], tn, preferred_element_type=F32))
                dv.append(lax.dot_general(pr_c.astype(MXU_DTYPE), do_cat[:, sl], tn, preferred_element_type=F32))
            dq, dqg_part = _head_norm_bwd(jnp.concatenate(dqn, axis=1), q_c, qgv, q_r, same)
            dk, dkg_part = _head_norm_bwd(jnp.concatenate(dkn, axis=1), k_c, kgv, k_r, same)
            _set_rows(dq_ref, cur, d, dq)
            _set_rows(dk_ref, cur, d, dk)
            _set_rows(dv_ref, cur, d, jnp.concatenate(dv, axis=1))
            dqg_ref[0:1, :] += jnp.sum(dqg_part, axis=0, keepdims=True)
            dkg_ref[0:1, :] += jnp.sum(dkg_part, axis=0, keepdims=True)

        _for_each_block(d, m, task, ATT_BWD_UNROLL)

    n_spans = T // span
    main = lambda c0: pl.BlockSpec((span, W), lambda hp, n: (n, c0 + hp))
    prev = lambda c0: pl.BlockSpec((group, W), lambda hp, n: (jnp.maximum(n * m - 1, 0), c0 + hp))
    nxt = lambda c0: pl.BlockSpec((group, W), lambda hp, n: (jnp.minimum((n + 1) * m, n_blocks - 1), c0 + hp))
    gain = pl.BlockSpec((1, W), lambda hp, n: (0, 0))
    acc = pl.BlockSpec((8, W), lambda hp, n: (0, 0))
    own = pl.BlockSpec((span, W), lambda hp, n: (n, hp))
    dq, dk, dv, dqg, dkg = pl.pallas_call(
        body, name=name, grid=(PW // W, n_spans),
        in_specs=[main(c_q), main(c_k), main(c_v), main(0), main(0), main(0), prev(c_k), prev(c_v),
                  nxt(c_q), nxt(0), nxt(0), nxt(0), gain, gain],
        out_specs=[own, own, own, acc, acc],
        out_shape=[_sds((T, PW), F32)] * 3 + [_sds((8, W), F32)] * 2,
        compiler_params=_cparams("arbitrary", "arbitrary"),
    )(p, p, p, do, lse, corr, p, p, p, do, lse, corr, qg, kg)
    return dq, dk, dv, dqg[0], dkg[0]


def _softmax3(lses):
    mx = jnp.maximum(jnp.maximum(lses[0], lses[1]), lses[2])
    ex = [jnp.exp(l - mx) for l in lses]
    inv = 1.0 / (ex[0] + ex[1] + ex[2])
    return [e * inv for e in ex]


def _mix_fwd(os_, lses, name, tr=512):
    T, PW = os_[0].shape

    def body(o0, o1, o2, l0, l1, l2, y_ref):
        alpha = _softmax3([l0[...], l1[...], l2[...]])
        for g, o_ref in enumerate((o0, o1, o2)):
            y_ref[:, g * PW:(g + 1) * PW] = (o_ref[...] * alpha[g]).astype(y_ref.dtype)

    blk = pl.BlockSpec((tr, PW), lambda i: (i, 0))
    return pl.pallas_call(
        body, name=name, grid=(T // tr,),
        in_specs=[blk] * 6, out_specs=pl.BlockSpec((tr, 3 * PW), lambda i: (i, 0)),
        out_shape=_sds((T, 3 * PW), MXU_DTYPE),
        compiler_params=_cparams("parallel"),
    )(*os_, *lses)


def _mix_bwd(dymix, os_, lses, c_start, name, tr=512):
    T, PW = os_[0].shape
    HP = PW // HEAD_DIM
    c0 = c_start // PW

    def body(d0, d1, d2, o0, o1, o2, l0, l1, l2, do0, do1, do2, dl0, dl1, dl2):
        col_head = lax.broadcasted_iota(jnp.int32, (tr, PW), 1) // HEAD_DIM
        alpha = _softmax3([l0[...], l1[...], l2[...]])
        dys = [d0[...], d1[...], d2[...]]
        dots = [_head_sum(dy * o_ref[...], col_head, HP) for dy, o_ref in zip(dys, (o0, o1, o2))]
        mean_dot = alpha[0] * dots[0] + alpha[1] * dots[1] + alpha[2] * dots[2]
        for g, (do_ref, dl_ref) in enumerate(((do0, dl0), (do1, dl1), (do2, dl2))):
            do_ref[...] = dys[g] * alpha[g]
            dl_ref[...] = -alpha[g] * mean_dot

    blk = pl.BlockSpec((tr, PW), lambda i: (i, 0))
    dy_specs = [pl.BlockSpec((tr, PW), lambda i, g=g: (i, c0 + g)) for g in range(3)]
    outs = pl.pallas_call(
        body, name=name, grid=(T // tr,),
        in_specs=dy_specs + [blk] * 6, out_specs=[blk] * 6,
        out_shape=[_sds((T, PW), F32)] * 6,
        compiler_params=_cparams("parallel"),
    )(dymix, dymix, dymix, *os_, *lses)
    return outs[:3], outs[3:]


def _adamw_math(w, g, m, v):
    m2 = ADAM_B1 * m + (1.0 - ADAM_B1) * g
    v2 = ADAM_B2 * v + (1.0 - ADAM_B2) * (g * g)
    m_hat = m2 / (1.0 - ADAM_B1 ** ADAM_STEP)
    v_hat = v2 / (1.0 - ADAM_B2 ** ADAM_STEP)
    delta = -ADAM_LR * (m_hat / (jnp.sqrt(v_hat) + ADAM_EPS) + ADAM_WD * w)
    return delta, m2, v2


def _adamw_layer(layer, w, m, v, own, landed, me, prev, name, tr=256):
    _, R, C = w.shape
    tr = min(tr, R)

    def body(me_ref, w_ref, m_ref, v_ref, own_ref, land_ref, *rest):
        g_ref, d_ref, m2_ref, v2_ref = rest[-4:]
        g = own_ref[...].astype(F32)
        for j in range(N_PEER):
            g = g + land_ref[j].astype(F32)
        delta, m2, v2 = _adamw_math(w_ref[...], g, m_ref[...], v_ref[...])
        g_ref[...] = g
        d_ref[...] = delta
        m2_ref[...] = m2
        v2_ref[...] = v2

    lay = pl.BlockSpec((None, tr, C), lambda i, me_ref: (layer, i, 0))
    in_specs = [lay, lay, lay, pl.BlockSpec((None, tr, C), lambda i, me_ref: (me_ref[0], i, 0)),
                pl.BlockSpec((N_PEER, tr, C), lambda i, me_ref: (0, i, 0))]
    args = [me, w, m, v, own, landed]
    aliases = {}
    if prev is not None:
        in_specs += [pl.BlockSpec(memory_space=pl.ANY)] * 4
        args += list(prev)
        aliases = {6 + i: i for i in range(4)}
    return pl.pallas_call(
        body, name=name,
        grid_spec=pltpu.PrefetchScalarGridSpec(num_scalar_prefetch=1, grid=(R // tr,), in_specs=in_specs, out_specs=[lay] * 4),
        out_shape=[_sds(w.shape, F32)] * 4,
        input_output_aliases=aliases,
        compiler_params=_cparams("parallel"),
    )(*args)


def _sum_parts(parts, name):
    _, R, C = parts.shape

    def body(p_ref, out_ref):
        g = p_ref[0]
        for j in range(1, N_DEV):
            g = g + p_ref[j]
        out_ref[...] = g

    return pl.pallas_call(
        body, name=name, grid=(1,),
        in_specs=[pl.BlockSpec((N_DEV, R, C), lambda i: (0, 0, 0))], out_specs=pl.BlockSpec((R, C), lambda i: (0, 0)),
        out_shape=_sds((R, C), F32), compiler_params=_cparams("arbitrary"),
    )(parts)


def _adamw_flat(w, g, m, v, name):
    R, C = w.shape

    def body(w_ref, g_ref, m_ref, v_ref, d_ref, m2_ref, v2_ref):
        delta, m2, v2 = _adamw_math(w_ref[...], g_ref[...], m_ref[...], v_ref[...])
        d_ref[...] = delta
        m2_ref[...] = m2
        v2_ref[...] = v2

    blk = pl.BlockSpec((R, C), lambda i: (0, 0))
    return pl.pallas_call(
        body, name=name, grid=(1,), in_specs=[blk] * 4, out_specs=[blk] * 3, out_shape=[_sds((R, C), F32)] * 3,
        compiler_params=_cparams("arbitrary"),
    )(w, g, m, v)


def _pack(arrays, rows_multiple=8):
    flat = []
    for a in arrays:
        a = a.reshape(-1).astype(F32)
        flat.append(jnp.pad(a, (0, (-a.shape[0]) % LANE)))
    flat = jnp.concatenate(flat)
    flat = jnp.pad(flat, (0, (-flat.shape[0]) % (LANE * rows_multiple)))
    return flat.reshape(-1, LANE)


def _unpack(packed, shapes):
    flat = packed.reshape(-1)
    out, off = [], 0
    for s in shapes:
        size = 1
        for dim in s:
            size *= dim
        out.append(flat[off:off + size].reshape(s))
        off += size + (-size) % LANE
    return out


def _layer_fwd(x, wts, getw, dims, dep=None):
    AW, BW, PW, DP = dims["AW"], dims["BW"], dims["PW"], dims["DP"]
    q_start = 2 * AW + 3 * BW
    h, r1 = _rmsnorm_fwd(x, wts["attn_norm"], "rmsnorm_fwd", dep=dep)
    p = _mm_nn(h, getw("w_in", h), "proj_in", tm=1024, tn=512)
    y_a = _sgu_fwd(p, wts["sgu_tril"], wts["sgu_bmat"], "sgu_fwd")
    y_b = _conv_fwd(p, getw("conv_w", y_a), AW, "conv_fwd")
    os_, lses = [], []
    for g, d in enumerate(DILATIONS):
        o, lse = _attn_fwd(p, wts["q_gain"], wts["k_gain"], g, d, DP, PW, q_start, "attn_fwd_%d" % d)
        os_.append(o)
        lses.append(lse)
    y_c = _mix_fwd(os_, lses, "mix_fwd")
    ymix = jnp.concatenate([y_a, y_b, y_c], axis=1)
    x1 = _mm_nn(ymix, getw("w_out", ymix), "proj_out", residual=x, tm=1024, tn=1024)
    h2, r2 = _rmsnorm_fwd(x1, wts["mlp_norm"], "rmsnorm_fwd")
    a, hid = _mm_nn(h2, getw("w_mlp_in", h2), "mlp_in", out_dtype=MXU_DTYPE, relu2=True, tm=1024, tn=1024)
    x2 = _mm_nn(hid, getw("w_mlp_out", hid), "mlp_out", residual=x1, columns_first=True, tm=512, tn=1024)
    saved = dict(x=x, h=h, r1=r1, p=p, os=os_, lses=lses, ymix=ymix, x1=x1, h2=h2, r2=r2, a=a, hid=hid)
    return x2, saved


def _layer_bwd(dx, dxb, wts, getw, scatter, saved, dims):
    AW, BW, PW, DP = dims["AW"], dims["BW"], dims["PW"], dims["DP"]
    q_start = 2 * AW + 3 * BW
    D = dx.shape[1]
    g_w2 = _wgrad_wide_a(saved["hid"], dxb, "mlp_out_wgrad")
    token = scatter("w_mlp_out", g_w2.reshape(N_DEV, -1, D))
    da = _mm_nt(dxb, getw("w_mlp_out", None), "mlp_out_dgrad", out_dtype=MXU_DTYPE, relu2_pre=saved["a"], dep=token,
                tm=1024, tn=1024)
    g_w1 = _wgrad_wide_b(saved["h2"], da, "mlp_in_wgrad", groups=N_DEV)
    token = scatter("w_mlp_in", g_w1)
    dh2 = _mm_nt(da, getw("w_mlp_in", None), "mlp_in_dgrad", dep=token, a_single=True, tm=1024, tn=512)
    dx1, dx1b, g_mlp_norm = _rmsnorm_bwd(dh2, saved["x1"], wts["mlp_norm"], saved["r2"], dx, "rmsnorm_bwd")
    g_wout = _wgrad_wide_b(saved["ymix"], dx1b, "proj_out_wgrad")
    token = scatter("w_out", g_wout.reshape(N_DEV, -1, D))
    dymix = _mm_nt(dx1b, getw("w_out", None), "proj_out_dgrad", dep=token, tm=1024, tn=1024)
    p = saved["p"]
    du, dv, g_sgu_w, g_sgu_bmat = _sgu_bwd(dymix, p, wts["sgu_tril"], wts["sgu_tril_t"], wts["sgu_bmat"], "sgu_bwd")
    d_b, d_c, d_xb, g_conv = _conv_bwd(dymix, p, getw("conv_w", None), AW, "conv_bwd")
    dos, corrs = _mix_bwd(dymix, saved["os"], saved["lses"], AW + BW, "mix_bwd")
    dqs, dks, dvs = [], [], []
    g_q = g_k = 0.0
    for g, d in enumerate(DILATIONS):
        dq, dk, dvv, dqg, dkg = _attn_bwd(p, wts["q_gain"], wts["k_gain"], saved["lses"][g], dos[g], corrs[g],
                                          g, d, PW, q_start, "attn_bwd_%d" % d)
        dqs.append(dq.astype(MXU_DTYPE))
        dks.append(dk.astype(MXU_DTYPE))
        dvs.append(dvv.astype(MXU_DTYPE))
        g_q = g_q + dqg.reshape(-1, HEAD_DIM).sum(0)
        g_k = g_k + dkg.reshape(-1, HEAD_DIM).sum(0)
    dp = jnp.concatenate([du, dv, d_b, d_c, d_xb] + dqs + dks + dvs, axis=1)
    g_win = _wgrad_wide_b(saved["h"], dp, "proj_in_wgrad")
    token = scatter("w_in", g_win.reshape(D, N_DEV, DP // N_DEV).transpose(1, 0, 2))
    dh = _mm_nt(dp, getw("w_in", None), "proj_in_dgrad", dep=token, tm=1024, tn=512)
    dx0, dx0b, g_attn_norm = _rmsnorm_bwd(dh, saved["x"], wts["attn_norm"], saved["r1"], dx1, "rmsnorm_bwd")
    H = AW // HEAD_DIM
    tril = jnp.tril(jnp.ones((CHUNK, CHUNK), F32))
    small = [g_attn_norm.reshape(-1), g_sgu_w * tril, g_sgu_bmat.reshape(CHUNK, H, HEAD_DIM).sum(-1).T,
             g_conv, g_q, g_k, g_mlp_norm.reshape(-1)]
    return dx0, dx0b, small


def kernel(x, attn_norm, w_in, sgu_w, sgu_b, conv_w, q_norm, k_norm, w_out, mlp_norm, w_mlp_in, w_mlp_out, loss_target, m_attn_norm, m_w_in, m_sgu_w, m_sgu_b, m_conv_w, m_q_norm, m_k_norm, m_w_out, m_mlp_norm, m_w_mlp_in, m_w_mlp_out, v_attn_norm, v_w_in, v_sgu_w, v_sgu_b, v_conv_w, v_q_norm, v_k_norm, v_w_out, v_mlp_norm, v_w_mlp_in, v_w_mlp_out):
    n_layers = attn_norm.shape[0]
    T, D = x.shape[1], x.shape[2]
    H = sgu_w.shape[1]
    AW = H * HEAD_DIM
    BW = conv_w.shape[2] * N_DEV
    DP = w_in.shape[2] * N_DEV
    DMIX = w_out.shape[1] * N_DEV
    DFF = w_mlp_in.shape[2] * N_DEV
    PW = (DMIX - AW - BW) // 3
    HP = PW // HEAD_DIM
    dims = dict(AW=AW, BW=BW, PW=PW, DP=DP)
    me = 4 * lax.axis_index("x") + 2 * lax.axis_index("y") + lax.axis_index("c")

    big_names = ("w_in", "w_out", "w_mlp_in", "w_mlp_out")
    big_w = dict(zip(big_names, (w_in, w_out, w_mlp_in, w_mlp_out)))
    big_m = dict(zip(big_names, (m_w_in, m_w_out, m_w_mlp_in, m_w_mlp_out)))
    big_v = dict(zip(big_names, (v_w_in, v_w_out, v_w_mlp_in, v_w_mlp_out)))

    keys = []
    for l in range(n_layers):
        keys += [(l, nm) for nm in big_names]
    keys.insert(1, (0, "conv_w"))
    srcs = [_pack([conv_w]) if nm == "conv_w" else big_w[nm][l].astype(MXU_DTYPE) for l, nm in keys]
    flights, gather_token = _exchange_start(srcs, [_own_in_place(s, me) for s in srcs], "gather", name="gather_start")
    arriving = dict(zip(keys, flights))
    forwarding = {}
    relayout = dict(
        w_in=lambda g: g.transpose(1, 0, 2).reshape(D, DP), w_out=lambda g: g.reshape(DMIX, D),
        w_mlp_in=lambda g: g, w_mlp_out=lambda g: g.reshape(DFF, D),
        conv_w=lambda g: jnp.stack([_unpack(g[j], [conv_w.shape])[0] for j in range(N_DEV)], axis=2).reshape(
            n_layers, CONV_WIDTH, BW))
    gathered = {}

    def forward(key, after):
        _, land = _exchange_wait(arriving[key], after, "gather", name="gather_arrive_%d_%s" % key)
        fl, token = _exchange_start(None, [land], "forward", name="gather_forward_%d_%s" % key)
        forwarding[key] = fl[0]
        return token

    def weight_getter(l):
        def getw(nm, after):
            key = (0, nm) if nm == "conv_w" else (l, nm)
            if key not in gathered:
                ahead = keys[keys.index(key):][:2]
                for k in ahead:
                    if k not in forwarding:
                        after = forward(k, after)
                _, land = _exchange_wait(forwarding[key], after, "forward", name="gather_wait_%d_%s" % key)
                gathered[key] = relayout[nm](land)
            return gathered[key][l] if nm == "conv_w" else gathered[key]
        return getw

    tril = jnp.tril(jnp.ones((CHUNK, CHUNK), F32))
    layers = []
    for l in range(n_layers):
        w_tril = sgu_w[l] * tril
        layers.append(dict(
            attn_norm=attn_norm[l][None], mlp_norm=mlp_norm[l][None],
            sgu_tril=w_tril.astype(MXU_DTYPE), sgu_tril_t=w_tril.transpose(0, 2, 1).astype(MXU_DTYPE),
            sgu_bmat=jnp.repeat(sgu_b[l].T, HEAD_DIM, axis=1),
            q_gain=jnp.tile(q_norm[l], HEADS_PER_LANES)[None], k_gain=jnp.tile(k_norm[l], HEADS_PER_LANES)[None]))

    xs = x[0]
    saved = []
    for l in range(n_layers):
        xs, sv = _layer_fwd(xs, layers[l], weight_getter(l), dims, dep=gather_token if l == 0 else None)
        saved.append(sv)
    loss_blk, dx, dxb = _loss_and_grad(xs, loss_target[0], "loss")
    loss = lax.psum(loss_blk[0, 0], ("x", "y", "c"))

    scattering = {}

    def scatter_starter(l):
        def scatter(nm, partials):
            land = lax.empty((N_PEER,) + partials.shape[1:], partials.dtype)
            fl, tok = _exchange_start([partials], [land], "scatter", name="scatter_start_%d_%s" % (l, nm))
            scattering[(l, nm)] = fl[0]
            return tok
        return scatter

    small = [None] * n_layers
    for l in reversed(range(n_layers)):
        dx, dxb, small[l] = _layer_bwd(dx, dxb, layers[l], weight_getter(l), scatter_starter(l), saved[l], dims)

    small_shapes = [s.shape for s in small[0]]
    small_src = [_pack([s for l in range(n_layers) for s in small[l]])]
    small_flights, small_token = _exchange_start(small_src, [_own_in_place(s, me) for s in small_src], "gather_all",
                                                 name="small_start")
    grad_x = dx[None]

    me1 = me.astype(jnp.int32).reshape(1)
    res = {nm: None for nm in big_names}
    after = small_token
    for l in reversed(range(n_layers)):
        for nm in reversed(big_names):
            own, landed = _exchange_wait(scattering[(l, nm)], after, "scatter", name="scatter_wait_%d_%s" % (l, nm))
            res[nm] = _adamw_layer(l, big_w[nm], big_m[nm], big_v[nm], own, landed, me1, res[nm], "adamw_" + nm)
            after = res[nm][0]
    big_out = [res[nm] for nm in big_names]

    _, gathered_small = _exchange_wait(small_flights[0], after, "gather_all", name="small_wait")
    summed = _unpack(_sum_parts(gathered_small, "sum_small"), small_shapes * n_layers)
    ns = len(small_shapes)
    g_small = [jnp.stack([summed[l * ns + i] for l in range(n_layers)]) for i in range(ns)]
    g_attn_norm, g_sgu_w, g_sgu_b, g_conv_full, g_q, g_k, g_mlp_norm = g_small
    cs = conv_w.shape[2]
    g_conv = lax.dynamic_slice_in_dim(g_conv_full, me * cs, cs, axis=2)
    sm_w = (attn_norm, sgu_w, sgu_b, conv_w, q_norm, k_norm, mlp_norm)
    sm_m = (m_attn_norm, m_sgu_w, m_sgu_b, m_conv_w, m_q_norm, m_k_norm, m_mlp_norm)
    sm_v = (v_attn_norm, v_sgu_w, v_sgu_b, v_conv_w, v_q_norm, v_k_norm, v_mlp_norm)
    sm_g = (g_attn_norm, g_sgu_w, g_sgu_b, g_conv, g_q, g_k, g_mlp_norm)
    sm_res = _adamw_flat(_pack(sm_w), _pack(sm_g), _pack(sm_m), _pack(sm_v), "adamw_small")
    shapes = [w.shape for w in sm_w]
    sm_delta, sm_m2, sm_v2 = (_unpack(r, shapes) for r in sm_res)

    def ordered(small_list, big_kind):
        b = [big_out[i][big_kind] for i in range(4)]
        return [small_list[0], b[0], small_list[1], small_list[2], small_list[3], small_list[4], small_list[5],
                b[1], small_list[6], b[2], b[3]]

    return (loss, grad_x, *ordered(list(sm_g), 0), *ordered(sm_delta, 1), *ordered(sm_m2, 2), *ordered(sm_v2, 3))
```

```python
import jax
import jax.numpy as jnp
from jax import lax
from jax.experimental import pallas as pl
from jax.experimental.pallas import tpu as pltpu

N_DEV = 8
HEAD_DIM = 64
CHUNK = 128
ATT_BLK = 128
DILATIONS = (1, 4, 16)
CONV_WIDTH = 3
EPS = 1e-6
ADAM_LR = 0.001
ADAM_B1 = 0.9
ADAM_B2 = 0.999
ADAM_EPS = 1e-08
ADAM_WD = 0.01
ADAM_STEP = 10
MXU_DTYPE = jnp.bfloat16
F32 = jnp.float32
LANE = 128
VMEM_LIMIT_BYTES = 56 * 1024 * 1024
NEG_INF = float("-inf")


def _cparams(*sem):
    return pltpu.CompilerParams(dimension_semantics=sem, vmem_limit_bytes=VMEM_LIMIT_BYTES)


def _sds(shape, dtype):
    return jax.ShapeDtypeStruct(shape, dtype)


def _fit(n, tile):
    for t in range(min(tile, n) // LANE * LANE, 0, -LANE):
        if n % t == 0:
            return t
    return n


_HBM = pl.BlockSpec(memory_space=pltpu.HBM)
_SEM = pl.BlockSpec(memory_space=pltpu.SEMAPHORE)
_DATAFLOW = pltpu.SideEffectType.DATAFLOW_SIDE_EFFECTING
N_PEER = N_DEV - 1


def _mesh_pos():
    x, y, c = lax.axis_index("x"), lax.axis_index("y"), lax.axis_index("c")
    return x, y, c, 4 * x + 2 * y + c


OTHER_CHIPS = (4, 2, 6)
EXCHANGE_PEERS = dict(
    scatter=tuple(range(1, N_DEV)),
    gather_all=tuple(range(1, N_DEV)),
    gather=(1,) + OTHER_CHIPS,
    forward=OTHER_CHIPS)


def _remote_copies(src, land, send_sems, recv_sems, mode):
    x, y, c, me = _mesh_pos()
    copies = []
    for i, k in enumerate(EXCHANGE_PEERS[mode]):
        px = (1 - x) if (k & 4) else x
        py = (1 - y) if (k & 2) else y
        pc = (1 - c) if (k & 1) else c
        if mode == "scatter":
            src_ref, dst_ref, dev = src.at[4 * px + 2 * py + pc], land.at[i], (px, py, pc)
        elif mode == "forward":
            slot = 4 * px + 2 * py + c
            src_ref, dst_ref, dev = land.at[slot], land.at[slot], (x, y, 1 - c)
        else:
            src_ref, dst_ref, dev = src, land.at[me], (px, py, pc)
        copies.append(pltpu.make_async_remote_copy(
            src_ref=src_ref, dst_ref=dst_ref, send_sem=send_sems.at[i], recv_sem=recv_sems.at[i],
            device_id=dev, device_id_type=pl.DeviceIdType.MESH))
    return copies


def _own_in_place(src, me):
    land = lax.empty((N_DEV,) + src.shape, src.dtype)
    return lax.dynamic_update_slice(land, src[None], (me,) + (0,) * src.ndim)


def _exchange_start(srcs, lands, mode, name):
    n = len(lands)
    has_src = srcs is not None
    arrays = (list(srcs) if has_src else []) + list(lands)
    n_arr = len(arrays)
    n_copies = len(EXCHANGE_PEERS[mode])

    def body(*refs):
        src = refs[:n] if has_src else [None] * n
        land = refs[n_arr - n:n_arr]
        send, recv = refs[n_arr:n_arr + n], refs[n_arr + n:n_arr + 2 * n]
        token = refs[2 * n_arr + 2 * n]
        for t in range(n):
            for cp in _remote_copies(src[t], land[t], send[t], recv[t], mode):
                cp.start()
        token[...] = jnp.zeros_like(token)

    outs = pl.pallas_call(
        body, name=name,
        out_shape=([pltpu.SemaphoreType.DMA((n_copies,))] * (2 * n) + [pltpu.HBM(a.shape, a.dtype) for a in arrays]
                   + [_sds((8, LANE), F32)]),
        in_specs=[_HBM] * n_arr,
        out_specs=[_SEM] * (2 * n) + [_HBM] * n_arr + [pl.BlockSpec(memory_space=pltpu.VMEM)],
        input_output_aliases={i: 2 * n + i for i in range(n_arr)},
        compiler_params=pltpu.CompilerParams(has_side_effects=_DATAFLOW),
    )(*[pltpu.with_memory_space_constraint(a, pltpu.HBM) for a in arrays])
    thru = outs[2 * n:2 * n + n_arr]
    flights = [(outs[t], outs[n + t], thru[t] if has_src else None, thru[n_arr - n + t]) for t in range(n)]
    return flights, outs[2 * n + n_arr]


def _exchange_wait(flight, after, mode, name):
    send, recv, src, land = flight
    arrays = [land] if src is None else [src, land]
    n_arr = len(arrays)

    def body(*refs):
        src_ref = refs[0] if n_arr == 2 else None
        land_ref, send_ref, recv_ref = refs[n_arr - 1], refs[n_arr], refs[n_arr + 1]
        for cp in _remote_copies(src_ref, land_ref, send_ref, recv_ref, mode):
            cp.wait_send()
            cp.wait_recv()

    outs = pl.pallas_call(
        body, name=name, out_shape=[pltpu.HBM(a.shape, a.dtype) for a in arrays],
        in_specs=[_HBM] * n_arr + [_SEM, _SEM, pl.BlockSpec(memory_space=pl.ANY)], out_specs=[_HBM] * n_arr,
        input_output_aliases={i: i for i in range(n_arr)},
        compiler_params=pltpu.CompilerParams(has_side_effects=_DATAFLOW),
    )(*arrays, send, recv, after)
    return (None, outs[0]) if src is None else (outs[0], outs[1])


def _rmsnorm_fwd(x, g, name, dep=None, tr=512):
    T, D = x.shape

    def body(x_ref, g_ref, *rest):
        h_ref, r_ref = rest[-2:]
        xv = x_ref[...]
        r = lax.rsqrt(jnp.mean(xv * xv, axis=-1, keepdims=True) + EPS)
        h_ref[...] = (xv * r * g_ref[...]).astype(h_ref.dtype)
        r_ref[...] = r

    in_specs = [pl.BlockSpec((tr, D), lambda i: (i, 0)), pl.BlockSpec((1, D), lambda i: (0, 0))]
    args = [x, g]
    if dep is not None:
        in_specs.append(pl.BlockSpec(dep.shape, lambda i: (0, 0)))
        args.append(dep)
    return pl.pallas_call(
        body, name=name, grid=(T // tr,),
        in_specs=in_specs,
        out_specs=[pl.BlockSpec((tr, D), lambda i: (i, 0)), pl.BlockSpec((tr, 1), lambda i: (i, 0))],
        out_shape=[_sds((T, D), MXU_DTYPE), _sds((T, 1), F32)],
        compiler_params=_cparams("parallel"),
    )(*args)


def _rmsnorm_bwd(dh, x, g, r, dres, name, tr=256):
    T, D = x.shape

    def body(dh_ref, x_ref, g_ref, r_ref, dres_ref, dx_ref, dxb_ref, dg_ref):
        @pl.when(pl.program_id(0) == 0)
        def _():
            dg_ref[...] = jnp.zeros_like(dg_ref)

        dh_v, xv, rv = dh_ref[...], x_ref[...], r_ref[...]
        gdy = dh_v * g_ref[...]
        mean_xg = jnp.mean(xv * gdy, axis=-1, keepdims=True)
        dx = dres_ref[...] + rv * gdy - xv * (rv * rv * rv) * mean_xg
        dx_ref[...] = dx
        dxb_ref[...] = dx.astype(dxb_ref.dtype)
        dg_ref[...] += jnp.sum(dh_v * xv * rv, axis=0, keepdims=True)

    row = lambda i: (i, 0)
    return pl.pallas_call(
        body, name=name, grid=(T // tr,),
        in_specs=[pl.BlockSpec((tr, D), row), pl.BlockSpec((tr, D), row), pl.BlockSpec((1, D), lambda i: (0, 0)),
                  pl.BlockSpec((tr, 1), row), pl.BlockSpec((tr, D), row)],
        out_specs=[pl.BlockSpec((tr, D), row), pl.BlockSpec((tr, D), row), pl.BlockSpec((1, D), lambda i: (0, 0))],
        out_shape=[_sds((T, D), F32), _sds((T, D), MXU_DTYPE), _sds((1, D), F32)],
        compiler_params=_cparams("arbitrary"),
    )(dh, x, g, r, dres)


def _mm_nn(a, b, name, out_dtype=F32, residual=None, relu2=False, dep=None, columns_first=False, tm=512, tn=512):
    M, K = a.shape
    grouped = b.ndim == 3
    N = b.shape[0] * b.shape[2] if grouped else b.shape[1]
    tm, tn = _fit(M, tm), _fit(b.shape[2] if grouped else N, tn)
    tile = (lambda j, i: (i, j)) if columns_first else (lambda i, j: (i, j))
    b_mode = pl.Buffered(1 if columns_first else 2)
    if grouped:
        per = b.shape[2] // tn
        b_spec = pl.BlockSpec((None, K, tn), lambda *g: (tile(*g)[1] // per, 0, tile(*g)[1] % per), pipeline_mode=b_mode)
    else:
        b_spec = pl.BlockSpec((K, tn), lambda *g: (0, tile(*g)[1]), pipeline_mode=b_mode)
    n_out = 2 if relu2 else 1

    def body(*refs):
        a_ref, b_ref = refs[0], refs[1]
        r_ref = refs[2] if residual is not None else None
        outs = refs[2 + (residual is not None) + (dep is not None):]
        acc = jnp.dot(a_ref[...], b_ref[...], preferred_element_type=F32)
        if r_ref is not None:
            acc = acc + r_ref[...]
        outs[0][...] = acc.astype(outs[0].dtype)
        if relu2:
            rl = jnp.maximum(acc, 0.0)
            outs[1][...] = (rl * rl).astype(outs[1].dtype)

    out_blk = pl.BlockSpec((tm, tn), lambda *g: tile(*g))
    in_specs = [pl.BlockSpec((tm, K), lambda *g: (tile(*g)[0], 0)), b_spec]
    args = [a, b]
    if residual is not None:
        in_specs.append(out_blk)
        args.append(residual)
    if dep is not None:
        in_specs.append(pl.BlockSpec(dep.shape, lambda *g: (0, 0)))
        args.append(dep)
    out_shape = [_sds((M, N), out_dtype)]
    if relu2:
        out_shape.append(_sds((M, N), MXU_DTYPE))
    outs = pl.pallas_call(
        body, name=name, grid=(N // tn, M // tm) if columns_first else (M // tm, N // tn),
        in_specs=in_specs, out_specs=[out_blk] * n_out, out_shape=out_shape,
        compiler_params=_cparams("parallel", "parallel"),
    )(*args)
    return outs if relu2 else outs[0]


def _mm_nt(a, b, name, out_dtype=F32, relu2_pre=None, dep=None, a_single=False, tm=512, tn=512):
    M, K = a.shape
    grouped = b.ndim == 3
    N = b.shape[1] if grouped else b.shape[0]
    tm, tn = _fit(M, tm), _fit(N, tn)
    nt = (((1,), (1,)), ((), ()))
    if grouped:
        G, _, Kg = b.shape
        b_spec = pl.BlockSpec((G, tn, Kg), lambda i, j: (0, j, 0))
    else:
        b_spec = pl.BlockSpec((tn, K), lambda i, j: (j, 0))

    def body(*refs):
        a_ref, b_ref = refs[0], refs[1]
        p_ref = refs[2] if relu2_pre is not None else None
        out_ref = refs[2 + (relu2_pre is not None) + (dep is not None)]
        if grouped:
            acc = lax.dot_general(a_ref[:, 0:Kg], b_ref[0], nt, preferred_element_type=F32)
            for g in range(1, G):
                acc += lax.dot_general(a_ref[:, g * Kg:(g + 1) * Kg], b_ref[g], nt, preferred_element_type=F32)
        else:
            acc = lax.dot_general(a_ref[...], b_ref[...], nt, preferred_element_type=F32)
        if p_ref is not None:
            acc = acc * (2.0 * jnp.maximum(p_ref[...].astype(F32), 0.0))
        out_ref[...] = acc.astype(out_ref.dtype)

    out_blk = pl.BlockSpec((tm, tn), lambda i, j: (i, j))
    in_specs = [pl.BlockSpec((tm, K), lambda i, j: (i, 0), pipeline_mode=pl.Buffered(1 if a_single else 2)), b_spec]
    args = [a, b]
    if relu2_pre is not None:
        in_specs.append(out_blk)
        args.append(relu2_pre)
    if dep is not None:
        in_specs.append(pl.BlockSpec(dep.shape, lambda i, j: (0, 0)))
        args.append(dep)
    return pl.pallas_call(
        body, name=name, grid=(M // tm, N // tn),
        in_specs=in_specs, out_specs=out_blk, out_shape=_sds((M, N), out_dtype),
        compiler_params=_cparams("parallel", "parallel"),
    )(*args)


def _wgrad_wide_a(a, b, name, tm=512):
    T, M = a.shape
    N = b.shape[1]
    tm = _fit(M, tm)

    def body(a_ref, b_ref, out_ref):
        out_ref[...] = lax.dot_general(a_ref[...], b_ref[...], (((0,), (0,)), ((), ())),
                                       preferred_element_type=F32).astype(out_ref.dtype)

    return pl.pallas_call(
        body, name=name, grid=(M // tm,),
        in_specs=[pl.BlockSpec((T, tm), lambda i: (0, i)),
                  pl.BlockSpec((T, N), lambda i: (0, 0), pipeline_mode=pl.Buffered(1))],
        out_specs=pl.BlockSpec((tm, N), lambda i: (i, 0)), out_shape=_sds((M, N), MXU_DTYPE),
        compiler_params=_cparams("parallel"),
    )(a, b)


def _wgrad_wide_b(a, b, name, groups=None, tn=512, t_chunk=512):
    T, M = a.shape
    N = b.shape[1]
    tn = _fit(N if groups is None else N // groups, tn)
    t_chunk = _fit(T, t_chunk)

    def body(a_ref, b_ref, out_ref, at_ref):
        @pl.when(pl.program_id(0) == 0)
        def _():
            for c in range(0, T, t_chunk):
                at_ref[:, c:c + t_chunk] = a_ref[c:c + t_chunk, :].T

        out_ref[...] = jnp.dot(at_ref[...], b_ref[...], preferred_element_type=F32).astype(out_ref.dtype)

    if groups is None:
        out_spec = pl.BlockSpec((M, tn), lambda j: (0, j))
        out_shape = _sds((M, N), MXU_DTYPE)
    else:
        per = N // groups // tn
        out_spec = pl.BlockSpec((None, M, tn), lambda j: (j // per, 0, j % per))
        out_shape = _sds((groups, M, N // groups), MXU_DTYPE)
    return pl.pallas_call(
        body, name=name, grid=(N // tn,),
        in_specs=[pl.BlockSpec((T, M), lambda j: (0, 0), pipeline_mode=pl.Buffered(1)),
                  pl.BlockSpec((T, tn), lambda j: (0, j))],
        out_specs=out_spec, out_shape=out_shape,
        scratch_shapes=[pltpu.VMEM((M, T), MXU_DTYPE)],
        compiler_params=_cparams("arbitrary"),
    )(a, b)


def _loss_and_grad(y, target, name, tr=512):
    T, D = y.shape

    def body(y_ref, t_ref, loss_ref, dx_ref, dxb_ref):
        @pl.when(pl.program_id(0) == 0)
        def _():
            loss_ref[...] = jnp.zeros_like(loss_ref)

        err = y_ref[...] - t_ref[...]
        loss_ref[...] += 0.5 * jnp.sum(jnp.mean(err * err, axis=-1, keepdims=True), axis=0, keepdims=True)
        dx = err * (1.0 / D)
        dx_ref[...] = dx
        dxb_ref[...] = dx.astype(dxb_ref.dtype)

    row = lambda i: (i, 0)
    return pl.pallas_call(
        body, name=name, grid=(T // tr,),
        in_specs=[pl.BlockSpec((tr, D), row), pl.BlockSpec((tr, D), row)],
        out_specs=[pl.BlockSpec((8, LANE), lambda i: (0, 0)), pl.BlockSpec((tr, D), row), pl.BlockSpec((tr, D), row)],
        out_shape=[_sds((8, LANE), F32), _sds((T, D), F32), _sds((T, D), MXU_DTYPE)],
        compiler_params=_cparams("arbitrary"),
    )(y, target)


def _sgu_mixed(v, w_ref, b_ref, col_head, n_heads):
    mixed = b_ref[...]
    for h in range(n_heads):
        full = jnp.dot(w_ref[h], v, preferred_element_type=F32)
        mixed = mixed + jnp.where(col_head == h, full, 0.0)
    return mixed


def _sgu_fwd(p, w_tril, bmat, name):
    T = p.shape[0]
    H = w_tril.shape[0]
    AW = H * HEAD_DIM

    def body(u_ref, v_ref, w_ref, b_ref, y_ref):
        col_head = lax.broadcasted_iota(jnp.int32, (CHUNK, AW), 1) // HEAD_DIM
        mixed = _sgu_mixed(v_ref[...].astype(MXU_DTYPE), w_ref, b_ref, col_head, H)
        y_ref[...] = (u_ref[...] * mixed).astype(y_ref.dtype)

    const3 = lambda c: (0, 0, 0)
    return pl.pallas_call(
        body, name=name, grid=(T // CHUNK,),
        in_specs=[pl.BlockSpec((CHUNK, AW), lambda c: (c, 0)), pl.BlockSpec((CHUNK, AW), lambda c: (c, 1)),
                  pl.BlockSpec((H, CHUNK, CHUNK), const3), pl.BlockSpec((CHUNK, AW), lambda c: (0, 0))],
        out_specs=pl.BlockSpec((CHUNK, AW), lambda c: (c, 0)),
        out_shape=_sds((T, AW), MXU_DTYPE),
        compiler_params=_cparams("parallel"),
    )(p, p, w_tril, bmat)


def _sgu_bwd(dymix, p, w_tril, w_tril_t, bmat, name):
    T = p.shape[0]
    H = w_tril.shape[0]
    AW = H * HEAD_DIM

    def body(dy_ref, u_ref, v_ref, w_ref, wt_ref, b_ref, du_ref, dv_ref, dw_ref, db_ref):
        @pl.when(pl.program_id(0) == 0)
        def _():
            dw_ref[...] = jnp.zeros_like(dw_ref)
            db_ref[...] = jnp.zeros_like(db_ref)

        col_head = lax.broadcasted_iota(jnp.int32, (CHUNK, AW), 1) // HEAD_DIM
        v = v_ref[...].astype(MXU_DTYPE)
        dy = dy_ref[...]
        mixed = _sgu_mixed(v, w_ref, b_ref, col_head, H)
        du_ref[...] = (dy * mixed).astype(du_ref.dtype)
        dm = dy * u_ref[...]
        db_ref[...] += dm
        dm_c = dm.astype(MXU_DTYPE)
        dv = jnp.zeros((CHUNK, AW), F32)
        for h in range(H):
            sel = col_head == h
            dv = dv + jnp.where(sel, jnp.dot(wt_ref[h], dm_c, preferred_element_type=F32), 0.0)
            dm_h = jnp.where(sel, dm, 0.0).astype(MXU_DTYPE)
            dw_ref[h] += lax.dot_general(dm_h, v, (((1,), (1,)), ((), ())), preferred_element_type=F32)
        dv_ref[...] = dv.astype(dv_ref.dtype)

    const3 = lambda c: (0, 0, 0)
    blk = pl.BlockSpec((CHUNK, AW), lambda c: (c, 0))
    return pl.pallas_call(
        body, name=name, grid=(T // CHUNK,),
        in_specs=[blk, blk, pl.BlockSpec((CHUNK, AW), lambda c: (c, 1)),
                  pl.BlockSpec((H, CHUNK, CHUNK), const3), pl.BlockSpec((H, CHUNK, CHUNK), const3),
                  pl.BlockSpec((CHUNK, AW), lambda c: (0, 0))],
        out_specs=[blk, blk, pl.BlockSpec((H, CHUNK, CHUNK), const3), pl.BlockSpec((CHUNK, AW), lambda c: (0, 0))],
        out_shape=[_sds((T, AW), MXU_DTYPE), _sds((T, AW), MXU_DTYPE), _sds((H, CHUNK, CHUNK), F32), _sds((CHUNK, AW), F32)],
        compiler_params=_cparams("arbitrary"),
    )(dymix, p, p, w_tril, w_tril_t, bmat)


def _shift_down(z, s, row):
    return jnp.where(row >= s, pltpu.roll(z, s, 0), 0.0)


def _shift_up(z, s, row, T):
    return jnp.where(row < T - s, pltpu.roll(z, T - s, 0), 0.0)


def _conv_fwd(p, w_conv, AW, name):
    T = p.shape[0]
    BW = w_conv.shape[1]
    nb = BW // LANE
    b0 = 2 * AW // LANE

    def body(b_ref, c_ref, x_ref, w_ref, y_ref):
        row = lax.broadcasted_iota(jnp.int32, (T, LANE), 0)
        z = c_ref[...] * x_ref[...]
        w0, w1, w2 = w_ref[0:1, :], w_ref[1:2, :], w_ref[2:3, :]
        conv = w2 * z + w1 * _shift_down(z, 1, row) + w0 * _shift_down(z, 2, row)
        y_ref[...] = (b_ref[...] * conv).astype(y_ref.dtype)

    return pl.pallas_call(
        body, name=name, grid=(nb,),
        in_specs=[pl.BlockSpec((T, LANE), lambda j: (0, b0 + j)), pl.BlockSpec((T, LANE), lambda j: (0, b0 + nb + j)),
                  pl.BlockSpec((T, LANE), lambda j: (0, b0 + 2 * nb + j)), pl.BlockSpec((CONV_WIDTH, LANE), lambda j: (0, j))],
        out_specs=pl.BlockSpec((T, LANE), lambda j: (0, j)),
        out_shape=_sds((T, BW), MXU_DTYPE),
        compiler_params=_cparams("parallel"),
    )(p, p, p, w_conv)


def _conv_bwd(dymix, p, w_conv, AW, name):
    T = p.shape[0]
    BW = w_conv.shape[1]
    nb = BW // LANE
    b0 = 2 * AW // LANE
    y0 = AW // LANE

    def body(dy_ref, b_ref, c_ref, x_ref, w_ref, db_ref, dc_ref, dxb_ref, dw_ref):
        row = lax.broadcasted_iota(jnp.int32, (T, LANE), 0)
        cv, xv, dy = c_ref[...], x_ref[...], dy_ref[...]
        w0, w1, w2 = w_ref[0:1, :], w_ref[1:2, :], w_ref[2:3, :]
        z = cv * xv
        z1 = _shift_down(z, 1, row)
        z2 = _shift_down(z, 2, row)
        conv = w2 * z + w1 * z1 + w0 * z2
        db_ref[...] = (dy * conv).astype(db_ref.dtype)
        dconv = dy * b_ref[...]
        dz = w2 * dconv + w1 * _shift_up(dconv, 1, row, T) + w0 * _shift_up(dconv, 2, row, T)
        dc_ref[...] = (dz * xv).astype(dc_ref.dtype)
        dxb_ref[...] = (dz * cv).astype(dxb_ref.dtype)
        dw_ref[0:1, :] = jnp.sum(dconv * z2, axis=0, keepdims=True)
        dw_ref[1:2, :] = jnp.sum(dconv * z1, axis=0, keepdims=True)
        dw_ref[2:3, :] = jnp.sum(dconv * z, axis=0, keepdims=True)

    col = lambda j: (0, j)
    return pl.pallas_call(
        body, name=name, grid=(nb,),
        in_specs=[pl.BlockSpec((T, LANE), lambda j: (0, y0 + j)),
                  pl.BlockSpec((T, LANE), lambda j: (0, b0 + j)), pl.BlockSpec((T, LANE), lambda j: (0, b0 + nb + j)),
                  pl.BlockSpec((T, LANE), lambda j: (0, b0 + 2 * nb + j)), pl.BlockSpec((CONV_WIDTH, LANE), col)],
        out_specs=[pl.BlockSpec((T, LANE), col)] * 3 + [pl.BlockSpec((CONV_WIDTH, LANE), col)],
        out_shape=[_sds((T, BW), MXU_DTYPE)] * 3 + [_sds((CONV_WIDTH, BW), F32)],
        compiler_params=_cparams("parallel"),
    )(dymix, p, p, p, w_conv)


def _head_sum(x, col_head, n_heads):
    out = jnp.zeros_like(x)
    for h in range(n_heads):
        sel = col_head == h
        out = jnp.where(sel, jnp.sum(jnp.where(sel, x, 0.0), axis=-1, keepdims=True), out)
    return out


def _same_head(width):
    assert width == 2 * HEAD_DIM
    return lax.broadcasted_iota(jnp.int32, (1, width), 1) < HEAD_DIM


def _head_sum2(x, first):
    s0 = jnp.sum(jnp.where(first, x, 0.0), axis=-1, keepdims=True)
    s1 = jnp.sum(jnp.where(first, 0.0, x), axis=-1, keepdims=True)
    return jnp.where(first, s0, s1)


def _head_norm(x, g, first):
    r = lax.rsqrt(_head_sum2(x * x, first) * (1.0 / HEAD_DIM) + EPS)
    return x * r * g, r


def _head_norm_bwd(dy, x, g, r, first):
    gdy = dy * g
    mean_xg = _head_sum2(x * gdy, first) * (1.0 / HEAD_DIM)
    return r * gdy - x * (r * r * r) * mean_xg, dy * x * r


ATT_SPAN_MIN = 512
ATT_FWD_UNROLL = 4
ATT_BWD_UNROLL = 2
HEADS_PER_LANES = LANE // HEAD_DIM


def _attn_geometry(T, d):
    m = max(1, ATT_SPAN_MIN // (ATT_BLK * d))
    return m, ATT_BLK * d * m, ATT_BLK * d, T // (ATT_BLK * d)


def _rows(ref, start, d):
    return ref[pl.ds(start, ATT_BLK, stride=d), :] if d > 1 else ref[pl.ds(start, ATT_BLK), :]


def _set_rows(ref, start, d, value):
    if d > 1:
        ref[pl.ds(start, ATT_BLK, stride=d), :] = value
    else:
        ref[pl.ds(start, ATT_BLK), :] = value


def _for_each_block(d, m, task, unroll):
    for j in range(m):
        if d == 1:
            task(0, j)
        else:
            lax.fori_loop(0, d, lambda r, carry, j=j: (task(r, j), carry)[1], 0, unroll=min(unroll, d))


def _head_slices():
    return [slice(h * HEAD_DIM, (h + 1) * HEAD_DIM) for h in range(HEADS_PER_LANES)]


def _attn_fwd(p, qg, kg, g, d, DP, PW, q_start, name):
    T = p.shape[0]
    B, W = ATT_BLK, LANE
    m, span, group, _ = _attn_geometry(T, d)
    c_q = (q_start + g * PW) // W
    c_k, c_v = c_q + 3 * PW // W, c_q + 6 * PW // W
    scale = HEAD_DIM ** -0.5

    def body(q_ref, k_ref, v_ref, kp_ref, vp_ref, qg_ref, kg_ref, o_ref, lse_ref):
        n = pl.program_id(1)
        same = _same_head(W)
        qi = lax.broadcasted_iota(jnp.int32, (B, 2 * B), 0)
        kj = lax.broadcasted_iota(jnp.int32, (B, 2 * B), 1)
        band = (kj >= qi) & (kj <= qi + B)
        qgv, kgv = qg_ref[...], kg_ref[...]

        def task(r, j):
            cur = j * group + r
            if j == 0:
                kp, vp = _rows(kp_ref, r, d), _rows(vp_ref, r, d)
            else:
                kp, vp = _rows(k_ref, cur - group, d), _rows(v_ref, cur - group, d)
            mask = band & ((n * m + j > 0) | (kj >= B))
            qn, _ = _head_norm(_rows(q_ref, cur, d), qgv, same)
            kn, _ = _head_norm(jnp.concatenate([kp, _rows(k_ref, cur, d)], axis=0), kgv, same)
            qn, kn = qn.astype(MXU_DTYPE), kn.astype(MXU_DTYPE)
            vcat = jnp.concatenate([vp, _rows(v_ref, cur, d)], axis=0).astype(MXU_DTYPE)
            o_parts, lse_parts = [], []
            for sl in _head_slices():
                s = lax.dot_general(qn[:, sl], kn[:, sl], (((1,), (1,)), ((), ())), preferred_element_type=F32) * scale
                s = jnp.where(mask, s, NEG_INF)
                mx = jnp.max(s, axis=-1, keepdims=True)
                e = jnp.exp(s - mx)
                den = jnp.sum(e, axis=-1, keepdims=True)
                o_parts.append(jnp.dot(e.astype(MXU_DTYPE), vcat[:, sl], preferred_element_type=F32) / den)
                lse_parts.append(jnp.broadcast_to(mx + jnp.log(den), (B, HEAD_DIM)))
            _set_rows(o_ref, cur, d, jnp.concatenate(o_parts, axis=1))
            _set_rows(lse_ref, cur, d, jnp.concatenate(lse_parts, axis=1))

        _for_each_block(d, m, task, ATT_FWD_UNROLL)

    main = lambda c0: pl.BlockSpec((span, W), lambda hp, n: (n, c0 + hp))
    prev = lambda c0: pl.BlockSpec((group, W), lambda hp, n: (jnp.maximum(n * m - 1, 0), c0 + hp))
    gain = pl.BlockSpec((1, W), lambda hp, n: (0, 0))
    out_blk = pl.BlockSpec((span, W), lambda hp, n: (n, hp))
    return pl.pallas_call(
        body, name=name, grid=(PW // W, T // span),
        in_specs=[main(c_q), main(c_k), main(c_v), prev(c_k), prev(c_v), gain, gain],
        out_specs=[out_blk, out_blk],
        out_shape=[_sds((T, PW), F32), _sds((T, PW), F32)],
        compiler_params=_cparams("parallel", "parallel"),
    )(p, p, p, p, p, qg, kg)


def _attn_bwd(p, qg, kg, lse, do, corr, g, d, PW, q_start, name):
    T = p.shape[0]
    B, W = ATT_BLK, LANE
    m, span, group, n_blocks = _attn_geometry(T, d)
    c_q = (q_start + g * PW) // W
    c_k, c_v = c_q + 3 * PW // W, c_q + 6 * PW // W
    scale = HEAD_DIM ** -0.5
    nt = (((1,), (1,)), ((), ()))
    tn = (((0,), (0,)), ((), ()))

    def body(q_ref, k_ref, v_ref, do_ref, l_ref, c_ref, kp_ref, vp_ref, qx_ref, dox_ref, lx_ref, cx_ref, qg_ref, kg_ref,
             dq_ref, dk_ref, dv_ref, dqg_ref, dkg_ref):
        hp, n = pl.program_id(0), pl.program_id(1)

        @pl.when((hp == 0) & (n == 0))
        def _():
            dqg_ref[...] = jnp.zeros_like(dqg_ref)
            dkg_ref[...] = jnp.zeros_like(dkg_ref)

        same = _same_head(W)
        qgv, kgv = qg_ref[...], kg_ref[...]
        i1 = lax.broadcasted_iota(jnp.int32, (B, B), 0)
        j1 = lax.broadcasted_iota(jnp.int32, (B, B), 1)
        i2 = lax.broadcasted_iota(jnp.int32, (2 * B, B), 0)
        j2 = lax.broadcasted_iota(jnp.int32, (2 * B, B), 1)

        def task(r, j):
            cur = j * group + r
            blk = n * m + j
            q_c, k_c, v_c = _rows(q_ref, cur, d), _rows(k_ref, cur, d), _rows(v_ref, cur, d)
            do_c, l_c, c_c = _rows(do_ref, cur, d), _rows(l_ref, cur, d), _rows(c_ref, cur, d)
            if j == 0:
                k_p, v_p = _rows(kp_ref, r, d), _rows(vp_ref, r, d)
            else:
                k_p, v_p = _rows(k_ref, cur - group, d), _rows(v_ref, cur - group, d)
            if j == m - 1:
                nxt = [_rows(ref, r, d) for ref in (qx_ref, dox_ref, lx_ref, cx_ref)]
            else:
                nxt = [_rows(ref, cur + group, d) for ref in (q_ref, do_ref, l_ref, c_ref)]
            q_x, do_x, l_x, c_x = nxt
            qn_c, q_r = _head_norm(q_c, qgv, same)
            kn_c, k_r = _head_norm(k_c, kgv, same)
            kn_p, _ = _head_norm(k_p, kgv, same)
            qn_x, _ = _head_norm(q_x, qgv, same)
            kn_c, kn_p, v_c, v_p = (a.astype(MXU_DTYPE) for a in (kn_c, kn_p, v_c, v_p))
            qn_c = qn_c.astype(MXU_DTYPE)
            qn_cat = jnp.concatenate([qn_c, qn_x.astype(MXU_DTYPE)], axis=0)
            do_cb = do_c.astype(MXU_DTYPE)
            do_cat = jnp.concatenate([do_cb, do_x.astype(MXU_DTYPE)], axis=0)
            l_cat = jnp.concatenate([l_c, l_x], axis=0)
            c_cat = jnp.concatenate([c_c, c_x], axis=0)
            mask_p = (j1 >= i1) & (blk > 0)
            mask_c = ((i2 < B) & (j2 <= i2)) | ((i2 >= B) & (j2 >= i2 - B) & (blk + 1 < n_blocks))
            dqn, dkn, dv = [], [], []
            for h, sl in enumerate(_head_slices()):
                lane = slice(h * HEAD_DIM, h * HEAD_DIM + 1)
                s_p = lax.dot_general(qn_c[:, sl], kn_p[:, sl], nt, preferred_element_type=F32) * scale
                pr_p = jnp.where(mask_p, jnp.exp(s_p - l_c[:, lane]), 0.0)
                dp_p = lax.dot_general(do_cb[:, sl], v_p[:, sl], nt, preferred_element_type=F32)
                ds_p = (pr_p * (dp_p + c_c[:, lane]) * scale).astype(MXU_DTYPE)
                s_c = lax.dot_general(qn_cat[:, sl], kn_c[:, sl], nt, preferred_element_type=F32) * scale
                pr_c = jnp.where(mask_c, jnp.exp(s_c - l_cat[:, lane]), 0.0)
                dp_c = lax.dot_general(do_cat[:, sl], v_c[:, sl], nt, preferred_element_type=F32)
                ds_c = (pr_c * (dp_c + c_cat[:, lane]) * scale).astype(MXU_DTYPE)
                dqn.append(jnp.dot(ds_p, kn_p[:, sl], preferred_element_type=F32)
                           + jnp.dot(ds_c[:B], kn_c[:, sl], preferred_element_type=F32))
                dkn.append(lax.dot_general(ds_c, qn_cat[:, sl], tn, preferred_element_type=F32))
                dv.append(lax.dot_general(pr_c.astype(MXU_DTYPE), do_cat[:, sl], tn, preferred_element_type=F32))
            dq, dqg_part = _head_norm_bwd(jnp.concatenate(dqn, axis=1), q_c, qgv, q_r, same)
            dk, dkg_part = _head_norm_bwd(jnp.concatenate(dkn, axis=1), k_c, kgv, k_r, same)
            _set_rows(dq_ref, cur, d, dq)
            _set_rows(dk_ref, cur, d, dk)
            _set_rows(dv_ref, cur, d, jnp.concatenate(dv, axis=1))
            dqg_ref[0:1, :] += jnp.sum(dqg_part, axis=0, keepdims=True)
            dkg_ref[0:1, :] += jnp.sum(dkg_part, axis=0, keepdims=True)

        _for_each_block(d, m, task, ATT_BWD_UNROLL)

    n_spans = T // span
    main = lambda c0: pl.BlockSpec((span, W), lambda hp, n: (n, c0 + hp))
    prev = lambda c0: pl.BlockSpec((group, W), lambda hp, n: (jnp.maximum(n * m - 1, 0), c0 + hp))
    nxt = lambda c0: pl.BlockSpec((group, W), lambda hp, n: (jnp.minimum((n + 1) * m, n_blocks - 1), c0 + hp))
    gain = pl.BlockSpec((1, W), lambda hp, n: (0, 0))
    acc = pl.BlockSpec((8, W), lambda hp, n: (0, 0))
    own = pl.BlockSpec((span, W), lambda hp, n: (n, hp))
    dq, dk, dv, dqg, dkg = pl.pallas_call(
        body, name=name, grid=(PW // W, n_spans),
        in_specs=[main(c_q), main(c_k), main(c_v), main(0), main(0), main(0), prev(c_k), prev(c_v),
                  nxt(c_q), nxt(0), nxt(0), nxt(0), gain, gain],
        out_specs=[own, own, own, acc, acc],
        out_shape=[_sds((T, PW), F32)] * 3 + [_sds((8, W), F32)] * 2,
        compiler_params=_cparams("arbitrary", "arbitrary"),
    )(p, p, p, do, lse, corr, p, p, p, do, lse, corr, qg, kg)
    return dq, dk, dv, dqg[0], dkg[0]


def _softmax3(lses):
    mx = jnp.maximum(jnp.maximum(lses[0], lses[1]), lses[2])
    ex = [jnp.exp(l - mx) for l in lses]
    inv = 1.0 / (ex[0] + ex[1] + ex[2])
    return [e * inv for e in ex]


def _mix_fwd(os_, lses, name, tr=512):
    T, PW = os_[0].shape

    def body(o0, o1, o2, l0, l1, l2, y_ref):
        alpha = _softmax3([l0[...], l1[...], l2[...]])
        for g, o_ref in enumerate((o0, o1, o2)):
            y_ref[:, g * PW:(g + 1) * PW] = (o_ref[...] * alpha[g]).astype(y_ref.dtype)

    blk = pl.BlockSpec((tr, PW), lambda i: (i, 0))
    return pl.pallas_call(
        body, name=name, grid=(T // tr,),
        in_specs=[blk] * 6, out_specs=pl.BlockSpec((tr, 3 * PW), lambda i: (i, 0)),
        out_shape=_sds((T, 3 * PW), MXU_DTYPE),
        compiler_params=_cparams("parallel"),
    )(*os_, *lses)


def _mix_bwd(dymix, os_, lses, c_start, name, tr=512):
    T, PW = os_[0].shape
    HP = PW // HEAD_DIM
    c0 = c_start // PW

    def body(d0, d1, d2, o0, o1, o2, l0, l1, l2, do0, do1, do2, dl0, dl1, dl2):
        col_head = lax.broadcasted_iota(jnp.int32, (tr, PW), 1) // HEAD_DIM
        alpha = _softmax3([l0[...], l1[...], l2[...]])
        dys = [d0[...], d1[...], d2[...]]
        dots = [_head_sum(dy * o_ref[...], col_head, HP) for dy, o_ref in zip(dys, (o0, o1, o2))]
        mean_dot = alpha[0] * dots[0] + alpha[1] * dots[1] + alpha[2] * dots[2]
        for g, (do_ref, dl_ref) in enumerate(((do0, dl0), (do1, dl1), (do2, dl2))):
            do_ref[...] = dys[g] * alpha[g]
            dl_ref[...] = -alpha[g] * mean_dot

    blk = pl.BlockSpec((tr, PW), lambda i: (i, 0))
    dy_specs = [pl.BlockSpec((tr, PW), lambda i, g=g: (i, c0 + g)) for g in range(3)]
    outs = pl.pallas_call(
        body, name=name, grid=(T // tr,),
        in_specs=dy_specs + [blk] * 6, out_specs=[blk] * 6,
        out_shape=[_sds((T, PW), F32)] * 6,
        compiler_params=_cparams("parallel"),
    )(dymix, dymix, dymix, *os_, *lses)
    return outs[:3], outs[3:]


def _adamw_math(w, g, m, v):
    m2 = ADAM_B1 * m + (1.0 - ADAM_B1) * g
    v2 = ADAM_B2 * v + (1.0 - ADAM_B2) * (g * g)
    m_hat = m2 / (1.0 - ADAM_B1 ** ADAM_STEP)
    v_hat = v2 / (1.0 - ADAM_B2 ** ADAM_STEP)
    delta = -ADAM_LR * (m_hat / (jnp.sqrt(v_hat) + ADAM_EPS) + ADAM_WD * w)
    return delta, m2, v2


def _adamw_layer(layer, w, m, v, own, landed, me, prev, name, tr=256):
    _, R, C = w.shape
    tr = next(t for t in range(min(tr, R) // 16 * 16, 0, -16) if R % t == 0)

    def body(me_ref, w_ref, m_ref, v_ref, own_ref, land_ref, *rest):
        g_ref, d_ref, m2_ref, v2_ref = rest[-4:]
        g = own_ref[...].astype(F32)
        for j in range(N_PEER):
            g = g + land_ref[j].astype(F32)
        delta, m2, v2 = _adamw_math(w_ref[...], g, m_ref[...], v_ref[...])
        g_ref[...] = g
        d_ref[...] = delta
        m2_ref[...] = m2
        v2_ref[...] = v2

    lay = pl.BlockSpec((None, tr, C), lambda i, me_ref: (layer, i, 0))
    in_specs = [lay, lay, lay, pl.BlockSpec((None, tr, C), lambda i, me_ref: (me_ref[0], i, 0)),
                pl.BlockSpec((N_PEER, tr, C), lambda i, me_ref: (0, i, 0))]
    args = [me, w, m, v, own, landed]
    aliases = {}
    if prev is not None:
        in_specs += [pl.BlockSpec(memory_space=pl.ANY)] * 4
        args += list(prev)
        aliases = {6 + i: i for i in range(4)}
    return pl.pallas_call(
        body, name=name,
        grid_spec=pltpu.PrefetchScalarGridSpec(num_scalar_prefetch=1, grid=(R // tr,), in_specs=in_specs, out_specs=[lay] * 4),
        out_shape=[_sds(w.shape, F32)] * 4,
        input_output_aliases=aliases,
        compiler_params=_cparams("parallel"),
    )(*args)


def _sum_parts(parts, name):
    _, R, C = parts.shape

    def body(p_ref, out_ref):
        g = p_ref[0]
        for j in range(1, N_DEV):
            g = g + p_ref[j]
        out_ref[...] = g

    return pl.pallas_call(
        body, name=name, grid=(1,),
        in_specs=[pl.BlockSpec((N_DEV, R, C), lambda i: (0, 0, 0))], out_specs=pl.BlockSpec((R, C), lambda i: (0, 0)),
        out_shape=_sds((R, C), F32), compiler_params=_cparams("arbitrary"),
    )(parts)


def _adamw_flat(w, g, m, v, name):
    R, C = w.shape

    def body(w_ref, g_ref, m_ref, v_ref, d_ref, m2_ref, v2_ref):
        delta, m2, v2 = _adamw_math(w_ref[...], g_ref[...], m_ref[...], v_ref[...])
        d_ref[...] = delta
        m2_ref[...] = m2
        v2_ref[...] = v2

    blk = pl.BlockSpec((R, C), lambda i: (0, 0))
    return pl.pallas_call(
        body, name=name, grid=(1,), in_specs=[blk] * 4, out_specs=[blk] * 3, out_shape=[_sds((R, C), F32)] * 3,
        compiler_params=_cparams("arbitrary"),
    )(w, g, m, v)


def _pack(arrays, rows_multiple=8):
    flat = []
    for a in arrays:
        a = a.reshape(-1).astype(F32)
        flat.append(jnp.pad(a, (0, (-a.shape[0]) % LANE)))
    flat = jnp.concatenate(flat)
    flat = jnp.pad(flat, (0, (-flat.shape[0]) % (LANE * rows_multiple)))
    return flat.reshape(-1, LANE)


def _unpack(packed, shapes):
    flat = packed.reshape(-1)
    out, off = [], 0
    for s in shapes:
        size = 1
        for dim in s:
            size *= dim
        out.append(flat[off:off + size].reshape(s))
        off += size + (-size) % LANE
    return out


def _layer_fwd(x, wts, getw, dims, dep=None):
    AW, BW, PW, DP = dims["AW"], dims["BW"], dims["PW"], dims["DP"]
    q_start = 2 * AW + 3 * BW
    h, r1 = _rmsnorm_fwd(x, wts["attn_norm"], "rmsnorm_fwd", dep=dep)
    p = _mm_nt(h, getw("w_in", h), "proj_in", tm=1024, tn=512)
    y_a = _sgu_fwd(p, wts["sgu_tril"], wts["sgu_bmat"], "sgu_fwd")
    y_b = _conv_fwd(p, getw("conv_w", y_a), AW, "conv_fwd")
    os_, lses = [], []
    for g, d in enumerate(DILATIONS):
        o, lse = _attn_fwd(p, wts["q_gain"], wts["k_gain"], g, d, DP, PW, q_start, "attn_fwd_%d" % d)
        os_.append(o)
        lses.append(lse)
    y_c = _mix_fwd(os_, lses, "mix_fwd")
    ymix = jnp.concatenate([y_a, y_b, y_c], axis=1)
    x1 = _mm_nn(ymix, getw("w_out", ymix), "proj_out", residual=x, tm=1024, tn=1024)
    h2, r2 = _rmsnorm_fwd(x1, wts["mlp_norm"], "rmsnorm_fwd")
    a, hid = _mm_nn(h2, getw("w_mlp_in", h2), "mlp_in", out_dtype=MXU_DTYPE, relu2=True, tm=1024, tn=1024)
    x2 = _mm_nn(hid, getw("w_mlp_out", hid), "mlp_out", residual=x1, columns_first=True, tm=512, tn=1024)
    saved = dict(x=x, h=h, r1=r1, p=p, os=os_, lses=lses, ymix=ymix, x1=x1, h2=h2, r2=r2, a=a, hid=hid)
    return x2, saved


def _layer_bwd(dx, dxb, wts, getw, scatter, saved, dims):
    AW, BW, PW, DP = dims["AW"], dims["BW"], dims["PW"], dims["DP"]
    q_start = 2 * AW + 3 * BW
    D = dx.shape[1]
    g_w2 = _wgrad_wide_a(saved["hid"], dxb, "mlp_out_wgrad")
    token = scatter("w_mlp_out", g_w2.reshape(N_DEV, -1, D))
    da = _mm_nt(dxb, getw("w_mlp_out", None), "mlp_out_dgrad", out_dtype=MXU_DTYPE, relu2_pre=saved["a"], dep=token,
                tm=1024, tn=1024)
    g_w1 = _wgrad_wide_b(saved["h2"], da, "mlp_in_wgrad", groups=N_DEV)
    token = scatter("w_mlp_in", g_w1)
    dh2 = _mm_nt(da, getw("w_mlp_in", None), "mlp_in_dgrad", dep=token, tm=1024, tn=512)
    dx1, dx1b, g_mlp_norm = _rmsnorm_bwd(dh2, saved["x1"], wts["mlp_norm"], saved["r2"], dx, "rmsnorm_bwd")
    g_wout = _wgrad_wide_b(saved["ymix"], dx1b, "proj_out_wgrad")
    token = scatter("w_out", g_wout.reshape(N_DEV, -1, D))
    dymix = _mm_nt(dx1b, getw("w_out", None), "proj_out_dgrad", dep=token, tm=1024, tn=1024)
    p = saved["p"]
    du, dv, g_sgu_w, g_sgu_bmat = _sgu_bwd(dymix, p, wts["sgu_tril"], wts["sgu_tril_t"], wts["sgu_bmat"], "sgu_bwd")
    d_b, d_c, d_xb, g_conv = _conv_bwd(dymix, p, getw("conv_w", None), AW, "conv_bwd")
    dos, corrs = _mix_bwd(dymix, saved["os"], saved["lses"], AW + BW, "mix_bwd")
    dqs, dks, dvs = [], [], []
    g_q = g_k = 0.0
    for g, d in enumerate(DILATIONS):
        dq, dk, dvv, dqg, dkg = _attn_bwd(p, wts["q_gain"], wts["k_gain"], saved["lses"][g], dos[g], corrs[g],
                                          g, d, PW, q_start, "attn_bwd_%d" % d)
        dqs.append(dq.astype(MXU_DTYPE))
        dks.append(dk.astype(MXU_DTYPE))
        dvs.append(dvv.astype(MXU_DTYPE))
        g_q = g_q + dqg.reshape(-1, HEAD_DIM).sum(0)
        g_k = g_k + dkg.reshape(-1, HEAD_DIM).sum(0)
    dp = jnp.concatenate([du, dv, d_b, d_c, d_xb] + dqs + dks + dvs, axis=1)
    g_win_t = _wgrad_wide_a(dp, saved["h"], "proj_in_wgrad")
    token = scatter("w_in", g_win_t.reshape(N_DEV, DP // N_DEV, D))
    dh = _mm_nn(dp, getw("w_in", None), "proj_in_dgrad", dep=token, tm=1024, tn=512)
    dx0, dx0b, g_attn_norm = _rmsnorm_bwd(dh, saved["x"], wts["attn_norm"], saved["r1"], dx1, "rmsnorm_bwd")
    H = AW // HEAD_DIM
    tril = jnp.tril(jnp.ones((CHUNK, CHUNK), F32))
    small = [g_attn_norm.reshape(-1), g_sgu_w * tril, g_sgu_bmat.reshape(CHUNK, H, HEAD_DIM).sum(-1).T,
             g_conv, g_q, g_k, g_mlp_norm.reshape(-1)]
    return dx0, dx0b, small


def kernel(x, attn_norm, w_in, sgu_w, sgu_b, conv_w, q_norm, k_norm, w_out, mlp_norm, w_mlp_in, w_mlp_out, loss_target, m_attn_norm, m_w_in, m_sgu_w, m_sgu_b, m_conv_w, m_q_norm, m_k_norm, m_w_out, m_mlp_norm, m_w_mlp_in, m_w_mlp_out, v_attn_norm, v_w_in, v_sgu_w, v_sgu_b, v_conv_w, v_q_norm, v_k_norm, v_w_out, v_mlp_norm, v_w_mlp_in, v_w_mlp_out):
    n_layers = attn_norm.shape[0]
    T, D = x.shape[1], x.shape[2]
    H = sgu_w.shape[1]
    AW = H * HEAD_DIM
    BW = conv_w.shape[2] * N_DEV
    DP = w_in.shape[2] * N_DEV
    DMIX = w_out.shape[1] * N_DEV
    DFF = w_mlp_in.shape[2] * N_DEV
    PW = (DMIX - AW - BW) // 3
    HP = PW // HEAD_DIM
    dims = dict(AW=AW, BW=BW, PW=PW, DP=DP)
    me = 4 * lax.axis_index("x") + 2 * lax.axis_index("y") + lax.axis_index("c")

    big_names = ("w_in", "w_out", "w_mlp_in", "w_mlp_out")
    tr_in = lambda a: jnp.swapaxes(a, 1, 2)
    big_w = dict(zip(big_names, (tr_in(w_in), w_out, w_mlp_in, w_mlp_out)))
    big_m = dict(zip(big_names, (tr_in(m_w_in), m_w_out, m_w_mlp_in, m_w_mlp_out)))
    big_v = dict(zip(big_names, (tr_in(v_w_in), v_w_out, v_w_mlp_in, v_w_mlp_out)))

    keys = []
    for l in range(n_layers):
        keys += [(l, nm) for nm in big_names]
    keys.insert(1, (0, "conv_w"))
    srcs = [_pack([conv_w]) if nm == "conv_w" else big_w[nm][l].astype(MXU_DTYPE) for l, nm in keys]
    flights, gather_token = _exchange_start(srcs, [_own_in_place(s, me) for s in srcs], "gather", name="gather_start")
    arriving = dict(zip(keys, flights))
    forwarding = {}
    relayout = dict(
        w_in=lambda g: g.reshape(DP, D), w_out=lambda g: g.reshape(DMIX, D),
        w_mlp_in=lambda g: g, w_mlp_out=lambda g: g.reshape(DFF, D),
        conv_w=lambda g: jnp.stack([_unpack(g[j], [conv_w.shape])[0] for j in range(N_DEV)], axis=2).reshape(
            n_layers, CONV_WIDTH, BW))
    gathered = {}

    def forward(key, after):
        _, land = _exchange_wait(arriving[key], after, "gather", name="gather_arrive_%d_%s" % key)
        fl, token = _exchange_start(None, [land], "forward", name="gather_forward_%d_%s" % key)
        forwarding[key] = fl[0]
        return token

    def weight_getter(l):
        def getw(nm, after):
            key = (0, nm) if nm == "conv_w" else (l, nm)
            if key not in gathered:
                ahead = keys[keys.index(key):][:2]
                for k in ahead:
                    if k not in forwarding:
                        after = forward(k, after)
                _, land = _exchange_wait(forwarding[key], after, "forward", name="gather_wait_%d_%s" % key)
                gathered[key] = relayout[nm](land)
            return gathered[key][l] if nm == "conv_w" else gathered[key]
        return getw

    tril = jnp.tril(jnp.ones((CHUNK, CHUNK), F32))
    layers = []
    for l in range(n_layers):
        w_tril = sgu_w[l] * tril
        layers.append(dict(
            attn_norm=attn_norm[l][None], mlp_norm=mlp_norm[l][None],
            sgu_tril=w_tril.astype(MXU_DTYPE), sgu_tril_t=w_tril.transpose(0, 2, 1).astype(MXU_DTYPE),
            sgu_bmat=jnp.repeat(sgu_b[l].T, HEAD_DIM, axis=1),
            q_gain=jnp.tile(q_norm[l], HEADS_PER_LANES)[None], k_gain=jnp.tile(k_norm[l], HEADS_PER_LANES)[None]))

    xs = x[0]
    saved = []
    for l in range(n_layers):
        xs, sv = _layer_fwd(xs, layers[l], weight_getter(l), dims, dep=gather_token if l == 0 else None)
        saved.append(sv)
    loss_blk, dx, dxb = _loss_and_grad(xs, loss_target[0], "loss")
    loss = lax.psum(loss_blk[0, 0], ("x", "y", "c"))

    scattering = {}

    def scatter_starter(l):
        def scatter(nm, partials):
            land = lax.empty((N_PEER,) + partials.shape[1:], partials.dtype)
            fl, tok = _exchange_start([partials], [land], "scatter", name="scatter_start_%d_%s" % (l, nm))
            scattering[(l, nm)] = fl[0]
            return tok
        return scatter

    small = [None] * n_layers
    for l in reversed(range(n_layers)):
        dx, dxb, small[l] = _layer_bwd(dx, dxb, layers[l], weight_getter(l), scatter_starter(l), saved[l], dims)

    small_shapes = [s.shape for s in small[0]]
    small_src = [_pack([s for l in range(n_layers) for s in small[l]])]
    small_flights, small_token = _exchange_start(small_src, [_own_in_place(s, me) for s in small_src], "gather_all",
                                                 name="small_start")
    grad_x = dx[None]

    me1 = me.astype(jnp.int32).reshape(1)
    res = {nm: None for nm in big_names}
    after = small_token
    for l in reversed(range(n_layers)):
        for nm in reversed(big_names):
            own, landed = _exchange_wait(scattering[(l, nm)], after, "scatter", name="scatter_wait_%d_%s" % (l, nm))
            res[nm] = _adamw_layer(l, big_w[nm], big_m[nm], big_v[nm], own, landed, me1, res[nm], "adamw_" + nm)
            after = res[nm][0]
    res["w_in"] = [tr_in(a) for a in res["w_in"]]
    big_out = [res[nm] for nm in big_names]

    _, gathered_small = _exchange_wait(small_flights[0], after, "gather_all", name="small_wait")
    summed = _unpack(_sum_parts(gathered_small, "sum_small"), small_shapes * n_layers)
    ns = len(small_shapes)
    g_small = [jnp.stack([summed[l * ns + i] for l in range(n_layers)]) for i in range(ns)]
    g_attn_norm, g_sgu_w, g_sgu_b, g_conv_full, g_q, g_k, g_mlp_norm = g_small
    cs = conv_w.shape[2]
    g_conv = lax.dynamic_slice_in_dim(g_conv_full, me * cs, cs, axis=2)
    sm_w = (attn_norm, sgu_w, sgu_b, conv_w, q_norm, k_norm, mlp_norm)
    sm_m = (m_attn_norm, m_sgu_w, m_sgu_b, m_conv_w, m_q_norm, m_k_norm, m_mlp_norm)
    sm_v = (v_attn_norm, v_sgu_w, v_sgu_b, v_conv_w, v_q_norm, v_k_norm, v_mlp_norm)
    sm_g = (g_attn_norm, g_sgu_w, g_sgu_b, g_conv, g_q, g_k, g_mlp_norm)
    sm_res = _adamw_flat(_pack(sm_w), _pack(sm_g), _pack(sm_m), _pack(sm_v), "adamw_small")
    shapes = [w.shape for w in sm_w]
    sm_delta, sm_m2, sm_v2 = (_unpack(r, shapes) for r in sm_res)

    def ordered(small_list, big_kind):
        b = [big_out[i][big_kind] for i in range(4)]
        return [small_list[0], b[0], small_list[1], small_list[2], small_list[3], small_list[4], small_list[5],
                b[1], small_list[6], b[2], b[3]]

    return (loss, grad_x, *ordered(list(sm_g), 0), *ordered(sm_delta, 1), *ordered(sm_m2, 2), *ordered(sm_v2, 3))
```

```python
import jax
import jax.numpy as jnp
from jax import lax
from jax.experimental import pallas as pl
from jax.experimental.pallas import tpu as pltpu

N_DEV = 8
HEAD_DIM = 64
CHUNK = 128
ATT_BLK = 128
DILATIONS = (1, 4, 16)
CONV_WIDTH = 3
EPS = 1e-6
ADAM_LR = 0.001
ADAM_B1 = 0.9
ADAM_B2 = 0.999
ADAM_EPS = 1e-08
ADAM_WD = 0.01
ADAM_STEP = 10
MXU_DTYPE = jnp.bfloat16
F32 = jnp.float32
LANE = 128
VMEM_LIMIT_BYTES = 56 * 1024 * 1024
NEG_INF = float("-inf")


def _cparams(*sem):
    return pltpu.CompilerParams(dimension_semantics=sem, vmem_limit_bytes=VMEM_LIMIT_BYTES)


def _sds(shape, dtype):
    return jax.ShapeDtypeStruct(shape, dtype)


def _fit(n, tile):
    for t in range(min(tile, n) // LANE * LANE, 0, -LANE):
        if n % t == 0:
            return t
    return n


_HBM = pl.BlockSpec(memory_space=pltpu.HBM)
_SEM = pl.BlockSpec(memory_space=pltpu.SEMAPHORE)
_DATAFLOW = pltpu.SideEffectType.DATAFLOW_SIDE_EFFECTING
N_PEER = N_DEV - 1


def _mesh_pos():
    x, y, c = lax.axis_index("x"), lax.axis_index("y"), lax.axis_index("c")
    return x, y, c, 4 * x + 2 * y + c


OTHER_CHIPS = (4, 2, 6)
EXCHANGE_PEERS = dict(
    scatter=tuple(range(1, N_DEV)),
    gather_all=tuple(range(1, N_DEV)),
    gather=(1,) + OTHER_CHIPS,
    forward=OTHER_CHIPS)


def _remote_copies(src, land, send_sems, recv_sems, mode):
    x, y, c, me = _mesh_pos()
    copies = []
    for i, k in enumerate(EXCHANGE_PEERS[mode]):
        px = (1 - x) if (k & 4) else x
        py = (1 - y) if (k & 2) else y
        pc = (1 - c) if (k & 1) else c
        if mode == "scatter":
            src_ref, dst_ref, dev = src.at[4 * px + 2 * py + pc], land.at[i], (px, py, pc)
        elif mode == "forward":
            slot = 4 * px + 2 * py + c
            src_ref, dst_ref, dev = land.at[slot], land.at[slot], (x, y, 1 - c)
        else:
            src_ref, dst_ref, dev = src, land.at[me], (px, py, pc)
        copies.append(pltpu.make_async_remote_copy(
            src_ref=src_ref, dst_ref=dst_ref, send_sem=send_sems.at[i], recv_sem=recv_sems.at[i],
            device_id=dev, device_id_type=pl.DeviceIdType.MESH))
    return copies


def _own_in_place(src, me):
    land = lax.empty((N_DEV,) + src.shape, src.dtype)
    return lax.dynamic_update_slice(land, src[None], (me,) + (0,) * src.ndim)


def _exchange_start(srcs, lands, mode, name):
    n = len(lands)
    has_src = srcs is not None
    arrays = (list(srcs) if has_src else []) + list(lands)
    n_arr = len(arrays)
    n_copies = len(EXCHANGE_PEERS[mode])

    def body(*refs):
        src = refs[:n] if has_src else [None] * n
        land = refs[n_arr - n:n_arr]
        send, recv = refs[n_arr:n_arr + n], refs[n_arr + n:n_arr + 2 * n]
        token = refs[2 * n_arr + 2 * n]
        for t in range(n):
            for cp in _remote_copies(src[t], land[t], send[t], recv[t], mode):
                cp.start()
        token[...] = jnp.zeros_like(token)

    outs = pl.pallas_call(
        body, name=name,
        out_shape=([pltpu.SemaphoreType.DMA((n_copies,))] * (2 * n) + [pltpu.HBM(a.shape, a.dtype) for a in arrays]
                   + [_sds((8, LANE), F32)]),
        in_specs=[_HBM] * n_arr,
        out_specs=[_SEM] * (2 * n) + [_HBM] * n_arr + [pl.BlockSpec(memory_space=pltpu.VMEM)],
        input_output_aliases={i: 2 * n + i for i in range(n_arr)},
        compiler_params=pltpu.CompilerParams(has_side_effects=_DATAFLOW),
    )(*[pltpu.with_memory_space_constraint(a, pltpu.HBM) for a in arrays])
    thru = outs[2 * n:2 * n + n_arr]
    flights = [(outs[t], outs[n + t], thru[t] if has_src else None, thru[n_arr - n + t]) for t in range(n)]
    return flights, outs[2 * n + n_arr]


def _exchange_wait(flight, after, mode, name):
    send, recv, src, land = flight
    arrays = [land] if src is None else [src, land]
    n_arr = len(arrays)

    def body(*refs):
        src_ref = refs[0] if n_arr == 2 else None
        land_ref, send_ref, recv_ref = refs[n_arr - 1], refs[n_arr], refs[n_arr + 1]
        for cp in _remote_copies(src_ref, land_ref, send_ref, recv_ref, mode):
            cp.wait_send()
            cp.wait_recv()

    outs = pl.pallas_call(
        body, name=name, out_shape=[pltpu.HBM(a.shape, a.dtype) for a in arrays],
        in_specs=[_HBM] * n_arr + [_SEM, _SEM, pl.BlockSpec(memory_space=pl.ANY)], out_specs=[_HBM] * n_arr,
        input_output_aliases={i: i for i in range(n_arr)},
        compiler_params=pltpu.CompilerParams(has_side_effects=_DATAFLOW),
    )(*arrays, send, recv, after)
    return (None, outs[0]) if src is None else (outs[0], outs[1])


def _rmsnorm_fwd(x, g, name, dep=None, tr=512):
    T, D = x.shape

    def body(x_ref, g_ref, *rest):
        h_ref, r_ref = rest[-2:]
        xv = x_ref[...]
        r = lax.rsqrt(jnp.mean(xv * xv, axis=-1, keepdims=True) + EPS)
        h_ref[...] = (xv * r * g_ref[...]).astype(h_ref.dtype)
        r_ref[...] = r

    in_specs = [pl.BlockSpec((tr, D), lambda i: (i, 0)), pl.BlockSpec((1, D), lambda i: (0, 0))]
    args = [x, g]
    if dep is not None:
        in_specs.append(pl.BlockSpec(dep.shape, lambda i: (0, 0)))
        args.append(dep)
    return pl.pallas_call(
        body, name=name, grid=(T // tr,),
        in_specs=in_specs,
        out_specs=[pl.BlockSpec((tr, D), lambda i: (i, 0)), pl.BlockSpec((tr, 1), lambda i: (i, 0))],
        out_shape=[_sds((T, D), MXU_DTYPE), _sds((T, 1), F32)],
        compiler_params=_cparams("parallel"),
    )(*args)


def _rmsnorm_bwd(dh, x, g, r, dres, name, tr=256):
    T, D = x.shape

    def body(dh_ref, x_ref, g_ref, r_ref, dres_ref, dx_ref, dxb_ref, dg_ref):
        @pl.when(pl.program_id(0) == 0)
        def _():
            dg_ref[...] = jnp.zeros_like(dg_ref)

        dh_v, xv, rv = dh_ref[...].astype(F32), x_ref[...], r_ref[...]
        gdy = dh_v * g_ref[...]
        mean_xg = jnp.mean(xv * gdy, axis=-1, keepdims=True)
        dx = dres_ref[...] + rv * gdy - xv * (rv * rv * rv) * mean_xg
        dx_ref[...] = dx
        dxb_ref[...] = dx.astype(dxb_ref.dtype)
        dg_ref[...] += jnp.sum(dh_v * xv * rv, axis=0, keepdims=True)

    row = lambda i: (i, 0)
    return pl.pallas_call(
        body, name=name, grid=(T // tr,),
        in_specs=[pl.BlockSpec((tr, D), row), pl.BlockSpec((tr, D), row), pl.BlockSpec((1, D), lambda i: (0, 0)),
                  pl.BlockSpec((tr, 1), row), pl.BlockSpec((tr, D), row)],
        out_specs=[pl.BlockSpec((tr, D), row), pl.BlockSpec((tr, D), row), pl.BlockSpec((1, D), lambda i: (0, 0))],
        out_shape=[_sds((T, D), F32), _sds((T, D), MXU_DTYPE), _sds((1, D), F32)],
        compiler_params=_cparams("arbitrary"),
    )(dh, x, g, r, dres)


def _mm_nn(a, b, name, out_dtype=F32, residual=None, relu2=False, dep=None, columns_first=False, tm=512, tn=512):
    M, K = a.shape
    grouped = b.ndim == 3
    N = b.shape[0] * b.shape[2] if grouped else b.shape[1]
    tm, tn = _fit(M, tm), _fit(b.shape[2] if grouped else N, tn)
    tile = (lambda j, i: (i, j)) if columns_first else (lambda i, j: (i, j))
    b_mode = pl.Buffered(1 if columns_first else 2)
    if grouped:
        per = b.shape[2] // tn
        b_spec = pl.BlockSpec((None, K, tn), lambda *g: (tile(*g)[1] // per, 0, tile(*g)[1] % per), pipeline_mode=b_mode)
    else:
        b_spec = pl.BlockSpec((K, tn), lambda *g: (0, tile(*g)[1]), pipeline_mode=b_mode)
    n_out = 2 if relu2 else 1

    def body(*refs):
        a_ref, b_ref = refs[0], refs[1]
        r_ref = refs[2] if residual is not None else None
        outs = refs[2 + (residual is not None) + (dep is not None):]
        acc = jnp.dot(a_ref[...], b_ref[...], preferred_element_type=F32)
        if r_ref is not None:
            acc = acc + r_ref[...]
        outs[0][...] = acc.astype(outs[0].dtype)
        if relu2:
            rl = jnp.maximum(acc, 0.0)
            outs[1][...] = (rl * rl).astype(outs[1].dtype)

    out_blk = pl.BlockSpec((tm, tn), lambda *g: tile(*g))
    in_specs = [pl.BlockSpec((tm, K), lambda *g: (tile(*g)[0], 0)), b_spec]
    args = [a, b]
    if residual is not None:
        in_specs.append(out_blk)
        args.append(residual)
    if dep is not None:
        in_specs.append(pl.BlockSpec(dep.shape, lambda *g: (0, 0)))
        args.append(dep)
    out_shape = [_sds((M, N), out_dtype)]
    if relu2:
        out_shape.append(_sds((M, N), MXU_DTYPE))
    outs = pl.pallas_call(
        body, name=name, grid=(N // tn, M // tm) if columns_first else (M // tm, N // tn),
        in_specs=in_specs, out_specs=[out_blk] * n_out, out_shape=out_shape,
        compiler_params=_cparams("parallel", "parallel"),
    )(*args)
    return outs if relu2 else outs[0]


def _mm_nt(a, b, name, out_dtype=F32, relu2_pre=None, dep=None, a_single=False, tm=512, tn=512):
    M, K = a.shape
    grouped = b.ndim == 3
    N = b.shape[1] if grouped else b.shape[0]
    tm, tn = _fit(M, tm), _fit(N, tn)
    nt = (((1,), (1,)), ((), ()))
    if grouped:
        G, _, Kg = b.shape
        b_spec = pl.BlockSpec((G, tn, Kg), lambda i, j: (0, j, 0))
    else:
        b_spec = pl.BlockSpec((tn, K), lambda i, j: (j, 0))

    def body(*refs):
        a_ref, b_ref = refs[0], refs[1]
        p_ref = refs[2] if relu2_pre is not None else None
        out_ref = refs[2 + (relu2_pre is not None) + (dep is not None)]
        if grouped:
            acc = lax.dot_general(a_ref[:, 0:Kg], b_ref[0], nt, preferred_element_type=F32)
            for g in range(1, G):
                acc += lax.dot_general(a_ref[:, g * Kg:(g + 1) * Kg], b_ref[g], nt, preferred_element_type=F32)
        else:
            acc = lax.dot_general(a_ref[...], b_ref[...], nt, preferred_element_type=F32)
        if p_ref is not None:
            acc = acc * (2.0 * jnp.maximum(p_ref[...].astype(F32), 0.0))
        out_ref[...] = acc.astype(out_ref.dtype)

    out_blk = pl.BlockSpec((tm, tn), lambda i, j: (i, j))
    in_specs = [pl.BlockSpec((tm, K), lambda i, j: (i, 0), pipeline_mode=pl.Buffered(1 if a_single else 2)), b_spec]
    args = [a, b]
    if relu2_pre is not None:
        in_specs.append(out_blk)
        args.append(relu2_pre)
    if dep is not None:
        in_specs.append(pl.BlockSpec(dep.shape, lambda i, j: (0, 0)))
        args.append(dep)
    return pl.pallas_call(
        body, name=name, grid=(M // tm, N // tn),
        in_specs=in_specs, out_specs=out_blk, out_shape=_sds((M, N), out_dtype),
        compiler_params=_cparams("parallel", "parallel"),
    )(*args)


def _wgrad_wide_a(a, b, name, tm=512):
    T, M = a.shape
    N = b.shape[1]
    tm = _fit(M, tm)

    def body(a_ref, b_ref, out_ref):
        out_ref[...] = lax.dot_general(a_ref[...], b_ref[...], (((0,), (0,)), ((), ())),
                                       preferred_element_type=F32).astype(out_ref.dtype)

    return pl.pallas_call(
        body, name=name, grid=(M // tm,),
        in_specs=[pl.BlockSpec((T, tm), lambda i: (0, i)),
                  pl.BlockSpec((T, N), lambda i: (0, 0), pipeline_mode=pl.Buffered(1))],
        out_specs=pl.BlockSpec((tm, N), lambda i: (i, 0)), out_shape=_sds((M, N), MXU_DTYPE),
        compiler_params=_cparams("parallel"),
    )(a, b)


def _wgrad_wide_b(a, b, name, groups=None, tn=512, t_chunk=512):
    T, M = a.shape
    N = b.shape[1]
    tn = _fit(N if groups is None else N // groups, tn)
    t_chunk = _fit(T, t_chunk)

    def body(a_ref, b_ref, out_ref, at_ref):
        @pl.when(pl.program_id(0) == 0)
        def _():
            for c in range(0, T, t_chunk):
                at_ref[:, c:c + t_chunk] = a_ref[c:c + t_chunk, :].T

        out_ref[...] = jnp.dot(at_ref[...], b_ref[...], preferred_element_type=F32).astype(out_ref.dtype)

    if groups is None:
        out_spec = pl.BlockSpec((M, tn), lambda j: (0, j))
        out_shape = _sds((M, N), MXU_DTYPE)
    else:
        per = N // groups // tn
        out_spec = pl.BlockSpec((None, M, tn), lambda j: (j // per, 0, j % per))
        out_shape = _sds((groups, M, N // groups), MXU_DTYPE)
    return pl.pallas_call(
        body, name=name, grid=(N // tn,),
        in_specs=[pl.BlockSpec((T, M), lambda j: (0, 0), pipeline_mode=pl.Buffered(1)),
                  pl.BlockSpec((T, tn), lambda j: (0, j))],
        out_specs=out_spec, out_shape=out_shape,
        scratch_shapes=[pltpu.VMEM((M, T), MXU_DTYPE)],
        compiler_params=_cparams("arbitrary"),
    )(a, b)


def _loss_and_grad(y, target, name, tr=512):
    T, D = y.shape

    def body(y_ref, t_ref, loss_ref, dx_ref, dxb_ref):
        @pl.when(pl.program_id(0) == 0)
        def _():
            loss_ref[...] = jnp.zeros_like(loss_ref)

        err = y_ref[...] - t_ref[...]
        loss_ref[...] += 0.5 * jnp.sum(jnp.mean(err * err, axis=-1, keepdims=True), axis=0, keepdims=True)
        dx = err * (1.0 / D)
        dx_ref[...] = dx
        dxb_ref[...] = dx.astype(dxb_ref.dtype)

    row = lambda i: (i, 0)
    return pl.pallas_call(
        body, name=name, grid=(T // tr,),
        in_specs=[pl.BlockSpec((tr, D), row), pl.BlockSpec((tr, D), row)],
        out_specs=[pl.BlockSpec((8, LANE), lambda i: (0, 0)), pl.BlockSpec((tr, D), row), pl.BlockSpec((tr, D), row)],
        out_shape=[_sds((8, LANE), F32), _sds((T, D), F32), _sds((T, D), MXU_DTYPE)],
        compiler_params=_cparams("arbitrary"),
    )(y, target)


def _sgu_mixed(v, w_ref, b_ref, col_head, n_heads):
    mixed = b_ref[...]
    for h in range(n_heads):
        full = jnp.dot(w_ref[h], v, preferred_element_type=F32)
        mixed = mixed + jnp.where(col_head == h, full, 0.0)
    return mixed


def _sgu_fwd(p, w_tril, bmat, name):
    T = p.shape[0]
    H = w_tril.shape[0]
    AW = H * HEAD_DIM

    def body(u_ref, v_ref, w_ref, b_ref, y_ref):
        col_head = lax.broadcasted_iota(jnp.int32, (CHUNK, AW), 1) // HEAD_DIM
        mixed = _sgu_mixed(v_ref[...].astype(MXU_DTYPE), w_ref, b_ref, col_head, H)
        y_ref[...] = (u_ref[...] * mixed).astype(y_ref.dtype)

    const3 = lambda c: (0, 0, 0)
    return pl.pallas_call(
        body, name=name, grid=(T // CHUNK,),
        in_specs=[pl.BlockSpec((CHUNK, AW), lambda c: (c, 0)), pl.BlockSpec((CHUNK, AW), lambda c: (c, 1)),
                  pl.BlockSpec((H, CHUNK, CHUNK), const3), pl.BlockSpec((CHUNK, AW), lambda c: (0, 0))],
        out_specs=pl.BlockSpec((CHUNK, AW), lambda c: (c, 0)),
        out_shape=_sds((T, AW), MXU_DTYPE),
        compiler_params=_cparams("parallel"),
    )(p, p, w_tril, bmat)


def _sgu_bwd(dymix, p, w_tril, w_tril_t, bmat, name):
    T = p.shape[0]
    H = w_tril.shape[0]
    AW = H * HEAD_DIM

    def body(dy_ref, u_ref, v_ref, w_ref, wt_ref, b_ref, du_ref, dv_ref, dw_ref, db_ref):
        @pl.when(pl.program_id(0) == 0)
        def _():
            dw_ref[...] = jnp.zeros_like(dw_ref)
            db_ref[...] = jnp.zeros_like(db_ref)

        col_head = lax.broadcasted_iota(jnp.int32, (CHUNK, AW), 1) // HEAD_DIM
        v = v_ref[...].astype(MXU_DTYPE)
        dy = dy_ref[...].astype(F32)
        mixed = _sgu_mixed(v, w_ref, b_ref, col_head, H)
        du_ref[...] = (dy * mixed).astype(du_ref.dtype)
        dm = dy * u_ref[...]
        db_ref[...] += dm
        dm_c = dm.astype(MXU_DTYPE)
        dv = jnp.zeros((CHUNK, AW), F32)
        for h in range(H):
            sel = col_head == h
            dv = dv + jnp.where(sel, jnp.dot(wt_ref[h], dm_c, preferred_element_type=F32), 0.0)
            dm_h = jnp.where(sel, dm, 0.0).astype(MXU_DTYPE)
            dw_ref[h] += lax.dot_general(dm_h, v, (((1,), (1,)), ((), ())), preferred_element_type=F32)
        dv_ref[...] = dv.astype(dv_ref.dtype)

    const3 = lambda c: (0, 0, 0)
    blk = pl.BlockSpec((CHUNK, AW), lambda c: (c, 0))
    return pl.pallas_call(
        body, name=name, grid=(T // CHUNK,),
        in_specs=[blk, blk, pl.BlockSpec((CHUNK, AW), lambda c: (c, 1)),
                  pl.BlockSpec((H, CHUNK, CHUNK), const3), pl.BlockSpec((H, CHUNK, CHUNK), const3),
                  pl.BlockSpec((CHUNK, AW), lambda c: (0, 0))],
        out_specs=[blk, blk, pl.BlockSpec((H, CHUNK, CHUNK), const3), pl.BlockSpec((CHUNK, AW), lambda c: (0, 0))],
        out_shape=[_sds((T, AW), MXU_DTYPE), _sds((T, AW), MXU_DTYPE), _sds((H, CHUNK, CHUNK), F32), _sds((CHUNK, AW), F32)],
        compiler_params=_cparams("arbitrary"),
    )(dymix, p, p, w_tril, w_tril_t, bmat)


def _shift_down(z, s, row):
    return jnp.where(row >= s, pltpu.roll(z, s, 0), 0.0)


def _shift_up(z, s, row, T):
    return jnp.where(row < T - s, pltpu.roll(z, T - s, 0), 0.0)


def _conv_fwd(p, w_conv, AW, name):
    T = p.shape[0]
    BW = w_conv.shape[1]
    nb = BW // LANE
    b0 = 2 * AW // LANE

    def body(b_ref, c_ref, x_ref, w_ref, y_ref):
        row = lax.broadcasted_iota(jnp.int32, (T, LANE), 0)
        z = c_ref[...] * x_ref[...]
        w0, w1, w2 = w_ref[0:1, :], w_ref[1:2, :], w_ref[2:3, :]
        conv = w2 * z + w1 * _shift_down(z, 1, row) + w0 * _shift_down(z, 2, row)
        y_ref[...] = (b_ref[...] * conv).astype(y_ref.dtype)

    return pl.pallas_call(
        body, name=name, grid=(nb,),
        in_specs=[pl.BlockSpec((T, LANE), lambda j: (0, b0 + j)), pl.BlockSpec((T, LANE), lambda j: (0, b0 + nb + j)),
                  pl.BlockSpec((T, LANE), lambda j: (0, b0 + 2 * nb + j)), pl.BlockSpec((CONV_WIDTH, LANE), lambda j: (0, j))],
        out_specs=pl.BlockSpec((T, LANE), lambda j: (0, j)),
        out_shape=_sds((T, BW), MXU_DTYPE),
        compiler_params=_cparams("parallel"),
    )(p, p, p, w_conv)


def _conv_bwd(dymix, p, w_conv, AW, name):
    T = p.shape[0]
    BW = w_conv.shape[1]
    nb = BW // LANE
    b0 = 2 * AW // LANE
    y0 = AW // LANE

    def body(dy_ref, b_ref, c_ref, x_ref, w_ref, db_ref, dc_ref, dxb_ref, dw_ref):
        row = lax.broadcasted_iota(jnp.int32, (T, LANE), 0)
        cv, xv, dy = c_ref[...], x_ref[...], dy_ref[...].astype(F32)
        w0, w1, w2 = w_ref[0:1, :], w_ref[1:2, :], w_ref[2:3, :]
        z = cv * xv
        z1 = _shift_down(z, 1, row)
        z2 = _shift_down(z, 2, row)
        conv = w2 * z + w1 * z1 + w0 * z2
        db_ref[...] = (dy * conv).astype(db_ref.dtype)
        dconv = dy * b_ref[...]
        dz = w2 * dconv + w1 * _shift_up(dconv, 1, row, T) + w0 * _shift_up(dconv, 2, row, T)
        dc_ref[...] = (dz * xv).astype(dc_ref.dtype)
        dxb_ref[...] = (dz * cv).astype(dxb_ref.dtype)
        dw_ref[0:1, :] = jnp.sum(dconv * z2, axis=0, keepdims=True)
        dw_ref[1:2, :] = jnp.sum(dconv * z1, axis=0, keepdims=True)
        dw_ref[2:3, :] = jnp.sum(dconv * z, axis=0, keepdims=True)

    col = lambda j: (0, j)
    return pl.pallas_call(
        body, name=name, grid=(nb,),
        in_specs=[pl.BlockSpec((T, LANE), lambda j: (0, y0 + j)),
                  pl.BlockSpec((T, LANE), lambda j: (0, b0 + j)), pl.BlockSpec((T, LANE), lambda j: (0, b0 + nb + j)),
                  pl.BlockSpec((T, LANE), lambda j: (0, b0 + 2 * nb + j)), pl.BlockSpec((CONV_WIDTH, LANE), col)],
        out_specs=[pl.BlockSpec((T, LANE), col)] * 3 + [pl.BlockSpec((CONV_WIDTH, LANE), col)],
        out_shape=[_sds((T, BW), MXU_DTYPE)] * 3 + [_sds((CONV_WIDTH, BW), F32)],
        compiler_params=_cparams("parallel"),
    )(dymix, p, p, p, w_conv)


def _head_sum(x, col_head, n_heads):
    out = jnp.zeros_like(x)
    for h in range(n_heads):
        sel = col_head == h
        out = jnp.where(sel, jnp.sum(jnp.where(sel, x, 0.0), axis=-1, keepdims=True), out)
    return out


def _same_head(width):
    assert width == 2 * HEAD_DIM
    return lax.broadcasted_iota(jnp.int32, (1, width), 1) < HEAD_DIM


def _head_sum2(x, first):
    s0 = jnp.sum(jnp.where(first, x, 0.0), axis=-1, keepdims=True)
    s1 = jnp.sum(jnp.where(first, 0.0, x), axis=-1, keepdims=True)
    return jnp.where(first, s0, s1)


def _head_norm(x, g, first):
    r = lax.rsqrt(_head_sum2(x * x, first) * (1.0 / HEAD_DIM) + EPS)
    return x * r * g, r


def _head_norm_bwd(dy, x, g, r, first):
    gdy = dy * g
    mean_xg = _head_sum2(x * gdy, first) * (1.0 / HEAD_DIM)
    return r * gdy - x * (r * r * r) * mean_xg, dy * x * r


ATT_SPAN_MIN = 512
ATT_FWD_UNROLL = 4
ATT_BWD_UNROLL = 2
HEADS_PER_LANES = LANE // HEAD_DIM


def _attn_geometry(T, d):
    m = max(1, ATT_SPAN_MIN // (ATT_BLK * d))
    return m, ATT_BLK * d * m, ATT_BLK * d, T // (ATT_BLK * d)


def _rows(ref, start, d):
    return ref[pl.ds(start, ATT_BLK, stride=d), :] if d > 1 else ref[pl.ds(start, ATT_BLK), :]


def _set_rows(ref, start, d, value):
    if d > 1:
        ref[pl.ds(start, ATT_BLK, stride=d), :] = value
    else:
        ref[pl.ds(start, ATT_BLK), :] = value


def _for_each_block(d, m, task, unroll):
    for j in range(m):
        if d == 1:
            task(0, j)
        else:
            lax.fori_loop(0, d, lambda r, carry, j=j: (task(r, j), carry)[1], 0, unroll=min(unroll, d))


def _head_slices():
    return [slice(h * HEAD_DIM, (h + 1) * HEAD_DIM) for h in range(HEADS_PER_LANES)]


def _qk_norm_fwd(p, gains, q_start, PW, name, tr=512):
    T = p.shape[0]
    n = gains.shape[1] // PW
    c0 = q_start // PW

    def body(*refs):
        x_refs, g_ref, out_ref = refs[:n], refs[n], refs[n + 1]
        first = _same_head(LANE)
        for i in range(n):
            for c in range(0, PW, LANE):
                lo = i * PW + c
                y, _ = _head_norm(x_refs[i][:, c:c + LANE], g_ref[:, lo:lo + LANE], first)
                out_ref[:, lo:lo + LANE] = y

    return pl.pallas_call(
        body, name=name, grid=(T // tr,),
        in_specs=[pl.BlockSpec((tr, PW), lambda i, j=j: (i, c0 + j)) for j in range(n)]
        + [pl.BlockSpec((1, n * PW), lambda i: (0, 0))],
        out_specs=pl.BlockSpec((tr, n * PW), lambda i: (i, 0)), out_shape=_sds((T, n * PW), F32),
        compiler_params=_cparams("parallel"),
    )(*([p] * n), gains)


def _qk_norm_bwd(dns, p, gains, q_start, PW, name, tr=512):
    T = p.shape[0]
    n = len(dns)
    c0 = q_start // PW

    def body(*refs):
        d_refs, x_refs, g_ref, out_ref, acc_ref = refs[:n], refs[n:2 * n], refs[2 * n], refs[2 * n + 1], refs[2 * n + 2]

        @pl.when(pl.program_id(0) == 0)
        def _():
            acc_ref[...] = jnp.zeros_like(acc_ref)

        first = _same_head(LANE)
        for i in range(n):
            for c in range(0, PW, LANE):
                lo = i * PW + c
                x, gv = x_refs[i][:, c:c + LANE], g_ref[:, lo:lo + LANE]
                _, r = _head_norm(x, gv, first)
                dx, g_part = _head_norm_bwd(d_refs[i][:, c:c + LANE], x, gv, r, first)
                out_ref[:, lo:lo + LANE] = dx.astype(out_ref.dtype)
                acc_ref[0:1, lo:lo + LANE] += jnp.sum(g_part, axis=0, keepdims=True)

    return pl.pallas_call(
        body, name=name, grid=(T // tr,),
        in_specs=[pl.BlockSpec((tr, PW), lambda i: (i, 0))] * n
        + [pl.BlockSpec((tr, PW), lambda i, j=j: (i, c0 + j)) for j in range(n)]
        + [pl.BlockSpec((1, n * PW), lambda i: (0, 0))],
        out_specs=[pl.BlockSpec((tr, n * PW), lambda i: (i, 0)), pl.BlockSpec((8, n * PW), lambda i: (0, 0))],
        out_shape=[_sds((T, n * PW), MXU_DTYPE), _sds((8, n * PW), F32)],
        compiler_params=_cparams("arbitrary"),
    )(*dns, *([p] * n), gains)


def _attn_fwd(p, qkn, g, d, PW, q_start, name):
    T = p.shape[0]
    B, W = ATT_BLK, LANE
    m, span, group, _ = _attn_geometry(T, d)
    c_q = g * PW // W
    c_k = c_q + qkn.shape[1] // 2 // W
    c_v = (q_start + g * PW) // W + 6 * PW // W
    scale = HEAD_DIM ** -0.5

    def body(q_ref, k_ref, v_ref, kp_ref, vp_ref, o_ref, lse_ref):
        n = pl.program_id(1)
        qi = lax.broadcasted_iota(jnp.int32, (B, 2 * B), 0)
        kj = lax.broadcasted_iota(jnp.int32, (B, 2 * B), 1)
        band = (kj >= qi) & (kj <= qi + B)

        def task(r, j):
            cur = j * group + r
            if j == 0:
                kp, vp = _rows(kp_ref, r, d), _rows(vp_ref, r, d)
            else:
                kp, vp = _rows(k_ref, cur - group, d), _rows(v_ref, cur - group, d)
            mask = band & ((n * m + j > 0) | (kj >= B))
            qn = _rows(q_ref, cur, d).astype(MXU_DTYPE)
            kn = jnp.concatenate([kp, _rows(k_ref, cur, d)], axis=0).astype(MXU_DTYPE)
            vcat = jnp.concatenate([vp, _rows(v_ref, cur, d)], axis=0).astype(MXU_DTYPE)
            o_parts, lse_parts = [], []
            for sl in _head_slices():
                s = lax.dot_general(qn[:, sl], kn[:, sl], (((1,), (1,)), ((), ())), preferred_element_type=F32) * scale
                s = jnp.where(mask, s, NEG_INF)
                mx = jnp.max(s, axis=-1, keepdims=True)
                e = jnp.exp(s - mx)
                den = jnp.sum(e, axis=-1, keepdims=True)
                o_parts.append(jnp.dot(e.astype(MXU_DTYPE), vcat[:, sl], preferred_element_type=F32) / den)
                lse_parts.append(jnp.broadcast_to(mx + jnp.log(den), (B, HEAD_DIM)))
            _set_rows(o_ref, cur, d, jnp.concatenate(o_parts, axis=1))
            _set_rows(lse_ref, cur, d, jnp.concatenate(lse_parts, axis=1))

        _for_each_block(d, m, task, ATT_FWD_UNROLL)

    main = lambda c0: pl.BlockSpec((span, W), lambda hp, n: (n, c0 + hp))
    prev = lambda c0: pl.BlockSpec((group, W), lambda hp, n: (jnp.maximum(n * m - 1, 0), c0 + hp))
    out_blk = pl.BlockSpec((span, W), lambda hp, n: (n, hp))
    return pl.pallas_call(
        body, name=name, grid=(PW // W, T // span),
        in_specs=[main(c_q), main(c_k), main(c_v), prev(c_k), prev(c_v)],
        out_specs=[out_blk, out_blk],
        out_shape=[_sds((T, PW), F32), _sds((T, PW), F32)],
        compiler_params=_cparams("parallel", "parallel"),
    )(qkn, qkn, p, qkn, p)


def _attn_bwd(p, qkn, lse, do, corr, g, d, PW, q_start, name):
    T = p.shape[0]
    B, W = ATT_BLK, LANE
    m, span, group, n_blocks = _attn_geometry(T, d)
    c_q = g * PW // W
    c_k = c_q + qkn.shape[1] // 2 // W
    c_v = (q_start + g * PW) // W + 6 * PW // W
    scale = HEAD_DIM ** -0.5
    nt = (((1,), (1,)), ((), ()))
    tn = (((0,), (0,)), ((), ()))

    def body(q_ref, k_ref, v_ref, do_ref, l_ref, c_ref, kp_ref, vp_ref, qx_ref, dox_ref, lx_ref, cx_ref,
             dq_ref, dk_ref, dv_ref):
        n = pl.program_id(1)
        i1 = lax.broadcasted_iota(jnp.int32, (B, B), 0)
        j1 = lax.broadcasted_iota(jnp.int32, (B, B), 1)
        i2 = lax.broadcasted_iota(jnp.int32, (2 * B, B), 0)
        j2 = lax.broadcasted_iota(jnp.int32, (2 * B, B), 1)

        def task(r, j):
            cur = j * group + r
            blk = n * m + j
            q_c, k_c, v_c = _rows(q_ref, cur, d), _rows(k_ref, cur, d), _rows(v_ref, cur, d)
            do_c, l_c, c_c = _rows(do_ref, cur, d), _rows(l_ref, cur, d), _rows(c_ref, cur, d)
            if j == 0:
                k_p, v_p = _rows(kp_ref, r, d), _rows(vp_ref, r, d)
            else:
                k_p, v_p = _rows(k_ref, cur - group, d), _rows(v_ref, cur - group, d)
            if j == m - 1:
                nxt = [_rows(ref, r, d) for ref in (qx_ref, dox_ref, lx_ref, cx_ref)]
            else:
                nxt = [_rows(ref, cur + group, d) for ref in (q_ref, do_ref, l_ref, c_ref)]
            q_x, do_x, l_x, c_x = nxt
            kn_c, kn_p, v_c, v_p = (a.astype(MXU_DTYPE) for a in (k_c, k_p, v_c, v_p))
            qn_c = q_c.astype(MXU_DTYPE)
            qn_cat = jnp.concatenate([qn_c, q_x.astype(MXU_DTYPE)], axis=0)
            do_cb = do_c.astype(MXU_DTYPE)
            do_cat = jnp.concatenate([do_cb, do_x.astype(MXU_DTYPE)], axis=0)
            l_cat = jnp.concatenate([l_c, l_x], axis=0)
            c_cat = jnp.concatenate([c_c, c_x], axis=0)
            mask_p = (j1 >= i1) & (blk > 0)
            mask_c = ((i2 < B) & (j2 <= i2)) | ((i2 >= B) & (j2 >= i2 - B) & (blk + 1 < n_blocks))
            dqn, dkn, dv = [], [], []
            for h, sl in enumerate(_head_slices()):
                lane = slice(h * HEAD_DIM, h * HEAD_DIM + 1)
                s_p = lax.dot_general(qn_c[:, sl], kn_p[:, sl], nt, preferred_element_type=F32) * scale
                pr_p = jnp.where(mask_p, jnp.exp(s_p - l_c[:, lane]), 0.0)
                dp_p = lax.dot_general(do_cb[:, sl], v_p[:, sl], nt, preferred_element_type=F32)
                ds_p = (pr_p * (dp_p + c_c[:, lane]) * scale).astype(MXU_DTYPE)
                s_c = lax.dot_general(qn_cat[:, sl], kn_c[:, sl], nt, preferred_element_type=F32) * scale
                pr_c = jnp.where(mask_c, jnp.exp(s_c - l_cat[:, lane]), 0.0)
                dp_c = lax.dot_general(do_cat[:, sl], v_c[:, sl], nt, preferred_element_type=F32)
                ds_c = (pr_c * (dp_c + c_cat[:, lane]) * scale).astype(MXU_DTYPE)
                dqn.append(jnp.dot(ds_p, kn_p[:, sl], preferred_element_type=F32)
                           + jnp.dot(ds_c[:B], kn_c[:, sl], preferred_element_type=F32))
                dkn.append(lax.dot_general(ds_c, qn_cat[:, sl], tn, preferred_element_type=F32))
                dv.append(lax.dot_general(pr_c.astype(MXU_DTYPE), do_cat[:, sl], tn, preferred_element_type=F32))
            _set_rows(dq_ref, cur, d, jnp.concatenate(dqn, axis=1))
            _set_rows(dk_ref, cur, d, jnp.concatenate(dkn, axis=1))
            _set_rows(dv_ref, cur, d, jnp.concatenate(dv, axis=1))

        _for_each_block(d, m, task, ATT_BWD_UNROLL)

    main = lambda c0: pl.BlockSpec((span, W), lambda hp, n: (n, c0 + hp))
    prev = lambda c0: pl.BlockSpec((group, W), lambda hp, n: (jnp.maximum(n * m - 1, 0), c0 + hp))
    nxt = lambda c0: pl.BlockSpec((group, W), lambda hp, n: (jnp.minimum((n + 1) * m, n_blocks - 1), c0 + hp))
    own = pl.BlockSpec((span, W), lambda hp, n: (n, hp))
    return pl.pallas_call(
        body, name=name, grid=(PW // W, T // span),
        in_specs=[main(c_q), main(c_k), main(c_v), main(0), main(0), main(0), prev(c_k), prev(c_v),
                  nxt(c_q), nxt(0), nxt(0), nxt(0)],
        out_specs=[own, own, own],
        out_shape=[_sds((T, PW), F32)] * 3,
        compiler_params=_cparams("parallel", "parallel"),
    )(qkn, qkn, p, do, lse, corr, qkn, p, qkn, do, lse, corr)


def _softmax3(lses):
    mx = jnp.maximum(jnp.maximum(lses[0], lses[1]), lses[2])
    ex = [jnp.exp(l - mx) for l in lses]
    inv = 1.0 / (ex[0] + ex[1] + ex[2])
    return [e * inv for e in ex]


def _mix_fwd(os_, lses, name, tr=512):
    T, PW = os_[0].shape

    def body(o0, o1, o2, l0, l1, l2, y_ref):
        alpha = _softmax3([l0[...], l1[...], l2[...]])
        for g, o_ref in enumerate((o0, o1, o2)):
            y_ref[:, g * PW:(g + 1) * PW] = (o_ref[...] * alpha[g]).astype(y_ref.dtype)

    blk = pl.BlockSpec((tr, PW), lambda i: (i, 0))
    return pl.pallas_call(
        body, name=name, grid=(T // tr,),
        in_specs=[blk] * 6, out_specs=pl.BlockSpec((tr, 3 * PW), lambda i: (i, 0)),
        out_shape=_sds((T, 3 * PW), MXU_DTYPE),
        compiler_params=_cparams("parallel"),
    )(*os_, *lses)


def _mix_bwd(dymix, os_, lses, c_start, name, tr=512):
    T, PW = os_[0].shape
    HP = PW // HEAD_DIM
    c0 = c_start // PW

    def body(d0, d1, d2, o0, o1, o2, l0, l1, l2, do0, do1, do2, dl0, dl1, dl2):
        col_head = lax.broadcasted_iota(jnp.int32, (tr, PW), 1) // HEAD_DIM
        alpha = _softmax3([l0[...], l1[...], l2[...]])
        dys = [d0[...].astype(F32), d1[...].astype(F32), d2[...].astype(F32)]
        dots = [_head_sum(dy * o_ref[...], col_head, HP) for dy, o_ref in zip(dys, (o0, o1, o2))]
        mean_dot = alpha[0] * dots[0] + alpha[1] * dots[1] + alpha[2] * dots[2]
        for g, (do_ref, dl_ref) in enumerate(((do0, dl0), (do1, dl1), (do2, dl2))):
            do_ref[...] = dys[g] * alpha[g]
            dl_ref[...] = -alpha[g] * mean_dot

    blk = pl.BlockSpec((tr, PW), lambda i: (i, 0))
    dy_specs = [pl.BlockSpec((tr, PW), lambda i, g=g: (i, c0 + g)) for g in range(3)]
    outs = pl.pallas_call(
        body, name=name, grid=(T // tr,),
        in_specs=dy_specs + [blk] * 6, out_specs=[blk] * 6,
        out_shape=[_sds((T, PW), F32)] * 6,
        compiler_params=_cparams("parallel"),
    )(dymix, dymix, dymix, *os_, *lses)
    return outs[:3], outs[3:]


def _adamw_math(w, g, m, v):
    m2 = ADAM_B1 * m + (1.0 - ADAM_B1) * g
    v2 = ADAM_B2 * v + (1.0 - ADAM_B2) * (g * g)
    m_hat = m2 / (1.0 - ADAM_B1 ** ADAM_STEP)
    v_hat = v2 / (1.0 - ADAM_B2 ** ADAM_STEP)
    delta = -ADAM_LR * (m_hat / (jnp.sqrt(v_hat) + ADAM_EPS) + ADAM_WD * w)
    return delta, m2, v2


def _adamw_layer(layer, w, m, v, own, landed, me, prev, name, tr=256):
    _, R, C = w.shape
    tr = next(t for t in range(min(tr, R) // 16 * 16, 0, -16) if R % t == 0)

    def body(me_ref, w_ref, m_ref, v_ref, own_ref, land_ref, *rest):
        g_ref, d_ref, m2_ref, v2_ref = rest[-4:]
        g = own_ref[...].astype(F32)
        for j in range(N_PEER):
            g = g + land_ref[j].astype(F32)
        delta, m2, v2 = _adamw_math(w_ref[...], g, m_ref[...], v_ref[...])
        g_ref[...] = g
        d_ref[...] = delta
        m2_ref[...] = m2
        v2_ref[...] = v2

    lay = pl.BlockSpec((None, tr, C), lambda i, me_ref: (layer, i, 0))
    in_specs = [lay, lay, lay, pl.BlockSpec((None, tr, C), lambda i, me_ref: (me_ref[0], i, 0)),
                pl.BlockSpec((N_PEER, tr, C), lambda i, me_ref: (0, i, 0))]
    args = [me, w, m, v, own, landed]
    aliases = {}
    if prev is not None:
        in_specs += [pl.BlockSpec(memory_space=pl.ANY)] * 4
        args += list(prev)
        aliases = {6 + i: i for i in range(4)}
    return pl.pallas_call(
        body, name=name,
        grid_spec=pltpu.PrefetchScalarGridSpec(num_scalar_prefetch=1, grid=(R // tr,), in_specs=in_specs, out_specs=[lay] * 4),
        out_shape=[_sds(w.shape, F32)] * 4,
        input_output_aliases=aliases,
        compiler_params=_cparams("parallel"),
    )(*args)


def _sum_parts(parts, name):
    _, R, C = parts.shape

    def body(p_ref, out_ref):
        g = p_ref[0]
        for j in range(1, N_DEV):
            g = g + p_ref[j]
        out_ref[...] = g

    return pl.pallas_call(
        body, name=name, grid=(1,),
        in_specs=[pl.BlockSpec((N_DEV, R, C), lambda i: (0, 0, 0))], out_specs=pl.BlockSpec((R, C), lambda i: (0, 0)),
        out_shape=_sds((R, C), F32), compiler_params=_cparams("arbitrary"),
    )(parts)


def _adamw_small(ws, gs, ms, vs, name):
    n = len(ws)

    def body(*refs):
        ins, outs = refs[:4 * n], refs[4 * n:]
        for i in range(n):
            delta, m2, v2 = _adamw_math(ins[i][...], ins[n + i][...], ins[2 * n + i][...], ins[3 * n + i][...])
            outs[i][...] = delta
            outs[n + i][...] = m2
            outs[2 * n + i][...] = v2

    outs = pl.pallas_call(
        body, name=name, out_shape=[_sds(w.shape, F32) for w in ws] * 3,
    )(*ws, *gs, *ms, *vs)
    return outs[:n], outs[n:2 * n], outs[2 * n:]


def _pack(arrays, rows_multiple=8):
    flat = []
    for a in arrays:
        a = a.reshape(-1).astype(F32)
        flat.append(jnp.pad(a, (0, (-a.shape[0]) % LANE)))
    flat = jnp.concatenate(flat)
    flat = jnp.pad(flat, (0, (-flat.shape[0]) % (LANE * rows_multiple)))
    return flat.reshape(-1, LANE)


def _unpack(packed, shapes):
    flat = packed.reshape(-1)
    out, off = [], 0
    for s in shapes:
        size = 1
        for dim in s:
            size *= dim
        out.append(flat[off:off + size].reshape(s))
        off += size + (-size) % LANE
    return out


def _layer_fwd(x, wts, getw, dims, dep=None):
    AW, BW, PW, DP = dims["AW"], dims["BW"], dims["PW"], dims["DP"]
    q_start = 2 * AW + 3 * BW
    h, r1 = _rmsnorm_fwd(x, wts["attn_norm"], "rmsnorm_fwd", dep=dep)
    p = _mm_nt(h, getw("w_in", h), "proj_in", tm=1024, tn=512)
    y_a = _sgu_fwd(p, wts["sgu_tril"], wts["sgu_bmat"], "sgu_fwd")
    y_b = _conv_fwd(p, getw("conv_w", y_a), AW, "conv_fwd")
    qkn = _qk_norm_fwd(p, wts["qk_gain"], q_start, PW, "qk_norm_fwd")
    os_, lses = [], []
    for g, d in enumerate(DILATIONS):
        o, lse = _attn_fwd(p, qkn, g, d, PW, q_start, "attn_fwd_%d" % d)
        os_.append(o)
        lses.append(lse)
    y_c = _mix_fwd(os_, lses, "mix_fwd")
    ymix = jnp.concatenate([y_a, y_b, y_c], axis=1)
    x1 = _mm_nn(ymix, getw("w_out", ymix), "proj_out", residual=x, tm=1024, tn=1024)
    h2, r2 = _rmsnorm_fwd(x1, wts["mlp_norm"], "rmsnorm_fwd")
    a, hid = _mm_nn(h2, getw("w_mlp_in", h2), "mlp_in", out_dtype=MXU_DTYPE, relu2=True, tm=1024, tn=1024)
    x2 = _mm_nn(hid, getw("w_mlp_out", hid), "mlp_out", residual=x1, columns_first=True, tm=512, tn=1024)
    saved = dict(x=x, h=h, r1=r1, p=p, qkn=qkn, os=os_, lses=lses, ymix=ymix, x1=x1, h2=h2, r2=r2, a=a, hid=hid)
    return x2, saved


def _layer_bwd(dx, dxb, wts, getw, scatter, saved, dims):
    AW, BW, PW, DP = dims["AW"], dims["BW"], dims["PW"], dims["DP"]
    q_start = 2 * AW + 3 * BW
    D = dx.shape[1]
    g_w2 = _wgrad_wide_a(saved["hid"], dxb, "mlp_out_wgrad")
    token = scatter("w_mlp_out", g_w2.reshape(N_DEV, -1, D))
    da = _mm_nt(dxb, getw("w_mlp_out", None), "mlp_out_dgrad", out_dtype=MXU_DTYPE, relu2_pre=saved["a"], dep=token,
                tm=1024, tn=1024)
    g_w1 = _wgrad_wide_b(saved["h2"], da, "mlp_in_wgrad", groups=N_DEV)
    token = scatter("w_mlp_in", g_w1)
    dh2 = _mm_nt(da, getw("w_mlp_in", None), "mlp_in_dgrad", out_dtype=MXU_DTYPE, dep=token, tm=1024, tn=512)
    dx1, dx1b, g_mlp_norm = _rmsnorm_bwd(dh2, saved["x1"], wts["mlp_norm"], saved["r2"], dx, "rmsnorm_bwd")
    g_wout = _wgrad_wide_b(saved["ymix"], dx1b, "proj_out_wgrad")
    token = scatter("w_out", g_wout.reshape(N_DEV, -1, D))
    dymix = _mm_nt(dx1b, getw("w_out", None), "proj_out_dgrad", out_dtype=MXU_DTYPE, dep=token, tm=1024, tn=1024)
    p = saved["p"]
    du, dv, g_sgu_w, g_sgu_bmat = _sgu_bwd(dymix, p, wts["sgu_tril"], wts["sgu_tril_t"], wts["sgu_bmat"], "sgu_bwd")
    d_b, d_c, d_xb, g_conv = _conv_bwd(dymix, p, getw("conv_w", None), AW, "conv_bwd")
    dos, corrs = _mix_bwd(dymix, saved["os"], saved["lses"], AW + BW, "mix_bwd")
    dqns, dkns, dvs = [], [], []
    for g, d in enumerate(DILATIONS):
        dqn, dkn, dvv = _attn_bwd(p, saved["qkn"], saved["lses"][g], dos[g], corrs[g], g, d, PW, q_start,
                                  "attn_bwd_%d" % d)
        dqns.append(dqn)
        dkns.append(dkn)
        dvs.append(dvv.astype(MXU_DTYPE))
    dqk, g_qk = _qk_norm_bwd(dqns + dkns, p, wts["qk_gain"], q_start, PW, "qk_norm_bwd")
    g_q, g_k = (part.reshape(-1, HEAD_DIM).sum(0) for part in jnp.split(g_qk[0], 2))
    dp = jnp.concatenate([du, dv, d_b, d_c, d_xb, dqk] + dvs, axis=1)
    g_win_t = _wgrad_wide_a(dp, saved["h"], "proj_in_wgrad")
    token = scatter("w_in", g_win_t.reshape(N_DEV, DP // N_DEV, D))
    dh = _mm_nn(dp, getw("w_in", None), "proj_in_dgrad", out_dtype=MXU_DTYPE, dep=token, tm=1024, tn=512)
    dx0, dx0b, g_attn_norm = _rmsnorm_bwd(dh, saved["x"], wts["attn_norm"], saved["r1"], dx1, "rmsnorm_bwd")
    H = AW // HEAD_DIM
    tril = jnp.tril(jnp.ones((CHUNK, CHUNK), F32))
    small = [g_attn_norm.reshape(-1), g_sgu_w * tril, g_sgu_bmat.reshape(CHUNK, H, HEAD_DIM).sum(-1).T,
             g_conv, g_q, g_k, g_mlp_norm.reshape(-1)]
    return dx0, dx0b, small


def kernel(x, attn_norm, w_in, sgu_w, sgu_b, conv_w, q_norm, k_norm, w_out, mlp_norm, w_mlp_in, w_mlp_out, loss_target, m_attn_norm, m_w_in, m_sgu_w, m_sgu_b, m_conv_w, m_q_norm, m_k_norm, m_w_out, m_mlp_norm, m_w_mlp_in, m_w_mlp_out, v_attn_norm, v_w_in, v_sgu_w, v_sgu_b, v_conv_w, v_q_norm, v_k_norm, v_w_out, v_mlp_norm, v_w_mlp_in, v_w_mlp_out):
    n_layers = attn_norm.shape[0]
    T, D = x.shape[1], x.shape[2]
    H = sgu_w.shape[1]
    AW = H * HEAD_DIM
    BW = conv_w.shape[2] * N_DEV
    DP = w_in.shape[2] * N_DEV
    DMIX = w_out.shape[1] * N_DEV
    DFF = w_mlp_in.shape[2] * N_DEV
    PW = (DMIX - AW - BW) // 3
    HP = PW // HEAD_DIM
    dims = dict(AW=AW, BW=BW, PW=PW, DP=DP)
    me = 4 * lax.axis_index("x") + 2 * lax.axis_index("y") + lax.axis_index("c")

    big_names = ("w_in", "w_out", "w_mlp_in", "w_mlp_out")
    tr_in = lambda a: jnp.swapaxes(a, 1, 2)
    big_w = dict(zip(big_names, (tr_in(w_in), w_out, w_mlp_in, w_mlp_out)))
    big_m = dict(zip(big_names, (tr_in(m_w_in), m_w_out, m_w_mlp_in, m_w_mlp_out)))
    big_v = dict(zip(big_names, (tr_in(v_w_in), v_w_out, v_w_mlp_in, v_w_mlp_out)))

    keys = []
    for l in range(n_layers):
        keys += [(l, nm) for nm in big_names]
    keys.insert(1, (0, "conv_w"))
    srcs = [_pack([conv_w]) if nm == "conv_w" else big_w[nm][l].astype(MXU_DTYPE) for l, nm in keys]
    flights, gather_token = _exchange_start(srcs, [_own_in_place(s, me) for s in srcs], "gather", name="gather_start")
    arriving = dict(zip(keys, flights))
    forwarding = {}
    relayout = dict(
        w_in=lambda g: g.reshape(DP, D), w_out=lambda g: g.reshape(DMIX, D),
        w_mlp_in=lambda g: g, w_mlp_out=lambda g: g.reshape(DFF, D),
        conv_w=lambda g: jnp.stack([_unpack(g[j], [conv_w.shape])[0] for j in range(N_DEV)], axis=2).reshape(
            n_layers, CONV_WIDTH, BW))
    gathered = {}

    def forward(key, after):
        _, land = _exchange_wait(arriving[key], after, "gather", name="gather_arrive_%d_%s" % key)
        fl, token = _exchange_start(None, [land], "forward", name="gather_forward_%d_%s" % key)
        forwarding[key] = fl[0]
        return token

    def weight_getter(l):
        def getw(nm, after):
            key = (0, nm) if nm == "conv_w" else (l, nm)
            if key not in gathered:
                ahead = keys[keys.index(key):][:2]
                for k in ahead:
                    if k not in forwarding:
                        after = forward(k, after)
                _, land = _exchange_wait(forwarding[key], after, "forward", name="gather_wait_%d_%s" % key)
                gathered[key] = relayout[nm](land)
            return gathered[key][l] if nm == "conv_w" else gathered[key]
        return getw

    tril = jnp.tril(jnp.ones((CHUNK, CHUNK), F32))
    layers = []
    for l in range(n_layers):
        w_tril = sgu_w[l] * tril
        layers.append(dict(
            attn_norm=attn_norm[l][None], mlp_norm=mlp_norm[l][None],
            sgu_tril=w_tril.astype(MXU_DTYPE), sgu_tril_t=w_tril.transpose(0, 2, 1).astype(MXU_DTYPE),
            sgu_bmat=jnp.repeat(sgu_b[l].T, HEAD_DIM, axis=1),
            qk_gain=jnp.concatenate([jnp.tile(q_norm[l], 3 * HP), jnp.tile(k_norm[l], 3 * HP)])[None]))

    xs = x[0]
    saved = []
    for l in range(n_layers):
        xs, sv = _layer_fwd(xs, layers[l], weight_getter(l), dims, dep=gather_token if l == 0 else None)
        saved.append(sv)
    loss_blk, dx, dxb = _loss_and_grad(xs, loss_target[0], "loss")
    loss = lax.psum(loss_blk[0, 0], ("x", "y", "c"))

    scattering = {}

    def scatter_starter(l):
        def scatter(nm, partials):
            land = lax.empty((N_PEER,) + partials.shape[1:], partials.dtype)
            fl, tok = _exchange_start([partials], [land], "scatter", name="scatter_start_%d_%s" % (l, nm))
            scattering[(l, nm)] = fl[0]
            return tok
        return scatter

    small = [None] * n_layers
    for l in reversed(range(n_layers)):
        dx, dxb, small[l] = _layer_bwd(dx, dxb, layers[l], weight_getter(l), scatter_starter(l), saved[l], dims)

    small_shapes = [s.shape for s in small[0]]
    small_src = [_pack([s for l in range(n_layers) for s in small[l]])]
    small_flights, small_token = _exchange_start(small_src, [_own_in_place(s, me) for s in small_src], "gather_all",
                                                 name="small_start")
    grad_x = dx[None]

    me1 = me.astype(jnp.int32).reshape(1)
    res = {nm: None for nm in big_names}
    after = small_token
    for l in reversed(range(n_layers)):
        for nm in reversed(big_names):
            own, landed = _exchange_wait(scattering[(l, nm)], after, "scatter", name="scatter_wait_%d_%s" % (l, nm))
            res[nm] = _adamw_layer(l, big_w[nm], big_m[nm], big_v[nm], own, landed, me1, res[nm], "adamw_" + nm)
            after = res[nm][0]
    res["w_in"] = [tr_in(a) for a in res["w_in"]]
    big_out = [res[nm] for nm in big_names]

    _, gathered_small = _exchange_wait(small_flights[0], after, "gather_all", name="small_wait")
    summed = _unpack(_sum_parts(gathered_small, "sum_small"), small_shapes * n_layers)
    ns = len(small_shapes)
    g_small = [jnp.stack([summed[l * ns + i] for l in range(n_layers)]) for i in range(ns)]
    g_attn_norm, g_sgu_w, g_sgu_b, g_conv_full, g_q, g_k, g_mlp_norm = g_small
    cs = conv_w.shape[2]
    g_conv = lax.dynamic_slice_in_dim(g_conv_full, me * cs, cs, axis=2)
    sm_w = (attn_norm, sgu_w, sgu_b, conv_w, q_norm, k_norm, mlp_norm)
    sm_m = (m_attn_norm, m_sgu_w, m_sgu_b, m_conv_w, m_q_norm, m_k_norm, m_mlp_norm)
    sm_v = (v_attn_norm, v_sgu_w, v_sgu_b, v_conv_w, v_q_norm, v_k_norm, v_mlp_norm)
    sm_g = (g_attn_norm, g_sgu_w, g_sgu_b, g_conv, g_q, g_k, g_mlp_norm)
    sm_delta, sm_m2, sm_v2 = _adamw_small(sm_w, sm_g, sm_m, sm_v, "adamw_small")

    def ordered(small_list, big_kind):
        b = [big_out[i][big_kind] for i in range(4)]
        return [small_list[0], b[0], small_list[1], small_list[2], small_list[3], small_list[4], small_list[5],
                b[1], small_list[6], b[2], b[3]]

    return (loss, grad_x, *ordered(list(sm_g), 0), *ordered(sm_delta, 1), *ordered(sm_m2, 2), *ordered(sm_v2, 3))
```

```python
import jax
import jax.numpy as jnp
from jax import lax
from jax.experimental import pallas as pl
from jax.experimental.pallas import tpu as pltpu

N_DEV = 8
HEAD_DIM = 64
CHUNK = 128
ATT_BLK = 128
DILATIONS = (1, 4, 16)
CONV_WIDTH = 3
EPS = 1e-6
ADAM_LR = 0.001
ADAM_B1 = 0.9
ADAM_B2 = 0.999
ADAM_EPS = 1e-08
ADAM_WD = 0.01
ADAM_STEP = 10
MXU_DTYPE = jnp.bfloat16
F32 = jnp.float32
LANE = 128
VMEM_LIMIT_BYTES = 56 * 1024 * 1024
NEG_INF = float("-inf")


def _cparams(*sem):
    return pltpu.CompilerParams(dimension_semantics=sem, vmem_limit_bytes=VMEM_LIMIT_BYTES)


def _sds(shape, dtype):
    return jax.ShapeDtypeStruct(shape, dtype)


def _fit(n, tile):
    for t in range(min(tile, n) // LANE * LANE, 0, -LANE):
        if n % t == 0:
            return t
    return n


_HBM = pl.BlockSpec(memory_space=pltpu.HBM)
_SEM = pl.BlockSpec(memory_space=pltpu.SEMAPHORE)
_DATAFLOW = pltpu.SideEffectType.DATAFLOW_SIDE_EFFECTING
N_PEER = N_DEV - 1


def _mesh_pos():
    x, y, c = lax.axis_index("x"), lax.axis_index("y"), lax.axis_index("c")
    return x, y, c, 4 * x + 2 * y + c


OTHER_CHIPS = (4, 2, 6)
EXCHANGE_PEERS = dict(
    scatter=tuple(range(1, N_DEV)),
    gather_all=tuple(range(1, N_DEV)),
    gather=(1,) + OTHER_CHIPS,
    forward=OTHER_CHIPS)


def _remote_copies(src, land, send_sems, recv_sems, mode):
    x, y, c, me = _mesh_pos()
    copies = []
    for i, k in enumerate(EXCHANGE_PEERS[mode]):
        px = (1 - x) if (k & 4) else x
        py = (1 - y) if (k & 2) else y
        pc = (1 - c) if (k & 1) else c
        if mode == "scatter":
            src_ref, dst_ref, dev = src.at[4 * px + 2 * py + pc], land.at[i], (px, py, pc)
        elif mode == "forward":
            slot = 4 * px + 2 * py + c
            src_ref, dst_ref, dev = land.at[slot], land.at[slot], (x, y, 1 - c)
        else:
            src_ref, dst_ref, dev = src, land.at[me], (px, py, pc)
        copies.append(pltpu.make_async_remote_copy(
            src_ref=src_ref, dst_ref=dst_ref, send_sem=send_sems.at[i], recv_sem=recv_sems.at[i],
            device_id=dev, device_id_type=pl.DeviceIdType.MESH))
    return copies


def _own_in_place(src, me):
    land = lax.empty((N_DEV,) + src.shape, src.dtype)
    return lax.dynamic_update_slice(land, src[None], (me,) + (0,) * src.ndim)


def _exchange_start(srcs, lands, mode, name):
    n = len(lands)
    has_src = srcs is not None
    arrays = (list(srcs) if has_src else []) + list(lands)
    n_arr = len(arrays)
    n_copies = len(EXCHANGE_PEERS[mode])

    def body(*refs):
        src = refs[:n] if has_src else [None] * n
        land = refs[n_arr - n:n_arr]
        send, recv = refs[n_arr:n_arr + n], refs[n_arr + n:n_arr + 2 * n]
        token = refs[2 * n_arr + 2 * n]
        for t in range(n):
            for cp in _remote_copies(src[t], land[t], send[t], recv[t], mode):
                cp.start()
        token[...] = jnp.zeros_like(token)

    outs = pl.pallas_call(
        body, name=name,
        out_shape=([pltpu.SemaphoreType.DMA((n_copies,))] * (2 * n) + [pltpu.HBM(a.shape, a.dtype) for a in arrays]
                   + [_sds((8, LANE), F32)]),
        in_specs=[_HBM] * n_arr,
        out_specs=[_SEM] * (2 * n) + [_HBM] * n_arr + [pl.BlockSpec(memory_space=pltpu.VMEM)],
        input_output_aliases={i: 2 * n + i for i in range(n_arr)},
        compiler_params=pltpu.CompilerParams(has_side_effects=_DATAFLOW),
    )(*[pltpu.with_memory_space_constraint(a, pltpu.HBM) for a in arrays])
    thru = outs[2 * n:2 * n + n_arr]
    flights = [(outs[t], outs[n + t], thru[t] if has_src else None, thru[n_arr - n + t]) for t in range(n)]
    return flights, outs[2 * n + n_arr]


def _exchange_wait(flight, after, mode, name):
    send, recv, src, land = flight
    arrays = [land] if src is None else [src, land]
    n_arr = len(arrays)

    def body(*refs):
        src_ref = refs[0] if n_arr == 2 else None
        land_ref, send_ref, recv_ref = refs[n_arr - 1], refs[n_arr], refs[n_arr + 1]
        for cp in _remote_copies(src_ref, land_ref, send_ref, recv_ref, mode):
            cp.wait_send()
            cp.wait_recv()

    outs = pl.pallas_call(
        body, name=name, out_shape=[pltpu.HBM(a.shape, a.dtype) for a in arrays],
        in_specs=[_HBM] * n_arr + [_SEM, _SEM, pl.BlockSpec(memory_space=pl.ANY)], out_specs=[_HBM] * n_arr,
        input_output_aliases={i: i for i in range(n_arr)},
        compiler_params=pltpu.CompilerParams(has_side_effects=_DATAFLOW),
    )(*arrays, send, recv, after)
    return (None, outs[0]) if src is None else (outs[0], outs[1])


def _rmsnorm_fwd(x, g, name, dep=None, tr=512):
    T, D = x.shape

    def body(x_ref, g_ref, *rest):
        h_ref, r_ref = rest[-2:]
        xv = x_ref[...]
        r = lax.rsqrt(jnp.mean(xv * xv, axis=-1, keepdims=True) + EPS)
        h_ref[...] = (xv * r * g_ref[...]).astype(h_ref.dtype)
        r_ref[...] = r

    in_specs = [pl.BlockSpec((tr, D), lambda i: (i, 0)), pl.BlockSpec((1, D), lambda i: (0, 0))]
    args = [x, g]
    if dep is not None:
        in_specs.append(pl.BlockSpec(dep.shape, lambda i: (0, 0)))
        args.append(dep)
    return pl.pallas_call(
        body, name=name, grid=(T // tr,),
        in_specs=in_specs,
        out_specs=[pl.BlockSpec((tr, D), lambda i: (i, 0)), pl.BlockSpec((tr, 1), lambda i: (i, 0))],
        out_shape=[_sds((T, D), MXU_DTYPE), _sds((T, 1), F32)],
        compiler_params=_cparams("parallel"),
    )(*args)


def _rmsnorm_bwd(dh, x, g, r, dres, name, tr=256):
    T, D = x.shape

    def body(dh_ref, x_ref, g_ref, r_ref, dres_ref, dx_ref, dxb_ref, dg_ref):
        @pl.when(pl.program_id(0) == 0)
        def _():
            dg_ref[...] = jnp.zeros_like(dg_ref)

        dh_v, xv, rv = dh_ref[...].astype(F32), x_ref[...], r_ref[...]
        gdy = dh_v * g_ref[...]
        mean_xg = jnp.mean(xv * gdy, axis=-1, keepdims=True)
        dx = dres_ref[...] + rv * gdy - xv * (rv * rv * rv) * mean_xg
        dx_ref[...] = dx
        dxb_ref[...] = dx.astype(dxb_ref.dtype)
        dg_ref[...] += jnp.sum(dh_v * xv * rv, axis=0, keepdims=True)

    row = lambda i: (i, 0)
    return pl.pallas_call(
        body, name=name, grid=(T // tr,),
        in_specs=[pl.BlockSpec((tr, D), row), pl.BlockSpec((tr, D), row), pl.BlockSpec((1, D), lambda i: (0, 0)),
                  pl.BlockSpec((tr, 1), row), pl.BlockSpec((tr, D), row)],
        out_specs=[pl.BlockSpec((tr, D), row), pl.BlockSpec((tr, D), row), pl.BlockSpec((1, D), lambda i: (0, 0))],
        out_shape=[_sds((T, D), F32), _sds((T, D), MXU_DTYPE), _sds((1, D), F32)],
        compiler_params=_cparams("arbitrary"),
    )(dh, x, g, r, dres)


def _mm_nn(a, b, name, out_dtype=F32, residual=None, relu2=False, dep=None, columns_first=False, tm=512, tn=512):
    M, K = a.shape
    grouped = b.ndim == 3
    N = b.shape[0] * b.shape[2] if grouped else b.shape[1]
    tm, tn = _fit(M, tm), _fit(b.shape[2] if grouped else N, tn)
    tile = (lambda j, i: (i, j)) if columns_first else (lambda i, j: (i, j))
    b_mode = pl.Buffered(1 if columns_first else 2)
    if grouped:
        per = b.shape[2] // tn
        b_spec = pl.BlockSpec((None, K, tn), lambda *g: (tile(*g)[1] // per, 0, tile(*g)[1] % per), pipeline_mode=b_mode)
    else:
        b_spec = pl.BlockSpec((K, tn), lambda *g: (0, tile(*g)[1]), pipeline_mode=b_mode)
    n_out = 2 if relu2 else 1

    def body(*refs):
        a_ref, b_ref = refs[0], refs[1]
        r_ref = refs[2] if residual is not None else None
        outs = refs[2 + (residual is not None) + (dep is not None):]
        acc = jnp.dot(a_ref[...], b_ref[...], preferred_element_type=F32)
        if r_ref is not None:
            acc = acc + r_ref[...]
        outs[0][...] = acc.astype(outs[0].dtype)
        if relu2:
            rl = jnp.maximum(acc, 0.0)
            outs[1][...] = (rl * rl).astype(outs[1].dtype)

    out_blk = pl.BlockSpec((tm, tn), lambda *g: tile(*g))
    in_specs = [pl.BlockSpec((tm, K), lambda *g: (tile(*g)[0], 0)), b_spec]
    args = [a, b]
    if residual is not None:
        in_specs.append(out_blk)
        args.append(residual)
    if dep is not None:
        in_specs.append(pl.BlockSpec(dep.shape, lambda *g: (0, 0)))
        args.append(dep)
    out_shape = [_sds((M, N), out_dtype)]
    if relu2:
        out_shape.append(_sds((M, N), MXU_DTYPE))
    outs = pl.pallas_call(
        body, name=name, grid=(N // tn, M // tm) if columns_first else (M // tm, N // tn),
        in_specs=in_specs, out_specs=[out_blk] * n_out, out_shape=out_shape,
        compiler_params=_cparams("parallel", "parallel"),
    )(*args)
    return outs if relu2 else outs[0]


def _mm_nt(a, b, name, out_dtype=F32, relu2_pre=None, dep=None, a_single=False, tm=512, tn=512):
    M, K = a.shape
    grouped = b.ndim == 3
    N = b.shape[1] if grouped else b.shape[0]
    tm, tn = _fit(M, tm), _fit(N, tn)
    nt = (((1,), (1,)), ((), ()))
    if grouped:
        G, _, Kg = b.shape
        b_spec = pl.BlockSpec((G, tn, Kg), lambda i, j: (0, j, 0))
    else:
        b_spec = pl.BlockSpec((tn, K), lambda i, j: (j, 0))

    def body(*refs):
        a_ref, b_ref = refs[0], refs[1]
        p_ref = refs[2] if relu2_pre is not None else None
        out_ref = refs[2 + (relu2_pre is not None) + (dep is not None)]
        if grouped:
            acc = lax.dot_general(a_ref[:, 0:Kg], b_ref[0], nt, preferred_element_type=F32)
            for g in range(1, G):
                acc += lax.dot_general(a_ref[:, g * Kg:(g + 1) * Kg], b_ref[g], nt, preferred_element_type=F32)
        else:
            acc = lax.dot_general(a_ref[...], b_ref[...], nt, preferred_element_type=F32)
        if p_ref is not None:
            acc = acc * (2.0 * jnp.maximum(p_ref[...].astype(F32), 0.0))
        out_ref[...] = acc.astype(out_ref.dtype)

    out_blk = pl.BlockSpec((tm, tn), lambda i, j: (i, j))
    in_specs = [pl.BlockSpec((tm, K), lambda i, j: (i, 0), pipeline_mode=pl.Buffered(1 if a_single else 2)), b_spec]
    args = [a, b]
    if relu2_pre is not None:
        in_specs.append(out_blk)
        args.append(relu2_pre)
    if dep is not None:
        in_specs.append(pl.BlockSpec(dep.shape, lambda i, j: (0, 0)))
        args.append(dep)
    return pl.pallas_call(
        body, name=name, grid=(M // tm, N // tn),
        in_specs=in_specs, out_specs=out_blk, out_shape=_sds((M, N), out_dtype),
        compiler_params=_cparams("parallel", "parallel"),
    )(*args)


def _wgrad_wide_a(a, b, name, tm=512):
    T, M = a.shape
    N = b.shape[1]
    tm = _fit(M, tm)

    def body(a_ref, b_ref, out_ref):
        out_ref[...] = lax.dot_general(a_ref[...], b_ref[...], (((0,), (0,)), ((), ())),
                                       preferred_element_type=F32).astype(out_ref.dtype)

    return pl.pallas_call(
        body, name=name, grid=(M // tm,),
        in_specs=[pl.BlockSpec((T, tm), lambda i: (0, i)),
                  pl.BlockSpec((T, N), lambda i: (0, 0), pipeline_mode=pl.Buffered(1))],
        out_specs=pl.BlockSpec((tm, N), lambda i: (i, 0)), out_shape=_sds((M, N), MXU_DTYPE),
        compiler_params=_cparams("parallel"),
    )(a, b)


def _wgrad_wide_b(a, b, name, groups=None, tn=512, t_chunk=512):
    T, M = a.shape
    N = b.shape[1]
    tn = _fit(N if groups is None else N // groups, tn)
    t_chunk = _fit(T, t_chunk)

    def body(a_ref, b_ref, out_ref, at_ref):
        @pl.when(pl.program_id(0) == 0)
        def _():
            for c in range(0, T, t_chunk):
                at_ref[:, c:c + t_chunk] = a_ref[c:c + t_chunk, :].T

        out_ref[...] = jnp.dot(at_ref[...], b_ref[...], preferred_element_type=F32).astype(out_ref.dtype)

    if groups is None:
        out_spec = pl.BlockSpec((M, tn), lambda j: (0, j))
        out_shape = _sds((M, N), MXU_DTYPE)
    else:
        per = N // groups // tn
        out_spec = pl.BlockSpec((None, M, tn), lambda j: (j // per, 0, j % per))
        out_shape = _sds((groups, M, N // groups), MXU_DTYPE)
    return pl.pallas_call(
        body, name=name, grid=(N // tn,),
        in_specs=[pl.BlockSpec((T, M), lambda j: (0, 0), pipeline_mode=pl.Buffered(1)),
                  pl.BlockSpec((T, tn), lambda j: (0, j))],
        out_specs=out_spec, out_shape=out_shape,
        scratch_shapes=[pltpu.VMEM((M, T), MXU_DTYPE)],
        compiler_params=_cparams("arbitrary"),
    )(a, b)


def _loss_and_grad(y, target, name, tr=512):
    T, D = y.shape

    def body(y_ref, t_ref, loss_ref, dx_ref, dxb_ref):
        @pl.when(pl.program_id(0) == 0)
        def _():
            loss_ref[...] = jnp.zeros_like(loss_ref)

        err = y_ref[...] - t_ref[...]
        loss_ref[...] += 0.5 * jnp.sum(jnp.mean(err * err, axis=-1, keepdims=True), axis=0, keepdims=True)
        dx = err * (1.0 / D)
        dx_ref[...] = dx
        dxb_ref[...] = dx.astype(dxb_ref.dtype)

    row = lambda i: (i, 0)
    return pl.pallas_call(
        body, name=name, grid=(T // tr,),
        in_specs=[pl.BlockSpec((tr, D), row), pl.BlockSpec((tr, D), row)],
        out_specs=[pl.BlockSpec((8, LANE), lambda i: (0, 0)), pl.BlockSpec((tr, D), row), pl.BlockSpec((tr, D), row)],
        out_shape=[_sds((8, LANE), F32), _sds((T, D), F32), _sds((T, D), MXU_DTYPE)],
        compiler_params=_cparams("arbitrary"),
    )(y, target)


SGU_ROWS = 512


def _sgu_mixed(v, w_ref, b_ref, n_heads):
    parts = [jnp.dot(w_ref[h], v[:, h * HEAD_DIM:(h + 1) * HEAD_DIM], preferred_element_type=F32) for h in range(n_heads)]
    return jnp.concatenate(parts, axis=1) + b_ref[...]


def _sgu_fwd(p, w_tril, bmat, name):
    T = p.shape[0]
    H = w_tril.shape[0]
    AW = H * HEAD_DIM
    rows = _fit(T, SGU_ROWS)

    def body(u_ref, v_ref, w_ref, b_ref, y_ref):
        for c in range(0, rows, CHUNK):
            ch = pl.ds(c, CHUNK)
            mixed = _sgu_mixed(v_ref[ch, :].astype(MXU_DTYPE), w_ref, b_ref, H)
            y_ref[ch, :] = (u_ref[ch, :].astype(F32) * mixed).astype(y_ref.dtype)

    const3 = lambda c: (0, 0, 0)
    return pl.pallas_call(
        body, name=name, grid=(T // rows,),
        in_specs=[pl.BlockSpec((rows, AW), lambda c: (c, 0)), pl.BlockSpec((rows, AW), lambda c: (c, 1)),
                  pl.BlockSpec((H, CHUNK, CHUNK), const3), pl.BlockSpec((CHUNK, AW), lambda c: (0, 0))],
        out_specs=pl.BlockSpec((rows, AW), lambda c: (c, 0)),
        out_shape=_sds((T, AW), MXU_DTYPE),
        compiler_params=_cparams("parallel"),
    )(p, p, w_tril, bmat)


def _sgu_bwd(dymix, p, w_tril, w_tril_t, bmat, name):
    T = p.shape[0]
    H = w_tril.shape[0]
    AW = H * HEAD_DIM
    rows = _fit(T, SGU_ROWS)

    def body(dy_ref, u_ref, v_ref, w_ref, wt_ref, b_ref, du_ref, dv_ref, dw_ref, db_ref):
        @pl.when(pl.program_id(0) == 0)
        def _():
            dw_ref[...] = jnp.zeros_like(dw_ref)
            db_ref[...] = jnp.zeros_like(db_ref)

        for c in range(0, rows, CHUNK):
            ch = pl.ds(c, CHUNK)
            v = v_ref[ch, :].astype(MXU_DTYPE)
            dy = dy_ref[ch, :].astype(F32)
            du_ref[ch, :] = (dy * _sgu_mixed(v, w_ref, b_ref, H)).astype(du_ref.dtype)
            dm = dy * u_ref[ch, :].astype(F32)
            db_ref[...] += dm
            dm_c = dm.astype(MXU_DTYPE)
            dv = []
            for h in range(H):
                sl = slice(h * HEAD_DIM, (h + 1) * HEAD_DIM)
                dv.append(jnp.dot(wt_ref[h], dm_c[:, sl], preferred_element_type=F32))
                dw_ref[h] += lax.dot_general(dm_c[:, sl], v[:, sl], (((1,), (1,)), ((), ())), preferred_element_type=F32)
            dv_ref[ch, :] = jnp.concatenate(dv, axis=1).astype(dv_ref.dtype)

    const3 = lambda c: (0, 0, 0)
    blk = pl.BlockSpec((rows, AW), lambda c: (c, 0))
    return pl.pallas_call(
        body, name=name, grid=(T // rows,),
        in_specs=[blk, blk, pl.BlockSpec((rows, AW), lambda c: (c, 1)),
                  pl.BlockSpec((H, CHUNK, CHUNK), const3), pl.BlockSpec((H, CHUNK, CHUNK), const3),
                  pl.BlockSpec((CHUNK, AW), lambda c: (0, 0))],
        out_specs=[blk, blk, pl.BlockSpec((H, CHUNK, CHUNK), const3), pl.BlockSpec((CHUNK, AW), lambda c: (0, 0))],
        out_shape=[_sds((T, AW), MXU_DTYPE), _sds((T, AW), MXU_DTYPE), _sds((H, CHUNK, CHUNK), F32), _sds((CHUNK, AW), F32)],
        compiler_params=_cparams("arbitrary"),
    )(dymix, p, p, w_tril, w_tril_t, bmat)


def _shift_down(z, s, row):
    return jnp.where(row >= s, pltpu.roll(z, s, 0), 0.0)


def _shift_up(z, s, row, T):
    return jnp.where(row < T - s, pltpu.roll(z, T - s, 0), 0.0)


def _conv_fwd(p, w_conv, AW, name):
    T = p.shape[0]
    BW = w_conv.shape[1]
    nb = BW // LANE
    b0 = 2 * AW // LANE

    def body(b_ref, c_ref, x_ref, w_ref, y_ref):
        row = lax.broadcasted_iota(jnp.int32, (T, LANE), 0)
        z = c_ref[...].astype(F32) * x_ref[...].astype(F32)
        w0, w1, w2 = w_ref[0:1, :], w_ref[1:2, :], w_ref[2:3, :]
        conv = w2 * z + w1 * _shift_down(z, 1, row) + w0 * _shift_down(z, 2, row)
        y_ref[...] = (b_ref[...].astype(F32) * conv).astype(y_ref.dtype)

    return pl.pallas_call(
        body, name=name, grid=(nb,),
        in_specs=[pl.BlockSpec((T, LANE), lambda j: (0, b0 + j)), pl.BlockSpec((T, LANE), lambda j: (0, b0 + nb + j)),
                  pl.BlockSpec((T, LANE), lambda j: (0, b0 + 2 * nb + j)), pl.BlockSpec((CONV_WIDTH, LANE), lambda j: (0, j))],
        out_specs=pl.BlockSpec((T, LANE), lambda j: (0, j)),
        out_shape=_sds((T, BW), MXU_DTYPE),
        compiler_params=_cparams("parallel"),
    )(p, p, p, w_conv)


def _conv_bwd(dymix, p, w_conv, AW, name):
    T = p.shape[0]
    BW = w_conv.shape[1]
    nb = BW // LANE
    b0 = 2 * AW // LANE
    y0 = AW // LANE

    def body(dy_ref, b_ref, c_ref, x_ref, w_ref, db_ref, dc_ref, dxb_ref, dw_ref):
        row = lax.broadcasted_iota(jnp.int32, (T, LANE), 0)
        cv, xv, dy = c_ref[...].astype(F32), x_ref[...].astype(F32), dy_ref[...].astype(F32)
        w0, w1, w2 = w_ref[0:1, :], w_ref[1:2, :], w_ref[2:3, :]
        z = cv * xv
        z1 = _shift_down(z, 1, row)
        z2 = _shift_down(z, 2, row)
        conv = w2 * z + w1 * z1 + w0 * z2
        db_ref[...] = (dy * conv).astype(db_ref.dtype)
        dconv = dy * b_ref[...].astype(F32)
        dz = w2 * dconv + w1 * _shift_up(dconv, 1, row, T) + w0 * _shift_up(dconv, 2, row, T)
        dc_ref[...] = (dz * xv).astype(dc_ref.dtype)
        dxb_ref[...] = (dz * cv).astype(dxb_ref.dtype)
        dw_ref[0:1, :] = jnp.sum(dconv * z2, axis=0, keepdims=True)
        dw_ref[1:2, :] = jnp.sum(dconv * z1, axis=0, keepdims=True)
        dw_ref[2:3, :] = jnp.sum(dconv * z, axis=0, keepdims=True)

    col = lambda j: (0, j)
    return pl.pallas_call(
        body, name=name, grid=(nb,),
        in_specs=[pl.BlockSpec((T, LANE), lambda j: (0, y0 + j)),
                  pl.BlockSpec((T, LANE), lambda j: (0, b0 + j)), pl.BlockSpec((T, LANE), lambda j: (0, b0 + nb + j)),
                  pl.BlockSpec((T, LANE), lambda j: (0, b0 + 2 * nb + j)), pl.BlockSpec((CONV_WIDTH, LANE), col)],
        out_specs=[pl.BlockSpec((T, LANE), col)] * 3 + [pl.BlockSpec((CONV_WIDTH, LANE), col)],
        out_shape=[_sds((T, BW), MXU_DTYPE)] * 3 + [_sds((CONV_WIDTH, BW), F32)],
        compiler_params=_cparams("parallel"),
    )(dymix, p, p, p, w_conv)


def _head_sum(x, col_head, n_heads):
    out = jnp.zeros_like(x)
    for h in range(n_heads):
        sel = col_head == h
        out = jnp.where(sel, jnp.sum(jnp.where(sel, x, 0.0), axis=-1, keepdims=True), out)
    return out


def _same_head(width):
    assert width == 2 * HEAD_DIM
    return lax.broadcasted_iota(jnp.int32, (1, width), 1) < HEAD_DIM


def _head_sum2(x, first):
    s0 = jnp.sum(jnp.where(first, x, 0.0), axis=-1, keepdims=True)
    s1 = jnp.sum(jnp.where(first, 0.0, x), axis=-1, keepdims=True)
    return jnp.where(first, s0, s1)


def _head_norm(x, g, first):
    r = lax.rsqrt(_head_sum2(x * x, first) * (1.0 / HEAD_DIM) + EPS)
    return x * r * g, r


def _head_norm_bwd(dy, x, g, r, first):
    gdy = dy * g
    mean_xg = _head_sum2(x * gdy, first) * (1.0 / HEAD_DIM)
    return r * gdy - x * (r * r * r) * mean_xg, dy * x * r


ATT_SPAN_MIN = 512
ATT_FWD_UNROLL = 4
ATT_BWD_UNROLL = 2
HEADS_PER_LANES = LANE // HEAD_DIM


def _attn_geometry(T, d):
    m = max(1, ATT_SPAN_MIN // (ATT_BLK * d))
    return m, ATT_BLK * d * m, ATT_BLK * d, T // (ATT_BLK * d)


def _rows(ref, start, d):
    return ref[pl.ds(start, ATT_BLK, stride=d), :] if d > 1 else ref[pl.ds(start, ATT_BLK), :]


def _set_rows(ref, start, d, value):
    if d > 1:
        ref[pl.ds(start, ATT_BLK, stride=d), :] = value
    else:
        ref[pl.ds(start, ATT_BLK), :] = value


def _for_each_block(d, m, task, unroll):
    for j in range(m):
        if d == 1:
            task(0, j)
        else:
            lax.fori_loop(0, d, lambda r, carry, j=j: (task(r, j), carry)[1], 0, unroll=min(unroll, d))


def _head_slices():
    return [slice(h * HEAD_DIM, (h + 1) * HEAD_DIM) for h in range(HEADS_PER_LANES)]


def _qk_norm_fwd(p, gains, q_start, PW, name, tr=512):
    T = p.shape[0]
    n_norm = gains.shape[1] // PW
    n = n_norm * 3 // 2
    c0 = q_start // PW

    def body(*refs):
        x_refs, g_ref, out_ref = refs[:n], refs[n], refs[n + 1]
        first = _same_head(LANE)
        for i in range(n):
            for c in range(0, PW, LANE):
                lo = i * PW + c
                x = x_refs[i][:, c:c + LANE].astype(F32)
                out_ref[:, lo:lo + LANE] = _head_norm(x, g_ref[:, lo:lo + LANE], first)[0] if i < n_norm else x

    return pl.pallas_call(
        body, name=name, grid=(T // tr,),
        in_specs=[pl.BlockSpec((tr, PW), lambda i, j=j: (i, c0 + j)) for j in range(n)]
        + [pl.BlockSpec((1, n_norm * PW), lambda i: (0, 0))],
        out_specs=pl.BlockSpec((tr, n * PW), lambda i: (i, 0)), out_shape=_sds((T, n * PW), F32),
        compiler_params=_cparams("parallel"),
    )(*([p] * n), gains)


def _qk_norm_bwd(dns, p, gains, q_start, PW, name, tr=512):
    T = p.shape[0]
    n = len(dns)
    c0 = q_start // PW

    def body(*refs):
        d_refs, x_refs, g_ref, out_ref, acc_ref = refs[:n], refs[n:2 * n], refs[2 * n], refs[2 * n + 1], refs[2 * n + 2]

        @pl.when(pl.program_id(0) == 0)
        def _():
            acc_ref[...] = jnp.zeros_like(acc_ref)

        first = _same_head(LANE)
        for i in range(n):
            for c in range(0, PW, LANE):
                lo = i * PW + c
                x, gv = x_refs[i][:, c:c + LANE].astype(F32), g_ref[:, lo:lo + LANE]
                _, r = _head_norm(x, gv, first)
                dx, g_part = _head_norm_bwd(d_refs[i][:, c:c + LANE], x, gv, r, first)
                out_ref[:, lo:lo + LANE] = dx.astype(out_ref.dtype)
                acc_ref[0:1, lo:lo + LANE] += jnp.sum(g_part, axis=0, keepdims=True)

    return pl.pallas_call(
        body, name=name, grid=(T // tr,),
        in_specs=[pl.BlockSpec((tr, PW), lambda i: (i, 0))] * n
        + [pl.BlockSpec((tr, PW), lambda i, j=j: (i, c0 + j)) for j in range(n)]
        + [pl.BlockSpec((1, n * PW), lambda i: (0, 0))],
        out_specs=[pl.BlockSpec((tr, n * PW), lambda i: (i, 0)), pl.BlockSpec((8, n * PW), lambda i: (0, 0))],
        out_shape=[_sds((T, n * PW), MXU_DTYPE), _sds((8, n * PW), F32)],
        compiler_params=_cparams("arbitrary"),
    )(*dns, *([p] * n), gains)


def _attn_fwd(qkv, g, d, PW, name):
    T = qkv.shape[0]
    B, W = ATT_BLK, LANE
    m, span, group, _ = _attn_geometry(T, d)
    c_q = g * PW // W
    c_k, c_v = c_q + 3 * PW // W, c_q + 6 * PW // W
    scale = HEAD_DIM ** -0.5

    def body(q_ref, k_ref, v_ref, kp_ref, vp_ref, o_ref, lse_ref):
        n = pl.program_id(1)
        qi = lax.broadcasted_iota(jnp.int32, (B, 2 * B), 0)
        kj = lax.broadcasted_iota(jnp.int32, (B, 2 * B), 1)
        band = (kj >= qi) & (kj <= qi + B)

        def task(r, j):
            cur = j * group + r
            if j == 0:
                kp, vp = _rows(kp_ref, r, d), _rows(vp_ref, r, d)
            else:
                kp, vp = _rows(k_ref, cur - group, d), _rows(v_ref, cur - group, d)
            mask = band & ((n * m + j > 0) | (kj >= B))
            qn = _rows(q_ref, cur, d).astype(MXU_DTYPE)
            kn = jnp.concatenate([kp, _rows(k_ref, cur, d)], axis=0).astype(MXU_DTYPE)
            vcat = jnp.concatenate([vp, _rows(v_ref, cur, d)], axis=0).astype(MXU_DTYPE)
            o_parts, lse_parts = [], []
            for sl in _head_slices():
                s = lax.dot_general(qn[:, sl], kn[:, sl], (((1,), (1,)), ((), ())), preferred_element_type=F32) * scale
                s = jnp.where(mask, s, NEG_INF)
                mx = jnp.max(s, axis=-1, keepdims=True)
                e = jnp.exp(s - mx)
                den = jnp.sum(e, axis=-1, keepdims=True)
                o_parts.append(jnp.dot(e.astype(MXU_DTYPE), vcat[:, sl], preferred_element_type=F32) / den)
                lse_parts.append(jnp.broadcast_to(mx + jnp.log(den), (B, HEAD_DIM)))
            _set_rows(o_ref, cur, d, jnp.concatenate(o_parts, axis=1))
            _set_rows(lse_ref, cur, d, jnp.concatenate(lse_parts, axis=1))

        _for_each_block(d, m, task, ATT_FWD_UNROLL)

    main = lambda c0: pl.BlockSpec((span, W), lambda hp, n: (n, c0 + hp))
    prev = lambda c0: pl.BlockSpec((group, W), lambda hp, n: (jnp.maximum(n * m - 1, 0), c0 + hp))
    out_blk = pl.BlockSpec((span, W), lambda hp, n: (n, hp))
    return pl.pallas_call(
        body, name=name, grid=(PW // W, T // span),
        in_specs=[main(c_q), main(c_k), main(c_v), prev(c_k), prev(c_v)],
        out_specs=[out_blk, out_blk],
        out_shape=[_sds((T, PW), F32), _sds((T, PW), F32)],
        compiler_params=_cparams("parallel", "parallel"),
    )(qkv, qkv, qkv, qkv, qkv)


def _attn_bwd(qkv, lse, do, corr, g, d, PW, name):
    T = qkv.shape[0]
    B, W = ATT_BLK, LANE
    m, span, group, n_blocks = _attn_geometry(T, d)
    c_q = g * PW // W
    c_k, c_v = c_q + 3 * PW // W, c_q + 6 * PW // W
    scale = HEAD_DIM ** -0.5
    nt = (((1,), (1,)), ((), ()))
    tn = (((0,), (0,)), ((), ()))

    def body(q_ref, k_ref, v_ref, do_ref, l_ref, c_ref, kp_ref, vp_ref, qx_ref, dox_ref, lx_ref, cx_ref,
             dq_ref, dk_ref, dv_ref):
        n = pl.program_id(1)
        i1 = lax.broadcasted_iota(jnp.int32, (B, B), 0)
        j1 = lax.broadcasted_iota(jnp.int32, (B, B), 1)
        i2 = lax.broadcasted_iota(jnp.int32, (2 * B, B), 0)
        j2 = lax.broadcasted_iota(jnp.int32, (2 * B, B), 1)

        def task(r, j):
            cur = j * group + r
            blk = n * m + j
            q_c, k_c, v_c = _rows(q_ref, cur, d), _rows(k_ref, cur, d), _rows(v_ref, cur, d)
            do_c, l_c, c_c = _rows(do_ref, cur, d), _rows(l_ref, cur, d), _rows(c_ref, cur, d)
            if j == 0:
                k_p, v_p = _rows(kp_ref, r, d), _rows(vp_ref, r, d)
            else:
                k_p, v_p = _rows(k_ref, cur - group, d), _rows(v_ref, cur - group, d)
            if j == m - 1:
                nxt = [_rows(ref, r, d) for ref in (qx_ref, dox_ref, lx_ref, cx_ref)]
            else:
                nxt = [_rows(ref, cur + group, d) for ref in (q_ref, do_ref, l_ref, c_ref)]
            q_x, do_x, l_x, c_x = nxt
            kn_c, kn_p, v_c, v_p = (a.astype(MXU_DTYPE) for a in (k_c, k_p, v_c, v_p))
            qn_c = q_c.astype(MXU_DTYPE)
            qn_cat = jnp.concatenate([qn_c, q_x.astype(MXU_DTYPE)], axis=0)
            do_cb = do_c.astype(MXU_DTYPE)
            do_cat = jnp.concatenate([do_cb, do_x.astype(MXU_DTYPE)], axis=0)
            l_cat = jnp.concatenate([l_c, l_x], axis=0)
            c_cat = jnp.concatenate([c_c, c_x], axis=0)
            mask_p = (j1 >= i1) & (blk > 0)
            mask_c = ((i2 < B) & (j2 <= i2)) | ((i2 >= B) & (j2 >= i2 - B) & (blk + 1 < n_blocks))
            dqn, dkn, dv = [], [], []
            for h, sl in enumerate(_head_slices()):
                lane = slice(h * HEAD_DIM, h * HEAD_DIM + 1)
                s_p = lax.dot_general(qn_c[:, sl], kn_p[:, sl], nt, preferred_element_type=F32) * scale
                pr_p = jnp.where(mask_p, jnp.exp(s_p - l_c[:, lane]), 0.0)
                dp_p = lax.dot_general(do_cb[:, sl], v_p[:, sl], nt, preferred_element_type=F32)
                ds_p = (pr_p * (dp_p + c_c[:, lane]) * scale).astype(MXU_DTYPE)
                s_c = lax.dot_general(qn_cat[:, sl], kn_c[:, sl], nt, preferred_element_type=F32) * scale
                pr_c = jnp.where(mask_c, jnp.exp(s_c - l_cat[:, lane]), 0.0)
                dp_c = lax.dot_general(do_cat[:, sl], v_c[:, sl], nt, preferred_element_type=F32)
                ds_c = (pr_c * (dp_c + c_cat[:, lane]) * scale).astype(MXU_DTYPE)
                dqn.append(jnp.dot(ds_p, kn_p[:, sl], preferred_element_type=F32)
                           + jnp.dot(ds_c[:B], kn_c[:, sl], preferred_element_type=F32))
                dkn.append(lax.dot_general(ds_c, qn_cat[:, sl], tn, preferred_element_type=F32))
                dv.append(lax.dot_general(pr_c.astype(MXU_DTYPE), do_cat[:, sl], tn, preferred_element_type=F32))
            _set_rows(dq_ref, cur, d, jnp.concatenate(dqn, axis=1))
            _set_rows(dk_ref, cur, d, jnp.concatenate(dkn, axis=1))
            _set_rows(dv_ref, cur, d, jnp.concatenate(dv, axis=1))

        _for_each_block(d, m, task, ATT_BWD_UNROLL)

    main = lambda c0: pl.BlockSpec((span, W), lambda hp, n: (n, c0 + hp))
    prev = lambda c0: pl.BlockSpec((group, W), lambda hp, n: (jnp.maximum(n * m - 1, 0), c0 + hp))
    nxt = lambda c0: pl.BlockSpec((group, W), lambda hp, n: (jnp.minimum((n + 1) * m, n_blocks - 1), c0 + hp))
    own = pl.BlockSpec((span, W), lambda hp, n: (n, hp))
    return pl.pallas_call(
        body, name=name, grid=(PW // W, T // span),
        in_specs=[main(c_q), main(c_k), main(c_v), main(0), main(0), main(0), prev(c_k), prev(c_v),
                  nxt(c_q), nxt(0), nxt(0), nxt(0)],
        out_specs=[own, own, own],
        out_shape=[_sds((T, PW), F32)] * 3,
        compiler_params=_cparams("parallel", "parallel"),
    )(qkv, qkv, qkv, do, lse, corr, qkv, qkv, qkv, do, lse, corr)


def _softmax3(lses):
    mx = jnp.maximum(jnp.maximum(lses[0], lses[1]), lses[2])
    ex = [jnp.exp(l - mx) for l in lses]
    inv = 1.0 / (ex[0] + ex[1] + ex[2])
    return [e * inv for e in ex]


def _mix_fwd(os_, lses, name, tr=512):
    T, PW = os_[0].shape

    def body(o0, o1, o2, l0, l1, l2, y_ref):
        alpha = _softmax3([l0[...], l1[...], l2[...]])
        for g, o_ref in enumerate((o0, o1, o2)):
            y_ref[:, g * PW:(g + 1) * PW] = (o_ref[...] * alpha[g]).astype(y_ref.dtype)

    blk = pl.BlockSpec((tr, PW), lambda i: (i, 0))
    return pl.pallas_call(
        body, name=name, grid=(T // tr,),
        in_specs=[blk] * 6, out_specs=pl.BlockSpec((tr, 3 * PW), lambda i: (i, 0)),
        out_shape=_sds((T, 3 * PW), MXU_DTYPE),
        compiler_params=_cparams("parallel"),
    )(*os_, *lses)


def _mix_bwd(dymix, os_, lses, c_start, name, tr=512):
    T, PW = os_[0].shape
    HP = PW // HEAD_DIM
    c0 = c_start // PW

    def body(d0, d1, d2, o0, o1, o2, l0, l1, l2, do0, do1, do2, dl0, dl1, dl2):
        col_head = lax.broadcasted_iota(jnp.int32, (tr, PW), 1) // HEAD_DIM
        alpha = _softmax3([l0[...], l1[...], l2[...]])
        dys = [d0[...].astype(F32), d1[...].astype(F32), d2[...].astype(F32)]
        dots = [_head_sum(dy * o_ref[...], col_head, HP) for dy, o_ref in zip(dys, (o0, o1, o2))]
        mean_dot = alpha[0] * dots[0] + alpha[1] * dots[1] + alpha[2] * dots[2]
        for g, (do_ref, dl_ref) in enumerate(((do0, dl0), (do1, dl1), (do2, dl2))):
            do_ref[...] = dys[g] * alpha[g]
            dl_ref[...] = -alpha[g] * mean_dot

    blk = pl.BlockSpec((tr, PW), lambda i: (i, 0))
    dy_specs = [pl.BlockSpec((tr, PW), lambda i, g=g: (i, c0 + g)) for g in range(3)]
    outs = pl.pallas_call(
        body, name=name, grid=(T // tr,),
        in_specs=dy_specs + [blk] * 6, out_specs=[blk] * 6,
        out_shape=[_sds((T, PW), F32)] * 6,
        compiler_params=_cparams("parallel"),
    )(dymix, dymix, dymix, *os_, *lses)
    return outs[:3], outs[3:]


def _adamw_math(w, g, m, v):
    m2 = ADAM_B1 * m + (1.0 - ADAM_B1) * g
    v2 = ADAM_B2 * v + (1.0 - ADAM_B2) * (g * g)
    m_hat = m2 / (1.0 - ADAM_B1 ** ADAM_STEP)
    v_hat = v2 / (1.0 - ADAM_B2 ** ADAM_STEP)
    delta = -ADAM_LR * (m_hat / (jnp.sqrt(v_hat) + ADAM_EPS) + ADAM_WD * w)
    return delta, m2, v2


def _adamw_layer(layer, w, m, v, own, landed, me, prev, name, tr=256):
    _, R, C = w.shape
    tr = next(t for t in range(min(tr, R) // 16 * 16, 0, -16) if R % t == 0)

    def body(me_ref, w_ref, m_ref, v_ref, own_ref, land_ref, *rest):
        g_ref, d_ref, m2_ref, v2_ref = rest[-4:]
        g = own_ref[...].astype(F32)
        for j in range(N_PEER):
            g = g + land_ref[j].astype(F32)
        delta, m2, v2 = _adamw_math(w_ref[...], g, m_ref[...], v_ref[...])
        g_ref[...] = g
        d_ref[...] = delta
        m2_ref[...] = m2
        v2_ref[...] = v2

    lay = pl.BlockSpec((None, tr, C), lambda i, me_ref: (layer, i, 0))
    in_specs = [lay, lay, lay, pl.BlockSpec((None, tr, C), lambda i, me_ref: (me_ref[0], i, 0)),
                pl.BlockSpec((N_PEER, tr, C), lambda i, me_ref: (0, i, 0))]
    args = [me, w, m, v, own, landed]
    aliases = {}
    if prev is not None:
        in_specs += [pl.BlockSpec(memory_space=pl.ANY)] * 4
        args += list(prev)
        aliases = {6 + i: i for i in range(4)}
    return pl.pallas_call(
        body, name=name,
        grid_spec=pltpu.PrefetchScalarGridSpec(num_scalar_prefetch=1, grid=(R // tr,), in_specs=in_specs, out_specs=[lay] * 4),
        out_shape=[_sds(w.shape, F32)] * 4,
        input_output_aliases=aliases,
        compiler_params=_cparams("parallel"),
    )(*args)


def _sum_parts(parts, name):
    _, R, C = parts.shape

    def body(p_ref, out_ref):
        g = p_ref[0]
        for j in range(1, N_DEV):
            g = g + p_ref[j]
        out_ref[...] = g

    return pl.pallas_call(
        body, name=name, grid=(1,),
        in_specs=[pl.BlockSpec((N_DEV, R, C), lambda i: (0, 0, 0))], out_specs=pl.BlockSpec((R, C), lambda i: (0, 0)),
        out_shape=_sds((R, C), F32), compiler_params=_cparams("arbitrary"),
    )(parts)


def _adamw_small(ws, gs, ms, vs, name):
    n = len(ws)

    def body(*refs):
        ins, outs = refs[:4 * n], refs[4 * n:]
        for i in range(n):
            delta, m2, v2 = _adamw_math(ins[i][...], ins[n + i][...], ins[2 * n + i][...], ins[3 * n + i][...])
            outs[i][...] = delta
            outs[n + i][...] = m2
            outs[2 * n + i][...] = v2

    outs = pl.pallas_call(
        body, name=name, out_shape=[_sds(w.shape, F32) for w in ws] * 3,
    )(*ws, *gs, *ms, *vs)
    return outs[:n], outs[n:2 * n], outs[2 * n:]


def _pack(arrays, rows_multiple=8):
    flat = []
    for a in arrays:
        a = a.reshape(-1).astype(F32)
        flat.append(jnp.pad(a, (0, (-a.shape[0]) % LANE)))
    flat = jnp.concatenate(flat)
    flat = jnp.pad(flat, (0, (-flat.shape[0]) % (LANE * rows_multiple)))
    return flat.reshape(-1, LANE)


def _unpack(packed, shapes):
    flat = packed.reshape(-1)
    out, off = [], 0
    for s in shapes:
        size = 1
        for dim in s:
            size *= dim
        out.append(flat[off:off + size].reshape(s))
        off += size + (-size) % LANE
    return out


def _layer_fwd(x, wts, getw, dims, dep=None):
    AW, BW, PW, DP = dims["AW"], dims["BW"], dims["PW"], dims["DP"]
    q_start = 2 * AW + 3 * BW
    h, r1 = _rmsnorm_fwd(x, wts["attn_norm"], "rmsnorm_fwd", dep=dep)
    p = _mm_nt(h, getw("w_in", h), "proj_in", out_dtype=MXU_DTYPE, tm=1024, tn=512)
    y_a = _sgu_fwd(p, wts["sgu_tril"], wts["sgu_bmat"], "sgu_fwd")
    y_b = _conv_fwd(p, getw("conv_w", y_a), AW, "conv_fwd")
    qkn = _qk_norm_fwd(p, wts["qk_gain"], q_start, PW, "qk_norm_fwd")
    os_, lses = [], []
    for g, d in enumerate(DILATIONS):
        o, lse = _attn_fwd(qkn, g, d, PW, "attn_fwd_%d" % d)
        os_.append(o)
        lses.append(lse)
    y_c = _mix_fwd(os_, lses, "mix_fwd")
    ymix = jnp.concatenate([y_a, y_b, y_c], axis=1)
    x1 = _mm_nn(ymix, getw("w_out", ymix), "proj_out", residual=x, tm=1024, tn=1024)
    h2, r2 = _rmsnorm_fwd(x1, wts["mlp_norm"], "rmsnorm_fwd")
    a, hid = _mm_nn(h2, getw("w_mlp_in", h2), "mlp_in", out_dtype=MXU_DTYPE, relu2=True, tm=1024, tn=1024)
    x2 = _mm_nn(hid, getw("w_mlp_out", hid), "mlp_out", residual=x1, columns_first=True, tm=512, tn=1024)
    saved = dict(x=x, h=h, r1=r1, p=p, qkn=qkn, os=os_, lses=lses, ymix=ymix, x1=x1, h2=h2, r2=r2, a=a, hid=hid)
    return x2, saved


def _layer_bwd(dx, dxb, wts, getw, scatter, saved, dims):
    AW, BW, PW, DP = dims["AW"], dims["BW"], dims["PW"], dims["DP"]
    q_start = 2 * AW + 3 * BW
    D = dx.shape[1]
    g_w2 = _wgrad_wide_a(saved["hid"], dxb, "mlp_out_wgrad")
    token = scatter("w_mlp_out", g_w2.reshape(N_DEV, -1, D))
    da = _mm_nt(dxb, getw("w_mlp_out", None), "mlp_out_dgrad", out_dtype=MXU_DTYPE, relu2_pre=saved["a"], dep=token,
                tm=1024, tn=1024)
    g_w1 = _wgrad_wide_b(saved["h2"], da, "mlp_in_wgrad", groups=N_DEV)
    token = scatter("w_mlp_in", g_w1)
    dh2 = _mm_nt(da, getw("w_mlp_in", None), "mlp_in_dgrad", out_dtype=MXU_DTYPE, dep=token, tm=1024, tn=512)
    dx1, dx1b, g_mlp_norm = _rmsnorm_bwd(dh2, saved["x1"], wts["mlp_norm"], saved["r2"], dx, "rmsnorm_bwd")
    g_wout = _wgrad_wide_b(saved["ymix"], dx1b, "proj_out_wgrad")
    token = scatter("w_out", g_wout.reshape(N_DEV, -1, D))
    dymix = _mm_nt(dx1b, getw("w_out", None), "proj_out_dgrad", out_dtype=MXU_DTYPE, dep=token, tm=1024, tn=1024)
    p = saved["p"]
    du, dv, g_sgu_w, g_sgu_bmat = _sgu_bwd(dymix, p, wts["sgu_tril"], wts["sgu_tril_t"], wts["sgu_bmat"], "sgu_bwd")
    d_b, d_c, d_xb, g_conv = _conv_bwd(dymix, p, getw("conv_w", None), AW, "conv_bwd")
    dos, corrs = _mix_bwd(dymix, saved["os"], saved["lses"], AW + BW, "mix_bwd")
    dqns, dkns, dvs = [], [], []
    for g, d in enumerate(DILATIONS):
        dqn, dkn, dvv = _attn_bwd(saved["qkn"], saved["lses"][g], dos[g], corrs[g], g, d, PW, "attn_bwd_%d" % d)
        dqns.append(dqn)
        dkns.append(dkn)
        dvs.append(dvv.astype(MXU_DTYPE))
    dqk, g_qk = _qk_norm_bwd(dqns + dkns, p, wts["qk_gain"], q_start, PW, "qk_norm_bwd")
    g_q, g_k = (part.reshape(-1, HEAD_DIM).sum(0) for part in jnp.split(g_qk[0], 2))
    dp = jnp.concatenate([du, dv, d_b, d_c, d_xb, dqk] + dvs, axis=1)
    g_win_t = _wgrad_wide_a(dp, saved["h"], "proj_in_wgrad")
    token = scatter("w_in", g_win_t.reshape(N_DEV, DP // N_DEV, D))
    dh = _mm_nn(dp, getw("w_in", None), "proj_in_dgrad", out_dtype=MXU_DTYPE, dep=token, tm=1024, tn=512)
    dx0, dx0b, g_attn_norm = _rmsnorm_bwd(dh, saved["x"], wts["attn_norm"], saved["r1"], dx1, "rmsnorm_bwd")
    H = AW // HEAD_DIM
    tril = jnp.tril(jnp.ones((CHUNK, CHUNK), F32))
    small = [g_attn_norm.reshape(-1), g_sgu_w * tril, g_sgu_bmat.reshape(CHUNK, H, HEAD_DIM).sum(-1).T,
             g_conv, g_q, g_k, g_mlp_norm.reshape(-1)]
    return dx0, dx0b, small


def kernel(x, attn_norm, w_in, sgu_w, sgu_b, conv_w, q_norm, k_norm, w_out, mlp_norm, w_mlp_in, w_mlp_out, loss_target, m_attn_norm, m_w_in, m_sgu_w, m_sgu_b, m_conv_w, m_q_norm, m_k_norm, m_w_out, m_mlp_norm, m_w_mlp_in, m_w_mlp_out, v_attn_norm, v_w_in, v_sgu_w, v_sgu_b, v_conv_w, v_q_norm, v_k_norm, v_w_out, v_mlp_norm, v_w_mlp_in, v_w_mlp_out):
    n_layers = attn_norm.shape[0]
    T, D = x.shape[1], x.shape[2]
    H = sgu_w.shape[1]
    AW = H * HEAD_DIM
    BW = conv_w.shape[2] * N_DEV
    DP = w_in.shape[2] * N_DEV
    DMIX = w_out.shape[1] * N_DEV
    DFF = w_mlp_in.shape[2] * N_DEV
    PW = (DMIX - AW - BW) // 3
    HP = PW // HEAD_DIM
    dims = dict(AW=AW, BW=BW, PW=PW, DP=DP)
    me = 4 * lax.axis_index("x") + 2 * lax.axis_index("y") + lax.axis_index("c")

    big_names = ("w_in", "w_out", "w_mlp_in", "w_mlp_out")
    tr_in = lambda a: jnp.swapaxes(a, 1, 2)
    big_w = dict(zip(big_names, (tr_in(w_in), w_out, w_mlp_in, w_mlp_out)))
    big_m = dict(zip(big_names, (tr_in(m_w_in), m_w_out, m_w_mlp_in, m_w_mlp_out)))
    big_v = dict(zip(big_names, (tr_in(v_w_in), v_w_out, v_w_mlp_in, v_w_mlp_out)))

    keys = []
    for l in range(n_layers):
        keys += [(l, nm) for nm in big_names]
    keys.insert(1, (0, "conv_w"))
    srcs = [_pack([conv_w]) if nm == "conv_w" else big_w[nm][l].astype(MXU_DTYPE) for l, nm in keys]
    flights, gather_token = _exchange_start(srcs, [_own_in_place(s, me) for s in srcs], "gather", name="gather_start")
    arriving = dict(zip(keys, flights))
    forwarding = {}
    relayout = dict(
        w_in=lambda g: g.reshape(DP, D), w_out=lambda g: g.reshape(DMIX, D),
        w_mlp_in=lambda g: g, w_mlp_out=lambda g: g.reshape(DFF, D),
        conv_w=lambda g: jnp.stack([_unpack(g[j], [conv_w.shape])[0] for j in range(N_DEV)], axis=2).reshape(
            n_layers, CONV_WIDTH, BW))
    gathered = {}

    def forward(key, after):
        _, land = _exchange_wait(arriving[key], after, "gather", name="gather_arrive_%d_%s" % key)
        fl, token = _exchange_start(None, [land], "forward", name="gather_forward_%d_%s" % key)
        forwarding[key] = fl[0]
        return token

    def weight_getter(l):
        def getw(nm, after):
            key = (0, nm) if nm == "conv_w" else (l, nm)
            if key not in gathered:
                ahead = keys[keys.index(key):][:2]
                for k in ahead:
                    if k not in forwarding:
                        after = forward(k, after)
                _, land = _exchange_wait(forwarding[key], after, "forward", name="gather_wait_%d_%s" % key)
                gathered[key] = relayout[nm](land)
            return gathered[key][l] if nm == "conv_w" else gathered[key]
        return getw

    tril = jnp.tril(jnp.ones((CHUNK, CHUNK), F32))
    layers = []
    for l in range(n_layers):
        w_tril = sgu_w[l] * tril
        layers.append(dict(
            attn_norm=attn_norm[l][None], mlp_norm=mlp_norm[l][None],
            sgu_tril=w_tril.astype(MXU_DTYPE), sgu_tril_t=w_tril.transpose(0, 2, 1).astype(MXU_DTYPE),
            sgu_bmat=jnp.repeat(sgu_b[l].T, HEAD_DIM, axis=1),
            qk_gain=jnp.concatenate([jnp.tile(q_norm[l], 3 * HP), jnp.tile(k_norm[l], 3 * HP)])[None]))

    xs = x[0]
    saved = []
    for l in range(n_layers):
        xs, sv = _layer_fwd(xs, layers[l], weight_getter(l), dims, dep=gather_token if l == 0 else None)
        saved.append(sv)
    loss_blk, dx, dxb = _loss_and_grad(xs, loss_target[0], "loss")
    loss = lax.psum(loss_blk[0, 0], ("x", "y", "c"))

    scattering = {}

    def scatter_starter(l):
        def scatter(nm, partials):
            land = lax.empty((N_PEER,) + partials.shape[1:], partials.dtype)
            fl, tok = _exchange_start([partials], [land], "scatter", name="scatter_start_%d_%s" % (l, nm))
            scattering[(l, nm)] = fl[0]
            return tok
        return scatter

    small = [None] * n_layers
    for l in reversed(range(n_layers)):
        dx, dxb, small[l] = _layer_bwd(dx, dxb, layers[l], weight_getter(l), scatter_starter(l), saved[l], dims)

    small_shapes = [s.shape for s in small[0]]
    small_src = [_pack([s for l in range(n_layers) for s in small[l]])]
    small_flights, small_token = _exchange_start(small_src, [_own_in_place(s, me) for s in small_src], "gather_all",
                                                 name="small_start")
    grad_x = dx[None]

    me1 = me.astype(jnp.int32).reshape(1)
    res = {nm: None for nm in big_names}
    after = small_token
    for l in reversed(range(n_layers)):
        for nm in reversed(big_names):
            own, landed = _exchange_wait(scattering[(l, nm)], after, "scatter", name="scatter_wait_%d_%s" % (l, nm))
            res[nm] = _adamw_layer(l, big_w[nm], big_m[nm], big_v[nm], own, landed, me1, res[nm], "adamw_" + nm)
            after = res[nm][0]
    res["w_in"] = [tr_in(a) for a in res["w_in"]]
    big_out = [res[nm] for nm in big_names]

    _, gathered_small = _exchange_wait(small_flights[0], after, "gather_all", name="small_wait")
    summed = _unpack(_sum_parts(gathered_small, "sum_small"), small_shapes * n_layers)
    ns = len(small_shapes)
    g_small = [jnp.stack([summed[l * ns + i] for l in range(n_layers)]) for i in range(ns)]
    g_attn_norm, g_sgu_w, g_sgu_b, g_conv_full, g_q, g_k, g_mlp_norm = g_small
    cs = conv_w.shape[2]
    g_conv = lax.dynamic_slice_in_dim(g_conv_full, me * cs, cs, axis=2)
    sm_w = (attn_norm, sgu_w, sgu_b, conv_w, q_norm, k_norm, mlp_norm)
    sm_m = (m_attn_norm, m_sgu_w, m_sgu_b, m_conv_w, m_q_norm, m_k_norm, m_mlp_norm)
    sm_v = (v_attn_norm, v_sgu_w, v_sgu_b, v_conv_w, v_q_norm, v_k_norm, v_mlp_norm)
    sm_g = (g_attn_norm, g_sgu_w, g_sgu_b, g_conv, g_q, g_k, g_mlp_norm)
    sm_delta, sm_m2, sm_v2 = _adamw_small(sm_w, sm_g, sm_m, sm_v, "adamw_small")

    def ordered(small_list, big_kind):
        b = [big_out[i][big_kind] for i in range(4)]
        return [small_list[0], b[0], small_list[1], small_list[2], small_list[3], small_list[4], small_list[5],
                b[1], small_list[6], b[2], b[3]]

    return (loss, grad_x, *ordered(list(sm_g), 0), *ordered(sm_delta, 1), *ordered(sm_m2, 2), *ordered(sm_v2, 3))
```

```python
import jax
import jax.numpy as jnp
from jax import lax
from jax.experimental import pallas as pl
from jax.experimental.pallas import tpu as pltpu

N_DEV = 8
HEAD_DIM = 64
CHUNK = 128
ATT_BLK = 128
DILATIONS = (1, 4, 16)
CONV_WIDTH = 3
EPS = 1e-6
ADAM_LR = 0.001
ADAM_B1 = 0.9
ADAM_B2 = 0.999
ADAM_EPS = 1e-08
ADAM_WD = 0.01
ADAM_STEP = 10
MXU_DTYPE = jnp.bfloat16
F32 = jnp.float32
LANE = 128
VMEM_LIMIT_BYTES = 56 * 1024 * 1024
NEG_INF = float("-inf")


def _cparams(*sem):
    return pltpu.CompilerParams(dimension_semantics=sem, vmem_limit_bytes=VMEM_LIMIT_BYTES)


def _sds(shape, dtype):
    return jax.ShapeDtypeStruct(shape, dtype)


def _fit(n, tile):
    for t in range(min(tile, n) // LANE * LANE, 0, -LANE):
        if n % t == 0:
            return t
    return n


_HBM = pl.BlockSpec(memory_space=pltpu.HBM)
_SEM = pl.BlockSpec(memory_space=pltpu.SEMAPHORE)
_DATAFLOW = pltpu.SideEffectType.DATAFLOW_SIDE_EFFECTING
N_PEER = N_DEV - 1


def _mesh_pos():
    x, y, c = lax.axis_index("x"), lax.axis_index("y"), lax.axis_index("c")
    return x, y, c, 4 * x + 2 * y + c


OTHER_CHIPS = (4, 2, 6)
EXCHANGE_PEERS = dict(
    scatter=tuple(range(1, N_DEV)),
    gather_all=tuple(range(1, N_DEV)),
    gather=(1,) + OTHER_CHIPS,
    forward=OTHER_CHIPS)


def _remote_copies(src, land, send_sems, recv_sems, mode):
    x, y, c, me = _mesh_pos()
    copies = []
    for i, k in enumerate(EXCHANGE_PEERS[mode]):
        px = (1 - x) if (k & 4) else x
        py = (1 - y) if (k & 2) else y
        pc = (1 - c) if (k & 1) else c
        if mode == "scatter":
            src_ref, dst_ref, dev = src.at[4 * px + 2 * py + pc], land.at[i], (px, py, pc)
        elif mode == "forward":
            slot = 4 * px + 2 * py + c
            src_ref, dst_ref, dev = land.at[slot], land.at[slot], (x, y, 1 - c)
        else:
            src_ref, dst_ref, dev = src, land.at[me], (px, py, pc)
        copies.append(pltpu.make_async_remote_copy(
            src_ref=src_ref, dst_ref=dst_ref, send_sem=send_sems.at[i], recv_sem=recv_sems.at[i],
            device_id=dev, device_id_type=pl.DeviceIdType.MESH))
    return copies


def _own_in_place(src, me):
    land = lax.empty((N_DEV,) + src.shape, src.dtype)
    return lax.dynamic_update_slice(land, src[None], (me,) + (0,) * src.ndim)


def _exchange_start(srcs, lands, mode, name):
    n = len(lands)
    has_src = srcs is not None
    arrays = (list(srcs) if has_src else []) + list(lands)
    n_arr = len(arrays)
    n_copies = len(EXCHANGE_PEERS[mode])

    def body(*refs):
        src = refs[:n] if has_src else [None] * n
        land = refs[n_arr - n:n_arr]
        send, recv = refs[n_arr:n_arr + n], refs[n_arr + n:n_arr + 2 * n]
        token = refs[2 * n_arr + 2 * n]
        for t in range(n):
            for cp in _remote_copies(src[t], land[t], send[t], recv[t], mode):
                cp.start()
        token[...] = jnp.zeros_like(token)

    outs = pl.pallas_call(
        body, name=name,
        out_shape=([pltpu.SemaphoreType.DMA((n_copies,))] * (2 * n) + [pltpu.HBM(a.shape, a.dtype) for a in arrays]
                   + [_sds((8, LANE), F32)]),
        in_specs=[_HBM] * n_arr,
        out_specs=[_SEM] * (2 * n) + [_HBM] * n_arr + [pl.BlockSpec(memory_space=pltpu.VMEM)],
        input_output_aliases={i: 2 * n + i for i in range(n_arr)},
        compiler_params=pltpu.CompilerParams(has_side_effects=_DATAFLOW),
    )(*[pltpu.with_memory_space_constraint(a, pltpu.HBM) for a in arrays])
    thru = outs[2 * n:2 * n + n_arr]
    flights = [(outs[t], outs[n + t], thru[t] if has_src else None, thru[n_arr - n + t]) for t in range(n)]
    return flights, outs[2 * n + n_arr]


def _exchange_wait(flight, after, mode, name):
    send, recv, src, land = flight
    arrays = [land] if src is None else [src, land]
    n_arr = len(arrays)

    def body(*refs):
        src_ref = refs[0] if n_arr == 2 else None
        land_ref, send_ref, recv_ref = refs[n_arr - 1], refs[n_arr], refs[n_arr + 1]
        for cp in _remote_copies(src_ref, land_ref, send_ref, recv_ref, mode):
            cp.wait_send()
            cp.wait_recv()

    outs = pl.pallas_call(
        body, name=name, out_shape=[pltpu.HBM(a.shape, a.dtype) for a in arrays],
        in_specs=[_HBM] * n_arr + [_SEM, _SEM, pl.BlockSpec(memory_space=pl.ANY)], out_specs=[_HBM] * n_arr,
        input_output_aliases={i: i for i in range(n_arr)},
        compiler_params=pltpu.CompilerParams(has_side_effects=_DATAFLOW),
    )(*arrays, send, recv, after)
    return (None, outs[0]) if src is None else (outs[0], outs[1])


def _rmsnorm_fwd(x, g, name, dep=None, tr=512):
    T, D = x.shape

    def body(x_ref, g_ref, *rest):
        h_ref, r_ref = rest[-2:]
        xv = x_ref[...]
        r = lax.rsqrt(jnp.mean(xv * xv, axis=-1, keepdims=True) + EPS)
        h_ref[...] = (xv * r * g_ref[...]).astype(h_ref.dtype)
        r_ref[...] = r

    in_specs = [pl.BlockSpec((tr, D), lambda i: (i, 0)), pl.BlockSpec((1, D), lambda i: (0, 0))]
    args = [x, g]
    if dep is not None:
        in_specs.append(pl.BlockSpec(dep.shape, lambda i: (0, 0)))
        args.append(dep)
    return pl.pallas_call(
        body, name=name, grid=(T // tr,),
        in_specs=in_specs,
        out_specs=[pl.BlockSpec((tr, D), lambda i: (i, 0)), pl.BlockSpec((tr, 1), lambda i: (i, 0))],
        out_shape=[_sds((T, D), MXU_DTYPE), _sds((T, 1), F32)],
        compiler_params=_cparams("parallel"),
    )(*args)


def _rmsnorm_bwd(dh, x, g, r, dres, name, tr=256):
    T, D = x.shape

    def body(dh_ref, x_ref, g_ref, r_ref, dres_ref, dx_ref, dxb_ref, dg_ref):
        @pl.when(pl.program_id(0) == 0)
        def _():
            dg_ref[...] = jnp.zeros_like(dg_ref)

        dh_v, xv, rv = dh_ref[...].astype(F32), x_ref[...], r_ref[...]
        gdy = dh_v * g_ref[...]
        mean_xg = jnp.mean(xv * gdy, axis=-1, keepdims=True)
        dx = dres_ref[...] + rv * gdy - xv * (rv * rv * rv) * mean_xg
        dx_ref[...] = dx
        dxb_ref[...] = dx.astype(dxb_ref.dtype)
        dg_ref[...] += jnp.sum(dh_v * xv * rv, axis=0, keepdims=True)

    row = lambda i: (i, 0)
    return pl.pallas_call(
        body, name=name, grid=(T // tr,),
        in_specs=[pl.BlockSpec((tr, D), row), pl.BlockSpec((tr, D), row), pl.BlockSpec((1, D), lambda i: (0, 0)),
                  pl.BlockSpec((tr, 1), row), pl.BlockSpec((tr, D), row)],
        out_specs=[pl.BlockSpec((tr, D), row), pl.BlockSpec((tr, D), row), pl.BlockSpec((1, D), lambda i: (0, 0))],
        out_shape=[_sds((T, D), F32), _sds((T, D), MXU_DTYPE), _sds((1, D), F32)],
        compiler_params=_cparams("arbitrary"),
    )(dh, x, g, r, dres)


def _mm_nn(a, b, name, out_dtype=F32, residual=None, relu2=False, dep=None, columns_first=False, tm=512, tn=512):
    M, K = a.shape
    grouped = b.ndim == 3
    N = b.shape[0] * b.shape[2] if grouped else b.shape[1]
    tm, tn = _fit(M, tm), _fit(b.shape[2] if grouped else N, tn)
    tile = (lambda j, i: (i, j)) if columns_first else (lambda i, j: (i, j))
    b_mode = pl.Buffered(1 if columns_first else 2)
    if grouped:
        per = b.shape[2] // tn
        b_spec = pl.BlockSpec((None, K, tn), lambda *g: (tile(*g)[1] // per, 0, tile(*g)[1] % per), pipeline_mode=b_mode)
    else:
        b_spec = pl.BlockSpec((K, tn), lambda *g: (0, tile(*g)[1]), pipeline_mode=b_mode)
    n_out = 2 if relu2 else 1

    def body(*refs):
        a_ref, b_ref = refs[0], refs[1]
        r_ref = refs[2] if residual is not None else None
        outs = refs[2 + (residual is not None) + (dep is not None):]
        acc = jnp.dot(a_ref[...], b_ref[...], preferred_element_type=F32)
        if r_ref is not None:
            acc = acc + r_ref[...]
        outs[0][...] = acc.astype(outs[0].dtype)
        if relu2:
            rl = jnp.maximum(acc, 0.0)
            outs[1][...] = (rl * rl).astype(outs[1].dtype)

    out_blk = pl.BlockSpec((tm, tn), lambda *g: tile(*g))
    in_specs = [pl.BlockSpec((tm, K), lambda *g: (tile(*g)[0], 0)), b_spec]
    args = [a, b]
    if residual is not None:
        in_specs.append(out_blk)
        args.append(residual)
    if dep is not None:
        in_specs.append(pl.BlockSpec(dep.shape, lambda *g: (0, 0)))
        args.append(dep)
    out_shape = [_sds((M, N), out_dtype)]
    if relu2:
        out_shape.append(_sds((M, N), MXU_DTYPE))
    outs = pl.pallas_call(
        body, name=name, grid=(N // tn, M // tm) if columns_first else (M // tm, N // tn),
        in_specs=in_specs, out_specs=[out_blk] * n_out, out_shape=out_shape,
        compiler_params=_cparams("parallel", "parallel"),
    )(*args)
    return outs if relu2 else outs[0]


def _mm_nt(a, b, name, out_dtype=F32, relu2_pre=None, dep=None, tm=512, tn=512):
    M, K = a.shape
    grouped = b.ndim == 3
    N = b.shape[1] if grouped else b.shape[0]
    tm, tn = _fit(M, tm), _fit(N, tn)
    nt = (((1,), (1,)), ((), ()))
    if grouped:
        G, _, Kg = b.shape
        b_spec = pl.BlockSpec((G, tn, Kg), lambda i, j: (0, j, 0))
    else:
        b_spec = pl.BlockSpec((tn, K), lambda i, j: (j, 0))

    def body(*refs):
        a_ref, b_ref = refs[0], refs[1]
        p_ref = refs[2] if relu2_pre is not None else None
        out_ref = refs[2 + (relu2_pre is not None) + (dep is not None)]
        if grouped:
            acc = lax.dot_general(a_ref[:, 0:Kg], b_ref[0], nt, preferred_element_type=F32)
            for g in range(1, G):
                acc += lax.dot_general(a_ref[:, g * Kg:(g + 1) * Kg], b_ref[g], nt, preferred_element_type=F32)
        else:
            acc = lax.dot_general(a_ref[...], b_ref[...], nt, preferred_element_type=F32)
        if p_ref is not None:
            acc = acc * (2.0 * jnp.maximum(p_ref[...].astype(F32), 0.0))
        out_ref[...] = acc.astype(out_ref.dtype)

    out_blk = pl.BlockSpec((tm, tn), lambda i, j: (i, j))
    in_specs = [pl.BlockSpec((tm, K), lambda i, j: (i, 0)), b_spec]
    args = [a, b]
    if relu2_pre is not None:
        in_specs.append(out_blk)
        args.append(relu2_pre)
    if dep is not None:
        in_specs.append(pl.BlockSpec(dep.shape, lambda i, j: (0, 0)))
        args.append(dep)
    return pl.pallas_call(
        body, name=name, grid=(M // tm, N // tn),
        in_specs=in_specs, out_specs=out_blk, out_shape=_sds((M, N), out_dtype),
        compiler_params=_cparams("parallel", "parallel"),
    )(*args)


def _wgrad_wide_a(a, b, name, tm=512):
    T, M = a.shape
    N = b.shape[1]
    tm = _fit(M, tm)

    def body(a_ref, b_ref, out_ref):
        out_ref[...] = lax.dot_general(a_ref[...], b_ref[...], (((0,), (0,)), ((), ())),
                                       preferred_element_type=F32).astype(out_ref.dtype)

    return pl.pallas_call(
        body, name=name, grid=(M // tm,),
        in_specs=[pl.BlockSpec((T, tm), lambda i: (0, i)),
                  pl.BlockSpec((T, N), lambda i: (0, 0), pipeline_mode=pl.Buffered(1))],
        out_specs=pl.BlockSpec((tm, N), lambda i: (i, 0)), out_shape=_sds((M, N), MXU_DTYPE),
        compiler_params=_cparams("parallel"),
    )(a, b)


def _wgrad_wide_b(a, b, name, groups=None, tn=512, t_chunk=512):
    T, M = a.shape
    N = b.shape[1]
    tn = _fit(N if groups is None else N // groups, tn)
    t_chunk = _fit(T, t_chunk)

    def body(a_ref, b_ref, out_ref, at_ref):
        @pl.when(pl.program_id(0) == 0)
        def _():
            for c in range(0, T, t_chunk):
                at_ref[:, c:c + t_chunk] = a_ref[c:c + t_chunk, :].T

        out_ref[...] = jnp.dot(at_ref[...], b_ref[...], preferred_element_type=F32).astype(out_ref.dtype)

    if groups is None:
        out_spec = pl.BlockSpec((M, tn), lambda j: (0, j))
        out_shape = _sds((M, N), MXU_DTYPE)
    else:
        per = N // groups // tn
        out_spec = pl.BlockSpec((None, M, tn), lambda j: (j // per, 0, j % per))
        out_shape = _sds((groups, M, N // groups), MXU_DTYPE)
    return pl.pallas_call(
        body, name=name, grid=(N // tn,),
        in_specs=[pl.BlockSpec((T, M), lambda j: (0, 0), pipeline_mode=pl.Buffered(1)),
                  pl.BlockSpec((T, tn), lambda j: (0, j))],
        out_specs=out_spec, out_shape=out_shape,
        scratch_shapes=[pltpu.VMEM((M, T), MXU_DTYPE)],
        compiler_params=_cparams("arbitrary"),
    )(a, b)


def _loss_and_grad(y, target, name, tr=512):
    T, D = y.shape

    def body(y_ref, t_ref, loss_ref, dx_ref, dxb_ref):
        @pl.when(pl.program_id(0) == 0)
        def _():
            loss_ref[...] = jnp.zeros_like(loss_ref)

        err = y_ref[...] - t_ref[...]
        loss_ref[...] += 0.5 * jnp.sum(jnp.mean(err * err, axis=-1, keepdims=True), axis=0, keepdims=True)
        dx = err * (1.0 / D)
        dx_ref[...] = dx
        dxb_ref[...] = dx.astype(dxb_ref.dtype)

    row = lambda i: (i, 0)
    return pl.pallas_call(
        body, name=name, grid=(T // tr,),
        in_specs=[pl.BlockSpec((tr, D), row), pl.BlockSpec((tr, D), row)],
        out_specs=[pl.BlockSpec((8, LANE), lambda i: (0, 0)), pl.BlockSpec((tr, D), row), pl.BlockSpec((tr, D), row)],
        out_shape=[_sds((8, LANE), F32), _sds((T, D), F32), _sds((T, D), MXU_DTYPE)],
        compiler_params=_cparams("arbitrary"),
    )(y, target)


SGU_ROWS = 512


def _sgu_mixed(v, w_ref, b_ref, n_heads):
    parts = [jnp.dot(w_ref[h], v[:, h * HEAD_DIM:(h + 1) * HEAD_DIM], preferred_element_type=F32) for h in range(n_heads)]
    return jnp.concatenate(parts, axis=1) + b_ref[...]


def _sgu_fwd(p, w_tril, bmat, name):
    T = p.shape[0]
    H = w_tril.shape[0]
    AW = H * HEAD_DIM
    rows = _fit(T, SGU_ROWS)

    def body(u_ref, v_ref, w_ref, b_ref, y_ref):
        for c in range(0, rows, CHUNK):
            ch = pl.ds(c, CHUNK)
            mixed = _sgu_mixed(v_ref[ch, :].astype(MXU_DTYPE), w_ref, b_ref, H)
            y_ref[ch, :] = (u_ref[ch, :].astype(F32) * mixed).astype(y_ref.dtype)

    const3 = lambda c: (0, 0, 0)
    return pl.pallas_call(
        body, name=name, grid=(T // rows,),
        in_specs=[pl.BlockSpec((rows, AW), lambda c: (c, 0)), pl.BlockSpec((rows, AW), lambda c: (c, 1)),
                  pl.BlockSpec((H, CHUNK, CHUNK), const3), pl.BlockSpec((CHUNK, AW), lambda c: (0, 0))],
        out_specs=pl.BlockSpec((rows, AW), lambda c: (c, 0)),
        out_shape=_sds((T, AW), MXU_DTYPE),
        compiler_params=_cparams("parallel"),
    )(p, p, w_tril, bmat)


def _sgu_bwd(dymix, p, w_tril, w_tril_t, bmat, name):
    T = p.shape[0]
    H = w_tril.shape[0]
    AW = H * HEAD_DIM
    rows = _fit(T, SGU_ROWS)

    def body(dy_ref, u_ref, v_ref, w_ref, wt_ref, b_ref, du_ref, dv_ref, dw_ref, db_ref):
        @pl.when(pl.program_id(0) == 0)
        def _():
            dw_ref[...] = jnp.zeros_like(dw_ref)
            db_ref[...] = jnp.zeros_like(db_ref)

        for c in range(0, rows, CHUNK):
            ch = pl.ds(c, CHUNK)
            v = v_ref[ch, :].astype(MXU_DTYPE)
            dy = dy_ref[ch, :].astype(F32)
            du_ref[ch, :] = (dy * _sgu_mixed(v, w_ref, b_ref, H)).astype(du_ref.dtype)
            dm = dy * u_ref[ch, :].astype(F32)
            db_ref[...] += dm
            dm_c = dm.astype(MXU_DTYPE)
            dv = []
            for h in range(H):
                sl = slice(h * HEAD_DIM, (h + 1) * HEAD_DIM)
                dv.append(jnp.dot(wt_ref[h], dm_c[:, sl], preferred_element_type=F32))
                dw_ref[h] += lax.dot_general(dm_c[:, sl], v[:, sl], (((1,), (1,)), ((), ())), preferred_element_type=F32)
            dv_ref[ch, :] = jnp.concatenate(dv, axis=1).astype(dv_ref.dtype)

    const3 = lambda c: (0, 0, 0)
    blk = pl.BlockSpec((rows, AW), lambda c: (c, 0))
    return pl.pallas_call(
        body, name=name, grid=(T // rows,),
        in_specs=[blk, blk, pl.BlockSpec((rows, AW), lambda c: (c, 1)),
                  pl.BlockSpec((H, CHUNK, CHUNK), const3), pl.BlockSpec((H, CHUNK, CHUNK), const3),
                  pl.BlockSpec((CHUNK, AW), lambda c: (0, 0))],
        out_specs=[blk, blk, pl.BlockSpec((H, CHUNK, CHUNK), const3), pl.BlockSpec((CHUNK, AW), lambda c: (0, 0))],
        out_shape=[_sds((T, AW), MXU_DTYPE), _sds((T, AW), MXU_DTYPE), _sds((H, CHUNK, CHUNK), F32), _sds((CHUNK, AW), F32)],
        compiler_params=_cparams("arbitrary"),
    )(dymix, p, p, w_tril, w_tril_t, bmat)


def _shift_down(z, s, row):
    return jnp.where(row >= s, pltpu.roll(z, s, 0), 0.0)


def _shift_up(z, s, row, T):
    return jnp.where(row < T - s, pltpu.roll(z, T - s, 0), 0.0)


def _conv_fwd(p, w_conv, AW, name):
    T = p.shape[0]
    BW = w_conv.shape[1]
    nb = BW // LANE
    b0 = 2 * AW // LANE

    def body(b_ref, c_ref, x_ref, w_ref, y_ref):
        row = lax.broadcasted_iota(jnp.int32, (T, LANE), 0)
        z = c_ref[...].astype(F32) * x_ref[...].astype(F32)
        w0, w1, w2 = w_ref[0:1, :], w_ref[1:2, :], w_ref[2:3, :]
        conv = w2 * z + w1 * _shift_down(z, 1, row) + w0 * _shift_down(z, 2, row)
        y_ref[...] = (b_ref[...].astype(F32) * conv).astype(y_ref.dtype)

    return pl.pallas_call(
        body, name=name, grid=(nb,),
        in_specs=[pl.BlockSpec((T, LANE), lambda j: (0, b0 + j)), pl.BlockSpec((T, LANE), lambda j: (0, b0 + nb + j)),
                  pl.BlockSpec((T, LANE), lambda j: (0, b0 + 2 * nb + j)), pl.BlockSpec((CONV_WIDTH, LANE), lambda j: (0, j))],
        out_specs=pl.BlockSpec((T, LANE), lambda j: (0, j)),
        out_shape=_sds((T, BW), MXU_DTYPE),
        compiler_params=_cparams("parallel"),
    )(p, p, p, w_conv)


def _conv_bwd(dymix, p, w_conv, AW, name):
    T = p.shape[0]
    BW = w_conv.shape[1]
    nb = BW // LANE
    b0 = 2 * AW // LANE
    y0 = AW // LANE

    def body(dy_ref, b_ref, c_ref, x_ref, w_ref, db_ref, dc_ref, dxb_ref, dw_ref):
        row = lax.broadcasted_iota(jnp.int32, (T, LANE), 0)
        cv, xv, dy = c_ref[...].astype(F32), x_ref[...].astype(F32), dy_ref[...].astype(F32)
        w0, w1, w2 = w_ref[0:1, :], w_ref[1:2, :], w_ref[2:3, :]
        z = cv * xv
        z1 = _shift_down(z, 1, row)
        z2 = _shift_down(z, 2, row)
        conv = w2 * z + w1 * z1 + w0 * z2
        db_ref[...] = (dy * conv).astype(db_ref.dtype)
        dconv = dy * b_ref[...].astype(F32)
        dz = w2 * dconv + w1 * _shift_up(dconv, 1, row, T) + w0 * _shift_up(dconv, 2, row, T)
        dc_ref[...] = (dz * xv).astype(dc_ref.dtype)
        dxb_ref[...] = (dz * cv).astype(dxb_ref.dtype)
        dw_ref[0:1, :] = jnp.sum(dconv * z2, axis=0, keepdims=True)
        dw_ref[1:2, :] = jnp.sum(dconv * z1, axis=0, keepdims=True)
        dw_ref[2:3, :] = jnp.sum(dconv * z, axis=0, keepdims=True)

    col = lambda j: (0, j)
    return pl.pallas_call(
        body, name=name, grid=(nb,),
        in_specs=[pl.BlockSpec((T, LANE), lambda j: (0, y0 + j)),
                  pl.BlockSpec((T, LANE), lambda j: (0, b0 + j)), pl.BlockSpec((T, LANE), lambda j: (0, b0 + nb + j)),
                  pl.BlockSpec((T, LANE), lambda j: (0, b0 + 2 * nb + j)), pl.BlockSpec((CONV_WIDTH, LANE), col)],
        out_specs=[pl.BlockSpec((T, LANE), col)] * 3 + [pl.BlockSpec((CONV_WIDTH, LANE), col)],
        out_shape=[_sds((T, BW), MXU_DTYPE)] * 3 + [_sds((CONV_WIDTH, BW), F32)],
        compiler_params=_cparams("parallel"),
    )(dymix, p, p, p, w_conv)


def _head_sum(x, col_head, n_heads):
    out = jnp.zeros_like(x)
    for h in range(n_heads):
        sel = col_head == h
        out = jnp.where(sel, jnp.sum(jnp.where(sel, x, 0.0), axis=-1, keepdims=True), out)
    return out


def _same_head(width):
    assert width == 2 * HEAD_DIM
    return lax.broadcasted_iota(jnp.int32, (1, width), 1) < HEAD_DIM


def _head_sum2(x, first):
    s0 = jnp.sum(jnp.where(first, x, 0.0), axis=-1, keepdims=True)
    s1 = jnp.sum(jnp.where(first, 0.0, x), axis=-1, keepdims=True)
    return jnp.where(first, s0, s1)


def _head_norm(x, g, first):
    r = lax.rsqrt(_head_sum2(x * x, first) * (1.0 / HEAD_DIM) + EPS)
    return x * r * g, r


def _head_norm_bwd(dy, x, g, r, first):
    gdy = dy * g
    mean_xg = _head_sum2(x * gdy, first) * (1.0 / HEAD_DIM)
    return r * gdy - x * (r * r * r) * mean_xg, dy * x * r


ATT_SPAN_MIN = 512
ATT_FWD_UNROLL = 4
ATT_BWD_UNROLL = 4
HEADS_PER_LANES = LANE // HEAD_DIM


def _attn_geometry(T, d):
    m = max(1, ATT_SPAN_MIN // (ATT_BLK * d))
    return m, ATT_BLK * d * m, ATT_BLK * d, T // (ATT_BLK * d)


def _rows(ref, start, d):
    return ref[pl.ds(start, ATT_BLK, stride=d), :] if d > 1 else ref[pl.ds(start, ATT_BLK), :]


def _set_rows(ref, start, d, value):
    if d > 1:
        ref[pl.ds(start, ATT_BLK, stride=d), :] = value
    else:
        ref[pl.ds(start, ATT_BLK), :] = value


def _for_each_block(d, m, task, unroll):
    for j in range(m):
        if d == 1:
            task(0, j)
        else:
            lax.fori_loop(0, d, lambda r, carry, j=j: (task(r, j), carry)[1], 0, unroll=min(unroll, d))


def _head_slices():
    return [slice(h * HEAD_DIM, (h + 1) * HEAD_DIM) for h in range(HEADS_PER_LANES)]


def _qk_norm_fwd(p, gains, q_start, PW, name, tr=512):
    T = p.shape[0]
    n_norm = gains.shape[1] // PW
    n = n_norm * 3 // 2
    c0 = q_start // PW

    def body(*refs):
        x_refs, g_ref, out_ref = refs[:n], refs[n], refs[n + 1]
        first = _same_head(LANE)
        for i in range(n):
            for c in range(0, PW, LANE):
                lo = i * PW + c
                x = x_refs[i][:, c:c + LANE].astype(F32)
                out_ref[:, lo:lo + LANE] = _head_norm(x, g_ref[:, lo:lo + LANE], first)[0] if i < n_norm else x

    return pl.pallas_call(
        body, name=name, grid=(T // tr,),
        in_specs=[pl.BlockSpec((tr, PW), lambda i, j=j: (i, c0 + j)) for j in range(n)]
        + [pl.BlockSpec((1, n_norm * PW), lambda i: (0, 0))],
        out_specs=pl.BlockSpec((tr, n * PW), lambda i: (i, 0)), out_shape=_sds((T, n * PW), F32),
        compiler_params=_cparams("parallel"),
    )(*([p] * n), gains)


def _qk_norm_bwd(dns, p, gains, q_start, PW, name, tr=512):
    T = p.shape[0]
    n = len(dns)
    c0 = q_start // PW

    def body(*refs):
        d_refs, x_refs, g_ref, out_ref, acc_ref = refs[:n], refs[n:2 * n], refs[2 * n], refs[2 * n + 1], refs[2 * n + 2]

        @pl.when(pl.program_id(0) == 0)
        def _():
            acc_ref[...] = jnp.zeros_like(acc_ref)

        first = _same_head(LANE)
        for i in range(n):
            for c in range(0, PW, LANE):
                lo = i * PW + c
                x, gv = x_refs[i][:, c:c + LANE].astype(F32), g_ref[:, lo:lo + LANE]
                _, r = _head_norm(x, gv, first)
                dx, g_part = _head_norm_bwd(d_refs[i][:, c:c + LANE], x, gv, r, first)
                out_ref[:, lo:lo + LANE] = dx.astype(out_ref.dtype)
                acc_ref[0:1, lo:lo + LANE] += jnp.sum(g_part, axis=0, keepdims=True)

    return pl.pallas_call(
        body, name=name, grid=(T // tr,),
        in_specs=[pl.BlockSpec((tr, PW), lambda i: (i, 0))] * n
        + [pl.BlockSpec((tr, PW), lambda i, j=j: (i, c0 + j)) for j in range(n)]
        + [pl.BlockSpec((1, n * PW), lambda i: (0, 0))],
        out_specs=[pl.BlockSpec((tr, n * PW), lambda i: (i, 0)), pl.BlockSpec((8, n * PW), lambda i: (0, 0))],
        out_shape=[_sds((T, n * PW), MXU_DTYPE), _sds((8, n * PW), F32)],
        compiler_params=_cparams("arbitrary"),
    )(*dns, *([p] * n), gains)


def _attn_fwd(qkv, g, d, PW, name):
    T = qkv.shape[0]
    B, W = ATT_BLK, LANE
    m, span, group, _ = _attn_geometry(T, d)
    c_q = g * PW // W
    c_k, c_v = c_q + 3 * PW // W, c_q + 6 * PW // W
    scale = HEAD_DIM ** -0.5

    def body(q_ref, k_ref, v_ref, kp_ref, vp_ref, o_ref, lse_ref):
        n = pl.program_id(1)
        qi = lax.broadcasted_iota(jnp.int32, (B, 2 * B), 0)
        kj = lax.broadcasted_iota(jnp.int32, (B, 2 * B), 1)
        band = (kj >= qi) & (kj <= qi + B)

        def task(r, j):
            cur = j * group + r
            if j == 0:
                kp, vp = _rows(kp_ref, r, d), _rows(vp_ref, r, d)
            else:
                kp, vp = _rows(k_ref, cur - group, d), _rows(v_ref, cur - group, d)
            mask = band & ((n * m + j > 0) | (kj >= B))
            qn = _rows(q_ref, cur, d).astype(MXU_DTYPE)
            kn = jnp.concatenate([kp, _rows(k_ref, cur, d)], axis=0).astype(MXU_DTYPE)
            vcat = jnp.concatenate([vp, _rows(v_ref, cur, d)], axis=0).astype(MXU_DTYPE)
            o_parts, lse_parts = [], []
            for sl in _head_slices():
                s = lax.dot_general(qn[:, sl], kn[:, sl], (((1,), (1,)), ((), ())), preferred_element_type=F32) * scale
                s = jnp.where(mask, s, NEG_INF)
                mx = jnp.max(s, axis=-1, keepdims=True)
                e = jnp.exp(s - mx)
                den = jnp.sum(e, axis=-1, keepdims=True)
                o_parts.append(jnp.dot(e.astype(MXU_DTYPE), vcat[:, sl], preferred_element_type=F32) / den)
                lse_parts.append(jnp.broadcast_to(mx + jnp.log(den), (B, HEAD_DIM)))
            _set_rows(o_ref, cur, d, jnp.concatenate(o_parts, axis=1))
            _set_rows(lse_ref, cur, d, jnp.concatenate(lse_parts, axis=1))

        _for_each_block(d, m, task, ATT_FWD_UNROLL)

    main = lambda c0: pl.BlockSpec((span, W), lambda hp, n: (n, c0 + hp))
    prev = lambda c0: pl.BlockSpec((group, W), lambda hp, n: (jnp.maximum(n * m - 1, 0), c0 + hp))
    out_blk = pl.BlockSpec((span, W), lambda hp, n: (n, hp))
    return pl.pallas_call(
        body, name=name, grid=(PW // W, T // span),
        in_specs=[main(c_q), main(c_k), main(c_v), prev(c_k), prev(c_v)],
        out_specs=[out_blk, out_blk],
        out_shape=[_sds((T, PW), F32), _sds((T, PW), F32)],
        compiler_params=_cparams("parallel", "parallel"),
    )(qkv, qkv, qkv, qkv, qkv)


def _attn_bwd(qkv, lse, do, corr, g, d, PW, name):
    T = qkv.shape[0]
    B, W = ATT_BLK, LANE
    m, span, group, n_blocks = _attn_geometry(T, d)
    c_q = g * PW // W
    c_k, c_v = c_q + 3 * PW // W, c_q + 6 * PW // W
    scale = HEAD_DIM ** -0.5
    nt = (((1,), (1,)), ((), ()))
    tn = (((0,), (0,)), ((), ()))

    def body(q_ref, k_ref, v_ref, do_ref, l_ref, c_ref, kp_ref, vp_ref, qx_ref, dox_ref, lx_ref, cx_ref,
             dq_ref, dk_ref, dv_ref):
        n = pl.program_id(1)
        i1 = lax.broadcasted_iota(jnp.int32, (B, B), 0)
        j1 = lax.broadcasted_iota(jnp.int32, (B, B), 1)
        i2 = lax.broadcasted_iota(jnp.int32, (2 * B, B), 0)
        j2 = lax.broadcasted_iota(jnp.int32, (2 * B, B), 1)

        def task(r, j):
            cur = j * group + r
            blk = n * m + j
            q_c, k_c, v_c = _rows(q_ref, cur, d), _rows(k_ref, cur, d), _rows(v_ref, cur, d)
            do_c, l_c, c_c = _rows(do_ref, cur, d), _rows(l_ref, cur, d), _rows(c_ref, cur, d)
            if j == 0:
                k_p, v_p = _rows(kp_ref, r, d), _rows(vp_ref, r, d)
            else:
                k_p, v_p = _rows(k_ref, cur - group, d), _rows(v_ref, cur - group, d)
            if j == m - 1:
                nxt = [_rows(ref, r, d) for ref in (qx_ref, dox_ref, lx_ref, cx_ref)]
            else:
                nxt = [_rows(ref, cur + group, d) for ref in (q_ref, do_ref, l_ref, c_ref)]
            q_x, do_x, l_x, c_x = nxt
            kn_c, kn_p, v_c, v_p = (a.astype(MXU_DTYPE) for a in (k_c, k_p, v_c, v_p))
            qn_c = q_c.astype(MXU_DTYPE)
            qn_cat = jnp.concatenate([qn_c, q_x.astype(MXU_DTYPE)], axis=0)
            do_cb = do_c.astype(MXU_DTYPE)
            do_cat = jnp.concatenate([do_cb, do_x.astype(MXU_DTYPE)], axis=0)
            l_cat = jnp.concatenate([l_c, l_x], axis=0)
            c_cat = jnp.concatenate([c_c, c_x], axis=0)
            mask_p = (j1 >= i1) & (blk > 0)
            mask_c = ((i2 < B) & (j2 <= i2)) | ((i2 >= B) & (j2 >= i2 - B) & (blk + 1 < n_blocks))
            dqn, dkn, dv = [], [], []
            for h, sl in enumerate(_head_slices()):
                lane = slice(h * HEAD_DIM, h * HEAD_DIM + 1)
                s_p = lax.dot_general(qn_c[:, sl], kn_p[:, sl], nt, preferred_element_type=F32) * scale
                pr_p = jnp.where(mask_p, jnp.exp(s_p - l_c[:, lane]), 0.0)
                dp_p = lax.dot_general(do_cb[:, sl], v_p[:, sl], nt, preferred_element_type=F32)
                ds_p = (pr_p * (dp_p + c_c[:, lane]) * scale).astype(MXU_DTYPE)
                s_c = lax.dot_general(qn_cat[:, sl], kn_c[:, sl], nt, preferred_element_type=F32) * scale
                pr_c = jnp.where(mask_c, jnp.exp(s_c - l_cat[:, lane]), 0.0)
                dp_c = lax.dot_general(do_cat[:, sl], v_c[:, sl], nt, preferred_element_type=F32)
                ds_c = (pr_c * (dp_c + c_cat[:, lane]) * scale).astype(MXU_DTYPE)
                dqn.append(jnp.dot(ds_p, kn_p[:, sl], preferred_element_type=F32)
                           + jnp.dot(ds_c[:B], kn_c[:, sl], preferred_element_type=F32))
                dkn.append(lax.dot_general(ds_c, qn_cat[:, sl], tn, preferred_element_type=F32))
                dv.append(lax.dot_general(pr_c.astype(MXU_DTYPE), do_cat[:, sl], tn, preferred_element_type=F32))
            _set_rows(dq_ref, cur, d, jnp.concatenate(dqn, axis=1))
            _set_rows(dk_ref, cur, d, jnp.concatenate(dkn, axis=1))
            _set_rows(dv_ref, cur, d, jnp.concatenate(dv, axis=1))

        _for_each_block(d, m, task, ATT_BWD_UNROLL)

    main = lambda c0: pl.BlockSpec((span, W), lambda hp, n: (n, c0 + hp))
    prev = lambda c0: pl.BlockSpec((group, W), lambda hp, n: (jnp.maximum(n * m - 1, 0), c0 + hp))
    nxt = lambda c0: pl.BlockSpec((group, W), lambda hp, n: (jnp.minimum((n + 1) * m, n_blocks - 1), c0 + hp))
    own = pl.BlockSpec((span, W), lambda hp, n: (n, hp))
    return pl.pallas_call(
        body, name=name, grid=(PW // W, T // span),
        in_specs=[main(c_q), main(c_k), main(c_v), main(0), main(0), main(0), prev(c_k), prev(c_v),
                  nxt(c_q), nxt(0), nxt(0), nxt(0)],
        out_specs=[own, own, own],
        out_shape=[_sds((T, PW), F32)] * 3,
        compiler_params=_cparams("parallel", "parallel"),
    )(qkv, qkv, qkv, do, lse, corr, qkv, qkv, qkv, do, lse, corr)


def _softmax3(lses):
    mx = jnp.maximum(jnp.maximum(lses[0], lses[1]), lses[2])
    ex = [jnp.exp(l - mx) for l in lses]
    inv = 1.0 / (ex[0] + ex[1] + ex[2])
    return [e * inv for e in ex]


def _mix_fwd(os_, lses, name, tr=512):
    T, PW = os_[0].shape

    def body(o0, o1, o2, l0, l1, l2, y_ref):
        alpha = _softmax3([l0[...], l1[...], l2[...]])
        for g, o_ref in enumerate((o0, o1, o2)):
            y_ref[:, g * PW:(g + 1) * PW] = (o_ref[...] * alpha[g]).astype(y_ref.dtype)

    blk = pl.BlockSpec((tr, PW), lambda i: (i, 0))
    return pl.pallas_call(
        body, name=name, grid=(T // tr,),
        in_specs=[blk] * 6, out_specs=pl.BlockSpec((tr, 3 * PW), lambda i: (i, 0)),
        out_shape=_sds((T, 3 * PW), MXU_DTYPE),
        compiler_params=_cparams("parallel"),
    )(*os_, *lses)


def _mix_bwd(dymix, os_, lses, c_start, name, tr=512):
    T, PW = os_[0].shape
    HP = PW // HEAD_DIM
    c0 = c_start // PW

    def body(d0, d1, d2, o0, o1, o2, l0, l1, l2, do0, do1, do2, dl0, dl1, dl2):
        col_head = lax.broadcasted_iota(jnp.int32, (tr, PW), 1) // HEAD_DIM
        alpha = _softmax3([l0[...], l1[...], l2[...]])
        dys = [d0[...].astype(F32), d1[...].astype(F32), d2[...].astype(F32)]
        dots = [_head_sum(dy * o_ref[...], col_head, HP) for dy, o_ref in zip(dys, (o0, o1, o2))]
        mean_dot = alpha[0] * dots[0] + alpha[1] * dots[1] + alpha[2] * dots[2]
        for g, (do_ref, dl_ref) in enumerate(((do0, dl0), (do1, dl1), (do2, dl2))):
            do_ref[...] = dys[g] * alpha[g]
            dl_ref[...] = -alpha[g] * mean_dot

    blk = pl.BlockSpec((tr, PW), lambda i: (i, 0))
    dy_specs = [pl.BlockSpec((tr, PW), lambda i, g=g: (i, c0 + g)) for g in range(3)]
    outs = pl.pallas_call(
        body, name=name, grid=(T // tr,),
        in_specs=dy_specs + [blk] * 6, out_specs=[blk] * 6,
        out_shape=[_sds((T, PW), F32)] * 6,
        compiler_params=_cparams("parallel"),
    )(dymix, dymix, dymix, *os_, *lses)
    return outs[:3], outs[3:]


def _adamw_math(w, g, m, v):
    m2 = ADAM_B1 * m + (1.0 - ADAM_B1) * g
    v2 = ADAM_B2 * v + (1.0 - ADAM_B2) * (g * g)
    m_hat = m2 / (1.0 - ADAM_B1 ** ADAM_STEP)
    v_hat = v2 / (1.0 - ADAM_B2 ** ADAM_STEP)
    delta = -ADAM_LR * (m_hat / (jnp.sqrt(v_hat) + ADAM_EPS) + ADAM_WD * w)
    return delta, m2, v2


def _adamw_layer(layer, w, m, v, own, landed, me, prev, name, tr=256):
    _, R, C = w.shape
    tr = next(t for t in range(min(tr, R) // 16 * 16, 0, -16) if R % t == 0)

    def body(me_ref, w_ref, m_ref, v_ref, own_ref, land_ref, *rest):
        g_ref, d_ref, m2_ref, v2_ref = rest[-4:]
        g = own_ref[...].astype(F32)
        for j in range(N_PEER):
            g = g + land_ref[j].astype(F32)
        delta, m2, v2 = _adamw_math(w_ref[...], g, m_ref[...], v_ref[...])
        g_ref[...] = g
        d_ref[...] = delta
        m2_ref[...] = m2
        v2_ref[...] = v2

    lay = pl.BlockSpec((None, tr, C), lambda i, me_ref: (layer, i, 0))
    in_specs = [lay, lay, lay, pl.BlockSpec((None, tr, C), lambda i, me_ref: (me_ref[0], i, 0)),
                pl.BlockSpec((N_PEER, tr, C), lambda i, me_ref: (0, i, 0))]
    args = [me, w, m, v, own, landed]
    aliases = {}
    if prev is not None:
        in_specs += [pl.BlockSpec(memory_space=pl.ANY)] * 4
        args += list(prev)
        aliases = {6 + i: i for i in range(4)}
    return pl.pallas_call(
        body, name=name,
        grid_spec=pltpu.PrefetchScalarGridSpec(num_scalar_prefetch=1, grid=(R // tr,), in_specs=in_specs, out_specs=[lay] * 4),
        out_shape=[_sds(w.shape, F32)] * 4,
        input_output_aliases=aliases,
        compiler_params=_cparams("parallel"),
    )(*args)


def _sum_parts(parts, name):
    _, R, C = parts.shape

    def body(p_ref, out_ref):
        g = p_ref[0]
        for j in range(1, N_DEV):
            g = g + p_ref[j]
        out_ref[...] = g

    return pl.pallas_call(
        body, name=name, grid=(1,),
        in_specs=[pl.BlockSpec((N_DEV, R, C), lambda i: (0, 0, 0))], out_specs=pl.BlockSpec((R, C), lambda i: (0, 0)),
        out_shape=_sds((R, C), F32), compiler_params=_cparams("arbitrary"),
    )(parts)


def _adamw_small(ws, gs, ms, vs, name):
    n = len(ws)

    def body(*refs):
        ins, outs = refs[:4 * n], refs[4 * n:]
        for i in range(n):
            delta, m2, v2 = _adamw_math(ins[i][...], ins[n + i][...], ins[2 * n + i][...], ins[3 * n + i][...])
            outs[i][...] = delta
            outs[n + i][...] = m2
            outs[2 * n + i][...] = v2

    outs = pl.pallas_call(
        body, name=name, out_shape=[_sds(w.shape, F32) for w in ws] * 3,
    )(*ws, *gs, *ms, *vs)
    return outs[:n], outs[n:2 * n], outs[2 * n:]


def _pack(arrays, rows_multiple=8):
    flat = []
    for a in arrays:
        a = a.reshape(-1).astype(F32)
        flat.append(jnp.pad(a, (0, (-a.shape[0]) % LANE)))
    flat = jnp.concatenate(flat)
    flat = jnp.pad(flat, (0, (-flat.shape[0]) % (LANE * rows_multiple)))
    return flat.reshape(-1, LANE)


def _unpack(packed, shapes):
    flat = packed.reshape(-1)
    out, off = [], 0
    for s in shapes:
        size = 1
        for dim in s:
            size *= dim
        out.append(flat[off:off + size].reshape(s))
        off += size + (-size) % LANE
    return out


def _layer_fwd(x, wts, getw, dims, dep=None):
    AW, BW, PW, DP = dims["AW"], dims["BW"], dims["PW"], dims["DP"]
    q_start = 2 * AW + 3 * BW
    h, r1 = _rmsnorm_fwd(x, wts["attn_norm"], "rmsnorm_fwd", dep=dep)
    p = _mm_nt(h, getw("w_in", h), "proj_in", out_dtype=MXU_DTYPE, tm=1024, tn=1408)
    y_a = _sgu_fwd(p, wts["sgu_tril"], wts["sgu_bmat"], "sgu_fwd")
    y_b = _conv_fwd(p, getw("conv_w", y_a), AW, "conv_fwd")
    qkn = _qk_norm_fwd(p, wts["qk_gain"], q_start, PW, "qk_norm_fwd")
    os_, lses = [], []
    for g, d in enumerate(DILATIONS):
        o, lse = _attn_fwd(qkn, g, d, PW, "attn_fwd_%d" % d)
        os_.append(o)
        lses.append(lse)
    y_c = _mix_fwd(os_, lses, "mix_fwd")
    ymix = jnp.concatenate([y_a, y_b, y_c], axis=1)
    x1 = _mm_nn(ymix, getw("w_out", ymix), "proj_out", residual=x, tm=1024, tn=1024)
    h2, r2 = _rmsnorm_fwd(x1, wts["mlp_norm"], "rmsnorm_fwd")
    a, hid = _mm_nn(h2, getw("w_mlp_in", h2), "mlp_in", out_dtype=MXU_DTYPE, relu2=True, tm=1024, tn=1024)
    x2 = _mm_nn(hid, getw("w_mlp_out", hid), "mlp_out", residual=x1, columns_first=True, tm=512, tn=1024)
    saved = dict(x=x, h=h, r1=r1, p=p, qkn=qkn, os=os_, lses=lses, ymix=ymix, x1=x1, h2=h2, r2=r2, a=a, hid=hid)
    return x2, saved


def _layer_bwd(dx, dxb, wts, getw, scatter, saved, dims):
    AW, BW, PW, DP = dims["AW"], dims["BW"], dims["PW"], dims["DP"]
    q_start = 2 * AW + 3 * BW
    D = dx.shape[1]
    g_w2 = _wgrad_wide_a(saved["hid"], dxb, "mlp_out_wgrad")
    token = scatter("w_mlp_out", g_w2.reshape(N_DEV, -1, D))
    da = _mm_nt(dxb, getw("w_mlp_out", None), "mlp_out_dgrad", out_dtype=MXU_DTYPE, relu2_pre=saved["a"], dep=token,
                tm=1024, tn=1024)
    g_w1 = _wgrad_wide_b(saved["h2"], da, "mlp_in_wgrad", groups=N_DEV)
    token = scatter("w_mlp_in", g_w1)
    dh2 = _mm_nt(da, getw("w_mlp_in", None), "mlp_in_dgrad", out_dtype=MXU_DTYPE, dep=token, tm=1024, tn=512)
    dx1, dx1b, g_mlp_norm = _rmsnorm_bwd(dh2, saved["x1"], wts["mlp_norm"], saved["r2"], dx, "rmsnorm_bwd")
    g_wout = _wgrad_wide_b(saved["ymix"], dx1b, "proj_out_wgrad")
    token = scatter("w_out", g_wout.reshape(N_DEV, -1, D))
    dymix = _mm_nt(dx1b, getw("w_out", None), "proj_out_dgrad", out_dtype=MXU_DTYPE, dep=token, tm=1024, tn=1024)
    p = saved["p"]
    du, dv, g_sgu_w, g_sgu_bmat = _sgu_bwd(dymix, p, wts["sgu_tril"], wts["sgu_tril_t"], wts["sgu_bmat"], "sgu_bwd")
    d_b, d_c, d_xb, g_conv = _conv_bwd(dymix, p, getw("conv_w", None), AW, "conv_bwd")
    dos, corrs = _mix_bwd(dymix, saved["os"], saved["lses"], AW + BW, "mix_bwd")
    dqns, dkns, dvs = [], [], []
    for g, d in enumerate(DILATIONS):
        dqn, dkn, dvv = _attn_bwd(saved["qkn"], saved["lses"][g], dos[g], corrs[g], g, d, PW, "attn_bwd_%d" % d)
        dqns.append(dqn)
        dkns.append(dkn)
        dvs.append(dvv.astype(MXU_DTYPE))
    dqk, g_qk = _qk_norm_bwd(dqns + dkns, p, wts["qk_gain"], q_start, PW, "qk_norm_bwd")
    g_q, g_k = (part.reshape(-1, HEAD_DIM).sum(0) for part in jnp.split(g_qk[0], 2))
    dp = jnp.concatenate([du, dv, d_b, d_c, d_xb, dqk] + dvs, axis=1)
    g_win_t = _wgrad_wide_a(dp, saved["h"], "proj_in_wgrad")
    token = scatter("w_in", g_win_t.reshape(N_DEV, DP // N_DEV, D))
    dh = _mm_nn(dp, getw("w_in", None), "proj_in_dgrad", out_dtype=MXU_DTYPE, dep=token, tm=1024, tn=1024)
    dx0, dx0b, g_attn_norm = _rmsnorm_bwd(dh, saved["x"], wts["attn_norm"], saved["r1"], dx1, "rmsnorm_bwd")
    H = AW // HEAD_DIM
    tril = jnp.tril(jnp.ones((CHUNK, CHUNK), F32))
    small = [g_attn_norm.reshape(-1), g_sgu_w * tril, g_sgu_bmat.reshape(CHUNK, H, HEAD_DIM).sum(-1).T,
             g_conv, g_q, g_k, g_mlp_norm.reshape(-1)]
    return dx0, dx0b, small


def kernel(x, attn_norm, w_in, sgu_w, sgu_b, conv_w, q_norm, k_norm, w_out, mlp_norm, w_mlp_in, w_mlp_out, loss_target, m_attn_norm, m_w_in, m_sgu_w, m_sgu_b, m_conv_w, m_q_norm, m_k_norm, m_w_out, m_mlp_norm, m_w_mlp_in, m_w_mlp_out, v_attn_norm, v_w_in, v_sgu_w, v_sgu_b, v_conv_w, v_q_norm, v_k_norm, v_w_out, v_mlp_norm, v_w_mlp_in, v_w_mlp_out):
    n_layers = attn_norm.shape[0]
    T, D = x.shape[1], x.shape[2]
    H = sgu_w.shape[1]
    AW = H * HEAD_DIM
    BW = conv_w.shape[2] * N_DEV
    DP = w_in.shape[2] * N_DEV
    DMIX = w_out.shape[1] * N_DEV
    DFF = w_mlp_in.shape[2] * N_DEV
    PW = (DMIX - AW - BW) // 3
    HP = PW // HEAD_DIM
    dims = dict(AW=AW, BW=BW, PW=PW, DP=DP)
    me = 4 * lax.axis_index("x") + 2 * lax.axis_index("y") + lax.axis_index("c")

    big_names = ("w_in", "w_out", "w_mlp_in", "w_mlp_out")
    tr_in = lambda a: jnp.swapaxes(a, 1, 2)
    big_w = dict(zip(big_names, (tr_in(w_in), w_out, w_mlp_in, w_mlp_out)))
    big_m = dict(zip(big_names, (tr_in(m_w_in), m_w_out, m_w_mlp_in, m_w_mlp_out)))
    big_v = dict(zip(big_names, (tr_in(v_w_in), v_w_out, v_w_mlp_in, v_w_mlp_out)))

    keys = []
    for l in range(n_layers):
        keys += [(l, nm) for nm in big_names]
    keys.insert(1, (0, "conv_w"))
    srcs = [_pack([conv_w]) if nm == "conv_w" else big_w[nm][l].astype(MXU_DTYPE) for l, nm in keys]
    flights, gather_token = _exchange_start(srcs, [_own_in_place(s, me) for s in srcs], "gather", name="gather_start")
    arriving = dict(zip(keys, flights))
    forwarding = {}
    relayout = dict(
        w_in=lambda g: g.reshape(DP, D), w_out=lambda g: g.reshape(DMIX, D),
        w_mlp_in=lambda g: g, w_mlp_out=lambda g: g.reshape(DFF, D),
        conv_w=lambda g: jnp.stack([_unpack(g[j], [conv_w.shape])[0] for j in range(N_DEV)], axis=2).reshape(
            n_layers, CONV_WIDTH, BW))
    gathered = {}

    def forward(key, after):
        _, land = _exchange_wait(arriving[key], after, "gather", name="gather_arrive_%d_%s" % key)
        fl, token = _exchange_start(None, [land], "forward", name="gather_forward_%d_%s" % key)
        forwarding[key] = fl[0]
        return token

    def weight_getter(l):
        def getw(nm, after):
            key = (0, nm) if nm == "conv_w" else (l, nm)
            if key not in gathered:
                ahead = keys[keys.index(key):][:2]
                for k in ahead:
                    if k not in forwarding:
                        after = forward(k, after)
                _, land = _exchange_wait(forwarding[key], after, "forward", name="gather_wait_%d_%s" % key)
                gathered[key] = relayout[nm](land)
            return gathered[key][l] if nm == "conv_w" else gathered[key]
        return getw

    tril = jnp.tril(jnp.ones((CHUNK, CHUNK), F32))
    layers = []
    for l in range(n_layers):
        w_tril = sgu_w[l] * tril
        layers.append(dict(
            attn_norm=attn_norm[l][None], mlp_norm=mlp_norm[l][None],
            sgu_tril=w_tril.astype(MXU_DTYPE), sgu_tril_t=w_tril.transpose(0, 2, 1).astype(MXU_DTYPE),
            sgu_bmat=jnp.repeat(sgu_b[l].T, HEAD_DIM, axis=1),
            qk_gain=jnp.concatenate([jnp.tile(q_norm[l], 3 * HP), jnp.tile(k_norm[l], 3 * HP)])[None]))

    xs = x[0]
    saved = []
    for l in range(n_layers):
        xs, sv = _layer_fwd(xs, layers[l], weight_getter(l), dims, dep=gather_token if l == 0 else None)
        saved.append(sv)
    loss_blk, dx, dxb = _loss_and_grad(xs, loss_target[0], "loss")
    loss = lax.psum(loss_blk[0, 0], ("x", "y", "c"))

    scattering = {}

    def scatter_starter(l):
        def scatter(nm, partials):
            land = lax.empty((N_PEER,) + partials.shape[1:], partials.dtype)
            fl, tok = _exchange_start([partials], [land], "scatter", name="scatter_start_%d_%s" % (l, nm))
            scattering[(l, nm)] = fl[0]
            return tok
        return scatter

    small = [None] * n_layers
    for l in reversed(range(n_layers)):
        dx, dxb, small[l] = _layer_bwd(dx, dxb, layers[l], weight_getter(l), scatter_starter(l), saved[l], dims)

    small_shapes = [s.shape for s in small[0]]
    small_src = [_pack([s for l in range(n_layers) for s in small[l]])]
    small_flights, small_token = _exchange_start(small_src, [_own_in_place(s, me) for s in small_src], "gather_all",
                                                 name="small_start")
    grad_x = dx[None]

    me1 = me.astype(jnp.int32).reshape(1)
    res = {nm: None for nm in big_names}
    after = small_token
    for l in reversed(range(n_layers)):
        for nm in reversed(big_names):
            own, landed = _exchange_wait(scattering[(l, nm)], after, "scatter", name="scatter_wait_%d_%s" % (l, nm))
            res[nm] = _adamw_layer(l, big_w[nm], big_m[nm], big_v[nm], own, landed, me1, res[nm], "adamw_" + nm)
            after = res[nm][0]
    res["w_in"] = [tr_in(a) for a in res["w_in"]]
    big_out = [res[nm] for nm in big_names]

    _, gathered_small = _exchange_wait(small_flights[0], after, "gather_all", name="small_wait")
    summed = _unpack(_sum_parts(gathered_small, "sum_small"), small_shapes * n_layers)
    ns = len(small_shapes)
    g_small = [jnp.stack([summed[l * ns + i] for l in range(n_layers)]) for i in range(ns)]
    g_attn_norm, g_sgu_w, g_sgu_b, g_conv_full, g_q, g_k, g_mlp_norm = g_small
    cs = conv_w.shape[2]
    g_conv = lax.dynamic_slice_in_dim(g_conv_full, me * cs, cs, axis=2)
    sm_w = (attn_norm, sgu_w, sgu_b, conv_w, q_norm, k_norm, mlp_norm)
    sm_m = (m_attn_norm, m_sgu_w, m_sgu_b, m_conv_w, m_q_norm, m_k_norm, m_mlp_norm)
    sm_v = (v_attn_norm, v_sgu_w, v_sgu_b, v_conv_w, v_q_norm, v_k_norm, v_mlp_norm)
    sm_g = (g_attn_norm, g_sgu_w, g_sgu_b, g_conv, g_q, g_k, g_mlp_norm)
    sm_delta, sm_m2, sm_v2 = _adamw_small(sm_w, sm_g, sm_m, sm_v, "adamw_small")

    def ordered(small_list, big_kind):
        b = [big_out[i][big_kind] for i in range(4)]
        return [small_list[0], b[0], small_list[1], small_list[2], small_list[3], small_list[4], small_list[5],
                b[1], small_list[6], b[2], b[3]]

    return (loss, grad_x, *ordered(list(sm_g), 0), *ordered(sm_delta, 1), *ordered(sm_m2, 2), *ordered(sm_v2, 3))
```

```python
import jax
import jax.numpy as jnp
from jax import lax
from jax.experimental import pallas as pl
from jax.experimental.pallas import tpu as pltpu

N_DEV = 8
HEAD_DIM = 64
CHUNK = 128
ATT_BLK = 128
DILATIONS = (1, 4, 16)
CONV_WIDTH = 3
EPS = 1e-6
ADAM_LR = 0.001
ADAM_B1 = 0.9
ADAM_B2 = 0.999
ADAM_EPS = 1e-08
ADAM_WD = 0.01
ADAM_STEP = 10
MXU_DTYPE = jnp.bfloat16
F32 = jnp.float32
LANE = 128
VMEM_LIMIT_BYTES = 56 * 1024 * 1024
NEG_INF = float("-inf")


def _cparams(*sem):
    return pltpu.CompilerParams(dimension_semantics=sem, vmem_limit_bytes=VMEM_LIMIT_BYTES)


def _sds(shape, dtype):
    return jax.ShapeDtypeStruct(shape, dtype)


def _fit(n, tile):
    for t in range(min(tile, n) // LANE * LANE, 0, -LANE):
        if n % t == 0:
            return t
    return n


_HBM = pl.BlockSpec(memory_space=pltpu.HBM)
_SEM = pl.BlockSpec(memory_space=pltpu.SEMAPHORE)
_DATAFLOW = pltpu.SideEffectType.DATAFLOW_SIDE_EFFECTING
N_PEER = N_DEV - 1


def _mesh_pos():
    x, y, c = lax.axis_index("x"), lax.axis_index("y"), lax.axis_index("c")
    return x, y, c, 4 * x + 2 * y + c


OTHER_CHIPS = (4, 2, 6)
EXCHANGE_PEERS = dict(
    scatter=tuple(range(1, N_DEV)),
    gather_all=tuple(range(1, N_DEV)),
    gather=(1,) + OTHER_CHIPS,
    forward=OTHER_CHIPS)


def _remote_copies(src, land, send_sems, recv_sems, mode):
    x, y, c, me = _mesh_pos()
    copies = []
    for i, k in enumerate(EXCHANGE_PEERS[mode]):
        px = (1 - x) if (k & 4) else x
        py = (1 - y) if (k & 2) else y
        pc = (1 - c) if (k & 1) else c
        if mode == "scatter":
            src_ref, dst_ref, dev = src.at[4 * px + 2 * py + pc], land.at[i], (px, py, pc)
        elif mode == "forward":
            slot = 4 * px + 2 * py + c
            src_ref, dst_ref, dev = land.at[slot], land.at[slot], (x, y, 1 - c)
        else:
            src_ref, dst_ref, dev = src, land.at[me], (px, py, pc)
        copies.append(pltpu.make_async_remote_copy(
            src_ref=src_ref, dst_ref=dst_ref, send_sem=send_sems.at[i], recv_sem=recv_sems.at[i],
            device_id=dev, device_id_type=pl.DeviceIdType.MESH))
    return copies


def _own_in_place(src, me):
    land = lax.empty((N_DEV,) + src.shape, src.dtype)
    return lax.dynamic_update_slice(land, src[None], (me,) + (0,) * src.ndim)


def _exchange_start(srcs, lands, mode, name):
    n = len(lands)
    has_src = srcs is not None
    arrays = (list(srcs) if has_src else []) + list(lands)
    n_arr = len(arrays)
    n_copies = len(EXCHANGE_PEERS[mode])

    def body(*refs):
        src = refs[:n] if has_src else [None] * n
        land = refs[n_arr - n:n_arr]
        send, recv = refs[n_arr:n_arr + n], refs[n_arr + n:n_arr + 2 * n]
        token = refs[2 * n_arr + 2 * n]
        for t in range(n):
            for cp in _remote_copies(src[t], land[t], send[t], recv[t], mode):
                cp.start()
        token[...] = jnp.zeros_like(token)

    outs = pl.pallas_call(
        body, name=name,
        out_shape=([pltpu.SemaphoreType.DMA((n_copies,))] * (2 * n) + [pltpu.HBM(a.shape, a.dtype) for a in arrays]
                   + [_sds((8, LANE), F32)]),
        in_specs=[_HBM] * n_arr,
        out_specs=[_SEM] * (2 * n) + [_HBM] * n_arr + [pl.BlockSpec(memory_space=pltpu.VMEM)],
        input_output_aliases={i: 2 * n + i for i in range(n_arr)},
        compiler_params=pltpu.CompilerParams(has_side_effects=_DATAFLOW),
    )(*[pltpu.with_memory_space_constraint(a, pltpu.HBM) for a in arrays])
    thru = outs[2 * n:2 * n + n_arr]
    flights = [(outs[t], outs[n + t], thru[t] if has_src else None, thru[n_arr - n + t]) for t in range(n)]
    return flights, outs[2 * n + n_arr]


def _exchange_wait(flight, after, mode, name):
    send, recv, src, land = flight
    arrays = [land] if src is None else [src, land]
    n_arr = len(arrays)

    def body(*refs):
        src_ref = refs[0] if n_arr == 2 else None
        land_ref, send_ref, recv_ref = refs[n_arr - 1], refs[n_arr], refs[n_arr + 1]
        for cp in _remote_copies(src_ref, land_ref, send_ref, recv_ref, mode):
            cp.wait_send()
            cp.wait_recv()

    outs = pl.pallas_call(
        body, name=name, out_shape=[pltpu.HBM(a.shape, a.dtype) for a in arrays],
        in_specs=[_HBM] * n_arr + [_SEM, _SEM, pl.BlockSpec(memory_space=pl.ANY)], out_specs=[_HBM] * n_arr,
        input_output_aliases={i: i for i in range(n_arr)},
        compiler_params=pltpu.CompilerParams(has_side_effects=_DATAFLOW),
    )(*arrays, send, recv, after)
    return (None, outs[0]) if src is None else (outs[0], outs[1])


def _rmsnorm_fwd(x, g, name, dep=None, tr=512):
    T, D = x.shape

    def body(x_ref, g_ref, *rest):
        h_ref, r_ref = rest[-2:]
        xv = x_ref[...]
        r = lax.rsqrt(jnp.mean(xv * xv, axis=-1, keepdims=True) + EPS)
        h_ref[...] = (xv * r * g_ref[...]).astype(h_ref.dtype)
        r_ref[...] = r

    in_specs = [pl.BlockSpec((tr, D), lambda i: (i, 0)), pl.BlockSpec((1, D), lambda i: (0, 0))]
    args = [x, g]
    if dep is not None:
        in_specs.append(pl.BlockSpec(dep.shape, lambda i: (0, 0)))
        args.append(dep)
    return pl.pallas_call(
        body, name=name, grid=(T // tr,),
        in_specs=in_specs,
        out_specs=[pl.BlockSpec((tr, D), lambda i: (i, 0)), pl.BlockSpec((tr, 1), lambda i: (i, 0))],
        out_shape=[_sds((T, D), MXU_DTYPE), _sds((T, 1), F32)],
        compiler_params=_cparams("parallel"),
    )(*args)


def _rmsnorm_bwd(dh, x, g, r, dres, name, tr=256):
    T, D = x.shape

    def body(dh_ref, x_ref, g_ref, r_ref, dres_ref, dx_ref, dxb_ref, dg_ref):
        @pl.when(pl.program_id(0) == 0)
        def _():
            dg_ref[...] = jnp.zeros_like(dg_ref)

        dh_v, xv, rv = dh_ref[...].astype(F32), x_ref[...], r_ref[...]
        gdy = dh_v * g_ref[...]
        mean_xg = jnp.mean(xv * gdy, axis=-1, keepdims=True)
        dx = dres_ref[...] + rv * gdy - xv * (rv * rv * rv) * mean_xg
        dx_ref[...] = dx
        dxb_ref[...] = dx.astype(dxb_ref.dtype)
        dg_ref[...] += jnp.sum(dh_v * xv * rv, axis=0, keepdims=True)

    row = lambda i: (i, 0)
    return pl.pallas_call(
        body, name=name, grid=(T // tr,),
        in_specs=[pl.BlockSpec((tr, D), row), pl.BlockSpec((tr, D), row), pl.BlockSpec((1, D), lambda i: (0, 0)),
                  pl.BlockSpec((tr, 1), row), pl.BlockSpec((tr, D), row)],
        out_specs=[pl.BlockSpec((tr, D), row), pl.BlockSpec((tr, D), row), pl.BlockSpec((1, D), lambda i: (0, 0))],
        out_shape=[_sds((T, D), F32), _sds((T, D), MXU_DTYPE), _sds((1, D), F32)],
        compiler_params=_cparams("arbitrary"),
    )(dh, x, g, r, dres)


def _mm_nn(a, b, name, out_dtype=F32, residual=None, relu2=False, dep=None, columns_first=False, tm=512, tn=512):
    M, K = a.shape
    grouped = b.ndim == 3
    N = b.shape[0] * b.shape[2] if grouped else b.shape[1]
    tm, tn = _fit(M, tm), _fit(b.shape[2] if grouped else N, tn)
    tile = (lambda j, i: (i, j)) if columns_first else (lambda i, j: (i, j))
    b_mode = pl.Buffered(1 if columns_first else 2)
    if grouped:
        per = b.shape[2] // tn
        b_spec = pl.BlockSpec((None, K, tn), lambda *g: (tile(*g)[1] // per, 0, tile(*g)[1] % per), pipeline_mode=b_mode)
    else:
        b_spec = pl.BlockSpec((K, tn), lambda *g: (0, tile(*g)[1]), pipeline_mode=b_mode)

    def body(*refs):
        a_ref, b_ref = refs[0], refs[1]
        r_ref = refs[2] if residual is not None else None
        out_ref = refs[2 + (residual is not None) + (dep is not None)]
        acc = jnp.dot(a_ref[...], b_ref[...], preferred_element_type=F32)
        if r_ref is not None:
            acc = acc + r_ref[...]
        if relu2:
            acc = jnp.maximum(acc, 0.0)
            acc = acc * acc
        out_ref[...] = acc.astype(out_ref.dtype)

    out_blk = pl.BlockSpec((tm, tn), lambda *g: tile(*g))
    in_specs = [pl.BlockSpec((tm, K), lambda *g: (tile(*g)[0], 0)), b_spec]
    args = [a, b]
    if residual is not None:
        in_specs.append(out_blk)
        args.append(residual)
    if dep is not None:
        in_specs.append(pl.BlockSpec(dep.shape, lambda *g: (0, 0)))
        args.append(dep)
    return pl.pallas_call(
        body, name=name, grid=(N // tn, M // tm) if columns_first else (M // tm, N // tn),
        in_specs=in_specs, out_specs=out_blk, out_shape=_sds((M, N), out_dtype),
        compiler_params=_cparams("parallel", "parallel"),
    )(*args)


def _mm_nt(a, b, name, out_dtype=F32, relu2_out=None, dep=None, tm=512, tn=512):
    M, K = a.shape
    grouped = b.ndim == 3
    N = b.shape[1] if grouped else b.shape[0]
    tm, tn = _fit(M, tm), _fit(N, tn)
    nt = (((1,), (1,)), ((), ()))
    if grouped:
        G, _, Kg = b.shape
        b_spec = pl.BlockSpec((G, tn, Kg), lambda i, j: (0, j, 0))
    else:
        b_spec = pl.BlockSpec((tn, K), lambda i, j: (j, 0))

    def body(*refs):
        a_ref, b_ref = refs[0], refs[1]
        y_ref = refs[2] if relu2_out is not None else None
        out_ref = refs[2 + (relu2_out is not None) + (dep is not None)]
        if grouped:
            acc = lax.dot_general(a_ref[:, 0:Kg], b_ref[0], nt, preferred_element_type=F32)
            for g in range(1, G):
                acc += lax.dot_general(a_ref[:, g * Kg:(g + 1) * Kg], b_ref[g], nt, preferred_element_type=F32)
        else:
            acc = lax.dot_general(a_ref[...], b_ref[...], nt, preferred_element_type=F32)
        if y_ref is not None:
            acc = acc * (2.0 * jnp.sqrt(y_ref[...].astype(F32)))
        out_ref[...] = acc.astype(out_ref.dtype)

    out_blk = pl.BlockSpec((tm, tn), lambda i, j: (i, j))
    in_specs = [pl.BlockSpec((tm, K), lambda i, j: (i, 0)), b_spec]
    args = [a, b]
    if relu2_out is not None:
        in_specs.append(out_blk)
        args.append(relu2_out)
    if dep is not None:
        in_specs.append(pl.BlockSpec(dep.shape, lambda i, j: (0, 0)))
        args.append(dep)
    return pl.pallas_call(
        body, name=name, grid=(M // tm, N // tn),
        in_specs=in_specs, out_specs=out_blk, out_shape=_sds((M, N), out_dtype),
        compiler_params=_cparams("parallel", "parallel"),
    )(*args)


def _wgrad_wide_a(a, b, name, tm=512):
    T, M = a.shape
    N = b.shape[1]
    tm = _fit(M, tm)

    def body(a_ref, b_ref, out_ref):
        out_ref[...] = lax.dot_general(a_ref[...], b_ref[...], (((0,), (0,)), ((), ())),
                                       preferred_element_type=F32).astype(out_ref.dtype)

    return pl.pallas_call(
        body, name=name, grid=(M // tm,),
        in_specs=[pl.BlockSpec((T, tm), lambda i: (0, i)),
                  pl.BlockSpec((T, N), lambda i: (0, 0), pipeline_mode=pl.Buffered(1))],
        out_specs=pl.BlockSpec((tm, N), lambda i: (i, 0)), out_shape=_sds((M, N), MXU_DTYPE),
        compiler_params=_cparams("parallel"),
    )(a, b)


def _wgrad_wide_b(a, b, name, groups=None, tn=512, t_chunk=512):
    T, M = a.shape
    N = b.shape[1]
    tn = _fit(N if groups is None else N // groups, tn)
    t_chunk = _fit(T, t_chunk)

    def body(a_ref, b_ref, out_ref, at_ref):
        @pl.when(pl.program_id(0) == 0)
        def _():
            for c in range(0, T, t_chunk):
                at_ref[:, c:c + t_chunk] = a_ref[c:c + t_chunk, :].T

        out_ref[...] = jnp.dot(at_ref[...], b_ref[...], preferred_element_type=F32).astype(out_ref.dtype)

    if groups is None:
        out_spec = pl.BlockSpec((M, tn), lambda j: (0, j))
        out_shape = _sds((M, N), MXU_DTYPE)
    else:
        per = N // groups // tn
        out_spec = pl.BlockSpec((None, M, tn), lambda j: (j // per, 0, j % per))
        out_shape = _sds((groups, M, N // groups), MXU_DTYPE)
    return pl.pallas_call(
        body, name=name, grid=(N // tn,),
        in_specs=[pl.BlockSpec((T, M), lambda j: (0, 0), pipeline_mode=pl.Buffered(1)),
                  pl.BlockSpec((T, tn), lambda j: (0, j))],
        out_specs=out_spec, out_shape=out_shape,
        scratch_shapes=[pltpu.VMEM((M, T), MXU_DTYPE)],
        compiler_params=_cparams("arbitrary"),
    )(a, b)


def _loss_and_grad(y, target, name, tr=512):
    T, D = y.shape

    def body(y_ref, t_ref, loss_ref, dx_ref, dxb_ref):
        @pl.when(pl.program_id(0) == 0)
        def _():
            loss_ref[...] = jnp.zeros_like(loss_ref)

        err = y_ref[...] - t_ref[...]
        loss_ref[...] += 0.5 * jnp.sum(jnp.mean(err * err, axis=-1, keepdims=True), axis=0, keepdims=True)
        dx = err * (1.0 / D)
        dx_ref[...] = dx
        dxb_ref[...] = dx.astype(dxb_ref.dtype)

    row = lambda i: (i, 0)
    return pl.pallas_call(
        body, name=name, grid=(T // tr,),
        in_specs=[pl.BlockSpec((tr, D), row), pl.BlockSpec((tr, D), row)],
        out_specs=[pl.BlockSpec((8, LANE), lambda i: (0, 0)), pl.BlockSpec((tr, D), row), pl.BlockSpec((tr, D), row)],
        out_shape=[_sds((8, LANE), F32), _sds((T, D), F32), _sds((T, D), MXU_DTYPE)],
        compiler_params=_cparams("arbitrary"),
    )(y, target)


SGU_ROWS = 512


def _sgu_mixed(v, w_ref, b_ref, n_heads):
    parts = [jnp.dot(w_ref[h], v[:, h * HEAD_DIM:(h + 1) * HEAD_DIM], preferred_element_type=F32) for h in range(n_heads)]
    return jnp.concatenate(parts, axis=1) + b_ref[...]


def _sgu_fwd(p, w_tril, bmat, name):
    T = p.shape[0]
    H = w_tril.shape[0]
    AW = H * HEAD_DIM
    rows = _fit(T, SGU_ROWS)

    def body(u_ref, v_ref, w_ref, b_ref, y_ref):
        for c in range(0, rows, CHUNK):
            ch = pl.ds(c, CHUNK)
            mixed = _sgu_mixed(v_ref[ch, :].astype(MXU_DTYPE), w_ref, b_ref, H)
            y_ref[ch, :] = (u_ref[ch, :].astype(F32) * mixed).astype(y_ref.dtype)

    const3 = lambda c: (0, 0, 0)
    return pl.pallas_call(
        body, name=name, grid=(T // rows,),
        in_specs=[pl.BlockSpec((rows, AW), lambda c: (c, 0)), pl.BlockSpec((rows, AW), lambda c: (c, 1)),
                  pl.BlockSpec((H, CHUNK, CHUNK), const3), pl.BlockSpec((CHUNK, AW), lambda c: (0, 0))],
        out_specs=pl.BlockSpec((rows, AW), lambda c: (c, 0)),
        out_shape=_sds((T, AW), MXU_DTYPE),
        compiler_params=_cparams("parallel"),
    )(p, p, w_tril, bmat)


def _sgu_bwd(dymix, p, w_tril, w_tril_t, bmat, name):
    T = p.shape[0]
    H = w_tril.shape[0]
    AW = H * HEAD_DIM
    rows = _fit(T, SGU_ROWS)

    def body(dy_ref, u_ref, v_ref, w_ref, wt_ref, b_ref, du_ref, dv_ref, dw_ref, db_ref):
        @pl.when(pl.program_id(0) == 0)
        def _():
            dw_ref[...] = jnp.zeros_like(dw_ref)
            db_ref[...] = jnp.zeros_like(db_ref)

        for c in range(0, rows, CHUNK):
            ch = pl.ds(c, CHUNK)
            v = v_ref[ch, :].astype(MXU_DTYPE)
            dy = dy_ref[ch, :].astype(F32)
            du_ref[ch, :] = (dy * _sgu_mixed(v, w_ref, b_ref, H)).astype(du_ref.dtype)
            dm = dy * u_ref[ch, :].astype(F32)
            db_ref[...] += dm
            dm_c = dm.astype(MXU_DTYPE)
            dv = []
            for h in range(H):
                sl = slice(h * HEAD_DIM, (h + 1) * HEAD_DIM)
                dv.append(jnp.dot(wt_ref[h], dm_c[:, sl], preferred_element_type=F32))
                dw_ref[h] += lax.dot_general(dm_c[:, sl], v[:, sl], (((1,), (1,)), ((), ())), preferred_element_type=F32)
            dv_ref[ch, :] = jnp.concatenate(dv, axis=1).astype(dv_ref.dtype)

    const3 = lambda c: (0, 0, 0)
    blk = pl.BlockSpec((rows, AW), lambda c: (c, 0))
    return pl.pallas_call(
        body, name=name, grid=(T // rows,),
        in_specs=[blk, blk, pl.BlockSpec((rows, AW), lambda c: (c, 1)),
                  pl.BlockSpec((H, CHUNK, CHUNK), const3), pl.BlockSpec((H, CHUNK, CHUNK), const3),
                  pl.BlockSpec((CHUNK, AW), lambda c: (0, 0))],
        out_specs=[blk, blk, pl.BlockSpec((H, CHUNK, CHUNK), const3), pl.BlockSpec((CHUNK, AW), lambda c: (0, 0))],
        out_shape=[_sds((T, AW), MXU_DTYPE), _sds((T, AW), MXU_DTYPE), _sds((H, CHUNK, CHUNK), F32), _sds((CHUNK, AW), F32)],
        compiler_params=_cparams("arbitrary"),
    )(dymix, p, p, w_tril, w_tril_t, bmat)


def _shift_down(z, s, row):
    return jnp.where(row >= s, pltpu.roll(z, s, 0), 0.0)


def _shift_up(z, s, row, T):
    return jnp.where(row < T - s, pltpu.roll(z, T - s, 0), 0.0)


def _conv_fwd(p, w_conv, AW, name):
    T = p.shape[0]
    BW = w_conv.shape[1]
    nb = BW // LANE
    b0 = 2 * AW // LANE

    def body(b_ref, c_ref, x_ref, w_ref, y_ref):
        row = lax.broadcasted_iota(jnp.int32, (T, LANE), 0)
        z = c_ref[...].astype(F32) * x_ref[...].astype(F32)
        w0, w1, w2 = w_ref[0:1, :], w_ref[1:2, :], w_ref[2:3, :]
        conv = w2 * z + w1 * _shift_down(z, 1, row) + w0 * _shift_down(z, 2, row)
        y_ref[...] = (b_ref[...].astype(F32) * conv).astype(y_ref.dtype)

    return pl.pallas_call(
        body, name=name, grid=(nb,),
        in_specs=[pl.BlockSpec((T, LANE), lambda j: (0, b0 + j)), pl.BlockSpec((T, LANE), lambda j: (0, b0 + nb + j)),
                  pl.BlockSpec((T, LANE), lambda j: (0, b0 + 2 * nb + j)), pl.BlockSpec((CONV_WIDTH, LANE), lambda j: (0, j))],
        out_specs=pl.BlockSpec((T, LANE), lambda j: (0, j)),
        out_shape=_sds((T, BW), MXU_DTYPE),
        compiler_params=_cparams("parallel"),
    )(p, p, p, w_conv)


def _conv_bwd(dymix, p, w_conv, AW, name):
    T = p.shape[0]
    BW = w_conv.shape[1]
    nb = BW // LANE
    b0 = 2 * AW // LANE
    y0 = AW // LANE

    def body(dy_ref, b_ref, c_ref, x_ref, w_ref, db_ref, dc_ref, dxb_ref, dw_ref):
        row = lax.broadcasted_iota(jnp.int32, (T, LANE), 0)
        cv, xv, dy = c_ref[...].astype(F32), x_ref[...].astype(F32), dy_ref[...].astype(F32)
        w0, w1, w2 = w_ref[0:1, :], w_ref[1:2, :], w_ref[2:3, :]
        z = cv * xv
        z1 = _shift_down(z, 1, row)
        z2 = _shift_down(z, 2, row)
        conv = w2 * z + w1 * z1 + w0 * z2
        db_ref[...] = (dy * conv).astype(db_ref.dtype)
        dconv = dy * b_ref[...].astype(F32)
        dz = w2 * dconv + w1 * _shift_up(dconv, 1, row, T) + w0 * _shift_up(dconv, 2, row, T)
        dc_ref[...] = (dz * xv).astype(dc_ref.dtype)
        dxb_ref[...] = (dz * cv).astype(dxb_ref.dtype)
        dw_ref[0:1, :] = jnp.sum(dconv * z2, axis=0, keepdims=True)
        dw_ref[1:2, :] = jnp.sum(dconv * z1, axis=0, keepdims=True)
        dw_ref[2:3, :] = jnp.sum(dconv * z, axis=0, keepdims=True)

    col = lambda j: (0, j)
    return pl.pallas_call(
        body, name=name, grid=(nb,),
        in_specs=[pl.BlockSpec((T, LANE), lambda j: (0, y0 + j)),
                  pl.BlockSpec((T, LANE), lambda j: (0, b0 + j)), pl.BlockSpec((T, LANE), lambda j: (0, b0 + nb + j)),
                  pl.BlockSpec((T, LANE), lambda j: (0, b0 + 2 * nb + j)), pl.BlockSpec((CONV_WIDTH, LANE), col)],
        out_specs=[pl.BlockSpec((T, LANE), col)] * 3 + [pl.BlockSpec((CONV_WIDTH, LANE), col)],
        out_shape=[_sds((T, BW), MXU_DTYPE)] * 3 + [_sds((CONV_WIDTH, BW), F32)],
        compiler_params=_cparams("parallel"),
    )(dymix, p, p, p, w_conv)


def _head_sum(x, col_head, n_heads):
    out = jnp.zeros_like(x)
    for h in range(n_heads):
        sel = col_head == h
        out = jnp.where(sel, jnp.sum(jnp.where(sel, x, 0.0), axis=-1, keepdims=True), out)
    return out


def _same_head(width):
    assert width == 2 * HEAD_DIM
    return lax.broadcasted_iota(jnp.int32, (1, width), 1) < HEAD_DIM


def _head_sum2(x, first):
    s0 = jnp.sum(jnp.where(first, x, 0.0), axis=-1, keepdims=True)
    s1 = jnp.sum(jnp.where(first, 0.0, x), axis=-1, keepdims=True)
    return jnp.where(first, s0, s1)


def _head_norm(x, g, first):
    r = lax.rsqrt(_head_sum2(x * x, first) * (1.0 / HEAD_DIM) + EPS)
    return x * r * g, r


def _head_norm_bwd(dy, x, g, r, first):
    gdy = dy * g
    mean_xg = _head_sum2(x * gdy, first) * (1.0 / HEAD_DIM)
    return r * gdy - x * (r * r * r) * mean_xg, dy * x * r


ATT_SPAN_MIN = 512
ATT_FWD_UNROLL = 4
ATT_BWD_UNROLL = 4
HEADS_PER_LANES = LANE // HEAD_DIM


def _attn_geometry(T, d):
    m = max(1, ATT_SPAN_MIN // (ATT_BLK * d))
    return m, ATT_BLK * d * m, ATT_BLK * d, T // (ATT_BLK * d)


def _rows(ref, start, d):
    return ref[pl.ds(start, ATT_BLK, stride=d), :] if d > 1 else ref[pl.ds(start, ATT_BLK), :]


def _set_rows(ref, start, d, value):
    if d > 1:
        ref[pl.ds(start, ATT_BLK, stride=d), :] = value
    else:
        ref[pl.ds(start, ATT_BLK), :] = value


def _for_each_block(d, m, task, unroll):
    for j in range(m):
        if d == 1:
            task(0, j)
        else:
            lax.fori_loop(0, d, lambda r, carry, j=j: (task(r, j), carry)[1], 0, unroll=min(unroll, d))


def _head_slices():
    return [slice(h * HEAD_DIM, (h + 1) * HEAD_DIM) for h in range(HEADS_PER_LANES)]


def _qk_norm_fwd(p, gains, q_start, PW, name, tr=512):
    T = p.shape[0]
    n_norm = gains.shape[1] // PW
    n = n_norm * 3 // 2
    c0 = q_start // PW

    def body(*refs):
        x_refs, g_ref, out_ref = refs[:n], refs[n], refs[n + 1]
        first = _same_head(LANE)
        for i in range(n):
            for c in range(0, PW, LANE):
                lo = i * PW + c
                x = x_refs[i][:, c:c + LANE].astype(F32)
                out_ref[:, lo:lo + LANE] = _head_norm(x, g_ref[:, lo:lo + LANE], first)[0] if i < n_norm else x

    return pl.pallas_call(
        body, name=name, grid=(T // tr,),
        in_specs=[pl.BlockSpec((tr, PW), lambda i, j=j: (i, c0 + j)) for j in range(n)]
        + [pl.BlockSpec((1, n_norm * PW), lambda i: (0, 0))],
        out_specs=pl.BlockSpec((tr, n * PW), lambda i: (i, 0)), out_shape=_sds((T, n * PW), F32),
        compiler_params=_cparams("parallel"),
    )(*([p] * n), gains)


def _qk_norm_bwd(dns, p, gains, q_start, PW, name, tr=512):
    T = p.shape[0]
    n = len(dns)
    c0 = q_start // PW

    def body(*refs):
        d_refs, x_refs, g_ref, out_ref, acc_ref = refs[:n], refs[n:2 * n], refs[2 * n], refs[2 * n + 1], refs[2 * n + 2]

        @pl.when(pl.program_id(0) == 0)
        def _():
            acc_ref[...] = jnp.zeros_like(acc_ref)

        first = _same_head(LANE)
        for i in range(n):
            for c in range(0, PW, LANE):
                lo = i * PW + c
                x, gv = x_refs[i][:, c:c + LANE].astype(F32), g_ref[:, lo:lo + LANE]
                _, r = _head_norm(x, gv, first)
                dx, g_part = _head_norm_bwd(d_refs[i][:, c:c + LANE], x, gv, r, first)
                out_ref[:, lo:lo + LANE] = dx.astype(out_ref.dtype)
                acc_ref[0:1, lo:lo + LANE] += jnp.sum(g_part, axis=0, keepdims=True)

    return pl.pallas_call(
        body, name=name, grid=(T // tr,),
        in_specs=[pl.BlockSpec((tr, PW), lambda i: (i, 0))] * n
        + [pl.BlockSpec((tr, PW), lambda i, j=j: (i, c0 + j)) for j in range(n)]
        + [pl.BlockSpec((1, n * PW), lambda i: (0, 0))],
        out_specs=[pl.BlockSpec((tr, n * PW), lambda i: (i, 0)), pl.BlockSpec((8, n * PW), lambda i: (0, 0))],
        out_shape=[_sds((T, n * PW), MXU_DTYPE), _sds((8, n * PW), F32)],
        compiler_params=_cparams("arbitrary"),
    )(*dns, *([p] * n), gains)


def _attn_fwd(qkv, g, d, PW, name):
    T = qkv.shape[0]
    B, W = ATT_BLK, LANE
    m, span, group, _ = _attn_geometry(T, d)
    c_q = g * PW // W
    c_k, c_v = c_q + 3 * PW // W, c_q + 6 * PW // W
    scale = HEAD_DIM ** -0.5

    def body(q_ref, k_ref, v_ref, kp_ref, vp_ref, o_ref, lse_ref):
        n = pl.program_id(1)
        qi = lax.broadcasted_iota(jnp.int32, (B, 2 * B), 0)
        kj = lax.broadcasted_iota(jnp.int32, (B, 2 * B), 1)
        band = (kj >= qi) & (kj <= qi + B)

        def task(r, j):
            cur = j * group + r
            if j == 0:
                kp, vp = _rows(kp_ref, r, d), _rows(vp_ref, r, d)
            else:
                kp, vp = _rows(k_ref, cur - group, d), _rows(v_ref, cur - group, d)
            mask = band & ((n * m + j > 0) | (kj >= B))
            qn = _rows(q_ref, cur, d).astype(MXU_DTYPE)
            kn = jnp.concatenate([kp, _rows(k_ref, cur, d)], axis=0).astype(MXU_DTYPE)
            vcat = jnp.concatenate([vp, _rows(v_ref, cur, d)], axis=0).astype(MXU_DTYPE)
            o_parts, lse_parts = [], []
            for sl in _head_slices():
                s = lax.dot_general(qn[:, sl], kn[:, sl], (((1,), (1,)), ((), ())), preferred_element_type=F32) * scale
                s = jnp.where(mask, s, NEG_INF)
                mx = jnp.max(s, axis=-1, keepdims=True)
                e = jnp.exp(s - mx)
                den = jnp.sum(e, axis=-1, keepdims=True)
                o_parts.append(jnp.dot(e.astype(MXU_DTYPE), vcat[:, sl], preferred_element_type=F32) / den)
                lse_parts.append(jnp.broadcast_to(mx + jnp.log(den), (B, HEAD_DIM)))
            _set_rows(o_ref, cur, d, jnp.concatenate(o_parts, axis=1))
            _set_rows(lse_ref, cur, d, jnp.concatenate(lse_parts, axis=1))

        _for_each_block(d, m, task, ATT_FWD_UNROLL)

    main = lambda c0: pl.BlockSpec((span, W), lambda hp, n: (n, c0 + hp))
    prev = lambda c0: pl.BlockSpec((group, W), lambda hp, n: (jnp.maximum(n * m - 1, 0), c0 + hp))
    out_blk = pl.BlockSpec((span, W), lambda hp, n: (n, hp))
    return pl.pallas_call(
        body, name=name, grid=(PW // W, T // span),
        in_specs=[main(c_q), main(c_k), main(c_v), prev(c_k), prev(c_v)],
        out_specs=[out_blk, out_blk],
        out_shape=[_sds((T, PW), F32), _sds((T, PW), F32)],
        compiler_params=_cparams("parallel", "parallel"),
    )(qkv, qkv, qkv, qkv, qkv)


def _attn_bwd(qkv, lse, do, corr, g, d, PW, name):
    T = qkv.shape[0]
    B, W = ATT_BLK, LANE
    m, span, group, n_blocks = _attn_geometry(T, d)
    c_q = g * PW // W
    c_k, c_v = c_q + 3 * PW // W, c_q + 6 * PW // W
    scale = HEAD_DIM ** -0.5
    nt = (((1,), (1,)), ((), ()))
    tn = (((0,), (0,)), ((), ()))

    def body(q_ref, k_ref, v_ref, do_ref, l_ref, c_ref, kp_ref, vp_ref, qx_ref, dox_ref, lx_ref, cx_ref,
             dq_ref, dk_ref, dv_ref):
        n = pl.program_id(1)
        i1 = lax.broadcasted_iota(jnp.int32, (B, B), 0)
        j1 = lax.broadcasted_iota(jnp.int32, (B, B), 1)
        i2 = lax.broadcasted_iota(jnp.int32, (2 * B, B), 0)
        j2 = lax.broadcasted_iota(jnp.int32, (2 * B, B), 1)

        def task(r, j):
            cur = j * group + r
            blk = n * m + j
            q_c, k_c, v_c = _rows(q_ref, cur, d), _rows(k_ref, cur, d), _rows(v_ref, cur, d)
            do_c, l_c, c_c = _rows(do_ref, cur, d), _rows(l_ref, cur, d), _rows(c_ref, cur, d)
            if j == 0:
                k_p, v_p = _rows(kp_ref, r, d), _rows(vp_ref, r, d)
            else:
                k_p, v_p = _rows(k_ref, cur - group, d), _rows(v_ref, cur - group, d)
            if j == m - 1:
                nxt = [_rows(ref, r, d) for ref in (qx_ref, dox_ref, lx_ref, cx_ref)]
            else:
                nxt = [_rows(ref, cur + group, d) for ref in (q_ref, do_ref, l_ref, c_ref)]
            q_x, do_x, l_x, c_x = nxt
            kn_c, kn_p, v_c, v_p = (a.astype(MXU_DTYPE) for a in (k_c, k_p, v_c, v_p))
            qn_c = q_c.astype(MXU_DTYPE)
            qn_cat = jnp.concatenate([qn_c, q_x.astype(MXU_DTYPE)], axis=0)
            do_cb = do_c.astype(MXU_DTYPE)
            do_cat = jnp.concatenate([do_cb, do_x.astype(MXU_DTYPE)], axis=0)
            l_cat = jnp.concatenate([l_c, l_x], axis=0)
            c_cat = jnp.concatenate([c_c, c_x], axis=0)
            mask_p = (j1 >= i1) & (blk > 0)
            mask_c = ((i2 < B) & (j2 <= i2)) | ((i2 >= B) & (j2 >= i2 - B) & (blk + 1 < n_blocks))
            dqn, dkn, dv = [], [], []
            for h, sl in enumerate(_head_slices()):
                lane = slice(h * HEAD_DIM, h * HEAD_DIM + 1)
                s_p = lax.dot_general(qn_c[:, sl], kn_p[:, sl], nt, preferred_element_type=F32) * scale
                pr_p = jnp.where(mask_p, jnp.exp(s_p - l_c[:, lane]), 0.0)
                dp_p = lax.dot_general(do_cb[:, sl], v_p[:, sl], nt, preferred_element_type=F32)
                ds_p = (pr_p * (dp_p + c_c[:, lane]) * scale).astype(MXU_DTYPE)
                s_c = lax.dot_general(qn_cat[:, sl], kn_c[:, sl], nt, preferred_element_type=F32) * scale
                pr_c = jnp.where(mask_c, jnp.exp(s_c - l_cat[:, lane]), 0.0)
                dp_c = lax.dot_general(do_cat[:, sl], v_c[:, sl], nt, preferred_element_type=F32)
                ds_c = (pr_c * (dp_c + c_cat[:, lane]) * scale).astype(MXU_DTYPE)
                dqn.append(jnp.dot(ds_p, kn_p[:, sl], preferred_element_type=F32)
                           + jnp.dot(ds_c[:B], kn_c[:, sl], preferred_element_type=F32))
                dkn.append(lax.dot_general(ds_c, qn_cat[:, sl], tn, preferred_element_type=F32))
                dv.append(lax.dot_general(pr_c.astype(MXU_DTYPE), do_cat[:, sl], tn, preferred_element_type=F32))
            _set_rows(dq_ref, cur, d, jnp.concatenate(dqn, axis=1))
            _set_rows(dk_ref, cur, d, jnp.concatenate(dkn, axis=1))
            _set_rows(dv_ref, cur, d, jnp.concatenate(dv, axis=1))

        _for_each_block(d, m, task, ATT_BWD_UNROLL)

    main = lambda c0: pl.BlockSpec((span, W), lambda hp, n: (n, c0 + hp))
    prev = lambda c0: pl.BlockSpec((group, W), lambda hp, n: (jnp.maximum(n * m - 1, 0), c0 + hp))
    nxt = lambda c0: pl.BlockSpec((group, W), lambda hp, n: (jnp.minimum((n + 1) * m, n_blocks - 1), c0 + hp))
    own = pl.BlockSpec((span, W), lambda hp, n: (n, hp))
    return pl.pallas_call(
        body, name=name, grid=(PW // W, T // span),
        in_specs=[main(c_q), main(c_k), main(c_v), main(0), main(0), main(0), prev(c_k), prev(c_v),
                  nxt(c_q), nxt(0), nxt(0), nxt(0)],
        out_specs=[own, own, own],
        out_shape=[_sds((T, PW), F32)] * 3,
        compiler_params=_cparams("parallel", "parallel"),
    )(qkv, qkv, qkv, do, lse, corr, qkv, qkv, qkv, do, lse, corr)


def _softmax3(lses):
    mx = jnp.maximum(jnp.maximum(lses[0], lses[1]), lses[2])
    ex = [jnp.exp(l - mx) for l in lses]
    inv = 1.0 / (ex[0] + ex[1] + ex[2])
    return [e * inv for e in ex]


def _mix_fwd(os_, lses, name, tr=512):
    T, PW = os_[0].shape

    def body(o0, o1, o2, l0, l1, l2, y_ref):
        alpha = _softmax3([l0[...], l1[...], l2[...]])
        for g, o_ref in enumerate((o0, o1, o2)):
            y_ref[:, g * PW:(g + 1) * PW] = (o_ref[...] * alpha[g]).astype(y_ref.dtype)

    blk = pl.BlockSpec((tr, PW), lambda i: (i, 0))
    return pl.pallas_call(
        body, name=name, grid=(T // tr,),
        in_specs=[blk] * 6, out_specs=pl.BlockSpec((tr, 3 * PW), lambda i: (i, 0)),
        out_shape=_sds((T, 3 * PW), MXU_DTYPE),
        compiler_params=_cparams("parallel"),
    )(*os_, *lses)


def _mix_bwd(dymix, os_, lses, c_start, name, tr=512):
    T, PW = os_[0].shape
    HP = PW // HEAD_DIM
    c0 = c_start // PW

    def body(d0, d1, d2, o0, o1, o2, l0, l1, l2, do0, do1, do2, dl0, dl1, dl2):
        col_head = lax.broadcasted_iota(jnp.int32, (tr, PW), 1) // HEAD_DIM
        alpha = _softmax3([l0[...], l1[...], l2[...]])
        dys = [d0[...].astype(F32), d1[...].astype(F32), d2[...].astype(F32)]
        dots = [_head_sum(dy * o_ref[...], col_head, HP) for dy, o_ref in zip(dys, (o0, o1, o2))]
        mean_dot = alpha[0] * dots[0] + alpha[1] * dots[1] + alpha[2] * dots[2]
        for g, (do_ref, dl_ref) in enumerate(((do0, dl0), (do1, dl1), (do2, dl2))):
            do_ref[...] = dys[g] * alpha[g]
            dl_ref[...] = -alpha[g] * mean_dot

    blk = pl.BlockSpec((tr, PW), lambda i: (i, 0))
    dy_specs = [pl.BlockSpec((tr, PW), lambda i, g=g: (i, c0 + g)) for g in range(3)]
    outs = pl.pallas_call(
        body, name=name, grid=(T // tr,),
        in_specs=dy_specs + [blk] * 6, out_specs=[blk] * 6,
        out_shape=[_sds((T, PW), F32)] * 6,
        compiler_params=_cparams("parallel"),
    )(dymix, dymix, dymix, *os_, *lses)
    return outs[:3], outs[3:]


def _adamw_math(w, g, m, v):
    m2 = ADAM_B1 * m + (1.0 - ADAM_B1) * g
    v2 = ADAM_B2 * v + (1.0 - ADAM_B2) * (g * g)
    m_hat = m2 / (1.0 - ADAM_B1 ** ADAM_STEP)
    v_hat = v2 / (1.0 - ADAM_B2 ** ADAM_STEP)
    delta = -ADAM_LR * (m_hat / (jnp.sqrt(v_hat) + ADAM_EPS) + ADAM_WD * w)
    return delta, m2, v2


def _adamw_layer(layer, w, m, v, own, landed, me, prev, name, tr=256):
    _, R, C = w.shape
    tr = next(t for t in range(min(tr, R) // 16 * 16, 0, -16) if R % t == 0)

    def body(me_ref, w_ref, m_ref, v_ref, own_ref, land_ref, *rest):
        g_ref, d_ref, m2_ref, v2_ref = rest[-4:]
        g = own_ref[...].astype(F32)
        for j in range(N_PEER):
            g = g + land_ref[j].astype(F32)
        delta, m2, v2 = _adamw_math(w_ref[...], g, m_ref[...], v_ref[...])
        g_ref[...] = g
        d_ref[...] = delta
        m2_ref[...] = m2
        v2_ref[...] = v2

    lay = pl.BlockSpec((None, tr, C), lambda i, me_ref: (layer, i, 0))
    in_specs = [lay, lay, lay, pl.BlockSpec((None, tr, C), lambda i, me_ref: (me_ref[0], i, 0)),
                pl.BlockSpec((N_PEER, tr, C), lambda i, me_ref: (0, i, 0))]
    args = [me, w, m, v, own, landed]
    aliases = {}
    if prev is not None:
        in_specs += [pl.BlockSpec(memory_space=pl.ANY)] * 4
        args += list(prev)
        aliases = {6 + i: i for i in range(4)}
    return pl.pallas_call(
        body, name=name,
        grid_spec=pltpu.PrefetchScalarGridSpec(num_scalar_prefetch=1, grid=(R // tr,), in_specs=in_specs, out_specs=[lay] * 4),
        out_shape=[_sds(w.shape, F32)] * 4,
        input_output_aliases=aliases,
        compiler_params=_cparams("parallel"),
    )(*args)


def _sum_parts(parts, name):
    _, R, C = parts.shape

    def body(p_ref, out_ref):
        g = p_ref[0]
        for j in range(1, N_DEV):
            g = g + p_ref[j]
        out_ref[...] = g

    return pl.pallas_call(
        body, name=name, grid=(1,),
        in_specs=[pl.BlockSpec((N_DEV, R, C), lambda i: (0, 0, 0))], out_specs=pl.BlockSpec((R, C), lambda i: (0, 0)),
        out_shape=_sds((R, C), F32), compiler_params=_cparams("arbitrary"),
    )(parts)


def _adamw_small(ws, gs, ms, vs, name):
    n = len(ws)

    def body(*refs):
        ins, outs = refs[:4 * n], refs[4 * n:]
        for i in range(n):
            delta, m2, v2 = _adamw_math(ins[i][...], ins[n + i][...], ins[2 * n + i][...], ins[3 * n + i][...])
            outs[i][...] = delta
            outs[n + i][...] = m2
            outs[2 * n + i][...] = v2

    outs = pl.pallas_call(
        body, name=name, out_shape=[_sds(w.shape, F32) for w in ws] * 3,
    )(*ws, *gs, *ms, *vs)
    return outs[:n], outs[n:2 * n], outs[2 * n:]


def _pack(arrays, rows_multiple=8):
    flat = []
    for a in arrays:
        a = a.reshape(-1).astype(F32)
        flat.append(jnp.pad(a, (0, (-a.shape[0]) % LANE)))
    flat = jnp.concatenate(flat)
    flat = jnp.pad(flat, (0, (-flat.shape[0]) % (LANE * rows_multiple)))
    return flat.reshape(-1, LANE)


def _unpack(packed, shapes):
    flat = packed.reshape(-1)
    out, off = [], 0
    for s in shapes:
        size = 1
        for dim in s:
            size *= dim
        out.append(flat[off:off + size].reshape(s))
        off += size + (-size) % LANE
    return out


def _layer_fwd(x, wts, getw, dims, dep=None):
    AW, BW, PW, DP = dims["AW"], dims["BW"], dims["PW"], dims["DP"]
    q_start = 2 * AW + 3 * BW
    h, r1 = _rmsnorm_fwd(x, wts["attn_norm"], "rmsnorm_fwd", dep=dep)
    p = _mm_nt(h, getw("w_in", h), "proj_in", out_dtype=MXU_DTYPE, tm=1024, tn=1408)
    y_a = _sgu_fwd(p, wts["sgu_tril"], wts["sgu_bmat"], "sgu_fwd")
    y_b = _conv_fwd(p, getw("conv_w", y_a), AW, "conv_fwd")
    qkn = _qk_norm_fwd(p, wts["qk_gain"], q_start, PW, "qk_norm_fwd")
    os_, lses = [], []
    for g, d in enumerate(DILATIONS):
        o, lse = _attn_fwd(qkn, g, d, PW, "attn_fwd_%d" % d)
        os_.append(o)
        lses.append(lse)
    y_c = _mix_fwd(os_, lses, "mix_fwd")
    ymix = jnp.concatenate([y_a, y_b, y_c], axis=1)
    x1 = _mm_nn(ymix, getw("w_out", ymix), "proj_out", residual=x, tm=1024, tn=1024)
    h2, r2 = _rmsnorm_fwd(x1, wts["mlp_norm"], "rmsnorm_fwd")
    hid = _mm_nn(h2, getw("w_mlp_in", h2), "mlp_in", out_dtype=MXU_DTYPE, relu2=True, tm=1024, tn=1024)
    x2 = _mm_nn(hid, getw("w_mlp_out", hid), "mlp_out", residual=x1, columns_first=True, tm=512, tn=1024)
    saved = dict(x=x, h=h, r1=r1, p=p, qkn=qkn, os=os_, lses=lses, ymix=ymix, x1=x1, h2=h2, r2=r2, hid=hid)
    return x2, saved


def _layer_bwd(dx, dxb, wts, getw, scatter, saved, dims):
    AW, BW, PW, DP = dims["AW"], dims["BW"], dims["PW"], dims["DP"]
    q_start = 2 * AW + 3 * BW
    D = dx.shape[1]
    g_w2 = _wgrad_wide_a(saved["hid"], dxb, "mlp_out_wgrad")
    token = scatter("w_mlp_out", g_w2.reshape(N_DEV, -1, D))
    da = _mm_nt(dxb, getw("w_mlp_out", None), "mlp_out_dgrad", out_dtype=MXU_DTYPE, relu2_out=saved["hid"], dep=token,
                tm=1024, tn=1024)
    g_w1 = _wgrad_wide_b(saved["h2"], da, "mlp_in_wgrad", groups=N_DEV)
    token = scatter("w_mlp_in", g_w1)
    dh2 = _mm_nt(da, getw("w_mlp_in", None), "mlp_in_dgrad", out_dtype=MXU_DTYPE, dep=token, tm=1024, tn=512)
    dx1, dx1b, g_mlp_norm = _rmsnorm_bwd(dh2, saved["x1"], wts["mlp_norm"], saved["r2"], dx, "rmsnorm_bwd")
    g_wout = _wgrad_wide_b(saved["ymix"], dx1b, "proj_out_wgrad")
    token = scatter("w_out", g_wout.reshape(N_DEV, -1, D))
    dymix = _mm_nt(dx1b, getw("w_out", None), "proj_out_dgrad", out_dtype=MXU_DTYPE, dep=token, tm=1024, tn=1024)
    p = saved["p"]
    du, dv, g_sgu_w, g_sgu_bmat = _sgu_bwd(dymix, p, wts["sgu_tril"], wts["sgu_tril_t"], wts["sgu_bmat"], "sgu_bwd")
    d_b, d_c, d_xb, g_conv = _conv_bwd(dymix, p, getw("conv_w", None), AW, "conv_bwd")
    dos, corrs = _mix_bwd(dymix, saved["os"], saved["lses"], AW + BW, "mix_bwd")
    dqns, dkns, dvs = [], [], []
    for g, d in enumerate(DILATIONS):
        dqn, dkn, dvv = _attn_bwd(saved["qkn"], saved["lses"][g], dos[g], corrs[g], g, d, PW, "attn_bwd_%d" % d)
        dqns.append(dqn)
        dkns.append(dkn)
        dvs.append(dvv.astype(MXU_DTYPE))
    dqk, g_qk = _qk_norm_bwd(dqns + dkns, p, wts["qk_gain"], q_start, PW, "qk_norm_bwd")
    g_q, g_k = (part.reshape(-1, HEAD_DIM).sum(0) for part in jnp.split(g_qk[0], 2))
    dp = jnp.concatenate([du, dv, d_b, d_c, d_xb, dqk] + dvs, axis=1)
    g_win_t = _wgrad_wide_a(dp, saved["h"], "proj_in_wgrad")
    token = scatter("w_in", g_win_t.reshape(N_DEV, DP // N_DEV, D))
    dh = _mm_nn(dp, getw("w_in", None), "proj_in_dgrad", out_dtype=MXU_DTYPE, dep=token, tm=1024, tn=1024)
    dx0, dx0b, g_attn_norm = _rmsnorm_bwd(dh, saved["x"], wts["attn_norm"], saved["r1"], dx1, "rmsnorm_bwd")
    H = AW // HEAD_DIM
    tril = jnp.tril(jnp.ones((CHUNK, CHUNK), F32))
    small = [g_attn_norm.reshape(-1), g_sgu_w * tril, g_sgu_bmat.reshape(CHUNK, H, HEAD_DIM).sum(-1).T,
             g_conv, g_q, g_k, g_mlp_norm.reshape(-1)]
    return dx0, dx0b, small


def kernel(x, attn_norm, w_in, sgu_w, sgu_b, conv_w, q_norm, k_norm, w_out, mlp_norm, w_mlp_in, w_mlp_out, loss_target, m_attn_norm, m_w_in, m_sgu_w, m_sgu_b, m_conv_w, m_q_norm, m_k_norm, m_w_out, m_mlp_norm, m_w_mlp_in, m_w_mlp_out, v_attn_norm, v_w_in, v_sgu_w, v_sgu_b, v_conv_w, v_q_norm, v_k_norm, v_w_out, v_mlp_norm, v_w_mlp_in, v_w_mlp_out):
    n_layers = attn_norm.shape[0]
    T, D = x.shape[1], x.shape[2]
    H = sgu_w.shape[1]
    AW = H * HEAD_DIM
    BW = conv_w.shape[2] * N_DEV
    DP = w_in.shape[2] * N_DEV
    DMIX = w_out.shape[1] * N_DEV
    DFF = w_mlp_in.shape[2] * N_DEV
    PW = (DMIX - AW - BW) // 3
    HP = PW // HEAD_DIM
    dims = dict(AW=AW, BW=BW, PW=PW, DP=DP)
    me = 4 * lax.axis_index("x") + 2 * lax.axis_index("y") + lax.axis_index("c")

    big_names = ("w_in", "w_out", "w_mlp_in", "w_mlp_out")
    tr_in = lambda a: jnp.swapaxes(a, 1, 2)
    big_w = dict(zip(big_names, (tr_in(w_in), w_out, w_mlp_in, w_mlp_out)))
    big_m = dict(zip(big_names, (tr_in(m_w_in), m_w_out, m_w_mlp_in, m_w_mlp_out)))
    big_v = dict(zip(big_names, (tr_in(v_w_in), v_w_out, v_w_mlp_in, v_w_mlp_out)))

    keys = []
    for l in range(n_layers):
        keys += [(l, nm) for nm in big_names]
    keys.insert(1, (0, "conv_w"))
    srcs = [_pack([conv_w]) if nm == "conv_w" else big_w[nm][l].astype(MXU_DTYPE) for l, nm in keys]
    flights, gather_token = _exchange_start(srcs, [_own_in_place(s, me) for s in srcs], "gather", name="gather_start")
    arriving = dict(zip(keys, flights))
    forwarding = {}
    relayout = dict(
        w_in=lambda g: g.reshape(DP, D), w_out=lambda g: g.reshape(DMIX, D),
        w_mlp_in=lambda g: g, w_mlp_out=lambda g: g.reshape(DFF, D),
        conv_w=lambda g: jnp.stack([_unpack(g[j], [conv_w.shape])[0] for j in range(N_DEV)], axis=2).reshape(
            n_layers, CONV_WIDTH, BW))
    gathered = {}

    def forward(key, after):
        _, land = _exchange_wait(arriving[key], after, "gather", name="gather_arrive_%d_%s" % key)
        fl, token = _exchange_start(None, [land], "forward", name="gather_forward_%d_%s" % key)
        forwarding[key] = fl[0]
        return token

    def weight_getter(l):
        def getw(nm, after):
            key = (0, nm) if nm == "conv_w" else (l, nm)
            if key not in gathered:
                ahead = keys[keys.index(key):][:2]
                for k in ahead:
                    if k not in forwarding:
                        after = forward(k, after)
                _, land = _exchange_wait(forwarding[key], after, "forward", name="gather_wait_%d_%s" % key)
                gathered[key] = relayout[nm](land)
            return gathered[key][l] if nm == "conv_w" else gathered[key]
        return getw

    tril = jnp.tril(jnp.ones((CHUNK, CHUNK), F32))
    layers = []
    for l in range(n_layers):
        w_tril = sgu_w[l] * tril
        layers.append(dict(
            attn_norm=attn_norm[l][None], mlp_norm=mlp_norm[l][None],
            sgu_tril=w_tril.astype(MXU_DTYPE), sgu_tril_t=w_tril.transpose(0, 2, 1).astype(MXU_DTYPE),
            sgu_bmat=jnp.repeat(sgu_b[l].T, HEAD_DIM, axis=1),
            qk_gain=jnp.concatenate([jnp.tile(q_norm[l], 3 * HP), jnp.tile(k_norm[l], 3 * HP)])[None]))

    xs = x[0]
    saved = []
    for l in range(n_layers):
        xs, sv = _layer_fwd(xs, layers[l], weight_getter(l), dims, dep=gather_token if l == 0 else None)
        saved.append(sv)
    loss_blk, dx, dxb = _loss_and_grad(xs, loss_target[0], "loss")
    loss = lax.psum(loss_blk[0, 0], ("x", "y", "c"))

    scattering = {}

    def scatter_starter(l):
        def scatter(nm, partials):
            land = lax.empty((N_PEER,) + partials.shape[1:], partials.dtype)
            fl, tok = _exchange_start([partials], [land], "scatter", name="scatter_start_%d_%s" % (l, nm))
            scattering[(l, nm)] = fl[0]
            return tok
        return scatter

    small = [None] * n_layers
    for l in reversed(range(n_layers)):
        dx, dxb, small[l] = _layer_bwd(dx, dxb, layers[l], weight_getter(l), scatter_starter(l), saved[l], dims)

    small_shapes = [s.shape for s in small[0]]
    small_src = [_pack([s for l in range(n_layers) for s in small[l]])]
    small_flights, small_token = _exchange_start(small_src, [_own_in_place(s, me) for s in small_src], "gather_all",
                                                 name="small_start")
    grad_x = dx[None]

    me1 = me.astype(jnp.int32).reshape(1)
    res = {nm: None for nm in big_names}
    after = small_token
    for l in reversed(range(n_layers)):
        for nm in reversed(big_names):
            own, landed = _exchange_wait(scattering[(l, nm)], after, "scatter", name="scatter_wait_%d_%s" % (l, nm))
            res[nm] = _adamw_layer(l, big_w[nm], big_m[nm], big_v[nm], own, landed, me1, res[nm], "adamw_" + nm)
            after = res[nm][0]
    res["w_in"] = [tr_in(a) for a in res["w_in"]]
    big_out = [res[nm] for nm in big_names]

    _, gathered_small = _exchange_wait(small_flights[0], after, "gather_all", name="small_wait")
    summed = _unpack(_sum_parts(gathered_small, "sum_small"), small_shapes * n_layers)
    ns = len(small_shapes)
    g_small = [jnp.stack([summed[l * ns + i] for l in range(n_layers)]) for i in range(ns)]
    g_attn_norm, g_sgu_w, g_sgu_b, g_conv_full, g_q, g_k, g_mlp_norm = g_small
    cs = conv_w.shape[2]
    g_conv = lax.dynamic_slice_in_dim(g_conv_full, me * cs, cs, axis=2)
    sm_w = (attn_norm, sgu_w, sgu_b, conv_w, q_norm, k_norm, mlp_norm)
    sm_m = (m_attn_norm, m_sgu_w, m_sgu_b, m_conv_w, m_q_norm, m_k_norm, m_mlp_norm)
    sm_v = (v_attn_norm, v_sgu_w, v_sgu_b, v_conv_w, v_q_norm, v_k_norm, v_mlp_norm)
    sm_g = (g_attn_norm, g_sgu_w, g_sgu_b, g_conv, g_q, g_k, g_mlp_norm)
    sm_delta, sm_m2, sm_v2 = _adamw_small(sm_w, sm_g, sm_m, sm_v, "adamw_small")

    def ordered(small_list, big_kind):
        b = [big_out[i][big_kind] for i in range(4)]
        return [small_list[0], b[0], small_list[1], small_list[2], small_list[3], small_list[4], small_list[5],
                b[1], small_list[6], b[2], b[3]]

    return (loss, grad_x, *ordered(list(sm_g), 0), *ordered(sm_delta, 1), *ordered(sm_m2, 2), *ordered(sm_v2, 3))
```

```python
import jax
import jax.numpy as jnp
from jax import lax
from jax.experimental import pallas as pl
from jax.experimental.pallas import tpu as pltpu

N_DEV = 8
HEAD_DIM = 64
CHUNK = 128
ATT_BLK = 128
DILATIONS = (1, 4, 16)
CONV_WIDTH = 3
EPS = 1e-6
ADAM_LR = 0.001
ADAM_B1 = 0.9
ADAM_B2 = 0.999
ADAM_EPS = 1e-08
ADAM_WD = 0.01
ADAM_STEP = 10
MXU_DTYPE = jnp.bfloat16
F32 = jnp.float32
LANE = 128
VMEM_LIMIT_BYTES = 56 * 1024 * 1024
NEG_INF = float("-inf")


def _cparams(*sem):
    return pltpu.CompilerParams(dimension_semantics=sem, vmem_limit_bytes=VMEM_LIMIT_BYTES)


def _sds(shape, dtype):
    return jax.ShapeDtypeStruct(shape, dtype)


def _fit(n, tile):
    for t in range(min(tile, n) // LANE * LANE, 0, -LANE):
        if n % t == 0:
            return t
    return n


_HBM = pl.BlockSpec(memory_space=pltpu.HBM)
_SEM = pl.BlockSpec(memory_space=pltpu.SEMAPHORE)
_DATAFLOW = pltpu.SideEffectType.DATAFLOW_SIDE_EFFECTING
N_PEER = N_DEV - 1


def _mesh_pos():
    x, y, c = lax.axis_index("x"), lax.axis_index("y"), lax.axis_index("c")
    return x, y, c, 4 * x + 2 * y + c


OTHER_CHIPS = (4, 2, 6)
EXCHANGE_PEERS = dict(
    scatter=tuple(range(1, N_DEV)),
    gather_all=tuple(range(1, N_DEV)),
    gather=(1,) + OTHER_CHIPS,
    forward=OTHER_CHIPS)


def _remote_copies(src, land, send_sems, recv_sems, mode):
    x, y, c, me = _mesh_pos()
    copies = []
    for i, k in enumerate(EXCHANGE_PEERS[mode]):
        px = (1 - x) if (k & 4) else x
        py = (1 - y) if (k & 2) else y
        pc = (1 - c) if (k & 1) else c
        if mode == "scatter":
            src_ref, dst_ref, dev = src.at[4 * px + 2 * py + pc], land.at[i], (px, py, pc)
        elif mode == "forward":
            slot = 4 * px + 2 * py + c
            src_ref, dst_ref, dev = land.at[slot], land.at[slot], (x, y, 1 - c)
        else:
            src_ref, dst_ref, dev = src, land.at[me], (px, py, pc)
        copies.append(pltpu.make_async_remote_copy(
            src_ref=src_ref, dst_ref=dst_ref, send_sem=send_sems.at[i], recv_sem=recv_sems.at[i],
            device_id=dev, device_id_type=pl.DeviceIdType.MESH))
    return copies


def _own_in_place(src, me):
    land = lax.empty((N_DEV,) + src.shape, src.dtype)
    return lax.dynamic_update_slice(land, src[None], (me,) + (0,) * src.ndim)


def _exchange_start(srcs, lands, mode, name):
    n = len(lands)
    has_src = srcs is not None
    arrays = (list(srcs) if has_src else []) + list(lands)
    n_arr = len(arrays)
    n_copies = len(EXCHANGE_PEERS[mode])

    def body(*refs):
        src = refs[:n] if has_src else [None] * n
        land = refs[n_arr - n:n_arr]
        send, recv = refs[n_arr:n_arr + n], refs[n_arr + n:n_arr + 2 * n]
        token = refs[2 * n_arr + 2 * n]
        for t in range(n):
            for cp in _remote_copies(src[t], land[t], send[t], recv[t], mode):
                cp.start()
        token[...] = jnp.zeros_like(token)

    outs = pl.pallas_call(
        body, name=name,
        out_shape=([pltpu.SemaphoreType.DMA((n_copies,))] * (2 * n) + [pltpu.HBM(a.shape, a.dtype) for a in arrays]
                   + [_sds((8, LANE), F32)]),
        in_specs=[_HBM] * n_arr,
        out_specs=[_SEM] * (2 * n) + [_HBM] * n_arr + [pl.BlockSpec(memory_space=pltpu.VMEM)],
        input_output_aliases={i: 2 * n + i for i in range(n_arr)},
        compiler_params=pltpu.CompilerParams(has_side_effects=_DATAFLOW),
    )(*[pltpu.with_memory_space_constraint(a, pltpu.HBM) for a in arrays])
    thru = outs[2 * n:2 * n + n_arr]
    flights = [(outs[t], outs[n + t], thru[t] if has_src else None, thru[n_arr - n + t]) for t in range(n)]
    return flights, outs[2 * n + n_arr]


def _exchange_wait(flight, after, mode, name):
    send, recv, src, land = flight
    arrays = [land] if src is None else [src, land]
    n_arr = len(arrays)

    def body(*refs):
        src_ref = refs[0] if n_arr == 2 else None
        land_ref, send_ref, recv_ref = refs[n_arr - 1], refs[n_arr], refs[n_arr + 1]
        for cp in _remote_copies(src_ref, land_ref, send_ref, recv_ref, mode):
            cp.wait_send()
            cp.wait_recv()

    outs = pl.pallas_call(
        body, name=name, out_shape=[pltpu.HBM(a.shape, a.dtype) for a in arrays],
        in_specs=[_HBM] * n_arr + [_SEM, _SEM, pl.BlockSpec(memory_space=pl.ANY)], out_specs=[_HBM] * n_arr,
        input_output_aliases={i: i for i in range(n_arr)},
        compiler_params=pltpu.CompilerParams(has_side_effects=_DATAFLOW),
    )(*arrays, send, recv, after)
    return (None, outs[0]) if src is None else (outs[0], outs[1])


def _rmsnorm_fwd(x, g, name, dep=None, tr=512):
    T, D = x.shape

    def body(x_ref, g_ref, *rest):
        h_ref, r_ref = rest[-2:]
        xv = x_ref[...]
        r = lax.rsqrt(jnp.mean(xv * xv, axis=-1, keepdims=True) + EPS)
        h_ref[...] = (xv * r * g_ref[...]).astype(h_ref.dtype)
        r_ref[...] = r

    in_specs = [pl.BlockSpec((tr, D), lambda i: (i, 0)), pl.BlockSpec((1, D), lambda i: (0, 0))]
    args = [x, g]
    if dep is not None:
        in_specs.append(pl.BlockSpec(dep.shape, lambda i: (0, 0)))
        args.append(dep)
    return pl.pallas_call(
        body, name=name, grid=(T // tr,),
        in_specs=in_specs,
        out_specs=[pl.BlockSpec((tr, D), lambda i: (i, 0)), pl.BlockSpec((tr, 1), lambda i: (i, 0))],
        out_shape=[_sds((T, D), MXU_DTYPE), _sds((T, 1), F32)],
        compiler_params=_cparams("parallel"),
    )(*args)


def _rmsnorm_bwd(dh, x, g, r, dres, name, tr=256):
    T, D = x.shape

    def body(dh_ref, x_ref, g_ref, r_ref, dres_ref, dx_ref, dxb_ref, dg_ref):
        @pl.when(pl.program_id(0) == 0)
        def _():
            dg_ref[...] = jnp.zeros_like(dg_ref)

        dh_v, xv, rv = dh_ref[...].astype(F32), x_ref[...], r_ref[...]
        gdy = dh_v * g_ref[...]
        mean_xg = jnp.mean(xv * gdy, axis=-1, keepdims=True)
        dx = dres_ref[...] + rv * gdy - xv * (rv * rv * rv) * mean_xg
        dx_ref[...] = dx
        dxb_ref[...] = dx.astype(dxb_ref.dtype)
        dg_ref[...] += jnp.sum(dh_v * xv * rv, axis=0, keepdims=True)

    row = lambda i: (i, 0)
    return pl.pallas_call(
        body, name=name, grid=(T // tr,),
        in_specs=[pl.BlockSpec((tr, D), row), pl.BlockSpec((tr, D), row), pl.BlockSpec((1, D), lambda i: (0, 0)),
                  pl.BlockSpec((tr, 1), row), pl.BlockSpec((tr, D), row)],
        out_specs=[pl.BlockSpec((tr, D), row), pl.BlockSpec((tr, D), row), pl.BlockSpec((1, D), lambda i: (0, 0))],
        out_shape=[_sds((T, D), F32), _sds((T, D), MXU_DTYPE), _sds((1, D), F32)],
        compiler_params=_cparams("arbitrary"),
    )(dh, x, g, r, dres)


def _mm_nn(a, b, name, out_dtype=F32, residual=None, relu2=False, dep=None, columns_first=False, tm=512, tn=512):
    M, K = a.shape
    grouped = b.ndim == 3
    N = b.shape[0] * b.shape[2] if grouped else b.shape[1]
    tm, tn = _fit(M, tm), _fit(b.shape[2] if grouped else N, tn)
    tile = (lambda j, i: (i, j)) if columns_first else (lambda i, j: (i, j))
    b_mode = pl.Buffered(1 if columns_first else 2)
    if grouped:
        per = b.shape[2] // tn
        b_spec = pl.BlockSpec((None, K, tn), lambda *g: (tile(*g)[1] // per, 0, tile(*g)[1] % per), pipeline_mode=b_mode)
    else:
        b_spec = pl.BlockSpec((K, tn), lambda *g: (0, tile(*g)[1]), pipeline_mode=b_mode)
    n_out = 2 if relu2 else 1

    def body(*refs):
        a_ref, b_ref = refs[0], refs[1]
        r_ref = refs[2] if residual is not None else None
        outs = refs[2 + (residual is not None) + (dep is not None):]
        acc = jnp.dot(a_ref[...], b_ref[...], preferred_element_type=F32)
        if r_ref is not None:
            acc = acc + r_ref[...]
        outs[0][...] = acc.astype(outs[0].dtype)
        if relu2:
            rl = jnp.maximum(acc, 0.0)
            outs[1][...] = (rl * rl).astype(outs[1].dtype)

    out_blk = pl.BlockSpec((tm, tn), lambda *g: tile(*g))
    in_specs = [pl.BlockSpec((tm, K), lambda *g: (tile(*g)[0], 0)), b_spec]
    args = [a, b]
    if residual is not None:
        in_specs.append(out_blk)
        args.append(residual)
    if dep is not None:
        in_specs.append(pl.BlockSpec(dep.shape, lambda *g: (0, 0)))
        args.append(dep)
    outs = pl.pallas_call(
        body, name=name, grid=(N // tn, M // tm) if columns_first else (M // tm, N // tn),
        in_specs=in_specs, out_specs=[out_blk] * n_out, out_shape=[_sds((M, N), out_dtype)] * n_out,
        compiler_params=_cparams("parallel", "parallel"),
    )(*args)
    return outs if relu2 else outs[0]


def _mm_nt(a, b, name, out_dtype=F32, relu2_pre=None, dep=None, tm=512, tn=512):
    M, K = a.shape
    grouped = b.ndim == 3
    N = b.shape[1] if grouped else b.shape[0]
    tm, tn = _fit(M, tm), _fit(N, tn)
    nt = (((1,), (1,)), ((), ()))
    if grouped:
        G, _, Kg = b.shape
        b_spec = pl.BlockSpec((G, tn, Kg), lambda i, j: (0, j, 0))
    else:
        b_spec = pl.BlockSpec((tn, K), lambda i, j: (j, 0))

    def body(*refs):
        a_ref, b_ref = refs[0], refs[1]
        p_ref = refs[2] if relu2_pre is not None else None
        out_ref = refs[2 + (relu2_pre is not None) + (dep is not None)]
        if grouped:
            acc = lax.dot_general(a_ref[:, 0:Kg], b_ref[0], nt, preferred_element_type=F32)
            for g in range(1, G):
                acc += lax.dot_general(a_ref[:, g * Kg:(g + 1) * Kg], b_ref[g], nt, preferred_element_type=F32)
        else:
            acc = lax.dot_general(a_ref[...], b_ref[...], nt, preferred_element_type=F32)
        if p_ref is not None:
            acc = acc * (2.0 * jnp.maximum(p_ref[...].astype(F32), 0.0))
        out_ref[...] = acc.astype(out_ref.dtype)

    out_blk = pl.BlockSpec((tm, tn), lambda i, j: (i, j))
    in_specs = [pl.BlockSpec((tm, K), lambda i, j: (i, 0)), b_spec]
    args = [a, b]
    if relu2_pre is not None:
        in_specs.append(out_blk)
        args.append(relu2_pre)
    if dep is not None:
        in_specs.append(pl.BlockSpec(dep.shape, lambda i, j: (0, 0)))
        args.append(dep)
    return pl.pallas_call(
        body, name=name, grid=(M // tm, N // tn),
        in_specs=in_specs, out_specs=out_blk, out_shape=_sds((M, N), out_dtype),
        compiler_params=_cparams("parallel", "parallel"),
    )(*args)


def _wgrad_wide_a(a, b, name, tm=512):
    T, M = a.shape
    N = b.shape[1]
    tm = _fit(M, tm)

    def body(a_ref, b_ref, out_ref):
        out_ref[...] = lax.dot_general(a_ref[...], b_ref[...], (((0,), (0,)), ((), ())),
                                       preferred_element_type=F32).astype(out_ref.dtype)

    return pl.pallas_call(
        body, name=name, grid=(M // tm,),
        in_specs=[pl.BlockSpec((T, tm), lambda i: (0, i)),
                  pl.BlockSpec((T, N), lambda i: (0, 0), pipeline_mode=pl.Buffered(1))],
        out_specs=pl.BlockSpec((tm, N), lambda i: (i, 0)), out_shape=_sds((M, N), MXU_DTYPE),
        compiler_params=_cparams("parallel"),
    )(a, b)


def _wgrad_wide_b(a, b, name, groups=None, tn=512, t_chunk=512):
    T, M = a.shape
    N = b.shape[1]
    tn = _fit(N if groups is None else N // groups, tn)
    t_chunk = _fit(T, t_chunk)

    def body(a_ref, b_ref, out_ref, at_ref):
        @pl.when(pl.program_id(0) == 0)
        def _():
            for c in range(0, T, t_chunk):
                at_ref[:, c:c + t_chunk] = a_ref[c:c + t_chunk, :].T

        out_ref[...] = jnp.dot(at_ref[...], b_ref[...], preferred_element_type=F32).astype(out_ref.dtype)

    if groups is None:
        out_spec = pl.BlockSpec((M, tn), lambda j: (0, j))
        out_shape = _sds((M, N), MXU_DTYPE)
    else:
        per = N // groups // tn
        out_spec = pl.BlockSpec((None, M, tn), lambda j: (j // per, 0, j % per))
        out_shape = _sds((groups, M, N // groups), MXU_DTYPE)
    return pl.pallas_call(
        body, name=name, grid=(N // tn,),
        in_specs=[pl.BlockSpec((T, M), lambda j: (0, 0), pipeline_mode=pl.Buffered(1)),
                  pl.BlockSpec((T, tn), lambda j: (0, j))],
        out_specs=out_spec, out_shape=out_shape,
        scratch_shapes=[pltpu.VMEM((M, T), MXU_DTYPE)],
        compiler_params=_cparams("arbitrary"),
    )(a, b)


def _loss_and_grad(y, target, name, tr=512):
    T, D = y.shape

    def body(y_ref, t_ref, loss_ref, dx_ref, dxb_ref):
        @pl.when(pl.program_id(0) == 0)
        def _():
            loss_ref[...] = jnp.zeros_like(loss_ref)

        err = y_ref[...] - t_ref[...]
        loss_ref[...] += 0.5 * jnp.sum(jnp.mean(err * err, axis=-1, keepdims=True), axis=0, keepdims=True)
        dx = err * (1.0 / D)
        dx_ref[...] = dx
        dxb_ref[...] = dx.astype(dxb_ref.dtype)

    row = lambda i: (i, 0)
    return pl.pallas_call(
        body, name=name, grid=(T // tr,),
        in_specs=[pl.BlockSpec((tr, D), row), pl.BlockSpec((tr, D), row)],
        out_specs=[pl.BlockSpec((8, LANE), lambda i: (0, 0)), pl.BlockSpec((tr, D), row), pl.BlockSpec((tr, D), row)],
        out_shape=[_sds((8, LANE), F32), _sds((T, D), F32), _sds((T, D), MXU_DTYPE)],
        compiler_params=_cparams("arbitrary"),
    )(y, target)


SGU_ROWS = 512


def _sgu_mixed(v, w_ref, b_ref, n_heads):
    parts = [jnp.dot(w_ref[h], v[:, h * HEAD_DIM:(h + 1) * HEAD_DIM], preferred_element_type=F32) for h in range(n_heads)]
    return jnp.concatenate(parts, axis=1) + b_ref[...]


def _sgu_fwd(p, w_tril, bmat, name):
    T = p.shape[0]
    H = w_tril.shape[0]
    AW = H * HEAD_DIM
    rows = _fit(T, SGU_ROWS)

    def body(u_ref, v_ref, w_ref, b_ref, y_ref):
        for c in range(0, rows, CHUNK):
            ch = pl.ds(c, CHUNK)
            mixed = _sgu_mixed(v_ref[ch, :].astype(MXU_DTYPE), w_ref, b_ref, H)
            y_ref[ch, :] = (u_ref[ch, :].astype(F32) * mixed).astype(y_ref.dtype)

    const3 = lambda c: (0, 0, 0)
    return pl.pallas_call(
        body, name=name, grid=(T // rows,),
        in_specs=[pl.BlockSpec((rows, AW), lambda c: (c, 0)), pl.BlockSpec((rows, AW), lambda c: (c, 1)),
                  pl.BlockSpec((H, CHUNK, CHUNK), const3), pl.BlockSpec((CHUNK, AW), lambda c: (0, 0))],
        out_specs=pl.BlockSpec((rows, AW), lambda c: (c, 0)),
        out_shape=_sds((T, AW), MXU_DTYPE),
        compiler_params=_cparams("parallel"),
    )(p, p, w_tril, bmat)


def _sgu_bwd(dymix, p, w_tril, w_tril_t, bmat, name):
    T = p.shape[0]
    H = w_tril.shape[0]
    AW = H * HEAD_DIM
    rows = _fit(T, SGU_ROWS)

    def body(dy_ref, u_ref, v_ref, w_ref, wt_ref, b_ref, du_ref, dv_ref, dw_ref, db_ref):
        @pl.when(pl.program_id(0) == 0)
        def _():
            dw_ref[...] = jnp.zeros_like(dw_ref)
            db_ref[...] = jnp.zeros_like(db_ref)

        for c in range(0, rows, CHUNK):
            ch = pl.ds(c, CHUNK)
            v = v_ref[ch, :].astype(MXU_DTYPE)
            dy = dy_ref[ch, :].astype(F32)
            du_ref[ch, :] = (dy * _sgu_mixed(v, w_ref, b_ref, H)).astype(du_ref.dtype)
            dm = dy * u_ref[ch, :].astype(F32)
            db_ref[...] += dm
            dm_c = dm.astype(MXU_DTYPE)
            dv = []
            for h in range(H):
                sl = slice(h * HEAD_DIM, (h + 1) * HEAD_DIM)
                dv.append(jnp.dot(wt_ref[h], dm_c[:, sl], preferred_element_type=F32))
                dw_ref[h] += lax.dot_general(dm_c[:, sl], v[:, sl], (((1,), (1,)), ((), ())), preferred_element_type=F32)
            dv_ref[ch, :] = jnp.concatenate(dv, axis=1).astype(dv_ref.dtype)

    const3 = lambda c: (0, 0, 0)
    blk = pl.BlockSpec((rows, AW), lambda c: (c, 0))
    return pl.pallas_call(
        body, name=name, grid=(T // rows,),
        in_specs=[blk, blk, pl.BlockSpec((rows, AW), lambda c: (c, 1)),
                  pl.BlockSpec((H, CHUNK, CHUNK), const3), pl.BlockSpec((H, CHUNK, CHUNK), const3),
                  pl.BlockSpec((CHUNK, AW), lambda c: (0, 0))],
        out_specs=[blk, blk, pl.BlockSpec((H, CHUNK, CHUNK), const3), pl.BlockSpec((CHUNK, AW), lambda c: (0, 0))],
        out_shape=[_sds((T, AW), MXU_DTYPE), _sds((T, AW), MXU_DTYPE), _sds((H, CHUNK, CHUNK), F32), _sds((CHUNK, AW), F32)],
        compiler_params=_cparams("arbitrary"),
    )(dymix, p, p, w_tril, w_tril_t, bmat)


def _shift_down(z, s, row):
    return jnp.where(row >= s, pltpu.roll(z, s, 0), 0.0)


def _shift_up(z, s, row, T):
    return jnp.where(row < T - s, pltpu.roll(z, T - s, 0), 0.0)


def _conv_fwd(p, w_conv, AW, name):
    T = p.shape[0]
    BW = w_conv.shape[1]
    nb = BW // LANE
    b0 = 2 * AW // LANE

    def body(b_ref, c_ref, x_ref, w_ref, y_ref):
        row = lax.broadcasted_iota(jnp.int32, (T, LANE), 0)
        z = c_ref[...].astype(F32) * x_ref[...].astype(F32)
        w0, w1, w2 = w_ref[0:1, :], w_ref[1:2, :], w_ref[2:3, :]
        conv = w2 * z + w1 * _shift_down(z, 1, row) + w0 * _shift_down(z, 2, row)
        y_ref[...] = (b_ref[...].astype(F32) * conv).astype(y_ref.dtype)

    return pl.pallas_call(
        body, name=name, grid=(nb,),
        in_specs=[pl.BlockSpec((T, LANE), lambda j: (0, b0 + j)), pl.BlockSpec((T, LANE), lambda j: (0, b0 + nb + j)),
                  pl.BlockSpec((T, LANE), lambda j: (0, b0 + 2 * nb + j)), pl.BlockSpec((CONV_WIDTH, LANE), lambda j: (0, j))],
        out_specs=pl.BlockSpec((T, LANE), lambda j: (0, j)),
        out_shape=_sds((T, BW), MXU_DTYPE),
        compiler_params=_cparams("parallel"),
    )(p, p, p, w_conv)


def _conv_bwd(dymix, p, w_conv, AW, name):
    T = p.shape[0]
    BW = w_conv.shape[1]
    nb = BW // LANE
    b0 = 2 * AW // LANE
    y0 = AW // LANE

    def body(dy_ref, b_ref, c_ref, x_ref, w_ref, db_ref, dc_ref, dxb_ref, dw_ref):
        row = lax.broadcasted_iota(jnp.int32, (T, LANE), 0)
        cv, xv, dy = c_ref[...].astype(F32), x_ref[...].astype(F32), dy_ref[...].astype(F32)
        w0, w1, w2 = w_ref[0:1, :], w_ref[1:2, :], w_ref[2:3, :]
        z = cv * xv
        z1 = _shift_down(z, 1, row)
        z2 = _shift_down(z, 2, row)
        conv = w2 * z + w1 * z1 + w0 * z2
        db_ref[...] = (dy * conv).astype(db_ref.dtype)
        dconv = dy * b_ref[...].astype(F32)
        dz = w2 * dconv + w1 * _shift_up(dconv, 1, row, T) + w0 * _shift_up(dconv, 2, row, T)
        dc_ref[...] = (dz * xv).astype(dc_ref.dtype)
        dxb_ref[...] = (dz * cv).astype(dxb_ref.dtype)
        dw_ref[0:1, :] = jnp.sum(dconv * z2, axis=0, keepdims=True)
        dw_ref[1:2, :] = jnp.sum(dconv * z1, axis=0, keepdims=True)
        dw_ref[2:3, :] = jnp.sum(dconv * z, axis=0, keepdims=True)

    col = lambda j: (0, j)
    return pl.pallas_call(
        body, name=name, grid=(nb,),
        in_specs=[pl.BlockSpec((T, LANE), lambda j: (0, y0 + j)),
                  pl.BlockSpec((T, LANE), lambda j: (0, b0 + j)), pl.BlockSpec((T, LANE), lambda j: (0, b0 + nb + j)),
                  pl.BlockSpec((T, LANE), lambda j: (0, b0 + 2 * nb + j)), pl.BlockSpec((CONV_WIDTH, LANE), col)],
        out_specs=[pl.BlockSpec((T, LANE), col)] * 3 + [pl.BlockSpec((CONV_WIDTH, LANE), col)],
        out_shape=[_sds((T, BW), MXU_DTYPE)] * 3 + [_sds((CONV_WIDTH, BW), F32)],
        compiler_params=_cparams("parallel"),
    )(dymix, p, p, p, w_conv)


def _head_sum(x, col_head, n_heads):
    out = jnp.zeros_like(x)
    for h in range(n_heads):
        sel = col_head == h
        out = jnp.where(sel, jnp.sum(jnp.where(sel, x, 0.0), axis=-1, keepdims=True), out)
    return out


def _same_head(width):
    assert width == 2 * HEAD_DIM
    return lax.broadcasted_iota(jnp.int32, (1, width), 1) < HEAD_DIM


def _head_sum2(x, first):
    s0 = jnp.sum(jnp.where(first, x, 0.0), axis=-1, keepdims=True)
    s1 = jnp.sum(jnp.where(first, 0.0, x), axis=-1, keepdims=True)
    return jnp.where(first, s0, s1)


def _head_norm(x, g, first):
    r = lax.rsqrt(_head_sum2(x * x, first) * (1.0 / HEAD_DIM) + EPS)
    return x * r * g, r


def _head_norm_bwd(dy, x, g, r, first):
    gdy = dy * g
    mean_xg = _head_sum2(x * gdy, first) * (1.0 / HEAD_DIM)
    return r * gdy - x * (r * r * r) * mean_xg, dy * x * r


ATT_SPAN_MIN = 512
ATT_FWD_UNROLL = 4
ATT_BWD_UNROLL = 4
HEADS_PER_LANES = LANE // HEAD_DIM


def _attn_geometry(T, d):
    m = max(1, ATT_SPAN_MIN // (ATT_BLK * d))
    return m, ATT_BLK * d * m, ATT_BLK * d, T // (ATT_BLK * d)


def _rows(ref, start, d):
    return ref[pl.ds(start, ATT_BLK, stride=d), :] if d > 1 else ref[pl.ds(start, ATT_BLK), :]


def _set_rows(ref, start, d, value):
    if d > 1:
        ref[pl.ds(start, ATT_BLK, stride=d), :] = value
    else:
        ref[pl.ds(start, ATT_BLK), :] = value


def _for_each_block(d, m, task, unroll):
    for j in range(m):
        if d == 1:
            task(0, j)
        else:
            lax.fori_loop(0, d, lambda r, carry, j=j: (task(r, j), carry)[1], 0, unroll=min(unroll, d))


def _head_slices():
    return [slice(h * HEAD_DIM, (h + 1) * HEAD_DIM) for h in range(HEADS_PER_LANES)]


def _qk_norm_fwd(p, gains, q_start, PW, name, tr=512):
    T = p.shape[0]
    n_norm = gains.shape[1] // PW
    n = n_norm * 3 // 2
    c0 = q_start // PW

    def body(*refs):
        x_refs, g_ref, out_ref = refs[:n], refs[n], refs[n + 1]
        first = _same_head(LANE)
        for i in range(n):
            for c in range(0, PW, LANE):
                lo = i * PW + c
                x = x_refs[i][:, c:c + LANE].astype(F32)
                out_ref[:, lo:lo + LANE] = _head_norm(x, g_ref[:, lo:lo + LANE], first)[0] if i < n_norm else x

    return pl.pallas_call(
        body, name=name, grid=(T // tr,),
        in_specs=[pl.BlockSpec((tr, PW), lambda i, j=j: (i, c0 + j)) for j in range(n)]
        + [pl.BlockSpec((1, n_norm * PW), lambda i: (0, 0))],
        out_specs=pl.BlockSpec((tr, n * PW), lambda i: (i, 0)), out_shape=_sds((T, n * PW), F32),
        compiler_params=_cparams("parallel"),
    )(*([p] * n), gains)


def _qk_norm_bwd(dns, p, gains, q_start, PW, name, tr=512):
    T = p.shape[0]
    n = len(dns)
    c0 = q_start // PW

    def body(*refs):
        d_refs, x_refs, g_ref, out_ref, acc_ref = refs[:n], refs[n:2 * n], refs[2 * n], refs[2 * n + 1], refs[2 * n + 2]

        @pl.when(pl.program_id(0) == 0)
        def _():
            acc_ref[...] = jnp.zeros_like(acc_ref)

        first = _same_head(LANE)
        for i in range(n):
            for c in range(0, PW, LANE):
                lo = i * PW + c
                x, gv = x_refs[i][:, c:c + LANE].astype(F32), g_ref[:, lo:lo + LANE]
                _, r = _head_norm(x, gv, first)
                dx, g_part = _head_norm_bwd(d_refs[i][:, c:c + LANE], x, gv, r, first)
                out_ref[:, lo:lo + LANE] = dx.astype(out_ref.dtype)
                acc_ref[0:1, lo:lo + LANE] += jnp.sum(g_part, axis=0, keepdims=True)

    return pl.pallas_call(
        body, name=name, grid=(T // tr,),
        in_specs=[pl.BlockSpec((tr, PW), lambda i: (i, 0))] * n
        + [pl.BlockSpec((tr, PW), lambda i, j=j: (i, c0 + j)) for j in range(n)]
        + [pl.BlockSpec((1, n * PW), lambda i: (0, 0))],
        out_specs=[pl.BlockSpec((tr, n * PW), lambda i: (i, 0)), pl.BlockSpec((8, n * PW), lambda i: (0, 0))],
        out_shape=[_sds((T, n * PW), MXU_DTYPE), _sds((8, n * PW), F32)],
        compiler_params=_cparams("arbitrary"),
    )(*dns, *([p] * n), gains)


def _attn_fwd(qkv, g, d, PW, name):
    T = qkv.shape[0]
    B, W = ATT_BLK, LANE
    m, span, group, _ = _attn_geometry(T, d)
    c_q = g * PW // W
    c_k, c_v = c_q + 3 * PW // W, c_q + 6 * PW // W
    scale = HEAD_DIM ** -0.5

    def body(q_ref, k_ref, v_ref, kp_ref, vp_ref, o_ref, lse_ref):
        n = pl.program_id(1)
        qi = lax.broadcasted_iota(jnp.int32, (B, 2 * B), 0)
        kj = lax.broadcasted_iota(jnp.int32, (B, 2 * B), 1)
        band = (kj >= qi) & (kj <= qi + B)

        def task(r, j):
            cur = j * group + r
            if j == 0:
                kp, vp = _rows(kp_ref, r, d), _rows(vp_ref, r, d)
            else:
                kp, vp = _rows(k_ref, cur - group, d), _rows(v_ref, cur - group, d)
            mask = band & ((n * m + j > 0) | (kj >= B))
            qn = _rows(q_ref, cur, d).astype(MXU_DTYPE)
            kn = jnp.concatenate([kp, _rows(k_ref, cur, d)], axis=0).astype(MXU_DTYPE)
            vcat = jnp.concatenate([vp, _rows(v_ref, cur, d)], axis=0).astype(MXU_DTYPE)
            o_parts, lse_parts = [], []
            for sl in _head_slices():
                s = lax.dot_general(qn[:, sl], kn[:, sl], (((1,), (1,)), ((), ())), preferred_element_type=F32) * scale
                s = jnp.where(mask, s, NEG_INF)
                mx = jnp.max(s, axis=-1, keepdims=True)
                e = jnp.exp(s - mx)
                den = jnp.sum(e, axis=-1, keepdims=True)
                o_parts.append(jnp.dot(e.astype(MXU_DTYPE), vcat[:, sl], preferred_element_type=F32) / den)
                lse_parts.append(jnp.broadcast_to(mx + jnp.log(den), (B, HEAD_DIM)))
            _set_rows(o_ref, cur, d, jnp.concatenate(o_parts, axis=1))
            _set_rows(lse_ref, cur, d, jnp.concatenate(lse_parts, axis=1))

        _for_each_block(d, m, task, ATT_FWD_UNROLL)

    main = lambda c0: pl.BlockSpec((span, W), lambda hp, n: (n, c0 + hp))
    prev = lambda c0: pl.BlockSpec((group, W), lambda hp, n: (jnp.maximum(n * m - 1, 0), c0 + hp))
    out_blk = pl.BlockSpec((span, W), lambda hp, n: (n, hp))
    return pl.pallas_call(
        body, name=name, grid=(PW // W, T // span),
        in_specs=[main(c_q), main(c_k), main(c_v), prev(c_k), prev(c_v)],
        out_specs=[out_blk, out_blk],
        out_shape=[_sds((T, PW), F32), _sds((T, PW), F32)],
        compiler_params=_cparams("parallel", "parallel"),
    )(qkv, qkv, qkv, qkv, qkv)


def _attn_bwd(qkv, lse, do, corr, g, d, PW, name):
    T = qkv.shape[0]
    B, W = ATT_BLK, LANE
    m, span, group, n_blocks = _attn_geometry(T, d)
    c_q = g * PW // W
    c_k, c_v = c_q + 3 * PW // W, c_q + 6 * PW // W
    scale = HEAD_DIM ** -0.5
    nt = (((1,), (1,)), ((), ()))
    tn = (((0,), (0,)), ((), ()))

    def body(q_ref, k_ref, v_ref, do_ref, l_ref, c_ref, kp_ref, vp_ref, qx_ref, dox_ref, lx_ref, cx_ref,
             dq_ref, dk_ref, dv_ref):
        n = pl.program_id(1)
        i1 = lax.broadcasted_iota(jnp.int32, (B, B), 0)
        j1 = lax.broadcasted_iota(jnp.int32, (B, B), 1)
        i2 = lax.broadcasted_iota(jnp.int32, (2 * B, B), 0)
        j2 = lax.broadcasted_iota(jnp.int32, (2 * B, B), 1)

        def task(r, j):
            cur = j * group + r
            blk = n * m + j
            q_c, k_c, v_c = _rows(q_ref, cur, d), _rows(k_ref, cur, d), _rows(v_ref, cur, d)
            do_c, l_c, c_c = _rows(do_ref, cur, d), _rows(l_ref, cur, d), _rows(c_ref, cur, d)
            if j == 0:
                k_p, v_p = _rows(kp_ref, r, d), _rows(vp_ref, r, d)
            else:
                k_p, v_p = _rows(k_ref, cur - group, d), _rows(v_ref, cur - group, d)
            if j == m - 1:
                nxt = [_rows(ref, r, d) for ref in (qx_ref, dox_ref, lx_ref, cx_ref)]
            else:
                nxt = [_rows(ref, cur + group, d) for ref in (q_ref, do_ref, l_ref, c_ref)]
            q_x, do_x, l_x, c_x = nxt
            kn_c, kn_p, v_c, v_p = (a.astype(MXU_DTYPE) for a in (k_c, k_p, v_c, v_p))
            qn_c = q_c.astype(MXU_DTYPE)
            qn_cat = jnp.concatenate([qn_c, q_x.astype(MXU_DTYPE)], axis=0)
            do_cb = do_c.astype(MXU_DTYPE)
            do_cat = jnp.concatenate([do_cb, do_x.astype(MXU_DTYPE)], axis=0)
            l_cat = jnp.concatenate([l_c, l_x], axis=0)
            c_cat = jnp.concatenate([c_c, c_x], axis=0)
            mask_p = (j1 >= i1) & (blk > 0)
            mask_c = ((i2 < B) & (j2 <= i2)) | ((i2 >= B) & (j2 >= i2 - B) & (blk + 1 < n_blocks))
            dqn, dkn, dv = [], [], []
            for h, sl in enumerate(_head_slices()):
                lane = slice(h * HEAD_DIM, h * HEAD_DIM + 1)
                s_p = lax.dot_general(qn_c[:, sl], kn_p[:, sl], nt, preferred_element_type=F32) * scale
                pr_p = jnp.where(mask_p, jnp.exp(s_p - l_c[:, lane]), 0.0)
                dp_p = lax.dot_general(do_cb[:, sl], v_p[:, sl], nt, preferred_element_type=F32)
                ds_p = (pr_p * (dp_p + c_c[:, lane]) * scale).astype(MXU_DTYPE)
                s_c = lax.dot_general(qn_cat[:, sl], kn_c[:, sl], nt, preferred_element_type=F32) * scale
                pr_c = jnp.where(mask_c, jnp.exp(s_c - l_cat[:, lane]), 0.0)
                dp_c = lax.dot_general(do_cat[:, sl], v_c[:, sl], nt, preferred_element_type=F32)
                ds_c = (pr_c * (dp_c + c_cat[:, lane]) * scale).astype(MXU_DTYPE)
                dqn.append(jnp.dot(ds_p, kn_p[:, sl], preferred_element_type=F32)
                           + jnp.dot(ds_c[:B], kn_c[:, sl], preferred_element_type=F32))
                dkn.append(lax.dot_general(ds_c, qn_cat[:, sl], tn, preferred_element_type=F32))
                dv.append(lax.dot_general(pr_c.astype(MXU_DTYPE), do_cat[:, sl], tn, preferred_element_type=F32))
            _set_rows(dq_ref, cur, d, jnp.concatenate(dqn, axis=1))
            _set_rows(dk_ref, cur, d, jnp.concatenate(dkn, axis=1))
            _set_rows(dv_ref, cur, d, jnp.concatenate(dv, axis=1))

        _for_each_block(d, m, task, ATT_BWD_UNROLL)

    main = lambda c0: pl.BlockSpec((span, W), lambda hp, n: (n, c0 + hp))
    prev = lambda c0: pl.BlockSpec((group, W), lambda hp, n: (jnp.maximum(n * m - 1, 0), c0 + hp))
    nxt = lambda c0: pl.BlockSpec((group, W), lambda hp, n: (jnp.minimum((n + 1) * m, n_blocks - 1), c0 + hp))
    own = pl.BlockSpec((span, W), lambda hp, n: (n, hp))
    return pl.pallas_call(
        body, name=name, grid=(PW // W, T // span),
        in_specs=[main(c_q), main(c_k), main(c_v), main(0), main(0), main(0), prev(c_k), prev(c_v),
                  nxt(c_q), nxt(0), nxt(0), nxt(0)],
        out_specs=[own, own, own],
        out_shape=[_sds((T, PW), F32)] * 3,
        compiler_params=_cparams("parallel", "parallel"),
    )(qkv, qkv, qkv, do, lse, corr, qkv, qkv, qkv, do, lse, corr)


def _softmax3(lses):
    mx = jnp.maximum(jnp.maximum(lses[0], lses[1]), lses[2])
    ex = [jnp.exp(l - mx) for l in lses]
    inv = 1.0 / (ex[0] + ex[1] + ex[2])
    return [e * inv for e in ex]


def _mix_fwd(os_, lses, name, tr=512):
    T, PW = os_[0].shape

    def body(o0, o1, o2, l0, l1, l2, y_ref):
        alpha = _softmax3([l0[...], l1[...], l2[...]])
        for g, o_ref in enumerate((o0, o1, o2)):
            y_ref[:, g * PW:(g + 1) * PW] = (o_ref[...] * alpha[g]).astype(y_ref.dtype)

    blk = pl.BlockSpec((tr, PW), lambda i: (i, 0))
    return pl.pallas_call(
        body, name=name, grid=(T // tr,),
        in_specs=[blk] * 6, out_specs=pl.BlockSpec((tr, 3 * PW), lambda i: (i, 0)),
        out_shape=_sds((T, 3 * PW), MXU_DTYPE),
        compiler_params=_cparams("parallel"),
    )(*os_, *lses)


def _mix_bwd(dymix, os_, lses, c_start, name, tr=512):
    T, PW = os_[0].shape
    HP = PW // HEAD_DIM
    c0 = c_start // PW

    def body(d0, d1, d2, o0, o1, o2, l0, l1, l2, do0, do1, do2, dl0, dl1, dl2):
        col_head = lax.broadcasted_iota(jnp.int32, (tr, PW), 1) // HEAD_DIM
        alpha = _softmax3([l0[...], l1[...], l2[...]])
        dys = [d0[...].astype(F32), d1[...].astype(F32), d2[...].astype(F32)]
        dots = [_head_sum(dy * o_ref[...], col_head, HP) for dy, o_ref in zip(dys, (o0, o1, o2))]
        mean_dot = alpha[0] * dots[0] + alpha[1] * dots[1] + alpha[2] * dots[2]
        for g, (do_ref, dl_ref) in enumerate(((do0, dl0), (do1, dl1), (do2, dl2))):
            do_ref[...] = dys[g] * alpha[g]
            dl_ref[...] = -alpha[g] * mean_dot

    blk = pl.BlockSpec((tr, PW), lambda i: (i, 0))
    dy_specs = [pl.BlockSpec((tr, PW), lambda i, g=g: (i, c0 + g)) for g in range(3)]
    outs = pl.pallas_call(
        body, name=name, grid=(T // tr,),
        in_specs=dy_specs + [blk] * 6, out_specs=[blk] * 6,
        out_shape=[_sds((T, PW), F32)] * 6,
        compiler_params=_cparams("parallel"),
    )(dymix, dymix, dymix, *os_, *lses)
    return outs[:3], outs[3:]


def _adamw_math(w, g, m, v):
    m2 = ADAM_B1 * m + (1.0 - ADAM_B1) * g
    v2 = ADAM_B2 * v + (1.0 - ADAM_B2) * (g * g)
    m_hat = m2 / (1.0 - ADAM_B1 ** ADAM_STEP)
    v_hat = v2 / (1.0 - ADAM_B2 ** ADAM_STEP)
    delta = -ADAM_LR * (m_hat / (jnp.sqrt(v_hat) + ADAM_EPS) + ADAM_WD * w)
    return delta, m2, v2


def _adamw_layer(layer, w, m, v, own, landed, me, prev, name, tr=256):
    _, R, C = w.shape
    tr = next(t for t in range(min(tr, R) // 16 * 16, 0, -16) if R % t == 0)

    def body(me_ref, w_ref, m_ref, v_ref, own_ref, land_ref, *rest):
        g_ref, d_ref, m2_ref, v2_ref = rest[-4:]
        g = own_ref[...].astype(F32)
        for j in range(N_PEER):
            g = g + land_ref[j].astype(F32)
        delta, m2, v2 = _adamw_math(w_ref[...], g, m_ref[...], v_ref[...])
        g_ref[...] = g
        d_ref[...] = delta
        m2_ref[...] = m2
        v2_ref[...] = v2

    lay = pl.BlockSpec((None, tr, C), lambda i, me_ref: (layer, i, 0))
    in_specs = [lay, lay, lay, pl.BlockSpec((None, tr, C), lambda i, me_ref: (me_ref[0], i, 0)),
                pl.BlockSpec((N_PEER, tr, C), lambda i, me_ref: (0, i, 0))]
    args = [me, w, m, v, own, landed]
    aliases = {}
    if prev is not None:
        in_specs += [pl.BlockSpec(memory_space=pl.ANY)] * 4
        args += list(prev)
        aliases = {6 + i: i for i in range(4)}
    return pl.pallas_call(
        body, name=name,
        grid_spec=pltpu.PrefetchScalarGridSpec(num_scalar_prefetch=1, grid=(R // tr,), in_specs=in_specs, out_specs=[lay] * 4),
        out_shape=[_sds(w.shape, F32)] * 4,
        input_output_aliases=aliases,
        compiler_params=_cparams("parallel"),
    )(*args)


def _sum_parts(parts, name):
    _, R, C = parts.shape

    def body(p_ref, out_ref):
        g = p_ref[0]
        for j in range(1, N_DEV):
            g = g + p_ref[j]
        out_ref[...] = g

    return pl.pallas_call(
        body, name=name, grid=(1,),
        in_specs=[pl.BlockSpec((N_DEV, R, C), lambda i: (0, 0, 0))], out_specs=pl.BlockSpec((R, C), lambda i: (0, 0)),
        out_shape=_sds((R, C), F32), compiler_params=_cparams("arbitrary"),
    )(parts)


def _adamw_small(ws, gs, ms, vs, name):
    n = len(ws)

    def body(*refs):
        ins, outs = refs[:4 * n], refs[4 * n:]
        for i in range(n):
            delta, m2, v2 = _adamw_math(ins[i][...], ins[n + i][...], ins[2 * n + i][...], ins[3 * n + i][...])
            outs[i][...] = delta
            outs[n + i][...] = m2
            outs[2 * n + i][...] = v2

    outs = pl.pallas_call(
        body, name=name, out_shape=[_sds(w.shape, F32) for w in ws] * 3,
    )(*ws, *gs, *ms, *vs)
    return outs[:n], outs[n:2 * n], outs[2 * n:]


def _pack(arrays, rows_multiple=8):
    flat = []
    for a in arrays:
        a = a.reshape(-1).astype(F32)
        flat.append(jnp.pad(a, (0, (-a.shape[0]) % LANE)))
    flat = jnp.concatenate(flat)
    flat = jnp.pad(flat, (0, (-flat.shape[0]) % (LANE * rows_multiple)))
    return flat.reshape(-1, LANE)


def _unpack(packed, shapes):
    flat = packed.reshape(-1)
    out, off = [], 0
    for s in shapes:
        size = 1
        for dim in s:
            size *= dim
        out.append(flat[off:off + size].reshape(s))
        off += size + (-size) % LANE
    return out


def _layer_fwd(x, wts, getw, dims, dep=None):
    AW, BW, PW, DP = dims["AW"], dims["BW"], dims["PW"], dims["DP"]
    q_start = 2 * AW + 3 * BW
    h, r1 = _rmsnorm_fwd(x, wts["attn_norm"], "rmsnorm_fwd", dep=dep)
    p = _mm_nt(h, getw("w_in", h), "proj_in", out_dtype=MXU_DTYPE, tm=1024, tn=1408)
    y_a = _sgu_fwd(p, wts["sgu_tril"], wts["sgu_bmat"], "sgu_fwd")
    y_b = _conv_fwd(p, getw("conv_w", y_a), AW, "conv_fwd")
    qkn = _qk_norm_fwd(p, wts["qk_gain"], q_start, PW, "qk_norm_fwd")
    os_, lses = [], []
    for g, d in enumerate(DILATIONS):
        o, lse = _attn_fwd(qkn, g, d, PW, "attn_fwd_%d" % d)
        os_.append(o)
        lses.append(lse)
    y_c = _mix_fwd(os_, lses, "mix_fwd")
    ymix = jnp.concatenate([y_a, y_b, y_c], axis=1)
    x1 = _mm_nn(ymix, getw("w_out", ymix), "proj_out", residual=x, tm=1024, tn=1024)
    h2, r2 = _rmsnorm_fwd(x1, wts["mlp_norm"], "rmsnorm_fwd")
    a, hid = _mm_nn(h2, getw("w_mlp_in", h2), "mlp_in", out_dtype=MXU_DTYPE, relu2=True, tm=1024, tn=1024)
    x2 = _mm_nn(hid, getw("w_mlp_out", hid), "mlp_out", residual=x1, columns_first=True, tm=512, tn=1024)
    saved = dict(x=x, h=h, r1=r1, p=p, qkn=qkn, os=os_, lses=lses, ymix=ymix, x1=x1, h2=h2, r2=r2, a=a, hid=hid)
    return x2, saved


def _layer_bwd(dx, dxb, wts, getw, scatter, saved, dims):
    AW, BW, PW, DP = dims["AW"], dims["BW"], dims["PW"], dims["DP"]
    q_start = 2 * AW + 3 * BW
    D = dx.shape[1]
    g_w2 = _wgrad_wide_a(saved["hid"], dxb, "mlp_out_wgrad")
    token = scatter("w_mlp_out", g_w2.reshape(N_DEV, -1, D))
    da = _mm_nt(dxb, getw("w_mlp_out", None), "mlp_out_dgrad", out_dtype=MXU_DTYPE, relu2_pre=saved["a"], dep=token,
                tm=1024, tn=1024)
    g_w1 = _wgrad_wide_b(saved["h2"], da, "mlp_in_wgrad", groups=N_DEV)
    token = scatter("w_mlp_in", g_w1)
    dh2 = _mm_nt(da, getw("w_mlp_in", None), "mlp_in_dgrad", out_dtype=MXU_DTYPE, dep=token, tm=1024, tn=512)
    dx1, dx1b, g_mlp_norm = _rmsnorm_bwd(dh2, saved["x1"], wts["mlp_norm"], saved["r2"], dx, "rmsnorm_bwd")
    g_wout = _wgrad_wide_b(saved["ymix"], dx1b, "proj_out_wgrad")
    token = scatter("w_out", g_wout.reshape(N_DEV, -1, D))
    dymix = _mm_nt(dx1b, getw("w_out", None), "proj_out_dgrad", out_dtype=MXU_DTYPE, dep=token, tm=1024, tn=1024)
    p = saved["p"]
    du, dv, g_sgu_w, g_sgu_bmat = _sgu_bwd(dymix, p, wts["sgu_tril"], wts["sgu_tril_t"], wts["sgu_bmat"], "sgu_bwd")
    d_b, d_c, d_xb, g_conv = _conv_bwd(dymix, p, getw("conv_w", None), AW, "conv_bwd")
    dos, corrs = _mix_bwd(dymix, saved["os"], saved["lses"], AW + BW, "mix_bwd")
    dqns, dkns, dvs = [], [], []
    for g, d in enumerate(DILATIONS):
        dqn, dkn, dvv = _attn_bwd(saved["qkn"], saved["lses"][g], dos[g], corrs[g], g, d, PW, "attn_bwd_%d" % d)
        dqns.append(dqn)
        dkns.append(dkn)
        dvs.append(dvv.astype(MXU_DTYPE))
    dqk, g_qk = _qk_norm_bwd(dqns + dkns, p, wts["qk_gain"], q_start, PW, "qk_norm_bwd")
    g_q, g_k = (part.reshape(-1, HEAD_DIM).sum(0) for part in jnp.split(g_qk[0], 2))
    dp = jnp.concatenate([du, dv, d_b, d_c, d_xb, dqk] + dvs, axis=1)
    g_win_t = _wgrad_wide_a(dp, saved["h"], "proj_in_wgrad")
    token = scatter("w_in", g_win_t.reshape(N_DEV, DP // N_DEV, D))
    dh = _mm_nn(dp, getw("w_in", None), "proj_in_dgrad", out_dtype=MXU_DTYPE, dep=token, tm=1024, tn=1024)
    dx0, dx0b, g_attn_norm = _rmsnorm_bwd(dh, saved["x"], wts["attn_norm"], saved["r1"], dx1, "rmsnorm_bwd")
    H = AW // HEAD_DIM
    tril = jnp.tril(jnp.ones((CHUNK, CHUNK), F32))
    small = [g_attn_norm.reshape(-1), g_sgu_w * tril, g_sgu_bmat.reshape(CHUNK, H, HEAD_DIM).sum(-1).T,
             g_conv, g_q, g_k, g_mlp_norm.reshape(-1)]
    return dx0, dx0b, small


def kernel(x, attn_norm, w_in, sgu_w, sgu_b, conv_w, q_norm, k_norm, w_out, mlp_norm, w_mlp_in, w_mlp_out, loss_target, m_attn_norm, m_w_in, m_sgu_w, m_sgu_b, m_conv_w, m_q_norm, m_k_norm, m_w_out, m_mlp_norm, m_w_mlp_in, m_w_mlp_out, v_attn_norm, v_w_in, v_sgu_w, v_sgu_b, v_conv_w, v_q_norm, v_k_norm, v_w_out, v_mlp_norm, v_w_mlp_in, v_w_mlp_out):
    n_layers = attn_norm.shape[0]
    T, D = x.shape[1], x.shape[2]
    H = sgu_w.shape[1]
    AW = H * HEAD_DIM
    BW = conv_w.shape[2] * N_DEV
    DP = w_in.shape[2] * N_DEV
    DMIX = w_out.shape[1] * N_DEV
    DFF = w_mlp_in.shape[2] * N_DEV
    PW = (DMIX - AW - BW) // 3
    HP = PW // HEAD_DIM
    dims = dict(AW=AW, BW=BW, PW=PW, DP=DP)
    me = 4 * lax.axis_index("x") + 2 * lax.axis_index("y") + lax.axis_index("c")

    big_names = ("w_in", "w_out", "w_mlp_in", "w_mlp_out")
    tr_in = lambda a: jnp.swapaxes(a, 1, 2)
    big_w = dict(zip(big_names, (tr_in(w_in), w_out, w_mlp_in, w_mlp_out)))
    big_m = dict(zip(big_names, (tr_in(m_w_in), m_w_out, m_w_mlp_in, m_w_mlp_out)))
    big_v = dict(zip(big_names, (tr_in(v_w_in), v_w_out, v_w_mlp_in, v_w_mlp_out)))

    keys = []
    for l in range(n_layers):
        keys += [(l, nm) for nm in big_names]
    keys.insert(1, (0, "conv_w"))
    first_src = big_w[keys[0][1]][keys[0][0]].astype(MXU_DTYPE)
    first_flights, first_token = _exchange_start([first_src], [_own_in_place(first_src, me)], "gather", name="gather_start_first")
    zero = first_token[0, 0]
    srcs = [_pack([conv_w]) + zero if nm == "conv_w" else (big_w[nm][l] + zero).astype(MXU_DTYPE) for l, nm in keys[1:]]
    flights, gather_token = _exchange_start(srcs, [_own_in_place(s, me) for s in srcs], "gather", name="gather_start")
    arriving = dict(zip(keys, first_flights + flights))
    forwarding = {}
    relayout = dict(
        w_in=lambda g: g.reshape(DP, D), w_out=lambda g: g.reshape(DMIX, D),
        w_mlp_in=lambda g: g, w_mlp_out=lambda g: g.reshape(DFF, D),
        conv_w=lambda g: jnp.stack([_unpack(g[j], [conv_w.shape])[0] for j in range(N_DEV)], axis=2).reshape(
            n_layers, CONV_WIDTH, BW))
    gathered = {}

    def forward(key, after):
        _, land = _exchange_wait(arriving[key], after, "gather", name="gather_arrive_%d_%s" % key)
        fl, token = _exchange_start(None, [land], "forward", name="gather_forward_%d_%s" % key)
        forwarding[key] = fl[0]
        return token

    def weight_getter(l):
        def getw(nm, after):
            key = (0, nm) if nm == "conv_w" else (l, nm)
            if key not in gathered:
                ahead = keys[keys.index(key):][:2]
                for k in ahead:
                    if k not in forwarding:
                        after = forward(k, after)
                _, land = _exchange_wait(forwarding[key], after, "forward", name="gather_wait_%d_%s" % key)
                gathered[key] = relayout[nm](land)
            return gathered[key][l] if nm == "conv_w" else gathered[key]
        return getw

    tril = jnp.tril(jnp.ones((CHUNK, CHUNK), F32))
    layers = []
    for l in range(n_layers):
        w_tril = sgu_w[l] * tril
        layers.append(dict(
            attn_norm=attn_norm[l][None], mlp_norm=mlp_norm[l][None],
            sgu_tril=w_tril.astype(MXU_DTYPE), sgu_tril_t=w_tril.transpose(0, 2, 1).astype(MXU_DTYPE),
            sgu_bmat=jnp.repeat(sgu_b[l].T, HEAD_DIM, axis=1),
            qk_gain=jnp.concatenate([jnp.tile(q_norm[l], 3 * HP), jnp.tile(k_norm[l], 3 * HP)])[None]))

    xs = x[0]
    saved = []
    for l in range(n_layers):
        xs, sv = _layer_fwd(xs, layers[l], weight_getter(l), dims, dep=gather_token if l == 0 else None)
        saved.append(sv)
    loss_blk, dx, dxb = _loss_and_grad(xs, loss_target[0], "loss")
    loss = lax.psum(loss_blk[0, 0], ("x", "y", "c"))

    scattering = {}

    def scatter_starter(l):
        def scatter(nm, partials):
            land = lax.empty((N_PEER,) + partials.shape[1:], partials.dtype)
            fl, tok = _exchange_start([partials], [land], "scatter", name="scatter_start_%d_%s" % (l, nm))
            scattering[(l, nm)] = fl[0]
            return tok
        return scatter

    small = [None] * n_layers
    for l in reversed(range(n_layers)):
        dx, dxb, small[l] = _layer_bwd(dx, dxb, layers[l], weight_getter(l), scatter_starter(l), saved[l], dims)

    small_shapes = [s.shape for s in small[0]]
    small_src = [_pack([s for l in range(n_layers) for s in small[l]])]
    small_flights, small_token = _exchange_start(small_src, [_own_in_place(s, me) for s in small_src], "gather_all",
                                                 name="small_start")
    grad_x = dx[None]

    me1 = me.astype(jnp.int32).reshape(1)
    res = {nm: None for nm in big_names}
    after = small_token
    for l in reversed(range(n_layers)):
        for nm in reversed(big_names):
            own, landed = _exchange_wait(scattering[(l, nm)], after, "scatter", name="scatter_wait_%d_%s" % (l, nm))
            res[nm] = _adamw_layer(l, big_w[nm], big_m[nm], big_v[nm], own, landed, me1, res[nm], "adamw_" + nm)
            after = res[nm][0]
    res["w_in"] = [tr_in(a) for a in res["w_in"]]
    big_out = [res[nm] for nm in big_names]

    _, gathered_small = _exchange_wait(small_flights[0], after, "gather_all", name="small_wait")
    summed = _unpack(_sum_parts(gathered_small, "sum_small"), small_shapes * n_layers)
    ns = len(small_shapes)
    g_small = [jnp.stack([summed[l * ns + i] for l in range(n_layers)]) for i in range(ns)]
    g_attn_norm, g_sgu_w, g_sgu_b, g_conv_full, g_q, g_k, g_mlp_norm = g_small
    cs = conv_w.shape[2]
    g_conv = lax.dynamic_slice_in_dim(g_conv_full, me * cs, cs, axis=2)
    sm_w = (attn_norm, sgu_w, sgu_b, conv_w, q_norm, k_norm, mlp_norm)
    sm_m = (m_attn_norm, m_sgu_w, m_sgu_b, m_conv_w, m_q_norm, m_k_norm, m_mlp_norm)
    sm_v = (v_attn_norm, v_sgu_w, v_sgu_b, v_conv_w, v_q_norm, v_k_norm, v_mlp_norm)
    sm_g = (g_attn_norm, g_sgu_w, g_sgu_b, g_conv, g_q, g_k, g_mlp_norm)
    sm_delta, sm_m2, sm_v2 = _adamw_small(sm_w, sm_g, sm_m, sm_v, "adamw_small")

    def ordered(small_list, big_kind):
        b = [big_out[i][big_kind] for i in range(4)]
        return [small_list[0], b[0], small_list[1], small_list[2], small_list[3], small_list[4], small_list[5],
                b[1], small_list[6], b[2], b[3]]

    return (loss, grad_x, *ordered(list(sm_g), 0), *ordered(sm_delta, 1), *ordered(sm_m2, 2), *ordered(sm_v2, 3))
```

```python
import jax
import jax.numpy as jnp
from jax import lax
from jax.experimental import pallas as pl
from jax.experimental.pallas import tpu as pltpu

N_DEV = 8
HEAD_DIM = 64
CHUNK = 128
ATT_BLK = 128
DILATIONS = (1, 4, 16)
CONV_WIDTH = 3
EPS = 1e-6
ADAM_LR = 0.001
ADAM_B1 = 0.9
ADAM_B2 = 0.999
ADAM_EPS = 1e-08
ADAM_WD = 0.01
ADAM_STEP = 10
MXU_DTYPE = jnp.bfloat16
F32 = jnp.float32
LANE = 128
VMEM_LIMIT_BYTES = 56 * 1024 * 1024
NEG_INF = float("-inf")


def _cparams(*sem):
    return pltpu.CompilerParams(dimension_semantics=sem, vmem_limit_bytes=VMEM_LIMIT_BYTES)


def _sds(shape, dtype):
    return jax.ShapeDtypeStruct(shape, dtype)


def _fit(n, tile):
    for t in range(min(tile, n) // LANE * LANE, 0, -LANE):
        if n % t == 0:
            return t
    return n


_HBM = pl.BlockSpec(memory_space=pltpu.HBM)
_SEM = pl.BlockSpec(memory_space=pltpu.SEMAPHORE)
_DATAFLOW = pltpu.SideEffectType.DATAFLOW_SIDE_EFFECTING
N_PEER = N_DEV - 1


def _mesh_pos():
    x, y, c = lax.axis_index("x"), lax.axis_index("y"), lax.axis_index("c")
    return x, y, c, 4 * x + 2 * y + c


OTHER_CHIPS = (4, 2, 6)
EXCHANGE_PEERS = dict(
    scatter=tuple(range(1, N_DEV)),
    gather_all=tuple(range(1, N_DEV)),
    gather=(1,) + OTHER_CHIPS,
    forward=OTHER_CHIPS)


def _remote_copies(src, land, send_sems, recv_sems, mode):
    x, y, c, me = _mesh_pos()
    copies = []
    for i, k in enumerate(EXCHANGE_PEERS[mode]):
        px = (1 - x) if (k & 4) else x
        py = (1 - y) if (k & 2) else y
        pc = (1 - c) if (k & 1) else c
        if mode == "scatter":
            src_ref, dst_ref, dev = src.at[4 * px + 2 * py + pc], land.at[i], (px, py, pc)
        elif mode == "forward":
            slot = 4 * px + 2 * py + c
            src_ref, dst_ref, dev = land.at[slot], land.at[slot], (x, y, 1 - c)
        else:
            src_ref, dst_ref, dev = src, land.at[me], (px, py, pc)
        copies.append(pltpu.make_async_remote_copy(
            src_ref=src_ref, dst_ref=dst_ref, send_sem=send_sems.at[i], recv_sem=recv_sems.at[i],
            device_id=dev, device_id_type=pl.DeviceIdType.MESH))
    return copies


def _own_in_place(src, me):
    land = lax.empty((N_DEV,) + src.shape, src.dtype)
    return lax.dynamic_update_slice(land, src[None], (me,) + (0,) * src.ndim)


def _exchange_start(srcs, lands, mode, name):
    n = len(lands)
    has_src = srcs is not None
    arrays = (list(srcs) if has_src else []) + list(lands)
    n_arr = len(arrays)
    n_copies = len(EXCHANGE_PEERS[mode])

    def body(*refs):
        src = refs[:n] if has_src else [None] * n
        land = refs[n_arr - n:n_arr]
        send, recv = refs[n_arr:n_arr + n], refs[n_arr + n:n_arr + 2 * n]
        token = refs[2 * n_arr + 2 * n]
        for t in range(n):
            for cp in _remote_copies(src[t], land[t], send[t], recv[t], mode):
                cp.start()
        token[...] = jnp.zeros_like(token)

    outs = pl.pallas_call(
        body, name=name,
        out_shape=([pltpu.SemaphoreType.DMA((n_copies,))] * (2 * n) + [pltpu.HBM(a.shape, a.dtype) for a in arrays]
                   + [_sds((8, LANE), F32)]),
        in_specs=[_HBM] * n_arr,
        out_specs=[_SEM] * (2 * n) + [_HBM] * n_arr + [pl.BlockSpec(memory_space=pltpu.VMEM)],
        input_output_aliases={i: 2 * n + i for i in range(n_arr)},
        compiler_params=pltpu.CompilerParams(has_side_effects=_DATAFLOW),
    )(*[pltpu.with_memory_space_constraint(a, pltpu.HBM) for a in arrays])
    thru = outs[2 * n:2 * n + n_arr]
    flights = [(outs[t], outs[n + t], thru[t] if has_src else None, thru[n_arr - n + t]) for t in range(n)]
    return flights, outs[2 * n + n_arr]


def _exchange_wait(flight, after, mode, name):
    send, recv, src, land = flight
    arrays = [land] if src is None else [src, land]
    n_arr = len(arrays)

    def body(*refs):
        src_ref = refs[0] if n_arr == 2 else None
        land_ref, send_ref, recv_ref = refs[n_arr - 1], refs[n_arr], refs[n_arr + 1]
        for cp in _remote_copies(src_ref, land_ref, send_ref, recv_ref, mode):
            cp.wait_send()
            cp.wait_recv()

    outs = pl.pallas_call(
        body, name=name, out_shape=[pltpu.HBM(a.shape, a.dtype) for a in arrays],
        in_specs=[_HBM] * n_arr + [_SEM, _SEM, pl.BlockSpec(memory_space=pl.ANY)], out_specs=[_HBM] * n_arr,
        input_output_aliases={i: i for i in range(n_arr)},
        compiler_params=pltpu.CompilerParams(has_side_effects=_DATAFLOW),
    )(*arrays, send, recv, after)
    return (None, outs[0]) if src is None else (outs[0], outs[1])


def _rmsnorm_fwd(x, g, name, dep=None, tr=1024):
    T, D = x.shape
    tr = _fit(T, tr)

    def body(x_ref, g_ref, *rest):
        h_ref, r_ref = rest[-2:]
        xv = x_ref[...]
        r = lax.rsqrt(jnp.mean(xv * xv, axis=-1, keepdims=True) + EPS)
        h_ref[...] = (xv * r * g_ref[...]).astype(h_ref.dtype)
        r_ref[...] = r

    in_specs = [pl.BlockSpec((tr, D), lambda i: (i, 0)), pl.BlockSpec((1, D), lambda i: (0, 0))]
    args = [x, g]
    if dep is not None:
        in_specs.append(pl.BlockSpec(dep.shape, lambda i: (0, 0)))
        args.append(dep)
    return pl.pallas_call(
        body, name=name, grid=(T // tr,),
        in_specs=in_specs,
        out_specs=[pl.BlockSpec((tr, D), lambda i: (i, 0)), pl.BlockSpec((tr, 1), lambda i: (i, 0))],
        out_shape=[_sds((T, D), MXU_DTYPE), _sds((T, 1), F32)],
        compiler_params=_cparams("parallel"),
    )(*args)


def _rmsnorm_bwd(dh, x, g, r, dres, name, tr=512):
    T, D = x.shape
    tr = _fit(T, tr)

    def body(dh_ref, x_ref, g_ref, r_ref, dres_ref, dx_ref, dxb_ref, dg_ref):
        @pl.when(pl.program_id(0) == 0)
        def _():
            dg_ref[...] = jnp.zeros_like(dg_ref)

        dh_v, xv, rv = dh_ref[...].astype(F32), x_ref[...], r_ref[...]
        gdy = dh_v * g_ref[...]
        mean_xg = jnp.mean(xv * gdy, axis=-1, keepdims=True)
        dx = dres_ref[...] + rv * gdy - xv * (rv * rv * rv) * mean_xg
        dx_ref[...] = dx
        dxb_ref[...] = dx.astype(dxb_ref.dtype)
        dg_ref[...] += jnp.sum(dh_v * xv * rv, axis=0, keepdims=True)

    row = lambda i: (i, 0)
    return pl.pallas_call(
        body, name=name, grid=(T // tr,),
        in_specs=[pl.BlockSpec((tr, D), row), pl.BlockSpec((tr, D), row), pl.BlockSpec((1, D), lambda i: (0, 0)),
                  pl.BlockSpec((tr, 1), row), pl.BlockSpec((tr, D), row)],
        out_specs=[pl.BlockSpec((tr, D), row), pl.BlockSpec((tr, D), row), pl.BlockSpec((1, D), lambda i: (0, 0))],
        out_shape=[_sds((T, D), F32), _sds((T, D), MXU_DTYPE), _sds((1, D), F32)],
        compiler_params=_cparams("arbitrary"),
    )(dh, x, g, r, dres)


def _mm_nn(a, b, name, out_dtype=F32, residual=None, relu2=False, dep=None, columns_first=False, tm=512, tn=512):
    M, K = a.shape
    grouped = b.ndim == 3
    N = b.shape[0] * b.shape[2] if grouped else b.shape[1]
    tm, tn = _fit(M, tm), _fit(b.shape[2] if grouped else N, tn)
    tile = (lambda j, i: (i, j)) if columns_first else (lambda i, j: (i, j))
    b_mode = pl.Buffered(1 if columns_first else 2)
    if grouped:
        per = b.shape[2] // tn
        b_spec = pl.BlockSpec((None, K, tn), lambda *g: (tile(*g)[1] // per, 0, tile(*g)[1] % per), pipeline_mode=b_mode)
    else:
        b_spec = pl.BlockSpec((K, tn), lambda *g: (0, tile(*g)[1]), pipeline_mode=b_mode)
    n_out = 2 if relu2 else 1

    def body(*refs):
        a_ref, b_ref = refs[0], refs[1]
        r_ref = refs[2] if residual is not None else None
        outs = refs[2 + (residual is not None) + (dep is not None):]
        acc = jnp.dot(a_ref[...], b_ref[...], preferred_element_type=F32)
        if r_ref is not None:
            acc = acc + r_ref[...]
        outs[0][...] = acc.astype(outs[0].dtype)
        if relu2:
            rl = jnp.maximum(acc, 0.0)
            outs[1][...] = (rl * rl).astype(outs[1].dtype)

    out_blk = pl.BlockSpec((tm, tn), lambda *g: tile(*g))
    in_specs = [pl.BlockSpec((tm, K), lambda *g: (tile(*g)[0], 0)), b_spec]
    args = [a, b]
    if residual is not None:
        in_specs.append(out_blk)
        args.append(residual)
    if dep is not None:
        in_specs.append(pl.BlockSpec(dep.shape, lambda *g: (0, 0)))
        args.append(dep)
    outs = pl.pallas_call(
        body, name=name, grid=(N // tn, M // tm) if columns_first else (M // tm, N // tn),
        in_specs=in_specs, out_specs=[out_blk] * n_out, out_shape=[_sds((M, N), out_dtype)] * n_out,
        compiler_params=_cparams("parallel", "parallel"),
    )(*args)
    return outs if relu2 else outs[0]


def _mm_nt(a, b, name, out_dtype=F32, relu2_pre=None, dep=None, tm=512, tn=512):
    M, K = a.shape
    grouped = b.ndim == 3
    N = b.shape[1] if grouped else b.shape[0]
    tm, tn = _fit(M, tm), _fit(N, tn)
    nt = (((1,), (1,)), ((), ()))
    if grouped:
        G, _, Kg = b.shape
        b_spec = pl.BlockSpec((G, tn, Kg), lambda i, j: (0, j, 0))
    else:
        b_spec = pl.BlockSpec((tn, K), lambda i, j: (j, 0))

    def body(*refs):
        a_ref, b_ref = refs[0], refs[1]
        p_ref = refs[2] if relu2_pre is not None else None
        out_ref = refs[2 + (relu2_pre is not None) + (dep is not None)]
        if grouped:
            acc = lax.dot_general(a_ref[:, 0:Kg], b_ref[0], nt, preferred_element_type=F32)
            for g in range(1, G):
                acc += lax.dot_general(a_ref[:, g * Kg:(g + 1) * Kg], b_ref[g], nt, preferred_element_type=F32)
        else:
            acc = lax.dot_general(a_ref[...], b_ref[...], nt, preferred_element_type=F32)
        if p_ref is not None:
            acc = acc * (2.0 * jnp.maximum(p_ref[...].astype(F32), 0.0))
        out_ref[...] = acc.astype(out_ref.dtype)

    out_blk = pl.BlockSpec((tm, tn), lambda i, j: (i, j))
    in_specs = [pl.BlockSpec((tm, K), lambda i, j: (i, 0)), b_spec]
    args = [a, b]
    if relu2_pre is not None:
        in_specs.append(out_blk)
        args.append(relu2_pre)
    if dep is not None:
        in_specs.append(pl.BlockSpec(dep.shape, lambda i, j: (0, 0)))
        args.append(dep)
    return pl.pallas_call(
        body, name=name, grid=(M // tm, N // tn),
        in_specs=in_specs, out_specs=out_blk, out_shape=_sds((M, N), out_dtype),
        compiler_params=_cparams("parallel", "parallel"),
    )(*args)


def _wgrad_wide_a(a, b, name, tm=512):
    T, M = a.shape
    N = b.shape[1]
    tm = _fit(M, tm)

    def body(a_ref, b_ref, out_ref):
        out_ref[...] = lax.dot_general(a_ref[...], b_ref[...], (((0,), (0,)), ((), ())),
                                       preferred_element_type=F32).astype(out_ref.dtype)

    return pl.pallas_call(
        body, name=name, grid=(M // tm,),
        in_specs=[pl.BlockSpec((T, tm), lambda i: (0, i)),
                  pl.BlockSpec((T, N), lambda i: (0, 0), pipeline_mode=pl.Buffered(1))],
        out_specs=pl.BlockSpec((tm, N), lambda i: (i, 0)), out_shape=_sds((M, N), MXU_DTYPE),
        compiler_params=_cparams("parallel"),
    )(a, b)


def _wgrad_wide_b(a, b, name, groups=None, tn=512, t_chunk=512):
    T, M = a.shape
    N = b.shape[1]
    tn = _fit(N if groups is None else N // groups, tn)
    t_chunk = _fit(T, t_chunk)

    def body(a_ref, b_ref, out_ref, at_ref):
        @pl.when(pl.program_id(0) == 0)
        def _():
            for c in range(0, T, t_chunk):
                at_ref[:, c:c + t_chunk] = a_ref[c:c + t_chunk, :].T

        out_ref[...] = jnp.dot(at_ref[...], b_ref[...], preferred_element_type=F32).astype(out_ref.dtype)

    if groups is None:
        out_spec = pl.BlockSpec((M, tn), lambda j: (0, j))
        out_shape = _sds((M, N), MXU_DTYPE)
    else:
        per = N // groups // tn
        out_spec = pl.BlockSpec((None, M, tn), lambda j: (j // per, 0, j % per))
        out_shape = _sds((groups, M, N // groups), MXU_DTYPE)
    return pl.pallas_call(
        body, name=name, grid=(N // tn,),
        in_specs=[pl.BlockSpec((T, M), lambda j: (0, 0), pipeline_mode=pl.Buffered(1)),
                  pl.BlockSpec((T, tn), lambda j: (0, j))],
        out_specs=out_spec, out_shape=out_shape,
        scratch_shapes=[pltpu.VMEM((M, T), MXU_DTYPE)],
        compiler_params=_cparams("arbitrary"),
    )(a, b)


def _loss_and_grad(y, target, name, tr=512):
    T, D = y.shape
    tr = _fit(T, tr)

    def body(y_ref, t_ref, loss_ref, dx_ref, dxb_ref):
        @pl.when(pl.program_id(0) == 0)
        def _():
            loss_ref[...] = jnp.zeros_like(loss_ref)

        err = y_ref[...] - t_ref[...]
        loss_ref[...] += 0.5 * jnp.sum(jnp.mean(err * err, axis=-1, keepdims=True), axis=0, keepdims=True)
        dx = err * (1.0 / D)
        dx_ref[...] = dx
        dxb_ref[...] = dx.astype(dxb_ref.dtype)

    row = lambda i: (i, 0)
    return pl.pallas_call(
        body, name=name, grid=(T // tr,),
        in_specs=[pl.BlockSpec((tr, D), row), pl.BlockSpec((tr, D), row)],
        out_specs=[pl.BlockSpec((8, LANE), lambda i: (0, 0)), pl.BlockSpec((tr, D), row), pl.BlockSpec((tr, D), row)],
        out_shape=[_sds((8, LANE), F32), _sds((T, D), F32), _sds((T, D), MXU_DTYPE)],
        compiler_params=_cparams("arbitrary"),
    )(y, target)


SGU_ROWS = 512


def _sgu_mixed(v, w_ref, b_ref, n_heads):
    parts = [jnp.dot(w_ref[h], v[:, h * HEAD_DIM:(h + 1) * HEAD_DIM], preferred_element_type=F32) for h in range(n_heads)]
    return jnp.concatenate(parts, axis=1) + b_ref[...]


def _sgu_fwd(p, w_tril, bmat, name):
    T = p.shape[0]
    H = w_tril.shape[0]
    AW = H * HEAD_DIM
    rows = _fit(T, SGU_ROWS)

    def body(u_ref, v_ref, w_ref, b_ref, y_ref):
        for c in range(0, rows, CHUNK):
            ch = pl.ds(c, CHUNK)
            mixed = _sgu_mixed(v_ref[ch, :].astype(MXU_DTYPE), w_ref, b_ref, H)
            y_ref[ch, :] = (u_ref[ch, :].astype(F32) * mixed).astype(y_ref.dtype)

    const3 = lambda c: (0, 0, 0)
    return pl.pallas_call(
        body, name=name, grid=(T // rows,),
        in_specs=[pl.BlockSpec((rows, AW), lambda c: (c, 0)), pl.BlockSpec((rows, AW), lambda c: (c, 1)),
                  pl.BlockSpec((H, CHUNK, CHUNK), const3), pl.BlockSpec((CHUNK, AW), lambda c: (0, 0))],
        out_specs=pl.BlockSpec((rows, AW), lambda c: (c, 0)),
        out_shape=_sds((T, AW), MXU_DTYPE),
        compiler_params=_cparams("parallel"),
    )(p, p, w_tril, bmat)


def _sgu_bwd(dymix, p, w_tril, w_tril_t, bmat, name):
    T = p.shape[0]
    H = w_tril.shape[0]
    AW = H * HEAD_DIM
    rows = _fit(T, SGU_ROWS)

    def body(dy_ref, u_ref, v_ref, w_ref, wt_ref, b_ref, du_ref, dv_ref, dw_ref, db_ref):
        @pl.when(pl.program_id(0) == 0)
        def _():
            dw_ref[...] = jnp.zeros_like(dw_ref)
            db_ref[...] = jnp.zeros_like(db_ref)

        for c in range(0, rows, CHUNK):
            ch = pl.ds(c, CHUNK)
            v = v_ref[ch, :].astype(MXU_DTYPE)
            dy = dy_ref[ch, :].astype(F32)
            du_ref[ch, :] = (dy * _sgu_mixed(v, w_ref, b_ref, H)).astype(du_ref.dtype)
            dm = dy * u_ref[ch, :].astype(F32)
            db_ref[...] += dm
            dm_c = dm.astype(MXU_DTYPE)
            dv = []
            for h in range(H):
                sl = slice(h * HEAD_DIM, (h + 1) * HEAD_DIM)
                dv.append(jnp.dot(wt_ref[h], dm_c[:, sl], preferred_element_type=F32))
                dw_ref[h] += lax.dot_general(dm_c[:, sl], v[:, sl], (((1,), (1,)), ((), ())), preferred_element_type=F32)
            dv_ref[ch, :] = jnp.concatenate(dv, axis=1).astype(dv_ref.dtype)

    const3 = lambda c: (0, 0, 0)
    blk = pl.BlockSpec((rows, AW), lambda c: (c, 0))
    return pl.pallas_call(
        body, name=name, grid=(T // rows,),
        in_specs=[blk, blk, pl.BlockSpec((rows, AW), lambda c: (c, 1)),
                  pl.BlockSpec((H, CHUNK, CHUNK), const3), pl.BlockSpec((H, CHUNK, CHUNK), const3),
                  pl.BlockSpec((CHUNK, AW), lambda c: (0, 0))],
        out_specs=[blk, blk, pl.BlockSpec((H, CHUNK, CHUNK), const3), pl.BlockSpec((CHUNK, AW), lambda c: (0, 0))],
        out_shape=[_sds((T, AW), MXU_DTYPE), _sds((T, AW), MXU_DTYPE), _sds((H, CHUNK, CHUNK), F32), _sds((CHUNK, AW), F32)],
        compiler_params=_cparams("arbitrary"),
    )(dymix, p, p, w_tril, w_tril_t, bmat)


def _shift_down(z, s, row):
    return jnp.where(row >= s, pltpu.roll(z, s, 0), 0.0)


def _shift_up(z, s, row, T):
    return jnp.where(row < T - s, pltpu.roll(z, T - s, 0), 0.0)


def _conv_fwd(p, w_conv, AW, name):
    T = p.shape[0]
    BW = w_conv.shape[1]
    nb = BW // LANE
    b0 = 2 * AW // LANE

    def body(b_ref, c_ref, x_ref, w_ref, y_ref):
        row = lax.broadcasted_iota(jnp.int32, (T, LANE), 0)
        z = c_ref[...].astype(F32) * x_ref[...].astype(F32)
        w0, w1, w2 = w_ref[0:1, :], w_ref[1:2, :], w_ref[2:3, :]
        conv = w2 * z + w1 * _shift_down(z, 1, row) + w0 * _shift_down(z, 2, row)
        y_ref[...] = (b_ref[...].astype(F32) * conv).astype(y_ref.dtype)

    return pl.pallas_call(
        body, name=name, grid=(nb,),
        in_specs=[pl.BlockSpec((T, LANE), lambda j: (0, b0 + j)), pl.BlockSpec((T, LANE), lambda j: (0, b0 + nb + j)),
                  pl.BlockSpec((T, LANE), lambda j: (0, b0 + 2 * nb + j)), pl.BlockSpec((CONV_WIDTH, LANE), lambda j: (0, j))],
        out_specs=pl.BlockSpec((T, LANE), lambda j: (0, j)),
        out_shape=_sds((T, BW), MXU_DTYPE),
        compiler_params=_cparams("parallel"),
    )(p, p, p, w_conv)


def _conv_bwd(dymix, p, w_conv, AW, name):
    T = p.shape[0]
    BW = w_conv.shape[1]
    nb = BW // LANE
    b0 = 2 * AW // LANE
    y0 = AW // LANE

    def body(dy_ref, b_ref, c_ref, x_ref, w_ref, db_ref, dc_ref, dxb_ref, dw_ref):
        row = lax.broadcasted_iota(jnp.int32, (T, LANE), 0)
        cv, xv, dy = c_ref[...].astype(F32), x_ref[...].astype(F32), dy_ref[...].astype(F32)
        w0, w1, w2 = w_ref[0:1, :], w_ref[1:2, :], w_ref[2:3, :]
        z = cv * xv
        z1 = _shift_down(z, 1, row)
        z2 = _shift_down(z, 2, row)
        conv = w2 * z + w1 * z1 + w0 * z2
        db_ref[...] = (dy * conv).astype(db_ref.dtype)
        dconv = dy * b_ref[...].astype(F32)
        dz = w2 * dconv + w1 * _shift_up(dconv, 1, row, T) + w0 * _shift_up(dconv, 2, row, T)
        dc_ref[...] = (dz * xv).astype(dc_ref.dtype)
        dxb_ref[...] = (dz * cv).astype(dxb_ref.dtype)
        dw_ref[0:1, :] = jnp.sum(dconv * z2, axis=0, keepdims=True)
        dw_ref[1:2, :] = jnp.sum(dconv * z1, axis=0, keepdims=True)
        dw_ref[2:3, :] = jnp.sum(dconv * z, axis=0, keepdims=True)

    col = lambda j: (0, j)
    return pl.pallas_call(
        body, name=name, grid=(nb,),
        in_specs=[pl.BlockSpec((T, LANE), lambda j: (0, y0 + j)),
                  pl.BlockSpec((T, LANE), lambda j: (0, b0 + j)), pl.BlockSpec((T, LANE), lambda j: (0, b0 + nb + j)),
                  pl.BlockSpec((T, LANE), lambda j: (0, b0 + 2 * nb + j)), pl.BlockSpec((CONV_WIDTH, LANE), col)],
        out_specs=[pl.BlockSpec((T, LANE), col)] * 3 + [pl.BlockSpec((CONV_WIDTH, LANE), col)],
        out_shape=[_sds((T, BW), MXU_DTYPE)] * 3 + [_sds((CONV_WIDTH, BW), F32)],
        compiler_params=_cparams("parallel"),
    )(dymix, p, p, p, w_conv)


def _head_sum(x, col_head, n_heads):
    out = jnp.zeros_like(x)
    for h in range(n_heads):
        sel = col_head == h
        out = jnp.where(sel, jnp.sum(jnp.where(sel, x, 0.0), axis=-1, keepdims=True), out)
    return out


def _same_head(width):
    assert width == 2 * HEAD_DIM
    return lax.broadcasted_iota(jnp.int32, (1, width), 1) < HEAD_DIM


def _head_sum2(x, first):
    s0 = jnp.sum(jnp.where(first, x, 0.0), axis=-1, keepdims=True)
    s1 = jnp.sum(jnp.where(first, 0.0, x), axis=-1, keepdims=True)
    return jnp.where(first, s0, s1)


def _head_norm(x, g, first):
    r = lax.rsqrt(_head_sum2(x * x, first) * (1.0 / HEAD_DIM) + EPS)
    return x * r * g, r


def _head_norm_bwd(dy, x, g, r, first):
    gdy = dy * g
    mean_xg = _head_sum2(x * gdy, first) * (1.0 / HEAD_DIM)
    return r * gdy - x * (r * r * r) * mean_xg, dy * x * r


ATT_SPAN_MIN = 512
ATT_FWD_UNROLL = 4
ATT_BWD_UNROLL = 4
HEADS_PER_LANES = LANE // HEAD_DIM


def _attn_geometry(T, d):
    m = max(1, ATT_SPAN_MIN // (ATT_BLK * d))
    return m, ATT_BLK * d * m, ATT_BLK * d, T // (ATT_BLK * d)


def _rows(ref, start, d):
    return ref[pl.ds(start, ATT_BLK, stride=d), :] if d > 1 else ref[pl.ds(start, ATT_BLK), :]


def _set_rows(ref, start, d, value):
    if d > 1:
        ref[pl.ds(start, ATT_BLK, stride=d), :] = value
    else:
        ref[pl.ds(start, ATT_BLK), :] = value


def _for_each_block(d, m, task, unroll):
    for j in range(m):
        if d == 1:
            task(0, j)
        else:
            lax.fori_loop(0, d, lambda r, carry, j=j: (task(r, j), carry)[1], 0, unroll=min(unroll, d))


def _head_slices():
    return [slice(h * HEAD_DIM, (h + 1) * HEAD_DIM) for h in range(HEADS_PER_LANES)]


def _qk_norm_fwd(p, gains, q_start, PW, name, tr=1024):
    T = p.shape[0]
    tr = _fit(T, tr)
    n_norm = gains.shape[1] // PW
    n = n_norm * 3 // 2
    c0 = q_start // PW

    def body(*refs):
        x_refs, g_ref, out_ref = refs[:n], refs[n], refs[n + 1]
        first = _same_head(LANE)
        for i in range(n):
            for c in range(0, PW, LANE):
                lo = i * PW + c
                x = x_refs[i][:, c:c + LANE].astype(F32)
                out_ref[:, lo:lo + LANE] = _head_norm(x, g_ref[:, lo:lo + LANE], first)[0] if i < n_norm else x

    return pl.pallas_call(
        body, name=name, grid=(T // tr,),
        in_specs=[pl.BlockSpec((tr, PW), lambda i, j=j: (i, c0 + j)) for j in range(n)]
        + [pl.BlockSpec((1, n_norm * PW), lambda i: (0, 0))],
        out_specs=pl.BlockSpec((tr, n * PW), lambda i: (i, 0)), out_shape=_sds((T, n * PW), F32),
        compiler_params=_cparams("parallel"),
    )(*([p] * n), gains)


def _qk_norm_bwd(dns, p, gains, q_start, PW, name, tr=1024):
    T = p.shape[0]
    tr = _fit(T, tr)
    n = len(dns)
    c0 = q_start // PW

    def body(*refs):
        d_refs, x_refs, g_ref, out_ref, acc_ref = refs[:n], refs[n:2 * n], refs[2 * n], refs[2 * n + 1], refs[2 * n + 2]

        @pl.when(pl.program_id(0) == 0)
        def _():
            acc_ref[...] = jnp.zeros_like(acc_ref)

        first = _same_head(LANE)
        for i in range(n):
            for c in range(0, PW, LANE):
                lo = i * PW + c
                x, gv = x_refs[i][:, c:c + LANE].astype(F32), g_ref[:, lo:lo + LANE]
                _, r = _head_norm(x, gv, first)
                dx, g_part = _head_norm_bwd(d_refs[i][:, c:c + LANE], x, gv, r, first)
                out_ref[:, lo:lo + LANE] = dx.astype(out_ref.dtype)
                acc_ref[0:1, lo:lo + LANE] += jnp.sum(g_part, axis=0, keepdims=True)

    return pl.pallas_call(
        body, name=name, grid=(T // tr,),
        in_specs=[pl.BlockSpec((tr, PW), lambda i: (i, 0))] * n
        + [pl.BlockSpec((tr, PW), lambda i, j=j: (i, c0 + j)) for j in range(n)]
        + [pl.BlockSpec((1, n * PW), lambda i: (0, 0))],
        out_specs=[pl.BlockSpec((tr, n * PW), lambda i: (i, 0)), pl.BlockSpec((8, n * PW), lambda i: (0, 0))],
        out_shape=[_sds((T, n * PW), MXU_DTYPE), _sds((8, n * PW), F32)],
        compiler_params=_cparams("arbitrary"),
    )(*dns, *([p] * n), gains)


def _attn_fwd(qkv, g, d, PW, name):
    T = qkv.shape[0]
    B, W = ATT_BLK, LANE
    m, span, group, _ = _attn_geometry(T, d)
    c_q = g * PW // W
    c_k, c_v = c_q + 3 * PW // W, c_q + 6 * PW // W
    scale = HEAD_DIM ** -0.5

    def body(q_ref, k_ref, v_ref, kp_ref, vp_ref, o_ref, lse_ref):
        n = pl.program_id(1)
        qi = lax.broadcasted_iota(jnp.int32, (B, 2 * B), 0)
        kj = lax.broadcasted_iota(jnp.int32, (B, 2 * B), 1)
        band = (kj >= qi) & (kj <= qi + B)

        def task(r, j):
            cur = j * group + r
            if j == 0:
                kp, vp = _rows(kp_ref, r, d), _rows(vp_ref, r, d)
            else:
                kp, vp = _rows(k_ref, cur - group, d), _rows(v_ref, cur - group, d)
            mask = band & ((n * m + j > 0) | (kj >= B))
            qn = _rows(q_ref, cur, d).astype(MXU_DTYPE)
            kn = jnp.concatenate([kp, _rows(k_ref, cur, d)], axis=0).astype(MXU_DTYPE)
            vcat = jnp.concatenate([vp, _rows(v_ref, cur, d)], axis=0).astype(MXU_DTYPE)
            o_parts, lse_parts = [], []
            for sl in _head_slices():
                s = lax.dot_general(qn[:, sl], kn[:, sl], (((1,), (1,)), ((), ())), preferred_element_type=F32) * scale
                s = jnp.where(mask, s, NEG_INF)
                mx = jnp.max(s, axis=-1, keepdims=True)
                e = jnp.exp(s - mx)
                den = jnp.sum(e, axis=-1, keepdims=True)
                o_parts.append(jnp.dot(e.astype(MXU_DTYPE), vcat[:, sl], preferred_element_type=F32) / den)
                lse_parts.append(jnp.broadcast_to(mx + jnp.log(den), (B, HEAD_DIM)))
            _set_rows(o_ref, cur, d, jnp.concatenate(o_parts, axis=1))
            _set_rows(lse_ref, cur, d, jnp.concatenate(lse_parts, axis=1))

        _for_each_block(d, m, task, ATT_FWD_UNROLL)

    main = lambda c0: pl.BlockSpec((span, W), lambda hp, n: (n, c0 + hp))
    prev = lambda c0: pl.BlockSpec((group, W), lambda hp, n: (jnp.maximum(n * m - 1, 0), c0 + hp))
    out_blk = pl.BlockSpec((span, W), lambda hp, n: (n, hp))
    return pl.pallas_call(
        body, name=name, grid=(PW // W, T // span),
        in_specs=[main(c_q), main(c_k), main(c_v), prev(c_k), prev(c_v)],
        out_specs=[out_blk, out_blk],
        out_shape=[_sds((T, PW), F32), _sds((T, PW), F32)],
        compiler_params=_cparams("parallel", "parallel"),
    )(qkv, qkv, qkv, qkv, qkv)


def _attn_bwd(qkv, lse, do, corr, g, d, PW, name):
    T = qkv.shape[0]
    B, W = ATT_BLK, LANE
    m, span, group, n_blocks = _attn_geometry(T, d)
    c_q = g * PW // W
    c_k, c_v = c_q + 3 * PW // W, c_q + 6 * PW // W
    scale = HEAD_DIM ** -0.5
    nt = (((1,), (1,)), ((), ()))
    tn = (((0,), (0,)), ((), ()))

    def body(q_ref, k_ref, v_ref, do_ref, l_ref, c_ref, kp_ref, vp_ref, qx_ref, dox_ref, lx_ref, cx_ref,
             dq_ref, dk_ref, dv_ref):
        n = pl.program_id(1)
        i1 = lax.broadcasted_iota(jnp.int32, (B, B), 0)
        j1 = lax.broadcasted_iota(jnp.int32, (B, B), 1)
        i2 = lax.broadcasted_iota(jnp.int32, (2 * B, B), 0)
        j2 = lax.broadcasted_iota(jnp.int32, (2 * B, B), 1)

        def task(r, j):
            cur = j * group + r
            blk = n * m + j
            q_c, k_c, v_c = _rows(q_ref, cur, d), _rows(k_ref, cur, d), _rows(v_ref, cur, d)
            do_c, l_c, c_c = _rows(do_ref, cur, d), _rows(l_ref, cur, d), _rows(c_ref, cur, d)
            if j == 0:
                k_p, v_p = _rows(kp_ref, r, d), _rows(vp_ref, r, d)
            else:
                k_p, v_p = _rows(k_ref, cur - group, d), _rows(v_ref, cur - group, d)
            if j == m - 1:
                nxt = [_rows(ref, r, d) for ref in (qx_ref, dox_ref, lx_ref, cx_ref)]
            else:
                nxt = [_rows(ref, cur + group, d) for ref in (q_ref, do_ref, l_ref, c_ref)]
            q_x, do_x, l_x, c_x = nxt
            kn_c, kn_p, v_c, v_p = (a.astype(MXU_DTYPE) for a in (k_c, k_p, v_c, v_p))
            qn_c = q_c.astype(MXU_DTYPE)
            qn_cat = jnp.concatenate([qn_c, q_x.astype(MXU_DTYPE)], axis=0)
            do_cb = do_c.astype(MXU_DTYPE)
            do_cat = jnp.concatenate([do_cb, do_x.astype(MXU_DTYPE)], axis=0)
            l_cat = jnp.concatenate([l_c, l_x], axis=0)
            c_cat = jnp.concatenate([c_c, c_x], axis=0)
            mask_p = (j1 >= i1) & (blk > 0)
            mask_c = ((i2 < B) & (j2 <= i2)) | ((i2 >= B) & (j2 >= i2 - B) & (blk + 1 < n_blocks))
            dqn, dkn, dv = [], [], []
            for h, sl in enumerate(_head_slices()):
                lane = slice(h * HEAD_DIM, h * HEAD_DIM + 1)
                s_p = lax.dot_general(qn_c[:, sl], kn_p[:, sl], nt, preferred_element_type=F32) * scale
                pr_p = jnp.where(mask_p, jnp.exp(s_p - l_c[:, lane]), 0.0)
                dp_p = lax.dot_general(do_cb[:, sl], v_p[:, sl], nt, preferred_element_type=F32)
                ds_p = (pr_p * (dp_p + c_c[:, lane]) * scale).astype(MXU_DTYPE)
                s_c = lax.dot_general(qn_cat[:, sl], kn_c[:, sl], nt, preferred_element_type=F32) * scale
                pr_c = jnp.where(mask_c, jnp.exp(s_c - l_cat[:, lane]), 0.0)
                dp_c = lax.dot_general(do_cat[:, sl], v_c[:, sl], nt, preferred_element_type=F32)
                ds_c = (pr_c * (dp_c + c_cat[:, lane]) * scale).astype(MXU_DTYPE)
                dqn.append(jnp.dot(ds_p, kn_p[:, sl], preferred_element_type=F32)
                           + jnp.dot(ds_c[:B], kn_c[:, sl], preferred_element_type=F32))
                dkn.append(lax.dot_general(ds_c, qn_cat[:, sl], tn, preferred_element_type=F32))
                dv.append(lax.dot_general(pr_c.astype(MXU_DTYPE), do_cat[:, sl], tn, preferred_element_type=F32))
            _set_rows(dq_ref, cur, d, jnp.concatenate(dqn, axis=1))
            _set_rows(dk_ref, cur, d, jnp.concatenate(dkn, axis=1))
            _set_rows(dv_ref, cur, d, jnp.concatenate(dv, axis=1))

        _for_each_block(d, m, task, ATT_BWD_UNROLL)

    main = lambda c0: pl.BlockSpec((span, W), lambda hp, n: (n, c0 + hp))
    prev = lambda c0: pl.BlockSpec((group, W), lambda hp, n: (jnp.maximum(n * m - 1, 0), c0 + hp))
    nxt = lambda c0: pl.BlockSpec((group, W), lambda hp, n: (jnp.minimum((n + 1) * m, n_blocks - 1), c0 + hp))
    own = pl.BlockSpec((span, W), lambda hp, n: (n, hp))
    return pl.pallas_call(
        body, name=name, grid=(PW // W, T // span),
        in_specs=[main(c_q), main(c_k), main(c_v), main(0), main(0), main(0), prev(c_k), prev(c_v),
                  nxt(c_q), nxt(0), nxt(0), nxt(0)],
        out_specs=[own, own, own],
        out_shape=[_sds((T, PW), F32)] * 3,
        compiler_params=_cparams("parallel", "parallel"),
    )(qkv, qkv, qkv, do, lse, corr, qkv, qkv, qkv, do, lse, corr)


def _softmax3(lses):
    mx = jnp.maximum(jnp.maximum(lses[0], lses[1]), lses[2])
    ex = [jnp.exp(l - mx) for l in lses]
    inv = 1.0 / (ex[0] + ex[1] + ex[2])
    return [e * inv for e in ex]


def _mix_fwd(os_, lses, name, tr=1024):
    T, PW = os_[0].shape
    tr = _fit(T, tr)

    def body(o0, o1, o2, l0, l1, l2, y_ref):
        alpha = _softmax3([l0[...], l1[...], l2[...]])
        for g, o_ref in enumerate((o0, o1, o2)):
            y_ref[:, g * PW:(g + 1) * PW] = (o_ref[...] * alpha[g]).astype(y_ref.dtype)

    blk = pl.BlockSpec((tr, PW), lambda i: (i, 0))
    return pl.pallas_call(
        body, name=name, grid=(T // tr,),
        in_specs=[blk] * 6, out_specs=pl.BlockSpec((tr, 3 * PW), lambda i: (i, 0)),
        out_shape=_sds((T, 3 * PW), MXU_DTYPE),
        compiler_params=_cparams("parallel"),
    )(*os_, *lses)


def _mix_bwd(dymix, os_, lses, c_start, name, tr=1024):
    T, PW = os_[0].shape
    tr = _fit(T, tr)
    HP = PW // HEAD_DIM
    c0 = c_start // PW

    def body(d0, d1, d2, o0, o1, o2, l0, l1, l2, do0, do1, do2, dl0, dl1, dl2):
        col_head = lax.broadcasted_iota(jnp.int32, (tr, PW), 1) // HEAD_DIM
        alpha = _softmax3([l0[...], l1[...], l2[...]])
        dys = [d0[...].astype(F32), d1[...].astype(F32), d2[...].astype(F32)]
        dots = [_head_sum(dy * o_ref[...], col_head, HP) for dy, o_ref in zip(dys, (o0, o1, o2))]
        mean_dot = alpha[0] * dots[0] + alpha[1] * dots[1] + alpha[2] * dots[2]
        for g, (do_ref, dl_ref) in enumerate(((do0, dl0), (do1, dl1), (do2, dl2))):
            do_ref[...] = dys[g] * alpha[g]
            dl_ref[...] = -alpha[g] * mean_dot

    blk = pl.BlockSpec((tr, PW), lambda i: (i, 0))
    dy_specs = [pl.BlockSpec((tr, PW), lambda i, g=g: (i, c0 + g)) for g in range(3)]
    outs = pl.pallas_call(
        body, name=name, grid=(T // tr,),
        in_specs=dy_specs + [blk] * 6, out_specs=[blk] * 6,
        out_shape=[_sds((T, PW), F32)] * 6,
        compiler_params=_cparams("parallel"),
    )(dymix, dymix, dymix, *os_, *lses)
    return outs[:3], outs[3:]


def _adamw_math(w, g, m, v):
    m2 = ADAM_B1 * m + (1.0 - ADAM_B1) * g
    v2 = ADAM_B2 * v + (1.0 - ADAM_B2) * (g * g)
    m_hat = m2 / (1.0 - ADAM_B1 ** ADAM_STEP)
    v_hat = v2 / (1.0 - ADAM_B2 ** ADAM_STEP)
    delta = -ADAM_LR * (m_hat / (jnp.sqrt(v_hat) + ADAM_EPS) + ADAM_WD * w)
    return delta, m2, v2


def _adamw_layer(layer, w, m, v, own, landed, me, prev, name, tr=256):
    _, R, C = w.shape
    tr = next(t for t in range(min(tr, R) // 16 * 16, 0, -16) if R % t == 0)

    def body(me_ref, w_ref, m_ref, v_ref, own_ref, land_ref, *rest):
        g_ref, d_ref, m2_ref, v2_ref = rest[-4:]
        g = own_ref[...].astype(F32)
        for j in range(N_PEER):
            g = g + land_ref[j].astype(F32)
        delta, m2, v2 = _adamw_math(w_ref[...], g, m_ref[...], v_ref[...])
        g_ref[...] = g
        d_ref[...] = delta
        m2_ref[...] = m2
        v2_ref[...] = v2

    lay = pl.BlockSpec((None, tr, C), lambda i, me_ref: (layer, i, 0))
    in_specs = [lay, lay, lay, pl.BlockSpec((None, tr, C), lambda i, me_ref: (me_ref[0], i, 0)),
                pl.BlockSpec((N_PEER, tr, C), lambda i, me_ref: (0, i, 0))]
    args = [me, w, m, v, own, landed]
    aliases = {}
    if prev is not None:
        in_specs += [pl.BlockSpec(memory_space=pl.ANY)] * 4
        args += list(prev)
        aliases = {6 + i: i for i in range(4)}
    return pl.pallas_call(
        body, name=name,
        grid_spec=pltpu.PrefetchScalarGridSpec(num_scalar_prefetch=1, grid=(R // tr,), in_specs=in_specs, out_specs=[lay] * 4),
        out_shape=[_sds(w.shape, F32)] * 4,
        input_output_aliases=aliases,
        compiler_params=_cparams("parallel"),
    )(*args)


def _sum_parts(parts, name):
    _, R, C = parts.shape

    def body(p_ref, out_ref):
        g = p_ref[0]
        for j in range(1, N_DEV):
            g = g + p_ref[j]
        out_ref[...] = g

    return pl.pallas_call(
        body, name=name, grid=(1,),
        in_specs=[pl.BlockSpec((N_DEV, R, C), lambda i: (0, 0, 0))], out_specs=pl.BlockSpec((R, C), lambda i: (0, 0)),
        out_shape=_sds((R, C), F32), compiler_params=_cparams("arbitrary"),
    )(parts)


def _adamw_small(ws, gs, ms, vs, name):
    n = len(ws)

    def body(*refs):
        ins, outs = refs[:4 * n], refs[4 * n:]
        for i in range(n):
            delta, m2, v2 = _adamw_math(ins[i][...], ins[n + i][...], ins[2 * n + i][...], ins[3 * n + i][...])
            outs[i][...] = delta
            outs[n + i][...] = m2
            outs[2 * n + i][...] = v2

    outs = pl.pallas_call(
        body, name=name, out_shape=[_sds(w.shape, F32) for w in ws] * 3,
    )(*ws, *gs, *ms, *vs)
    return outs[:n], outs[n:2 * n], outs[2 * n:]


def _pack(arrays, rows_multiple=8):
    flat = []
    for a in arrays:
        a = a.reshape(-1).astype(F32)
        flat.append(jnp.pad(a, (0, (-a.shape[0]) % LANE)))
    flat = jnp.concatenate(flat)
    flat = jnp.pad(flat, (0, (-flat.shape[0]) % (LANE * rows_multiple)))
    return flat.reshape(-1, LANE)


def _unpack(packed, shapes):
    flat = packed.reshape(-1)
    out, off = [], 0
    for s in shapes:
        size = 1
        for dim in s:
            size *= dim
        out.append(flat[off:off + size].reshape(s))
        off += size + (-size) % LANE
    return out


def _layer_fwd(x, wts, getw, dims, dep=None):
    AW, BW, PW, DP = dims["AW"], dims["BW"], dims["PW"], dims["DP"]
    q_start = 2 * AW + 3 * BW
    h, r1 = _rmsnorm_fwd(x, wts["attn_norm"], "rmsnorm_fwd", dep=dep)
    p = _mm_nt(h, getw("w_in", h), "proj_in", out_dtype=MXU_DTYPE, tm=1024, tn=1408)
    y_a = _sgu_fwd(p, wts["sgu_tril"], wts["sgu_bmat"], "sgu_fwd")
    y_b = _conv_fwd(p, getw("conv_w", y_a), AW, "conv_fwd")
    qkn = _qk_norm_fwd(p, wts["qk_gain"], q_start, PW, "qk_norm_fwd")
    os_, lses = [], []
    for g, d in enumerate(DILATIONS):
        o, lse = _attn_fwd(qkn, g, d, PW, "attn_fwd_%d" % d)
        os_.append(o)
        lses.append(lse)
    y_c = _mix_fwd(os_, lses, "mix_fwd")
    ymix = jnp.concatenate([y_a, y_b, y_c], axis=1)
    x1 = _mm_nn(ymix, getw("w_out", ymix), "proj_out", residual=x, tm=1024, tn=1024)
    h2, r2 = _rmsnorm_fwd(x1, wts["mlp_norm"], "rmsnorm_fwd")
    a, hid = _mm_nn(h2, getw("w_mlp_in", h2), "mlp_in", out_dtype=MXU_DTYPE, relu2=True, tm=1024, tn=1024)
    x2 = _mm_nn(hid, getw("w_mlp_out", hid), "mlp_out", residual=x1, columns_first=True, tm=512, tn=1024)
    saved = dict(x=x, h=h, r1=r1, p=p, qkn=qkn, os=os_, lses=lses, ymix=ymix, x1=x1, h2=h2, r2=r2, a=a, hid=hid)
    return x2, saved


def _layer_bwd(dx, dxb, wts, getw, scatter, saved, dims):
    AW, BW, PW, DP = dims["AW"], dims["BW"], dims["PW"], dims["DP"]
    q_start = 2 * AW + 3 * BW
    D = dx.shape[1]
    g_w2 = _wgrad_wide_a(saved["hid"], dxb, "mlp_out_wgrad")
    token = scatter("w_mlp_out", g_w2.reshape(N_DEV, -1, D))
    da = _mm_nt(dxb, getw("w_mlp_out", None), "mlp_out_dgrad", out_dtype=MXU_DTYPE, relu2_pre=saved["a"], dep=token,
                tm=1024, tn=1024)
    g_w1 = _wgrad_wide_b(saved["h2"], da, "mlp_in_wgrad", groups=N_DEV)
    token = scatter("w_mlp_in", g_w1)
    dh2 = _mm_nt(da, getw("w_mlp_in", None), "mlp_in_dgrad", out_dtype=MXU_DTYPE, dep=token, tm=1024, tn=512)
    dx1, dx1b, g_mlp_norm = _rmsnorm_bwd(dh2, saved["x1"], wts["mlp_norm"], saved["r2"], dx, "rmsnorm_bwd")
    g_wout = _wgrad_wide_b(saved["ymix"], dx1b, "proj_out_wgrad")
    token = scatter("w_out", g_wout.reshape(N_DEV, -1, D))
    dymix = _mm_nt(dx1b, getw("w_out", None), "proj_out_dgrad", out_dtype=MXU_DTYPE, dep=token, tm=1024, tn=1024)
    p = saved["p"]
    du, dv, g_sgu_w, g_sgu_bmat = _sgu_bwd(dymix, p, wts["sgu_tril"], wts["sgu_tril_t"], wts["sgu_bmat"], "sgu_bwd")
    d_b, d_c, d_xb, g_conv = _conv_bwd(dymix, p, getw("conv_w", None), AW, "conv_bwd")
    dos, corrs = _mix_bwd(dymix, saved["os"], saved["lses"], AW + BW, "mix_bwd")
    dqns, dkns, dvs = [], [], []
    for g, d in enumerate(DILATIONS):
        dqn, dkn, dvv = _attn_bwd(saved["qkn"], saved["lses"][g], dos[g], corrs[g], g, d, PW, "attn_bwd_%d" % d)
        dqns.append(dqn)
        dkns.append(dkn)
        dvs.append(dvv.astype(MXU_DTYPE))
    dqk, g_qk = _qk_norm_bwd(dqns + dkns, p, wts["qk_gain"], q_start, PW, "qk_norm_bwd")
    g_q, g_k = (part.reshape(-1, HEAD_DIM).sum(0) for part in jnp.split(g_qk[0], 2))
    dp = jnp.concatenate([du, dv, d_b, d_c, d_xb, dqk] + dvs, axis=1)
    g_win_t = _wgrad_wide_a(dp, saved["h"], "proj_in_wgrad")
    token = scatter("w_in", g_win_t.reshape(N_DEV, DP // N_DEV, D))
    dh = _mm_nn(dp, getw("w_in", None), "proj_in_dgrad", out_dtype=MXU_DTYPE, dep=token, tm=1024, tn=1024)
    dx0, dx0b, g_attn_norm = _rmsnorm_bwd(dh, saved["x"], wts["attn_norm"], saved["r1"], dx1, "rmsnorm_bwd")
    H = AW // HEAD_DIM
    tril = jnp.tril(jnp.ones((CHUNK, CHUNK), F32))
    small = [g_attn_norm.reshape(-1), g_sgu_w * tril, g_sgu_bmat.reshape(CHUNK, H, HEAD_DIM).sum(-1).T,
             g_conv, g_q, g_k, g_mlp_norm.reshape(-1)]
    return dx0, dx0b, small


def kernel(x, attn_norm, w_in, sgu_w, sgu_b, conv_w, q_norm, k_norm, w_out, mlp_norm, w_mlp_in, w_mlp_out, loss_target, m_attn_norm, m_w_in, m_sgu_w, m_sgu_b, m_conv_w, m_q_norm, m_k_norm, m_w_out, m_mlp_norm, m_w_mlp_in, m_w_mlp_out, v_attn_norm, v_w_in, v_sgu_w, v_sgu_b, v_conv_w, v_q_norm, v_k_norm, v_w_out, v_mlp_norm, v_w_mlp_in, v_w_mlp_out):
    n_layers = attn_norm.shape[0]
    T, D = x.shape[1], x.shape[2]
    H = sgu_w.shape[1]
    AW = H * HEAD_DIM
    BW = conv_w.shape[2] * N_DEV
    DP = w_in.shape[2] * N_DEV
    DMIX = w_out.shape[1] * N_DEV
    DFF = w_mlp_in.shape[2] * N_DEV
    PW = (DMIX - AW - BW) // 3
    HP = PW // HEAD_DIM
    dims = dict(AW=AW, BW=BW, PW=PW, DP=DP)
    me = 4 * lax.axis_index("x") + 2 * lax.axis_index("y") + lax.axis_index("c")

    big_names = ("w_in", "w_out", "w_mlp_in", "w_mlp_out")
    tr_in = lambda a: jnp.swapaxes(a, 1, 2)
    big_w = dict(zip(big_names, (tr_in(w_in), w_out, w_mlp_in, w_mlp_out)))
    big_m = dict(zip(big_names, (tr_in(m_w_in), m_w_out, m_w_mlp_in, m_w_mlp_out)))
    big_v = dict(zip(big_names, (tr_in(v_w_in), v_w_out, v_w_mlp_in, v_w_mlp_out)))

    keys = []
    for l in range(n_layers):
        keys += [(l, nm) for nm in big_names]
    keys.insert(1, (0, "conv_w"))
    first_src = big_w[keys[0][1]][keys[0][0]].astype(MXU_DTYPE)
    first_flights, first_token = _exchange_start([first_src], [_own_in_place(first_src, me)], "gather", name="gather_start_first")
    zero = first_token[0, 0]
    srcs = [_pack([conv_w]) + zero if nm == "conv_w" else (big_w[nm][l] + zero).astype(MXU_DTYPE) for l, nm in keys[1:]]
    flights, gather_token = _exchange_start(srcs, [_own_in_place(s, me) for s in srcs], "gather", name="gather_start")
    arriving = dict(zip(keys, first_flights + flights))
    forwarding = {}
    relayout = dict(
        w_in=lambda g: g.reshape(DP, D), w_out=lambda g: g.reshape(DMIX, D),
        w_mlp_in=lambda g: g, w_mlp_out=lambda g: g.reshape(DFF, D),
        conv_w=lambda g: jnp.stack([_unpack(g[j], [conv_w.shape])[0] for j in range(N_DEV)], axis=2).reshape(
            n_layers, CONV_WIDTH, BW))
    gathered = {}

    def forward(key, after):
        _, land = _exchange_wait(arriving[key], after, "gather", name="gather_arrive_%d_%s" % key)
        fl, token = _exchange_start(None, [land], "forward", name="gather_forward_%d_%s" % key)
        forwarding[key] = fl[0]
        return token

    def weight_getter(l):
        def getw(nm, after):
            key = (0, nm) if nm == "conv_w" else (l, nm)
            if key not in gathered:
                ahead = keys[keys.index(key):][:2]
                for k in ahead:
                    if k not in forwarding:
                        after = forward(k, after)
                _, land = _exchange_wait(forwarding[key], after, "forward", name="gather_wait_%d_%s" % key)
                gathered[key] = relayout[nm](land)
            return gathered[key][l] if nm == "conv_w" else gathered[key]
        return getw

    tril = jnp.tril(jnp.ones((CHUNK, CHUNK), F32))
    layers = []
    for l in range(n_layers):
        w_tril = sgu_w[l] * tril
        layers.append(dict(
            attn_norm=attn_norm[l][None], mlp_norm=mlp_norm[l][None],
            sgu_tril=w_tril.astype(MXU_DTYPE), sgu_tril_t=w_tril.transpose(0, 2, 1).astype(MXU_DTYPE),
            sgu_bmat=jnp.repeat(sgu_b[l].T, HEAD_DIM, axis=1),
            qk_gain=jnp.concatenate([jnp.tile(q_norm[l], 3 * HP), jnp.tile(k_norm[l], 3 * HP)])[None]))

    xs = x[0]
    saved = []
    for l in range(n_layers):
        xs, sv = _layer_fwd(xs, layers[l], weight_getter(l), dims, dep=gather_token if l == 0 else None)
        saved.append(sv)
    loss_blk, dx, dxb = _loss_and_grad(xs, loss_target[0], "loss")
    loss = lax.psum(loss_blk[0, 0], ("x", "y", "c"))

    scattering = {}

    def scatter_starter(l):
        def scatter(nm, partials):
            land = lax.empty((N_PEER,) + partials.shape[1:], partials.dtype)
            fl, tok = _exchange_start([partials], [land], "scatter", name="scatter_start_%d_%s" % (l, nm))
            scattering[(l, nm)] = fl[0]
            return tok
        return scatter

    small = [None] * n_layers
    for l in reversed(range(n_layers)):
        dx, dxb, small[l] = _layer_bwd(dx, dxb, layers[l], weight_getter(l), scatter_starter(l), saved[l], dims)

    small_shapes = [s.shape for s in small[0]]
    small_src = [_pack([s for l in range(n_layers) for s in small[l]])]
    small_flights, small_token = _exchange_start(small_src, [_own_in_place(s, me) for s in small_src], "gather_all",
                                                 name="small_start")
    grad_x = dx[None]

    me1 = me.astype(jnp.int32).reshape(1)
    res = {nm: None for nm in big_names}
    after = small_token
    for l in reversed(range(n_layers)):
        for nm in reversed(big_names):
            own, landed = _exchange_wait(scattering[(l, nm)], after, "scatter", name="scatter_wait_%d_%s" % (l, nm))
            res[nm] = _adamw_layer(l, big_w[nm], big_m[nm], big_v[nm], own, landed, me1, res[nm], "adamw_" + nm)
            after = res[nm][0]
    res["w_in"] = [tr_in(a) for a in res["w_in"]]
    big_out = [res[nm] for nm in big_names]

    _, gathered_small = _exchange_wait(small_flights[0], after, "gather_all", name="small_wait")
    summed = _unpack(_sum_parts(gathered_small, "sum_small"), small_shapes * n_layers)
    ns = len(small_shapes)
    g_small = [jnp.stack([summed[l * ns + i] for l in range(n_layers)]) for i in range(ns)]
    g_attn_norm, g_sgu_w, g_sgu_b, g_conv_full, g_q, g_k, g_mlp_norm = g_small
    cs = conv_w.shape[2]
    g_conv = lax.dynamic_slice_in_dim(g_conv_full, me * cs, cs, axis=2)
    sm_w = (attn_norm, sgu_w, sgu_b, conv_w, q_norm, k_norm, mlp_norm)
    sm_m = (m_attn_norm, m_sgu_w, m_sgu_b, m_conv_w, m_q_norm, m_k_norm, m_mlp_norm)
    sm_v = (v_attn_norm, v_sgu_w, v_sgu_b, v_conv_w, v_q_norm, v_k_norm, v_mlp_norm)
    sm_g = (g_attn_norm, g_sgu_w, g_sgu_b, g_conv, g_q, g_k, g_mlp_norm)
    sm_delta, sm_m2, sm_v2 = _adamw_small(sm_w, sm_g, sm_m, sm_v, "adamw_small")

    def ordered(small_list, big_kind):
        b = [big_out[i][big_kind] for i in range(4)]
        return [small_list[0], b[0], small_list[1], small_list[2], small_list[3], small_list[4], small_list[5],
                b[1], small_list[6], b[2], b[3]]

    return (loss, grad_x, *ordered(list(sm_g), 0), *ordered(sm_delta, 1), *ordered(sm_m2, 2), *ordered(sm_v2, 3))
```

```python
import jax
import jax.numpy as jnp
from jax import lax
from jax.experimental import pallas as pl
from jax.experimental.pallas import tpu as pltpu

N_DEV = 8
HEAD_DIM = 64
CHUNK = 128
ATT_BLK = 128
DILATIONS = (1, 4, 16)
CONV_WIDTH = 3
EPS = 1e-6
ADAM_LR = 0.001
ADAM_B1 = 0.9
ADAM_B2 = 0.999
ADAM_EPS = 1e-08
ADAM_WD = 0.01
ADAM_STEP = 10
MXU_DTYPE = jnp.bfloat16
F32 = jnp.float32
LANE = 128
VMEM_LIMIT_BYTES = 56 * 1024 * 1024
NEG_INF = float("-inf")


def _cparams(*sem):
    return pltpu.CompilerParams(dimension_semantics=sem, vmem_limit_bytes=VMEM_LIMIT_BYTES)


def _sds(shape, dtype):
    return jax.ShapeDtypeStruct(shape, dtype)


def _fit(n, tile):
    for t in range(min(tile, n) // LANE * LANE, 0, -LANE):
        if n % t == 0:
            return t
    return n


_HBM = pl.BlockSpec(memory_space=pltpu.HBM)
_SEM = pl.BlockSpec(memory_space=pltpu.SEMAPHORE)
_DATAFLOW = pltpu.SideEffectType.DATAFLOW_SIDE_EFFECTING
N_PEER = N_DEV - 1


def _mesh_pos():
    x, y, c = lax.axis_index("x"), lax.axis_index("y"), lax.axis_index("c")
    return x, y, c, 4 * x + 2 * y + c


OTHER_CHIPS = (4, 2, 6)
EXCHANGE_PEERS = dict(
    scatter=tuple(range(1, N_DEV)),
    gather_all=tuple(range(1, N_DEV)),
    gather=(1,) + OTHER_CHIPS,
    forward=OTHER_CHIPS)


def _remote_copies(src, land, send_sems, recv_sems, mode):
    x, y, c, me = _mesh_pos()
    copies = []
    for i, k in enumerate(EXCHANGE_PEERS[mode]):
        px = (1 - x) if (k & 4) else x
        py = (1 - y) if (k & 2) else y
        pc = (1 - c) if (k & 1) else c
        if mode == "scatter":
            src_ref, dst_ref, dev = src.at[4 * px + 2 * py + pc], land.at[i], (px, py, pc)
        elif mode == "forward":
            slot = 4 * px + 2 * py + c
            src_ref, dst_ref, dev = land.at[slot], land.at[slot], (x, y, 1 - c)
        else:
            src_ref, dst_ref, dev = src, land.at[me], (px, py, pc)
        copies.append(pltpu.make_async_remote_copy(
            src_ref=src_ref, dst_ref=dst_ref, send_sem=send_sems.at[i], recv_sem=recv_sems.at[i],
            device_id=dev, device_id_type=pl.DeviceIdType.MESH))
    return copies


def _own_in_place(src, me):
    land = lax.empty((N_DEV,) + src.shape, src.dtype)
    return lax.dynamic_update_slice(land, src[None], (me,) + (0,) * src.ndim)


def _exchange_start(srcs, lands, mode, name):
    n = len(lands)
    has_src = srcs is not None
    arrays = (list(srcs) if has_src else []) + list(lands)
    n_arr = len(arrays)
    n_copies = len(EXCHANGE_PEERS[mode])

    def body(*refs):
        src = refs[:n] if has_src else [None] * n
        land = refs[n_arr - n:n_arr]
        send, recv = refs[n_arr:n_arr + n], refs[n_arr + n:n_arr + 2 * n]
        token = refs[2 * n_arr + 2 * n]
        for t in range(n):
            for cp in _remote_copies(src[t], land[t], send[t], recv[t], mode):
                cp.start()
        token[...] = jnp.zeros_like(token)

    outs = pl.pallas_call(
        body, name=name,
        out_shape=([pltpu.SemaphoreType.DMA((n_copies,))] * (2 * n) + [pltpu.HBM(a.shape, a.dtype) for a in arrays]
                   + [_sds((8, LANE), F32)]),
        in_specs=[_HBM] * n_arr,
        out_specs=[_SEM] * (2 * n) + [_HBM] * n_arr + [pl.BlockSpec(memory_space=pltpu.VMEM)],
        input_output_aliases={i: 2 * n + i for i in range(n_arr)},
        compiler_params=pltpu.CompilerParams(has_side_effects=_DATAFLOW),
    )(*[pltpu.with_memory_space_constraint(a, pltpu.HBM) for a in arrays])
    thru = outs[2 * n:2 * n + n_arr]
    flights = [(outs[t], outs[n + t], thru[t] if has_src else None, thru[n_arr - n + t]) for t in range(n)]
    return flights, outs[2 * n + n_arr]


def _exchange_wait(flight, after, mode, name):
    send, recv, src, land = flight
    arrays = [land] if src is None else [src, land]
    n_arr = len(arrays)

    def body(*refs):
        src_ref = refs[0] if n_arr == 2 else None
        land_ref, send_ref, recv_ref = refs[n_arr - 1], refs[n_arr], refs[n_arr + 1]
        for cp in _remote_copies(src_ref, land_ref, send_ref, recv_ref, mode):
            cp.wait_send()
            cp.wait_recv()

    outs = pl.pallas_call(
        body, name=name, out_shape=[pltpu.HBM(a.shape, a.dtype) for a in arrays],
        in_specs=[_HBM] * n_arr + [_SEM, _SEM, pl.BlockSpec(memory_space=pl.ANY)], out_specs=[_HBM] * n_arr,
        input_output_aliases={i: i for i in range(n_arr)},
        compiler_params=pltpu.CompilerParams(has_side_effects=_DATAFLOW),
    )(*arrays, send, recv, after)
    return (None, outs[0]) if src is None else (outs[0], outs[1])


def _rmsnorm_fwd(x, g, name, dep=None, tr=1024):
    T, D = x.shape
    tr = _fit(T, tr)

    def body(x_ref, g_ref, *rest):
        h_ref, r_ref = rest[-2:]
        xv = x_ref[...]
        r = lax.rsqrt(jnp.mean(xv * xv, axis=-1, keepdims=True) + EPS)
        h_ref[...] = (xv * r * g_ref[...]).astype(h_ref.dtype)
        r_ref[...] = r

    in_specs = [pl.BlockSpec((tr, D), lambda i: (i, 0)), pl.BlockSpec((1, D), lambda i: (0, 0))]
    args = [x, g]
    if dep is not None:
        in_specs.append(pl.BlockSpec(dep.shape, lambda i: (0, 0)))
        args.append(dep)
    return pl.pallas_call(
        body, name=name, grid=(T // tr,),
        in_specs=in_specs,
        out_specs=[pl.BlockSpec((tr, D), lambda i: (i, 0)), pl.BlockSpec((tr, 1), lambda i: (i, 0))],
        out_shape=[_sds((T, D), MXU_DTYPE), _sds((T, 1), F32)],
        compiler_params=_cparams("parallel"),
    )(*args)


def _rmsnorm_bwd(dh, x, g, r, dres, name, tr=512):
    T, D = x.shape
    tr = _fit(T, tr)

    def body(dh_ref, x_ref, g_ref, r_ref, dres_ref, dx_ref, dxb_ref, dg_ref):
        @pl.when(pl.program_id(0) == 0)
        def _():
            dg_ref[...] = jnp.zeros_like(dg_ref)

        dh_v, xv, rv = dh_ref[...].astype(F32), x_ref[...], r_ref[...]
        gdy = dh_v * g_ref[...]
        mean_xg = jnp.mean(xv * gdy, axis=-1, keepdims=True)
        dx = dres_ref[...] + rv * gdy - xv * (rv * rv * rv) * mean_xg
        dx_ref[...] = dx
        dxb_ref[...] = dx.astype(dxb_ref.dtype)
        dg_ref[...] += jnp.sum(dh_v * xv * rv, axis=0, keepdims=True)

    row = lambda i: (i, 0)
    return pl.pallas_call(
        body, name=name, grid=(T // tr,),
        in_specs=[pl.BlockSpec((tr, D), row), pl.BlockSpec((tr, D), row), pl.BlockSpec((1, D), lambda i: (0, 0)),
                  pl.BlockSpec((tr, 1), row), pl.BlockSpec((tr, D), row)],
        out_specs=[pl.BlockSpec((tr, D), row), pl.BlockSpec((tr, D), row), pl.BlockSpec((1, D), lambda i: (0, 0))],
        out_shape=[_sds((T, D), F32), _sds((T, D), MXU_DTYPE), _sds((1, D), F32)],
        compiler_params=_cparams("arbitrary"),
    )(dh, x, g, r, dres)


def _mm_nn(a, b, name, out_dtype=F32, residual=None, relu2=False, dep=None, columns_first=False, tm=512, tn=512):
    M, K = a.shape
    grouped = b.ndim == 3
    N = b.shape[0] * b.shape[2] if grouped else b.shape[1]
    tm, tn = _fit(M, tm), _fit(b.shape[2] if grouped else N, tn)
    tile = (lambda j, i: (i, j)) if columns_first else (lambda i, j: (i, j))
    b_mode = pl.Buffered(1 if columns_first else 2)
    if grouped:
        per = b.shape[2] // tn
        b_spec = pl.BlockSpec((None, K, tn), lambda *g: (tile(*g)[1] // per, 0, tile(*g)[1] % per), pipeline_mode=b_mode)
    else:
        b_spec = pl.BlockSpec((K, tn), lambda *g: (0, tile(*g)[1]), pipeline_mode=b_mode)
    n_out = 2 if relu2 else 1

    def body(*refs):
        a_ref, b_ref = refs[0], refs[1]
        r_ref = refs[2] if residual is not None else None
        outs = refs[2 + (residual is not None) + (dep is not None):]
        acc = jnp.dot(a_ref[...], b_ref[...], preferred_element_type=F32)
        if r_ref is not None:
            acc = acc + r_ref[...]
        outs[0][...] = acc.astype(outs[0].dtype)
        if relu2:
            rl = jnp.maximum(acc, 0.0)
            outs[1][...] = (rl * rl).astype(outs[1].dtype)

    out_blk = pl.BlockSpec((tm, tn), lambda *g: tile(*g))
    in_specs = [pl.BlockSpec((tm, K), lambda *g: (tile(*g)[0], 0)), b_spec]
    args = [a, b]
    if residual is not None:
        in_specs.append(out_blk)
        args.append(residual)
    if dep is not None:
        in_specs.append(pl.BlockSpec(dep.shape, lambda *g: (0, 0)))
        args.append(dep)
    outs = pl.pallas_call(
        body, name=name, grid=(N // tn, M // tm) if columns_first else (M // tm, N // tn),
        in_specs=in_specs, out_specs=[out_blk] * n_out, out_shape=[_sds((M, N), out_dtype)] * n_out,
        compiler_params=_cparams("parallel", "parallel"),
    )(*args)
    return outs if relu2 else outs[0]


def _mm_residual_norm(a, b, residual, g, name, tm=512):
    M, K = a.shape
    N = b.shape[1]
    tm = _fit(M, tm)

    def body(a_ref, b_ref, res_ref, g_ref, x_ref, h_ref, r_ref):
        xv = jnp.dot(a_ref[...], b_ref[...], preferred_element_type=F32) + res_ref[...]
        x_ref[...] = xv
        r = lax.rsqrt(jnp.mean(xv * xv, axis=-1, keepdims=True) + EPS)
        h_ref[...] = (xv * r * g_ref[...]).astype(h_ref.dtype)
        r_ref[...] = r

    row = pl.BlockSpec((tm, N), lambda i: (i, 0))
    return pl.pallas_call(
        body, name=name, grid=(M // tm,),
        in_specs=[pl.BlockSpec((tm, K), lambda i: (i, 0)),
                  pl.BlockSpec((K, N), lambda i: (0, 0), pipeline_mode=pl.Buffered(1)), row,
                  pl.BlockSpec((1, N), lambda i: (0, 0))],
        out_specs=[row, row, pl.BlockSpec((tm, 1), lambda i: (i, 0))],
        out_shape=[_sds((M, N), F32), _sds((M, N), MXU_DTYPE), _sds((M, 1), F32)],
        compiler_params=_cparams("parallel"),
    )(a, b, residual, g)


def _mm_residual_loss(a, b, residual, target, name, tm=512, tn=1024):
    M, K = a.shape
    N = b.shape[1]
    tm, tn = _fit(M, tm), _fit(N, tn)

    def body(a_ref, b_ref, res_ref, t_ref, loss_ref, dx_ref, dxb_ref):
        @pl.when((pl.program_id(0) == 0) & (pl.program_id(1) == 0))
        def _():
            loss_ref[...] = jnp.zeros_like(loss_ref)

        err = jnp.dot(a_ref[...], b_ref[...], preferred_element_type=F32) + res_ref[...] - t_ref[...]
        sq = jnp.sum(jnp.sum(err * err, axis=-1, keepdims=True), axis=0, keepdims=True)
        loss_ref[...] += (0.5 / N) * sq
        dx = err * (1.0 / N)
        dx_ref[...] = dx
        dxb_ref[...] = dx.astype(dxb_ref.dtype)

    blk = pl.BlockSpec((tm, tn), lambda j, i: (i, j))
    return pl.pallas_call(
        body, name=name, grid=(N // tn, M // tm),
        in_specs=[pl.BlockSpec((tm, K), lambda j, i: (i, 0)),
                  pl.BlockSpec((K, tn), lambda j, i: (0, j), pipeline_mode=pl.Buffered(1)), blk, blk],
        out_specs=[pl.BlockSpec((8, LANE), lambda j, i: (0, 0)), blk, blk],
        out_shape=[_sds((8, LANE), F32), _sds((M, N), F32), _sds((M, N), MXU_DTYPE)],
        compiler_params=_cparams("arbitrary", "arbitrary"),
    )(a, b, residual, target)


def _mm_nt(a, b, name, out_dtype=F32, relu2_pre=None, dep=None, tm=512, tn=512):
    M, K = a.shape
    grouped = b.ndim == 3
    N = b.shape[1] if grouped else b.shape[0]
    tm, tn = _fit(M, tm), _fit(N, tn)
    nt = (((1,), (1,)), ((), ()))
    if grouped:
        G, _, Kg = b.shape
        b_spec = pl.BlockSpec((G, tn, Kg), lambda i, j: (0, j, 0))
    else:
        b_spec = pl.BlockSpec((tn, K), lambda i, j: (j, 0))

    def body(*refs):
        a_ref, b_ref = refs[0], refs[1]
        p_ref = refs[2] if relu2_pre is not None else None
        out_ref = refs[2 + (relu2_pre is not None) + (dep is not None)]
        if grouped:
            acc = lax.dot_general(a_ref[:, 0:Kg], b_ref[0], nt, preferred_element_type=F32)
            for g in range(1, G):
                acc += lax.dot_general(a_ref[:, g * Kg:(g + 1) * Kg], b_ref[g], nt, preferred_element_type=F32)
        else:
            acc = lax.dot_general(a_ref[...], b_ref[...], nt, preferred_element_type=F32)
        if p_ref is not None:
            acc = acc * (2.0 * jnp.maximum(p_ref[...].astype(F32), 0.0))
        out_ref[...] = acc.astype(out_ref.dtype)

    out_blk = pl.BlockSpec((tm, tn), lambda i, j: (i, j))
    in_specs = [pl.BlockSpec((tm, K), lambda i, j: (i, 0)), b_spec]
    args = [a, b]
    if relu2_pre is not None:
        in_specs.append(out_blk)
        args.append(relu2_pre)
    if dep is not None:
        in_specs.append(pl.BlockSpec(dep.shape, lambda i, j: (0, 0)))
        args.append(dep)
    return pl.pallas_call(
        body, name=name, grid=(M // tm, N // tn),
        in_specs=in_specs, out_specs=out_blk, out_shape=_sds((M, N), out_dtype),
        compiler_params=_cparams("parallel", "parallel"),
    )(*args)


def _wgrad_wide_a(a, b, name, tm=512):
    T, M = a.shape
    N = b.shape[1]
    tm = _fit(M, tm)

    def body(a_ref, b_ref, out_ref):
        out_ref[...] = lax.dot_general(a_ref[...], b_ref[...], (((0,), (0,)), ((), ())),
                                       preferred_element_type=F32).astype(out_ref.dtype)

    return pl.pallas_call(
        body, name=name, grid=(M // tm,),
        in_specs=[pl.BlockSpec((T, tm), lambda i: (0, i)),
                  pl.BlockSpec((T, N), lambda i: (0, 0), pipeline_mode=pl.Buffered(1))],
        out_specs=pl.BlockSpec((tm, N), lambda i: (i, 0)), out_shape=_sds((M, N), MXU_DTYPE),
        compiler_params=_cparams("parallel"),
    )(a, b)


def _wgrad_wide_b(a, b, name, groups=None, tn=512, t_chunk=512):
    T, M = a.shape
    N = b.shape[1]
    tn = _fit(N if groups is None else N // groups, tn)
    t_chunk = _fit(T, t_chunk)

    def body(a_ref, b_ref, out_ref, at_ref):
        @pl.when(pl.program_id(0) == 0)
        def _():
            for c in range(0, T, t_chunk):
                at_ref[:, c:c + t_chunk] = a_ref[c:c + t_chunk, :].T

        out_ref[...] = jnp.dot(at_ref[...], b_ref[...], preferred_element_type=F32).astype(out_ref.dtype)

    if groups is None:
        out_spec = pl.BlockSpec((M, tn), lambda j: (0, j))
        out_shape = _sds((M, N), MXU_DTYPE)
    else:
        per = N // groups // tn
        out_spec = pl.BlockSpec((None, M, tn), lambda j: (j // per, 0, j % per))
        out_shape = _sds((groups, M, N // groups), MXU_DTYPE)
    return pl.pallas_call(
        body, name=name, grid=(N // tn,),
        in_specs=[pl.BlockSpec((T, M), lambda j: (0, 0), pipeline_mode=pl.Buffered(1)),
                  pl.BlockSpec((T, tn), lambda j: (0, j))],
        out_specs=out_spec, out_shape=out_shape,
        scratch_shapes=[pltpu.VMEM((M, T), MXU_DTYPE)],
        compiler_params=_cparams("arbitrary"),
    )(a, b)


SGU_ROWS = 512


def _sgu_mixed(v, w_ref, b_ref, n_heads):
    parts = [jnp.dot(w_ref[h], v[:, h * HEAD_DIM:(h + 1) * HEAD_DIM], preferred_element_type=F32) for h in range(n_heads)]
    return jnp.concatenate(parts, axis=1) + b_ref[...]


def _sgu_fwd(p, w_tril, bmat, name):
    T = p.shape[0]
    H = w_tril.shape[0]
    AW = H * HEAD_DIM
    rows = _fit(T, SGU_ROWS)

    def body(u_ref, v_ref, w_ref, b_ref, y_ref):
        for c in range(0, rows, CHUNK):
            ch = pl.ds(c, CHUNK)
            mixed = _sgu_mixed(v_ref[ch, :].astype(MXU_DTYPE), w_ref, b_ref, H)
            y_ref[ch, :] = (u_ref[ch, :].astype(F32) * mixed).astype(y_ref.dtype)

    const3 = lambda c: (0, 0, 0)
    return pl.pallas_call(
        body, name=name, grid=(T // rows,),
        in_specs=[pl.BlockSpec((rows, AW), lambda c: (c, 0)), pl.BlockSpec((rows, AW), lambda c: (c, 1)),
                  pl.BlockSpec((H, CHUNK, CHUNK), const3), pl.BlockSpec((CHUNK, AW), lambda c: (0, 0))],
        out_specs=pl.BlockSpec((rows, AW), lambda c: (c, 0)),
        out_shape=_sds((T, AW), MXU_DTYPE),
        compiler_params=_cparams("parallel"),
    )(p, p, w_tril, bmat)


def _sgu_bwd(dymix, p, w_tril, w_tril_t, bmat, name):
    T = p.shape[0]
    H = w_tril.shape[0]
    AW = H * HEAD_DIM
    rows = _fit(T, SGU_ROWS)

    def body(dy_ref, u_ref, v_ref, w_ref, wt_ref, b_ref, du_ref, dv_ref, dw_ref, db_ref):
        @pl.when(pl.program_id(0) == 0)
        def _():
            dw_ref[...] = jnp.zeros_like(dw_ref)
            db_ref[...] = jnp.zeros_like(db_ref)

        for c in range(0, rows, CHUNK):
            ch = pl.ds(c, CHUNK)
            v = v_ref[ch, :].astype(MXU_DTYPE)
            dy = dy_ref[ch, :].astype(F32)
            du_ref[ch, :] = (dy * _sgu_mixed(v, w_ref, b_ref, H)).astype(du_ref.dtype)
            dm = dy * u_ref[ch, :].astype(F32)
            db_ref[...] += dm
            dm_c = dm.astype(MXU_DTYPE)
            dv = []
            for h in range(H):
                sl = slice(h * HEAD_DIM, (h + 1) * HEAD_DIM)
                dv.append(jnp.dot(wt_ref[h], dm_c[:, sl], preferred_element_type=F32))
                dw_ref[h] += lax.dot_general(dm_c[:, sl], v[:, sl], (((1,), (1,)), ((), ())), preferred_element_type=F32)
            dv_ref[ch, :] = jnp.concatenate(dv, axis=1).astype(dv_ref.dtype)

    const3 = lambda c: (0, 0, 0)
    blk = pl.BlockSpec((rows, AW), lambda c: (c, 0))
    return pl.pallas_call(
        body, name=name, grid=(T // rows,),
        in_specs=[blk, blk, pl.BlockSpec((rows, AW), lambda c: (c, 1)),
                  pl.BlockSpec((H, CHUNK, CHUNK), const3), pl.BlockSpec((H, CHUNK, CHUNK), const3),
                  pl.BlockSpec((CHUNK, AW), lambda c: (0, 0))],
        out_specs=[blk, blk, pl.BlockSpec((H, CHUNK, CHUNK), const3), pl.BlockSpec((CHUNK, AW), lambda c: (0, 0))],
        out_shape=[_sds((T, AW), MXU_DTYPE), _sds((T, AW), MXU_DTYPE), _sds((H, CHUNK, CHUNK), F32), _sds((CHUNK, AW), F32)],
        compiler_params=_cparams("arbitrary"),
    )(dymix, p, p, w_tril, w_tril_t, bmat)


def _shift_down(z, s, row):
    return jnp.where(row >= s, pltpu.roll(z, s, 0), 0.0)


def _shift_up(z, s, row, T):
    return jnp.where(row < T - s, pltpu.roll(z, T - s, 0), 0.0)


def _conv_fwd(p, w_conv, AW, name):
    T = p.shape[0]
    BW = w_conv.shape[1]
    nb = BW // LANE
    b0 = 2 * AW // LANE

    def body(b_ref, c_ref, x_ref, w_ref, y_ref):
        row = lax.broadcasted_iota(jnp.int32, (T, LANE), 0)
        z = c_ref[...].astype(F32) * x_ref[...].astype(F32)
        w0, w1, w2 = w_ref[0:1, :], w_ref[1:2, :], w_ref[2:3, :]
        conv = w2 * z + w1 * _shift_down(z, 1, row) + w0 * _shift_down(z, 2, row)
        y_ref[...] = (b_ref[...].astype(F32) * conv).astype(y_ref.dtype)

    return pl.pallas_call(
        body, name=name, grid=(nb,),
        in_specs=[pl.BlockSpec((T, LANE), lambda j: (0, b0 + j)), pl.BlockSpec((T, LANE), lambda j: (0, b0 + nb + j)),
                  pl.BlockSpec((T, LANE), lambda j: (0, b0 + 2 * nb + j)), pl.BlockSpec((CONV_WIDTH, LANE), lambda j: (0, j))],
        out_specs=pl.BlockSpec((T, LANE), lambda j: (0, j)),
        out_shape=_sds((T, BW), MXU_DTYPE),
        compiler_params=_cparams("parallel"),
    )(p, p, p, w_conv)


def _conv_bwd(dymix, p, w_conv, AW, name):
    T = p.shape[0]
    BW = w_conv.shape[1]
    nb = BW // LANE
    b0 = 2 * AW // LANE
    y0 = AW // LANE

    def body(dy_ref, b_ref, c_ref, x_ref, w_ref, db_ref, dc_ref, dxb_ref, dw_ref):
        row = lax.broadcasted_iota(jnp.int32, (T, LANE), 0)
        cv, xv, dy = c_ref[...].astype(F32), x_ref[...].astype(F32), dy_ref[...].astype(F32)
        w0, w1, w2 = w_ref[0:1, :], w_ref[1:2, :], w_ref[2:3, :]
        z = cv * xv
        z1 = _shift_down(z, 1, row)
        z2 = _shift_down(z, 2, row)
        conv = w2 * z + w1 * z1 + w0 * z2
        db_ref[...] = (dy * conv).astype(db_ref.dtype)
        dconv = dy * b_ref[...].astype(F32)
        dz = w2 * dconv + w1 * _shift_up(dconv, 1, row, T) + w0 * _shift_up(dconv, 2, row, T)
        dc_ref[...] = (dz * xv).astype(dc_ref.dtype)
        dxb_ref[...] = (dz * cv).astype(dxb_ref.dtype)
        dw_ref[0:1, :] = jnp.sum(dconv * z2, axis=0, keepdims=True)
        dw_ref[1:2, :] = jnp.sum(dconv * z1, axis=0, keepdims=True)
        dw_ref[2:3, :] = jnp.sum(dconv * z, axis=0, keepdims=True)

    col = lambda j: (0, j)
    return pl.pallas_call(
        body, name=name, grid=(nb,),
        in_specs=[pl.BlockSpec((T, LANE), lambda j: (0, y0 + j)),
                  pl.BlockSpec((T, LANE), lambda j: (0, b0 + j)), pl.BlockSpec((T, LANE), lambda j: (0, b0 + nb + j)),
                  pl.BlockSpec((T, LANE), lambda j: (0, b0 + 2 * nb + j)), pl.BlockSpec((CONV_WIDTH, LANE), col)],
        out_specs=[pl.BlockSpec((T, LANE), col)] * 3 + [pl.BlockSpec((CONV_WIDTH, LANE), col)],
        out_shape=[_sds((T, BW), MXU_DTYPE)] * 3 + [_sds((CONV_WIDTH, BW), F32)],
        compiler_params=_cparams("parallel"),
    )(dymix, p, p, p, w_conv)


def _head_sum(x, col_head, n_heads):
    out = jnp.zeros_like(x)
    for h in range(n_heads):
        sel = col_head == h
        out = jnp.where(sel, jnp.sum(jnp.where(sel, x, 0.0), axis=-1, keepdims=True), out)
    return out


def _same_head(width):
    assert width == 2 * HEAD_DIM
    return lax.broadcasted_iota(jnp.int32, (1, width), 1) < HEAD_DIM


def _head_sum2(x, first):
    s0 = jnp.sum(jnp.where(first, x, 0.0), axis=-1, keepdims=True)
    s1 = jnp.sum(jnp.where(first, 0.0, x), axis=-1, keepdims=True)
    return jnp.where(first, s0, s1)


def _head_norm(x, g, first):
    r = lax.rsqrt(_head_sum2(x * x, first) * (1.0 / HEAD_DIM) + EPS)
    return x * r * g, r


def _head_norm_bwd(dy, x, g, r, first):
    gdy = dy * g
    mean_xg = _head_sum2(x * gdy, first) * (1.0 / HEAD_DIM)
    return r * gdy - x * (r * r * r) * mean_xg, dy * x * r


ATT_SPAN_MIN = 512
ATT_FWD_UNROLL = 4
ATT_BWD_UNROLL = 4
HEADS_PER_LANES = LANE // HEAD_DIM


def _attn_geometry(T, d):
    m = max(1, ATT_SPAN_MIN // (ATT_BLK * d))
    return m, ATT_BLK * d * m, ATT_BLK * d, T // (ATT_BLK * d)


def _rows(ref, start, d):
    return ref[pl.ds(start, ATT_BLK, stride=d), :] if d > 1 else ref[pl.ds(start, ATT_BLK), :]


def _set_rows(ref, start, d, value):
    if d > 1:
        ref[pl.ds(start, ATT_BLK, stride=d), :] = value
    else:
        ref[pl.ds(start, ATT_BLK), :] = value


def _for_each_block(d, m, task, unroll):
    for j in range(m):
        if d == 1:
            task(0, j)
        else:
            lax.fori_loop(0, d, lambda r, carry, j=j: (task(r, j), carry)[1], 0, unroll=min(unroll, d))


def _head_slices():
    return [slice(h * HEAD_DIM, (h + 1) * HEAD_DIM) for h in range(HEADS_PER_LANES)]


def _qk_norm_fwd(p, gains, q_start, PW, name, tr=1024):
    T = p.shape[0]
    tr = _fit(T, tr)
    n_norm = gains.shape[1] // PW
    n = n_norm * 3 // 2
    c0 = q_start // PW

    def body(*refs):
        x_refs, g_ref, out_ref = refs[:n], refs[n], refs[n + 1]
        first = _same_head(LANE)
        for i in range(n):
            for c in range(0, PW, LANE):
                lo = i * PW + c
                x = x_refs[i][:, c:c + LANE].astype(F32)
                out_ref[:, lo:lo + LANE] = _head_norm(x, g_ref[:, lo:lo + LANE], first)[0] if i < n_norm else x

    return pl.pallas_call(
        body, name=name, grid=(T // tr,),
        in_specs=[pl.BlockSpec((tr, PW), lambda i, j=j: (i, c0 + j)) for j in range(n)]
        + [pl.BlockSpec((1, n_norm * PW), lambda i: (0, 0))],
        out_specs=pl.BlockSpec((tr, n * PW), lambda i: (i, 0)), out_shape=_sds((T, n * PW), F32),
        compiler_params=_cparams("parallel"),
    )(*([p] * n), gains)


def _qk_norm_bwd(dns, p, gains, q_start, PW, name, tr=1024):
    T = p.shape[0]
    tr = _fit(T, tr)
    n = len(dns)
    c0 = q_start // PW

    def body(*refs):
        d_refs, x_refs, g_ref, out_ref, acc_ref = refs[:n], refs[n:2 * n], refs[2 * n], refs[2 * n + 1], refs[2 * n + 2]

        @pl.when(pl.program_id(0) == 0)
        def _():
            acc_ref[...] = jnp.zeros_like(acc_ref)

        first = _same_head(LANE)
        for i in range(n):
            for c in range(0, PW, LANE):
                lo = i * PW + c
                x, gv = x_refs[i][:, c:c + LANE].astype(F32), g_ref[:, lo:lo + LANE]
                _, r = _head_norm(x, gv, first)
                dx, g_part = _head_norm_bwd(d_refs[i][:, c:c + LANE], x, gv, r, first)
                out_ref[:, lo:lo + LANE] = dx.astype(out_ref.dtype)
                acc_ref[0:1, lo:lo + LANE] += jnp.sum(g_part, axis=0, keepdims=True)

    return pl.pallas_call(
        body, name=name, grid=(T // tr,),
        in_specs=[pl.BlockSpec((tr, PW), lambda i: (i, 0))] * n
        + [pl.BlockSpec((tr, PW), lambda i, j=j: (i, c0 + j)) for j in range(n)]
        + [pl.BlockSpec((1, n * PW), lambda i: (0, 0))],
        out_specs=[pl.BlockSpec((tr, n * PW), lambda i: (i, 0)), pl.BlockSpec((8, n * PW), lambda i: (0, 0))],
        out_shape=[_sds((T, n * PW), MXU_DTYPE), _sds((8, n * PW), F32)],
        compiler_params=_cparams("arbitrary"),
    )(*dns, *([p] * n), gains)


def _attn_fwd(qkv, g, d, PW, name):
    T = qkv.shape[0]
    B, W = ATT_BLK, LANE
    m, span, group, _ = _attn_geometry(T, d)
    c_q = g * PW // W
    c_k, c_v = c_q + 3 * PW // W, c_q + 6 * PW // W
    scale = HEAD_DIM ** -0.5

    def body(q_ref, k_ref, v_ref, kp_ref, vp_ref, o_ref, lse_ref):
        n = pl.program_id(1)
        qi = lax.broadcasted_iota(jnp.int32, (B, 2 * B), 0)
        kj = lax.broadcasted_iota(jnp.int32, (B, 2 * B), 1)
        band = (kj >= qi) & (kj <= qi + B)

        def task(r, j):
            cur = j * group + r
            if j == 0:
                kp, vp = _rows(kp_ref, r, d), _rows(vp_ref, r, d)
            else:
                kp, vp = _rows(k_ref, cur - group, d), _rows(v_ref, cur - group, d)
            mask = band & ((n * m + j > 0) | (kj >= B))
            qn = _rows(q_ref, cur, d).astype(MXU_DTYPE)
            kn = jnp.concatenate([kp, _rows(k_ref, cur, d)], axis=0).astype(MXU_DTYPE)
            vcat = jnp.concatenate([vp, _rows(v_ref, cur, d)], axis=0).astype(MXU_DTYPE)
            o_parts, lse_parts = [], []
            for sl in _head_slices():
                s = lax.dot_general(qn[:, sl], kn[:, sl], (((1,), (1,)), ((), ())), preferred_element_type=F32) * scale
                s = jnp.where(mask, s, NEG_INF)
                mx = jnp.max(s, axis=-1, keepdims=True)
                e = jnp.exp(s - mx)
                den = jnp.sum(e, axis=-1, keepdims=True)
                o_parts.append(jnp.dot(e.astype(MXU_DTYPE), vcat[:, sl], preferred_element_type=F32) / den)
                lse_parts.append(jnp.broadcast_to(mx + jnp.log(den), (B, HEAD_DIM)))
            _set_rows(o_ref, cur, d, jnp.concatenate(o_parts, axis=1))
            _set_rows(lse_ref, cur, d, jnp.concatenate(lse_parts, axis=1))

        _for_each_block(d, m, task, ATT_FWD_UNROLL)

    main = lambda c0: pl.BlockSpec((span, W), lambda hp, n: (n, c0 + hp))
    prev = lambda c0: pl.BlockSpec((group, W), lambda hp, n: (jnp.maximum(n * m - 1, 0), c0 + hp))
    out_blk = pl.BlockSpec((span, W), lambda hp, n: (n, hp))
    return pl.pallas_call(
        body, name=name, grid=(PW // W, T // span),
        in_specs=[main(c_q), main(c_k), main(c_v), prev(c_k), prev(c_v)],
        out_specs=[out_blk, out_blk],
        out_shape=[_sds((T, PW), F32), _sds((T, PW), F32)],
        compiler_params=_cparams("parallel", "parallel"),
    )(qkv, qkv, qkv, qkv, qkv)


def _attn_bwd(qkv, lse, do, corr, g, d, PW, name):
    T = qkv.shape[0]
    B, W = ATT_BLK, LANE
    m, span, group, n_blocks = _attn_geometry(T, d)
    c_q = g * PW // W
    c_k, c_v = c_q + 3 * PW // W, c_q + 6 * PW // W
    scale = HEAD_DIM ** -0.5
    nt = (((1,), (1,)), ((), ()))
    tn = (((0,), (0,)), ((), ()))

    def body(q_ref, k_ref, v_ref, do_ref, l_ref, c_ref, kp_ref, vp_ref, qx_ref, dox_ref, lx_ref, cx_ref,
             dq_ref, dk_ref, dv_ref):
        n = pl.program_id(1)
        i1 = lax.broadcasted_iota(jnp.int32, (B, B), 0)
        j1 = lax.broadcasted_iota(jnp.int32, (B, B), 1)
        i2 = lax.broadcasted_iota(jnp.int32, (2 * B, B), 0)
        j2 = lax.broadcasted_iota(jnp.int32, (2 * B, B), 1)

        def task(r, j):
            cur = j * group + r
            blk = n * m + j
            q_c, k_c, v_c = _rows(q_ref, cur, d), _rows(k_ref, cur, d), _rows(v_ref, cur, d)
            do_c, l_c, c_c = _rows(do_ref, cur, d), _rows(l_ref, cur, d), _rows(c_ref, cur, d)
            if j == 0:
                k_p, v_p = _rows(kp_ref, r, d), _rows(vp_ref, r, d)
            else:
                k_p, v_p = _rows(k_ref, cur - group, d), _rows(v_ref, cur - group, d)
            if j == m - 1:
                nxt = [_rows(ref, r, d) for ref in (qx_ref, dox_ref, lx_ref, cx_ref)]
            else:
                nxt = [_rows(ref, cur + group, d) for ref in (q_ref, do_ref, l_ref, c_ref)]
            q_x, do_x, l_x, c_x = nxt
            kn_c, kn_p, v_c, v_p = (a.astype(MXU_DTYPE) for a in (k_c, k_p, v_c, v_p))
            qn_c = q_c.astype(MXU_DTYPE)
            qn_cat = jnp.concatenate([qn_c, q_x.astype(MXU_DTYPE)], axis=0)
            do_cb = do_c.astype(MXU_DTYPE)
            do_cat = jnp.concatenate([do_cb, do_x.astype(MXU_DTYPE)], axis=0)
            l_cat = jnp.concatenate([l_c, l_x], axis=0)
            c_cat = jnp.concatenate([c_c, c_x], axis=0)
            mask_p = (j1 >= i1) & (blk > 0)
            mask_c = ((i2 < B) & (j2 <= i2)) | ((i2 >= B) & (j2 >= i2 - B) & (blk + 1 < n_blocks))
            dqn, dkn, dv = [], [], []
            for h, sl in enumerate(_head_slices()):
                lane = slice(h * HEAD_DIM, h * HEAD_DIM + 1)
                s_p = lax.dot_general(qn_c[:, sl], kn_p[:, sl], nt, preferred_element_type=F32) * scale
                pr_p = jnp.where(mask_p, jnp.exp(s_p - l_c[:, lane]), 0.0)
                dp_p = lax.dot_general(do_cb[:, sl], v_p[:, sl], nt, preferred_element_type=F32)
                ds_p = (pr_p * (dp_p + c_c[:, lane]) * scale).astype(MXU_DTYPE)
                s_c = lax.dot_general(qn_cat[:, sl], kn_c[:, sl], nt, preferred_element_type=F32) * scale
                pr_c = jnp.where(mask_c, jnp.exp(s_c - l_cat[:, lane]), 0.0)
                dp_c = lax.dot_general(do_cat[:, sl], v_c[:, sl], nt, preferred_element_type=F32)
                ds_c = (pr_c * (dp_c + c_cat[:, lane]) * scale).astype(MXU_DTYPE)
                dqn.append(jnp.dot(ds_p, kn_p[:, sl], preferred_element_type=F32)
                           + jnp.dot(ds_c[:B], kn_c[:, sl], preferred_element_type=F32))
                dkn.append(lax.dot_general(ds_c, qn_cat[:, sl], tn, preferred_element_type=F32))
                dv.append(lax.dot_general(pr_c.astype(MXU_DTYPE), do_cat[:, sl], tn, preferred_element_type=F32))
            _set_rows(dq_ref, cur, d, jnp.concatenate(dqn, axis=1))
            _set_rows(dk_ref, cur, d, jnp.concatenate(dkn, axis=1))
            _set_rows(dv_ref, cur, d, jnp.concatenate(dv, axis=1))

        _for_each_block(d, m, task, ATT_BWD_UNROLL)

    main = lambda c0: pl.BlockSpec((span, W), lambda hp, n: (n, c0 + hp))
    prev = lambda c0: pl.BlockSpec((group, W), lambda hp, n: (jnp.maximum(n * m - 1, 0), c0 + hp))
    nxt = lambda c0: pl.BlockSpec((group, W), lambda hp, n: (jnp.minimum((n + 1) * m, n_blocks - 1), c0 + hp))
    own = pl.BlockSpec((span, W), lambda hp, n: (n, hp))
    return pl.pallas_call(
        body, name=name, grid=(PW // W, T // span),
        in_specs=[main(c_q), main(c_k), main(c_v), main(0), main(0), main(0), prev(c_k), prev(c_v),
                  nxt(c_q), nxt(0), nxt(0), nxt(0)],
        out_specs=[own, own, own],
        out_shape=[_sds((T, PW), F32)] * 3,
        compiler_params=_cparams("parallel", "parallel"),
    )(qkv, qkv, qkv, do, lse, corr, qkv, qkv, qkv, do, lse, corr)


def _softmax3(lses):
    mx = jnp.maximum(jnp.maximum(lses[0], lses[1]), lses[2])
    ex = [jnp.exp(l - mx) for l in lses]
    inv = 1.0 / (ex[0] + ex[1] + ex[2])
    return [e * inv for e in ex]


def _mix_fwd(os_, lses, name, tr=1024):
    T, PW = os_[0].shape
    tr = _fit(T, tr)

    def body(o0, o1, o2, l0, l1, l2, y_ref):
        alpha = _softmax3([l0[...], l1[...], l2[...]])
        for g, o_ref in enumerate((o0, o1, o2)):
            y_ref[:, g * PW:(g + 1) * PW] = (o_ref[...] * alpha[g]).astype(y_ref.dtype)

    blk = pl.BlockSpec((tr, PW), lambda i: (i, 0))
    return pl.pallas_call(
        body, name=name, grid=(T // tr,),
        in_specs=[blk] * 6, out_specs=pl.BlockSpec((tr, 3 * PW), lambda i: (i, 0)),
        out_shape=_sds((T, 3 * PW), MXU_DTYPE),
        compiler_params=_cparams("parallel"),
    )(*os_, *lses)


def _mix_bwd(dymix, os_, lses, c_start, name, tr=1024):
    T, PW = os_[0].shape
    tr = _fit(T, tr)
    HP = PW // HEAD_DIM
    c0 = c_start // PW

    def body(d0, d1, d2, o0, o1, o2, l0, l1, l2, do0, do1, do2, dl0, dl1, dl2):
        col_head = lax.broadcasted_iota(jnp.int32, (tr, PW), 1) // HEAD_DIM
        alpha = _softmax3([l0[...], l1[...], l2[...]])
        dys = [d0[...].astype(F32), d1[...].astype(F32), d2[...].astype(F32)]
        dots = [_head_sum(dy * o_ref[...], col_head, HP) for dy, o_ref in zip(dys, (o0, o1, o2))]
        mean_dot = alpha[0] * dots[0] + alpha[1] * dots[1] + alpha[2] * dots[2]
        for g, (do_ref, dl_ref) in enumerate(((do0, dl0), (do1, dl1), (do2, dl2))):
            do_ref[...] = dys[g] * alpha[g]
            dl_ref[...] = -alpha[g] * mean_dot

    blk = pl.BlockSpec((tr, PW), lambda i: (i, 0))
    dy_specs = [pl.BlockSpec((tr, PW), lambda i, g=g: (i, c0 + g)) for g in range(3)]
    outs = pl.pallas_call(
        body, name=name, grid=(T // tr,),
        in_specs=dy_specs + [blk] * 6, out_specs=[blk] * 6,
        out_shape=[_sds((T, PW), F32)] * 6,
        compiler_params=_cparams("parallel"),
    )(dymix, dymix, dymix, *os_, *lses)
    return outs[:3], outs[3:]


def _adamw_math(w, g, m, v):
    m2 = ADAM_B1 * m + (1.0 - ADAM_B1) * g
    v2 = ADAM_B2 * v + (1.0 - ADAM_B2) * (g * g)
    m_hat = m2 / (1.0 - ADAM_B1 ** ADAM_STEP)
    v_hat = v2 / (1.0 - ADAM_B2 ** ADAM_STEP)
    delta = -ADAM_LR * (m_hat / (jnp.sqrt(v_hat) + ADAM_EPS) + ADAM_WD * w)
    return delta, m2, v2


def _adamw_layer(layer, w, m, v, own, landed, me, prev, name, tr=256):
    _, R, C = w.shape
    tr = next(t for t in range(min(tr, R) // 16 * 16, 0, -16) if R % t == 0)

    def body(me_ref, w_ref, m_ref, v_ref, own_ref, land_ref, *rest):
        g_ref, d_ref, m2_ref, v2_ref = rest[-4:]
        g = own_ref[...].astype(F32)
        for j in range(N_PEER):
            g = g + land_ref[j].astype(F32)
        delta, m2, v2 = _adamw_math(w_ref[...], g, m_ref[...], v_ref[...])
        g_ref[...] = g
        d_ref[...] = delta
        m2_ref[...] = m2
        v2_ref[...] = v2

    lay = pl.BlockSpec((None, tr, C), lambda i, me_ref: (layer, i, 0))
    in_specs = [lay, lay, lay, pl.BlockSpec((None, tr, C), lambda i, me_ref: (me_ref[0], i, 0)),
                pl.BlockSpec((N_PEER, tr, C), lambda i, me_ref: (0, i, 0))]
    args = [me, w, m, v, own, landed]
    aliases = {}
    if prev is not None:
        in_specs += [pl.BlockSpec(memory_space=pl.ANY)] * 4
        args += list(prev)
        aliases = {6 + i: i for i in range(4)}
    return pl.pallas_call(
        body, name=name,
        grid_spec=pltpu.PrefetchScalarGridSpec(num_scalar_prefetch=1, grid=(R // tr,), in_specs=in_specs, out_specs=[lay] * 4),
        out_shape=[_sds(w.shape, F32)] * 4,
        input_output_aliases=aliases,
        compiler_params=_cparams("parallel"),
    )(*args)


def _sum_parts(parts, name):
    _, R, C = parts.shape

    def body(p_ref, out_ref):
        g = p_ref[0]
        for j in range(1, N_DEV):
            g = g + p_ref[j]
        out_ref[...] = g

    return pl.pallas_call(
        body, name=name, grid=(1,),
        in_specs=[pl.BlockSpec((N_DEV, R, C), lambda i: (0, 0, 0))], out_specs=pl.BlockSpec((R, C), lambda i: (0, 0)),
        out_shape=_sds((R, C), F32), compiler_params=_cparams("arbitrary"),
    )(parts)


def _adamw_small(ws, gs, ms, vs, name):
    n = len(ws)

    def body(*refs):
        ins, outs = refs[:4 * n], refs[4 * n:]
        for i in range(n):
            delta, m2, v2 = _adamw_math(ins[i][...], ins[n + i][...], ins[2 * n + i][...], ins[3 * n + i][...])
            outs[i][...] = delta
            outs[n + i][...] = m2
            outs[2 * n + i][...] = v2

    outs = pl.pallas_call(
        body, name=name, out_shape=[_sds(w.shape, F32) for w in ws] * 3,
    )(*ws, *gs, *ms, *vs)
    return outs[:n], outs[n:2 * n], outs[2 * n:]


def _pack(arrays, rows_multiple=8):
    flat = []
    for a in arrays:
        a = a.reshape(-1).astype(F32)
        flat.append(jnp.pad(a, (0, (-a.shape[0]) % LANE)))
    flat = jnp.concatenate(flat)
    flat = jnp.pad(flat, (0, (-flat.shape[0]) % (LANE * rows_multiple)))
    return flat.reshape(-1, LANE)


def _unpack(packed, shapes):
    flat = packed.reshape(-1)
    out, off = [], 0
    for s in shapes:
        size = 1
        for dim in s:
            size *= dim
        out.append(flat[off:off + size].reshape(s))
        off += size + (-size) % LANE
    return out


def _layer_fwd(x, wts, getw, dims, dep=None, loss_target=None):
    AW, BW, PW, DP = dims["AW"], dims["BW"], dims["PW"], dims["DP"]
    q_start = 2 * AW + 3 * BW
    h, r1 = _rmsnorm_fwd(x, wts["attn_norm"], "rmsnorm_fwd", dep=dep)
    p = _mm_nt(h, getw("w_in", h), "proj_in", out_dtype=MXU_DTYPE, tm=1024, tn=1408)
    y_a = _sgu_fwd(p, wts["sgu_tril"], wts["sgu_bmat"], "sgu_fwd")
    y_b = _conv_fwd(p, getw("conv_w", y_a), AW, "conv_fwd")
    qkn = _qk_norm_fwd(p, wts["qk_gain"], q_start, PW, "qk_norm_fwd")
    os_, lses = [], []
    for g, d in enumerate(DILATIONS):
        o, lse = _attn_fwd(qkn, g, d, PW, "attn_fwd_%d" % d)
        os_.append(o)
        lses.append(lse)
    y_c = _mix_fwd(os_, lses, "mix_fwd")
    ymix = jnp.concatenate([y_a, y_b, y_c], axis=1)
    x1, h2, r2 = _mm_residual_norm(ymix, getw("w_out", ymix), x, wts["mlp_norm"], "proj_out")
    a, hid = _mm_nn(h2, getw("w_mlp_in", h2), "mlp_in", out_dtype=MXU_DTYPE, relu2=True, tm=1024, tn=1024)
    if loss_target is None:
        x2 = _mm_nn(hid, getw("w_mlp_out", hid), "mlp_out", residual=x1, columns_first=True, tm=512, tn=1024)
    else:
        x2 = _mm_residual_loss(hid, getw("w_mlp_out", hid), x1, loss_target, "mlp_out_loss")
    saved = dict(x=x, h=h, r1=r1, p=p, qkn=qkn, os=os_, lses=lses, ymix=ymix, x1=x1, h2=h2, r2=r2, a=a, hid=hid)
    return x2, saved


def _layer_bwd(dx, dxb, wts, getw, scatter, saved, dims):
    AW, BW, PW, DP = dims["AW"], dims["BW"], dims["PW"], dims["DP"]
    q_start = 2 * AW + 3 * BW
    D = dx.shape[1]
    g_w2 = _wgrad_wide_a(saved["hid"], dxb, "mlp_out_wgrad")
    token = scatter("w_mlp_out", g_w2.reshape(N_DEV, -1, D))
    da = _mm_nt(dxb, getw("w_mlp_out", None), "mlp_out_dgrad", out_dtype=MXU_DTYPE, relu2_pre=saved["a"], dep=token,
                tm=1024, tn=1024)
    g_w1 = _wgrad_wide_b(saved["h2"], da, "mlp_in_wgrad", groups=N_DEV)
    token = scatter("w_mlp_in", g_w1)
    dh2 = _mm_nt(da, getw("w_mlp_in", None), "mlp_in_dgrad", out_dtype=MXU_DTYPE, dep=token, tm=1024, tn=512)
    dx1, dx1b, g_mlp_norm = _rmsnorm_bwd(dh2, saved["x1"], wts["mlp_norm"], saved["r2"], dx, "rmsnorm_bwd")
    g_wout = _wgrad_wide_b(saved["ymix"], dx1b, "proj_out_wgrad")
    token = scatter("w_out", g_wout.reshape(N_DEV, -1, D))
    dymix = _mm_nt(dx1b, getw("w_out", None), "proj_out_dgrad", out_dtype=MXU_DTYPE, dep=token, tm=1024, tn=1024)
    p = saved["p"]
    du, dv, g_sgu_w, g_sgu_bmat = _sgu_bwd(dymix, p, wts["sgu_tril"], wts["sgu_tril_t"], wts["sgu_bmat"], "sgu_bwd")
    d_b, d_c, d_xb, g_conv = _conv_bwd(dymix, p, getw("conv_w", None), AW, "conv_bwd")
    dos, corrs = _mix_bwd(dymix, saved["os"], saved["lses"], AW + BW, "mix_bwd")
    dqns, dkns, dvs = [], [], []
    for g, d in enumerate(DILATIONS):
        dqn, dkn, dvv = _attn_bwd(saved["qkn"], saved["lses"][g], dos[g], corrs[g], g, d, PW, "attn_bwd_%d" % d)
        dqns.append(dqn)
        dkns.append(dkn)
        dvs.append(dvv.astype(MXU_DTYPE))
    dqk, g_qk = _qk_norm_bwd(dqns + dkns, p, wts["qk_gain"], q_start, PW, "qk_norm_bwd")
    g_q, g_k = (part.reshape(-1, HEAD_DIM).sum(0) for part in jnp.split(g_qk[0], 2))
    dp = jnp.concatenate([du, dv, d_b, d_c, d_xb, dqk] + dvs, axis=1)
    g_win_t = _wgrad_wide_a(dp, saved["h"], "proj_in_wgrad")
    token = scatter("w_in", g_win_t.reshape(N_DEV, DP // N_DEV, D))
    dh = _mm_nn(dp, getw("w_in", None), "proj_in_dgrad", out_dtype=MXU_DTYPE, dep=token, tm=1024, tn=1024)
    dx0, dx0b, g_attn_norm = _rmsnorm_bwd(dh, saved["x"], wts["attn_norm"], saved["r1"], dx1, "rmsnorm_bwd")
    H = AW // HEAD_DIM
    tril = jnp.tril(jnp.ones((CHUNK, CHUNK), F32))
    small = [g_attn_norm.reshape(-1), g_sgu_w * tril, g_sgu_bmat.reshape(CHUNK, H, HEAD_DIM).sum(-1).T,
             g_conv, g_q, g_k, g_mlp_norm.reshape(-1)]
    return dx0, dx0b, small


def kernel(x, attn_norm, w_in, sgu_w, sgu_b, conv_w, q_norm, k_norm, w_out, mlp_norm, w_mlp_in, w_mlp_out, loss_target, m_attn_norm, m_w_in, m_sgu_w, m_sgu_b, m_conv_w, m_q_norm, m_k_norm, m_w_out, m_mlp_norm, m_w_mlp_in, m_w_mlp_out, v_attn_norm, v_w_in, v_sgu_w, v_sgu_b, v_conv_w, v_q_norm, v_k_norm, v_w_out, v_mlp_norm, v_w_mlp_in, v_w_mlp_out):
    n_layers = attn_norm.shape[0]
    T, D = x.shape[1], x.shape[2]
    H = sgu_w.shape[1]
    AW = H * HEAD_DIM
    BW = conv_w.shape[2] * N_DEV
    DP = w_in.shape[2] * N_DEV
    DMIX = w_out.shape[1] * N_DEV
    DFF = w_mlp_in.shape[2] * N_DEV
    PW = (DMIX - AW - BW) // 3
    HP = PW // HEAD_DIM
    dims = dict(AW=AW, BW=BW, PW=PW, DP=DP)
    me = 4 * lax.axis_index("x") + 2 * lax.axis_index("y") + lax.axis_index("c")

    big_names = ("w_in", "w_out", "w_mlp_in", "w_mlp_out")
    tr_in = lambda a: jnp.swapaxes(a, 1, 2)
    big_w = dict(zip(big_names, (tr_in(w_in), w_out, w_mlp_in, w_mlp_out)))
    big_m = dict(zip(big_names, (tr_in(m_w_in), m_w_out, m_w_mlp_in, m_w_mlp_out)))
    big_v = dict(zip(big_names, (tr_in(v_w_in), v_w_out, v_w_mlp_in, v_w_mlp_out)))

    keys = []
    for l in range(n_layers):
        keys += [(l, nm) for nm in big_names]
    keys.insert(1, (0, "conv_w"))
    first_src = big_w[keys[0][1]][keys[0][0]].astype(MXU_DTYPE)
    first_flights, first_token = _exchange_start([first_src], [_own_in_place(first_src, me)], "gather", name="gather_start_first")
    zero = first_token[0, 0]
    srcs = [_pack([conv_w]) + zero if nm == "conv_w" else (big_w[nm][l] + zero).astype(MXU_DTYPE) for l, nm in keys[1:]]
    flights, gather_token = _exchange_start(srcs, [_own_in_place(s, me) for s in srcs], "gather", name="gather_start")
    arriving = dict(zip(keys, first_flights + flights))
    forwarding = {}
    relayout = dict(
        w_in=lambda g: g.reshape(DP, D), w_out=lambda g: g.reshape(DMIX, D),
        w_mlp_in=lambda g: g, w_mlp_out=lambda g: g.reshape(DFF, D),
        conv_w=lambda g: jnp.stack([_unpack(g[j], [conv_w.shape])[0] for j in range(N_DEV)], axis=2).reshape(
            n_layers, CONV_WIDTH, BW))
    gathered = {}

    def forward(key, after):
        _, land = _exchange_wait(arriving[key], after, "gather", name="gather_arrive_%d_%s" % key)
        fl, token = _exchange_start(None, [land], "forward", name="gather_forward_%d_%s" % key)
        forwarding[key] = fl[0]
        return token

    def weight_getter(l):
        def getw(nm, after):
            key = (0, nm) if nm == "conv_w" else (l, nm)
            if key not in gathered:
                ahead = keys[keys.index(key):][:2]
                for k in ahead:
                    if k not in forwarding:
                        after = forward(k, after)
                _, land = _exchange_wait(forwarding[key], after, "forward", name="gather_wait_%d_%s" % key)
                gathered[key] = relayout[nm](land)
            return gathered[key][l] if nm == "conv_w" else gathered[key]
        return getw

    tril = jnp.tril(jnp.ones((CHUNK, CHUNK), F32))
    layers = []
    for l in range(n_layers):
        w_tril = sgu_w[l] * tril
        layers.append(dict(
            attn_norm=attn_norm[l][None], mlp_norm=mlp_norm[l][None],
            sgu_tril=w_tril.astype(MXU_DTYPE), sgu_tril_t=w_tril.transpose(0, 2, 1).astype(MXU_DTYPE),
            sgu_bmat=jnp.repeat(sgu_b[l].T, HEAD_DIM, axis=1),
            qk_gain=jnp.concatenate([jnp.tile(q_norm[l], 3 * HP), jnp.tile(k_norm[l], 3 * HP)])[None]))

    xs = x[0]
    saved = []
    for l in range(n_layers):
        xs, sv = _layer_fwd(xs, layers[l], weight_getter(l), dims, dep=gather_token if l == 0 else None,
                            loss_target=loss_target[0] if l == n_layers - 1 else None)
        saved.append(sv)
    loss_blk, dx, dxb = xs
    loss = lax.psum(loss_blk[0, 0], ("x", "y", "c"))

    scattering = {}

    def scatter_starter(l):
        def scatter(nm, partials):
            land = lax.empty((N_PEER,) + partials.shape[1:], partials.dtype)
            fl, tok = _exchange_start([partials], [land], "scatter", name="scatter_start_%d_%s" % (l, nm))
            scattering[(l, nm)] = fl[0]
            return tok
        return scatter

    small = [None] * n_layers
    for l in reversed(range(n_layers)):
        dx, dxb, small[l] = _layer_bwd(dx, dxb, layers[l], weight_getter(l), scatter_starter(l), saved[l], dims)

    small_shapes = [s.shape for s in small[0]]
    small_src = [_pack([s for l in range(n_layers) for s in small[l]])]
    small_flights, small_token = _exchange_start(small_src, [_own_in_place(s, me) for s in small_src], "gather_all",
                                                 name="small_start")
    grad_x = dx[None]

    me1 = me.astype(jnp.int32).reshape(1)
    res = {nm: None for nm in big_names}
    after = small_token
    for l in reversed(range(n_layers)):
        for nm in reversed(big_names):
            own, landed = _exchange_wait(scattering[(l, nm)], after, "scatter", name="scatter_wait_%d_%s" % (l, nm))
            res[nm] = _adamw_layer(l, big_w[nm], big_m[nm], big_v[nm], own, landed, me1, res[nm], "adamw_" + nm)
            after = res[nm][0]
    res["w_in"] = [tr_in(a) for a in res["w_in"]]
    big_out = [res[nm] for nm in big_names]

    _, gathered_small = _exchange_wait(small_flights[0], after, "gather_all", name="small_wait")
    summed = _unpack(_sum_parts(gathered_small, "sum_small"), small_shapes * n_layers)
    ns = len(small_shapes)
    g_small = [jnp.stack([summed[l * ns + i] for l in range(n_layers)]) for i in range(ns)]
    g_attn_norm, g_sgu_w, g_sgu_b, g_conv_full, g_q, g_k, g_mlp_norm = g_small
    cs = conv_w.shape[2]
    g_conv = lax.dynamic_slice_in_dim(g_conv_full, me * cs, cs, axis=2)
    sm_w = (attn_norm, sgu_w, sgu_b, conv_w, q_norm, k_norm, mlp_norm)
    sm_m = (m_attn_norm, m_sgu_w, m_sgu_b, m_conv_w, m_q_norm, m_k_norm, m_mlp_norm)
    sm_v = (v_attn_norm, v_sgu_w, v_sgu_b, v_conv_w, v_q_norm, v_k_norm, v_mlp_norm)
    sm_g = (g_attn_norm, g_sgu_w, g_sgu_b, g_conv, g_q, g_k, g_mlp_norm)
    sm_delta, sm_m2, sm_v2 = _adamw_small(sm_w, sm_g, sm_m, sm_v, "adamw_small")

    def ordered(small_list, big_kind):
        b = [big_out[i][big_kind] for i in range(4)]
        return [small_list[0], b[0], small_list[1], small_list[2], small_list[3], small_list[4], small_list[5],
                b[1], small_list[6], b[2], b[3]]

    return (loss, grad_x, *ordered(list(sm_g), 0), *ordered(sm_delta, 1), *ordered(sm_m2, 2), *ordered(sm_v2, 3))
```

```python
import jax
import jax.numpy as jnp
from jax import lax
from jax.experimental import pallas as pl
from jax.experimental.pallas import tpu as pltpu

N_DEV = 8
HEAD_DIM = 64
CHUNK = 128
ATT_BLK = 128
DILATIONS = (1, 4, 16)
CONV_WIDTH = 3
EPS = 1e-6
ADAM_LR = 0.001
ADAM_B1 = 0.9
ADAM_B2 = 0.999
ADAM_EPS = 1e-08
ADAM_WD = 0.01
ADAM_STEP = 10
MXU_DTYPE = jnp.bfloat16
F32 = jnp.float32
LANE = 128
VMEM_LIMIT_BYTES = 56 * 1024 * 1024
NEG_INF = float("-inf")


def _cparams(*sem):
    return pltpu.CompilerParams(dimension_semantics=sem, vmem_limit_bytes=VMEM_LIMIT_BYTES)


def _sds(shape, dtype):
    return jax.ShapeDtypeStruct(shape, dtype)


def _fit(n, tile):
    for t in range(min(tile, n) // LANE * LANE, 0, -LANE):
        if n % t == 0:
            return t
    return n


_HBM = pl.BlockSpec(memory_space=pltpu.HBM)
_SEM = pl.BlockSpec(memory_space=pltpu.SEMAPHORE)
_DATAFLOW = pltpu.SideEffectType.DATAFLOW_SIDE_EFFECTING
N_PEER = N_DEV - 1


def _mesh_pos():
    x, y, c = lax.axis_index("x"), lax.axis_index("y"), lax.axis_index("c")
    return x, y, c, 4 * x + 2 * y + c


OTHER_CHIPS = (4, 2, 6)
EXCHANGE_PEERS = dict(
    scatter=tuple(range(1, N_DEV)),
    gather_all=tuple(range(1, N_DEV)),
    gather=(1,) + OTHER_CHIPS,
    forward=OTHER_CHIPS)


def _remote_copies(src, land, send_sems, recv_sems, mode):
    x, y, c, me = _mesh_pos()
    copies = []
    for i, k in enumerate(EXCHANGE_PEERS[mode]):
        px = (1 - x) if (k & 4) else x
        py = (1 - y) if (k & 2) else y
        pc = (1 - c) if (k & 1) else c
        if mode == "scatter":
            src_ref, dst_ref, dev = src.at[4 * px + 2 * py + pc], land.at[i], (px, py, pc)
        elif mode == "forward":
            slot = 4 * px + 2 * py + c
            src_ref, dst_ref, dev = land.at[slot], land.at[slot], (x, y, 1 - c)
        else:
            src_ref, dst_ref, dev = src, land.at[me], (px, py, pc)
        copies.append(pltpu.make_async_remote_copy(
            src_ref=src_ref, dst_ref=dst_ref, send_sem=send_sems.at[i], recv_sem=recv_sems.at[i],
            device_id=dev, device_id_type=pl.DeviceIdType.MESH))
    return copies


def _own_in_place(src, me):
    land = lax.empty((N_DEV,) + src.shape, src.dtype)
    return lax.dynamic_update_slice(land, src[None], (me,) + (0,) * src.ndim)


def _exchange_start(srcs, lands, mode, name):
    n = len(lands)
    has_src = srcs is not None
    arrays = (list(srcs) if has_src else []) + list(lands)
    n_arr = len(arrays)
    n_copies = len(EXCHANGE_PEERS[mode])

    def body(*refs):
        src = refs[:n] if has_src else [None] * n
        land = refs[n_arr - n:n_arr]
        send, recv = refs[n_arr:n_arr + n], refs[n_arr + n:n_arr + 2 * n]
        token = refs[2 * n_arr + 2 * n]
        for t in range(n):
            for cp in _remote_copies(src[t], land[t], send[t], recv[t], mode):
                cp.start()
        token[...] = jnp.zeros_like(token)

    outs = pl.pallas_call(
        body, name=name,
        out_shape=([pltpu.SemaphoreType.DMA((n_copies,))] * (2 * n) + [pltpu.HBM(a.shape, a.dtype) for a in arrays]
                   + [_sds((8, LANE), F32)]),
        in_specs=[_HBM] * n_arr,
        out_specs=[_SEM] * (2 * n) + [_HBM] * n_arr + [pl.BlockSpec(memory_space=pltpu.VMEM)],
        input_output_aliases={i: 2 * n + i for i in range(n_arr)},
        compiler_params=pltpu.CompilerParams(has_side_effects=_DATAFLOW),
    )(*[pltpu.with_memory_space_constraint(a, pltpu.HBM) for a in arrays])
    thru = outs[2 * n:2 * n + n_arr]
    flights = [(outs[t], outs[n + t], thru[t] if has_src else None, thru[n_arr - n + t]) for t in range(n)]
    return flights, outs[2 * n + n_arr]


def _exchange_wait(flight, after, mode, name):
    send, recv, src, land = flight
    arrays = [land] if src is None else [src, land]
    n_arr = len(arrays)

    def body(*refs):
        src_ref = refs[0] if n_arr == 2 else None
        land_ref, send_ref, recv_ref = refs[n_arr - 1], refs[n_arr], refs[n_arr + 1]
        for cp in _remote_copies(src_ref, land_ref, send_ref, recv_ref, mode):
            cp.wait_send()
            cp.wait_recv()

    outs = pl.pallas_call(
        body, name=name, out_shape=[pltpu.HBM(a.shape, a.dtype) for a in arrays],
        in_specs=[_HBM] * n_arr + [_SEM, _SEM, pl.BlockSpec(memory_space=pl.ANY)], out_specs=[_HBM] * n_arr,
        input_output_aliases={i: i for i in range(n_arr)},
        compiler_params=pltpu.CompilerParams(has_side_effects=_DATAFLOW),
    )(*arrays, send, recv, after)
    return (None, outs[0]) if src is None else (outs[0], outs[1])


def _rmsnorm_fwd(x, g, name, dep=None, tr=1024):
    T, D = x.shape
    tr = _fit(T, tr)

    def body(x_ref, g_ref, *rest):
        h_ref, r_ref = rest[-2:]
        xv = x_ref[...]
        r = lax.rsqrt(jnp.mean(xv * xv, axis=-1, keepdims=True) + EPS)
        h_ref[...] = (xv * r * g_ref[...]).astype(h_ref.dtype)
        r_ref[...] = r

    in_specs = [pl.BlockSpec((tr, D), lambda i: (i, 0)), pl.BlockSpec((1, D), lambda i: (0, 0))]
    args = [x, g]
    if dep is not None:
        in_specs.append(pl.BlockSpec(dep.shape, lambda i: (0, 0)))
        args.append(dep)
    return pl.pallas_call(
        body, name=name, grid=(T // tr,),
        in_specs=in_specs,
        out_specs=[pl.BlockSpec((tr, D), lambda i: (i, 0)), pl.BlockSpec((tr, 1), lambda i: (i, 0))],
        out_shape=[_sds((T, D), MXU_DTYPE), _sds((T, 1), F32)],
        compiler_params=_cparams("parallel"),
    )(*args)


def _rmsnorm_bwd(dh, x, g, r, dres, name, tr=512):
    T, D = x.shape
    tr = _fit(T, tr)

    def body(dh_ref, x_ref, g_ref, r_ref, dres_ref, dx_ref, dxb_ref, dg_ref):
        @pl.when(pl.program_id(0) == 0)
        def _():
            dg_ref[...] = jnp.zeros_like(dg_ref)

        dh_v, xv, rv = dh_ref[...].astype(F32), x_ref[...], r_ref[...]
        gdy = dh_v * g_ref[...]
        mean_xg = jnp.mean(xv * gdy, axis=-1, keepdims=True)
        dx = dres_ref[...] + rv * gdy - xv * (rv * rv * rv) * mean_xg
        dx_ref[...] = dx
        dxb_ref[...] = dx.astype(dxb_ref.dtype)
        dg_ref[...] += jnp.sum(dh_v * xv * rv, axis=0, keepdims=True)

    row = lambda i: (i, 0)
    return pl.pallas_call(
        body, name=name, grid=(T // tr,),
        in_specs=[pl.BlockSpec((tr, D), row), pl.BlockSpec((tr, D), row), pl.BlockSpec((1, D), lambda i: (0, 0)),
                  pl.BlockSpec((tr, 1), row), pl.BlockSpec((tr, D), row)],
        out_specs=[pl.BlockSpec((tr, D), row), pl.BlockSpec((tr, D), row), pl.BlockSpec((1, D), lambda i: (0, 0))],
        out_shape=[_sds((T, D), F32), _sds((T, D), MXU_DTYPE), _sds((1, D), F32)],
        compiler_params=_cparams("arbitrary"),
    )(dh, x, g, r, dres)


def _mm_nn(a, b, name, out_dtype=F32, residual=None, relu2=False, dep=None, columns_first=False, tm=512, tn=512):
    M, K = a.shape
    grouped = b.ndim == 3
    N = b.shape[0] * b.shape[2] if grouped else b.shape[1]
    tm, tn = _fit(M, tm), _fit(b.shape[2] if grouped else N, tn)
    tile = (lambda j, i: (i, j)) if columns_first else (lambda i, j: (i, j))
    b_mode = pl.Buffered(1 if columns_first else 2)
    if grouped:
        per = b.shape[2] // tn
        b_spec = pl.BlockSpec((None, K, tn), lambda *g: (tile(*g)[1] // per, 0, tile(*g)[1] % per), pipeline_mode=b_mode)
    else:
        b_spec = pl.BlockSpec((K, tn), lambda *g: (0, tile(*g)[1]), pipeline_mode=b_mode)
    n_out = 2 if relu2 else 1

    def body(*refs):
        a_ref, b_ref = refs[0], refs[1]
        r_ref = refs[2] if residual is not None else None
        outs = refs[2 + (residual is not None) + (dep is not None):]
        acc = jnp.dot(a_ref[...], b_ref[...], preferred_element_type=F32)
        if r_ref is not None:
            acc = acc + r_ref[...]
        outs[0][...] = acc.astype(outs[0].dtype)
        if relu2:
            rl = jnp.maximum(acc, 0.0)
            outs[1][...] = (rl * rl).astype(outs[1].dtype)

    out_blk = pl.BlockSpec((tm, tn), lambda *g: tile(*g))
    in_specs = [pl.BlockSpec((tm, K), lambda *g: (tile(*g)[0], 0)), b_spec]
    args = [a, b]
    if residual is not None:
        in_specs.append(out_blk)
        args.append(residual)
    if dep is not None:
        in_specs.append(pl.BlockSpec(dep.shape, lambda *g: (0, 0)))
        args.append(dep)
    outs = pl.pallas_call(
        body, name=name, grid=(N // tn, M // tm) if columns_first else (M // tm, N // tn),
        in_specs=in_specs, out_specs=[out_blk] * n_out, out_shape=[_sds((M, N), out_dtype)] * n_out,
        compiler_params=_cparams("parallel", "parallel"),
    )(*args)
    return outs if relu2 else outs[0]


def _mm_residual_norm(a, b, residual, g, name, tm=512):
    M, K = a.shape
    N = b.shape[1]
    tm = _fit(M, tm)

    def body(a_ref, b_ref, res_ref, g_ref, x_ref, h_ref, r_ref):
        xv = jnp.dot(a_ref[...], b_ref[...], preferred_element_type=F32) + res_ref[...]
        x_ref[...] = xv
        r = lax.rsqrt(jnp.mean(xv * xv, axis=-1, keepdims=True) + EPS)
        h_ref[...] = (xv * r * g_ref[...]).astype(h_ref.dtype)
        r_ref[...] = r

    row = pl.BlockSpec((tm, N), lambda i: (i, 0))
    return pl.pallas_call(
        body, name=name, grid=(M // tm,),
        in_specs=[pl.BlockSpec((tm, K), lambda i: (i, 0)),
                  pl.BlockSpec((K, N), lambda i: (0, 0), pipeline_mode=pl.Buffered(1)), row,
                  pl.BlockSpec((1, N), lambda i: (0, 0))],
        out_specs=[row, row, pl.BlockSpec((tm, 1), lambda i: (i, 0))],
        out_shape=[_sds((M, N), F32), _sds((M, N), MXU_DTYPE), _sds((M, 1), F32)],
        compiler_params=_cparams("parallel"),
    )(a, b, residual, g)


def _mm_norm_bwd(a, b, x, g, r, dres, name, dep=None, tm=256):
    M, K = a.shape
    N = b.shape[1]
    tm = _fit(M, tm)

    def body(*refs):
        a_ref, b_ref, x_ref, g_ref, r_ref, dres_ref = refs[:6]
        dx_ref, dxb_ref, dg_ref = refs[-3:]

        @pl.when(pl.program_id(0) == 0)
        def _():
            dg_ref[...] = jnp.zeros_like(dg_ref)

        dh = jnp.dot(a_ref[...], b_ref[...], preferred_element_type=F32)
        xv, rv = x_ref[...], r_ref[...]
        gdy = dh * g_ref[...]
        mean_xg = jnp.mean(xv * gdy, axis=-1, keepdims=True)
        dx = dres_ref[...] + rv * gdy - xv * (rv * rv * rv) * mean_xg
        dx_ref[...] = dx
        dxb_ref[...] = dx.astype(dxb_ref.dtype)
        dg_ref[...] += jnp.sum(dh * xv * rv, axis=0, keepdims=True)

    row = pl.BlockSpec((tm, N), lambda i: (i, 0))
    in_specs = [pl.BlockSpec((tm, K), lambda i: (i, 0)), pl.BlockSpec((K, N), lambda i: (0, 0), pipeline_mode=pl.Buffered(1)),
                row, pl.BlockSpec((1, N), lambda i: (0, 0)), pl.BlockSpec((tm, 1), lambda i: (i, 0)), row]
    args = [a, b, x, g, r, dres]
    if dep is not None:
        in_specs.append(pl.BlockSpec(dep.shape, lambda i: (0, 0)))
        args.append(dep)
    return pl.pallas_call(
        body, name=name, grid=(M // tm,),
        in_specs=in_specs, out_specs=[row, row, pl.BlockSpec((1, N), lambda i: (0, 0))],
        out_shape=[_sds((M, N), F32), _sds((M, N), MXU_DTYPE), _sds((1, N), F32)],
        compiler_params=_cparams("arbitrary"),
    )(*args)


def _mm_residual_loss(a, b, residual, target, name, tm=512, tn=1024):
    M, K = a.shape
    N = b.shape[1]
    tm, tn = _fit(M, tm), _fit(N, tn)

    def body(a_ref, b_ref, res_ref, t_ref, loss_ref, dx_ref, dxb_ref):
        @pl.when((pl.program_id(0) == 0) & (pl.program_id(1) == 0))
        def _():
            loss_ref[...] = jnp.zeros_like(loss_ref)

        err = jnp.dot(a_ref[...], b_ref[...], preferred_element_type=F32) + res_ref[...] - t_ref[...]
        sq = jnp.sum(jnp.sum(err * err, axis=-1, keepdims=True), axis=0, keepdims=True)
        loss_ref[...] += (0.5 / N) * sq
        dx = err * (1.0 / N)
        dx_ref[...] = dx
        dxb_ref[...] = dx.astype(dxb_ref.dtype)

    blk = pl.BlockSpec((tm, tn), lambda j, i: (i, j))
    return pl.pallas_call(
        body, name=name, grid=(N // tn, M // tm),
        in_specs=[pl.BlockSpec((tm, K), lambda j, i: (i, 0)),
                  pl.BlockSpec((K, tn), lambda j, i: (0, j), pipeline_mode=pl.Buffered(1)), blk, blk],
        out_specs=[pl.BlockSpec((8, LANE), lambda j, i: (0, 0)), blk, blk],
        out_shape=[_sds((8, LANE), F32), _sds((M, N), F32), _sds((M, N), MXU_DTYPE)],
        compiler_params=_cparams("arbitrary", "arbitrary"),
    )(a, b, residual, target)


def _mm_nt(a, b, name, out_dtype=F32, relu2_pre=None, dep=None, tm=512, tn=512):
    M, K = a.shape
    grouped = b.ndim == 3
    N = b.shape[1] if grouped else b.shape[0]
    tm, tn = _fit(M, tm), _fit(N, tn)
    nt = (((1,), (1,)), ((), ()))
    if grouped:
        G, _, Kg = b.shape
        b_spec = pl.BlockSpec((G, tn, Kg), lambda i, j: (0, j, 0))
    else:
        b_spec = pl.BlockSpec((tn, K), lambda i, j: (j, 0))

    def body(*refs):
        a_ref, b_ref = refs[0], refs[1]
        p_ref = refs[2] if relu2_pre is not None else None
        out_ref = refs[2 + (relu2_pre is not None) + (dep is not None)]
        if grouped:
            acc = lax.dot_general(a_ref[:, 0:Kg], b_ref[0], nt, preferred_element_type=F32)
            for g in range(1, G):
                acc += lax.dot_general(a_ref[:, g * Kg:(g + 1) * Kg], b_ref[g], nt, preferred_element_type=F32)
        else:
            acc = lax.dot_general(a_ref[...], b_ref[...], nt, preferred_element_type=F32)
        if p_ref is not None:
            acc = acc * (2.0 * jnp.maximum(p_ref[...].astype(F32), 0.0))
        out_ref[...] = acc.astype(out_ref.dtype)

    out_blk = pl.BlockSpec((tm, tn), lambda i, j: (i, j))
    in_specs = [pl.BlockSpec((tm, K), lambda i, j: (i, 0)), b_spec]
    args = [a, b]
    if relu2_pre is not None:
        in_specs.append(out_blk)
        args.append(relu2_pre)
    if dep is not None:
        in_specs.append(pl.BlockSpec(dep.shape, lambda i, j: (0, 0)))
        args.append(dep)
    return pl.pallas_call(
        body, name=name, grid=(M // tm, N // tn),
        in_specs=in_specs, out_specs=out_blk, out_shape=_sds((M, N), out_dtype),
        compiler_params=_cparams("parallel", "parallel"),
    )(*args)


def _wgrad_wide_a(a, b, name, tm=512):
    T, M = a.shape
    N = b.shape[1]
    tm = _fit(M, tm)

    def body(a_ref, b_ref, out_ref):
        out_ref[...] = lax.dot_general(a_ref[...], b_ref[...], (((0,), (0,)), ((), ())),
                                       preferred_element_type=F32).astype(out_ref.dtype)

    return pl.pallas_call(
        body, name=name, grid=(M // tm,),
        in_specs=[pl.BlockSpec((T, tm), lambda i: (0, i)),
                  pl.BlockSpec((T, N), lambda i: (0, 0), pipeline_mode=pl.Buffered(1))],
        out_specs=pl.BlockSpec((tm, N), lambda i: (i, 0)), out_shape=_sds((M, N), MXU_DTYPE),
        compiler_params=_cparams("parallel"),
    )(a, b)


def _wgrad_wide_b(a, b, name, groups=None, tn=512, t_chunk=512):
    T, M = a.shape
    N = b.shape[1]
    tn = _fit(N if groups is None else N // groups, tn)
    t_chunk = _fit(T, t_chunk)

    def body(a_ref, b_ref, out_ref, at_ref):
        @pl.when(pl.program_id(0) == 0)
        def _():
            for c in range(0, T, t_chunk):
                at_ref[:, c:c + t_chunk] = a_ref[c:c + t_chunk, :].T

        out_ref[...] = jnp.dot(at_ref[...], b_ref[...], preferred_element_type=F32).astype(out_ref.dtype)

    if groups is None:
        out_spec = pl.BlockSpec((M, tn), lambda j: (0, j))
        out_shape = _sds((M, N), MXU_DTYPE)
    else:
        per = N // groups // tn
        out_spec = pl.BlockSpec((None, M, tn), lambda j: (j // per, 0, j % per))
        out_shape = _sds((groups, M, N // groups), MXU_DTYPE)
    return pl.pallas_call(
        body, name=name, grid=(N // tn,),
        in_specs=[pl.BlockSpec((T, M), lambda j: (0, 0), pipeline_mode=pl.Buffered(1)),
                  pl.BlockSpec((T, tn), lambda j: (0, j))],
        out_specs=out_spec, out_shape=out_shape,
        scratch_shapes=[pltpu.VMEM((M, T), MXU_DTYPE)],
        compiler_params=_cparams("arbitrary"),
    )(a, b)


SGU_ROWS = 512


def _sgu_mixed(v, w_ref, b_ref, n_heads):
    parts = [jnp.dot(w_ref[h], v[:, h * HEAD_DIM:(h + 1) * HEAD_DIM], preferred_element_type=F32) for h in range(n_heads)]
    return jnp.concatenate(parts, axis=1) + b_ref[...]


def _sgu_fwd(p, w_tril, bmat, name):
    T = p.shape[0]
    H = w_tril.shape[0]
    AW = H * HEAD_DIM
    rows = _fit(T, SGU_ROWS)

    def body(u_ref, v_ref, w_ref, b_ref, y_ref):
        for c in range(0, rows, CHUNK):
            ch = pl.ds(c, CHUNK)
            mixed = _sgu_mixed(v_ref[ch, :].astype(MXU_DTYPE), w_ref, b_ref, H)
            y_ref[ch, :] = (u_ref[ch, :].astype(F32) * mixed).astype(y_ref.dtype)

    const3 = lambda c: (0, 0, 0)
    return pl.pallas_call(
        body, name=name, grid=(T // rows,),
        in_specs=[pl.BlockSpec((rows, AW), lambda c: (c, 0)), pl.BlockSpec((rows, AW), lambda c: (c, 1)),
                  pl.BlockSpec((H, CHUNK, CHUNK), const3), pl.BlockSpec((CHUNK, AW), lambda c: (0, 0))],
        out_specs=pl.BlockSpec((rows, AW), lambda c: (c, 0)),
        out_shape=_sds((T, AW), MXU_DTYPE),
        compiler_params=_cparams("parallel"),
    )(p, p, w_tril, bmat)


def _sgu_bwd(dymix, p, w_tril, w_tril_t, bmat, name):
    T = p.shape[0]
    H = w_tril.shape[0]
    AW = H * HEAD_DIM
    rows = _fit(T, SGU_ROWS)

    def body(dy_ref, u_ref, v_ref, w_ref, wt_ref, b_ref, du_ref, dv_ref, dw_ref, db_ref):
        @pl.when(pl.program_id(0) == 0)
        def _():
            dw_ref[...] = jnp.zeros_like(dw_ref)
            db_ref[...] = jnp.zeros_like(db_ref)

        for c in range(0, rows, CHUNK):
            ch = pl.ds(c, CHUNK)
            v = v_ref[ch, :].astype(MXU_DTYPE)
            dy = dy_ref[ch, :].astype(F32)
            du_ref[ch, :] = (dy * _sgu_mixed(v, w_ref, b_ref, H)).astype(du_ref.dtype)
            dm = dy * u_ref[ch, :].astype(F32)
            db_ref[...] += dm
            dm_c = dm.astype(MXU_DTYPE)
            dv = []
            for h in range(H):
                sl = slice(h * HEAD_DIM, (h + 1) * HEAD_DIM)
                dv.append(jnp.dot(wt_ref[h], dm_c[:, sl], preferred_element_type=F32))
                dw_ref[h] += lax.dot_general(dm_c[:, sl], v[:, sl], (((1,), (1,)), ((), ())), preferred_element_type=F32)
            dv_ref[ch, :] = jnp.concatenate(dv, axis=1).astype(dv_ref.dtype)

    const3 = lambda c: (0, 0, 0)
    blk = pl.BlockSpec((rows, AW), lambda c: (c, 0))
    return pl.pallas_call(
        body, name=name, grid=(T // rows,),
        in_specs=[blk, blk, pl.BlockSpec((rows, AW), lambda c: (c, 1)),
                  pl.BlockSpec((H, CHUNK, CHUNK), const3), pl.BlockSpec((H, CHUNK, CHUNK), const3),
                  pl.BlockSpec((CHUNK, AW), lambda c: (0, 0))],
        out_specs=[blk, blk, pl.BlockSpec((H, CHUNK, CHUNK), const3), pl.BlockSpec((CHUNK, AW), lambda c: (0, 0))],
        out_shape=[_sds((T, AW), MXU_DTYPE), _sds((T, AW), MXU_DTYPE), _sds((H, CHUNK, CHUNK), F32), _sds((CHUNK, AW), F32)],
        compiler_params=_cparams("arbitrary"),
    )(dymix, p, p, w_tril, w_tril_t, bmat)


def _shift_down(z, s, row):
    return jnp.where(row >= s, pltpu.roll(z, s, 0), 0.0)


def _shift_up(z, s, row, T):
    return jnp.where(row < T - s, pltpu.roll(z, T - s, 0), 0.0)


def _conv_fwd(p, w_conv, AW, name):
    T = p.shape[0]
    BW = w_conv.shape[1]
    nb = BW // LANE
    b0 = 2 * AW // LANE

    def body(b_ref, c_ref, x_ref, w_ref, y_ref):
        row = lax.broadcasted_iota(jnp.int32, (T, LANE), 0)
        z = c_ref[...].astype(F32) * x_ref[...].astype(F32)
        w0, w1, w2 = w_ref[0:1, :], w_ref[1:2, :], w_ref[2:3, :]
        conv = w2 * z + w1 * _shift_down(z, 1, row) + w0 * _shift_down(z, 2, row)
        y_ref[...] = (b_ref[...].astype(F32) * conv).astype(y_ref.dtype)

    return pl.pallas_call(
        body, name=name, grid=(nb,),
        in_specs=[pl.BlockSpec((T, LANE), lambda j: (0, b0 + j)), pl.BlockSpec((T, LANE), lambda j: (0, b0 + nb + j)),
                  pl.BlockSpec((T, LANE), lambda j: (0, b0 + 2 * nb + j)), pl.BlockSpec((CONV_WIDTH, LANE), lambda j: (0, j))],
        out_specs=pl.BlockSpec((T, LANE), lambda j: (0, j)),
        out_shape=_sds((T, BW), MXU_DTYPE),
        compiler_params=_cparams("parallel"),
    )(p, p, p, w_conv)


def _conv_bwd(dymix, p, w_conv, AW, name):
    T = p.shape[0]
    BW = w_conv.shape[1]
    nb = BW // LANE
    b0 = 2 * AW // LANE
    y0 = AW // LANE

    def body(dy_ref, b_ref, c_ref, x_ref, w_ref, db_ref, dc_ref, dxb_ref, dw_ref):
        row = lax.broadcasted_iota(jnp.int32, (T, LANE), 0)
        cv, xv, dy = c_ref[...].astype(F32), x_ref[...].astype(F32), dy_ref[...].astype(F32)
        w0, w1, w2 = w_ref[0:1, :], w_ref[1:2, :], w_ref[2:3, :]
        z = cv * xv
        z1 = _shift_down(z, 1, row)
        z2 = _shift_down(z, 2, row)
        conv = w2 * z + w1 * z1 + w0 * z2
        db_ref[...] = (dy * conv).astype(db_ref.dtype)
        dconv = dy * b_ref[...].astype(F32)
        dz = w2 * dconv + w1 * _shift_up(dconv, 1, row, T) + w0 * _shift_up(dconv, 2, row, T)
        dc_ref[...] = (dz * xv).astype(dc_ref.dtype)
        dxb_ref[...] = (dz * cv).astype(dxb_ref.dtype)
        dw_ref[0:1, :] = jnp.sum(dconv * z2, axis=0, keepdims=True)
        dw_ref[1:2, :] = jnp.sum(dconv * z1, axis=0, keepdims=True)
        dw_ref[2:3, :] = jnp.sum(dconv * z, axis=0, keepdims=True)

    col = lambda j: (0, j)
    return pl.pallas_call(
        body, name=name, grid=(nb,),
        in_specs=[pl.BlockSpec((T, LANE), lambda j: (0, y0 + j)),
                  pl.BlockSpec((T, LANE), lambda j: (0, b0 + j)), pl.BlockSpec((T, LANE), lambda j: (0, b0 + nb + j)),
                  pl.BlockSpec((T, LANE), lambda j: (0, b0 + 2 * nb + j)), pl.BlockSpec((CONV_WIDTH, LANE), col)],
        out_specs=[pl.BlockSpec((T, LANE), col)] * 3 + [pl.BlockSpec((CONV_WIDTH, LANE), col)],
        out_shape=[_sds((T, BW), MXU_DTYPE)] * 3 + [_sds((CONV_WIDTH, BW), F32)],
        compiler_params=_cparams("parallel"),
    )(dymix, p, p, p, w_conv)


def _head_sum(x, col_head, n_heads):
    out = jnp.zeros_like(x)
    for h in range(n_heads):
        sel = col_head == h
        out = jnp.where(sel, jnp.sum(jnp.where(sel, x, 0.0), axis=-1, keepdims=True), out)
    return out


def _same_head(width):
    assert width == 2 * HEAD_DIM
    return lax.broadcasted_iota(jnp.int32, (1, width), 1) < HEAD_DIM


def _head_sum2(x, first):
    s0 = jnp.sum(jnp.where(first, x, 0.0), axis=-1, keepdims=True)
    s1 = jnp.sum(jnp.where(first, 0.0, x), axis=-1, keepdims=True)
    return jnp.where(first, s0, s1)


def _head_norm(x, g, first):
    r = lax.rsqrt(_head_sum2(x * x, first) * (1.0 / HEAD_DIM) + EPS)
    return x * r * g, r


def _head_norm_bwd(dy, x, g, r, first):
    gdy = dy * g
    mean_xg = _head_sum2(x * gdy, first) * (1.0 / HEAD_DIM)
    return r * gdy - x * (r * r * r) * mean_xg, dy * x * r


ATT_SPAN_MIN = 512
ATT_FWD_UNROLL = 4
ATT_BWD_UNROLL = 4
HEADS_PER_LANES = LANE // HEAD_DIM


def _attn_geometry(T, d):
    m = max(1, ATT_SPAN_MIN // (ATT_BLK * d))
    return m, ATT_BLK * d * m, ATT_BLK * d, T // (ATT_BLK * d)


def _rows(ref, start, d):
    return ref[pl.ds(start, ATT_BLK, stride=d), :] if d > 1 else ref[pl.ds(start, ATT_BLK), :]


def _set_rows(ref, start, d, value):
    if d > 1:
        ref[pl.ds(start, ATT_BLK, stride=d), :] = value
    else:
        ref[pl.ds(start, ATT_BLK), :] = value


def _for_each_block(d, m, task, unroll):
    for j in range(m):
        if d == 1:
            task(0, j)
        else:
            lax.fori_loop(0, d, lambda r, carry, j=j: (task(r, j), carry)[1], 0, unroll=min(unroll, d))


def _head_slices():
    return [slice(h * HEAD_DIM, (h + 1) * HEAD_DIM) for h in range(HEADS_PER_LANES)]


def _qk_norm_fwd(p, gains, q_start, PW, name, tr=1024):
    T = p.shape[0]
    tr = _fit(T, tr)
    n_norm = gains.shape[1] // PW
    n = n_norm * 3 // 2
    c0 = q_start // PW

    def body(*refs):
        x_refs, g_ref, out_ref = refs[:n], refs[n], refs[n + 1]
        first = _same_head(LANE)
        for i in range(n):
            for c in range(0, PW, LANE):
                lo = i * PW + c
                x = x_refs[i][:, c:c + LANE].astype(F32)
                out_ref[:, lo:lo + LANE] = _head_norm(x, g_ref[:, lo:lo + LANE], first)[0] if i < n_norm else x

    return pl.pallas_call(
        body, name=name, grid=(T // tr,),
        in_specs=[pl.BlockSpec((tr, PW), lambda i, j=j: (i, c0 + j)) for j in range(n)]
        + [pl.BlockSpec((1, n_norm * PW), lambda i: (0, 0))],
        out_specs=pl.BlockSpec((tr, n * PW), lambda i: (i, 0)), out_shape=_sds((T, n * PW), F32),
        compiler_params=_cparams("parallel"),
    )(*([p] * n), gains)


def _qk_norm_bwd(dns, p, gains, q_start, PW, name, tr=1024):
    T = p.shape[0]
    tr = _fit(T, tr)
    n = len(dns)
    c0 = q_start // PW

    def body(*refs):
        d_refs, x_refs, g_ref, out_ref, acc_ref = refs[:n], refs[n:2 * n], refs[2 * n], refs[2 * n + 1], refs[2 * n + 2]

        @pl.when(pl.program_id(0) == 0)
        def _():
            acc_ref[...] = jnp.zeros_like(acc_ref)

        first = _same_head(LANE)
        for i in range(n):
            for c in range(0, PW, LANE):
                lo = i * PW + c
                x, gv = x_refs[i][:, c:c + LANE].astype(F32), g_ref[:, lo:lo + LANE]
                _, r = _head_norm(x, gv, first)
                dx, g_part = _head_norm_bwd(d_refs[i][:, c:c + LANE], x, gv, r, first)
                out_ref[:, lo:lo + LANE] = dx.astype(out_ref.dtype)
                acc_ref[0:1, lo:lo + LANE] += jnp.sum(g_part, axis=0, keepdims=True)

    return pl.pallas_call(
        body, name=name, grid=(T // tr,),
        in_specs=[pl.BlockSpec((tr, PW), lambda i: (i, 0))] * n
        + [pl.BlockSpec((tr, PW), lambda i, j=j: (i, c0 + j)) for j in range(n)]
        + [pl.BlockSpec((1, n * PW), lambda i: (0, 0))],
        out_specs=[pl.BlockSpec((tr, n * PW), lambda i: (i, 0)), pl.BlockSpec((8, n * PW), lambda i: (0, 0))],
        out_shape=[_sds((T, n * PW), MXU_DTYPE), _sds((8, n * PW), F32)],
        compiler_params=_cparams("arbitrary"),
    )(*dns, *([p] * n), gains)


def _attn_fwd(qkv, g, d, PW, name):
    T = qkv.shape[0]
    B, W = ATT_BLK, LANE
    m, span, group, _ = _attn_geometry(T, d)
    c_q = g * PW // W
    c_k, c_v = c_q + 3 * PW // W, c_q + 6 * PW // W
    scale = HEAD_DIM ** -0.5

    def body(q_ref, k_ref, v_ref, kp_ref, vp_ref, o_ref, lse_ref):
        n = pl.program_id(1)
        qi = lax.broadcasted_iota(jnp.int32, (B, 2 * B), 0)
        kj = lax.broadcasted_iota(jnp.int32, (B, 2 * B), 1)
        band = (kj >= qi) & (kj <= qi + B)

        def task(r, j):
            cur = j * group + r
            if j == 0:
                kp, vp = _rows(kp_ref, r, d), _rows(vp_ref, r, d)
            else:
                kp, vp = _rows(k_ref, cur - group, d), _rows(v_ref, cur - group, d)
            mask = band & ((n * m + j > 0) | (kj >= B))
            qn = _rows(q_ref, cur, d).astype(MXU_DTYPE)
            kn = jnp.concatenate([kp, _rows(k_ref, cur, d)], axis=0).astype(MXU_DTYPE)
            vcat = jnp.concatenate([vp, _rows(v_ref, cur, d)], axis=0).astype(MXU_DTYPE)
            o_parts, lse_parts = [], []
            for sl in _head_slices():
                s = lax.dot_general(qn[:, sl], kn[:, sl], (((1,), (1,)), ((), ())), preferred_element_type=F32) * scale
                s = jnp.where(mask, s, NEG_INF)
                mx = jnp.max(s, axis=-1, keepdims=True)
                e = jnp.exp(s - mx)
                den = jnp.sum(e, axis=-1, keepdims=True)
                o_parts.append(jnp.dot(e.astype(MXU_DTYPE), vcat[:, sl], preferred_element_type=F32) / den)
                lse_parts.append(jnp.broadcast_to(mx + jnp.log(den), (B, HEAD_DIM)))
            _set_rows(o_ref, cur, d, jnp.concatenate(o_parts, axis=1))
            _set_rows(lse_ref, cur, d, jnp.concatenate(lse_parts, axis=1))

        _for_each_block(d, m, task, ATT_FWD_UNROLL)

    main = lambda c0: pl.BlockSpec((span, W), lambda hp, n: (n, c0 + hp))
    prev = lambda c0: pl.BlockSpec((group, W), lambda hp, n: (jnp.maximum(n * m - 1, 0), c0 + hp))
    out_blk = pl.BlockSpec((span, W), lambda hp, n: (n, hp))
    return pl.pallas_call(
        body, name=name, grid=(PW // W, T // span),
        in_specs=[main(c_q), main(c_k), main(c_v), prev(c_k), prev(c_v)],
        out_specs=[out_blk, out_blk],
        out_shape=[_sds((T, PW), F32), _sds((T, PW), F32)],
        compiler_params=_cparams("parallel", "parallel"),
    )(qkv, qkv, qkv, qkv, qkv)


def _attn_bwd(qkv, lse, do, corr, g, d, PW, name):
    T = qkv.shape[0]
    B, W = ATT_BLK, LANE
    m, span, group, n_blocks = _attn_geometry(T, d)
    c_q = g * PW // W
    c_k, c_v = c_q + 3 * PW // W, c_q + 6 * PW // W
    scale = HEAD_DIM ** -0.5
    nt = (((1,), (1,)), ((), ()))
    tn = (((0,), (0,)), ((), ()))

    def body(q_ref, k_ref, v_ref, do_ref, l_ref, c_ref, kp_ref, vp_ref, qx_ref, dox_ref, lx_ref, cx_ref,
             dq_ref, dk_ref, dv_ref):
        n = pl.program_id(1)
        i1 = lax.broadcasted_iota(jnp.int32, (B, B), 0)
        j1 = lax.broadcasted_iota(jnp.int32, (B, B), 1)
        i2 = lax.broadcasted_iota(jnp.int32, (2 * B, B), 0)
        j2 = lax.broadcasted_iota(jnp.int32, (2 * B, B), 1)

        def task(r, j):
            cur = j * group + r
            blk = n * m + j
            q_c, k_c, v_c = _rows(q_ref, cur, d), _rows(k_ref, cur, d), _rows(v_ref, cur, d)
            do_c, l_c, c_c = _rows(do_ref, cur, d), _rows(l_ref, cur, d), _rows(c_ref, cur, d)
            if j == 0:
                k_p, v_p = _rows(kp_ref, r, d), _rows(vp_ref, r, d)
            else:
                k_p, v_p = _rows(k_ref, cur - group, d), _rows(v_ref, cur - group, d)
            if j == m - 1:
                nxt = [_rows(ref, r, d) for ref in (qx_ref, dox_ref, lx_ref, cx_ref)]
            else:
                nxt = [_rows(ref, cur + group, d) for ref in (q_ref, do_ref, l_ref, c_ref)]
            q_x, do_x, l_x, c_x = nxt
            kn_c, kn_p, v_c, v_p = (a.astype(MXU_DTYPE) for a in (k_c, k_p, v_c, v_p))
            qn_c = q_c.astype(MXU_DTYPE)
            qn_cat = jnp.concatenate([qn_c, q_x.astype(MXU_DTYPE)], axis=0)
            do_cb = do_c.astype(MXU_DTYPE)
            do_cat = jnp.concatenate([do_cb, do_x.astype(MXU_DTYPE)], axis=0)
            l_cat = jnp.concatenate([l_c, l_x], axis=0)
            c_cat = jnp.concatenate([c_c, c_x], axis=0)
            mask_p = (j1 >= i1) & (blk > 0)
            mask_c = ((i2 < B) & (j2 <= i2)) | ((i2 >= B) & (j2 >= i2 - B) & (blk + 1 < n_blocks))
            dqn, dkn, dv = [], [], []
            for h, sl in enumerate(_head_slices()):
                lane = slice(h * HEAD_DIM, h * HEAD_DIM + 1)
                s_p = lax.dot_general(qn_c[:, sl], kn_p[:, sl], nt, preferred_element_type=F32) * scale
                pr_p = jnp.where(mask_p, jnp.exp(s_p - l_c[:, lane]), 0.0)
                dp_p = lax.dot_general(do_cb[:, sl], v_p[:, sl], nt, preferred_element_type=F32)
                ds_p = (pr_p * (dp_p + c_c[:, lane]) * scale).astype(MXU_DTYPE)
                s_c = lax.dot_general(qn_cat[:, sl], kn_c[:, sl], nt, preferred_element_type=F32) * scale
                pr_c = jnp.where(mask_c, jnp.exp(s_c - l_cat[:, lane]), 0.0)
                dp_c = lax.dot_general(do_cat[:, sl], v_c[:, sl], nt, preferred_element_type=F32)
                ds_c = (pr_c * (dp_c + c_cat[:, lane]) * scale).astype(MXU_DTYPE)
                dqn.append(jnp.dot(ds_p, kn_p[:, sl], preferred_element_type=F32)
                           + jnp.dot(ds_c[:B], kn_c[:, sl], preferred_element_type=F32))
                dkn.append(lax.dot_general(ds_c, qn_cat[:, sl], tn, preferred_element_type=F32))
                dv.append(lax.dot_general(pr_c.astype(MXU_DTYPE), do_cat[:, sl], tn, preferred_element_type=F32))
            _set_rows(dq_ref, cur, d, jnp.concatenate(dqn, axis=1))
            _set_rows(dk_ref, cur, d, jnp.concatenate(dkn, axis=1))
            _set_rows(dv_ref, cur, d, jnp.concatenate(dv, axis=1))

        _for_each_block(d, m, task, ATT_BWD_UNROLL)

    main = lambda c0: pl.BlockSpec((span, W), lambda hp, n: (n, c0 + hp))
    prev = lambda c0: pl.BlockSpec((group, W), lambda hp, n: (jnp.maximum(n * m - 1, 0), c0 + hp))
    nxt = lambda c0: pl.BlockSpec((group, W), lambda hp, n: (jnp.minimum((n + 1) * m, n_blocks - 1), c0 + hp))
    own = pl.BlockSpec((span, W), lambda hp, n: (n, hp))
    return pl.pallas_call(
        body, name=name, grid=(PW // W, T // span),
        in_specs=[main(c_q), main(c_k), main(c_v), main(0), main(0), main(0), prev(c_k), prev(c_v),
                  nxt(c_q), nxt(0), nxt(0), nxt(0)],
        out_specs=[own, own, own],
        out_shape=[_sds((T, PW), F32)] * 3,
        compiler_params=_cparams("parallel", "parallel"),
    )(qkv, qkv, qkv, do, lse, corr, qkv, qkv, qkv, do, lse, corr)


def _softmax3(lses):
    mx = jnp.maximum(jnp.maximum(lses[0], lses[1]), lses[2])
    ex = [jnp.exp(l - mx) for l in lses]
    inv = 1.0 / (ex[0] + ex[1] + ex[2])
    return [e * inv for e in ex]


def _mix_fwd(os_, lses, name, tr=1024):
    T, PW = os_[0].shape
    tr = _fit(T, tr)

    def body(o0, o1, o2, l0, l1, l2, y_ref):
        alpha = _softmax3([l0[...], l1[...], l2[...]])
        for g, o_ref in enumerate((o0, o1, o2)):
            y_ref[:, g * PW:(g + 1) * PW] = (o_ref[...] * alpha[g]).astype(y_ref.dtype)

    blk = pl.BlockSpec((tr, PW), lambda i: (i, 0))
    return pl.pallas_call(
        body, name=name, grid=(T // tr,),
        in_specs=[blk] * 6, out_specs=pl.BlockSpec((tr, 3 * PW), lambda i: (i, 0)),
        out_shape=_sds((T, 3 * PW), MXU_DTYPE),
        compiler_params=_cparams("parallel"),
    )(*os_, *lses)


def _mix_bwd(dymix, os_, lses, c_start, name, tr=1024):
    T, PW = os_[0].shape
    tr = _fit(T, tr)
    HP = PW // HEAD_DIM
    c0 = c_start // PW

    def body(d0, d1, d2, o0, o1, o2, l0, l1, l2, do0, do1, do2, dl0, dl1, dl2):
        col_head = lax.broadcasted_iota(jnp.int32, (tr, PW), 1) // HEAD_DIM
        alpha = _softmax3([l0[...], l1[...], l2[...]])
        dys = [d0[...].astype(F32), d1[...].astype(F32), d2[...].astype(F32)]
        dots = [_head_sum(dy * o_ref[...], col_head, HP) for dy, o_ref in zip(dys, (o0, o1, o2))]
        mean_dot = alpha[0] * dots[0] + alpha[1] * dots[1] + alpha[2] * dots[2]
        for g, (do_ref, dl_ref) in enumerate(((do0, dl0), (do1, dl1), (do2, dl2))):
            do_ref[...] = dys[g] * alpha[g]
            dl_ref[...] = -alpha[g] * mean_dot

    blk = pl.BlockSpec((tr, PW), lambda i: (i, 0))
    dy_specs = [pl.BlockSpec((tr, PW), lambda i, g=g: (i, c0 + g)) for g in range(3)]
    outs = pl.pallas_call(
        body, name=name, grid=(T // tr,),
        in_specs=dy_specs + [blk] * 6, out_specs=[blk] * 6,
        out_shape=[_sds((T, PW), F32)] * 6,
        compiler_params=_cparams("parallel"),
    )(dymix, dymix, dymix, *os_, *lses)
    return outs[:3], outs[3:]


def _adamw_math(w, g, m, v):
    m2 = ADAM_B1 * m + (1.0 - ADAM_B1) * g
    v2 = ADAM_B2 * v + (1.0 - ADAM_B2) * (g * g)
    m_hat = m2 / (1.0 - ADAM_B1 ** ADAM_STEP)
    v_hat = v2 / (1.0 - ADAM_B2 ** ADAM_STEP)
    delta = -ADAM_LR * (m_hat / (jnp.sqrt(v_hat) + ADAM_EPS) + ADAM_WD * w)
    return delta, m2, v2


def _adamw_layer(layer, w, m, v, own, landed, me, prev, name, tr=256):
    _, R, C = w.shape
    tr = next(t for t in range(min(tr, R) // 16 * 16, 0, -16) if R % t == 0)

    def body(me_ref, w_ref, m_ref, v_ref, own_ref, land_ref, *rest):
        g_ref, d_ref, m2_ref, v2_ref = rest[-4:]
        g = own_ref[...].astype(F32)
        for j in range(N_PEER):
            g = g + land_ref[j].astype(F32)
        delta, m2, v2 = _adamw_math(w_ref[...], g, m_ref[...], v_ref[...])
        g_ref[...] = g
        d_ref[...] = delta
        m2_ref[...] = m2
        v2_ref[...] = v2

    lay = pl.BlockSpec((None, tr, C), lambda i, me_ref: (layer, i, 0))
    in_specs = [lay, lay, lay, pl.BlockSpec((None, tr, C), lambda i, me_ref: (me_ref[0], i, 0)),
                pl.BlockSpec((N_PEER, tr, C), lambda i, me_ref: (0, i, 0))]
    args = [me, w, m, v, own, landed]
    aliases = {}
    if prev is not None:
        in_specs += [pl.BlockSpec(memory_space=pl.ANY)] * 4
        args += list(prev)
        aliases = {6 + i: i for i in range(4)}
    return pl.pallas_call(
        body, name=name,
        grid_spec=pltpu.PrefetchScalarGridSpec(num_scalar_prefetch=1, grid=(R // tr,), in_specs=in_specs, out_specs=[lay] * 4),
        out_shape=[_sds(w.shape, F32)] * 4,
        input_output_aliases=aliases,
        compiler_params=_cparams("parallel"),
    )(*args)


def _sum_parts(parts, name):
    _, R, C = parts.shape

    def body(p_ref, out_ref):
        g = p_ref[0]
        for j in range(1, N_DEV):
            g = g + p_ref[j]
        out_ref[...] = g

    return pl.pallas_call(
        body, name=name, grid=(1,),
        in_specs=[pl.BlockSpec((N_DEV, R, C), lambda i: (0, 0, 0))], out_specs=pl.BlockSpec((R, C), lambda i: (0, 0)),
        out_shape=_sds((R, C), F32), compiler_params=_cparams("arbitrary"),
    )(parts)


def _adamw_small(ws, gs, ms, vs, name):
    n = len(ws)

    def body(*refs):
        ins, outs = refs[:4 * n], refs[4 * n:]
        for i in range(n):
            delta, m2, v2 = _adamw_math(ins[i][...], ins[n + i][...], ins[2 * n + i][...], ins[3 * n + i][...])
            outs[i][...] = delta
            outs[n + i][...] = m2
            outs[2 * n + i][...] = v2

    outs = pl.pallas_call(
        body, name=name, out_shape=[_sds(w.shape, F32) for w in ws] * 3,
    )(*ws, *gs, *ms, *vs)
    return outs[:n], outs[n:2 * n], outs[2 * n:]


def _pack(arrays, rows_multiple=8):
    flat = []
    for a in arrays:
        a = a.reshape(-1).astype(F32)
        flat.append(jnp.pad(a, (0, (-a.shape[0]) % LANE)))
    flat = jnp.concatenate(flat)
    flat = jnp.pad(flat, (0, (-flat.shape[0]) % (LANE * rows_multiple)))
    return flat.reshape(-1, LANE)


def _unpack(packed, shapes):
    flat = packed.reshape(-1)
    out, off = [], 0
    for s in shapes:
        size = 1
        for dim in s:
            size *= dim
        out.append(flat[off:off + size].reshape(s))
        off += size + (-size) % LANE
    return out


def _layer_fwd(x, wts, getw, dims, dep=None, loss_target=None):
    AW, BW, PW, DP = dims["AW"], dims["BW"], dims["PW"], dims["DP"]
    q_start = 2 * AW + 3 * BW
    h, r1 = _rmsnorm_fwd(x, wts["attn_norm"], "rmsnorm_fwd", dep=dep)
    p = _mm_nt(h, getw("w_in", h), "proj_in", out_dtype=MXU_DTYPE, tm=1024, tn=1408)
    y_a = _sgu_fwd(p, wts["sgu_tril"], wts["sgu_bmat"], "sgu_fwd")
    y_b = _conv_fwd(p, getw("conv_w", y_a), AW, "conv_fwd")
    qkn = _qk_norm_fwd(p, wts["qk_gain"], q_start, PW, "qk_norm_fwd")
    os_, lses = [], []
    for g, d in enumerate(DILATIONS):
        o, lse = _attn_fwd(qkn, g, d, PW, "attn_fwd_%d" % d)
        os_.append(o)
        lses.append(lse)
    y_c = _mix_fwd(os_, lses, "mix_fwd")
    ymix = jnp.concatenate([y_a, y_b, y_c], axis=1)
    x1, h2, r2 = _mm_residual_norm(ymix, getw("w_out", ymix), x, wts["mlp_norm"], "proj_out")
    a, hid = _mm_nn(h2, getw("w_mlp_in", h2), "mlp_in", out_dtype=MXU_DTYPE, relu2=True, tm=1024, tn=1024)
    if loss_target is None:
        x2 = _mm_nn(hid, getw("w_mlp_out", hid), "mlp_out", residual=x1, columns_first=True, tm=512, tn=1024)
    else:
        x2 = _mm_residual_loss(hid, getw("w_mlp_out", hid), x1, loss_target, "mlp_out_loss")
    saved = dict(x=x, h=h, r1=r1, p=p, qkn=qkn, os=os_, lses=lses, ymix=ymix, x1=x1, h2=h2, r2=r2, a=a, hid=hid)
    return x2, saved


def _layer_bwd(dx, dxb, wts, getw, scatter, saved, dims):
    AW, BW, PW, DP = dims["AW"], dims["BW"], dims["PW"], dims["DP"]
    q_start = 2 * AW + 3 * BW
    D = dx.shape[1]
    g_w2 = _wgrad_wide_a(saved["hid"], dxb, "mlp_out_wgrad")
    token = scatter("w_mlp_out", g_w2.reshape(N_DEV, -1, D))
    da = _mm_nt(dxb, getw("w_mlp_out", None), "mlp_out_dgrad", out_dtype=MXU_DTYPE, relu2_pre=saved["a"], dep=token,
                tm=1024, tn=1024)
    g_w1 = _wgrad_wide_b(saved["h2"], da, "mlp_in_wgrad", groups=N_DEV)
    token = scatter("w_mlp_in", g_w1)
    dh2 = _mm_nt(da, getw("w_mlp_in", None), "mlp_in_dgrad", out_dtype=MXU_DTYPE, dep=token, tm=1024, tn=512)
    dx1, dx1b, g_mlp_norm = _rmsnorm_bwd(dh2, saved["x1"], wts["mlp_norm"], saved["r2"], dx, "rmsnorm_bwd")
    g_wout = _wgrad_wide_b(saved["ymix"], dx1b, "proj_out_wgrad")
    token = scatter("w_out", g_wout.reshape(N_DEV, -1, D))
    dymix = _mm_nt(dx1b, getw("w_out", None), "proj_out_dgrad", out_dtype=MXU_DTYPE, dep=token, tm=1024, tn=1024)
    p = saved["p"]
    du, dv, g_sgu_w, g_sgu_bmat = _sgu_bwd(dymix, p, wts["sgu_tril"], wts["sgu_tril_t"], wts["sgu_bmat"], "sgu_bwd")
    d_b, d_c, d_xb, g_conv = _conv_bwd(dymix, p, getw("conv_w", None), AW, "conv_bwd")
    dos, corrs = _mix_bwd(dymix, saved["os"], saved["lses"], AW + BW, "mix_bwd")
    dqns, dkns, dvs = [], [], []
    for g, d in enumerate(DILATIONS):
        dqn, dkn, dvv = _attn_bwd(saved["qkn"], saved["lses"][g], dos[g], corrs[g], g, d, PW, "attn_bwd_%d" % d)
        dqns.append(dqn)
        dkns.append(dkn)
        dvs.append(dvv.astype(MXU_DTYPE))
    dqk, g_qk = _qk_norm_bwd(dqns + dkns, p, wts["qk_gain"], q_start, PW, "qk_norm_bwd")
    g_q, g_k = (part.reshape(-1, HEAD_DIM).sum(0) for part in jnp.split(g_qk[0], 2))
    dp = jnp.concatenate([du, dv, d_b, d_c, d_xb, dqk] + dvs, axis=1)
    g_win_t = _wgrad_wide_a(dp, saved["h"], "proj_in_wgrad")
    token = scatter("w_in", g_win_t.reshape(N_DEV, DP // N_DEV, D))
    dx0, dx0b, g_attn_norm = _mm_norm_bwd(dp, getw("w_in", None), saved["x"], wts["attn_norm"], saved["r1"], dx1,
                                          "proj_in_dgrad", dep=token)
    H = AW // HEAD_DIM
    tril = jnp.tril(jnp.ones((CHUNK, CHUNK), F32))
    small = [g_attn_norm.reshape(-1), g_sgu_w * tril, g_sgu_bmat.reshape(CHUNK, H, HEAD_DIM).sum(-1).T,
             g_conv, g_q, g_k, g_mlp_norm.reshape(-1)]
    return dx0, dx0b, small


def kernel(x, attn_norm, w_in, sgu_w, sgu_b, conv_w, q_norm, k_norm, w_out, mlp_norm, w_mlp_in, w_mlp_out, loss_target, m_attn_norm, m_w_in, m_sgu_w, m_sgu_b, m_conv_w, m_q_norm, m_k_norm, m_w_out, m_mlp_norm, m_w_mlp_in, m_w_mlp_out, v_attn_norm, v_w_in, v_sgu_w, v_sgu_b, v_conv_w, v_q_norm, v_k_norm, v_w_out, v_mlp_norm, v_w_mlp_in, v_w_mlp_out):
    n_layers = attn_norm.shape[0]
    T, D = x.shape[1], x.shape[2]
    H = sgu_w.shape[1]
    AW = H * HEAD_DIM
    BW = conv_w.shape[2] * N_DEV
    DP = w_in.shape[2] * N_DEV
    DMIX = w_out.shape[1] * N_DEV
    DFF = w_mlp_in.shape[2] * N_DEV
    PW = (DMIX - AW - BW) // 3
    HP = PW // HEAD_DIM
    dims = dict(AW=AW, BW=BW, PW=PW, DP=DP)
    me = 4 * lax.axis_index("x") + 2 * lax.axis_index("y") + lax.axis_index("c")

    big_names = ("w_in", "w_out", "w_mlp_in", "w_mlp_out")
    tr_in = lambda a: jnp.swapaxes(a, 1, 2)
    big_w = dict(zip(big_names, (tr_in(w_in), w_out, w_mlp_in, w_mlp_out)))
    big_m = dict(zip(big_names, (tr_in(m_w_in), m_w_out, m_w_mlp_in, m_w_mlp_out)))
    big_v = dict(zip(big_names, (tr_in(v_w_in), v_w_out, v_w_mlp_in, v_w_mlp_out)))

    keys = []
    for l in range(n_layers):
        keys += [(l, nm) for nm in big_names]
    keys.insert(1, (0, "conv_w"))
    first_src = big_w[keys[0][1]][keys[0][0]].astype(MXU_DTYPE)
    first_flights, first_token = _exchange_start([first_src], [_own_in_place(first_src, me)], "gather", name="gather_start_first")
    zero = first_token[0, 0]
    srcs = [_pack([conv_w]) + zero if nm == "conv_w" else (big_w[nm][l] + zero).astype(MXU_DTYPE) for l, nm in keys[1:]]
    flights, gather_token = _exchange_start(srcs, [_own_in_place(s, me) for s in srcs], "gather", name="gather_start")
    arriving = dict(zip(keys, first_flights + flights))
    forwarding = {}
    relayout = dict(
        w_in=lambda g: g.reshape(DP, D), w_out=lambda g: g.reshape(DMIX, D),
        w_mlp_in=lambda g: g, w_mlp_out=lambda g: g.reshape(DFF, D),
        conv_w=lambda g: jnp.stack([_unpack(g[j], [conv_w.shape])[0] for j in range(N_DEV)], axis=2).reshape(
            n_layers, CONV_WIDTH, BW))
    gathered = {}

    def forward(key, after):
        _, land = _exchange_wait(arriving[key], after, "gather", name="gather_arrive_%d_%s" % key)
        fl, token = _exchange_start(None, [land], "forward", name="gather_forward_%d_%s" % key)
        forwarding[key] = fl[0]
        return token

    def weight_getter(l):
        def getw(nm, after):
            key = (0, nm) if nm == "conv_w" else (l, nm)
            if key not in gathered:
                ahead = keys[keys.index(key):][:2]
                for k in ahead:
                    if k not in forwarding:
                        after = forward(k, after)
                _, land = _exchange_wait(forwarding[key], after, "forward", name="gather_wait_%d_%s" % key)
                gathered[key] = relayout[nm](land)
            return gathered[key][l] if nm == "conv_w" else gathered[key]
        return getw

    tril = jnp.tril(jnp.ones((CHUNK, CHUNK), F32))
    layers = []
    for l in range(n_layers):
        w_tril = sgu_w[l] * tril
        layers.append(dict(
            attn_norm=attn_norm[l][None], mlp_norm=mlp_norm[l][None],
            sgu_tril=w_tril.astype(MXU_DTYPE), sgu_tril_t=w_tril.transpose(0, 2, 1).astype(MXU_DTYPE),
            sgu_bmat=jnp.repeat(sgu_b[l].T, HEAD_DIM, axis=1),
            qk_gain=jnp.concatenate([jnp.tile(q_norm[l], 3 * HP), jnp.tile(k_norm[l], 3 * HP)])[None]))

    xs = x[0]
    saved = []
    for l in range(n_layers):
        xs, sv = _layer_fwd(xs, layers[l], weight_getter(l), dims, dep=gather_token if l == 0 else None,
                            loss_target=loss_target[0] if l == n_layers - 1 else None)
        saved.append(sv)
    loss_blk, dx, dxb = xs
    loss = lax.psum(loss_blk[0, 0], ("x", "y", "c"))

    scattering = {}

    def scatter_starter(l):
        def scatter(nm, partials):
            land = lax.empty((N_PEER,) + partials.shape[1:], partials.dtype)
            fl, tok = _exchange_start([partials], [land], "scatter", name="scatter_start_%d_%s" % (l, nm))
            scattering[(l, nm)] = fl[0]
            return tok
        return scatter

    small = [None] * n_layers
    for l in reversed(range(n_layers)):
        dx, dxb, small[l] = _layer_bwd(dx, dxb, layers[l], weight_getter(l), scatter_starter(l), saved[l], dims)

    small_shapes = [s.shape for s in small[0]]
    small_src = [_pack([s for l in range(n_layers) for s in small[l]])]
    small_flights, small_token = _exchange_start(small_src, [_own_in_place(s, me) for s in small_src], "gather_all",
                                                 name="small_start")
    grad_x = dx[None]

    me1 = me.astype(jnp.int32).reshape(1)
    res = {nm: None for nm in big_names}
    after = small_token
    for l in reversed(range(n_layers)):
        for nm in reversed(big_names):
            own, landed = _exchange_wait(scattering[(l, nm)], after, "scatter", name="scatter_wait_%d_%s" % (l, nm))
            res[nm] = _adamw_layer(l, big_w[nm], big_m[nm], big_v[nm], own, landed, me1, res[nm], "adamw_" + nm)
            after = res[nm][0]
    res["w_in"] = [tr_in(a) for a in res["w_in"]]
    big_out = [res[nm] for nm in big_names]

    _, gathered_small = _exchange_wait(small_flights[0], after, "gather_all", name="small_wait")
    summed = _unpack(_sum_parts(gathered_small, "sum_small"), small_shapes * n_layers)
    ns = len(small_shapes)
    g_small = [jnp.stack([summed[l * ns + i] for l in range(n_layers)]) for i in range(ns)]
    g_attn_norm, g_sgu_w, g_sgu_b, g_conv_full, g_q, g_k, g_mlp_norm = g_small
    cs = conv_w.shape[2]
    g_conv = lax.dynamic_slice_in_dim(g_conv_full, me * cs, cs, axis=2)
    sm_w = (attn_norm, sgu_w, sgu_b, conv_w, q_norm, k_norm, mlp_norm)
    sm_m = (m_attn_norm, m_sgu_w, m_sgu_b, m_conv_w, m_q_norm, m_k_norm, m_mlp_norm)
    sm_v = (v_attn_norm, v_sgu_w, v_sgu_b, v_conv_w, v_q_norm, v_k_norm, v_mlp_norm)
    sm_g = (g_attn_norm, g_sgu_w, g_sgu_b, g_conv, g_q, g_k, g_mlp_norm)
    sm_delta, sm_m2, sm_v2 = _adamw_small(sm_w, sm_g, sm_m, sm_v, "adamw_small")

    def ordered(small_list, big_kind):
        b = [big_out[i][big_kind] for i in range(4)]
        return [small_list[0], b[0], small_list[1], small_list[2], small_list[3], small_list[4], small_list[5],
                b[1], small_list[6], b[2], b[3]]

    return (loss, grad_x, *ordered(list(sm_g), 0), *ordered(sm_delta, 1), *ordered(sm_m2, 2), *ordered(sm_v2, 3))
```

```python
import jax
import jax.numpy as jnp
from jax import lax
from jax.experimental import pallas as pl
from jax.experimental.pallas import tpu as pltpu

N_DEV = 8
HEAD_DIM = 64
CHUNK = 128
ATT_BLK = 128
DILATIONS = (1, 4, 16)
CONV_WIDTH = 3
EPS = 1e-6
ADAM_LR = 0.001
ADAM_B1 = 0.9
ADAM_B2 = 0.999
ADAM_EPS = 1e-08
ADAM_WD = 0.01
ADAM_STEP = 10
MXU_DTYPE = jnp.bfloat16
F32 = jnp.float32
LANE = 128
VMEM_PHYSICAL_BYTES = 64 * 1024 * 1024
VMEM_LIMIT_BYTES = 56 * 1024 * 1024
NEG_INF = float("-inf")


def _cparams(*sem):
    return pltpu.CompilerParams(dimension_semantics=sem, vmem_limit_bytes=VMEM_LIMIT_BYTES)


def _sds(shape, dtype):
    return jax.ShapeDtypeStruct(shape, dtype)


def _fit(n, tile):
    for t in range(min(tile, n) // LANE * LANE, 0, -LANE):
        if n % t == 0:
            return t
    return n


_HBM = pl.BlockSpec(memory_space=pltpu.HBM)
_SEM = pl.BlockSpec(memory_space=pltpu.SEMAPHORE)
_DATAFLOW = pltpu.SideEffectType.DATAFLOW_SIDE_EFFECTING
N_PEER = N_DEV - 1


def _mesh_pos():
    x, y, c = lax.axis_index("x"), lax.axis_index("y"), lax.axis_index("c")
    return x, y, c, 4 * x + 2 * y + c


OTHER_CHIPS = (4, 2, 6)
EXCHANGE_PEERS = dict(
    scatter=tuple(range(1, N_DEV)),
    gather_all=tuple(range(1, N_DEV)),
    gather=(1,) + OTHER_CHIPS,
    forward=OTHER_CHIPS)


def _remote_copies(src, land, send_sems, recv_sems, mode):
    x, y, c, me = _mesh_pos()
    copies = []
    for i, k in enumerate(EXCHANGE_PEERS[mode]):
        px = (1 - x) if (k & 4) else x
        py = (1 - y) if (k & 2) else y
        pc = (1 - c) if (k & 1) else c
        if mode == "scatter":
            src_ref, dst_ref, dev = src.at[4 * px + 2 * py + pc], land.at[i], (px, py, pc)
        elif mode == "forward":
            slot = 4 * px + 2 * py + c
            src_ref, dst_ref, dev = land.at[slot], land.at[slot], (x, y, 1 - c)
        else:
            src_ref, dst_ref, dev = src, land.at[me], (px, py, pc)
        copies.append(pltpu.make_async_remote_copy(
            src_ref=src_ref, dst_ref=dst_ref, send_sem=send_sems.at[i], recv_sem=recv_sems.at[i],
            device_id=dev, device_id_type=pl.DeviceIdType.MESH))
    return copies


def _own_in_place(src, me):
    land = lax.empty((N_DEV,) + src.shape, src.dtype)
    return lax.dynamic_update_slice(land, src[None], (me,) + (0,) * src.ndim)


def _exchange_start(srcs, lands, mode, name):
    n = len(lands)
    has_src = srcs is not None
    arrays = (list(srcs) if has_src else []) + list(lands)
    n_arr = len(arrays)
    n_copies = len(EXCHANGE_PEERS[mode])

    def body(*refs):
        src = refs[:n] if has_src else [None] * n
        land = refs[n_arr - n:n_arr]
        send, recv = refs[n_arr:n_arr + n], refs[n_arr + n:n_arr + 2 * n]
        token = refs[2 * n_arr + 2 * n]
        for t in range(n):
            for cp in _remote_copies(src[t], land[t], send[t], recv[t], mode):
                cp.start()
        token[...] = jnp.zeros_like(token)

    outs = pl.pallas_call(
        body, name=name,
        out_shape=([pltpu.SemaphoreType.DMA((n_copies,))] * (2 * n) + [pltpu.HBM(a.shape, a.dtype) for a in arrays]
                   + [_sds((8, LANE), F32)]),
        in_specs=[_HBM] * n_arr,
        out_specs=[_SEM] * (2 * n) + [_HBM] * n_arr + [pl.BlockSpec(memory_space=pltpu.VMEM)],
        input_output_aliases={i: 2 * n + i for i in range(n_arr)},
        compiler_params=pltpu.CompilerParams(has_side_effects=_DATAFLOW),
    )(*[pltpu.with_memory_space_constraint(a, pltpu.HBM) for a in arrays])
    thru = outs[2 * n:2 * n + n_arr]
    flights = [(outs[t], outs[n + t], thru[t] if has_src else None, thru[n_arr - n + t]) for t in range(n)]
    return flights, outs[2 * n + n_arr]


def _exchange_wait(flight, after, mode, name):
    send, recv, src, land = flight
    arrays = [land] if src is None else [src, land]
    n_arr = len(arrays)

    def body(*refs):
        src_ref = refs[0] if n_arr == 2 else None
        land_ref, send_ref, recv_ref = refs[n_arr - 1], refs[n_arr], refs[n_arr + 1]
        for cp in _remote_copies(src_ref, land_ref, send_ref, recv_ref, mode):
            cp.wait_send()
            cp.wait_recv()

    outs = pl.pallas_call(
        body, name=name, out_shape=[pltpu.HBM(a.shape, a.dtype) for a in arrays],
        in_specs=[_HBM] * n_arr + [_SEM, _SEM, pl.BlockSpec(memory_space=pl.ANY)], out_specs=[_HBM] * n_arr,
        input_output_aliases={i: i for i in range(n_arr)},
        compiler_params=pltpu.CompilerParams(has_side_effects=_DATAFLOW),
    )(*arrays, send, recv, after)
    return (None, outs[0]) if src is None else (outs[0], outs[1])


def _rmsnorm_fwd(x, g, name, dep=None, tr=1024):
    T, D = x.shape
    tr = _fit(T, tr)

    def body(x_ref, g_ref, *rest):
        h_ref, r_ref = rest[-2:]
        xv = x_ref[...]
        r = lax.rsqrt(jnp.mean(xv * xv, axis=-1, keepdims=True) + EPS)
        h_ref[...] = (xv * r * g_ref[...]).astype(h_ref.dtype)
        r_ref[...] = r

    in_specs = [pl.BlockSpec((tr, D), lambda i: (i, 0)), pl.BlockSpec((1, D), lambda i: (0, 0))]
    args = [x, g]
    if dep is not None:
        in_specs.append(pl.BlockSpec(dep.shape, lambda i: (0, 0)))
        args.append(dep)
    return pl.pallas_call(
        body, name=name, grid=(T // tr,),
        in_specs=in_specs,
        out_specs=[pl.BlockSpec((tr, D), lambda i: (i, 0)), pl.BlockSpec((tr, 1), lambda i: (i, 0))],
        out_shape=[_sds((T, D), MXU_DTYPE), _sds((T, 1), F32)],
        compiler_params=_cparams("parallel"),
    )(*args)


def _rmsnorm_bwd(dh, x, g, r, dres, name, tr=512):
    T, D = x.shape
    tr = _fit(T, tr)

    def body(dh_ref, x_ref, g_ref, r_ref, dres_ref, dx_ref, dxb_ref, dg_ref):
        @pl.when(pl.program_id(0) == 0)
        def _():
            dg_ref[...] = jnp.zeros_like(dg_ref)

        dh_v, xv, rv = dh_ref[...].astype(F32), x_ref[...], r_ref[...]
        gdy = dh_v * g_ref[...]
        mean_xg = jnp.mean(xv * gdy, axis=-1, keepdims=True)
        dx = dres_ref[...] + rv * gdy - xv * (rv * rv * rv) * mean_xg
        dx_ref[...] = dx
        dxb_ref[...] = dx.astype(dxb_ref.dtype)
        dg_ref[...] += jnp.sum(dh_v * xv * rv, axis=0, keepdims=True)

    row = lambda i: (i, 0)
    return pl.pallas_call(
        body, name=name, grid=(T // tr,),
        in_specs=[pl.BlockSpec((tr, D), row), pl.BlockSpec((tr, D), row), pl.BlockSpec((1, D), lambda i: (0, 0)),
                  pl.BlockSpec((tr, 1), row), pl.BlockSpec((tr, D), row)],
        out_specs=[pl.BlockSpec((tr, D), row), pl.BlockSpec((tr, D), row), pl.BlockSpec((1, D), lambda i: (0, 0))],
        out_shape=[_sds((T, D), F32), _sds((T, D), MXU_DTYPE), _sds((1, D), F32)],
        compiler_params=_cparams("arbitrary"),
    )(dh, x, g, r, dres)


def _mm_nn(a, b, name, out_dtype=F32, residual=None, relu2=False, dep=None, columns_first=False, tm=512, tn=512):
    M, K = a.shape
    grouped = b.ndim == 3
    N = b.shape[0] * b.shape[2] if grouped else b.shape[1]
    tm, tn = _fit(M, tm), _fit(b.shape[2] if grouped else N, tn)
    tile = (lambda j, i: (i, j)) if columns_first else (lambda i, j: (i, j))
    b_mode = pl.Buffered(1 if columns_first else 2)
    if grouped:
        per = b.shape[2] // tn
        b_spec = pl.BlockSpec((None, K, tn), lambda *g: (tile(*g)[1] // per, 0, tile(*g)[1] % per), pipeline_mode=b_mode)
    else:
        b_spec = pl.BlockSpec((K, tn), lambda *g: (0, tile(*g)[1]), pipeline_mode=b_mode)
    n_out = 2 if relu2 else 1

    def body(*refs):
        a_ref, b_ref = refs[0], refs[1]
        r_ref = refs[2] if residual is not None else None
        outs = refs[2 + (residual is not None) + (dep is not None):]
        acc = jnp.dot(a_ref[...], b_ref[...], preferred_element_type=F32)
        if r_ref is not None:
            acc = acc + r_ref[...]
        outs[0][...] = acc.astype(outs[0].dtype)
        if relu2:
            rl = jnp.maximum(acc, 0.0)
            outs[1][...] = (rl * rl).astype(outs[1].dtype)

    out_blk = pl.BlockSpec((tm, tn), lambda *g: tile(*g))
    in_specs = [pl.BlockSpec((tm, K), lambda *g: (tile(*g)[0], 0)), b_spec]
    args = [a, b]
    if residual is not None:
        in_specs.append(out_blk)
        args.append(residual)
    if dep is not None:
        in_specs.append(pl.BlockSpec(dep.shape, lambda *g: (0, 0)))
        args.append(dep)
    outs = pl.pallas_call(
        body, name=name, grid=(N // tn, M // tm) if columns_first else (M // tm, N // tn),
        in_specs=in_specs, out_specs=[out_blk] * n_out, out_shape=[_sds((M, N), out_dtype)] * n_out,
        compiler_params=_cparams("parallel", "parallel"),
    )(*args)
    return outs if relu2 else outs[0]


def _mm_residual_norm(a, b, residual, g, name, tm=512):
    M, K = a.shape
    N = b.shape[1]
    tm = _fit(M, tm)

    def body(a_ref, b_ref, res_ref, g_ref, x_ref, h_ref, r_ref):
        xv = jnp.dot(a_ref[...], b_ref[...], preferred_element_type=F32) + res_ref[...]
        x_ref[...] = xv
        r = lax.rsqrt(jnp.mean(xv * xv, axis=-1, keepdims=True) + EPS)
        h_ref[...] = (xv * r * g_ref[...]).astype(h_ref.dtype)
        r_ref[...] = r

    row = pl.BlockSpec((tm, N), lambda i: (i, 0))
    return pl.pallas_call(
        body, name=name, grid=(M // tm,),
        in_specs=[pl.BlockSpec((tm, K), lambda i: (i, 0)),
                  pl.BlockSpec((K, N), lambda i: (0, 0), pipeline_mode=pl.Buffered(1)), row,
                  pl.BlockSpec((1, N), lambda i: (0, 0))],
        out_specs=[row, row, pl.BlockSpec((tm, 1), lambda i: (i, 0))],
        out_shape=[_sds((M, N), F32), _sds((M, N), MXU_DTYPE), _sds((M, 1), F32)],
        compiler_params=_cparams("parallel"),
    )(a, b, residual, g)


def _mm_norm_bwd(a, b, x, g, r, dres, name, dep=None, tm=256):
    M, K = a.shape
    grouped = b.ndim == 3
    N = b.shape[1]
    tm = _fit(M, tm)
    nt = (((1,), (1,)), ((), ()))
    f32_row, mxu_row = tm * N * 4, tm * N * jnp.dtype(MXU_DTYPE).itemsize
    need = b.size * b.dtype.itemsize + 2 * (tm * K * a.dtype.itemsize + 3 * f32_row + mxu_row) + 5 * f32_row
    vmem_limit = max(VMEM_LIMIT_BYTES, min(need, VMEM_PHYSICAL_BYTES - 2 * 1024 * 1024))

    def body(*refs):
        a_ref, b_ref, x_ref, g_ref, r_ref, dres_ref = refs[:6]
        dx_ref, dxb_ref, dg_ref = refs[-3:]

        @pl.when(pl.program_id(0) == 0)
        def _():
            dg_ref[...] = jnp.zeros_like(dg_ref)

        if grouped:
            kg = b.shape[2]
            dh = lax.dot_general(a_ref[:, 0:kg], b_ref[0], nt, preferred_element_type=F32)
            for i in range(1, b.shape[0]):
                dh += lax.dot_general(a_ref[:, i * kg:(i + 1) * kg], b_ref[i], nt, preferred_element_type=F32)
        else:
            dh = jnp.dot(a_ref[...], b_ref[...], preferred_element_type=F32)
        xv, rv = x_ref[...], r_ref[...]
        gdy = dh * g_ref[...]
        mean_xg = jnp.mean(xv * gdy, axis=-1, keepdims=True)
        dx = dres_ref[...] + rv * gdy - xv * (rv * rv * rv) * mean_xg
        dx_ref[...] = dx
        dxb_ref[...] = dx.astype(dxb_ref.dtype)
        dg_ref[...] += jnp.sum(dh * xv * rv, axis=0, keepdims=True)

    row = pl.BlockSpec((tm, N), lambda i: (i, 0))
    in_specs = [pl.BlockSpec((tm, K), lambda i: (i, 0)),
                pl.BlockSpec(b.shape, lambda i: (0,) * b.ndim, pipeline_mode=pl.Buffered(1)),
                row, pl.BlockSpec((1, N), lambda i: (0, 0)), pl.BlockSpec((tm, 1), lambda i: (i, 0)), row]
    args = [a, b, x, g, r, dres]
    if dep is not None:
        in_specs.append(pl.BlockSpec(dep.shape, lambda i: (0, 0)))
        args.append(dep)
    return pl.pallas_call(
        body, name=name, grid=(M // tm,),
        in_specs=in_specs, out_specs=[row, row, pl.BlockSpec((1, N), lambda i: (0, 0))],
        out_shape=[_sds((M, N), F32), _sds((M, N), MXU_DTYPE), _sds((1, N), F32)],
        compiler_params=pltpu.CompilerParams(dimension_semantics=("arbitrary",), vmem_limit_bytes=vmem_limit),
    )(*args)


def _mm_residual_loss(a, b, residual, target, name, tm=512, tn=1024):
    M, K = a.shape
    N = b.shape[1]
    tm, tn = _fit(M, tm), _fit(N, tn)

    def body(a_ref, b_ref, res_ref, t_ref, loss_ref, dx_ref, dxb_ref):
        @pl.when((pl.program_id(0) == 0) & (pl.program_id(1) == 0))
        def _():
            loss_ref[...] = jnp.zeros_like(loss_ref)

        err = jnp.dot(a_ref[...], b_ref[...], preferred_element_type=F32) + res_ref[...] - t_ref[...]
        sq = jnp.sum(jnp.sum(err * err, axis=-1, keepdims=True), axis=0, keepdims=True)
        loss_ref[...] += (0.5 / N) * sq
        dx = err * (1.0 / N)
        dx_ref[...] = dx
        dxb_ref[...] = dx.astype(dxb_ref.dtype)

    blk = pl.BlockSpec((tm, tn), lambda j, i: (i, j))
    return pl.pallas_call(
        body, name=name, grid=(N // tn, M // tm),
        in_specs=[pl.BlockSpec((tm, K), lambda j, i: (i, 0)),
                  pl.BlockSpec((K, tn), lambda j, i: (0, j), pipeline_mode=pl.Buffered(1)), blk, blk],
        out_specs=[pl.BlockSpec((8, LANE), lambda j, i: (0, 0)), blk, blk],
        out_shape=[_sds((8, LANE), F32), _sds((M, N), F32), _sds((M, N), MXU_DTYPE)],
        compiler_params=_cparams("arbitrary", "arbitrary"),
    )(a, b, residual, target)


def _mm_nt(a, b, name, out_dtype=F32, relu2_pre=None, dep=None, tm=512, tn=512):
    M, K = a.shape
    grouped = b.ndim == 3
    N = b.shape[1] if grouped else b.shape[0]
    tm, tn = _fit(M, tm), _fit(N, tn)
    nt = (((1,), (1,)), ((), ()))
    if grouped:
        G, _, Kg = b.shape
        b_spec = pl.BlockSpec((G, tn, Kg), lambda i, j: (0, j, 0))
    else:
        b_spec = pl.BlockSpec((tn, K), lambda i, j: (j, 0))

    def body(*refs):
        a_ref, b_ref = refs[0], refs[1]
        p_ref = refs[2] if relu2_pre is not None else None
        out_ref = refs[2 + (relu2_pre is not None) + (dep is not None)]
        if grouped:
            acc = lax.dot_general(a_ref[:, 0:Kg], b_ref[0], nt, preferred_element_type=F32)
            for g in range(1, G):
                acc += lax.dot_general(a_ref[:, g * Kg:(g + 1) * Kg], b_ref[g], nt, preferred_element_type=F32)
        else:
            acc = lax.dot_general(a_ref[...], b_ref[...], nt, preferred_element_type=F32)
        if p_ref is not None:
            acc = acc * (2.0 * jnp.maximum(p_ref[...].astype(F32), 0.0))
        out_ref[...] = acc.astype(out_ref.dtype)

    out_blk = pl.BlockSpec((tm, tn), lambda i, j: (i, j))
    in_specs = [pl.BlockSpec((tm, K), lambda i, j: (i, 0)), b_spec]
    args = [a, b]
    if relu2_pre is not None:
        in_specs.append(out_blk)
        args.append(relu2_pre)
    if dep is not None:
        in_specs.append(pl.BlockSpec(dep.shape, lambda i, j: (0, 0)))
        args.append(dep)
    return pl.pallas_call(
        body, name=name, grid=(M // tm, N // tn),
        in_specs=in_specs, out_specs=out_blk, out_shape=_sds((M, N), out_dtype),
        compiler_params=_cparams("parallel", "parallel"),
    )(*args)


def _wgrad_wide_a(a, b, name, tm=512):
    T, M = a.shape
    N = b.shape[1]
    tm = _fit(M, tm)

    def body(a_ref, b_ref, out_ref):
        out_ref[...] = lax.dot_general(a_ref[...], b_ref[...], (((0,), (0,)), ((), ())),
                                       preferred_element_type=F32).astype(out_ref.dtype)

    return pl.pallas_call(
        body, name=name, grid=(M // tm,),
        in_specs=[pl.BlockSpec((T, tm), lambda i: (0, i)),
                  pl.BlockSpec((T, N), lambda i: (0, 0), pipeline_mode=pl.Buffered(1))],
        out_specs=pl.BlockSpec((tm, N), lambda i: (i, 0)), out_shape=_sds((M, N), MXU_DTYPE),
        compiler_params=_cparams("parallel"),
    )(a, b)


def _wgrad_wide_b(a, b, name, groups=None, tn=512, t_chunk=512):
    T, M = a.shape
    N = b.shape[1]
    tn = _fit(N if groups is None else N // groups, tn)
    t_chunk = _fit(T, t_chunk)

    def body(a_ref, b_ref, out_ref, at_ref):
        @pl.when(pl.program_id(0) == 0)
        def _():
            for c in range(0, T, t_chunk):
                at_ref[:, c:c + t_chunk] = a_ref[c:c + t_chunk, :].T

        out_ref[...] = jnp.dot(at_ref[...], b_ref[...], preferred_element_type=F32).astype(out_ref.dtype)

    if groups is None:
        out_spec = pl.BlockSpec((M, tn), lambda j: (0, j))
        out_shape = _sds((M, N), MXU_DTYPE)
    else:
        per = N // groups // tn
        out_spec = pl.BlockSpec((None, M, tn), lambda j: (j // per, 0, j % per))
        out_shape = _sds((groups, M, N // groups), MXU_DTYPE)
    return pl.pallas_call(
        body, name=name, grid=(N // tn,),
        in_specs=[pl.BlockSpec((T, M), lambda j: (0, 0), pipeline_mode=pl.Buffered(1)),
                  pl.BlockSpec((T, tn), lambda j: (0, j))],
        out_specs=out_spec, out_shape=out_shape,
        scratch_shapes=[pltpu.VMEM((M, T), MXU_DTYPE)],
        compiler_params=_cparams("arbitrary"),
    )(a, b)


SGU_ROWS = 512


def _sgu_mixed(v, w_ref, b_ref, n_heads):
    parts = [jnp.dot(w_ref[h], v[:, h * HEAD_DIM:(h + 1) * HEAD_DIM], preferred_element_type=F32) for h in range(n_heads)]
    return jnp.concatenate(parts, axis=1) + b_ref[...]


def _sgu_fwd(p, w_tril, bmat, name):
    T = p.shape[0]
    H = w_tril.shape[0]
    AW = H * HEAD_DIM
    rows = _fit(T, SGU_ROWS)

    def body(u_ref, v_ref, w_ref, b_ref, y_ref):
        for c in range(0, rows, CHUNK):
            ch = pl.ds(c, CHUNK)
            mixed = _sgu_mixed(v_ref[ch, :].astype(MXU_DTYPE), w_ref, b_ref, H)
            y_ref[ch, :] = (u_ref[ch, :].astype(F32) * mixed).astype(y_ref.dtype)

    const3 = lambda c: (0, 0, 0)
    return pl.pallas_call(
        body, name=name, grid=(T // rows,),
        in_specs=[pl.BlockSpec((rows, AW), lambda c: (c, 0)), pl.BlockSpec((rows, AW), lambda c: (c, 1)),
                  pl.BlockSpec((H, CHUNK, CHUNK), const3), pl.BlockSpec((CHUNK, AW), lambda c: (0, 0))],
        out_specs=pl.BlockSpec((rows, AW), lambda c: (c, 0)),
        out_shape=_sds((T, AW), MXU_DTYPE),
        compiler_params=_cparams("parallel"),
    )(p, p, w_tril, bmat)


def _sgu_bwd(dymix, p, w_tril, w_tril_t, bmat, name):
    T = p.shape[0]
    H = w_tril.shape[0]
    AW = H * HEAD_DIM
    rows = _fit(T, SGU_ROWS)

    def body(dy_ref, u_ref, v_ref, w_ref, wt_ref, b_ref, du_ref, dv_ref, dw_ref, db_ref):
        @pl.when(pl.program_id(0) == 0)
        def _():
            dw_ref[...] = jnp.zeros_like(dw_ref)
            db_ref[...] = jnp.zeros_like(db_ref)

        for c in range(0, rows, CHUNK):
            ch = pl.ds(c, CHUNK)
            v = v_ref[ch, :].astype(MXU_DTYPE)
            dy = dy_ref[ch, :].astype(F32)
            du_ref[ch, :] = (dy * _sgu_mixed(v, w_ref, b_ref, H)).astype(du_ref.dtype)
            dm = dy * u_ref[ch, :].astype(F32)
            db_ref[...] += dm
            dm_c = dm.astype(MXU_DTYPE)
            dv = []
            for h in range(H):
                sl = slice(h * HEAD_DIM, (h + 1) * HEAD_DIM)
                dv.append(jnp.dot(wt_ref[h], dm_c[:, sl], preferred_element_type=F32))
                dw_ref[h] += lax.dot_general(dm_c[:, sl], v[:, sl], (((1,), (1,)), ((), ())), preferred_element_type=F32)
            dv_ref[ch, :] = jnp.concatenate(dv, axis=1).astype(dv_ref.dtype)

    const3 = lambda c: (0, 0, 0)
    blk = pl.BlockSpec((rows, AW), lambda c: (c, 0))
    return pl.pallas_call(
        body, name=name, grid=(T // rows,),
        in_specs=[blk, blk, pl.BlockSpec((rows, AW), lambda c: (c, 1)),
                  pl.BlockSpec((H, CHUNK, CHUNK), const3), pl.BlockSpec((H, CHUNK, CHUNK), const3),
                  pl.BlockSpec((CHUNK, AW), lambda c: (0, 0))],
        out_specs=[blk, blk, pl.BlockSpec((H, CHUNK, CHUNK), const3), pl.BlockSpec((CHUNK, AW), lambda c: (0, 0))],
        out_shape=[_sds((T, AW), MXU_DTYPE), _sds((T, AW), MXU_DTYPE), _sds((H, CHUNK, CHUNK), F32), _sds((CHUNK, AW), F32)],
        compiler_params=_cparams("arbitrary"),
    )(dymix, p, p, w_tril, w_tril_t, bmat)


def _shift_down(z, s, row):
    return jnp.where(row >= s, pltpu.roll(z, s, 0), 0.0)


def _shift_up(z, s, row, T):
    return jnp.where(row < T - s, pltpu.roll(z, T - s, 0), 0.0)


def _conv_fwd(p, w_conv, AW, name):
    T = p.shape[0]
    BW = w_conv.shape[1]
    nb = BW // LANE
    b0 = 2 * AW // LANE

    def body(b_ref, c_ref, x_ref, w_ref, y_ref):
        row = lax.broadcasted_iota(jnp.int32, (T, LANE), 0)
        z = c_ref[...].astype(F32) * x_ref[...].astype(F32)
        w0, w1, w2 = w_ref[0:1, :], w_ref[1:2, :], w_ref[2:3, :]
        conv = w2 * z + w1 * _shift_down(z, 1, row) + w0 * _shift_down(z, 2, row)
        y_ref[...] = (b_ref[...].astype(F32) * conv).astype(y_ref.dtype)

    return pl.pallas_call(
        body, name=name, grid=(nb,),
        in_specs=[pl.BlockSpec((T, LANE), lambda j: (0, b0 + j)), pl.BlockSpec((T, LANE), lambda j: (0, b0 + nb + j)),
                  pl.BlockSpec((T, LANE), lambda j: (0, b0 + 2 * nb + j)), pl.BlockSpec((CONV_WIDTH, LANE), lambda j: (0, j))],
        out_specs=pl.BlockSpec((T, LANE), lambda j: (0, j)),
        out_shape=_sds((T, BW), MXU_DTYPE),
        compiler_params=_cparams("parallel"),
    )(p, p, p, w_conv)


def _conv_bwd(dymix, p, w_conv, AW, name):
    T = p.shape[0]
    BW = w_conv.shape[1]
    nb = BW // LANE
    b0 = 2 * AW // LANE
    y0 = AW // LANE

    def body(dy_ref, b_ref, c_ref, x_ref, w_ref, db_ref, dc_ref, dxb_ref, dw_ref):
        row = lax.broadcasted_iota(jnp.int32, (T, LANE), 0)
        cv, xv, dy = c_ref[...].astype(F32), x_ref[...].astype(F32), dy_ref[...].astype(F32)
        w0, w1, w2 = w_ref[0:1, :], w_ref[1:2, :], w_ref[2:3, :]
        z = cv * xv
        z1 = _shift_down(z, 1, row)
        z2 = _shift_down(z, 2, row)
        conv = w2 * z + w1 * z1 + w0 * z2
        db_ref[...] = (dy * conv).astype(db_ref.dtype)
        dconv = dy * b_ref[...].astype(F32)
        dz = w2 * dconv + w1 * _shift_up(dconv, 1, row, T) + w0 * _shift_up(dconv, 2, row, T)
        dc_ref[...] = (dz * xv).astype(dc_ref.dtype)
        dxb_ref[...] = (dz * cv).astype(dxb_ref.dtype)
        dw_ref[0:1, :] = jnp.sum(dconv * z2, axis=0, keepdims=True)
        dw_ref[1:2, :] = jnp.sum(dconv * z1, axis=0, keepdims=True)
        dw_ref[2:3, :] = jnp.sum(dconv * z, axis=0, keepdims=True)

    col = lambda j: (0, j)
    return pl.pallas_call(
        body, name=name, grid=(nb,),
        in_specs=[pl.BlockSpec((T, LANE), lambda j: (0, y0 + j)),
                  pl.BlockSpec((T, LANE), lambda j: (0, b0 + j)), pl.BlockSpec((T, LANE), lambda j: (0, b0 + nb + j)),
                  pl.BlockSpec((T, LANE), lambda j: (0, b0 + 2 * nb + j)), pl.BlockSpec((CONV_WIDTH, LANE), col)],
        out_specs=[pl.BlockSpec((T, LANE), col)] * 3 + [pl.BlockSpec((CONV_WIDTH, LANE), col)],
        out_shape=[_sds((T, BW), MXU_DTYPE)] * 3 + [_sds((CONV_WIDTH, BW), F32)],
        compiler_params=_cparams("parallel"),
    )(dymix, p, p, p, w_conv)


def _head_sum(x, col_head, n_heads):
    out = jnp.zeros_like(x)
    for h in range(n_heads):
        sel = col_head == h
        out = jnp.where(sel, jnp.sum(jnp.where(sel, x, 0.0), axis=-1, keepdims=True), out)
    return out


def _same_head(width):
    assert width == 2 * HEAD_DIM
    return lax.broadcasted_iota(jnp.int32, (1, width), 1) < HEAD_DIM


def _head_sum2(x, first):
    s0 = jnp.sum(jnp.where(first, x, 0.0), axis=-1, keepdims=True)
    s1 = jnp.sum(jnp.where(first, 0.0, x), axis=-1, keepdims=True)
    return jnp.where(first, s0, s1)


def _head_norm(x, g, first):
    r = lax.rsqrt(_head_sum2(x * x, first) * (1.0 / HEAD_DIM) + EPS)
    return x * r * g, r


def _head_norm_bwd(dy, x, g, r, first):
    gdy = dy * g
    mean_xg = _head_sum2(x * gdy, first) * (1.0 / HEAD_DIM)
    return r * gdy - x * (r * r * r) * mean_xg, dy * x * r


ATT_SPAN_MIN = 512
ATT_FWD_UNROLL = 4
ATT_BWD_UNROLL = 4
HEADS_PER_LANES = LANE // HEAD_DIM


def _attn_geometry(T, d):
    m = max(1, ATT_SPAN_MIN // (ATT_BLK * d))
    return m, ATT_BLK * d * m, ATT_BLK * d, T // (ATT_BLK * d)


def _rows(ref, start, d):
    return ref[pl.ds(start, ATT_BLK, stride=d), :] if d > 1 else ref[pl.ds(start, ATT_BLK), :]


def _set_rows(ref, start, d, value):
    if d > 1:
        ref[pl.ds(start, ATT_BLK, stride=d), :] = value
    else:
        ref[pl.ds(start, ATT_BLK), :] = value


def _for_each_block(d, m, task, unroll):
    for j in range(m):
        if d == 1:
            task(0, j)
        else:
            lax.fori_loop(0, d, lambda r, carry, j=j: (task(r, j), carry)[1], 0, unroll=min(unroll, d))


def _head_slices():
    return [slice(h * HEAD_DIM, (h + 1) * HEAD_DIM) for h in range(HEADS_PER_LANES)]


def _qk_norm_fwd(p, gains, q_start, PW, name, tr=1024):
    T = p.shape[0]
    tr = _fit(T, tr)
    n_norm = gains.shape[1] // PW
    n = n_norm * 3 // 2
    c0 = q_start // PW

    def body(*refs):
        x_refs, g_ref, out_ref = refs[:n], refs[n], refs[n + 1]
        first = _same_head(LANE)
        for i in range(n):
            for c in range(0, PW, LANE):
                lo = i * PW + c
                x = x_refs[i][:, c:c + LANE].astype(F32)
                out_ref[:, lo:lo + LANE] = _head_norm(x, g_ref[:, lo:lo + LANE], first)[0] if i < n_norm else x

    return pl.pallas_call(
        body, name=name, grid=(T // tr,),
        in_specs=[pl.BlockSpec((tr, PW), lambda i, j=j: (i, c0 + j)) for j in range(n)]
        + [pl.BlockSpec((1, n_norm * PW), lambda i: (0, 0))],
        out_specs=pl.BlockSpec((tr, n * PW), lambda i: (i, 0)), out_shape=_sds((T, n * PW), F32),
        compiler_params=_cparams("parallel"),
    )(*([p] * n), gains)


def _qk_norm_bwd(dns, p, gains, q_start, PW, name, tr=1024):
    T = p.shape[0]
    tr = _fit(T, tr)
    n = len(dns)
    c0 = q_start // PW

    def body(*refs):
        d_refs, x_refs, g_ref, out_ref, acc_ref = refs[:n], refs[n:2 * n], refs[2 * n], refs[2 * n + 1], refs[2 * n + 2]

        @pl.when(pl.program_id(0) == 0)
        def _():
            acc_ref[...] = jnp.zeros_like(acc_ref)

        first = _same_head(LANE)
        for i in range(n):
            for c in range(0, PW, LANE):
                lo = i * PW + c
                x, gv = x_refs[i][:, c:c + LANE].astype(F32), g_ref[:, lo:lo + LANE]
                _, r = _head_norm(x, gv, first)
                dx, g_part = _head_norm_bwd(d_refs[i][:, c:c + LANE], x, gv, r, first)
                out_ref[:, lo:lo + LANE] = dx.astype(out_ref.dtype)
                acc_ref[0:1, lo:lo + LANE] += jnp.sum(g_part, axis=0, keepdims=True)

    return pl.pallas_call(
        body, name=name, grid=(T // tr,),
        in_specs=[pl.BlockSpec((tr, PW), lambda i: (i, 0))] * n
        + [pl.BlockSpec((tr, PW), lambda i, j=j: (i, c0 + j)) for j in range(n)]
        + [pl.BlockSpec((1, n * PW), lambda i: (0, 0))],
        out_specs=[pl.BlockSpec((tr, n * PW), lambda i: (i, 0)), pl.BlockSpec((8, n * PW), lambda i: (0, 0))],
        out_shape=[_sds((T, n * PW), MXU_DTYPE), _sds((8, n * PW), F32)],
        compiler_params=_cparams("arbitrary"),
    )(*dns, *([p] * n), gains)


def _attn_fwd(qkv, g, d, PW, name):
    T = qkv.shape[0]
    B, W = ATT_BLK, LANE
    m, span, group, _ = _attn_geometry(T, d)
    c_q = g * PW // W
    c_k, c_v = c_q + 3 * PW // W, c_q + 6 * PW // W
    scale = HEAD_DIM ** -0.5

    def body(q_ref, k_ref, v_ref, kp_ref, vp_ref, o_ref, lse_ref):
        n = pl.program_id(1)
        qi = lax.broadcasted_iota(jnp.int32, (B, 2 * B), 0)
        kj = lax.broadcasted_iota(jnp.int32, (B, 2 * B), 1)
        band = (kj >= qi) & (kj <= qi + B)

        def task(r, j):
            cur = j * group + r
            if j == 0:
                kp, vp = _rows(kp_ref, r, d), _rows(vp_ref, r, d)
            else:
                kp, vp = _rows(k_ref, cur - group, d), _rows(v_ref, cur - group, d)
            mask = band & ((n * m + j > 0) | (kj >= B))
            qn = _rows(q_ref, cur, d).astype(MXU_DTYPE)
            kn = jnp.concatenate([kp, _rows(k_ref, cur, d)], axis=0).astype(MXU_DTYPE)
            vcat = jnp.concatenate([vp, _rows(v_ref, cur, d)], axis=0).astype(MXU_DTYPE)
            o_parts, lse_parts = [], []
            for sl in _head_slices():
                s = lax.dot_general(qn[:, sl], kn[:, sl], (((1,), (1,)), ((), ())), preferred_element_type=F32) * scale
                s = jnp.where(mask, s, NEG_INF)
                mx = jnp.max(s, axis=-1, keepdims=True)
                e = jnp.exp(s - mx)
                den = jnp.sum(e, axis=-1, keepdims=True)
                o_parts.append(jnp.dot(e.astype(MXU_DTYPE), vcat[:, sl], preferred_element_type=F32) / den)
                lse_parts.append(jnp.broadcast_to(mx + jnp.log(den), (B, HEAD_DIM)))
            _set_rows(o_ref, cur, d, jnp.concatenate(o_parts, axis=1))
            _set_rows(lse_ref, cur, d, jnp.concatenate(lse_parts, axis=1))

        _for_each_block(d, m, task, ATT_FWD_UNROLL)

    main = lambda c0: pl.BlockSpec((span, W), lambda hp, n: (n, c0 + hp))
    prev = lambda c0: pl.BlockSpec((group, W), lambda hp, n: (jnp.maximum(n * m - 1, 0), c0 + hp))
    out_blk = pl.BlockSpec((span, W), lambda hp, n: (n, hp))
    return pl.pallas_call(
        body, name=name, grid=(PW // W, T // span),
        in_specs=[main(c_q), main(c_k), main(c_v), prev(c_k), prev(c_v)],
        out_specs=[out_blk, out_blk],
        out_shape=[_sds((T, PW), F32), _sds((T, PW), F32)],
        compiler_params=_cparams("parallel", "parallel"),
    )(qkv, qkv, qkv, qkv, qkv)


def _attn_bwd(qkv, lse, do, corr, g, d, PW, name):
    T = qkv.shape[0]
    B, W = ATT_BLK, LANE
    m, span, group, n_blocks = _attn_geometry(T, d)
    c_q = g * PW // W
    c_k, c_v = c_q + 3 * PW // W, c_q + 6 * PW // W
    scale = HEAD_DIM ** -0.5
    nt = (((1,), (1,)), ((), ()))
    tn = (((0,), (0,)), ((), ()))

    def body(q_ref, k_ref, v_ref, do_ref, l_ref, c_ref, kp_ref, vp_ref, qx_ref, dox_ref, lx_ref, cx_ref,
             dq_ref, dk_ref, dv_ref):
        n = pl.program_id(1)
        i1 = lax.broadcasted_iota(jnp.int32, (B, B), 0)
        j1 = lax.broadcasted_iota(jnp.int32, (B, B), 1)
        i2 = lax.broadcasted_iota(jnp.int32, (2 * B, B), 0)
        j2 = lax.broadcasted_iota(jnp.int32, (2 * B, B), 1)

        def task(r, j):
            cur = j * group + r
            blk = n * m + j
            q_c, k_c, v_c = _rows(q_ref, cur, d), _rows(k_ref, cur, d), _rows(v_ref, cur, d)
            do_c, l_c, c_c = _rows(do_ref, cur, d), _rows(l_ref, cur, d), _rows(c_ref, cur, d)
            if j == 0:
                k_p, v_p = _rows(kp_ref, r, d), _rows(vp_ref, r, d)
            else:
                k_p, v_p = _rows(k_ref, cur - group, d), _rows(v_ref, cur - group, d)
            if j == m - 1:
                nxt = [_rows(ref, r, d) for ref in (qx_ref, dox_ref, lx_ref, cx_ref)]
            else:
                nxt = [_rows(ref, cur + group, d) for ref in (q_ref, do_ref, l_ref, c_ref)]
            q_x, do_x, l_x, c_x = nxt
            kn_c, kn_p, v_c, v_p = (a.astype(MXU_DTYPE) for a in (k_c, k_p, v_c, v_p))
            qn_c = q_c.astype(MXU_DTYPE)
            qn_cat = jnp.concatenate([qn_c, q_x.astype(MXU_DTYPE)], axis=0)
            do_cb = do_c.astype(MXU_DTYPE)
            do_cat = jnp.concatenate([do_cb, do_x.astype(MXU_DTYPE)], axis=0)
            l_cat = jnp.concatenate([l_c, l_x], axis=0)
            c_cat = jnp.concatenate([c_c, c_x], axis=0)
            mask_p = (j1 >= i1) & (blk > 0)
            mask_c = ((i2 < B) & (j2 <= i2)) | ((i2 >= B) & (j2 >= i2 - B) & (blk + 1 < n_blocks))
            dqn, dkn, dv = [], [], []
            for h, sl in enumerate(_head_slices()):
                lane = slice(h * HEAD_DIM, h * HEAD_DIM + 1)
                s_p = lax.dot_general(qn_c[:, sl], kn_p[:, sl], nt, preferred_element_type=F32) * scale
                pr_p = jnp.where(mask_p, jnp.exp(s_p - l_c[:, lane]), 0.0)
                dp_p = lax.dot_general(do_cb[:, sl], v_p[:, sl], nt, preferred_element_type=F32)
                ds_p = (pr_p * (dp_p + c_c[:, lane]) * scale).astype(MXU_DTYPE)
                s_c = lax.dot_general(qn_cat[:, sl], kn_c[:, sl], nt, preferred_element_type=F32) * scale
                pr_c = jnp.where(mask_c, jnp.exp(s_c - l_cat[:, lane]), 0.0)
                dp_c = lax.dot_general(do_cat[:, sl], v_c[:, sl], nt, preferred_element_type=F32)
                ds_c = (pr_c * (dp_c + c_cat[:, lane]) * scale).astype(MXU_DTYPE)
                dqn.append(jnp.dot(ds_p, kn_p[:, sl], preferred_element_type=F32)
                           + jnp.dot(ds_c[:B], kn_c[:, sl], preferred_element_type=F32))
                dkn.append(lax.dot_general(ds_c, qn_cat[:, sl], tn, preferred_element_type=F32))
                dv.append(lax.dot_general(pr_c.astype(MXU_DTYPE), do_cat[:, sl], tn, preferred_element_type=F32))
            _set_rows(dq_ref, cur, d, jnp.concatenate(dqn, axis=1))
            _set_rows(dk_ref, cur, d, jnp.concatenate(dkn, axis=1))
            _set_rows(dv_ref, cur, d, jnp.concatenate(dv, axis=1))

        _for_each_block(d, m, task, ATT_BWD_UNROLL)

    main = lambda c0: pl.BlockSpec((span, W), lambda hp, n: (n, c0 + hp))
    prev = lambda c0: pl.BlockSpec((group, W), lambda hp, n: (jnp.maximum(n * m - 1, 0), c0 + hp))
    nxt = lambda c0: pl.BlockSpec((group, W), lambda hp, n: (jnp.minimum((n + 1) * m, n_blocks - 1), c0 + hp))
    own = pl.BlockSpec((span, W), lambda hp, n: (n, hp))
    return pl.pallas_call(
        body, name=name, grid=(PW // W, T // span),
        in_specs=[main(c_q), main(c_k), main(c_v), main(0), main(0), main(0), prev(c_k), prev(c_v),
                  nxt(c_q), nxt(0), nxt(0), nxt(0)],
        out_specs=[own, own, own],
        out_shape=[_sds((T, PW), F32)] * 3,
        compiler_params=_cparams("parallel", "parallel"),
    )(qkv, qkv, qkv, do, lse, corr, qkv, qkv, qkv, do, lse, corr)


def _softmax3(lses):
    mx = jnp.maximum(jnp.maximum(lses[0], lses[1]), lses[2])
    ex = [jnp.exp(l - mx) for l in lses]
    inv = 1.0 / (ex[0] + ex[1] + ex[2])
    return [e * inv for e in ex]


def _mix_fwd(os_, lses, name, tr=1024):
    T, PW = os_[0].shape
    tr = _fit(T, tr)

    def body(o0, o1, o2, l0, l1, l2, y_ref):
        alpha = _softmax3([l0[...], l1[...], l2[...]])
        for g, o_ref in enumerate((o0, o1, o2)):
            y_ref[:, g * PW:(g + 1) * PW] = (o_ref[...] * alpha[g]).astype(y_ref.dtype)

    blk = pl.BlockSpec((tr, PW), lambda i: (i, 0))
    return pl.pallas_call(
        body, name=name, grid=(T // tr,),
        in_specs=[blk] * 6, out_specs=pl.BlockSpec((tr, 3 * PW), lambda i: (i, 0)),
        out_shape=_sds((T, 3 * PW), MXU_DTYPE),
        compiler_params=_cparams("parallel"),
    )(*os_, *lses)


def _mix_bwd(dymix, os_, lses, c_start, name, tr=1024):
    T, PW = os_[0].shape
    tr = _fit(T, tr)
    HP = PW // HEAD_DIM
    c0 = c_start // PW

    def body(d0, d1, d2, o0, o1, o2, l0, l1, l2, do0, do1, do2, dl0, dl1, dl2):
        col_head = lax.broadcasted_iota(jnp.int32, (tr, PW), 1) // HEAD_DIM
        alpha = _softmax3([l0[...], l1[...], l2[...]])
        dys = [d0[...].astype(F32), d1[...].astype(F32), d2[...].astype(F32)]
        dots = [_head_sum(dy * o_ref[...], col_head, HP) for dy, o_ref in zip(dys, (o0, o1, o2))]
        mean_dot = alpha[0] * dots[0] + alpha[1] * dots[1] + alpha[2] * dots[2]
        for g, (do_ref, dl_ref) in enumerate(((do0, dl0), (do1, dl1), (do2, dl2))):
            do_ref[...] = dys[g] * alpha[g]
            dl_ref[...] = -alpha[g] * mean_dot

    blk = pl.BlockSpec((tr, PW), lambda i: (i, 0))
    dy_specs = [pl.BlockSpec((tr, PW), lambda i, g=g: (i, c0 + g)) for g in range(3)]
    outs = pl.pallas_call(
        body, name=name, grid=(T // tr,),
        in_specs=dy_specs + [blk] * 6, out_specs=[blk] * 6,
        out_shape=[_sds((T, PW), F32)] * 6,
        compiler_params=_cparams("parallel"),
    )(dymix, dymix, dymix, *os_, *lses)
    return outs[:3], outs[3:]


def _adamw_math(w, g, m, v):
    m2 = ADAM_B1 * m + (1.0 - ADAM_B1) * g
    v2 = ADAM_B2 * v + (1.0 - ADAM_B2) * (g * g)
    m_hat = m2 / (1.0 - ADAM_B1 ** ADAM_STEP)
    v_hat = v2 / (1.0 - ADAM_B2 ** ADAM_STEP)
    delta = -ADAM_LR * (m_hat / (jnp.sqrt(v_hat) + ADAM_EPS) + ADAM_WD * w)
    return delta, m2, v2


def _adamw_layer(layer, w, m, v, own, landed, me, prev, name, tr=256):
    _, R, C = w.shape
    tr = next(t for t in range(min(tr, R) // 16 * 16, 0, -16) if R % t == 0)

    def body(me_ref, w_ref, m_ref, v_ref, own_ref, land_ref, *rest):
        g_ref, d_ref, m2_ref, v2_ref = rest[-4:]
        g = own_ref[...].astype(F32)
        for j in range(N_PEER):
            g = g + land_ref[j].astype(F32)
        delta, m2, v2 = _adamw_math(w_ref[...], g, m_ref[...], v_ref[...])
        g_ref[...] = g
        d_ref[...] = delta
        m2_ref[...] = m2
        v2_ref[...] = v2

    lay = pl.BlockSpec((None, tr, C), lambda i, me_ref: (layer, i, 0))
    in_specs = [lay, lay, lay, pl.BlockSpec((None, tr, C), lambda i, me_ref: (me_ref[0], i, 0)),
                pl.BlockSpec((N_PEER, tr, C), lambda i, me_ref: (0, i, 0))]
    args = [me, w, m, v, own, landed]
    aliases = {}
    if prev is not None:
        in_specs += [pl.BlockSpec(memory_space=pl.ANY)] * 4
        args += list(prev)
        aliases = {6 + i: i for i in range(4)}
    return pl.pallas_call(
        body, name=name,
        grid_spec=pltpu.PrefetchScalarGridSpec(num_scalar_prefetch=1, grid=(R // tr,), in_specs=in_specs, out_specs=[lay] * 4),
        out_shape=[_sds(w.shape, F32)] * 4,
        input_output_aliases=aliases,
        compiler_params=_cparams("parallel"),
    )(*args)


def _sum_parts(parts, name):
    _, R, C = parts.shape

    def body(p_ref, out_ref):
        g = p_ref[0]
        for j in range(1, N_DEV):
            g = g + p_ref[j]
        out_ref[...] = g

    return pl.pallas_call(
        body, name=name, grid=(1,),
        in_specs=[pl.BlockSpec((N_DEV, R, C), lambda i: (0, 0, 0))], out_specs=pl.BlockSpec((R, C), lambda i: (0, 0)),
        out_shape=_sds((R, C), F32), compiler_params=_cparams("arbitrary"),
    )(parts)


def _adamw_small(ws, gs, ms, vs, name):
    n = len(ws)

    def body(*refs):
        ins, outs = refs[:4 * n], refs[4 * n:]
        for i in range(n):
            delta, m2, v2 = _adamw_math(ins[i][...], ins[n + i][...], ins[2 * n + i][...], ins[3 * n + i][...])
            outs[i][...] = delta
            outs[n + i][...] = m2
            outs[2 * n + i][...] = v2

    outs = pl.pallas_call(
        body, name=name, out_shape=[_sds(w.shape, F32) for w in ws] * 3,
    )(*ws, *gs, *ms, *vs)
    return outs[:n], outs[n:2 * n], outs[2 * n:]


def _pack(arrays, rows_multiple=8):
    flat = []
    for a in arrays:
        a = a.reshape(-1).astype(F32)
        flat.append(jnp.pad(a, (0, (-a.shape[0]) % LANE)))
    flat = jnp.concatenate(flat)
    flat = jnp.pad(flat, (0, (-flat.shape[0]) % (LANE * rows_multiple)))
    return flat.reshape(-1, LANE)


def _unpack(packed, shapes):
    flat = packed.reshape(-1)
    out, off = [], 0
    for s in shapes:
        size = 1
        for dim in s:
            size *= dim
        out.append(flat[off:off + size].reshape(s))
        off += size + (-size) % LANE
    return out


def _layer_fwd(x, wts, getw, dims, dep=None, loss_target=None):
    AW, BW, PW, DP = dims["AW"], dims["BW"], dims["PW"], dims["DP"]
    q_start = 2 * AW + 3 * BW
    h, r1 = _rmsnorm_fwd(x, wts["attn_norm"], "rmsnorm_fwd", dep=dep)
    p = _mm_nt(h, getw("w_in", h), "proj_in", out_dtype=MXU_DTYPE, tm=1024, tn=1408)
    y_a = _sgu_fwd(p, wts["sgu_tril"], wts["sgu_bmat"], "sgu_fwd")
    y_b = _conv_fwd(p, getw("conv_w", y_a), AW, "conv_fwd")
    qkn = _qk_norm_fwd(p, wts["qk_gain"], q_start, PW, "qk_norm_fwd")
    os_, lses = [], []
    for g, d in enumerate(DILATIONS):
        o, lse = _attn_fwd(qkn, g, d, PW, "attn_fwd_%d" % d)
        os_.append(o)
        lses.append(lse)
    y_c = _mix_fwd(os_, lses, "mix_fwd")
    ymix = jnp.concatenate([y_a, y_b, y_c], axis=1)
    x1, h2, r2 = _mm_residual_norm(ymix, getw("w_out", ymix), x, wts["mlp_norm"], "proj_out")
    a, hid = _mm_nn(h2, getw("w_mlp_in", h2), "mlp_in", out_dtype=MXU_DTYPE, relu2=True, tm=1024, tn=1024)
    if loss_target is None:
        x2 = _mm_nn(hid, getw("w_mlp_out", hid), "mlp_out", residual=x1, columns_first=True, tm=512, tn=1024)
    else:
        x2 = _mm_residual_loss(hid, getw("w_mlp_out", hid), x1, loss_target, "mlp_out_loss")
    saved = dict(x=x, h=h, r1=r1, p=p, qkn=qkn, os=os_, lses=lses, ymix=ymix, x1=x1, h2=h2, r2=r2, a=a, hid=hid)
    return x2, saved


def _layer_bwd(dx, dxb, wts, getw, scatter, saved, dims):
    AW, BW, PW, DP = dims["AW"], dims["BW"], dims["PW"], dims["DP"]
    q_start = 2 * AW + 3 * BW
    D = dx.shape[1]
    g_w2 = _wgrad_wide_a(saved["hid"], dxb, "mlp_out_wgrad")
    token = scatter("w_mlp_out", g_w2.reshape(N_DEV, -1, D))
    da = _mm_nt(dxb, getw("w_mlp_out", None), "mlp_out_dgrad", out_dtype=MXU_DTYPE, relu2_pre=saved["a"], dep=token,
                tm=1024, tn=1024)
    g_w1 = _wgrad_wide_b(saved["h2"], da, "mlp_in_wgrad", groups=N_DEV)
    token = scatter("w_mlp_in", g_w1)
    dx1, dx1b, g_mlp_norm = _mm_norm_bwd(da, getw("w_mlp_in", None), saved["x1"], wts["mlp_norm"], saved["r2"], dx,
                                         "mlp_in_dgrad", dep=token)
    g_wout = _wgrad_wide_b(saved["ymix"], dx1b, "proj_out_wgrad")
    token = scatter("w_out", g_wout.reshape(N_DEV, -1, D))
    dymix = _mm_nt(dx1b, getw("w_out", None), "proj_out_dgrad", out_dtype=MXU_DTYPE, dep=token, tm=1024, tn=1024)
    p = saved["p"]
    du, dv, g_sgu_w, g_sgu_bmat = _sgu_bwd(dymix, p, wts["sgu_tril"], wts["sgu_tril_t"], wts["sgu_bmat"], "sgu_bwd")
    d_b, d_c, d_xb, g_conv = _conv_bwd(dymix, p, getw("conv_w", None), AW, "conv_bwd")
    dos, corrs = _mix_bwd(dymix, saved["os"], saved["lses"], AW + BW, "mix_bwd")
    dqns, dkns, dvs = [], [], []
    for g, d in enumerate(DILATIONS):
        dqn, dkn, dvv = _attn_bwd(saved["qkn"], saved["lses"][g], dos[g], corrs[g], g, d, PW, "attn_bwd_%d" % d)
        dqns.append(dqn)
        dkns.append(dkn)
        dvs.append(dvv.astype(MXU_DTYPE))
    dqk, g_qk = _qk_norm_bwd(dqns + dkns, p, wts["qk_gain"], q_start, PW, "qk_norm_bwd")
    g_q, g_k = (part.reshape(-1, HEAD_DIM).sum(0) for part in jnp.split(g_qk[0], 2))
    dp = jnp.concatenate([du, dv, d_b, d_c, d_xb, dqk] + dvs, axis=1)
    g_win_t = _wgrad_wide_a(dp, saved["h"], "proj_in_wgrad")
    token = scatter("w_in", g_win_t.reshape(N_DEV, DP // N_DEV, D))
    dx0, dx0b, g_attn_norm = _mm_norm_bwd(dp, getw("w_in", None), saved["x"], wts["attn_norm"], saved["r1"], dx1,
                                          "proj_in_dgrad", dep=token)
    H = AW // HEAD_DIM
    tril = jnp.tril(jnp.ones((CHUNK, CHUNK), F32))
    small = [g_attn_norm.reshape(-1), g_sgu_w * tril, g_sgu_bmat.reshape(CHUNK, H, HEAD_DIM).sum(-1).T,
             g_conv, g_q, g_k, g_mlp_norm.reshape(-1)]
    return dx0, dx0b, small


def kernel(x, attn_norm, w_in, sgu_w, sgu_b, conv_w, q_norm, k_norm, w_out, mlp_norm, w_mlp_in, w_mlp_out, loss_target, m_attn_norm, m_w_in, m_sgu_w, m_sgu_b, m_conv_w, m_q_norm, m_k_norm, m_w_out, m_mlp_norm, m_w_mlp_in, m_w_mlp_out, v_attn_norm, v_w_in, v_sgu_w, v_sgu_b, v_conv_w, v_q_norm, v_k_norm, v_w_out, v_mlp_norm, v_w_mlp_in, v_w_mlp_out):
    n_layers = attn_norm.shape[0]
    T, D = x.shape[1], x.shape[2]
    H = sgu_w.shape[1]
    AW = H * HEAD_DIM
    BW = conv_w.shape[2] * N_DEV
    DP = w_in.shape[2] * N_DEV
    DMIX = w_out.shape[1] * N_DEV
    DFF = w_mlp_in.shape[2] * N_DEV
    PW = (DMIX - AW - BW) // 3
    HP = PW // HEAD_DIM
    dims = dict(AW=AW, BW=BW, PW=PW, DP=DP)
    me = 4 * lax.axis_index("x") + 2 * lax.axis_index("y") + lax.axis_index("c")

    big_names = ("w_in", "w_out", "w_mlp_in", "w_mlp_out")
    tr_in = lambda a: jnp.swapaxes(a, 1, 2)
    big_w = dict(zip(big_names, (tr_in(w_in), w_out, w_mlp_in, w_mlp_out)))
    big_m = dict(zip(big_names, (tr_in(m_w_in), m_w_out, m_w_mlp_in, m_w_mlp_out)))
    big_v = dict(zip(big_names, (tr_in(v_w_in), v_w_out, v_w_mlp_in, v_w_mlp_out)))

    keys = []
    for l in range(n_layers):
        keys += [(l, nm) for nm in big_names]
    keys.insert(1, (0, "conv_w"))
    first_src = big_w[keys[0][1]][keys[0][0]].astype(MXU_DTYPE)
    first_flights, first_token = _exchange_start([first_src], [_own_in_place(first_src, me)], "gather", name="gather_start_first")
    zero = first_token[0, 0]
    srcs = [_pack([conv_w]) + zero if nm == "conv_w" else (big_w[nm][l] + zero).astype(MXU_DTYPE) for l, nm in keys[1:]]
    flights, gather_token = _exchange_start(srcs, [_own_in_place(s, me) for s in srcs], "gather", name="gather_start")
    arriving = dict(zip(keys, first_flights + flights))
    forwarding = {}
    relayout = dict(
        w_in=lambda g: g.reshape(DP, D), w_out=lambda g: g.reshape(DMIX, D),
        w_mlp_in=lambda g: g, w_mlp_out=lambda g: g.reshape(DFF, D),
        conv_w=lambda g: jnp.stack([_unpack(g[j], [conv_w.shape])[0] for j in range(N_DEV)], axis=2).reshape(
            n_layers, CONV_WIDTH, BW))
    gathered = {}

    def forward(key, after):
        _, land = _exchange_wait(arriving[key], after, "gather", name="gather_arrive_%d_%s" % key)
        fl, token = _exchange_start(None, [land], "forward", name="gather_forward_%d_%s" % key)
        forwarding[key] = fl[0]
        return token

    def weight_getter(l):
        def getw(nm, after):
            key = (0, nm) if nm == "conv_w" else (l, nm)
            if key not in gathered:
                ahead = keys[keys.index(key):][:2]
                for k in ahead:
                    if k not in forwarding:
                        after = forward(k, after)
                _, land = _exchange_wait(forwarding[key], after, "forward", name="gather_wait_%d_%s" % key)
                gathered[key] = relayout[nm](land)
            return gathered[key][l] if nm == "conv_w" else gathered[key]
        return getw

    tril = jnp.tril(jnp.ones((CHUNK, CHUNK), F32))
    layers = []
    for l in range(n_layers):
        w_tril = sgu_w[l] * tril
        layers.append(dict(
            attn_norm=attn_norm[l][None], mlp_norm=mlp_norm[l][None],
            sgu_tril=w_tril.astype(MXU_DTYPE), sgu_tril_t=w_tril.transpose(0, 2, 1).astype(MXU_DTYPE),
            sgu_bmat=jnp.repeat(sgu_b[l].T, HEAD_DIM, axis=1),
            qk_gain=jnp.concatenate([jnp.tile(q_norm[l], 3 * HP), jnp.tile(k_norm[l], 3 * HP)])[None]))

    xs = x[0]
    saved = []
    for l in range(n_layers):
        xs, sv = _layer_fwd(xs, layers[l], weight_getter(l), dims, dep=gather_token if l == 0 else None,
                            loss_target=loss_target[0] if l == n_layers - 1 else None)
        saved.append(sv)
    loss_blk, dx, dxb = xs
    loss = lax.psum(loss_blk[0, 0], ("x", "y", "c"))

    scattering = {}

    def scatter_starter(l):
        def scatter(nm, partials):
            land = lax.empty((N_PEER,) + partials.shape[1:], partials.dtype)
            fl, tok = _exchange_start([partials], [land], "scatter", name="scatter_start_%d_%s" % (l, nm))
            scattering[(l, nm)] = fl[0]
            return tok
        return scatter

    small = [None] * n_layers
    for l in reversed(range(n_layers)):
        dx, dxb, small[l] = _layer_bwd(dx, dxb, layers[l], weight_getter(l), scatter_starter(l), saved[l], dims)

    small_shapes = [s.shape for s in small[0]]
    small_src = [_pack([s for l in range(n_layers) for s in small[l]])]
    small_flights, small_token = _exchange_start(small_src, [_own_in_place(s, me) for s in small_src], "gather_all",
                                                 name="small_start")
    grad_x = dx[None]

    me1 = me.astype(jnp.int32).reshape(1)
    res = {nm: None for nm in big_names}
    after = small_token
    for l in reversed(range(n_layers)):
        for nm in reversed(big_names):
            own, landed = _exchange_wait(scattering[(l, nm)], after, "scatter", name="scatter_wait_%d_%s" % (l, nm))
            res[nm] = _adamw_layer(l, big_w[nm], big_m[nm], big_v[nm], own, landed, me1, res[nm], "adamw_" + nm)
            after = res[nm][0]
    res["w_in"] = [tr_in(a) for a in res["w_in"]]
    big_out = [res[nm] for nm in big_names]

    _, gathered_small = _exchange_wait(small_flights[0], after, "gather_all", name="small_wait")
    summed = _unpack(_sum_parts(gathered_small, "sum_small"), small_shapes * n_layers)
    ns = len(small_shapes)
    g_small = [jnp.stack([summed[l * ns + i] for l in range(n_layers)]) for i in range(ns)]
    g_attn_norm, g_sgu_w, g_sgu_b, g_conv_full, g_q, g_k, g_mlp_norm = g_small
    cs = conv_w.shape[2]
    g_conv = lax.dynamic_slice_in_dim(g_conv_full, me * cs, cs, axis=2)
    sm_w = (attn_norm, sgu_w, sgu_b, conv_w, q_norm, k_norm, mlp_norm)
    sm_m = (m_attn_norm, m_sgu_w, m_sgu_b, m_conv_w, m_q_norm, m_k_norm, m_mlp_norm)
    sm_v = (v_attn_norm, v_sgu_w, v_sgu_b, v_conv_w, v_q_norm, v_k_norm, v_mlp_norm)
    sm_g = (g_attn_norm, g_sgu_w, g_sgu_b, g_conv, g_q, g_k, g_mlp_norm)
    sm_delta, sm_m2, sm_v2 = _adamw_small(sm_w, sm_g, sm_m, sm_v, "adamw_small")

    def ordered(small_list, big_kind):
        b = [big_out[i][big_kind] for i in range(4)]
        return [small_list[0], b[0], small_list[1], small_list[2], small_list[3], small_list[4], small_list[5],
                b[1], small_list[6], b[2], b[3]]

    return (loss, grad_x, *ordered(list(sm_g), 0), *ordered(sm_delta, 1), *ordered(sm_m2, 2), *ordered(sm_v2, 3))
```

```python
import jax
import jax.numpy as jnp
from jax import lax
from jax.experimental import pallas as pl
from jax.experimental.pallas import tpu as pltpu

N_DEV = 8
HEAD_DIM = 64
CHUNK = 128
ATT_BLK = 128
DILATIONS = (1, 4, 16)
CONV_WIDTH = 3
EPS = 1e-6
ADAM_LR = 0.001
ADAM_B1 = 0.9
ADAM_B2 = 0.999
ADAM_EPS = 1e-08
ADAM_WD = 0.01
ADAM_STEP = 10
MXU_DTYPE = jnp.bfloat16
F32 = jnp.float32
LANE = 128
VMEM_PHYSICAL_BYTES = 64 * 1024 * 1024
VMEM_LIMIT_BYTES = 56 * 1024 * 1024
NEG_INF = float("-inf")


def _cparams(*sem):
    return pltpu.CompilerParams(dimension_semantics=sem, vmem_limit_bytes=VMEM_LIMIT_BYTES)


def _sds(shape, dtype):
    return jax.ShapeDtypeStruct(shape, dtype)


def _fit(n, tile):
    for t in range(min(tile, n) // LANE * LANE, 0, -LANE):
        if n % t == 0:
            return t
    return n


_HBM = pl.BlockSpec(memory_space=pltpu.HBM)
_SEM = pl.BlockSpec(memory_space=pltpu.SEMAPHORE)
_DATAFLOW = pltpu.SideEffectType.DATAFLOW_SIDE_EFFECTING
N_PEER = N_DEV - 1


def _mesh_pos():
    x, y, c = lax.axis_index("x"), lax.axis_index("y"), lax.axis_index("c")
    return x, y, c, 4 * x + 2 * y + c


OTHER_CHIPS = (4, 2, 6)
EXCHANGE_PEERS = dict(
    scatter=tuple(range(1, N_DEV)),
    gather_all=tuple(range(1, N_DEV)),
    gather=(1,) + OTHER_CHIPS,
    forward=OTHER_CHIPS)


def _remote_copies(src, land, send_sems, recv_sems, mode):
    x, y, c, me = _mesh_pos()
    copies = []
    for i, k in enumerate(EXCHANGE_PEERS[mode]):
        px = (1 - x) if (k & 4) else x
        py = (1 - y) if (k & 2) else y
        pc = (1 - c) if (k & 1) else c
        if mode == "scatter":
            src_ref, dst_ref, dev = src.at[4 * px + 2 * py + pc], land.at[i], (px, py, pc)
        elif mode == "forward":
            slot = 4 * px + 2 * py + c
            src_ref, dst_ref, dev = land.at[slot], land.at[slot], (x, y, 1 - c)
        else:
            src_ref, dst_ref, dev = src, land.at[me], (px, py, pc)
        copies.append(pltpu.make_async_remote_copy(
            src_ref=src_ref, dst_ref=dst_ref, send_sem=send_sems.at[i], recv_sem=recv_sems.at[i],
            device_id=dev, device_id_type=pl.DeviceIdType.MESH))
    return copies


def _own_in_place(src, me):
    land = lax.empty((N_DEV,) + src.shape, src.dtype)
    return lax.dynamic_update_slice(land, src[None], (me,) + (0,) * src.ndim)


def _exchange_start(srcs, lands, mode, name):
    n = len(lands)
    has_src = srcs is not None
    arrays = (list(srcs) if has_src else []) + list(lands)
    n_arr = len(arrays)
    n_copies = len(EXCHANGE_PEERS[mode])

    def body(*refs):
        src = refs[:n] if has_src else [None] * n
        land = refs[n_arr - n:n_arr]
        send, recv = refs[n_arr:n_arr + n], refs[n_arr + n:n_arr + 2 * n]
        token = refs[2 * n_arr + 2 * n]
        for t in range(n):
            for cp in _remote_copies(src[t], land[t], send[t], recv[t], mode):
                cp.start()
        token[...] = jnp.zeros_like(token)

    outs = pl.pallas_call(
        body, name=name,
        out_shape=([pltpu.SemaphoreType.DMA((n_copies,))] * (2 * n) + [pltpu.HBM(a.shape, a.dtype) for a in arrays]
                   + [_sds((8, LANE), F32)]),
        in_specs=[_HBM] * n_arr,
        out_specs=[_SEM] * (2 * n) + [_HBM] * n_arr + [pl.BlockSpec(memory_space=pltpu.VMEM)],
        input_output_aliases={i: 2 * n + i for i in range(n_arr)},
        compiler_params=pltpu.CompilerParams(has_side_effects=_DATAFLOW),
    )(*[pltpu.with_memory_space_constraint(a, pltpu.HBM) for a in arrays])
    thru = outs[2 * n:2 * n + n_arr]
    flights = [(outs[t], outs[n + t], thru[t] if has_src else None, thru[n_arr - n + t]) for t in range(n)]
    return flights, outs[2 * n + n_arr]


def _exchange_wait(flight, after, mode, name):
    send, recv, src, land = flight
    arrays = [land] if src is None else [src, land]
    n_arr = len(arrays)

    def body(*refs):
        src_ref = refs[0] if n_arr == 2 else None
        land_ref, send_ref, recv_ref = refs[n_arr - 1], refs[n_arr], refs[n_arr + 1]
        for cp in _remote_copies(src_ref, land_ref, send_ref, recv_ref, mode):
            cp.wait_send()
            cp.wait_recv()

    outs = pl.pallas_call(
        body, name=name, out_shape=[pltpu.HBM(a.shape, a.dtype) for a in arrays],
        in_specs=[_HBM] * n_arr + [_SEM, _SEM, pl.BlockSpec(memory_space=pl.ANY)], out_specs=[_HBM] * n_arr,
        input_output_aliases={i: i for i in range(n_arr)},
        compiler_params=pltpu.CompilerParams(has_side_effects=_DATAFLOW),
    )(*arrays, send, recv, after)
    return (None, outs[0]) if src is None else (outs[0], outs[1])


def _rmsnorm_fwd(x, g, name, dep=None, tr=1024):
    T, D = x.shape
    tr = _fit(T, tr)

    def body(x_ref, g_ref, *rest):
        h_ref, r_ref = rest[-2:]
        xv = x_ref[...]
        r = lax.rsqrt(jnp.mean(xv * xv, axis=-1, keepdims=True) + EPS)
        h_ref[...] = (xv * r * g_ref[...]).astype(h_ref.dtype)
        r_ref[...] = r

    in_specs = [pl.BlockSpec((tr, D), lambda i: (i, 0)), pl.BlockSpec((1, D), lambda i: (0, 0))]
    args = [x, g]
    if dep is not None:
        in_specs.append(pl.BlockSpec(dep.shape, lambda i: (0, 0)))
        args.append(dep)
    return pl.pallas_call(
        body, name=name, grid=(T // tr,),
        in_specs=in_specs,
        out_specs=[pl.BlockSpec((tr, D), lambda i: (i, 0)), pl.BlockSpec((tr, 1), lambda i: (i, 0))],
        out_shape=[_sds((T, D), MXU_DTYPE), _sds((T, 1), F32)],
        compiler_params=_cparams("parallel"),
    )(*args)


def _rmsnorm_bwd(dh, x, g, r, dres, name, tr=512):
    T, D = x.shape
    tr = _fit(T, tr)

    def body(dh_ref, x_ref, g_ref, r_ref, dres_ref, dx_ref, dxb_ref, dg_ref):
        @pl.when(pl.program_id(0) == 0)
        def _():
            dg_ref[...] = jnp.zeros_like(dg_ref)

        dh_v, xv, rv = dh_ref[...].astype(F32), x_ref[...], r_ref[...]
        gdy = dh_v * g_ref[...]
        mean_xg = jnp.mean(xv * gdy, axis=-1, keepdims=True)
        dx = dres_ref[...] + rv * gdy - xv * (rv * rv * rv) * mean_xg
        dx_ref[...] = dx
        dxb_ref[...] = dx.astype(dxb_ref.dtype)
        dg_ref[...] += jnp.sum(dh_v * xv * rv, axis=0, keepdims=True)

    row = lambda i: (i, 0)
    return pl.pallas_call(
        body, name=name, grid=(T // tr,),
        in_specs=[pl.BlockSpec((tr, D), row), pl.BlockSpec((tr, D), row), pl.BlockSpec((1, D), lambda i: (0, 0)),
                  pl.BlockSpec((tr, 1), row), pl.BlockSpec((tr, D), row)],
        out_specs=[pl.BlockSpec((tr, D), row), pl.BlockSpec((tr, D), row), pl.BlockSpec((1, D), lambda i: (0, 0))],
        out_shape=[_sds((T, D), F32), _sds((T, D), MXU_DTYPE), _sds((1, D), F32)],
        compiler_params=_cparams("arbitrary"),
    )(dh, x, g, r, dres)


def _mm_nn(a, b, name, out_dtype=F32, residual=None, relu2=False, dep=None, columns_first=False, tm=512, tn=512):
    M, K = a.shape
    grouped = b.ndim == 3
    N = b.shape[0] * b.shape[2] if grouped else b.shape[1]
    tm, tn = _fit(M, tm), _fit(b.shape[2] if grouped else N, tn)
    tile = (lambda j, i: (i, j)) if columns_first else (lambda i, j: (i, j))
    b_mode = pl.Buffered(1 if columns_first else 2)
    if grouped:
        per = b.shape[2] // tn
        b_spec = pl.BlockSpec((None, K, tn), lambda *g: (tile(*g)[1] // per, 0, tile(*g)[1] % per), pipeline_mode=b_mode)
    else:
        b_spec = pl.BlockSpec((K, tn), lambda *g: (0, tile(*g)[1]), pipeline_mode=b_mode)
    n_out = 2 if relu2 else 1

    def body(*refs):
        a_ref, b_ref = refs[0], refs[1]
        r_ref = refs[2] if residual is not None else None
        outs = refs[2 + (residual is not None) + (dep is not None):]
        acc = jnp.dot(a_ref[...], b_ref[...], preferred_element_type=F32)
        if r_ref is not None:
            acc = acc + r_ref[...]
        outs[0][...] = acc.astype(outs[0].dtype)
        if relu2:
            rl = jnp.maximum(acc, 0.0)
            outs[1][...] = (rl * rl).astype(outs[1].dtype)

    out_blk = pl.BlockSpec((tm, tn), lambda *g: tile(*g))
    in_specs = [pl.BlockSpec((tm, K), lambda *g: (tile(*g)[0], 0)), b_spec]
    args = [a, b]
    if residual is not None:
        in_specs.append(out_blk)
        args.append(residual)
    if dep is not None:
        in_specs.append(pl.BlockSpec(dep.shape, lambda *g: (0, 0)))
        args.append(dep)
    outs = pl.pallas_call(
        body, name=name, grid=(N // tn, M // tm) if columns_first else (M // tm, N // tn),
        in_specs=in_specs, out_specs=[out_blk] * n_out, out_shape=[_sds((M, N), out_dtype)] * n_out,
        compiler_params=_cparams("parallel", "parallel"),
    )(*args)
    return outs if relu2 else outs[0]


def _column_offsets(pieces):
    offs = [0]
    for piece in pieces:
        offs.append(offs[-1] + piece.shape[1])
    return offs


def _mm_residual_norm(a_pieces, b, residual, g, name, tm=512):
    n = len(a_pieces)
    M = a_pieces[0].shape[0]
    K, N = b.shape
    offs = _column_offsets(a_pieces)
    assert offs[-1] == K
    tm = _fit(M, tm)

    def body(*refs):
        a_refs, b_ref, res_ref, g_ref, x_ref, h_ref, r_ref = refs[:n], *refs[n:]
        xv = res_ref[...]
        for i in range(n):
            xv = xv + jnp.dot(a_refs[i][...], b_ref[offs[i]:offs[i + 1], :], preferred_element_type=F32)
        x_ref[...] = xv
        r = lax.rsqrt(jnp.mean(xv * xv, axis=-1, keepdims=True) + EPS)
        h_ref[...] = (xv * r * g_ref[...]).astype(h_ref.dtype)
        r_ref[...] = r

    row = pl.BlockSpec((tm, N), lambda i: (i, 0))
    return pl.pallas_call(
        body, name=name, grid=(M // tm,),
        in_specs=[pl.BlockSpec((tm, piece.shape[1]), lambda i: (i, 0)) for piece in a_pieces]
        + [pl.BlockSpec((K, N), lambda i: (0, 0), pipeline_mode=pl.Buffered(1)), row, pl.BlockSpec((1, N), lambda i: (0, 0))],
        out_specs=[row, row, pl.BlockSpec((tm, 1), lambda i: (i, 0))],
        out_shape=[_sds((M, N), F32), _sds((M, N), MXU_DTYPE), _sds((M, 1), F32)],
        compiler_params=_cparams("parallel"),
    )(*a_pieces, b, residual, g)


def _mm_norm_bwd(a, b, x, g, r, dres, name, dep=None, tm=256):
    M, K = a.shape
    grouped = b.ndim == 3
    N = b.shape[1]
    tm = _fit(M, tm)
    nt = (((1,), (1,)), ((), ()))
    f32_row, mxu_row = tm * N * 4, tm * N * jnp.dtype(MXU_DTYPE).itemsize
    need = b.size * b.dtype.itemsize + 2 * (tm * K * a.dtype.itemsize + 3 * f32_row + mxu_row) + 5 * f32_row
    vmem_limit = max(VMEM_LIMIT_BYTES, min(need, VMEM_PHYSICAL_BYTES - 2 * 1024 * 1024))

    def body(*refs):
        a_ref, b_ref, x_ref, g_ref, r_ref, dres_ref = refs[:6]
        dx_ref, dxb_ref, dg_ref = refs[-3:]

        @pl.when(pl.program_id(0) == 0)
        def _():
            dg_ref[...] = jnp.zeros_like(dg_ref)

        if grouped:
            kg = b.shape[2]
            dh = lax.dot_general(a_ref[:, 0:kg], b_ref[0], nt, preferred_element_type=F32)
            for i in range(1, b.shape[0]):
                dh += lax.dot_general(a_ref[:, i * kg:(i + 1) * kg], b_ref[i], nt, preferred_element_type=F32)
        else:
            dh = jnp.dot(a_ref[...], b_ref[...], preferred_element_type=F32)
        xv, rv = x_ref[...], r_ref[...]
        gdy = dh * g_ref[...]
        mean_xg = jnp.mean(xv * gdy, axis=-1, keepdims=True)
        dx = dres_ref[...] + rv * gdy - xv * (rv * rv * rv) * mean_xg
        dx_ref[...] = dx
        dxb_ref[...] = dx.astype(dxb_ref.dtype)
        dg_ref[...] += jnp.sum(dh * xv * rv, axis=0, keepdims=True)

    row = pl.BlockSpec((tm, N), lambda i: (i, 0))
    in_specs = [pl.BlockSpec((tm, K), lambda i: (i, 0)),
                pl.BlockSpec(b.shape, lambda i: (0,) * b.ndim, pipeline_mode=pl.Buffered(1)),
                row, pl.BlockSpec((1, N), lambda i: (0, 0)), pl.BlockSpec((tm, 1), lambda i: (i, 0)), row]
    args = [a, b, x, g, r, dres]
    if dep is not None:
        in_specs.append(pl.BlockSpec(dep.shape, lambda i: (0, 0)))
        args.append(dep)
    return pl.pallas_call(
        body, name=name, grid=(M // tm,),
        in_specs=in_specs, out_specs=[row, row, pl.BlockSpec((1, N), lambda i: (0, 0))],
        out_shape=[_sds((M, N), F32), _sds((M, N), MXU_DTYPE), _sds((1, N), F32)],
        compiler_params=pltpu.CompilerParams(dimension_semantics=("arbitrary",), vmem_limit_bytes=vmem_limit),
    )(*args)


def _mm_residual_loss(a, b, residual, target, name, tm=512, tn=1024):
    M, K = a.shape
    N = b.shape[1]
    tm, tn = _fit(M, tm), _fit(N, tn)

    def body(a_ref, b_ref, res_ref, t_ref, loss_ref, dx_ref, dxb_ref):
        @pl.when((pl.program_id(0) == 0) & (pl.program_id(1) == 0))
        def _():
            loss_ref[...] = jnp.zeros_like(loss_ref)

        err = jnp.dot(a_ref[...], b_ref[...], preferred_element_type=F32) + res_ref[...] - t_ref[...]
        sq = jnp.sum(jnp.sum(err * err, axis=-1, keepdims=True), axis=0, keepdims=True)
        loss_ref[...] += (0.5 / N) * sq
        dx = err * (1.0 / N)
        dx_ref[...] = dx
        dxb_ref[...] = dx.astype(dxb_ref.dtype)

    blk = pl.BlockSpec((tm, tn), lambda j, i: (i, j))
    return pl.pallas_call(
        body, name=name, grid=(N // tn, M // tm),
        in_specs=[pl.BlockSpec((tm, K), lambda j, i: (i, 0)),
                  pl.BlockSpec((K, tn), lambda j, i: (0, j), pipeline_mode=pl.Buffered(1)), blk, blk],
        out_specs=[pl.BlockSpec((8, LANE), lambda j, i: (0, 0)), blk, blk],
        out_shape=[_sds((8, LANE), F32), _sds((M, N), F32), _sds((M, N), MXU_DTYPE)],
        compiler_params=_cparams("arbitrary", "arbitrary"),
    )(a, b, residual, target)


def _mm_nt(a, b, name, out_dtype=F32, relu2_pre=None, dep=None, tm=512, tn=512):
    M, K = a.shape
    grouped = b.ndim == 3
    N = b.shape[1] if grouped else b.shape[0]
    tm, tn = _fit(M, tm), _fit(N, tn)
    nt = (((1,), (1,)), ((), ()))
    if grouped:
        G, _, Kg = b.shape
        b_spec = pl.BlockSpec((G, tn, Kg), lambda i, j: (0, j, 0))
    else:
        b_spec = pl.BlockSpec((tn, K), lambda i, j: (j, 0))

    def body(*refs):
        a_ref, b_ref = refs[0], refs[1]
        p_ref = refs[2] if relu2_pre is not None else None
        out_ref = refs[2 + (relu2_pre is not None) + (dep is not None)]
        if grouped:
            acc = lax.dot_general(a_ref[:, 0:Kg], b_ref[0], nt, preferred_element_type=F32)
            for g in range(1, G):
                acc += lax.dot_general(a_ref[:, g * Kg:(g + 1) * Kg], b_ref[g], nt, preferred_element_type=F32)
        else:
            acc = lax.dot_general(a_ref[...], b_ref[...], nt, preferred_element_type=F32)
        if p_ref is not None:
            acc = acc * (2.0 * jnp.maximum(p_ref[...].astype(F32), 0.0))
        out_ref[...] = acc.astype(out_ref.dtype)

    out_blk = pl.BlockSpec((tm, tn), lambda i, j: (i, j))
    in_specs = [pl.BlockSpec((tm, K), lambda i, j: (i, 0)), b_spec]
    args = [a, b]
    if relu2_pre is not None:
        in_specs.append(out_blk)
        args.append(relu2_pre)
    if dep is not None:
        in_specs.append(pl.BlockSpec(dep.shape, lambda i, j: (0, 0)))
        args.append(dep)
    return pl.pallas_call(
        body, name=name, grid=(M // tm, N // tn),
        in_specs=in_specs, out_specs=out_blk, out_shape=_sds((M, N), out_dtype),
        compiler_params=_cparams("parallel", "parallel"),
    )(*args)


def _wgrad_wide_a(a, b, name, tm=512):
    T, M = a.shape
    N = b.shape[1]
    tm = _fit(M, tm)

    def body(a_ref, b_ref, out_ref):
        out_ref[...] = lax.dot_general(a_ref[...], b_ref[...], (((0,), (0,)), ((), ())),
                                       preferred_element_type=F32).astype(out_ref.dtype)

    return pl.pallas_call(
        body, name=name, grid=(M // tm,),
        in_specs=[pl.BlockSpec((T, tm), lambda i: (0, i)),
                  pl.BlockSpec((T, N), lambda i: (0, 0), pipeline_mode=pl.Buffered(1))],
        out_specs=pl.BlockSpec((tm, N), lambda i: (i, 0)), out_shape=_sds((M, N), MXU_DTYPE),
        compiler_params=_cparams("parallel"),
    )(a, b)


def _wgrad_wide_b(a_pieces, b, name, groups=None, tn=512, t_chunk=512):
    n = len(a_pieces)
    T = a_pieces[0].shape[0]
    offs = _column_offsets(a_pieces)
    M = offs[-1]
    N = b.shape[1]
    tn = _fit(N if groups is None else N // groups, tn)
    t_chunk = _fit(T, t_chunk)

    def body(*refs):
        a_refs, b_ref, out_ref, at_ref = refs[:n], *refs[n:]

        @pl.when(pl.program_id(0) == 0)
        def _():
            for i in range(n):
                for c in range(0, T, t_chunk):
                    at_ref[offs[i]:offs[i + 1], c:c + t_chunk] = a_refs[i][c:c + t_chunk, :].T

        out_ref[...] = jnp.dot(at_ref[...], b_ref[...], preferred_element_type=F32).astype(out_ref.dtype)

    if groups is None:
        out_spec = pl.BlockSpec((M, tn), lambda j: (0, j))
        out_shape = _sds((M, N), MXU_DTYPE)
    else:
        per = N // groups // tn
        out_spec = pl.BlockSpec((None, M, tn), lambda j: (j // per, 0, j % per))
        out_shape = _sds((groups, M, N // groups), MXU_DTYPE)
    return pl.pallas_call(
        body, name=name, grid=(N // tn,),
        in_specs=[pl.BlockSpec(piece.shape, lambda j: (0, 0), pipeline_mode=pl.Buffered(1)) for piece in a_pieces]
        + [pl.BlockSpec((T, tn), lambda j: (0, j))],
        out_specs=out_spec, out_shape=out_shape,
        scratch_shapes=[pltpu.VMEM((M, T), MXU_DTYPE)],
        compiler_params=_cparams("arbitrary"),
    )(*a_pieces, b)


SGU_ROWS = 512


def _sgu_mixed(v, w_ref, b_ref, n_heads):
    parts = [jnp.dot(w_ref[h], v[:, h * HEAD_DIM:(h + 1) * HEAD_DIM], preferred_element_type=F32) for h in range(n_heads)]
    return jnp.concatenate(parts, axis=1) + b_ref[...]


def _sgu_fwd(p, w_tril, bmat, name):
    T = p.shape[0]
    H = w_tril.shape[0]
    AW = H * HEAD_DIM
    rows = _fit(T, SGU_ROWS)

    def body(u_ref, v_ref, w_ref, b_ref, y_ref):
        for c in range(0, rows, CHUNK):
            ch = pl.ds(c, CHUNK)
            mixed = _sgu_mixed(v_ref[ch, :].astype(MXU_DTYPE), w_ref, b_ref, H)
            y_ref[ch, :] = (u_ref[ch, :].astype(F32) * mixed).astype(y_ref.dtype)

    const3 = lambda c: (0, 0, 0)
    return pl.pallas_call(
        body, name=name, grid=(T // rows,),
        in_specs=[pl.BlockSpec((rows, AW), lambda c: (c, 0)), pl.BlockSpec((rows, AW), lambda c: (c, 1)),
                  pl.BlockSpec((H, CHUNK, CHUNK), const3), pl.BlockSpec((CHUNK, AW), lambda c: (0, 0))],
        out_specs=pl.BlockSpec((rows, AW), lambda c: (c, 0)),
        out_shape=_sds((T, AW), MXU_DTYPE),
        compiler_params=_cparams("parallel"),
    )(p, p, w_tril, bmat)


def _sgu_bwd(dymix, p, w_tril, w_tril_t, bmat, name):
    T = p.shape[0]
    H = w_tril.shape[0]
    AW = H * HEAD_DIM
    rows = _fit(T, SGU_ROWS)

    def body(dy_ref, u_ref, v_ref, w_ref, wt_ref, b_ref, du_ref, dv_ref, dw_ref, db_ref):
        @pl.when(pl.program_id(0) == 0)
        def _():
            dw_ref[...] = jnp.zeros_like(dw_ref)
            db_ref[...] = jnp.zeros_like(db_ref)

        for c in range(0, rows, CHUNK):
            ch = pl.ds(c, CHUNK)
            v = v_ref[ch, :].astype(MXU_DTYPE)
            dy = dy_ref[ch, :].astype(F32)
            du_ref[ch, :] = (dy * _sgu_mixed(v, w_ref, b_ref, H)).astype(du_ref.dtype)
            dm = dy * u_ref[ch, :].astype(F32)
            db_ref[...] += dm
            dm_c = dm.astype(MXU_DTYPE)
            dv = []
            for h in range(H):
                sl = slice(h * HEAD_DIM, (h + 1) * HEAD_DIM)
                dv.append(jnp.dot(wt_ref[h], dm_c[:, sl], preferred_element_type=F32))
                dw_ref[h] += lax.dot_general(dm_c[:, sl], v[:, sl], (((1,), (1,)), ((), ())), preferred_element_type=F32)
            dv_ref[ch, :] = jnp.concatenate(dv, axis=1).astype(dv_ref.dtype)

    const3 = lambda c: (0, 0, 0)
    blk = pl.BlockSpec((rows, AW), lambda c: (c, 0))
    return pl.pallas_call(
        body, name=name, grid=(T // rows,),
        in_specs=[blk, blk, pl.BlockSpec((rows, AW), lambda c: (c, 1)),
                  pl.BlockSpec((H, CHUNK, CHUNK), const3), pl.BlockSpec((H, CHUNK, CHUNK), const3),
                  pl.BlockSpec((CHUNK, AW), lambda c: (0, 0))],
        out_specs=[blk, blk, pl.BlockSpec((H, CHUNK, CHUNK), const3), pl.BlockSpec((CHUNK, AW), lambda c: (0, 0))],
        out_shape=[_sds((T, AW), MXU_DTYPE), _sds((T, AW), MXU_DTYPE), _sds((H, CHUNK, CHUNK), F32), _sds((CHUNK, AW), F32)],
        compiler_params=_cparams("arbitrary"),
    )(dymix, p, p, w_tril, w_tril_t, bmat)


def _shift_down(z, s, row):
    return jnp.where(row >= s, pltpu.roll(z, s, 0), 0.0)


def _shift_up(z, s, row, T):
    return jnp.where(row < T - s, pltpu.roll(z, T - s, 0), 0.0)


def _conv_fwd(p, w_conv, AW, name):
    T = p.shape[0]
    BW = w_conv.shape[1]
    nb = BW // LANE
    b0 = 2 * AW // LANE

    def body(b_ref, c_ref, x_ref, w_ref, y_ref):
        row = lax.broadcasted_iota(jnp.int32, (T, LANE), 0)
        z = c_ref[...].astype(F32) * x_ref[...].astype(F32)
        w0, w1, w2 = w_ref[0:1, :], w_ref[1:2, :], w_ref[2:3, :]
        conv = w2 * z + w1 * _shift_down(z, 1, row) + w0 * _shift_down(z, 2, row)
        y_ref[...] = (b_ref[...].astype(F32) * conv).astype(y_ref.dtype)

    return pl.pallas_call(
        body, name=name, grid=(nb,),
        in_specs=[pl.BlockSpec((T, LANE), lambda j: (0, b0 + j)), pl.BlockSpec((T, LANE), lambda j: (0, b0 + nb + j)),
                  pl.BlockSpec((T, LANE), lambda j: (0, b0 + 2 * nb + j)), pl.BlockSpec((CONV_WIDTH, LANE), lambda j: (0, j))],
        out_specs=pl.BlockSpec((T, LANE), lambda j: (0, j)),
        out_shape=_sds((T, BW), MXU_DTYPE),
        compiler_params=_cparams("parallel"),
    )(p, p, p, w_conv)


def _conv_bwd(dymix, p, w_conv, AW, name):
    T = p.shape[0]
    BW = w_conv.shape[1]
    nb = BW // LANE
    b0 = 2 * AW // LANE
    y0 = AW // LANE

    def body(dy_ref, b_ref, c_ref, x_ref, w_ref, db_ref, dc_ref, dxb_ref, dw_ref):
        row = lax.broadcasted_iota(jnp.int32, (T, LANE), 0)
        cv, xv, dy = c_ref[...].astype(F32), x_ref[...].astype(F32), dy_ref[...].astype(F32)
        w0, w1, w2 = w_ref[0:1, :], w_ref[1:2, :], w_ref[2:3, :]
        z = cv * xv
        z1 = _shift_down(z, 1, row)
        z2 = _shift_down(z, 2, row)
        conv = w2 * z + w1 * z1 + w0 * z2
        db_ref[...] = (dy * conv).astype(db_ref.dtype)
        dconv = dy * b_ref[...].astype(F32)
        dz = w2 * dconv + w1 * _shift_up(dconv, 1, row, T) + w0 * _shift_up(dconv, 2, row, T)
        dc_ref[...] = (dz * xv).astype(dc_ref.dtype)
        dxb_ref[...] = (dz * cv).astype(dxb_ref.dtype)
        dw_ref[0:1, :] = jnp.sum(dconv * z2, axis=0, keepdims=True)
        dw_ref[1:2, :] = jnp.sum(dconv * z1, axis=0, keepdims=True)
        dw_ref[2:3, :] = jnp.sum(dconv * z, axis=0, keepdims=True)

    col = lambda j: (0, j)
    return pl.pallas_call(
        body, name=name, grid=(nb,),
        in_specs=[pl.BlockSpec((T, LANE), lambda j: (0, y0 + j)),
                  pl.BlockSpec((T, LANE), lambda j: (0, b0 + j)), pl.BlockSpec((T, LANE), lambda j: (0, b0 + nb + j)),
                  pl.BlockSpec((T, LANE), lambda j: (0, b0 + 2 * nb + j)), pl.BlockSpec((CONV_WIDTH, LANE), col)],
        out_specs=[pl.BlockSpec((T, LANE), col)] * 3 + [pl.BlockSpec((CONV_WIDTH, LANE), col)],
        out_shape=[_sds((T, BW), MXU_DTYPE)] * 3 + [_sds((CONV_WIDTH, BW), F32)],
        compiler_params=_cparams("parallel"),
    )(dymix, p, p, p, w_conv)


def _head_sum(x, col_head, n_heads):
    out = jnp.zeros_like(x)
    for h in range(n_heads):
        sel = col_head == h
        out = jnp.where(sel, jnp.sum(jnp.where(sel, x, 0.0), axis=-1, keepdims=True), out)
    return out


def _same_head(width):
    assert width == 2 * HEAD_DIM
    return lax.broadcasted_iota(jnp.int32, (1, width), 1) < HEAD_DIM


def _head_sum2(x, first):
    s0 = jnp.sum(jnp.where(first, x, 0.0), axis=-1, keepdims=True)
    s1 = jnp.sum(jnp.where(first, 0.0, x), axis=-1, keepdims=True)
    return jnp.where(first, s0, s1)


def _head_norm(x, g, first):
    r = lax.rsqrt(_head_sum2(x * x, first) * (1.0 / HEAD_DIM) + EPS)
    return x * r * g, r


def _head_norm_bwd(dy, x, g, r, first):
    gdy = dy * g
    mean_xg = _head_sum2(x * gdy, first) * (1.0 / HEAD_DIM)
    return r * gdy - x * (r * r * r) * mean_xg, dy * x * r


ATT_SPAN_MIN = 512
ATT_FWD_UNROLL = 4
ATT_BWD_UNROLL = 4
HEADS_PER_LANES = LANE // HEAD_DIM


def _attn_geometry(T, d):
    m = max(1, ATT_SPAN_MIN // (ATT_BLK * d))
    return m, ATT_BLK * d * m, ATT_BLK * d, T // (ATT_BLK * d)


def _rows(ref, start, d):
    return ref[pl.ds(start, ATT_BLK, stride=d), :] if d > 1 else ref[pl.ds(start, ATT_BLK), :]


def _set_rows(ref, start, d, value):
    if d > 1:
        ref[pl.ds(start, ATT_BLK, stride=d), :] = value
    else:
        ref[pl.ds(start, ATT_BLK), :] = value


def _for_each_block(d, m, task, unroll):
    for j in range(m):
        if d == 1:
            task(0, j)
        else:
            lax.fori_loop(0, d, lambda r, carry, j=j: (task(r, j), carry)[1], 0, unroll=min(unroll, d))


def _head_slices():
    return [slice(h * HEAD_DIM, (h + 1) * HEAD_DIM) for h in range(HEADS_PER_LANES)]


def _qk_norm_fwd(p, gains, q_start, PW, name, tr=1024):
    T = p.shape[0]
    tr = _fit(T, tr)
    n_norm = gains.shape[1] // PW
    n = n_norm * 3 // 2
    c0 = q_start // PW

    def body(*refs):
        x_refs, g_ref, out_ref = refs[:n], refs[n], refs[n + 1]
        first = _same_head(LANE)
        for i in range(n):
            for c in range(0, PW, LANE):
                lo = i * PW + c
                x = x_refs[i][:, c:c + LANE].astype(F32)
                out_ref[:, lo:lo + LANE] = _head_norm(x, g_ref[:, lo:lo + LANE], first)[0] if i < n_norm else x

    return pl.pallas_call(
        body, name=name, grid=(T // tr,),
        in_specs=[pl.BlockSpec((tr, PW), lambda i, j=j: (i, c0 + j)) for j in range(n)]
        + [pl.BlockSpec((1, n_norm * PW), lambda i: (0, 0))],
        out_specs=pl.BlockSpec((tr, n * PW), lambda i: (i, 0)), out_shape=_sds((T, n * PW), F32),
        compiler_params=_cparams("parallel"),
    )(*([p] * n), gains)


def _qk_norm_bwd(dns, p, gains, q_start, PW, name, tr=1024):
    T = p.shape[0]
    tr = _fit(T, tr)
    n = len(dns)
    c0 = q_start // PW

    def body(*refs):
        d_refs, x_refs, g_ref, out_ref, acc_ref = refs[:n], refs[n:2 * n], refs[2 * n], refs[2 * n + 1], refs[2 * n + 2]

        @pl.when(pl.program_id(0) == 0)
        def _():
            acc_ref[...] = jnp.zeros_like(acc_ref)

        first = _same_head(LANE)
        for i in range(n):
            for c in range(0, PW, LANE):
                lo = i * PW + c
                x, gv = x_refs[i][:, c:c + LANE].astype(F32), g_ref[:, lo:lo + LANE]
                _, r = _head_norm(x, gv, first)
                dx, g_part = _head_norm_bwd(d_refs[i][:, c:c + LANE], x, gv, r, first)
                out_ref[:, lo:lo + LANE] = dx.astype(out_ref.dtype)
                acc_ref[0:1, lo:lo + LANE] += jnp.sum(g_part, axis=0, keepdims=True)

    return pl.pallas_call(
        body, name=name, grid=(T // tr,),
        in_specs=[pl.BlockSpec((tr, PW), lambda i: (i, 0))] * n
        + [pl.BlockSpec((tr, PW), lambda i, j=j: (i, c0 + j)) for j in range(n)]
        + [pl.BlockSpec((1, n * PW), lambda i: (0, 0))],
        out_specs=[pl.BlockSpec((tr, n * PW), lambda i: (i, 0)), pl.BlockSpec((8, n * PW), lambda i: (0, 0))],
        out_shape=[_sds((T, n * PW), MXU_DTYPE), _sds((8, n * PW), F32)],
        compiler_params=_cparams("arbitrary"),
    )(*dns, *([p] * n), gains)


def _attn_fwd(qkv, g, d, PW, name):
    T = qkv.shape[0]
    B, W = ATT_BLK, LANE
    m, span, group, _ = _attn_geometry(T, d)
    c_q = g * PW // W
    c_k, c_v = c_q + 3 * PW // W, c_q + 6 * PW // W
    scale = HEAD_DIM ** -0.5

    def body(q_ref, k_ref, v_ref, kp_ref, vp_ref, o_ref, lse_ref):
        n = pl.program_id(1)
        qi = lax.broadcasted_iota(jnp.int32, (B, 2 * B), 0)
        kj = lax.broadcasted_iota(jnp.int32, (B, 2 * B), 1)
        band = (kj >= qi) & (kj <= qi + B)

        def task(r, j):
            cur = j * group + r
            if j == 0:
                kp, vp = _rows(kp_ref, r, d), _rows(vp_ref, r, d)
            else:
                kp, vp = _rows(k_ref, cur - group, d), _rows(v_ref, cur - group, d)
            mask = band & ((n * m + j > 0) | (kj >= B))
            qn = _rows(q_ref, cur, d).astype(MXU_DTYPE)
            kn = jnp.concatenate([kp, _rows(k_ref, cur, d)], axis=0).astype(MXU_DTYPE)
            vcat = jnp.concatenate([vp, _rows(v_ref, cur, d)], axis=0).astype(MXU_DTYPE)
            o_parts, lse_parts = [], []
            for sl in _head_slices():
                s = lax.dot_general(qn[:, sl], kn[:, sl], (((1,), (1,)), ((), ())), preferred_element_type=F32) * scale
                s = jnp.where(mask, s, NEG_INF)
                mx = jnp.max(s, axis=-1, keepdims=True)
                e = jnp.exp(s - mx)
                den = jnp.sum(e, axis=-1, keepdims=True)
                o_parts.append(jnp.dot(e.astype(MXU_DTYPE), vcat[:, sl], preferred_element_type=F32) / den)
                lse_parts.append(jnp.broadcast_to(mx + jnp.log(den), (B, HEAD_DIM)))
            _set_rows(o_ref, cur, d, jnp.concatenate(o_parts, axis=1))
            _set_rows(lse_ref, cur, d, jnp.concatenate(lse_parts, axis=1))

        _for_each_block(d, m, task, ATT_FWD_UNROLL)

    main = lambda c0: pl.BlockSpec((span, W), lambda hp, n: (n, c0 + hp))
    prev = lambda c0: pl.BlockSpec((group, W), lambda hp, n: (jnp.maximum(n * m - 1, 0), c0 + hp))
    out_blk = pl.BlockSpec((span, W), lambda hp, n: (n, hp))
    return pl.pallas_call(
        body, name=name, grid=(PW // W, T // span),
        in_specs=[main(c_q), main(c_k), main(c_v), prev(c_k), prev(c_v)],
        out_specs=[out_blk, out_blk],
        out_shape=[_sds((T, PW), F32), _sds((T, PW), F32)],
        compiler_params=_cparams("parallel", "parallel"),
    )(qkv, qkv, qkv, qkv, qkv)


def _attn_bwd(qkv, lse, do, corr, g, d, PW, name):
    T = qkv.shape[0]
    B, W = ATT_BLK, LANE
    m, span, group, n_blocks = _attn_geometry(T, d)
    c_q = g * PW // W
    c_k, c_v = c_q + 3 * PW // W, c_q + 6 * PW // W
    scale = HEAD_DIM ** -0.5
    nt = (((1,), (1,)), ((), ()))
    tn = (((0,), (0,)), ((), ()))

    def body(q_ref, k_ref, v_ref, do_ref, l_ref, c_ref, kp_ref, vp_ref, qx_ref, dox_ref, lx_ref, cx_ref,
             dq_ref, dk_ref, dv_ref):
        n = pl.program_id(1)
        i1 = lax.broadcasted_iota(jnp.int32, (B, B), 0)
        j1 = lax.broadcasted_iota(jnp.int32, (B, B), 1)
        i2 = lax.broadcasted_iota(jnp.int32, (2 * B, B), 0)
        j2 = lax.broadcasted_iota(jnp.int32, (2 * B, B), 1)

        def task(r, j):
            cur = j * group + r
            blk = n * m + j
            q_c, k_c, v_c = _rows(q_ref, cur, d), _rows(k_ref, cur, d), _rows(v_ref, cur, d)
            do_c, l_c, c_c = _rows(do_ref, cur, d), _rows(l_ref, cur, d), _rows(c_ref, cur, d)
            if j == 0:
                k_p, v_p = _rows(kp_ref, r, d), _rows(vp_ref, r, d)
            else:
                k_p, v_p = _rows(k_ref, cur - group, d), _rows(v_ref, cur - group, d)
            if j == m - 1:
                nxt = [_rows(ref, r, d) for ref in (qx_ref, dox_ref, lx_ref, cx_ref)]
            else:
                nxt = [_rows(ref, cur + group, d) for ref in (q_ref, do_ref, l_ref, c_ref)]
            q_x, do_x, l_x, c_x = nxt
            kn_c, kn_p, v_c, v_p = (a.astype(MXU_DTYPE) for a in (k_c, k_p, v_c, v_p))
            qn_c = q_c.astype(MXU_DTYPE)
            qn_cat = jnp.concatenate([qn_c, q_x.astype(MXU_DTYPE)], axis=0)
            do_cb = do_c.astype(MXU_DTYPE)
            do_cat = jnp.concatenate([do_cb, do_x.astype(MXU_DTYPE)], axis=0)
            l_cat = jnp.concatenate([l_c, l_x], axis=0)
            c_cat = jnp.concatenate([c_c, c_x], axis=0)
            mask_p = (j1 >= i1) & (blk > 0)
            mask_c = ((i2 < B) & (j2 <= i2)) | ((i2 >= B) & (j2 >= i2 - B) & (blk + 1 < n_blocks))
            dqn, dkn, dv = [], [], []
            for h, sl in enumerate(_head_slices()):
                lane = slice(h * HEAD_DIM, h * HEAD_DIM + 1)
                s_p = lax.dot_general(qn_c[:, sl], kn_p[:, sl], nt, preferred_element_type=F32) * scale
                pr_p = jnp.where(mask_p, jnp.exp(s_p - l_c[:, lane]), 0.0)
                dp_p = lax.dot_general(do_cb[:, sl], v_p[:, sl], nt, preferred_element_type=F32)
                ds_p = (pr_p * (dp_p + c_c[:, lane]) * scale).astype(MXU_DTYPE)
                s_c = lax.dot_general(qn_cat[:, sl], kn_c[:, sl], nt, preferred_element_type=F32) * scale
                pr_c = jnp.where(mask_c, jnp.exp(s_c - l_cat[:, lane]), 0.0)
                dp_c = lax.dot_general(do_cat[:, sl], v_c[:, sl], nt, preferred_element_type=F32)
                ds_c = (pr_c * (dp_c + c_cat[:, lane]) * scale).astype(MXU_DTYPE)
                dqn.append(jnp.dot(ds_p, kn_p[:, sl], preferred_element_type=F32)
                           + jnp.dot(ds_c[:B], kn_c[:, sl], preferred_element_type=F32))
                dkn.append(lax.dot_general(ds_c, qn_cat[:, sl], tn, preferred_element_type=F32))
                dv.append(lax.dot_general(pr_c.astype(MXU_DTYPE), do_cat[:, sl], tn, preferred_element_type=F32))
            _set_rows(dq_ref, cur, d, jnp.concatenate(dqn, axis=1))
            _set_rows(dk_ref, cur, d, jnp.concatenate(dkn, axis=1))
            _set_rows(dv_ref, cur, d, jnp.concatenate(dv, axis=1))

        _for_each_block(d, m, task, ATT_BWD_UNROLL)

    main = lambda c0: pl.BlockSpec((span, W), lambda hp, n: (n, c0 + hp))
    prev = lambda c0: pl.BlockSpec((group, W), lambda hp, n: (jnp.maximum(n * m - 1, 0), c0 + hp))
    nxt = lambda c0: pl.BlockSpec((group, W), lambda hp, n: (jnp.minimum((n + 1) * m, n_blocks - 1), c0 + hp))
    own = pl.BlockSpec((span, W), lambda hp, n: (n, hp))
    return pl.pallas_call(
        body, name=name, grid=(PW // W, T // span),
        in_specs=[main(c_q), main(c_k), main(c_v), main(0), main(0), main(0), prev(c_k), prev(c_v),
                  nxt(c_q), nxt(0), nxt(0), nxt(0)],
        out_specs=[own, own, own],
        out_shape=[_sds((T, PW), F32)] * 3,
        compiler_params=_cparams("parallel", "parallel"),
    )(qkv, qkv, qkv, do, lse, corr, qkv, qkv, qkv, do, lse, corr)


def _softmax3(lses):
    mx = jnp.maximum(jnp.maximum(lses[0], lses[1]), lses[2])
    ex = [jnp.exp(l - mx) for l in lses]
    inv = 1.0 / (ex[0] + ex[1] + ex[2])
    return [e * inv for e in ex]


def _mix_fwd(os_, lses, name, tr=1024):
    T, PW = os_[0].shape
    tr = _fit(T, tr)

    def body(o0, o1, o2, l0, l1, l2, y_ref):
        alpha = _softmax3([l0[...], l1[...], l2[...]])
        for g, o_ref in enumerate((o0, o1, o2)):
            y_ref[:, g * PW:(g + 1) * PW] = (o_ref[...] * alpha[g]).astype(y_ref.dtype)

    blk = pl.BlockSpec((tr, PW), lambda i: (i, 0))
    return pl.pallas_call(
        body, name=name, grid=(T // tr,),
        in_specs=[blk] * 6, out_specs=pl.BlockSpec((tr, 3 * PW), lambda i: (i, 0)),
        out_shape=_sds((T, 3 * PW), MXU_DTYPE),
        compiler_params=_cparams("parallel"),
    )(*os_, *lses)


def _mix_bwd(dymix, os_, lses, c_start, name, tr=1024):
    T, PW = os_[0].shape
    tr = _fit(T, tr)
    HP = PW // HEAD_DIM
    c0 = c_start // PW

    def body(d0, d1, d2, o0, o1, o2, l0, l1, l2, do0, do1, do2, dl0, dl1, dl2):
        col_head = lax.broadcasted_iota(jnp.int32, (tr, PW), 1) // HEAD_DIM
        alpha = _softmax3([l0[...], l1[...], l2[...]])
        dys = [d0[...].astype(F32), d1[...].astype(F32), d2[...].astype(F32)]
        dots = [_head_sum(dy * o_ref[...], col_head, HP) for dy, o_ref in zip(dys, (o0, o1, o2))]
        mean_dot = alpha[0] * dots[0] + alpha[1] * dots[1] + alpha[2] * dots[2]
        for g, (do_ref, dl_ref) in enumerate(((do0, dl0), (do1, dl1), (do2, dl2))):
            do_ref[...] = dys[g] * alpha[g]
            dl_ref[...] = -alpha[g] * mean_dot

    blk = pl.BlockSpec((tr, PW), lambda i: (i, 0))
    dy_specs = [pl.BlockSpec((tr, PW), lambda i, g=g: (i, c0 + g)) for g in range(3)]
    outs = pl.pallas_call(
        body, name=name, grid=(T // tr,),
        in_specs=dy_specs + [blk] * 6, out_specs=[blk] * 6,
        out_shape=[_sds((T, PW), F32)] * 6,
        compiler_params=_cparams("parallel"),
    )(dymix, dymix, dymix, *os_, *lses)
    return outs[:3], outs[3:]


def _adamw_math(w, g, m, v):
    m2 = ADAM_B1 * m + (1.0 - ADAM_B1) * g
    v2 = ADAM_B2 * v + (1.0 - ADAM_B2) * (g * g)
    m_hat = m2 / (1.0 - ADAM_B1 ** ADAM_STEP)
    v_hat = v2 / (1.0 - ADAM_B2 ** ADAM_STEP)
    delta = -ADAM_LR * (m_hat / (jnp.sqrt(v_hat) + ADAM_EPS) + ADAM_WD * w)
    return delta, m2, v2


def _adamw_layer(layer, w, m, v, own, landed, me, prev, name, tr=256):
    _, R, C = w.shape
    tr = next(t for t in range(min(tr, R) // 16 * 16, 0, -16) if R % t == 0)

    def body(me_ref, w_ref, m_ref, v_ref, own_ref, land_ref, *rest):
        g_ref, d_ref, m2_ref, v2_ref = rest[-4:]
        g = own_ref[...].astype(F32)
        for j in range(N_PEER):
            g = g + land_ref[j].astype(F32)
        delta, m2, v2 = _adamw_math(w_ref[...], g, m_ref[...], v_ref[...])
        g_ref[...] = g
        d_ref[...] = delta
        m2_ref[...] = m2
        v2_ref[...] = v2

    lay = pl.BlockSpec((None, tr, C), lambda i, me_ref: (layer, i, 0))
    in_specs = [lay, lay, lay, pl.BlockSpec((None, tr, C), lambda i, me_ref: (me_ref[0], i, 0)),
                pl.BlockSpec((N_PEER, tr, C), lambda i, me_ref: (0, i, 0))]
    args = [me, w, m, v, own, landed]
    aliases = {}
    if prev is not None:
        in_specs += [pl.BlockSpec(memory_space=pl.ANY)] * 4
        args += list(prev)
        aliases = {6 + i: i for i in range(4)}
    return pl.pallas_call(
        body, name=name,
        grid_spec=pltpu.PrefetchScalarGridSpec(num_scalar_prefetch=1, grid=(R // tr,), in_specs=in_specs, out_specs=[lay] * 4),
        out_shape=[_sds(w.shape, F32)] * 4,
        input_output_aliases=aliases,
        compiler_params=_cparams("parallel"),
    )(*args)


def _sum_parts(parts, name):
    _, R, C = parts.shape

    def body(p_ref, out_ref):
        g = p_ref[0]
        for j in range(1, N_DEV):
            g = g + p_ref[j]
        out_ref[...] = g

    return pl.pallas_call(
        body, name=name, grid=(1,),
        in_specs=[pl.BlockSpec((N_DEV, R, C), lambda i: (0, 0, 0))], out_specs=pl.BlockSpec((R, C), lambda i: (0, 0)),
        out_shape=_sds((R, C), F32), compiler_params=_cparams("arbitrary"),
    )(parts)


def _adamw_small(ws, gs, ms, vs, name):
    n = len(ws)

    def body(*refs):
        ins, outs = refs[:4 * n], refs[4 * n:]
        for i in range(n):
            delta, m2, v2 = _adamw_math(ins[i][...], ins[n + i][...], ins[2 * n + i][...], ins[3 * n + i][...])
            outs[i][...] = delta
            outs[n + i][...] = m2
            outs[2 * n + i][...] = v2

    outs = pl.pallas_call(
        body, name=name, out_shape=[_sds(w.shape, F32) for w in ws] * 3,
    )(*ws, *gs, *ms, *vs)
    return outs[:n], outs[n:2 * n], outs[2 * n:]


def _pack(arrays, rows_multiple=8):
    flat = []
    for a in arrays:
        a = a.reshape(-1).astype(F32)
        flat.append(jnp.pad(a, (0, (-a.shape[0]) % LANE)))
    flat = jnp.concatenate(flat)
    flat = jnp.pad(flat, (0, (-flat.shape[0]) % (LANE * rows_multiple)))
    return flat.reshape(-1, LANE)


def _unpack(packed, shapes):
    flat = packed.reshape(-1)
    out, off = [], 0
    for s in shapes:
        size = 1
        for dim in s:
            size *= dim
        out.append(flat[off:off + size].reshape(s))
        off += size + (-size) % LANE
    return out


def _layer_fwd(x, wts, getw, dims, dep=None, loss_target=None):
    AW, BW, PW, DP = dims["AW"], dims["BW"], dims["PW"], dims["DP"]
    q_start = 2 * AW + 3 * BW
    h, r1 = _rmsnorm_fwd(x, wts["attn_norm"], "rmsnorm_fwd", dep=dep)
    p = _mm_nt(h, getw("w_in", h), "proj_in", out_dtype=MXU_DTYPE, tm=1024, tn=1408)
    y_a = _sgu_fwd(p, wts["sgu_tril"], wts["sgu_bmat"], "sgu_fwd")
    y_b = _conv_fwd(p, getw("conv_w", y_a), AW, "conv_fwd")
    qkn = _qk_norm_fwd(p, wts["qk_gain"], q_start, PW, "qk_norm_fwd")
    os_, lses = [], []
    for g, d in enumerate(DILATIONS):
        o, lse = _attn_fwd(qkn, g, d, PW, "attn_fwd_%d" % d)
        os_.append(o)
        lses.append(lse)
    y_c = _mix_fwd(os_, lses, "mix_fwd")
    ymix = [y_a, y_b, y_c]
    x1, h2, r2 = _mm_residual_norm(ymix, getw("w_out", y_c), x, wts["mlp_norm"], "proj_out")
    a, hid = _mm_nn(h2, getw("w_mlp_in", h2), "mlp_in", out_dtype=MXU_DTYPE, relu2=True, tm=1024, tn=1024)
    if loss_target is None:
        x2 = _mm_nn(hid, getw("w_mlp_out", hid), "mlp_out", residual=x1, columns_first=True, tm=512, tn=1024)
    else:
        x2 = _mm_residual_loss(hid, getw("w_mlp_out", hid), x1, loss_target, "mlp_out_loss")
    saved = dict(x=x, h=h, r1=r1, p=p, qkn=qkn, os=os_, lses=lses, ymix=ymix, x1=x1, h2=h2, r2=r2, a=a, hid=hid)
    return x2, saved


def _layer_bwd(dx, dxb, wts, getw, scatter, saved, dims):
    AW, BW, PW, DP = dims["AW"], dims["BW"], dims["PW"], dims["DP"]
    q_start = 2 * AW + 3 * BW
    D = dx.shape[1]
    g_w2 = _wgrad_wide_a(saved["hid"], dxb, "mlp_out_wgrad")
    token = scatter("w_mlp_out", g_w2.reshape(N_DEV, -1, D))
    da = _mm_nt(dxb, getw("w_mlp_out", None), "mlp_out_dgrad", out_dtype=MXU_DTYPE, relu2_pre=saved["a"], dep=token,
                tm=1024, tn=1024)
    g_w1 = _wgrad_wide_b([saved["h2"]], da, "mlp_in_wgrad", groups=N_DEV)
    token = scatter("w_mlp_in", g_w1)
    dx1, dx1b, g_mlp_norm = _mm_norm_bwd(da, getw("w_mlp_in", None), saved["x1"], wts["mlp_norm"], saved["r2"], dx,
                                         "mlp_in_dgrad", dep=token)
    g_wout = _wgrad_wide_b(saved["ymix"], dx1b, "proj_out_wgrad")
    token = scatter("w_out", g_wout.reshape(N_DEV, -1, D))
    dymix = _mm_nt(dx1b, getw("w_out", None), "proj_out_dgrad", out_dtype=MXU_DTYPE, dep=token, tm=1024, tn=1024)
    p = saved["p"]
    du, dv, g_sgu_w, g_sgu_bmat = _sgu_bwd(dymix, p, wts["sgu_tril"], wts["sgu_tril_t"], wts["sgu_bmat"], "sgu_bwd")
    d_b, d_c, d_xb, g_conv = _conv_bwd(dymix, p, getw("conv_w", None), AW, "conv_bwd")
    dos, corrs = _mix_bwd(dymix, saved["os"], saved["lses"], AW + BW, "mix_bwd")
    dqns, dkns, dvs = [], [], []
    for g, d in enumerate(DILATIONS):
        dqn, dkn, dvv = _attn_bwd(saved["qkn"], saved["lses"][g], dos[g], corrs[g], g, d, PW, "attn_bwd_%d" % d)
        dqns.append(dqn)
        dkns.append(dkn)
        dvs.append(dvv.astype(MXU_DTYPE))
    dqk, g_qk = _qk_norm_bwd(dqns + dkns, p, wts["qk_gain"], q_start, PW, "qk_norm_bwd")
    g_q, g_k = (part.reshape(-1, HEAD_DIM).sum(0) for part in jnp.split(g_qk[0], 2))
    dp = jnp.concatenate([du, dv, d_b, d_c, d_xb, dqk] + dvs, axis=1)
    g_win_t = _wgrad_wide_a(dp, saved["h"], "proj_in_wgrad")
    token = scatter("w_in", g_win_t.reshape(N_DEV, DP // N_DEV, D))
    dx0, dx0b, g_attn_norm = _mm_norm_bwd(dp, getw("w_in", None), saved["x"], wts["attn_norm"], saved["r1"], dx1,
                                          "proj_in_dgrad", dep=token)
    H = AW // HEAD_DIM
    tril = jnp.tril(jnp.ones((CHUNK, CHUNK), F32))
    small = [g_attn_norm.reshape(-1), g_sgu_w * tril, g_sgu_bmat.reshape(CHUNK, H, HEAD_DIM).sum(-1).T,
             g_conv, g_q, g_k, g_mlp_norm.reshape(-1)]
    return dx0, dx0b, small


def kernel(x, attn_norm, w_in, sgu_w, sgu_b, conv_w, q_norm, k_norm, w_out, mlp_norm, w_mlp_in, w_mlp_out, loss_target, m_attn_norm, m_w_in, m_sgu_w, m_sgu_b, m_conv_w, m_q_norm, m_k_norm, m_w_out, m_mlp_norm, m_w_mlp_in, m_w_mlp_out, v_attn_norm, v_w_in, v_sgu_w, v_sgu_b, v_conv_w, v_q_norm, v_k_norm, v_w_out, v_mlp_norm, v_w_mlp_in, v_w_mlp_out):
    n_layers = attn_norm.shape[0]
    T, D = x.shape[1], x.shape[2]
    H = sgu_w.shape[1]
    AW = H * HEAD_DIM
    BW = conv_w.shape[2] * N_DEV
    DP = w_in.shape[2] * N_DEV
    DMIX = w_out.shape[1] * N_DEV
    DFF = w_mlp_in.shape[2] * N_DEV
    PW = (DMIX - AW - BW) // 3
    HP = PW // HEAD_DIM
    dims = dict(AW=AW, BW=BW, PW=PW, DP=DP)
    me = 4 * lax.axis_index("x") + 2 * lax.axis_index("y") + lax.axis_index("c")

    big_names = ("w_in", "w_out", "w_mlp_in", "w_mlp_out")
    tr_in = lambda a: jnp.swapaxes(a, 1, 2)
    big_w = dict(zip(big_names, (tr_in(w_in), w_out, w_mlp_in, w_mlp_out)))
    big_m = dict(zip(big_names, (tr_in(m_w_in), m_w_out, m_w_mlp_in, m_w_mlp_out)))
    big_v = dict(zip(big_names, (tr_in(v_w_in), v_w_out, v_w_mlp_in, v_w_mlp_out)))

    keys = []
    for l in range(n_layers):
        keys += [(l, nm) for nm in big_names]
    keys.insert(1, (0, "conv_w"))
    first_src = big_w[keys[0][1]][keys[0][0]].astype(MXU_DTYPE)
    first_flights, first_token = _exchange_start([first_src], [_own_in_place(first_src, me)], "gather", name="gather_start_first")
    zero = first_token[0, 0]
    srcs = [_pack([conv_w]) + zero if nm == "conv_w" else (big_w[nm][l] + zero).astype(MXU_DTYPE) for l, nm in keys[1:]]
    flights, gather_token = _exchange_start(srcs, [_own_in_place(s, me) for s in srcs], "gather", name="gather_start")
    arriving = dict(zip(keys, first_flights + flights))
    forwarding = {}
    relayout = dict(
        w_in=lambda g: g.reshape(DP, D), w_out=lambda g: g.reshape(DMIX, D),
        w_mlp_in=lambda g: g, w_mlp_out=lambda g: g.reshape(DFF, D),
        conv_w=lambda g: jnp.stack([_unpack(g[j], [conv_w.shape])[0] for j in range(N_DEV)], axis=2).reshape(
            n_layers, CONV_WIDTH, BW))
    gathered = {}

    def forward(key, after):
        _, land = _exchange_wait(arriving[key], after, "gather", name="gather_arrive_%d_%s" % key)
        fl, token = _exchange_start(None, [land], "forward", name="gather_forward_%d_%s" % key)
        forwarding[key] = fl[0]
        return token

    def weight_getter(l):
        def getw(nm, after):
            key = (0, nm) if nm == "conv_w" else (l, nm)
            if key not in gathered:
                ahead = keys[keys.index(key):][:2]
                for k in ahead:
                    if k not in forwarding:
                        after = forward(k, after)
                _, land = _exchange_wait(forwarding[key], after, "forward", name="gather_wait_%d_%s" % key)
                gathered[key] = relayout[nm](land)
            return gathered[key][l] if nm == "conv_w" else gathered[key]
        return getw

    tril = jnp.tril(jnp.ones((CHUNK, CHUNK), F32))
    layers = []
    for l in range(n_layers):
        w_tril = sgu_w[l] * tril
        layers.append(dict(
            attn_norm=attn_norm[l][None], mlp_norm=mlp_norm[l][None],
            sgu_tril=w_tril.astype(MXU_DTYPE), sgu_tril_t=w_tril.transpose(0, 2, 1).astype(MXU_DTYPE),
            sgu_bmat=jnp.repeat(sgu_b[l].T, HEAD_DIM, axis=1),
            qk_gain=jnp.concatenate([jnp.tile(q_norm[l], 3 * HP), jnp.tile(k_norm[l], 3 * HP)])[None]))

    xs = x[0]
    saved = []
    for l in range(n_layers):
        xs, sv = _layer_fwd(xs, layers[l], weight_getter(l), dims, dep=gather_token if l == 0 else None,
                            loss_target=loss_target[0] if l == n_layers - 1 else None)
        saved.append(sv)
    loss_blk, dx, dxb = xs
    loss = lax.psum(loss_blk[0, 0], ("x", "y", "c"))

    scattering = {}

    def scatter_starter(l):
        def scatter(nm, partials):
            land = lax.empty((N_PEER,) + partials.shape[1:], partials.dtype)
            fl, tok = _exchange_start([partials], [land], "scatter", name="scatter_start_%d_%s" % (l, nm))
            scattering[(l, nm)] = fl[0]
            return tok
        return scatter

    small = [None] * n_layers
    for l in reversed(range(n_layers)):
        dx, dxb, small[l] = _layer_bwd(dx, dxb, layers[l], weight_getter(l), scatter_starter(l), saved[l], dims)

    small_shapes = [s.shape for s in small[0]]
    small_src = [_pack([s for l in range(n_layers) for s in small[l]])]
    small_flights, small_token = _exchange_start(small_src, [_own_in_place(s, me) for s in small_src], "gather_all",
                                                 name="small_start")
    grad_x = dx[None]

    me1 = me.astype(jnp.int32).reshape(1)
    res = {nm: None for nm in big_names}
    after = small_token
    for l in reversed(range(n_layers)):
        for nm in reversed(big_names):
            own, landed = _exchange_wait(scattering[(l, nm)], after, "scatter", name="scatter_wait_%d_%s" % (l, nm))
            res[nm] = _adamw_layer(l, big_w[nm], big_m[nm], big_v[nm], own, landed, me1, res[nm], "adamw_" + nm)
            after = res[nm][0]
    res["w_in"] = [tr_in(a) for a in res["w_in"]]
    big_out = [res[nm] for nm in big_names]

    _, gathered_small = _exchange_wait(small_flights[0], after, "gather_all", name="small_wait")
    summed = _unpack(_sum_parts(gathered_small, "sum_small"), small_shapes * n_layers)
    ns = len(small_shapes)
    g_small = [jnp.stack([summed[l * ns + i] for l in range(n_layers)]) for i in range(ns)]
    g_attn_norm, g_sgu_w, g_sgu_b, g_conv_full, g_q, g_k, g_mlp_norm = g_small
    cs = conv_w.shape[2]
    g_conv = lax.dynamic_slice_in_dim(g_conv_full, me * cs, cs, axis=2)
    sm_w = (attn_norm, sgu_w, sgu_b, conv_w, q_norm, k_norm, mlp_norm)
    sm_m = (m_attn_norm, m_sgu_w, m_sgu_b, m_conv_w, m_q_norm, m_k_norm, m_mlp_norm)
    sm_v = (v_attn_norm, v_sgu_w, v_sgu_b, v_conv_w, v_q_norm, v_k_norm, v_mlp_norm)
    sm_g = (g_attn_norm, g_sgu_w, g_sgu_b, g_conv, g_q, g_k, g_mlp_norm)
    sm_delta, sm_m2, sm_v2 = _adamw_small(sm_w, sm_g, sm_m, sm_v, "adamw_small")

    def ordered(small_list, big_kind):
        b = [big_out[i][big_kind] for i in range(4)]
        return [small_list[0], b[0], small_list[1], small_list[2], small_list[3], small_list[4], small_list[5],
                b[1], small_list[6], b[2], b[3]]

    return (loss, grad_x, *ordered(list(sm_g), 0), *ordered(sm_delta, 1), *ordered(sm_m2, 2), *ordered(sm_v2, 3))
```

```python
import jax
import jax.numpy as jnp
from jax import lax
from jax.experimental import pallas as pl
from jax.experimental.pallas import tpu as pltpu

N_DEV = 8
HEAD_DIM = 64
CHUNK = 128
ATT_BLK = 128
DILATIONS = (1, 4, 16)
CONV_WIDTH = 3
EPS = 1e-6
ADAM_LR = 0.001
ADAM_B1 = 0.9
ADAM_B2 = 0.999
ADAM_EPS = 1e-08
ADAM_WD = 0.01
ADAM_STEP = 10
MXU_DTYPE = jnp.bfloat16
F32 = jnp.float32
LANE = 128
VMEM_PHYSICAL_BYTES = 64 * 1024 * 1024
VMEM_LIMIT_BYTES = 56 * 1024 * 1024
NEG_INF = float("-inf")


def _cparams(*sem):
    return pltpu.CompilerParams(dimension_semantics=sem, vmem_limit_bytes=VMEM_LIMIT_BYTES)


def _sds(shape, dtype):
    return jax.ShapeDtypeStruct(shape, dtype)


def _fit(n, tile):
    for t in range(min(tile, n) // LANE * LANE, 0, -LANE):
        if n % t == 0:
            return t
    return n


_HBM = pl.BlockSpec(memory_space=pltpu.HBM)
_SEM = pl.BlockSpec(memory_space=pltpu.SEMAPHORE)
_DATAFLOW = pltpu.SideEffectType.DATAFLOW_SIDE_EFFECTING
N_PEER = N_DEV - 1


def _mesh_pos():
    x, y, c = lax.axis_index("x"), lax.axis_index("y"), lax.axis_index("c")
    return x, y, c, 4 * x + 2 * y + c


OTHER_CHIPS = (4, 2, 6)
EXCHANGE_PEERS = dict(
    scatter=tuple(range(1, N_DEV)),
    gather_all=tuple(range(1, N_DEV)),
    gather=(1,) + OTHER_CHIPS,
    forward=OTHER_CHIPS)


def _remote_copies(src, land, send_sems, recv_sems, mode):
    x, y, c, me = _mesh_pos()
    copies = []
    for i, k in enumerate(EXCHANGE_PEERS[mode]):
        px = (1 - x) if (k & 4) else x
        py = (1 - y) if (k & 2) else y
        pc = (1 - c) if (k & 1) else c
        if mode == "scatter":
            src_ref, dst_ref, dev = src.at[4 * px + 2 * py + pc], land.at[i], (px, py, pc)
        elif mode == "forward":
            slot = 4 * px + 2 * py + c
            src_ref, dst_ref, dev = land.at[slot], land.at[slot], (x, y, 1 - c)
        else:
            src_ref, dst_ref, dev = src, land.at[me], (px, py, pc)
        copies.append(pltpu.make_async_remote_copy(
            src_ref=src_ref, dst_ref=dst_ref, send_sem=send_sems.at[i], recv_sem=recv_sems.at[i],
            device_id=dev, device_id_type=pl.DeviceIdType.MESH))
    return copies


def _own_in_place(src, me):
    land = lax.empty((N_DEV,) + src.shape, src.dtype)
    return lax.dynamic_update_slice(land, src[None], (me,) + (0,) * src.ndim)


def _exchange_start(srcs, lands, mode, name):
    n = len(lands)
    has_src = srcs is not None
    arrays = (list(srcs) if has_src else []) + list(lands)
    n_arr = len(arrays)
    n_copies = len(EXCHANGE_PEERS[mode])

    def body(*refs):
        src = refs[:n] if has_src else [None] * n
        land = refs[n_arr - n:n_arr]
        send, recv = refs[n_arr:n_arr + n], refs[n_arr + n:n_arr + 2 * n]
        token = refs[2 * n_arr + 2 * n]
        for t in range(n):
            for cp in _remote_copies(src[t], land[t], send[t], recv[t], mode):
                cp.start()
        token[...] = jnp.zeros_like(token)

    outs = pl.pallas_call(
        body, name=name,
        out_shape=([pltpu.SemaphoreType.DMA((n_copies,))] * (2 * n) + [pltpu.HBM(a.shape, a.dtype) for a in arrays]
                   + [_sds((8, LANE), F32)]),
        in_specs=[_HBM] * n_arr,
        out_specs=[_SEM] * (2 * n) + [_HBM] * n_arr + [pl.BlockSpec(memory_space=pltpu.VMEM)],
        input_output_aliases={i: 2 * n + i for i in range(n_arr)},
        compiler_params=pltpu.CompilerParams(has_side_effects=_DATAFLOW),
    )(*[pltpu.with_memory_space_constraint(a, pltpu.HBM) for a in arrays])
    thru = outs[2 * n:2 * n + n_arr]
    flights = [(outs[t], outs[n + t], thru[t] if has_src else None, thru[n_arr - n + t]) for t in range(n)]
    return flights, outs[2 * n + n_arr]


def _exchange_wait(flight, after, mode, name):
    send, recv, src, land = flight
    arrays = [land] if src is None else [src, land]
    n_arr = len(arrays)

    def body(*refs):
        src_ref = refs[0] if n_arr == 2 else None
        land_ref, send_ref, recv_ref = refs[n_arr - 1], refs[n_arr], refs[n_arr + 1]
        for cp in _remote_copies(src_ref, land_ref, send_ref, recv_ref, mode):
            cp.wait_send()
            cp.wait_recv()

    outs = pl.pallas_call(
        body, name=name, out_shape=[pltpu.HBM(a.shape, a.dtype) for a in arrays],
        in_specs=[_HBM] * n_arr + [_SEM, _SEM, pl.BlockSpec(memory_space=pl.ANY)], out_specs=[_HBM] * n_arr,
        input_output_aliases={i: i for i in range(n_arr)},
        compiler_params=pltpu.CompilerParams(has_side_effects=_DATAFLOW),
    )(*arrays, send, recv, after)
    return (None, outs[0]) if src is None else (outs[0], outs[1])


def _rmsnorm_fwd(x, g, name, dep=None, tr=1024):
    T, D = x.shape
    tr = _fit(T, tr)

    def body(x_ref, g_ref, *rest):
        h_ref, r_ref = rest[-2:]
        xv = x_ref[...]
        r = lax.rsqrt(jnp.mean(xv * xv, axis=-1, keepdims=True) + EPS)
        h_ref[...] = (xv * r * g_ref[...]).astype(h_ref.dtype)
        r_ref[...] = r

    in_specs = [pl.BlockSpec((tr, D), lambda i: (i, 0)), pl.BlockSpec((1, D), lambda i: (0, 0))]
    args = [x, g]
    if dep is not None:
        in_specs.append(pl.BlockSpec(dep.shape, lambda i: (0, 0)))
        args.append(dep)
    return pl.pallas_call(
        body, name=name, grid=(T // tr,),
        in_specs=in_specs,
        out_specs=[pl.BlockSpec((tr, D), lambda i: (i, 0)), pl.BlockSpec((tr, 1), lambda i: (i, 0))],
        out_shape=[_sds((T, D), MXU_DTYPE), _sds((T, 1), F32)],
        compiler_params=_cparams("parallel"),
    )(*args)


def _mm_relu2(a, b, name, tm=1024, tn=1024):
    M, K = a.shape
    N = b.shape[0] * b.shape[2]
    tm, tn = _fit(M, tm), _fit(b.shape[2], tn)
    per = b.shape[2] // tn

    def body(a_ref, b_ref, y_ref, y2_ref):
        acc = jnp.dot(a_ref[...], b_ref[...], preferred_element_type=F32)
        y_ref[...] = acc.astype(y_ref.dtype)
        rl = jnp.maximum(acc, 0.0)
        y2_ref[...] = (rl * rl).astype(y2_ref.dtype)

    out_blk = pl.BlockSpec((tm, tn), lambda i, j: (i, j))
    return pl.pallas_call(
        body, name=name, grid=(M // tm, N // tn),
        in_specs=[pl.BlockSpec((tm, K), lambda i, j: (i, 0)), pl.BlockSpec((None, K, tn), lambda i, j: (j // per, 0, j % per))],
        out_specs=[out_blk] * 2, out_shape=[_sds((M, N), MXU_DTYPE)] * 2,
        compiler_params=_cparams("parallel", "parallel"),
    )(a, b)


def _column_offsets(pieces):
    offs = [0]
    for piece in pieces:
        offs.append(offs[-1] + piece.shape[1])
    return offs


def _mm_residual_norm(a_pieces, b, residual, g, name, tm=512):
    n = len(a_pieces)
    M = a_pieces[0].shape[0]
    K, N = b.shape
    offs = _column_offsets(a_pieces)
    assert offs[-1] == K
    tm = _fit(M, tm)
    f32_row, mxu_row = tm * N * 4, tm * N * jnp.dtype(MXU_DTYPE).itemsize
    need = b.size * b.dtype.itemsize + 2 * (tm * K * a_pieces[0].dtype.itemsize + 2 * f32_row + mxu_row) + 5 * f32_row
    vmem_limit = max(VMEM_LIMIT_BYTES, min(need, VMEM_PHYSICAL_BYTES - 2 * 1024 * 1024))

    def body(*refs):
        a_refs, b_ref, res_ref, g_ref, x_ref, h_ref, r_ref = refs[:n], *refs[n:]
        xv = res_ref[...]
        for i in range(n):
            xv = xv + jnp.dot(a_refs[i][...], b_ref[offs[i]:offs[i + 1], :], preferred_element_type=F32)
        x_ref[...] = xv
        r = lax.rsqrt(jnp.mean(xv * xv, axis=-1, keepdims=True) + EPS)
        h_ref[...] = (xv * r * g_ref[...]).astype(h_ref.dtype)
        r_ref[...] = r

    row = pl.BlockSpec((tm, N), lambda i: (i, 0))
    return pl.pallas_call(
        body, name=name, grid=(M // tm,),
        in_specs=[pl.BlockSpec((tm, piece.shape[1]), lambda i: (i, 0)) for piece in a_pieces]
        + [pl.BlockSpec((K, N), lambda i: (0, 0), pipeline_mode=pl.Buffered(1)), row, pl.BlockSpec((1, N), lambda i: (0, 0))],
        out_specs=[row, row, pl.BlockSpec((tm, 1), lambda i: (i, 0))],
        out_shape=[_sds((M, N), F32), _sds((M, N), MXU_DTYPE), _sds((M, 1), F32)],
        compiler_params=pltpu.CompilerParams(dimension_semantics=("parallel",), vmem_limit_bytes=vmem_limit),
    )(*a_pieces, b, residual, g)


def _mm_norm_bwd(a, b, x, g, r, dres, name, dep=None, tm=256):
    M, K = a.shape
    grouped = b.ndim == 3
    N = b.shape[1]
    tm = _fit(M, tm)
    nt = (((1,), (1,)), ((), ()))
    f32_row, mxu_row = tm * N * 4, tm * N * jnp.dtype(MXU_DTYPE).itemsize
    need = b.size * b.dtype.itemsize + 2 * (tm * K * a.dtype.itemsize + 3 * f32_row + mxu_row) + 5 * f32_row
    vmem_limit = max(VMEM_LIMIT_BYTES, min(need, VMEM_PHYSICAL_BYTES - 2 * 1024 * 1024))

    def body(*refs):
        a_ref, b_ref, x_ref, g_ref, r_ref, dres_ref = refs[:6]
        dx_ref, dxb_ref, dg_ref = refs[-3:]

        @pl.when(pl.program_id(0) == 0)
        def _():
            dg_ref[...] = jnp.zeros_like(dg_ref)

        if grouped:
            kg = b.shape[2]
            dh = lax.dot_general(a_ref[:, 0:kg], b_ref[0], nt, preferred_element_type=F32)
            for i in range(1, b.shape[0]):
                dh += lax.dot_general(a_ref[:, i * kg:(i + 1) * kg], b_ref[i], nt, preferred_element_type=F32)
        else:
            dh = jnp.dot(a_ref[...], b_ref[...], preferred_element_type=F32)
        xv, rv = x_ref[...], r_ref[...]
        gdy = dh * g_ref[...]
        mean_xg = jnp.mean(xv * gdy, axis=-1, keepdims=True)
        dx = dres_ref[...] + rv * gdy - xv * (rv * rv * rv) * mean_xg
        dx_ref[...] = dx
        dxb_ref[...] = dx.astype(dxb_ref.dtype)
        dg_ref[...] += jnp.sum(dh * xv * rv, axis=0, keepdims=True)

    row = pl.BlockSpec((tm, N), lambda i: (i, 0))
    in_specs = [pl.BlockSpec((tm, K), lambda i: (i, 0)),
                pl.BlockSpec(b.shape, lambda i: (0,) * b.ndim, pipeline_mode=pl.Buffered(1)),
                row, pl.BlockSpec((1, N), lambda i: (0, 0)), pl.BlockSpec((tm, 1), lambda i: (i, 0)), row]
    args = [a, b, x, g, r, dres]
    if dep is not None:
        in_specs.append(pl.BlockSpec(dep.shape, lambda i: (0, 0)))
        args.append(dep)
    return pl.pallas_call(
        body, name=name, grid=(M // tm,),
        in_specs=in_specs, out_specs=[row, row, pl.BlockSpec((1, N), lambda i: (0, 0))],
        out_shape=[_sds((M, N), F32), _sds((M, N), MXU_DTYPE), _sds((1, N), F32)],
        compiler_params=pltpu.CompilerParams(dimension_semantics=("arbitrary",), vmem_limit_bytes=vmem_limit),
    )(*args)


def _mm_residual_loss(a, b, residual, target, name, tm=512, tn=1024):
    M, K = a.shape
    N = b.shape[1]
    tm, tn = _fit(M, tm), _fit(N, tn)

    def body(a_ref, b_ref, res_ref, t_ref, loss_ref, dx_ref, dxb_ref):
        @pl.when((pl.program_id(0) == 0) & (pl.program_id(1) == 0))
        def _():
            loss_ref[...] = jnp.zeros_like(loss_ref)

        err = jnp.dot(a_ref[...], b_ref[...], preferred_element_type=F32) + res_ref[...] - t_ref[...]
        sq = jnp.sum(jnp.sum(err * err, axis=-1, keepdims=True), axis=0, keepdims=True)
        loss_ref[...] += (0.5 / N) * sq
        dx = err * (1.0 / N)
        dx_ref[...] = dx
        dxb_ref[...] = dx.astype(dxb_ref.dtype)

    blk = pl.BlockSpec((tm, tn), lambda j, i: (i, j))
    return pl.pallas_call(
        body, name=name, grid=(N // tn, M // tm),
        in_specs=[pl.BlockSpec((tm, K), lambda j, i: (i, 0)),
                  pl.BlockSpec((K, tn), lambda j, i: (0, j), pipeline_mode=pl.Buffered(1)), blk, blk],
        out_specs=[pl.BlockSpec((8, LANE), lambda j, i: (0, 0)), blk, blk],
        out_shape=[_sds((8, LANE), F32), _sds((M, N), F32), _sds((M, N), MXU_DTYPE)],
        compiler_params=_cparams("arbitrary", "arbitrary"),
    )(a, b, residual, target)


def _mm_nt(a, b, name, out_dtype=F32, relu2_pre=None, dep=None, tm=512, tn=512):
    M, K = a.shape
    N = b.shape[0]
    tm, tn = _fit(M, tm), _fit(N, tn)
    b_spec = pl.BlockSpec((tn, K), lambda i, j: (j, 0))

    def body(*refs):
        a_ref, b_ref = refs[0], refs[1]
        p_ref = refs[2] if relu2_pre is not None else None
        out_ref = refs[2 + (relu2_pre is not None) + (dep is not None)]
        acc = lax.dot_general(a_ref[...], b_ref[...], (((1,), (1,)), ((), ())), preferred_element_type=F32)
        if p_ref is not None:
            acc = acc * (2.0 * jnp.maximum(p_ref[...].astype(F32), 0.0))
        out_ref[...] = acc.astype(out_ref.dtype)

    out_blk = pl.BlockSpec((tm, tn), lambda i, j: (i, j))
    in_specs = [pl.BlockSpec((tm, K), lambda i, j: (i, 0)), b_spec]
    args = [a, b]
    if relu2_pre is not None:
        in_specs.append(out_blk)
        args.append(relu2_pre)
    if dep is not None:
        in_specs.append(pl.BlockSpec(dep.shape, lambda i, j: (0, 0)))
        args.append(dep)
    return pl.pallas_call(
        body, name=name, grid=(M // tm, N // tn),
        in_specs=in_specs, out_specs=out_blk, out_shape=_sds((M, N), out_dtype),
        compiler_params=_cparams("parallel", "parallel"),
    )(*args)


def _wgrad_wide_a(a, b, name, tm=512):
    T, M = a.shape
    N = b.shape[1]
    tm = _fit(M, tm)

    def body(a_ref, b_ref, out_ref):
        out_ref[...] = lax.dot_general(a_ref[...], b_ref[...], (((0,), (0,)), ((), ())),
                                       preferred_element_type=F32).astype(out_ref.dtype)

    return pl.pallas_call(
        body, name=name, grid=(M // tm,),
        in_specs=[pl.BlockSpec((T, tm), lambda i: (0, i)),
                  pl.BlockSpec((T, N), lambda i: (0, 0), pipeline_mode=pl.Buffered(1))],
        out_specs=pl.BlockSpec((tm, N), lambda i: (i, 0)), out_shape=_sds((M, N), MXU_DTYPE),
        compiler_params=_cparams("parallel"),
    )(a, b)


def _wgrad_wide_b(a_pieces, b, name, groups=None, tn=512, t_chunk=512):
    n = len(a_pieces)
    T = a_pieces[0].shape[0]
    offs = _column_offsets(a_pieces)
    M = offs[-1]
    N = b.shape[1]
    tn = _fit(N if groups is None else N // groups, tn)
    t_chunk = _fit(T, t_chunk)

    def body(*refs):
        a_refs, b_ref, out_ref, at_ref = refs[:n], *refs[n:]

        @pl.when(pl.program_id(0) == 0)
        def _():
            for i in range(n):
                for c in range(0, T, t_chunk):
                    at_ref[offs[i]:offs[i + 1], c:c + t_chunk] = a_refs[i][c:c + t_chunk, :].T

        out_ref[...] = jnp.dot(at_ref[...], b_ref[...], preferred_element_type=F32).astype(out_ref.dtype)

    if groups is None:
        out_spec = pl.BlockSpec((M, tn), lambda j: (0, j))
        out_shape = _sds((M, N), MXU_DTYPE)
    else:
        per = N // groups // tn
        out_spec = pl.BlockSpec((None, M, tn), lambda j: (j // per, 0, j % per))
        out_shape = _sds((groups, M, N // groups), MXU_DTYPE)
    return pl.pallas_call(
        body, name=name, grid=(N // tn,),
        in_specs=[pl.BlockSpec(piece.shape, lambda j: (0, 0), pipeline_mode=pl.Buffered(1)) for piece in a_pieces]
        + [pl.BlockSpec((T, tn), lambda j: (0, j))],
        out_specs=out_spec, out_shape=out_shape,
        scratch_shapes=[pltpu.VMEM((M, T), MXU_DTYPE)],
        compiler_params=_cparams("arbitrary"),
    )(*a_pieces, b)


SGU_ROWS = 512


def _sgu_mixed(v, w_ref, b_ref, n_heads):
    parts = [jnp.dot(w_ref[h], v[:, h * HEAD_DIM:(h + 1) * HEAD_DIM], preferred_element_type=F32) for h in range(n_heads)]
    return jnp.concatenate(parts, axis=1) + b_ref[...]


def _sgu_fwd(p, w_tril, bmat, name):
    T = p.shape[0]
    H = w_tril.shape[0]
    AW = H * HEAD_DIM
    rows = _fit(T, SGU_ROWS)

    def body(u_ref, v_ref, w_ref, b_ref, y_ref):
        for c in range(0, rows, CHUNK):
            ch = pl.ds(c, CHUNK)
            mixed = _sgu_mixed(v_ref[ch, :].astype(MXU_DTYPE), w_ref, b_ref, H)
            y_ref[ch, :] = (u_ref[ch, :].astype(F32) * mixed).astype(y_ref.dtype)

    const3 = lambda c: (0, 0, 0)
    return pl.pallas_call(
        body, name=name, grid=(T // rows,),
        in_specs=[pl.BlockSpec((rows, AW), lambda c: (c, 0)), pl.BlockSpec((rows, AW), lambda c: (c, 1)),
                  pl.BlockSpec((H, CHUNK, CHUNK), const3), pl.BlockSpec((CHUNK, AW), lambda c: (0, 0))],
        out_specs=pl.BlockSpec((rows, AW), lambda c: (c, 0)),
        out_shape=_sds((T, AW), MXU_DTYPE),
        compiler_params=_cparams("parallel"),
    )(p, p, w_tril, bmat)


def _sgu_bwd(dymix, p, w_tril, w_tril_t, bmat, name):
    T = p.shape[0]
    H = w_tril.shape[0]
    AW = H * HEAD_DIM
    rows = _fit(T, SGU_ROWS)

    def body(dy_ref, u_ref, v_ref, w_ref, wt_ref, b_ref, du_ref, dv_ref, dw_ref, db_ref):
        @pl.when(pl.program_id(0) == 0)
        def _():
            dw_ref[...] = jnp.zeros_like(dw_ref)
            db_ref[...] = jnp.zeros_like(db_ref)

        for c in range(0, rows, CHUNK):
            ch = pl.ds(c, CHUNK)
            v = v_ref[ch, :].astype(MXU_DTYPE)
            dy = dy_ref[ch, :].astype(F32)
            du_ref[ch, :] = (dy * _sgu_mixed(v, w_ref, b_ref, H)).astype(du_ref.dtype)
            dm = dy * u_ref[ch, :].astype(F32)
            db_ref[...] += dm
            dm_c = dm.astype(MXU_DTYPE)
            dv = []
            for h in range(H):
                sl = slice(h * HEAD_DIM, (h + 1) * HEAD_DIM)
                dv.append(jnp.dot(wt_ref[h], dm_c[:, sl], preferred_element_type=F32))
                dw_ref[h] += lax.dot_general(dm_c[:, sl], v[:, sl], (((1,), (1,)), ((), ())), preferred_element_type=F32)
            dv_ref[ch, :] = jnp.concatenate(dv, axis=1).astype(dv_ref.dtype)

    const3 = lambda c: (0, 0, 0)
    blk = pl.BlockSpec((rows, AW), lambda c: (c, 0))
    return pl.pallas_call(
        body, name=name, grid=(T // rows,),
        in_specs=[blk, blk, pl.BlockSpec((rows, AW), lambda c: (c, 1)),
                  pl.BlockSpec((H, CHUNK, CHUNK), const3), pl.BlockSpec((H, CHUNK, CHUNK), const3),
                  pl.BlockSpec((CHUNK, AW), lambda c: (0, 0))],
        out_specs=[blk, blk, pl.BlockSpec((H, CHUNK, CHUNK), const3), pl.BlockSpec((CHUNK, AW), lambda c: (0, 0))],
        out_shape=[_sds((T, AW), MXU_DTYPE), _sds((T, AW), MXU_DTYPE), _sds((H, CHUNK, CHUNK), F32), _sds((CHUNK, AW), F32)],
        compiler_params=_cparams("arbitrary"),
    )(dymix, p, p, w_tril, w_tril_t, bmat)


def _shift_down(z, s, row):
    return jnp.where(row >= s, pltpu.roll(z, s, 0), 0.0)


def _shift_up(z, s, row, T):
    return jnp.where(row < T - s, pltpu.roll(z, T - s, 0), 0.0)


def _conv_fwd(p, w_conv, AW, name):
    T = p.shape[0]
    BW = w_conv.shape[1]
    nb = BW // LANE
    b0 = 2 * AW // LANE

    def body(b_ref, c_ref, x_ref, w_ref, y_ref):
        row = lax.broadcasted_iota(jnp.int32, (T, LANE), 0)
        z = c_ref[...].astype(F32) * x_ref[...].astype(F32)
        w0, w1, w2 = w_ref[0:1, :], w_ref[1:2, :], w_ref[2:3, :]
        conv = w2 * z + w1 * _shift_down(z, 1, row) + w0 * _shift_down(z, 2, row)
        y_ref[...] = (b_ref[...].astype(F32) * conv).astype(y_ref.dtype)

    return pl.pallas_call(
        body, name=name, grid=(nb,),
        in_specs=[pl.BlockSpec((T, LANE), lambda j: (0, b0 + j)), pl.BlockSpec((T, LANE), lambda j: (0, b0 + nb + j)),
                  pl.BlockSpec((T, LANE), lambda j: (0, b0 + 2 * nb + j)), pl.BlockSpec((CONV_WIDTH, LANE), lambda j: (0, j))],
        out_specs=pl.BlockSpec((T, LANE), lambda j: (0, j)),
        out_shape=_sds((T, BW), MXU_DTYPE),
        compiler_params=_cparams("parallel"),
    )(p, p, p, w_conv)


def _conv_bwd(dymix, p, w_conv, AW, name):
    T = p.shape[0]
    BW = w_conv.shape[1]
    nb = BW // LANE
    b0 = 2 * AW // LANE
    y0 = AW // LANE

    def body(dy_ref, b_ref, c_ref, x_ref, w_ref, db_ref, dc_ref, dxb_ref, dw_ref):
        row = lax.broadcasted_iota(jnp.int32, (T, LANE), 0)
        cv, xv, dy = c_ref[...].astype(F32), x_ref[...].astype(F32), dy_ref[...].astype(F32)
        w0, w1, w2 = w_ref[0:1, :], w_ref[1:2, :], w_ref[2:3, :]
        z = cv * xv
        z1 = _shift_down(z, 1, row)
        z2 = _shift_down(z, 2, row)
        conv = w2 * z + w1 * z1 + w0 * z2
        db_ref[...] = (dy * conv).astype(db_ref.dtype)
        dconv = dy * b_ref[...].astype(F32)
        dz = w2 * dconv + w1 * _shift_up(dconv, 1, row, T) + w0 * _shift_up(dconv, 2, row, T)
        dc_ref[...] = (dz * xv).astype(dc_ref.dtype)
        dxb_ref[...] = (dz * cv).astype(dxb_ref.dtype)
        dw_ref[0:1, :] = jnp.sum(dconv * z2, axis=0, keepdims=True)
        dw_ref[1:2, :] = jnp.sum(dconv * z1, axis=0, keepdims=True)
        dw_ref[2:3, :] = jnp.sum(dconv * z, axis=0, keepdims=True)

    col = lambda j: (0, j)
    return pl.pallas_call(
        body, name=name, grid=(nb,),
        in_specs=[pl.BlockSpec((T, LANE), lambda j: (0, y0 + j)),
                  pl.BlockSpec((T, LANE), lambda j: (0, b0 + j)), pl.BlockSpec((T, LANE), lambda j: (0, b0 + nb + j)),
                  pl.BlockSpec((T, LANE), lambda j: (0, b0 + 2 * nb + j)), pl.BlockSpec((CONV_WIDTH, LANE), col)],
        out_specs=[pl.BlockSpec((T, LANE), col)] * 3 + [pl.BlockSpec((CONV_WIDTH, LANE), col)],
        out_shape=[_sds((T, BW), MXU_DTYPE)] * 3 + [_sds((CONV_WIDTH, BW), F32)],
        compiler_params=_cparams("parallel"),
    )(dymix, p, p, p, w_conv)


def _head_sum(x, col_head, n_heads):
    out = jnp.zeros_like(x)
    for h in range(n_heads):
        sel = col_head == h
        out = jnp.where(sel, jnp.sum(jnp.where(sel, x, 0.0), axis=-1, keepdims=True), out)
    return out


def _same_head(width):
    assert width == 2 * HEAD_DIM
    return lax.broadcasted_iota(jnp.int32, (1, width), 1) < HEAD_DIM


def _head_sum2(x, first):
    s0 = jnp.sum(jnp.where(first, x, 0.0), axis=-1, keepdims=True)
    s1 = jnp.sum(jnp.where(first, 0.0, x), axis=-1, keepdims=True)
    return jnp.where(first, s0, s1)


def _head_norm(x, g, first):
    r = lax.rsqrt(_head_sum2(x * x, first) * (1.0 / HEAD_DIM) + EPS)
    return x * r * g, r


def _head_norm_bwd(dy, x, g, r, first):
    gdy = dy * g
    mean_xg = _head_sum2(x * gdy, first) * (1.0 / HEAD_DIM)
    return r * gdy - x * (r * r * r) * mean_xg, dy * x * r


ATT_SPAN_MIN = 512
ATT_FWD_UNROLL = 4
ATT_BWD_UNROLL = 4
HEADS_PER_LANES = LANE // HEAD_DIM


def _attn_geometry(T, d):
    m = max(1, ATT_SPAN_MIN // (ATT_BLK * d))
    return m, ATT_BLK * d * m, ATT_BLK * d, T // (ATT_BLK * d)


def _rows(ref, start, d):
    return ref[pl.ds(start, ATT_BLK, stride=d), :] if d > 1 else ref[pl.ds(start, ATT_BLK), :]


def _set_rows(ref, start, d, value):
    if d > 1:
        ref[pl.ds(start, ATT_BLK, stride=d), :] = value
    else:
        ref[pl.ds(start, ATT_BLK), :] = value


def _for_each_block(d, m, task, unroll):
    for j in range(m):
        if d == 1:
            task(0, j)
        else:
            lax.fori_loop(0, d, lambda r, carry, j=j: (task(r, j), carry)[1], 0, unroll=min(unroll, d))


def _head_slices():
    return [slice(h * HEAD_DIM, (h + 1) * HEAD_DIM) for h in range(HEADS_PER_LANES)]


def _qk_norm_fwd(p, gains, q_start, PW, name, tr=1024):
    T = p.shape[0]
    tr = _fit(T, tr)
    n_norm = gains.shape[1] // PW
    n = n_norm * 3 // 2
    c0 = q_start // PW

    def body(*refs):
        x_refs, g_ref, out_ref = refs[:n], refs[n], refs[n + 1]
        first = _same_head(LANE)
        for i in range(n):
            for c in range(0, PW, LANE):
                lo = i * PW + c
                x = x_refs[i][:, c:c + LANE].astype(F32)
                out_ref[:, lo:lo + LANE] = _head_norm(x, g_ref[:, lo:lo + LANE], first)[0] if i < n_norm else x

    return pl.pallas_call(
        body, name=name, grid=(T // tr,),
        in_specs=[pl.BlockSpec((tr, PW), lambda i, j=j: (i, c0 + j)) for j in range(n)]
        + [pl.BlockSpec((1, n_norm * PW), lambda i: (0, 0))],
        out_specs=pl.BlockSpec((tr, n * PW), lambda i: (i, 0)), out_shape=_sds((T, n * PW), F32),
        compiler_params=_cparams("parallel"),
    )(*([p] * n), gains)


def _qk_norm_bwd(dns, p, gains, q_start, PW, name, tr=1024):
    T = p.shape[0]
    tr = _fit(T, tr)
    n = len(dns)
    c0 = q_start // PW

    def body(*refs):
        d_refs, x_refs, g_ref, out_ref, acc_ref = refs[:n], refs[n:2 * n], refs[2 * n], refs[2 * n + 1], refs[2 * n + 2]

        @pl.when(pl.program_id(0) == 0)
        def _():
            acc_ref[...] = jnp.zeros_like(acc_ref)

        first = _same_head(LANE)
        for i in range(n):
            for c in range(0, PW, LANE):
                lo = i * PW + c
                x, gv = x_refs[i][:, c:c + LANE].astype(F32), g_ref[:, lo:lo + LANE]
                _, r = _head_norm(x, gv, first)
                dx, g_part = _head_norm_bwd(d_refs[i][:, c:c + LANE], x, gv, r, first)
                out_ref[:, lo:lo + LANE] = dx.astype(out_ref.dtype)
                acc_ref[0:1, lo:lo + LANE] += jnp.sum(g_part, axis=0, keepdims=True)

    return pl.pallas_call(
        body, name=name, grid=(T // tr,),
        in_specs=[pl.BlockSpec((tr, PW), lambda i: (i, 0))] * n
        + [pl.BlockSpec((tr, PW), lambda i, j=j: (i, c0 + j)) for j in range(n)]
        + [pl.BlockSpec((1, n * PW), lambda i: (0, 0))],
        out_specs=[pl.BlockSpec((tr, n * PW), lambda i: (i, 0)), pl.BlockSpec((8, n * PW), lambda i: (0, 0))],
        out_shape=[_sds((T, n * PW), MXU_DTYPE), _sds((8, n * PW), F32)],
        compiler_params=_cparams("arbitrary"),
    )(*dns, *([p] * n), gains)


def _attn_fwd(qkv, g, d, PW, name):
    T = qkv.shape[0]
    B, W = ATT_BLK, LANE
    m, span, group, _ = _attn_geometry(T, d)
    c_q = g * PW // W
    c_k, c_v = c_q + 3 * PW // W, c_q + 6 * PW // W
    scale = HEAD_DIM ** -0.5

    def body(q_ref, k_ref, v_ref, kp_ref, vp_ref, o_ref, lse_ref):
        n = pl.program_id(1)
        qi = lax.broadcasted_iota(jnp.int32, (B, 2 * B), 0)
        kj = lax.broadcasted_iota(jnp.int32, (B, 2 * B), 1)
        band = (kj >= qi) & (kj <= qi + B)

        def task(r, j):
            cur = j * group + r
            if j == 0:
                kp, vp = _rows(kp_ref, r, d), _rows(vp_ref, r, d)
            else:
                kp, vp = _rows(k_ref, cur - group, d), _rows(v_ref, cur - group, d)
            mask = band & ((n * m + j > 0) | (kj >= B))
            qn = _rows(q_ref, cur, d).astype(MXU_DTYPE)
            kn = jnp.concatenate([kp, _rows(k_ref, cur, d)], axis=0).astype(MXU_DTYPE)
            vcat = jnp.concatenate([vp, _rows(v_ref, cur, d)], axis=0).astype(MXU_DTYPE)
            o_parts, lse_parts = [], []
            for sl in _head_slices():
                s = lax.dot_general(qn[:, sl], kn[:, sl], (((1,), (1,)), ((), ())), preferred_element_type=F32) * scale
                s = jnp.where(mask, s, NEG_INF)
                mx = jnp.max(s, axis=-1, keepdims=True)
                e = jnp.exp(s - mx)
                den = jnp.sum(e, axis=-1, keepdims=True)
                o_parts.append(jnp.dot(e.astype(MXU_DTYPE), vcat[:, sl], preferred_element_type=F32) / den)
                lse_parts.append(jnp.broadcast_to(mx + jnp.log(den), (B, HEAD_DIM)))
            _set_rows(o_ref, cur, d, jnp.concatenate(o_parts, axis=1))
            _set_rows(lse_ref, cur, d, jnp.concatenate(lse_parts, axis=1))

        _for_each_block(d, m, task, ATT_FWD_UNROLL)

    main = lambda c0: pl.BlockSpec((span, W), lambda hp, n: (n, c0 + hp))
    prev = lambda c0: pl.BlockSpec((group, W), lambda hp, n: (jnp.maximum(n * m - 1, 0), c0 + hp))
    out_blk = pl.BlockSpec((span, W), lambda hp, n: (n, hp))
    return pl.pallas_call(
        body, name=name, grid=(PW // W, T // span),
        in_specs=[main(c_q), main(c_k), main(c_v), prev(c_k), prev(c_v)],
        out_specs=[out_blk, out_blk],
        out_shape=[_sds((T, PW), F32), _sds((T, PW), F32)],
        compiler_params=_cparams("parallel", "parallel"),
    )(qkv, qkv, qkv, qkv, qkv)


def _attn_bwd(qkv, lse, do, corr, g, d, PW, name):
    T = qkv.shape[0]
    B, W = ATT_BLK, LANE
    m, span, group, n_blocks = _attn_geometry(T, d)
    c_q = g * PW // W
    c_k, c_v = c_q + 3 * PW // W, c_q + 6 * PW // W
    scale = HEAD_DIM ** -0.5
    nt = (((1,), (1,)), ((), ()))
    tn = (((0,), (0,)), ((), ()))

    def body(q_ref, k_ref, v_ref, do_ref, l_ref, c_ref, kp_ref, vp_ref, qx_ref, dox_ref, lx_ref, cx_ref,
             dq_ref, dk_ref, dv_ref):
        n = pl.program_id(1)
        i1 = lax.broadcasted_iota(jnp.int32, (B, B), 0)
        j1 = lax.broadcasted_iota(jnp.int32, (B, B), 1)
        i2 = lax.broadcasted_iota(jnp.int32, (2 * B, B), 0)
        j2 = lax.broadcasted_iota(jnp.int32, (2 * B, B), 1)

        def task(r, j):
            cur = j * group + r
            blk = n * m + j
            q_c, k_c, v_c = _rows(q_ref, cur, d), _rows(k_ref, cur, d), _rows(v_ref, cur, d)
            do_c, l_c, c_c = _rows(do_ref, cur, d), _rows(l_ref, cur, d), _rows(c_ref, cur, d)
            if j == 0:
                k_p, v_p = _rows(kp_ref, r, d), _rows(vp_ref, r, d)
            else:
                k_p, v_p = _rows(k_ref, cur - group, d), _rows(v_ref, cur - group, d)
            if j == m - 1:
                nxt = [_rows(ref, r, d) for ref in (qx_ref, dox_ref, lx_ref, cx_ref)]
            else:
                nxt = [_rows(ref, cur + group, d) for ref in (q_ref, do_ref, l_ref, c_ref)]
            q_x, do_x, l_x, c_x = nxt
            kn_c, kn_p, v_c, v_p = (a.astype(MXU_DTYPE) for a in (k_c, k_p, v_c, v_p))
            qn_c = q_c.astype(MXU_DTYPE)
            qn_cat = jnp.concatenate([qn_c, q_x.astype(MXU_DTYPE)], axis=0)
            do_cb = do_c.astype(MXU_DTYPE)
            do_cat = jnp.concatenate([do_cb, do_x.astype(MXU_DTYPE)], axis=0)
            l_cat = jnp.concatenate([l_c, l_x], axis=0)
            c_cat = jnp.concatenate([c_c, c_x], axis=0)
            mask_p = (j1 >= i1) & (blk > 0)
            mask_c = ((i2 < B) & (j2 <= i2)) | ((i2 >= B) & (j2 >= i2 - B) & (blk + 1 < n_blocks))
            dqn, dkn, dv = [], [], []
            for h, sl in enumerate(_head_slices()):
                lane = slice(h * HEAD_DIM, h * HEAD_DIM + 1)
                s_p = lax.dot_general(qn_c[:, sl], kn_p[:, sl], nt, preferred_element_type=F32) * scale
                pr_p = jnp.where(mask_p, jnp.exp(s_p - l_c[:, lane]), 0.0)
                dp_p = lax.dot_general(do_cb[:, sl], v_p[:, sl], nt, preferred_element_type=F32)
                ds_p = (pr_p * (dp_p + c_c[:, lane]) * scale).astype(MXU_DTYPE)
                s_c = lax.dot_general(qn_cat[:, sl], kn_c[:, sl], nt, preferred_element_type=F32) * scale
                pr_c = jnp.where(mask_c, jnp.exp(s_c - l_cat[:, lane]), 0.0)
                dp_c = lax.dot_general(do_cat[:, sl], v_c[:, sl], nt, preferred_element_type=F32)
                ds_c = (pr_c * (dp_c + c_cat[:, lane]) * scale).astype(MXU_DTYPE)
                dqn.append(jnp.dot(ds_p, kn_p[:, sl], preferred_element_type=F32)
                           + jnp.dot(ds_c[:B], kn_c[:, sl], preferred_element_type=F32))
                dkn.append(lax.dot_general(ds_c, qn_cat[:, sl], tn, preferred_element_type=F32))
                dv.append(lax.dot_general(pr_c.astype(MXU_DTYPE), do_cat[:, sl], tn, preferred_element_type=F32))
            _set_rows(dq_ref, cur, d, jnp.concatenate(dqn, axis=1))
            _set_rows(dk_ref, cur, d, jnp.concatenate(dkn, axis=1))
            _set_rows(dv_ref, cur, d, jnp.concatenate(dv, axis=1))

        _for_each_block(d, m, task, ATT_BWD_UNROLL)

    main = lambda c0: pl.BlockSpec((span, W), lambda hp, n: (n, c0 + hp))
    prev = lambda c0: pl.BlockSpec((group, W), lambda hp, n: (jnp.maximum(n * m - 1, 0), c0 + hp))
    nxt = lambda c0: pl.BlockSpec((group, W), lambda hp, n: (jnp.minimum((n + 1) * m, n_blocks - 1), c0 + hp))
    own = pl.BlockSpec((span, W), lambda hp, n: (n, hp))
    return pl.pallas_call(
        body, name=name, grid=(PW // W, T // span),
        in_specs=[main(c_q), main(c_k), main(c_v), main(0), main(0), main(0), prev(c_k), prev(c_v),
                  nxt(c_q), nxt(0), nxt(0), nxt(0)],
        out_specs=[own, own, own],
        out_shape=[_sds((T, PW), F32)] * 3,
        compiler_params=_cparams("parallel", "parallel"),
    )(qkv, qkv, qkv, do, lse, corr, qkv, qkv, qkv, do, lse, corr)


def _softmax3(lses):
    mx = jnp.maximum(jnp.maximum(lses[0], lses[1]), lses[2])
    ex = [jnp.exp(l - mx) for l in lses]
    inv = 1.0 / (ex[0] + ex[1] + ex[2])
    return [e * inv for e in ex]


def _mix_fwd(os_, lses, name, tr=1024):
    T, PW = os_[0].shape
    tr = _fit(T, tr)

    def body(o0, o1, o2, l0, l1, l2, y_ref):
        alpha = _softmax3([l0[...], l1[...], l2[...]])
        for g, o_ref in enumerate((o0, o1, o2)):
            y_ref[:, g * PW:(g + 1) * PW] = (o_ref[...] * alpha[g]).astype(y_ref.dtype)

    blk = pl.BlockSpec((tr, PW), lambda i: (i, 0))
    return pl.pallas_call(
        body, name=name, grid=(T // tr,),
        in_specs=[blk] * 6, out_specs=pl.BlockSpec((tr, 3 * PW), lambda i: (i, 0)),
        out_shape=_sds((T, 3 * PW), MXU_DTYPE),
        compiler_params=_cparams("parallel"),
    )(*os_, *lses)


def _mix_bwd(dymix, os_, lses, c_start, name, tr=1024):
    T, PW = os_[0].shape
    tr = _fit(T, tr)
    HP = PW // HEAD_DIM
    c0 = c_start // PW

    def body(d0, d1, d2, o0, o1, o2, l0, l1, l2, do0, do1, do2, dl0, dl1, dl2):
        col_head = lax.broadcasted_iota(jnp.int32, (tr, PW), 1) // HEAD_DIM
        alpha = _softmax3([l0[...], l1[...], l2[...]])
        dys = [d0[...].astype(F32), d1[...].astype(F32), d2[...].astype(F32)]
        dots = [_head_sum(dy * o_ref[...], col_head, HP) for dy, o_ref in zip(dys, (o0, o1, o2))]
        mean_dot = alpha[0] * dots[0] + alpha[1] * dots[1] + alpha[2] * dots[2]
        for g, (do_ref, dl_ref) in enumerate(((do0, dl0), (do1, dl1), (do2, dl2))):
            do_ref[...] = dys[g] * alpha[g]
            dl_ref[...] = -alpha[g] * mean_dot

    blk = pl.BlockSpec((tr, PW), lambda i: (i, 0))
    dy_specs = [pl.BlockSpec((tr, PW), lambda i, g=g: (i, c0 + g)) for g in range(3)]
    outs = pl.pallas_call(
        body, name=name, grid=(T // tr,),
        in_specs=dy_specs + [blk] * 6, out_specs=[blk] * 6,
        out_shape=[_sds((T, PW), F32)] * 6,
        compiler_params=_cparams("parallel"),
    )(dymix, dymix, dymix, *os_, *lses)
    return outs[:3], outs[3:]


def _adamw_math(w, g, m, v):
    m2 = ADAM_B1 * m + (1.0 - ADAM_B1) * g
    v2 = ADAM_B2 * v + (1.0 - ADAM_B2) * (g * g)
    m_hat = m2 / (1.0 - ADAM_B1 ** ADAM_STEP)
    v_hat = v2 / (1.0 - ADAM_B2 ** ADAM_STEP)
    delta = -ADAM_LR * (m_hat / (jnp.sqrt(v_hat) + ADAM_EPS) + ADAM_WD * w)
    return delta, m2, v2


def _adamw_layer(layer, w, m, v, own, landed, me, prev, name, tr=256):
    _, R, C = w.shape
    tr = next(t for t in range(min(tr, R) // 16 * 16, 0, -16) if R % t == 0)

    def body(me_ref, w_ref, m_ref, v_ref, own_ref, land_ref, *rest):
        g_ref, d_ref, m2_ref, v2_ref = rest[-4:]
        g = own_ref[...].astype(F32)
        for j in range(N_PEER):
            g = g + land_ref[j].astype(F32)
        delta, m2, v2 = _adamw_math(w_ref[...], g, m_ref[...], v_ref[...])
        g_ref[...] = g
        d_ref[...] = delta
        m2_ref[...] = m2
        v2_ref[...] = v2

    lay = pl.BlockSpec((None, tr, C), lambda i, me_ref: (layer, i, 0))
    in_specs = [lay, lay, lay, pl.BlockSpec((None, tr, C), lambda i, me_ref: (me_ref[0], i, 0)),
                pl.BlockSpec((N_PEER, tr, C), lambda i, me_ref: (0, i, 0))]
    args = [me, w, m, v, own, landed]
    aliases = {}
    if prev is not None:
        in_specs += [pl.BlockSpec(memory_space=pl.ANY)] * 4
        args += list(prev)
        aliases = {6 + i: i for i in range(4)}
    return pl.pallas_call(
        body, name=name,
        grid_spec=pltpu.PrefetchScalarGridSpec(num_scalar_prefetch=1, grid=(R // tr,), in_specs=in_specs, out_specs=[lay] * 4),
        out_shape=[_sds(w.shape, F32)] * 4,
        input_output_aliases=aliases,
        compiler_params=_cparams("parallel"),
    )(*args)


def _sum_parts(parts, name):
    _, R, C = parts.shape

    def body(p_ref, out_ref):
        g = p_ref[0]
        for j in range(1, N_DEV):
            g = g + p_ref[j]
        out_ref[...] = g

    return pl.pallas_call(
        body, name=name, grid=(1,),
        in_specs=[pl.BlockSpec((N_DEV, R, C), lambda i: (0, 0, 0))], out_specs=pl.BlockSpec((R, C), lambda i: (0, 0)),
        out_shape=_sds((R, C), F32), compiler_params=_cparams("arbitrary"),
    )(parts)


def _adamw_small(ws, gs, ms, vs, name):
    n = len(ws)

    def body(*refs):
        ins, outs = refs[:4 * n], refs[4 * n:]
        for i in range(n):
            delta, m2, v2 = _adamw_math(ins[i][...], ins[n + i][...], ins[2 * n + i][...], ins[3 * n + i][...])
            outs[i][...] = delta
            outs[n + i][...] = m2
            outs[2 * n + i][...] = v2

    outs = pl.pallas_call(
        body, name=name, out_shape=[_sds(w.shape, F32) for w in ws] * 3,
    )(*ws, *gs, *ms, *vs)
    return outs[:n], outs[n:2 * n], outs[2 * n:]


def _pack(arrays, rows_multiple=8):
    flat = []
    for a in arrays:
        a = a.reshape(-1).astype(F32)
        flat.append(jnp.pad(a, (0, (-a.shape[0]) % LANE)))
    flat = jnp.concatenate(flat)
    flat = jnp.pad(flat, (0, (-flat.shape[0]) % (LANE * rows_multiple)))
    return flat.reshape(-1, LANE)


def _unpack(packed, shapes):
    flat = packed.reshape(-1)
    out, off = [], 0
    for s in shapes:
        size = 1
        for dim in s:
            size *= dim
        out.append(flat[off:off + size].reshape(s))
        off += size + (-size) % LANE
    return out


def _layer_fwd(x, wts, getw, dims, dep=None, normed=None, next_gain=None, loss_target=None):
    AW, BW, PW, DP = dims["AW"], dims["BW"], dims["PW"], dims["DP"]
    q_start = 2 * AW + 3 * BW
    h, r1 = _rmsnorm_fwd(x, wts["attn_norm"], "rmsnorm_fwd", dep=dep) if normed is None else normed
    p = _mm_nt(h, getw("w_in", h), "proj_in", out_dtype=MXU_DTYPE, tm=1024, tn=1408)
    y_a = _sgu_fwd(p, wts["sgu_tril"], wts["sgu_bmat"], "sgu_fwd")
    y_b = _conv_fwd(p, getw("conv_w", y_a), AW, "conv_fwd")
    qkn = _qk_norm_fwd(p, wts["qk_gain"], q_start, PW, "qk_norm_fwd")
    os_, lses = [], []
    for g, d in enumerate(DILATIONS):
        o, lse = _attn_fwd(qkn, g, d, PW, "attn_fwd_%d" % d)
        os_.append(o)
        lses.append(lse)
    y_c = _mix_fwd(os_, lses, "mix_fwd")
    ymix = [y_a, y_b, y_c]
    x1, h2, r2 = _mm_residual_norm(ymix, getw("w_out", y_c), x, wts["mlp_norm"], "proj_out")
    a, hid = _mm_relu2(h2, getw("w_mlp_in", h2), "mlp_in")
    if loss_target is not None:
        x2 = _mm_residual_loss(hid, getw("w_mlp_out", hid), x1, loss_target, "mlp_out_loss")
    else:
        x2 = _mm_residual_norm([hid], getw("w_mlp_out", hid), x1, next_gain, "mlp_out", tm=256)
    saved = dict(x=x, h=h, r1=r1, p=p, qkn=qkn, os=os_, lses=lses, ymix=ymix, x1=x1, h2=h2, r2=r2, a=a, hid=hid)
    return x2, saved


def _layer_bwd(dx, dxb, wts, getw, scatter, saved, dims):
    AW, BW, PW, DP = dims["AW"], dims["BW"], dims["PW"], dims["DP"]
    q_start = 2 * AW + 3 * BW
    D = dx.shape[1]
    g_w2 = _wgrad_wide_a(saved["hid"], dxb, "mlp_out_wgrad")
    token = scatter("w_mlp_out", g_w2.reshape(N_DEV, -1, D))
    da = _mm_nt(dxb, getw("w_mlp_out", None), "mlp_out_dgrad", out_dtype=MXU_DTYPE, relu2_pre=saved["a"], dep=token,
                tm=1024, tn=1024)
    g_w1 = _wgrad_wide_b([saved["h2"]], da, "mlp_in_wgrad", groups=N_DEV)
    token = scatter("w_mlp_in", g_w1)
    dx1, dx1b, g_mlp_norm = _mm_norm_bwd(da, getw("w_mlp_in", None), saved["x1"], wts["mlp_norm"], saved["r2"], dx,
                                         "mlp_in_dgrad", dep=token)
    g_wout = _wgrad_wide_b(saved["ymix"], dx1b, "proj_out_wgrad")
    token = scatter("w_out", g_wout.reshape(N_DEV, -1, D))
    dymix = _mm_nt(dx1b, getw("w_out", None), "proj_out_dgrad", out_dtype=MXU_DTYPE, dep=token, tm=1024, tn=1024)
    p = saved["p"]
    du, dv, g_sgu_w, g_sgu_bmat = _sgu_bwd(dymix, p, wts["sgu_tril"], wts["sgu_tril_t"], wts["sgu_bmat"], "sgu_bwd")
    d_b, d_c, d_xb, g_conv = _conv_bwd(dymix, p, getw("conv_w", None), AW, "conv_bwd")
    dos, corrs = _mix_bwd(dymix, saved["os"], saved["lses"], AW + BW, "mix_bwd")
    dqns, dkns, dvs = [], [], []
    for g, d in enumerate(DILATIONS):
        dqn, dkn, dvv = _attn_bwd(saved["qkn"], saved["lses"][g], dos[g], corrs[g], g, d, PW, "attn_bwd_%d" % d)
        dqns.append(dqn)
        dkns.append(dkn)
        dvs.append(dvv.astype(MXU_DTYPE))
    dqk, g_qk = _qk_norm_bwd(dqns + dkns, p, wts["qk_gain"], q_start, PW, "qk_norm_bwd")
    g_q, g_k = (part.reshape(-1, HEAD_DIM).sum(0) for part in jnp.split(g_qk[0], 2))
    dp = jnp.concatenate([du, dv, d_b, d_c, d_xb, dqk] + dvs, axis=1)
    g_win_t = _wgrad_wide_a(dp, saved["h"], "proj_in_wgrad")
    token = scatter("w_in", g_win_t.reshape(N_DEV, DP // N_DEV, D))
    dx0, dx0b, g_attn_norm = _mm_norm_bwd(dp, getw("w_in", None), saved["x"], wts["attn_norm"], saved["r1"], dx1,
                                          "proj_in_dgrad", dep=token)
    H = AW // HEAD_DIM
    tril = jnp.tril(jnp.ones((CHUNK, CHUNK), F32))
    small = [g_attn_norm.reshape(-1), g_sgu_w * tril, g_sgu_bmat.reshape(CHUNK, H, HEAD_DIM).sum(-1).T,
             g_conv, g_q, g_k, g_mlp_norm.reshape(-1)]
    return dx0, dx0b, small


def kernel(x, attn_norm, w_in, sgu_w, sgu_b, conv_w, q_norm, k_norm, w_out, mlp_norm, w_mlp_in, w_mlp_out, loss_target, m_attn_norm, m_w_in, m_sgu_w, m_sgu_b, m_conv_w, m_q_norm, m_k_norm, m_w_out, m_mlp_norm, m_w_mlp_in, m_w_mlp_out, v_attn_norm, v_w_in, v_sgu_w, v_sgu_b, v_conv_w, v_q_norm, v_k_norm, v_w_out, v_mlp_norm, v_w_mlp_in, v_w_mlp_out):
    n_layers = attn_norm.shape[0]
    T, D = x.shape[1], x.shape[2]
    H = sgu_w.shape[1]
    AW = H * HEAD_DIM
    BW = conv_w.shape[2] * N_DEV
    DP = w_in.shape[2] * N_DEV
    DMIX = w_out.shape[1] * N_DEV
    DFF = w_mlp_in.shape[2] * N_DEV
    PW = (DMIX - AW - BW) // 3
    HP = PW // HEAD_DIM
    dims = dict(AW=AW, BW=BW, PW=PW, DP=DP)
    me = 4 * lax.axis_index("x") + 2 * lax.axis_index("y") + lax.axis_index("c")

    big_names = ("w_in", "w_out", "w_mlp_in", "w_mlp_out")
    tr_in = lambda a: jnp.swapaxes(a, 1, 2)
    big_w = dict(zip(big_names, (tr_in(w_in), w_out, w_mlp_in, w_mlp_out)))
    big_m = dict(zip(big_names, (tr_in(m_w_in), m_w_out, m_w_mlp_in, m_w_mlp_out)))
    big_v = dict(zip(big_names, (tr_in(v_w_in), v_w_out, v_w_mlp_in, v_w_mlp_out)))

    keys = []
    for l in range(n_layers):
        keys += [(l, nm) for nm in big_names]
    keys.insert(1, (0, "conv_w"))
    first_src = big_w[keys[0][1]][keys[0][0]].astype(MXU_DTYPE)
    first_flights, first_token = _exchange_start([first_src], [_own_in_place(first_src, me)], "gather", name="gather_start_first")
    zero = first_token[0, 0]
    srcs = [_pack([conv_w]) + zero if nm == "conv_w" else (big_w[nm][l] + zero).astype(MXU_DTYPE) for l, nm in keys[1:]]
    flights, gather_token = _exchange_start(srcs, [_own_in_place(s, me) for s in srcs], "gather", name="gather_start")
    arriving = dict(zip(keys, first_flights + flights))
    forwarding = {}
    relayout = dict(
        w_in=lambda g: g.reshape(DP, D), w_out=lambda g: g.reshape(DMIX, D),
        w_mlp_in=lambda g: g, w_mlp_out=lambda g: g.reshape(DFF, D),
        conv_w=lambda g: jnp.stack([_unpack(g[j], [conv_w.shape])[0] for j in range(N_DEV)], axis=2).reshape(
            n_layers, CONV_WIDTH, BW))
    gathered = {}

    def forward(key, after):
        _, land = _exchange_wait(arriving[key], after, "gather", name="gather_arrive_%d_%s" % key)
        fl, token = _exchange_start(None, [land], "forward", name="gather_forward_%d_%s" % key)
        forwarding[key] = fl[0]
        return token

    def weight_getter(l):
        def getw(nm, after):
            key = (0, nm) if nm == "conv_w" else (l, nm)
            if key not in gathered:
                ahead = keys[keys.index(key):][:2]
                for k in ahead:
                    if k not in forwarding:
                        after = forward(k, after)
                _, land = _exchange_wait(forwarding[key], after, "forward", name="gather_wait_%d_%s" % key)
                gathered[key] = relayout[nm](land)
            return gathered[key][l] if nm == "conv_w" else gathered[key]
        return getw

    tril = jnp.tril(jnp.ones((CHUNK, CHUNK), F32))
    layers = []
    for l in range(n_layers):
        w_tril = sgu_w[l] * tril
        layers.append(dict(
            attn_norm=attn_norm[l][None], mlp_norm=mlp_norm[l][None],
            sgu_tril=w_tril.astype(MXU_DTYPE), sgu_tril_t=w_tril.transpose(0, 2, 1).astype(MXU_DTYPE),
            sgu_bmat=jnp.repeat(sgu_b[l].T, HEAD_DIM, axis=1),
            qk_gain=jnp.concatenate([jnp.tile(q_norm[l], 3 * HP), jnp.tile(k_norm[l], 3 * HP)])[None]))

    xs, normed = x[0], None
    saved = []
    for l in range(n_layers):
        last = l == n_layers - 1
        out, sv = _layer_fwd(xs, layers[l], weight_getter(l), dims, dep=gather_token if l == 0 else None, normed=normed,
                             next_gain=None if last else layers[l + 1]["attn_norm"],
                             loss_target=loss_target[0] if last else None)
        saved.append(sv)
        if not last:
            xs, normed = out[0], (out[1], out[2])
    loss_blk, dx, dxb = out
    loss = lax.psum(loss_blk[0, 0], ("x", "y", "c"))

    scattering = {}

    def scatter_starter(l):
        def scatter(nm, partials):
            land = lax.empty((N_PEER,) + partials.shape[1:], partials.dtype)
            fl, tok = _exchange_start([partials], [land], "scatter", name="scatter_start_%d_%s" % (l, nm))
            scattering[(l, nm)] = fl[0]
            return tok
        return scatter

    small = [None] * n_layers
    for l in reversed(range(n_layers)):
        dx, dxb, small[l] = _layer_bwd(dx, dxb, layers[l], weight_getter(l), scatter_starter(l), saved[l], dims)

    small_shapes = [s.shape for s in small[0]]
    small_src = [_pack([s for l in range(n_layers) for s in small[l]])]
    small_flights, small_token = _exchange_start(small_src, [_own_in_place(s, me) for s in small_src], "gather_all",
                                                 name="small_start")
    grad_x = dx[None]

    me1 = me.astype(jnp.int32).reshape(1)
    res = {nm: None for nm in big_names}
    after = small_token
    for l in reversed(range(n_layers)):
        for nm in reversed(big_names):
            own, landed = _exchange_wait(scattering[(l, nm)], after, "scatter", name="scatter_wait_%d_%s" % (l, nm))
            res[nm] = _adamw_layer(l, big_w[nm], big_m[nm], big_v[nm], own, landed, me1, res[nm], "adamw_" + nm)
            after = res[nm][0]
    res["w_in"] = [tr_in(a) for a in res["w_in"]]
    big_out = [res[nm] for nm in big_names]

    _, gathered_small = _exchange_wait(small_flights[0], after, "gather_all", name="small_wait")
    summed = _unpack(_sum_parts(gathered_small, "sum_small"), small_shapes * n_layers)
    ns = len(small_shapes)
    g_small = [jnp.stack([summed[l * ns + i] for l in range(n_layers)]) for i in range(ns)]
    g_attn_norm, g_sgu_w, g_sgu_b, g_conv_full, g_q, g_k, g_mlp_norm = g_small
    cs = conv_w.shape[2]
    g_conv = lax.dynamic_slice_in_dim(g_conv_full, me * cs, cs, axis=2)
    sm_w = (attn_norm, sgu_w, sgu_b, conv_w, q_norm, k_norm, mlp_norm)
    sm_m = (m_attn_norm, m_sgu_w, m_sgu_b, m_conv_w, m_q_norm, m_k_norm, m_mlp_norm)
    sm_v = (v_attn_norm, v_sgu_w, v_sgu_b, v_conv_w, v_q_norm, v_k_norm, v_mlp_norm)
    sm_g = (g_attn_norm, g_sgu_w, g_sgu_b, g_conv, g_q, g_k, g_mlp_norm)
    sm_delta, sm_m2, sm_v2 = _adamw_small(sm_w, sm_g, sm_m, sm_v, "adamw_small")

    def ordered(small_list, big_kind):
        b = [big_out[i][big_kind] for i in range(4)]
        return [small_list[0], b[0], small_list[1], small_list[2], small_list[3], small_list[4], small_list[5],
                b[1], small_list[6], b[2], b[3]]

    return (loss, grad_x, *ordered(list(sm_g), 0), *ordered(sm_delta, 1), *ordered(sm_m2, 2), *ordered(sm_v2, 3))
```

```python
import jax
import jax.numpy as jnp
from jax import lax
from jax.experimental import pallas as pl
from jax.experimental.pallas import tpu as pltpu

N_DEV = 8
HEAD_DIM = 64
CHUNK = 128
ATT_BLK = 128
DILATIONS = (1, 4, 16)
CONV_WIDTH = 3
EPS = 1e-6
ADAM_LR = 0.001
ADAM_B1 = 0.9
ADAM_B2 = 0.999
ADAM_EPS = 1e-08
ADAM_WD = 0.01
ADAM_STEP = 10
MXU_DTYPE = jnp.bfloat16
F32 = jnp.float32
LANE = 128
VMEM_PHYSICAL_BYTES = 64 * 1024 * 1024
VMEM_LIMIT_BYTES = 56 * 1024 * 1024
NEG_INF = float("-inf")


def _cparams(*sem):
    return pltpu.CompilerParams(dimension_semantics=sem, vmem_limit_bytes=VMEM_LIMIT_BYTES)


def _sds(shape, dtype):
    return jax.ShapeDtypeStruct(shape, dtype)


def _fit(n, tile):
    for t in range(min(tile, n) // LANE * LANE, 0, -LANE):
        if n % t == 0:
            return t
    return n


_HBM = pl.BlockSpec(memory_space=pltpu.HBM)
_SEM = pl.BlockSpec(memory_space=pltpu.SEMAPHORE)
_DATAFLOW = pltpu.SideEffectType.DATAFLOW_SIDE_EFFECTING
N_PEER = N_DEV - 1


def _mesh_pos():
    x, y, c = lax.axis_index("x"), lax.axis_index("y"), lax.axis_index("c")
    return x, y, c, 4 * x + 2 * y + c


OTHER_CHIPS = (4, 2, 6)
EXCHANGE_PEERS = dict(
    scatter=tuple(range(1, N_DEV)),
    gather_all=tuple(range(1, N_DEV)),
    gather=(1,) + OTHER_CHIPS,
    forward=OTHER_CHIPS)


def _remote_copies(src, land, send_sems, recv_sems, mode):
    x, y, c, me = _mesh_pos()
    copies = []
    for i, k in enumerate(EXCHANGE_PEERS[mode]):
        px = (1 - x) if (k & 4) else x
        py = (1 - y) if (k & 2) else y
        pc = (1 - c) if (k & 1) else c
        if mode == "scatter":
            src_ref, dst_ref, dev = src.at[4 * px + 2 * py + pc], land.at[i], (px, py, pc)
        elif mode == "forward":
            slot = 4 * px + 2 * py + c
            src_ref, dst_ref, dev = land.at[slot], land.at[slot], (x, y, 1 - c)
        else:
            src_ref, dst_ref, dev = src, land.at[me], (px, py, pc)
        copies.append(pltpu.make_async_remote_copy(
            src_ref=src_ref, dst_ref=dst_ref, send_sem=send_sems.at[i], recv_sem=recv_sems.at[i],
            device_id=dev, device_id_type=pl.DeviceIdType.MESH))
    return copies


def _own_in_place(src, me):
    land = lax.empty((N_DEV,) + src.shape, src.dtype)
    return lax.dynamic_update_slice(land, src[None], (me,) + (0,) * src.ndim)


def _exchange_start(srcs, lands, mode, name):
    n = len(lands)
    has_src = srcs is not None
    arrays = (list(srcs) if has_src else []) + list(lands)
    n_arr = len(arrays)
    n_copies = len(EXCHANGE_PEERS[mode])

    def body(*refs):
        src = refs[:n] if has_src else [None] * n
        land = refs[n_arr - n:n_arr]
        send, recv = refs[n_arr:n_arr + n], refs[n_arr + n:n_arr + 2 * n]
        token = refs[2 * n_arr + 2 * n]
        for t in range(n):
            for cp in _remote_copies(src[t], land[t], send[t], recv[t], mode):
                cp.start()
        token[...] = jnp.zeros_like(token)

    outs = pl.pallas_call(
        body, name=name,
        out_shape=([pltpu.SemaphoreType.DMA((n_copies,))] * (2 * n) + [pltpu.HBM(a.shape, a.dtype) for a in arrays]
                   + [_sds((8, LANE), F32)]),
        in_specs=[_HBM] * n_arr,
        out_specs=[_SEM] * (2 * n) + [_HBM] * n_arr + [pl.BlockSpec(memory_space=pltpu.VMEM)],
        input_output_aliases={i: 2 * n + i for i in range(n_arr)},
        compiler_params=pltpu.CompilerParams(has_side_effects=_DATAFLOW),
    )(*[pltpu.with_memory_space_constraint(a, pltpu.HBM) for a in arrays])
    thru = outs[2 * n:2 * n + n_arr]
    flights = [(outs[t], outs[n + t], thru[t] if has_src else None, thru[n_arr - n + t]) for t in range(n)]
    return flights, outs[2 * n + n_arr]


def _exchange_wait(flight, after, mode, name):
    send, recv, src, land = flight
    arrays = [land] if src is None else [src, land]
    n_arr = len(arrays)

    def body(*refs):
        src_ref = refs[0] if n_arr == 2 else None
        land_ref, send_ref, recv_ref = refs[n_arr - 1], refs[n_arr], refs[n_arr + 1]
        for cp in _remote_copies(src_ref, land_ref, send_ref, recv_ref, mode):
            cp.wait_send()
            cp.wait_recv()

    outs = pl.pallas_call(
        body, name=name, out_shape=[pltpu.HBM(a.shape, a.dtype) for a in arrays],
        in_specs=[_HBM] * n_arr + [_SEM, _SEM, pl.BlockSpec(memory_space=pl.ANY)], out_specs=[_HBM] * n_arr,
        input_output_aliases={i: i for i in range(n_arr)},
        compiler_params=pltpu.CompilerParams(has_side_effects=_DATAFLOW),
    )(*arrays, send, recv, after)
    return (None, outs[0]) if src is None else (outs[0], outs[1])


def _rmsnorm_fwd(x, g, name, dep=None, tr=1024):
    T, D = x.shape
    tr = _fit(T, tr)

    def body(x_ref, g_ref, *rest):
        h_ref, r_ref = rest[-2:]
        xv = x_ref[...]
        r = lax.rsqrt(jnp.mean(xv * xv, axis=-1, keepdims=True) + EPS)
        h_ref[...] = (xv * r * g_ref[...]).astype(h_ref.dtype)
        r_ref[...] = r

    in_specs = [pl.BlockSpec((tr, D), lambda i: (i, 0)), pl.BlockSpec((1, D), lambda i: (0, 0))]
    args = [x, g]
    if dep is not None:
        in_specs.append(pl.BlockSpec(dep.shape, lambda i: (0, 0)))
        args.append(dep)
    return pl.pallas_call(
        body, name=name, grid=(T // tr,),
        in_specs=in_specs,
        out_specs=[pl.BlockSpec((tr, D), lambda i: (i, 0)), pl.BlockSpec((tr, 1), lambda i: (i, 0))],
        out_shape=[_sds((T, D), MXU_DTYPE), _sds((T, 1), F32)],
        compiler_params=_cparams("parallel"),
    )(*args)


def _mm_relu2(a, b, name, tm=1024, tn=1024):
    M, K = a.shape
    N = b.shape[0] * b.shape[2]
    tm, tn = _fit(M, tm), _fit(b.shape[2], tn)
    per = b.shape[2] // tn

    def body(a_ref, b_ref, y_ref, y2_ref):
        acc = jnp.dot(a_ref[...], b_ref[...], preferred_element_type=F32)
        y_ref[...] = acc.astype(y_ref.dtype)
        rl = jnp.maximum(acc, 0.0)
        y2_ref[...] = (rl * rl).astype(y2_ref.dtype)

    out_blk = pl.BlockSpec((tm, tn), lambda i, j: (i, j))
    return pl.pallas_call(
        body, name=name, grid=(M // tm, N // tn),
        in_specs=[pl.BlockSpec((tm, K), lambda i, j: (i, 0)), pl.BlockSpec((None, K, tn), lambda i, j: (j // per, 0, j % per))],
        out_specs=[out_blk] * 2, out_shape=[_sds((M, N), MXU_DTYPE)] * 2,
        compiler_params=_cparams("parallel", "parallel"),
    )(a, b)


def _column_offsets(pieces):
    offs = [0]
    for piece in pieces:
        offs.append(offs[-1] + piece.shape[1])
    return offs


def _mm_residual_norm(a_pieces, b, residual, g, name, tm=512):
    n = len(a_pieces)
    M = a_pieces[0].shape[0]
    K, N = b.shape
    offs = _column_offsets(a_pieces)
    assert offs[-1] == K
    tm = _fit(M, tm)
    f32_row, mxu_row = tm * N * 4, tm * N * jnp.dtype(MXU_DTYPE).itemsize
    need = b.size * b.dtype.itemsize + 2 * (tm * K * a_pieces[0].dtype.itemsize + 2 * f32_row + mxu_row) + 5 * f32_row
    vmem_limit = max(VMEM_LIMIT_BYTES, min(need, VMEM_PHYSICAL_BYTES - 2 * 1024 * 1024))

    def body(*refs):
        a_refs, b_ref, res_ref, g_ref, x_ref, h_ref, r_ref = refs[:n], *refs[n:]
        xv = res_ref[...]
        for i in range(n):
            xv = xv + jnp.dot(a_refs[i][...], b_ref[offs[i]:offs[i + 1], :], preferred_element_type=F32)
        x_ref[...] = xv
        r = lax.rsqrt(jnp.mean(xv * xv, axis=-1, keepdims=True) + EPS)
        h_ref[...] = (xv * r * g_ref[...]).astype(h_ref.dtype)
        r_ref[...] = r

    row = pl.BlockSpec((tm, N), lambda i: (i, 0))
    return pl.pallas_call(
        body, name=name, grid=(M // tm,),
        in_specs=[pl.BlockSpec((tm, piece.shape[1]), lambda i: (i, 0)) for piece in a_pieces]
        + [pl.BlockSpec((K, N), lambda i: (0, 0), pipeline_mode=pl.Buffered(1)), row, pl.BlockSpec((1, N), lambda i: (0, 0))],
        out_specs=[row, row, pl.BlockSpec((tm, 1), lambda i: (i, 0))],
        out_shape=[_sds((M, N), F32), _sds((M, N), MXU_DTYPE), _sds((M, 1), F32)],
        compiler_params=pltpu.CompilerParams(dimension_semantics=("parallel",), vmem_limit_bytes=vmem_limit),
    )(*a_pieces, b, residual, g)


def _mm_norm_bwd(a, b, x, g, r, dres, name, dep=None, tm=256):
    M, K = a.shape
    grouped = b.ndim == 3
    N = b.shape[1]
    tm = _fit(M, tm)
    nt = (((1,), (1,)), ((), ()))
    f32_row, mxu_row = tm * N * 4, tm * N * jnp.dtype(MXU_DTYPE).itemsize
    need = b.size * b.dtype.itemsize + 2 * (tm * K * a.dtype.itemsize + 3 * f32_row + mxu_row) + 5 * f32_row
    vmem_limit = max(VMEM_LIMIT_BYTES, min(need, VMEM_PHYSICAL_BYTES - 2 * 1024 * 1024))

    def body(*refs):
        a_ref, b_ref, x_ref, g_ref, r_ref, dres_ref = refs[:6]
        dx_ref, dxb_ref, dg_ref = refs[-3:]

        @pl.when(pl.program_id(0) == 0)
        def _():
            dg_ref[...] = jnp.zeros_like(dg_ref)

        if grouped:
            kg = b.shape[2]
            dh = lax.dot_general(a_ref[:, 0:kg], b_ref[0], nt, preferred_element_type=F32)
            for i in range(1, b.shape[0]):
                dh += lax.dot_general(a_ref[:, i * kg:(i + 1) * kg], b_ref[i], nt, preferred_element_type=F32)
        else:
            dh = jnp.dot(a_ref[...], b_ref[...], preferred_element_type=F32)
        xv, rv = x_ref[...], r_ref[...]
        gdy = dh * g_ref[...]
        mean_xg = jnp.mean(xv * gdy, axis=-1, keepdims=True)
        dx = dres_ref[...] + rv * gdy - xv * (rv * rv * rv) * mean_xg
        dx_ref[...] = dx
        dxb_ref[...] = dx.astype(dxb_ref.dtype)
        dg_ref[...] += jnp.sum(dh * xv * rv, axis=0, keepdims=True)

    row = pl.BlockSpec((tm, N), lambda i: (i, 0))
    in_specs = [pl.BlockSpec((tm, K), lambda i: (i, 0)),
                pl.BlockSpec(b.shape, lambda i: (0,) * b.ndim, pipeline_mode=pl.Buffered(1)),
                row, pl.BlockSpec((1, N), lambda i: (0, 0)), pl.BlockSpec((tm, 1), lambda i: (i, 0)), row]
    args = [a, b, x, g, r, dres]
    if dep is not None:
        in_specs.append(pl.BlockSpec(dep.shape, lambda i: (0, 0)))
        args.append(dep)
    return pl.pallas_call(
        body, name=name, grid=(M // tm,),
        in_specs=in_specs, out_specs=[row, row, pl.BlockSpec((1, N), lambda i: (0, 0))],
        out_shape=[_sds((M, N), F32), _sds((M, N), MXU_DTYPE), _sds((1, N), F32)],
        compiler_params=pltpu.CompilerParams(dimension_semantics=("arbitrary",), vmem_limit_bytes=vmem_limit),
    )(*args)


def _mm_residual_loss(a, b, residual, target, name, tm=512, tn=1024):
    M, K = a.shape
    N = b.shape[1]
    tm, tn = _fit(M, tm), _fit(N, tn)

    def body(a_ref, b_ref, res_ref, t_ref, loss_ref, dx_ref, dxb_ref):
        @pl.when((pl.program_id(0) == 0) & (pl.program_id(1) == 0))
        def _():
            loss_ref[...] = jnp.zeros_like(loss_ref)

        err = jnp.dot(a_ref[...], b_ref[...], preferred_element_type=F32) + res_ref[...] - t_ref[...]
        sq = jnp.sum(jnp.sum(err * err, axis=-1, keepdims=True), axis=0, keepdims=True)
        loss_ref[...] += (0.5 / N) * sq
        dx = err * (1.0 / N)
        dx_ref[...] = dx
        dxb_ref[...] = dx.astype(dxb_ref.dtype)

    blk = pl.BlockSpec((tm, tn), lambda j, i: (i, j))
    return pl.pallas_call(
        body, name=name, grid=(N // tn, M // tm),
        in_specs=[pl.BlockSpec((tm, K), lambda j, i: (i, 0)),
                  pl.BlockSpec((K, tn), lambda j, i: (0, j), pipeline_mode=pl.Buffered(1)), blk, blk],
        out_specs=[pl.BlockSpec((8, LANE), lambda j, i: (0, 0)), blk, blk],
        out_shape=[_sds((8, LANE), F32), _sds((M, N), F32), _sds((M, N), MXU_DTYPE)],
        compiler_params=_cparams("arbitrary", "arbitrary"),
    )(a, b, residual, target)


def _mm_nt(a, b, name, out_dtype=F32, relu2_pre=None, dep=None, tm=512, tn=512):
    M, K = a.shape
    N = b.shape[0]
    tm, tn = _fit(M, tm), _fit(N, tn)
    b_spec = pl.BlockSpec((tn, K), lambda i, j: (j, 0))

    def body(*refs):
        a_ref, b_ref = refs[0], refs[1]
        p_ref = refs[2] if relu2_pre is not None else None
        out_ref = refs[2 + (relu2_pre is not None) + (dep is not None)]
        acc = lax.dot_general(a_ref[...], b_ref[...], (((1,), (1,)), ((), ())), preferred_element_type=F32)
        if p_ref is not None:
            acc = acc * (2.0 * jnp.maximum(p_ref[...].astype(F32), 0.0))
        out_ref[...] = acc.astype(out_ref.dtype)

    out_blk = pl.BlockSpec((tm, tn), lambda i, j: (i, j))
    in_specs = [pl.BlockSpec((tm, K), lambda i, j: (i, 0)), b_spec]
    args = [a, b]
    if relu2_pre is not None:
        in_specs.append(out_blk)
        args.append(relu2_pre)
    if dep is not None:
        in_specs.append(pl.BlockSpec(dep.shape, lambda i, j: (0, 0)))
        args.append(dep)
    return pl.pallas_call(
        body, name=name, grid=(M // tm, N // tn),
        in_specs=in_specs, out_specs=out_blk, out_shape=_sds((M, N), out_dtype),
        compiler_params=_cparams("parallel", "parallel"),
    )(*args)


def _wgrad_wide_a(a, b, name, tm=512):
    T, M = a.shape
    N = b.shape[1]
    tm = _fit(M, tm)

    def body(a_ref, b_ref, out_ref):
        out_ref[...] = lax.dot_general(a_ref[...], b_ref[...], (((0,), (0,)), ((), ())),
                                       preferred_element_type=F32).astype(out_ref.dtype)

    return pl.pallas_call(
        body, name=name, grid=(M // tm,),
        in_specs=[pl.BlockSpec((T, tm), lambda i: (0, i)),
                  pl.BlockSpec((T, N), lambda i: (0, 0), pipeline_mode=pl.Buffered(1))],
        out_specs=pl.BlockSpec((tm, N), lambda i: (i, 0)), out_shape=_sds((M, N), MXU_DTYPE),
        compiler_params=_cparams("parallel"),
    )(a, b)


def _wgrad_wide_b(a_pieces, b, name, groups=None, tn=512, t_chunk=512):
    n = len(a_pieces)
    T = a_pieces[0].shape[0]
    offs = _column_offsets(a_pieces)
    M = offs[-1]
    N = b.shape[1]
    tn = _fit(N if groups is None else N // groups, tn)
    t_chunk = _fit(T, t_chunk)

    def body(*refs):
        a_refs, b_ref, out_ref, at_ref = refs[:n], *refs[n:]

        @pl.when(pl.program_id(0) == 0)
        def _():
            for i in range(n):
                for c in range(0, T, t_chunk):
                    at_ref[offs[i]:offs[i + 1], c:c + t_chunk] = a_refs[i][c:c + t_chunk, :].T

        out_ref[...] = jnp.dot(at_ref[...], b_ref[...], preferred_element_type=F32).astype(out_ref.dtype)

    if groups is None:
        out_spec = pl.BlockSpec((M, tn), lambda j: (0, j))
        out_shape = _sds((M, N), MXU_DTYPE)
    else:
        per = N // groups // tn
        out_spec = pl.BlockSpec((None, M, tn), lambda j: (j // per, 0, j % per))
        out_shape = _sds((groups, M, N // groups), MXU_DTYPE)
    return pl.pallas_call(
        body, name=name, grid=(N // tn,),
        in_specs=[pl.BlockSpec(piece.shape, lambda j: (0, 0), pipeline_mode=pl.Buffered(1)) for piece in a_pieces]
        + [pl.BlockSpec((T, tn), lambda j: (0, j))],
        out_specs=out_spec, out_shape=out_shape,
        scratch_shapes=[pltpu.VMEM((M, T), MXU_DTYPE)],
        compiler_params=_cparams("arbitrary"),
    )(*a_pieces, b)


SGU_ROWS = 512


def _sgu_mixed(v, w_ref, b_ref, n_heads):
    parts = [jnp.dot(w_ref[h], v[:, h * HEAD_DIM:(h + 1) * HEAD_DIM], preferred_element_type=F32) for h in range(n_heads)]
    return jnp.concatenate(parts, axis=1) + b_ref[...]


def _sgu_fwd(p, w_tril, bmat, name):
    T = p.shape[0]
    H = w_tril.shape[0]
    AW = H * HEAD_DIM
    rows = _fit(T, SGU_ROWS)

    def body(u_ref, v_ref, w_ref, b_ref, y_ref):
        for c in range(0, rows, CHUNK):
            ch = pl.ds(c, CHUNK)
            mixed = _sgu_mixed(v_ref[ch, :].astype(MXU_DTYPE), w_ref, b_ref, H)
            y_ref[ch, :] = (u_ref[ch, :].astype(F32) * mixed).astype(y_ref.dtype)

    const3 = lambda c: (0, 0, 0)
    return pl.pallas_call(
        body, name=name, grid=(T // rows,),
        in_specs=[pl.BlockSpec((rows, AW), lambda c: (c, 0)), pl.BlockSpec((rows, AW), lambda c: (c, 1)),
                  pl.BlockSpec((H, CHUNK, CHUNK), const3), pl.BlockSpec((CHUNK, AW), lambda c: (0, 0))],
        out_specs=pl.BlockSpec((rows, AW), lambda c: (c, 0)),
        out_shape=_sds((T, AW), MXU_DTYPE),
        compiler_params=_cparams("parallel"),
    )(p, p, w_tril, bmat)


def _sgu_bwd(dymix, p, w_tril, w_tril_t, bmat, name):
    T = p.shape[0]
    H = w_tril.shape[0]
    AW = H * HEAD_DIM
    rows = _fit(T, SGU_ROWS)

    def body(dy_ref, u_ref, v_ref, w_ref, wt_ref, b_ref, du_ref, dv_ref, dw_ref, db_ref):
        @pl.when(pl.program_id(0) == 0)
        def _():
            dw_ref[...] = jnp.zeros_like(dw_ref)
            db_ref[...] = jnp.zeros_like(db_ref)

        for c in range(0, rows, CHUNK):
            ch = pl.ds(c, CHUNK)
            v = v_ref[ch, :].astype(MXU_DTYPE)
            dy = dy_ref[ch, :].astype(F32)
            du_ref[ch, :] = (dy * _sgu_mixed(v, w_ref, b_ref, H)).astype(du_ref.dtype)
            dm = dy * u_ref[ch, :].astype(F32)
            db_ref[...] += dm
            dm_c = dm.astype(MXU_DTYPE)
            dv = []
            for h in range(H):
                sl = slice(h * HEAD_DIM, (h + 1) * HEAD_DIM)
                dv.append(jnp.dot(wt_ref[h], dm_c[:, sl], preferred_element_type=F32))
                dw_ref[h] += lax.dot_general(dm_c[:, sl], v[:, sl], (((1,), (1,)), ((), ())), preferred_element_type=F32)
            dv_ref[ch, :] = jnp.concatenate(dv, axis=1).astype(dv_ref.dtype)

    const3 = lambda c: (0, 0, 0)
    blk = pl.BlockSpec((rows, AW), lambda c: (c, 0))
    return pl.pallas_call(
        body, name=name, grid=(T // rows,),
        in_specs=[blk, blk, pl.BlockSpec((rows, AW), lambda c: (c, 1)),
                  pl.BlockSpec((H, CHUNK, CHUNK), const3), pl.BlockSpec((H, CHUNK, CHUNK), const3),
                  pl.BlockSpec((CHUNK, AW), lambda c: (0, 0))],
        out_specs=[blk, blk, pl.BlockSpec((H, CHUNK, CHUNK), const3), pl.BlockSpec((CHUNK, AW), lambda c: (0, 0))],
        out_shape=[_sds((T, AW), MXU_DTYPE), _sds((T, AW), MXU_DTYPE), _sds((H, CHUNK, CHUNK), F32), _sds((CHUNK, AW), F32)],
        compiler_params=_cparams("arbitrary"),
    )(dymix, p, p, w_tril, w_tril_t, bmat)


def _shift_down(z, s, row):
    return jnp.where(row >= s, pltpu.roll(z, s, 0), 0.0)


def _shift_up(z, s, row, T):
    return jnp.where(row < T - s, pltpu.roll(z, T - s, 0), 0.0)


def _conv_fwd(p, w_conv, AW, name):
    T = p.shape[0]
    BW = w_conv.shape[1]
    nb = BW // LANE
    b0 = 2 * AW // LANE

    def body(b_ref, c_ref, x_ref, w_ref, y_ref):
        row = lax.broadcasted_iota(jnp.int32, (T, LANE), 0)
        z = c_ref[...].astype(F32) * x_ref[...].astype(F32)
        w0, w1, w2 = w_ref[0:1, :], w_ref[1:2, :], w_ref[2:3, :]
        conv = w2 * z + w1 * _shift_down(z, 1, row) + w0 * _shift_down(z, 2, row)
        y_ref[...] = (b_ref[...].astype(F32) * conv).astype(y_ref.dtype)

    return pl.pallas_call(
        body, name=name, grid=(nb,),
        in_specs=[pl.BlockSpec((T, LANE), lambda j: (0, b0 + j)), pl.BlockSpec((T, LANE), lambda j: (0, b0 + nb + j)),
                  pl.BlockSpec((T, LANE), lambda j: (0, b0 + 2 * nb + j)), pl.BlockSpec((CONV_WIDTH, LANE), lambda j: (0, j))],
        out_specs=pl.BlockSpec((T, LANE), lambda j: (0, j)),
        out_shape=_sds((T, BW), MXU_DTYPE),
        compiler_params=_cparams("parallel"),
    )(p, p, p, w_conv)


def _conv_bwd(dymix, p, w_conv, AW, name):
    T = p.shape[0]
    BW = w_conv.shape[1]
    nb = BW // LANE
    b0 = 2 * AW // LANE
    y0 = AW // LANE

    def body(dy_ref, b_ref, c_ref, x_ref, w_ref, db_ref, dc_ref, dxb_ref, dw_ref):
        row = lax.broadcasted_iota(jnp.int32, (T, LANE), 0)
        cv, xv, dy = c_ref[...].astype(F32), x_ref[...].astype(F32), dy_ref[...].astype(F32)
        w0, w1, w2 = w_ref[0:1, :], w_ref[1:2, :], w_ref[2:3, :]
        z = cv * xv
        z1 = _shift_down(z, 1, row)
        z2 = _shift_down(z, 2, row)
        conv = w2 * z + w1 * z1 + w0 * z2
        db_ref[...] = (dy * conv).astype(db_ref.dtype)
        dconv = dy * b_ref[...].astype(F32)
        dz = w2 * dconv + w1 * _shift_up(dconv, 1, row, T) + w0 * _shift_up(dconv, 2, row, T)
        dc_ref[...] = (dz * xv).astype(dc_ref.dtype)
        dxb_ref[...] = (dz * cv).astype(dxb_ref.dtype)
        dw_ref[0:1, :] = jnp.sum(dconv * z2, axis=0, keepdims=True)
        dw_ref[1:2, :] = jnp.sum(dconv * z1, axis=0, keepdims=True)
        dw_ref[2:3, :] = jnp.sum(dconv * z, axis=0, keepdims=True)

    col = lambda j: (0, j)
    return pl.pallas_call(
        body, name=name, grid=(nb,),
        in_specs=[pl.BlockSpec((T, LANE), lambda j: (0, y0 + j)),
                  pl.BlockSpec((T, LANE), lambda j: (0, b0 + j)), pl.BlockSpec((T, LANE), lambda j: (0, b0 + nb + j)),
                  pl.BlockSpec((T, LANE), lambda j: (0, b0 + 2 * nb + j)), pl.BlockSpec((CONV_WIDTH, LANE), col)],
        out_specs=[pl.BlockSpec((T, LANE), col)] * 3 + [pl.BlockSpec((CONV_WIDTH, LANE), col)],
        out_shape=[_sds((T, BW), MXU_DTYPE)] * 3 + [_sds((CONV_WIDTH, BW), F32)],
        compiler_params=_cparams("parallel"),
    )(dymix, p, p, p, w_conv)


def _head_sum(x, col_head, n_heads):
    out = jnp.zeros_like(x)
    for h in range(n_heads):
        sel = col_head == h
        out = jnp.where(sel, jnp.sum(jnp.where(sel, x, 0.0), axis=-1, keepdims=True), out)
    return out


def _same_head(width):
    assert width == 2 * HEAD_DIM
    return lax.broadcasted_iota(jnp.int32, (1, width), 1) < HEAD_DIM


def _head_sum2(x, first):
    s0 = jnp.sum(jnp.where(first, x, 0.0), axis=-1, keepdims=True)
    s1 = jnp.sum(jnp.where(first, 0.0, x), axis=-1, keepdims=True)
    return jnp.where(first, s0, s1)


def _head_norm(x, g, first):
    r = lax.rsqrt(_head_sum2(x * x, first) * (1.0 / HEAD_DIM) + EPS)
    return x * r * g, r


def _head_norm_bwd(dy, x, g, r, first):
    gdy = dy * g
    mean_xg = _head_sum2(x * gdy, first) * (1.0 / HEAD_DIM)
    return r * gdy - x * (r * r * r) * mean_xg, dy * x * r


ATT_SPAN_MIN = 1024
ATT_FWD_UNROLL = 4
ATT_BWD_UNROLL = 4
HEADS_PER_LANES = LANE // HEAD_DIM


def _attn_geometry(T, d):
    m = max(1, ATT_SPAN_MIN // (ATT_BLK * d))
    return m, ATT_BLK * d * m, ATT_BLK * d, T // (ATT_BLK * d)


def _rows(ref, start, d):
    return ref[pl.ds(start, ATT_BLK, stride=d), :] if d > 1 else ref[pl.ds(start, ATT_BLK), :]


def _set_rows(ref, start, d, value):
    if d > 1:
        ref[pl.ds(start, ATT_BLK, stride=d), :] = value
    else:
        ref[pl.ds(start, ATT_BLK), :] = value


def _for_each_block(d, m, task, unroll):
    for j in range(m):
        if d == 1:
            task(0, j)
        else:
            lax.fori_loop(0, d, lambda r, carry, j=j: (task(r, j), carry)[1], 0, unroll=min(unroll, d))


def _head_slices():
    return [slice(h * HEAD_DIM, (h + 1) * HEAD_DIM) for h in range(HEADS_PER_LANES)]


def _qk_norm_fwd(p, gains, q_start, PW, name, tr=1024):
    T = p.shape[0]
    tr = _fit(T, tr)
    n_norm = gains.shape[1] // PW
    n = n_norm * 3 // 2
    c0 = q_start // PW

    def body(*refs):
        x_refs, g_ref, out_ref = refs[:n], refs[n], refs[n + 1]
        first = _same_head(LANE)
        for i in range(n):
            for c in range(0, PW, LANE):
                lo = i * PW + c
                x = x_refs[i][:, c:c + LANE].astype(F32)
                out_ref[:, lo:lo + LANE] = _head_norm(x, g_ref[:, lo:lo + LANE], first)[0] if i < n_norm else x

    return pl.pallas_call(
        body, name=name, grid=(T // tr,),
        in_specs=[pl.BlockSpec((tr, PW), lambda i, j=j: (i, c0 + j)) for j in range(n)]
        + [pl.BlockSpec((1, n_norm * PW), lambda i: (0, 0))],
        out_specs=pl.BlockSpec((tr, n * PW), lambda i: (i, 0)), out_shape=_sds((T, n * PW), F32),
        compiler_params=_cparams("parallel"),
    )(*([p] * n), gains)


def _qk_norm_bwd(dns, p, gains, q_start, PW, name, tr=1024):
    T = p.shape[0]
    tr = _fit(T, tr)
    n = len(dns)
    c0 = q_start // PW

    def body(*refs):
        d_refs, x_refs, g_ref, out_ref, acc_ref = refs[:n], refs[n:2 * n], refs[2 * n], refs[2 * n + 1], refs[2 * n + 2]

        @pl.when(pl.program_id(0) == 0)
        def _():
            acc_ref[...] = jnp.zeros_like(acc_ref)

        first = _same_head(LANE)
        for i in range(n):
            for c in range(0, PW, LANE):
                lo = i * PW + c
                x, gv = x_refs[i][:, c:c + LANE].astype(F32), g_ref[:, lo:lo + LANE]
                _, r = _head_norm(x, gv, first)
                dx, g_part = _head_norm_bwd(d_refs[i][:, c:c + LANE], x, gv, r, first)
                out_ref[:, lo:lo + LANE] = dx.astype(out_ref.dtype)
                acc_ref[0:1, lo:lo + LANE] += jnp.sum(g_part, axis=0, keepdims=True)

    return pl.pallas_call(
        body, name=name, grid=(T // tr,),
        in_specs=[pl.BlockSpec((tr, PW), lambda i: (i, 0))] * n
        + [pl.BlockSpec((tr, PW), lambda i, j=j: (i, c0 + j)) for j in range(n)]
        + [pl.BlockSpec((1, n * PW), lambda i: (0, 0))],
        out_specs=[pl.BlockSpec((tr, n * PW), lambda i: (i, 0)), pl.BlockSpec((8, n * PW), lambda i: (0, 0))],
        out_shape=[_sds((T, n * PW), MXU_DTYPE), _sds((8, n * PW), F32)],
        compiler_params=_cparams("arbitrary"),
    )(*dns, *([p] * n), gains)


def _attn_fwd(qkv, g, d, PW, name):
    T = qkv.shape[0]
    B, W = ATT_BLK, LANE
    m, span, group, _ = _attn_geometry(T, d)
    c_q = g * PW // W
    c_k, c_v = c_q + 3 * PW // W, c_q + 6 * PW // W
    scale = HEAD_DIM ** -0.5

    def body(q_ref, k_ref, v_ref, kp_ref, vp_ref, o_ref, lse_ref):
        n = pl.program_id(1)
        qi = lax.broadcasted_iota(jnp.int32, (B, 2 * B), 0)
        kj = lax.broadcasted_iota(jnp.int32, (B, 2 * B), 1)
        band = (kj >= qi) & (kj <= qi + B)

        def task(r, j):
            cur = j * group + r
            if j == 0:
                kp, vp = _rows(kp_ref, r, d), _rows(vp_ref, r, d)
            else:
                kp, vp = _rows(k_ref, cur - group, d), _rows(v_ref, cur - group, d)
            mask = band & ((n * m + j > 0) | (kj >= B))
            qn = _rows(q_ref, cur, d).astype(MXU_DTYPE)
            kn = jnp.concatenate([kp, _rows(k_ref, cur, d)], axis=0).astype(MXU_DTYPE)
            vcat = jnp.concatenate([vp, _rows(v_ref, cur, d)], axis=0).astype(MXU_DTYPE)
            o_parts, lse_parts = [], []
            for sl in _head_slices():
                s = lax.dot_general(qn[:, sl], kn[:, sl], (((1,), (1,)), ((), ())), preferred_element_type=F32) * scale
                s = jnp.where(mask, s, NEG_INF)
                mx = jnp.max(s, axis=-1, keepdims=True)
                e = jnp.exp(s - mx)
                den = jnp.sum(e, axis=-1, keepdims=True)
                o_parts.append(jnp.dot(e.astype(MXU_DTYPE), vcat[:, sl], preferred_element_type=F32) / den)
                lse_parts.append(jnp.broadcast_to(mx + jnp.log(den), (B, HEAD_DIM)))
            _set_rows(o_ref, cur, d, jnp.concatenate(o_parts, axis=1))
            _set_rows(lse_ref, cur, d, jnp.concatenate(lse_parts, axis=1))

        _for_each_block(d, m, task, ATT_FWD_UNROLL)

    main = lambda c0: pl.BlockSpec((span, W), lambda hp, n: (n, c0 + hp))
    prev = lambda c0: pl.BlockSpec((group, W), lambda hp, n: (jnp.maximum(n * m - 1, 0), c0 + hp))
    out_blk = pl.BlockSpec((span, W), lambda hp, n: (n, hp))
    return pl.pallas_call(
        body, name=name, grid=(PW // W, T // span),
        in_specs=[main(c_q), main(c_k), main(c_v), prev(c_k), prev(c_v)],
        out_specs=[out_blk, out_blk],
        out_shape=[_sds((T, PW), F32), _sds((T, PW), F32)],
        compiler_params=_cparams("parallel", "parallel"),
    )(qkv, qkv, qkv, qkv, qkv)


def _attn_bwd(qkv, lse, do, corr, g, d, PW, name):
    T = qkv.shape[0]
    B, W = ATT_BLK, LANE
    m, span, group, n_blocks = _attn_geometry(T, d)
    c_q = g * PW // W
    c_k, c_v = c_q + 3 * PW // W, c_q + 6 * PW // W
    scale = HEAD_DIM ** -0.5
    nt = (((1,), (1,)), ((), ()))
    tn = (((0,), (0,)), ((), ()))

    def body(q_ref, k_ref, v_ref, do_ref, l_ref, c_ref, kp_ref, vp_ref, qx_ref, dox_ref, lx_ref, cx_ref,
             dq_ref, dk_ref, dv_ref):
        n = pl.program_id(1)
        i1 = lax.broadcasted_iota(jnp.int32, (B, B), 0)
        j1 = lax.broadcasted_iota(jnp.int32, (B, B), 1)
        i2 = lax.broadcasted_iota(jnp.int32, (2 * B, B), 0)
        j2 = lax.broadcasted_iota(jnp.int32, (2 * B, B), 1)

        def task(r, j):
            cur = j * group + r
            blk = n * m + j
            q_c, k_c, v_c = _rows(q_ref, cur, d), _rows(k_ref, cur, d), _rows(v_ref, cur, d)
            do_c, l_c, c_c = _rows(do_ref, cur, d), _rows(l_ref, cur, d), _rows(c_ref, cur, d)
            if j == 0:
                k_p, v_p = _rows(kp_ref, r, d), _rows(vp_ref, r, d)
            else:
                k_p, v_p = _rows(k_ref, cur - group, d), _rows(v_ref, cur - group, d)
            if j == m - 1:
                nxt = [_rows(ref, r, d) for ref in (qx_ref, dox_ref, lx_ref, cx_ref)]
            else:
                nxt = [_rows(ref, cur + group, d) for ref in (q_ref, do_ref, l_ref, c_ref)]
            q_x, do_x, l_x, c_x = nxt
            kn_c, kn_p, v_c, v_p = (a.astype(MXU_DTYPE) for a in (k_c, k_p, v_c, v_p))
            qn_c = q_c.astype(MXU_DTYPE)
            qn_cat = jnp.concatenate([qn_c, q_x.astype(MXU_DTYPE)], axis=0)
            do_cb = do_c.astype(MXU_DTYPE)
            do_cat = jnp.concatenate([do_cb, do_x.astype(MXU_DTYPE)], axis=0)
            l_cat = jnp.concatenate([l_c, l_x], axis=0)
            c_cat = jnp.concatenate([c_c, c_x], axis=0)
            mask_p = (j1 >= i1) & (blk > 0)
            mask_c = ((i2 < B) & (j2 <= i2)) | ((i2 >= B) & (j2 >= i2 - B) & (blk + 1 < n_blocks))
            dqn, dkn, dv = [], [], []
            for h, sl in enumerate(_head_slices()):
                lane = slice(h * HEAD_DIM, h * HEAD_DIM + 1)
                s_p = lax.dot_general(qn_c[:, sl], kn_p[:, sl], nt, preferred_element_type=F32) * scale
                pr_p = jnp.where(mask_p, jnp.exp(s_p - l_c[:, lane]), 0.0)
                dp_p = lax.dot_general(do_cb[:, sl], v_p[:, sl], nt, preferred_element_type=F32)
                ds_p = (pr_p * (dp_p + c_c[:, lane]) * scale).astype(MXU_DTYPE)
                s_c = lax.dot_general(qn_cat[:, sl], kn_c[:, sl], nt, preferred_element_type=F32) * scale
                pr_c = jnp.where(mask_c, jnp.exp(s_c - l_cat[:, lane]), 0.0)
                dp_c = lax.dot_general(do_cat[:, sl], v_c[:, sl], nt, preferred_element_type=F32)
                ds_c = (pr_c * (dp_c + c_cat[:, lane]) * scale).astype(MXU_DTYPE)
                dqn.append(jnp.dot(ds_p, kn_p[:, sl], preferred_element_type=F32)
                           + jnp.dot(ds_c[:B], kn_c[:, sl], preferred_element_type=F32))
                dkn.append(lax.dot_general(ds_c, qn_cat[:, sl], tn, preferred_element_type=F32))
                dv.append(lax.dot_general(pr_c.astype(MXU_DTYPE), do_cat[:, sl], tn, preferred_element_type=F32))
            _set_rows(dq_ref, cur, d, jnp.concatenate(dqn, axis=1))
            _set_rows(dk_ref, cur, d, jnp.concatenate(dkn, axis=1))
            _set_rows(dv_ref, cur, d, jnp.concatenate(dv, axis=1))

        _for_each_block(d, m, task, ATT_BWD_UNROLL)

    main = lambda c0: pl.BlockSpec((span, W), lambda hp, n: (n, c0 + hp))
    prev = lambda c0: pl.BlockSpec((group, W), lambda hp, n: (jnp.maximum(n * m - 1, 0), c0 + hp))
    nxt = lambda c0: pl.BlockSpec((group, W), lambda hp, n: (jnp.minimum((n + 1) * m, n_blocks - 1), c0 + hp))
    own = pl.BlockSpec((span, W), lambda hp, n: (n, hp))
    return pl.pallas_call(
        body, name=name, grid=(PW // W, T // span),
        in_specs=[main(c_q), main(c_k), main(c_v), main(0), main(0), main(0), prev(c_k), prev(c_v),
                  nxt(c_q), nxt(0), nxt(0), nxt(0)],
        out_specs=[own, own, own],
        out_shape=[_sds((T, PW), F32)] * 3,
        compiler_params=_cparams("parallel", "parallel"),
    )(qkv, qkv, qkv, do, lse, corr, qkv, qkv, qkv, do, lse, corr)


def _softmax3(lses):
    mx = jnp.maximum(jnp.maximum(lses[0], lses[1]), lses[2])
    ex = [jnp.exp(l - mx) for l in lses]
    inv = 1.0 / (ex[0] + ex[1] + ex[2])
    return [e * inv for e in ex]


def _mix_fwd(os_, lses, name, tr=1024):
    T, PW = os_[0].shape
    tr = _fit(T, tr)

    def body(o0, o1, o2, l0, l1, l2, y_ref):
        alpha = _softmax3([l0[...], l1[...], l2[...]])
        for g, o_ref in enumerate((o0, o1, o2)):
            y_ref[:, g * PW:(g + 1) * PW] = (o_ref[...] * alpha[g]).astype(y_ref.dtype)

    blk = pl.BlockSpec((tr, PW), lambda i: (i, 0))
    return pl.pallas_call(
        body, name=name, grid=(T // tr,),
        in_specs=[blk] * 6, out_specs=pl.BlockSpec((tr, 3 * PW), lambda i: (i, 0)),
        out_shape=_sds((T, 3 * PW), MXU_DTYPE),
        compiler_params=_cparams("parallel"),
    )(*os_, *lses)


def _mix_bwd(dymix, os_, lses, c_start, name, tr=1024):
    T, PW = os_[0].shape
    tr = _fit(T, tr)
    HP = PW // HEAD_DIM
    c0 = c_start // PW

    def body(d0, d1, d2, o0, o1, o2, l0, l1, l2, do0, do1, do2, dl0, dl1, dl2):
        col_head = lax.broadcasted_iota(jnp.int32, (tr, PW), 1) // HEAD_DIM
        alpha = _softmax3([l0[...], l1[...], l2[...]])
        dys = [d0[...].astype(F32), d1[...].astype(F32), d2[...].astype(F32)]
        dots = [_head_sum(dy * o_ref[...], col_head, HP) for dy, o_ref in zip(dys, (o0, o1, o2))]
        mean_dot = alpha[0] * dots[0] + alpha[1] * dots[1] + alpha[2] * dots[2]
        for g, (do_ref, dl_ref) in enumerate(((do0, dl0), (do1, dl1), (do2, dl2))):
            do_ref[...] = dys[g] * alpha[g]
            dl_ref[...] = -alpha[g] * mean_dot

    blk = pl.BlockSpec((tr, PW), lambda i: (i, 0))
    dy_specs = [pl.BlockSpec((tr, PW), lambda i, g=g: (i, c0 + g)) for g in range(3)]
    outs = pl.pallas_call(
        body, name=name, grid=(T // tr,),
        in_specs=dy_specs + [blk] * 6, out_specs=[blk] * 6,
        out_shape=[_sds((T, PW), F32)] * 6,
        compiler_params=_cparams("parallel"),
    )(dymix, dymix, dymix, *os_, *lses)
    return outs[:3], outs[3:]


def _adamw_math(w, g, m, v):
    m2 = ADAM_B1 * m + (1.0 - ADAM_B1) * g
    v2 = ADAM_B2 * v + (1.0 - ADAM_B2) * (g * g)
    m_hat = m2 / (1.0 - ADAM_B1 ** ADAM_STEP)
    v_hat = v2 / (1.0 - ADAM_B2 ** ADAM_STEP)
    delta = -ADAM_LR * (m_hat / (jnp.sqrt(v_hat) + ADAM_EPS) + ADAM_WD * w)
    return delta, m2, v2


def _adamw_layer(layer, w, m, v, own, landed, me, prev, name, tr=256):
    _, R, C = w.shape
    tr = next(t for t in range(min(tr, R) // 16 * 16, 0, -16) if R % t == 0)

    def body(me_ref, w_ref, m_ref, v_ref, own_ref, land_ref, *rest):
        g_ref, d_ref, m2_ref, v2_ref = rest[-4:]
        g = own_ref[...].astype(F32)
        for j in range(N_PEER):
            g = g + land_ref[j].astype(F32)
        delta, m2, v2 = _adamw_math(w_ref[...], g, m_ref[...], v_ref[...])
        g_ref[...] = g
        d_ref[...] = delta
        m2_ref[...] = m2
        v2_ref[...] = v2

    lay = pl.BlockSpec((None, tr, C), lambda i, me_ref: (layer, i, 0))
    in_specs = [lay, lay, lay, pl.BlockSpec((None, tr, C), lambda i, me_ref: (me_ref[0], i, 0)),
                pl.BlockSpec((N_PEER, tr, C), lambda i, me_ref: (0, i, 0))]
    args = [me, w, m, v, own, landed]
    aliases = {}
    if prev is not None:
        in_specs += [pl.BlockSpec(memory_space=pl.ANY)] * 4
        args += list(prev)
        aliases = {6 + i: i for i in range(4)}
    return pl.pallas_call(
        body, name=name,
        grid_spec=pltpu.PrefetchScalarGridSpec(num_scalar_prefetch=1, grid=(R // tr,), in_specs=in_specs, out_specs=[lay] * 4),
        out_shape=[_sds(w.shape, F32)] * 4,
        input_output_aliases=aliases,
        compiler_params=_cparams("parallel"),
    )(*args)


def _sum_parts(parts, name):
    _, R, C = parts.shape

    def body(p_ref, out_ref):
        g = p_ref[0]
        for j in range(1, N_DEV):
            g = g + p_ref[j]
        out_ref[...] = g

    return pl.pallas_call(
        body, name=name, grid=(1,),
        in_specs=[pl.BlockSpec((N_DEV, R, C), lambda i: (0, 0, 0))], out_specs=pl.BlockSpec((R, C), lambda i: (0, 0)),
        out_shape=_sds((R, C), F32), compiler_params=_cparams("arbitrary"),
    )(parts)


def _adamw_small(ws, gs, ms, vs, name):
    n = len(ws)

    def body(*refs):
        ins, outs = refs[:4 * n], refs[4 * n:]
        for i in range(n):
            delta, m2, v2 = _adamw_math(ins[i][...], ins[n + i][...], ins[2 * n + i][...], ins[3 * n + i][...])
            outs[i][...] = delta
            outs[n + i][...] = m2
            outs[2 * n + i][...] = v2

    outs = pl.pallas_call(
        body, name=name, out_shape=[_sds(w.shape, F32) for w in ws] * 3,
    )(*ws, *gs, *ms, *vs)
    return outs[:n], outs[n:2 * n], outs[2 * n:]


def _pack(arrays, rows_multiple=8):
    flat = []
    for a in arrays:
        a = a.reshape(-1).astype(F32)
        flat.append(jnp.pad(a, (0, (-a.shape[0]) % LANE)))
    flat = jnp.concatenate(flat)
    flat = jnp.pad(flat, (0, (-flat.shape[0]) % (LANE * rows_multiple)))
    return flat.reshape(-1, LANE)


def _unpack(packed, shapes):
    flat = packed.reshape(-1)
    out, off = [], 0
    for s in shapes:
        size = 1
        for dim in s:
            size *= dim
        out.append(flat[off:off + size].reshape(s))
        off += size + (-size) % LANE
    return out


def _layer_fwd(x, wts, getw, dims, dep=None, normed=None, next_gain=None, loss_target=None):
    AW, BW, PW, DP = dims["AW"], dims["BW"], dims["PW"], dims["DP"]
    q_start = 2 * AW + 3 * BW
    h, r1 = _rmsnorm_fwd(x, wts["attn_norm"], "rmsnorm_fwd", dep=dep) if normed is None else normed
    p = _mm_nt(h, getw("w_in", h), "proj_in", out_dtype=MXU_DTYPE, tm=1024, tn=1408)
    y_a = _sgu_fwd(p, wts["sgu_tril"], wts["sgu_bmat"], "sgu_fwd")
    y_b = _conv_fwd(p, getw("conv_w", y_a), AW, "conv_fwd")
    qkn = _qk_norm_fwd(p, wts["qk_gain"], q_start, PW, "qk_norm_fwd")
    os_, lses = [], []
    for g, d in enumerate(DILATIONS):
        o, lse = _attn_fwd(qkn, g, d, PW, "attn_fwd_%d" % d)
        os_.append(o)
        lses.append(lse)
    y_c = _mix_fwd(os_, lses, "mix_fwd")
    ymix = [y_a, y_b, y_c]
    x1, h2, r2 = _mm_residual_norm(ymix, getw("w_out", y_c), x, wts["mlp_norm"], "proj_out")
    a, hid = _mm_relu2(h2, getw("w_mlp_in", h2), "mlp_in")
    if loss_target is not None:
        x2 = _mm_residual_loss(hid, getw("w_mlp_out", hid), x1, loss_target, "mlp_out_loss")
    else:
        x2 = _mm_residual_norm([hid], getw("w_mlp_out", hid), x1, next_gain, "mlp_out", tm=256)
    saved = dict(x=x, h=h, r1=r1, p=p, qkn=qkn, os=os_, lses=lses, ymix=ymix, x1=x1, h2=h2, r2=r2, a=a, hid=hid)
    return x2, saved


def _layer_bwd(dx, dxb, wts, getw, scatter, saved, dims):
    AW, BW, PW, DP = dims["AW"], dims["BW"], dims["PW"], dims["DP"]
    q_start = 2 * AW + 3 * BW
    D = dx.shape[1]
    g_w2 = _wgrad_wide_a(saved["hid"], dxb, "mlp_out_wgrad")
    token = scatter("w_mlp_out", g_w2.reshape(N_DEV, -1, D))
    da = _mm_nt(dxb, getw("w_mlp_out", None), "mlp_out_dgrad", out_dtype=MXU_DTYPE, relu2_pre=saved["a"], dep=token,
                tm=1024, tn=1024)
    g_w1 = _wgrad_wide_b([saved["h2"]], da, "mlp_in_wgrad", groups=N_DEV)
    token = scatter("w_mlp_in", g_w1)
    dx1, dx1b, g_mlp_norm = _mm_norm_bwd(da, getw("w_mlp_in", None), saved["x1"], wts["mlp_norm"], saved["r2"], dx,
                                         "mlp_in_dgrad", dep=token)
    g_wout = _wgrad_wide_b(saved["ymix"], dx1b, "proj_out_wgrad")
    token = scatter("w_out", g_wout.reshape(N_DEV, -1, D))
    dymix = _mm_nt(dx1b, getw("w_out", None), "proj_out_dgrad", out_dtype=MXU_DTYPE, dep=token, tm=1024, tn=1024)
    p = saved["p"]
    du, dv, g_sgu_w, g_sgu_bmat = _sgu_bwd(dymix, p, wts["sgu_tril"], wts["sgu_tril_t"], wts["sgu_bmat"], "sgu_bwd")
    d_b, d_c, d_xb, g_conv = _conv_bwd(dymix, p, getw("conv_w", None), AW, "conv_bwd")
    dos, corrs = _mix_bwd(dymix, saved["os"], saved["lses"], AW + BW, "mix_bwd")
    dqns, dkns, dvs = [], [], []
    for g, d in enumerate(DILATIONS):
        dqn, dkn, dvv = _attn_bwd(saved["qkn"], saved["lses"][g], dos[g], corrs[g], g, d, PW, "attn_bwd_%d" % d)
        dqns.append(dqn)
        dkns.append(dkn)
        dvs.append(dvv.astype(MXU_DTYPE))
    dqk, g_qk = _qk_norm_bwd(dqns + dkns, p, wts["qk_gain"], q_start, PW, "qk_norm_bwd")
    g_q, g_k = (part.reshape(-1, HEAD_DIM).sum(0) for part in jnp.split(g_qk[0], 2))
    dp = jnp.concatenate([du, dv, d_b, d_c, d_xb, dqk] + dvs, axis=1)
    g_win_t = _wgrad_wide_a(dp, saved["h"], "proj_in_wgrad")
    token = scatter("w_in", g_win_t.reshape(N_DEV, DP // N_DEV, D))
    dx0, dx0b, g_attn_norm = _mm_norm_bwd(dp, getw("w_in", None), saved["x"], wts["attn_norm"], saved["r1"], dx1,
                                          "proj_in_dgrad", dep=token)
    H = AW // HEAD_DIM
    tril = jnp.tril(jnp.ones((CHUNK, CHUNK), F32))
    small = [g_attn_norm.reshape(-1), g_sgu_w * tril, g_sgu_bmat.reshape(CHUNK, H, HEAD_DIM).sum(-1).T,
             g_conv, g_q, g_k, g_mlp_norm.reshape(-1)]
    return dx0, dx0b, small


def kernel(x, attn_norm, w_in, sgu_w, sgu_b, conv_w, q_norm, k_norm, w_out, mlp_norm, w_mlp_in, w_mlp_out, loss_target, m_attn_norm, m_w_in, m_sgu_w, m_sgu_b, m_conv_w, m_q_norm, m_k_norm, m_w_out, m_mlp_norm, m_w_mlp_in, m_w_mlp_out, v_attn_norm, v_w_in, v_sgu_w, v_sgu_b, v_conv_w, v_q_norm, v_k_norm, v_w_out, v_mlp_norm, v_w_mlp_in, v_w_mlp_out):
    n_layers = attn_norm.shape[0]
    T, D = x.shape[1], x.shape[2]
    H = sgu_w.shape[1]
    AW = H * HEAD_DIM
    BW = conv_w.shape[2] * N_DEV
    DP = w_in.shape[2] * N_DEV
    DMIX = w_out.shape[1] * N_DEV
    DFF = w_mlp_in.shape[2] * N_DEV
    PW = (DMIX - AW - BW) // 3
    HP = PW // HEAD_DIM
    dims = dict(AW=AW, BW=BW, PW=PW, DP=DP)
    me = 4 * lax.axis_index("x") + 2 * lax.axis_index("y") + lax.axis_index("c")

    big_names = ("w_in", "w_out", "w_mlp_in", "w_mlp_out")
    tr_in = lambda a: jnp.swapaxes(a, 1, 2)
    big_w = dict(zip(big_names, (tr_in(w_in), w_out, w_mlp_in, w_mlp_out)))
    big_m = dict(zip(big_names, (tr_in(m_w_in), m_w_out, m_w_mlp_in, m_w_mlp_out)))
    big_v = dict(zip(big_names, (tr_in(v_w_in), v_w_out, v_w_mlp_in, v_w_mlp_out)))

    keys = []
    for l in range(n_layers):
        keys += [(l, nm) for nm in big_names]
    keys.insert(1, (0, "conv_w"))
    first_src = big_w[keys[0][1]][keys[0][0]].astype(MXU_DTYPE)
    first_flights, first_token = _exchange_start([first_src], [_own_in_place(first_src, me)], "gather", name="gather_start_first")
    zero = first_token[0, 0]
    srcs = [_pack([conv_w]) + zero if nm == "conv_w" else (big_w[nm][l] + zero).astype(MXU_DTYPE) for l, nm in keys[1:]]
    flights, gather_token = _exchange_start(srcs, [_own_in_place(s, me) for s in srcs], "gather", name="gather_start")
    arriving = dict(zip(keys, first_flights + flights))
    forwarding = {}
    relayout = dict(
        w_in=lambda g: g.reshape(DP, D), w_out=lambda g: g.reshape(DMIX, D),
        w_mlp_in=lambda g: g, w_mlp_out=lambda g: g.reshape(DFF, D),
        conv_w=lambda g: jnp.stack([_unpack(g[j], [conv_w.shape])[0] for j in range(N_DEV)], axis=2).reshape(
            n_layers, CONV_WIDTH, BW))
    gathered = {}

    def forward(key, after):
        _, land = _exchange_wait(arriving[key], after, "gather", name="gather_arrive_%d_%s" % key)
        fl, token = _exchange_start(None, [land], "forward", name="gather_forward_%d_%s" % key)
        forwarding[key] = fl[0]
        return token

    def weight_getter(l):
        def getw(nm, after):
            key = (0, nm) if nm == "conv_w" else (l, nm)
            if key not in gathered:
                ahead = keys[keys.index(key):][:2]
                for k in ahead:
                    if k not in forwarding:
                        after = forward(k, after)
                _, land = _exchange_wait(forwarding[key], after, "forward", name="gather_wait_%d_%s" % key)
                gathered[key] = relayout[nm](land)
            return gathered[key][l] if nm == "conv_w" else gathered[key]
        return getw

    tril = jnp.tril(jnp.ones((CHUNK, CHUNK), F32))
    layers = []
    for l in range(n_layers):
        w_tril = sgu_w[l] * tril
        layers.append(dict(
            attn_norm=attn_norm[l][None], mlp_norm=mlp_norm[l][None],
            sgu_tril=w_tril.astype(MXU_DTYPE), sgu_tril_t=w_tril.transpose(0, 2, 1).astype(MXU_DTYPE),
            sgu_bmat=jnp.repeat(sgu_b[l].T, HEAD_DIM, axis=1),
            qk_gain=jnp.concatenate([jnp.tile(q_norm[l], 3 * HP), jnp.tile(k_norm[l], 3 * HP)])[None]))

    xs, normed = x[0], None
    saved = []
    for l in range(n_layers):
        last = l == n_layers - 1
        out, sv = _layer_fwd(xs, layers[l], weight_getter(l), dims, dep=gather_token if l == 0 else None, normed=normed,
                             next_gain=None if last else layers[l + 1]["attn_norm"],
                             loss_target=loss_target[0] if last else None)
        saved.append(sv)
        if not last:
            xs, normed = out[0], (out[1], out[2])
    loss_blk, dx, dxb = out
    loss = lax.psum(loss_blk[0, 0], ("x", "y", "c"))

    scattering = {}

    def scatter_starter(l):
        def scatter(nm, partials):
            land = lax.empty((N_PEER,) + partials.shape[1:], partials.dtype)
            fl, tok = _exchange_start([partials], [land], "scatter", name="scatter_start_%d_%s" % (l, nm))
            scattering[(l, nm)] = fl[0]
            return tok
        return scatter

    small = [None] * n_layers
    for l in reversed(range(n_layers)):
        dx, dxb, small[l] = _layer_bwd(dx, dxb, layers[l], weight_getter(l), scatter_starter(l), saved[l], dims)

    small_shapes = [s.shape for s in small[0]]
    small_src = [_pack([s for l in range(n_layers) for s in small[l]])]
    small_flights, small_token = _exchange_start(small_src, [_own_in_place(s, me) for s in small_src], "gather_all",
                                                 name="small_start")
    grad_x = dx[None]

    me1 = me.astype(jnp.int32).reshape(1)
    res = {nm: None for nm in big_names}
    after = small_token
    for l in reversed(range(n_layers)):
        for nm in reversed(big_names):
            own, landed = _exchange_wait(scattering[(l, nm)], after, "scatter", name="scatter_wait_%d_%s" % (l, nm))
            res[nm] = _adamw_layer(l, big_w[nm], big_m[nm], big_v[nm], own, landed, me1, res[nm], "adamw_" + nm)
            after = res[nm][0]
    res["w_in"] = [tr_in(a) for a in res["w_in"]]
    big_out = [res[nm] for nm in big_names]

    _, gathered_small = _exchange_wait(small_flights[0], after, "gather_all", name="small_wait")
    summed = _unpack(_sum_parts(gathered_small, "sum_small"), small_shapes * n_layers)
    ns = len(small_shapes)
    g_small = [jnp.stack([summed[l * ns + i] for l in range(n_layers)]) for i in range(ns)]
    g_attn_norm, g_sgu_w, g_sgu_b, g_conv_full, g_q, g_k, g_mlp_norm = g_small
    cs = conv_w.shape[2]
    g_conv = lax.dynamic_slice_in_dim(g_conv_full, me * cs, cs, axis=2)
    sm_w = (attn_norm, sgu_w, sgu_b, conv_w, q_norm, k_norm, mlp_norm)
    sm_m = (m_attn_norm, m_sgu_w, m_sgu_b, m_conv_w, m_q_norm, m_k_norm, m_mlp_norm)
    sm_v = (v_attn_norm, v_sgu_w, v_sgu_b, v_conv_w, v_q_norm, v_k_norm, v_mlp_norm)
    sm_g = (g_attn_norm, g_sgu_w, g_sgu_b, g_conv, g_q, g_k, g_mlp_norm)
    sm_delta, sm_m2, sm_v2 = _adamw_small(sm_w, sm_g, sm_m, sm_v, "adamw_small")

    def ordered(small_list, big_kind):
        b = [big_out[i][big_kind] for i in range(4)]
        return [small_list[0], b[0], small_list[1], small_list[2], small_list[3], small_list[4], small_list[5],
                b[1], small_list[6], b[2], b[3]]

    return (loss, grad_x, *ordered(list(sm_g), 0), *ordered(sm_delta, 1), *ordered(sm_m2, 2), *ordered(sm_v2, 3))
```

```python
import jax
import jax.numpy as jnp
from jax import lax
from jax.experimental import pallas as pl
from jax.experimental.pallas import tpu as pltpu

N_DEV = 8
HEAD_DIM = 64
CHUNK = 128
ATT_BLK = 128
DILATIONS = (1, 4, 16)
CONV_WIDTH = 3
EPS = 1e-6
ADAM_LR = 0.001
ADAM_B1 = 0.9
ADAM_B2 = 0.999
ADAM_EPS = 1e-08
ADAM_WD = 0.01
ADAM_STEP = 10
MXU_DTYPE = jnp.bfloat16
F32 = jnp.float32
LANE = 128
VMEM_PHYSICAL_BYTES = 64 * 1024 * 1024
VMEM_LIMIT_BYTES = 56 * 1024 * 1024
NEG_INF = float("-inf")


def _cparams(*sem):
    return pltpu.CompilerParams(dimension_semantics=sem, vmem_limit_bytes=VMEM_LIMIT_BYTES)


def _sds(shape, dtype):
    return jax.ShapeDtypeStruct(shape, dtype)


def _fit(n, tile):
    for t in range(min(tile, n) // LANE * LANE, 0, -LANE):
        if n % t == 0:
            return t
    return n


_HBM = pl.BlockSpec(memory_space=pltpu.HBM)
_SEM = pl.BlockSpec(memory_space=pltpu.SEMAPHORE)
_DATAFLOW = pltpu.SideEffectType.DATAFLOW_SIDE_EFFECTING
N_PEER = N_DEV - 1


def _mesh_pos():
    x, y, c = lax.axis_index("x"), lax.axis_index("y"), lax.axis_index("c")
    return x, y, c, 4 * x + 2 * y + c


OTHER_CHIPS = (4, 2, 6)
EXCHANGE_PEERS = dict(
    scatter=tuple(range(1, N_DEV)),
    gather_all=tuple(range(1, N_DEV)),
    gather=(1,) + OTHER_CHIPS,
    forward=OTHER_CHIPS)


def _remote_copies(src, land, send_sems, recv_sems, mode):
    x, y, c, me = _mesh_pos()
    copies = []
    for i, k in enumerate(EXCHANGE_PEERS[mode]):
        px = (1 - x) if (k & 4) else x
        py = (1 - y) if (k & 2) else y
        pc = (1 - c) if (k & 1) else c
        if mode == "scatter":
            src_ref, dst_ref, dev = src.at[4 * px + 2 * py + pc], land.at[i], (px, py, pc)
        elif mode == "forward":
            slot = 4 * px + 2 * py + c
            src_ref, dst_ref, dev = land.at[slot], land.at[slot], (x, y, 1 - c)
        else:
            src_ref, dst_ref, dev = src, land.at[me], (px, py, pc)
        copies.append(pltpu.make_async_remote_copy(
            src_ref=src_ref, dst_ref=dst_ref, send_sem=send_sems.at[i], recv_sem=recv_sems.at[i],
            device_id=dev, device_id_type=pl.DeviceIdType.MESH))
    return copies


def _own_in_place(src, me):
    land = lax.empty((N_DEV,) + src.shape, src.dtype)
    return lax.dynamic_update_slice(land, src[None], (me,) + (0,) * src.ndim)


def _exchange_start(srcs, lands, mode, name):
    n = len(lands)
    has_src = srcs is not None
    arrays = (list(srcs) if has_src else []) + list(lands)
    n_arr = len(arrays)
    n_copies = len(EXCHANGE_PEERS[mode])

    def body(*refs):
        src = refs[:n] if has_src else [None] * n
        land = refs[n_arr - n:n_arr]
        send, recv = refs[n_arr:n_arr + n], refs[n_arr + n:n_arr + 2 * n]
        token = refs[2 * n_arr + 2 * n]
        for t in range(n):
            for cp in _remote_copies(src[t], land[t], send[t], recv[t], mode):
                cp.start()
        token[...] = jnp.zeros_like(token)

    outs = pl.pallas_call(
        body, name=name,
        out_shape=([pltpu.SemaphoreType.DMA((n_copies,))] * (2 * n) + [pltpu.HBM(a.shape, a.dtype) for a in arrays]
                   + [_sds((8, LANE), F32)]),
        in_specs=[_HBM] * n_arr,
        out_specs=[_SEM] * (2 * n) + [_HBM] * n_arr + [pl.BlockSpec(memory_space=pltpu.VMEM)],
        input_output_aliases={i: 2 * n + i for i in range(n_arr)},
        compiler_params=pltpu.CompilerParams(has_side_effects=_DATAFLOW),
    )(*[pltpu.with_memory_space_constraint(a, pltpu.HBM) for a in arrays])
    thru = outs[2 * n:2 * n + n_arr]
    flights = [(outs[t], outs[n + t], thru[t] if has_src else None, thru[n_arr - n + t]) for t in range(n)]
    return flights, outs[2 * n + n_arr]


def _exchange_wait(flight, after, mode, name):
    send, recv, src, land = flight
    arrays = [land] if src is None else [src, land]
    n_arr = len(arrays)

    def body(*refs):
        src_ref = refs[0] if n_arr == 2 else None
        land_ref, send_ref, recv_ref = refs[n_arr - 1], refs[n_arr], refs[n_arr + 1]
        for cp in _remote_copies(src_ref, land_ref, send_ref, recv_ref, mode):
            cp.wait_send()
            cp.wait_recv()

    outs = pl.pallas_call(
        body, name=name, out_shape=[pltpu.HBM(a.shape, a.dtype) for a in arrays],
        in_specs=[_HBM] * n_arr + [_SEM, _SEM, pl.BlockSpec(memory_space=pl.ANY)], out_specs=[_HBM] * n_arr,
        input_output_aliases={i: i for i in range(n_arr)},
        compiler_params=pltpu.CompilerParams(has_side_effects=_DATAFLOW),
    )(*arrays, send, recv, after)
    return (None, outs[0]) if src is None else (outs[0], outs[1])


def _rmsnorm_fwd(x, g, name, dep=None, tr=1024):
    T, D = x.shape
    tr = _fit(T, tr)

    def body(x_ref, g_ref, *rest):
        h_ref, r_ref = rest[-2:]
        xv = x_ref[...]
        r = lax.rsqrt(jnp.mean(xv * xv, axis=-1, keepdims=True) + EPS)
        h_ref[...] = (xv * r * g_ref[...]).astype(h_ref.dtype)
        r_ref[...] = r

    in_specs = [pl.BlockSpec((tr, D), lambda i: (i, 0)), pl.BlockSpec((1, D), lambda i: (0, 0))]
    args = [x, g]
    if dep is not None:
        in_specs.append(pl.BlockSpec(dep.shape, lambda i: (0, 0)))
        args.append(dep)
    return pl.pallas_call(
        body, name=name, grid=(T // tr,),
        in_specs=in_specs,
        out_specs=[pl.BlockSpec((tr, D), lambda i: (i, 0)), pl.BlockSpec((tr, 1), lambda i: (i, 0))],
        out_shape=[_sds((T, D), MXU_DTYPE), _sds((T, 1), F32)],
        compiler_params=_cparams("parallel"),
    )(*args)


def _mm_relu2(a, b, name, tm=1024, tn=1024):
    M, K = a.shape
    N = b.shape[0] * b.shape[2]
    tm, tn = _fit(M, tm), _fit(b.shape[2], tn)
    per = b.shape[2] // tn

    def body(a_ref, b_ref, y_ref, y2_ref):
        acc = jnp.dot(a_ref[...], b_ref[...], preferred_element_type=F32)
        y_ref[...] = acc.astype(y_ref.dtype)
        rl = jnp.maximum(acc, 0.0)
        y2_ref[...] = (rl * rl).astype(y2_ref.dtype)

    out_blk = pl.BlockSpec((tm, tn), lambda i, j: (i, j))
    return pl.pallas_call(
        body, name=name, grid=(M // tm, N // tn),
        in_specs=[pl.BlockSpec((tm, K), lambda i, j: (i, 0)), pl.BlockSpec((None, K, tn), lambda i, j: (j // per, 0, j % per))],
        out_specs=[out_blk] * 2, out_shape=[_sds((M, N), MXU_DTYPE)] * 2,
        compiler_params=_cparams("parallel", "parallel"),
    )(a, b)


def _column_offsets(pieces):
    offs = [0]
    for piece in pieces:
        offs.append(offs[-1] + piece.shape[1])
    return offs


def _mm_residual_norm(a_pieces, b, residual, g, name, tm=512):
    n = len(a_pieces)
    M = a_pieces[0].shape[0]
    K, N = b.shape
    offs = _column_offsets(a_pieces)
    assert offs[-1] == K
    tm = _fit(M, tm)
    f32_row, mxu_row = tm * N * 4, tm * N * jnp.dtype(MXU_DTYPE).itemsize
    need = b.size * b.dtype.itemsize + 2 * (tm * K * a_pieces[0].dtype.itemsize + 2 * f32_row + mxu_row) + 5 * f32_row
    vmem_limit = max(VMEM_LIMIT_BYTES, min(need, VMEM_PHYSICAL_BYTES - 2 * 1024 * 1024))

    def body(*refs):
        a_refs, b_ref, res_ref, g_ref, x_ref, h_ref, r_ref = refs[:n], *refs[n:]
        xv = res_ref[...]
        for i in range(n):
            xv = xv + jnp.dot(a_refs[i][...], b_ref[offs[i]:offs[i + 1], :], preferred_element_type=F32)
        x_ref[...] = xv
        r = lax.rsqrt(jnp.mean(xv * xv, axis=-1, keepdims=True) + EPS)
        h_ref[...] = (xv * r * g_ref[...]).astype(h_ref.dtype)
        r_ref[...] = r

    row = pl.BlockSpec((tm, N), lambda i: (i, 0))
    return pl.pallas_call(
        body, name=name, grid=(M // tm,),
        in_specs=[pl.BlockSpec((tm, piece.shape[1]), lambda i: (i, 0)) for piece in a_pieces]
        + [pl.BlockSpec((K, N), lambda i: (0, 0), pipeline_mode=pl.Buffered(1)), row, pl.BlockSpec((1, N), lambda i: (0, 0))],
        out_specs=[row, row, pl.BlockSpec((tm, 1), lambda i: (i, 0))],
        out_shape=[_sds((M, N), F32), _sds((M, N), MXU_DTYPE), _sds((M, 1), F32)],
        compiler_params=pltpu.CompilerParams(dimension_semantics=("parallel",), vmem_limit_bytes=vmem_limit),
    )(*a_pieces, b, residual, g)


def _mm_norm_bwd(a, b, x, g, r, dres, name, dep=None, tm=256):
    M, K = a.shape
    grouped = b.ndim == 3
    N = b.shape[1]
    tm = _fit(M, tm)
    nt = (((1,), (1,)), ((), ()))
    f32_row, mxu_row = tm * N * 4, tm * N * jnp.dtype(MXU_DTYPE).itemsize
    need = b.size * b.dtype.itemsize + 2 * (tm * K * a.dtype.itemsize + 3 * f32_row + mxu_row) + 5 * f32_row
    vmem_limit = max(VMEM_LIMIT_BYTES, min(need, VMEM_PHYSICAL_BYTES - 2 * 1024 * 1024))

    def body(*refs):
        a_ref, b_ref, x_ref, g_ref, r_ref, dres_ref = refs[:6]
        dx_ref, dxb_ref, dg_ref = refs[-3:]

        @pl.when(pl.program_id(0) == 0)
        def _():
            dg_ref[...] = jnp.zeros_like(dg_ref)

        if grouped:
            kg = b.shape[2]
            dh = lax.dot_general(a_ref[:, 0:kg], b_ref[0], nt, preferred_element_type=F32)
            for i in range(1, b.shape[0]):
                dh += lax.dot_general(a_ref[:, i * kg:(i + 1) * kg], b_ref[i], nt, preferred_element_type=F32)
        else:
            dh = jnp.dot(a_ref[...], b_ref[...], preferred_element_type=F32)
        xv, rv = x_ref[...], r_ref[...]
        gdy = dh * g_ref[...]
        mean_xg = jnp.mean(xv * gdy, axis=-1, keepdims=True)
        dx = dres_ref[...] + rv * gdy - xv * (rv * rv * rv) * mean_xg
        dx_ref[...] = dx
        dxb_ref[...] = dx.astype(dxb_ref.dtype)
        dg_ref[...] += jnp.sum(dh * xv * rv, axis=0, keepdims=True)

    row = pl.BlockSpec((tm, N), lambda i: (i, 0))
    in_specs = [pl.BlockSpec((tm, K), lambda i: (i, 0)),
                pl.BlockSpec(b.shape, lambda i: (0,) * b.ndim, pipeline_mode=pl.Buffered(1)),
                row, pl.BlockSpec((1, N), lambda i: (0, 0)), pl.BlockSpec((tm, 1), lambda i: (i, 0)), row]
    args = [a, b, x, g, r, dres]
    if dep is not None:
        in_specs.append(pl.BlockSpec(dep.shape, lambda i: (0, 0)))
        args.append(dep)
    return pl.pallas_call(
        body, name=name, grid=(M // tm,),
        in_specs=in_specs, out_specs=[row, row, pl.BlockSpec((1, N), lambda i: (0, 0))],
        out_shape=[_sds((M, N), F32), _sds((M, N), MXU_DTYPE), _sds((1, N), F32)],
        compiler_params=pltpu.CompilerParams(dimension_semantics=("arbitrary",), vmem_limit_bytes=vmem_limit),
    )(*args)


def _mm_residual_loss(a, b, residual, target, name, tm=512, tn=1024):
    M, K = a.shape
    N = b.shape[1]
    tm, tn = _fit(M, tm), _fit(N, tn)

    def body(a_ref, b_ref, res_ref, t_ref, loss_ref, dx_ref, dxb_ref):
        @pl.when((pl.program_id(0) == 0) & (pl.program_id(1) == 0))
        def _():
            loss_ref[...] = jnp.zeros_like(loss_ref)

        err = jnp.dot(a_ref[...], b_ref[...], preferred_element_type=F32) + res_ref[...] - t_ref[...]
        sq = jnp.sum(jnp.sum(err * err, axis=-1, keepdims=True), axis=0, keepdims=True)
        loss_ref[...] += (0.5 / N) * sq
        dx = err * (1.0 / N)
        dx_ref[...] = dx
        dxb_ref[...] = dx.astype(dxb_ref.dtype)

    blk = pl.BlockSpec((tm, tn), lambda j, i: (i, j))
    return pl.pallas_call(
        body, name=name, grid=(N // tn, M // tm),
        in_specs=[pl.BlockSpec((tm, K), lambda j, i: (i, 0)),
                  pl.BlockSpec((K, tn), lambda j, i: (0, j), pipeline_mode=pl.Buffered(1)), blk, blk],
        out_specs=[pl.BlockSpec((8, LANE), lambda j, i: (0, 0)), blk, blk],
        out_shape=[_sds((8, LANE), F32), _sds((M, N), F32), _sds((M, N), MXU_DTYPE)],
        compiler_params=_cparams("arbitrary", "arbitrary"),
    )(a, b, residual, target)


def _mm_nt(a, b, name, out_dtype=F32, relu2_pre=None, dep=None, tm=512, tn=512):
    M, K = a.shape
    N = b.shape[0]
    tm, tn = _fit(M, tm), _fit(N, tn)
    b_spec = pl.BlockSpec((tn, K), lambda i, j: (j, 0))

    def body(*refs):
        a_ref, b_ref = refs[0], refs[1]
        p_ref = refs[2] if relu2_pre is not None else None
        out_ref = refs[2 + (relu2_pre is not None) + (dep is not None)]
        acc = lax.dot_general(a_ref[...], b_ref[...], (((1,), (1,)), ((), ())), preferred_element_type=F32)
        if p_ref is not None:
            acc = acc * (2.0 * jnp.maximum(p_ref[...].astype(F32), 0.0))
        out_ref[...] = acc.astype(out_ref.dtype)

    out_blk = pl.BlockSpec((tm, tn), lambda i, j: (i, j))
    in_specs = [pl.BlockSpec((tm, K), lambda i, j: (i, 0)), b_spec]
    args = [a, b]
    if relu2_pre is not None:
        in_specs.append(out_blk)
        args.append(relu2_pre)
    if dep is not None:
        in_specs.append(pl.BlockSpec(dep.shape, lambda i, j: (0, 0)))
        args.append(dep)
    return pl.pallas_call(
        body, name=name, grid=(M // tm, N // tn),
        in_specs=in_specs, out_specs=out_blk, out_shape=_sds((M, N), out_dtype),
        compiler_params=_cparams("parallel", "parallel"),
    )(*args)


def _wgrad_wide_a(a, b, name, tm=512):
    T, M = a.shape
    N = b.shape[1]
    tm = _fit(M, tm)

    def body(a_ref, b_ref, out_ref):
        out_ref[...] = lax.dot_general(a_ref[...], b_ref[...], (((0,), (0,)), ((), ())),
                                       preferred_element_type=F32).astype(out_ref.dtype)

    return pl.pallas_call(
        body, name=name, grid=(M // tm,),
        in_specs=[pl.BlockSpec((T, tm), lambda i: (0, i)),
                  pl.BlockSpec((T, N), lambda i: (0, 0), pipeline_mode=pl.Buffered(1))],
        out_specs=pl.BlockSpec((tm, N), lambda i: (i, 0)), out_shape=_sds((M, N), MXU_DTYPE),
        compiler_params=_cparams("parallel"),
    )(a, b)


def _wgrad_wide_b(a_pieces, b, name, groups=None, tn=512, t_chunk=512):
    n = len(a_pieces)
    T = a_pieces[0].shape[0]
    offs = _column_offsets(a_pieces)
    M = offs[-1]
    N = b.shape[1]
    tn = _fit(N if groups is None else N // groups, tn)
    t_chunk = _fit(T, t_chunk)

    def body(*refs):
        a_refs, b_ref, out_ref, at_ref = refs[:n], *refs[n:]

        @pl.when(pl.program_id(0) == 0)
        def _():
            for i in range(n):
                for c in range(0, T, t_chunk):
                    at_ref[offs[i]:offs[i + 1], c:c + t_chunk] = a_refs[i][c:c + t_chunk, :].T

        out_ref[...] = jnp.dot(at_ref[...], b_ref[...], preferred_element_type=F32).astype(out_ref.dtype)

    if groups is None:
        out_spec = pl.BlockSpec((M, tn), lambda j: (0, j))
        out_shape = _sds((M, N), MXU_DTYPE)
    else:
        per = N // groups // tn
        out_spec = pl.BlockSpec((None, M, tn), lambda j: (j // per, 0, j % per))
        out_shape = _sds((groups, M, N // groups), MXU_DTYPE)
    return pl.pallas_call(
        body, name=name, grid=(N // tn,),
        in_specs=[pl.BlockSpec(piece.shape, lambda j: (0, 0), pipeline_mode=pl.Buffered(1)) for piece in a_pieces]
        + [pl.BlockSpec((T, tn), lambda j: (0, j))],
        out_specs=out_spec, out_shape=out_shape,
        scratch_shapes=[pltpu.VMEM((M, T), MXU_DTYPE)],
        compiler_params=_cparams("arbitrary"),
    )(*a_pieces, b)


SGU_ROWS = 512


def _sgu_mixed(v, w_ref, b_ref, n_heads):
    parts = [jnp.dot(w_ref[h], v[:, h * HEAD_DIM:(h + 1) * HEAD_DIM], preferred_element_type=F32) for h in range(n_heads)]
    return jnp.concatenate(parts, axis=1) + b_ref[...]


def _sgu_fwd(p, w_tril, bmat, name):
    T = p.shape[0]
    H = w_tril.shape[0]
    AW = H * HEAD_DIM
    rows = _fit(T, SGU_ROWS)

    def body(u_ref, v_ref, w_ref, b_ref, y_ref):
        for c in range(0, rows, CHUNK):
            ch = pl.ds(c, CHUNK)
            mixed = _sgu_mixed(v_ref[ch, :].astype(MXU_DTYPE), w_ref, b_ref, H)
            y_ref[ch, :] = (u_ref[ch, :].astype(F32) * mixed).astype(y_ref.dtype)

    const3 = lambda c: (0, 0, 0)
    return pl.pallas_call(
        body, name=name, grid=(T // rows,),
        in_specs=[pl.BlockSpec((rows, AW), lambda c: (c, 0)), pl.BlockSpec((rows, AW), lambda c: (c, 1)),
                  pl.BlockSpec((H, CHUNK, CHUNK), const3), pl.BlockSpec((CHUNK, AW), lambda c: (0, 0))],
        out_specs=pl.BlockSpec((rows, AW), lambda c: (c, 0)),
        out_shape=_sds((T, AW), MXU_DTYPE),
        compiler_params=_cparams("parallel"),
    )(p, p, w_tril, bmat)


def _sgu_bwd(dymix, p, w_tril, w_tril_t, bmat, name):
    T = p.shape[0]
    H = w_tril.shape[0]
    AW = H * HEAD_DIM
    rows = _fit(T, SGU_ROWS)

    def body(dy_ref, u_ref, v_ref, w_ref, wt_ref, b_ref, du_ref, dv_ref, dw_ref, db_ref):
        @pl.when(pl.program_id(0) == 0)
        def _():
            dw_ref[...] = jnp.zeros_like(dw_ref)
            db_ref[...] = jnp.zeros_like(db_ref)

        for c in range(0, rows, CHUNK):
            ch = pl.ds(c, CHUNK)
            v = v_ref[ch, :].astype(MXU_DTYPE)
            dy = dy_ref[ch, :].astype(F32)
            du_ref[ch, :] = (dy * _sgu_mixed(v, w_ref, b_ref, H)).astype(du_ref.dtype)
            dm = dy * u_ref[ch, :].astype(F32)
            db_ref[...] += dm
            dm_c = dm.astype(MXU_DTYPE)
            dv = []
            for h in range(H):
                sl = slice(h * HEAD_DIM, (h + 1) * HEAD_DIM)
                dv.append(jnp.dot(wt_ref[h], dm_c[:, sl], preferred_element_type=F32))
                dw_ref[h] += lax.dot_general(dm_c[:, sl], v[:, sl], (((1,), (1,)), ((), ())), preferred_element_type=F32)
            dv_ref[ch, :] = jnp.concatenate(dv, axis=1).astype(dv_ref.dtype)

    const3 = lambda c: (0, 0, 0)
    blk = pl.BlockSpec((rows, AW), lambda c: (c, 0))
    return pl.pallas_call(
        body, name=name, grid=(T // rows,),
        in_specs=[blk, blk, pl.BlockSpec((rows, AW), lambda c: (c, 1)),
                  pl.BlockSpec((H, CHUNK, CHUNK), const3), pl.BlockSpec((H, CHUNK, CHUNK), const3),
                  pl.BlockSpec((CHUNK, AW), lambda c: (0, 0))],
        out_specs=[blk, blk, pl.BlockSpec((H, CHUNK, CHUNK), const3), pl.BlockSpec((CHUNK, AW), lambda c: (0, 0))],
        out_shape=[_sds((T, AW), MXU_DTYPE), _sds((T, AW), MXU_DTYPE), _sds((H, CHUNK, CHUNK), F32), _sds((CHUNK, AW), F32)],
        compiler_params=_cparams("arbitrary"),
    )(dymix, p, p, w_tril, w_tril_t, bmat)


def _shift_down(z, s, row):
    return jnp.where(row >= s, pltpu.roll(z, s, 0), 0.0)


def _shift_up(z, s, row, T):
    return jnp.where(row < T - s, pltpu.roll(z, T - s, 0), 0.0)


def _conv_fwd(p, w_conv, AW, name):
    T = p.shape[0]
    BW = w_conv.shape[1]
    nb = BW // LANE
    b0 = 2 * AW // LANE

    def body(b_ref, c_ref, x_ref, w_ref, y_ref):
        row = lax.broadcasted_iota(jnp.int32, (T, LANE), 0)
        z = c_ref[...].astype(F32) * x_ref[...].astype(F32)
        w0, w1, w2 = w_ref[0:1, :], w_ref[1:2, :], w_ref[2:3, :]
        conv = w2 * z + w1 * _shift_down(z, 1, row) + w0 * _shift_down(z, 2, row)
        y_ref[...] = (b_ref[...].astype(F32) * conv).astype(y_ref.dtype)

    return pl.pallas_call(
        body, name=name, grid=(nb,),
        in_specs=[pl.BlockSpec((T, LANE), lambda j: (0, b0 + j)), pl.BlockSpec((T, LANE), lambda j: (0, b0 + nb + j)),
                  pl.BlockSpec((T, LANE), lambda j: (0, b0 + 2 * nb + j)), pl.BlockSpec((CONV_WIDTH, LANE), lambda j: (0, j))],
        out_specs=pl.BlockSpec((T, LANE), lambda j: (0, j)),
        out_shape=_sds((T, BW), MXU_DTYPE),
        compiler_params=_cparams("parallel"),
    )(p, p, p, w_conv)


def _conv_bwd(dymix, p, w_conv, AW, name):
    T = p.shape[0]
    BW = w_conv.shape[1]
    nb = BW // LANE
    b0 = 2 * AW // LANE
    y0 = AW // LANE

    def body(dy_ref, b_ref, c_ref, x_ref, w_ref, db_ref, dc_ref, dxb_ref, dw_ref):
        row = lax.broadcasted_iota(jnp.int32, (T, LANE), 0)
        cv, xv, dy = c_ref[...].astype(F32), x_ref[...].astype(F32), dy_ref[...].astype(F32)
        w0, w1, w2 = w_ref[0:1, :], w_ref[1:2, :], w_ref[2:3, :]
        z = cv * xv
        z1 = _shift_down(z, 1, row)
        z2 = _shift_down(z, 2, row)
        conv = w2 * z + w1 * z1 + w0 * z2
        db_ref[...] = (dy * conv).astype(db_ref.dtype)
        dconv = dy * b_ref[...].astype(F32)
        dz = w2 * dconv + w1 * _shift_up(dconv, 1, row, T) + w0 * _shift_up(dconv, 2, row, T)
        dc_ref[...] = (dz * xv).astype(dc_ref.dtype)
        dxb_ref[...] = (dz * cv).astype(dxb_ref.dtype)
        dw_ref[0:1, :] = jnp.sum(dconv * z2, axis=0, keepdims=True)
        dw_ref[1:2, :] = jnp.sum(dconv * z1, axis=0, keepdims=True)
        dw_ref[2:3, :] = jnp.sum(dconv * z, axis=0, keepdims=True)

    col = lambda j: (0, j)
    return pl.pallas_call(
        body, name=name, grid=(nb,),
        in_specs=[pl.BlockSpec((T, LANE), lambda j: (0, y0 + j)),
                  pl.BlockSpec((T, LANE), lambda j: (0, b0 + j)), pl.BlockSpec((T, LANE), lambda j: (0, b0 + nb + j)),
                  pl.BlockSpec((T, LANE), lambda j: (0, b0 + 2 * nb + j)), pl.BlockSpec((CONV_WIDTH, LANE), col)],
        out_specs=[pl.BlockSpec((T, LANE), col)] * 3 + [pl.BlockSpec((CONV_WIDTH, LANE), col)],
        out_shape=[_sds((T, BW), MXU_DTYPE)] * 3 + [_sds((CONV_WIDTH, BW), F32)],
        compiler_params=_cparams("parallel"),
    )(dymix, p, p, p, w_conv)


def _head_sum(x, col_head, n_heads):
    out = jnp.zeros_like(x)
    for h in range(n_heads):
        sel = col_head == h
        out = jnp.where(sel, jnp.sum(jnp.where(sel, x, 0.0), axis=-1, keepdims=True), out)
    return out


def _same_head(width):
    assert width == 2 * HEAD_DIM
    return lax.broadcasted_iota(jnp.int32, (1, width), 1) < HEAD_DIM


def _head_sum2(x, first):
    s0 = jnp.sum(jnp.where(first, x, 0.0), axis=-1, keepdims=True)
    s1 = jnp.sum(jnp.where(first, 0.0, x), axis=-1, keepdims=True)
    return jnp.where(first, s0, s1)


def _head_norm(x, g, first):
    r = lax.rsqrt(_head_sum2(x * x, first) * (1.0 / HEAD_DIM) + EPS)
    return x * r * g, r


def _head_norm_bwd(dy, x, g, r, first):
    gdy = dy * g
    mean_xg = _head_sum2(x * gdy, first) * (1.0 / HEAD_DIM)
    return r * gdy - x * (r * r * r) * mean_xg, dy * x * r


ATT_SPAN_MIN = 1024
ATT_FWD_UNROLL = 4
ATT_BWD_UNROLL = 4
HEADS_PER_LANES = LANE // HEAD_DIM


def _attn_geometry(T, d):
    m = max(1, min(ATT_SPAN_MIN, T) // (ATT_BLK * d))
    return m, ATT_BLK * d * m, ATT_BLK * d, T // (ATT_BLK * d)


def _rows(ref, start, d):
    return ref[pl.ds(start, ATT_BLK, stride=d), :] if d > 1 else ref[pl.ds(start, ATT_BLK), :]


def _set_rows(ref, start, d, value):
    if d > 1:
        ref[pl.ds(start, ATT_BLK, stride=d), :] = value
    else:
        ref[pl.ds(start, ATT_BLK), :] = value


def _for_each_block(d, m, task, unroll):
    for j in range(m):
        if d == 1:
            task(0, j)
        else:
            lax.fori_loop(0, d, lambda r, carry, j=j: (task(r, j), carry)[1], 0, unroll=min(unroll, d))


def _head_slices():
    return [slice(h * HEAD_DIM, (h + 1) * HEAD_DIM) for h in range(HEADS_PER_LANES)]


def _qk_norm_fwd(p, gains, q_start, PW, name, tr=1024):
    T = p.shape[0]
    tr = _fit(T, tr)
    n_norm = gains.shape[1] // PW
    n = n_norm * 3 // 2
    c0 = q_start // PW

    def body(*refs):
        x_refs, g_ref, out_ref = refs[:n], refs[n], refs[n + 1]
        first = _same_head(LANE)
        for i in range(n):
            for c in range(0, PW, LANE):
                lo = i * PW + c
                x = x_refs[i][:, c:c + LANE].astype(F32)
                out_ref[:, lo:lo + LANE] = _head_norm(x, g_ref[:, lo:lo + LANE], first)[0] if i < n_norm else x

    return pl.pallas_call(
        body, name=name, grid=(T // tr,),
        in_specs=[pl.BlockSpec((tr, PW), lambda i, j=j: (i, c0 + j)) for j in range(n)]
        + [pl.BlockSpec((1, n_norm * PW), lambda i: (0, 0))],
        out_specs=pl.BlockSpec((tr, n * PW), lambda i: (i, 0)), out_shape=_sds((T, n * PW), F32),
        compiler_params=_cparams("parallel"),
    )(*([p] * n), gains)


def _qk_norm_bwd(dns, p, gains, q_start, PW, name, tr=1024):
    T = p.shape[0]
    tr = _fit(T, tr)
    n = len(dns)
    c0 = q_start // PW

    def body(*refs):
        d_refs, x_refs, g_ref, out_ref, acc_ref = refs[:n], refs[n:2 * n], refs[2 * n], refs[2 * n + 1], refs[2 * n + 2]

        @pl.when(pl.program_id(0) == 0)
        def _():
            acc_ref[...] = jnp.zeros_like(acc_ref)

        first = _same_head(LANE)
        for i in range(n):
            for c in range(0, PW, LANE):
                lo = i * PW + c
                x, gv = x_refs[i][:, c:c + LANE].astype(F32), g_ref[:, lo:lo + LANE]
                _, r = _head_norm(x, gv, first)
                dx, g_part = _head_norm_bwd(d_refs[i][:, c:c + LANE], x, gv, r, first)
                out_ref[:, lo:lo + LANE] = dx.astype(out_ref.dtype)
                acc_ref[0:1, lo:lo + LANE] += jnp.sum(g_part, axis=0, keepdims=True)

    return pl.pallas_call(
        body, name=name, grid=(T // tr,),
        in_specs=[pl.BlockSpec((tr, PW), lambda i: (i, 0))] * n
        + [pl.BlockSpec((tr, PW), lambda i, j=j: (i, c0 + j)) for j in range(n)]
        + [pl.BlockSpec((1, n * PW), lambda i: (0, 0))],
        out_specs=[pl.BlockSpec((tr, n * PW), lambda i: (i, 0)), pl.BlockSpec((8, n * PW), lambda i: (0, 0))],
        out_shape=[_sds((T, n * PW), MXU_DTYPE), _sds((8, n * PW), F32)],
        compiler_params=_cparams("arbitrary"),
    )(*dns, *([p] * n), gains)


def _attn_fwd(qkv, g, d, PW, name):
    T = qkv.shape[0]
    B, W = ATT_BLK, LANE
    m, span, group, _ = _attn_geometry(T, d)
    c_q = g * PW // W
    c_k, c_v = c_q + 3 * PW // W, c_q + 6 * PW // W
    scale = HEAD_DIM ** -0.5

    def body(q_ref, k_ref, v_ref, kp_ref, vp_ref, o_ref, lse_ref):
        n = pl.program_id(1)
        qi = lax.broadcasted_iota(jnp.int32, (B, 2 * B), 0)
        kj = lax.broadcasted_iota(jnp.int32, (B, 2 * B), 1)
        band = (kj >= qi) & (kj <= qi + B)

        def task(r, j):
            cur = j * group + r
            if j == 0:
                kp, vp = _rows(kp_ref, r, d), _rows(vp_ref, r, d)
            else:
                kp, vp = _rows(k_ref, cur - group, d), _rows(v_ref, cur - group, d)
            mask = band & ((n * m + j > 0) | (kj >= B))
            qn = _rows(q_ref, cur, d).astype(MXU_DTYPE)
            kn = jnp.concatenate([kp, _rows(k_ref, cur, d)], axis=0).astype(MXU_DTYPE)
            vcat = jnp.concatenate([vp, _rows(v_ref, cur, d)], axis=0).astype(MXU_DTYPE)
            o_parts, lse_parts = [], []
            for sl in _head_slices():
                s = lax.dot_general(qn[:, sl], kn[:, sl], (((1,), (1,)), ((), ())), preferred_element_type=F32) * scale
                s = jnp.where(mask, s, NEG_INF)
                mx = jnp.max(s, axis=-1, keepdims=True)
                e = jnp.exp(s - mx)
                den = jnp.sum(e, axis=-1, keepdims=True)
                o_parts.append(jnp.dot(e.astype(MXU_DTYPE), vcat[:, sl], preferred_element_type=F32) / den)
                lse_parts.append(jnp.broadcast_to(mx + jnp.log(den), (B, HEAD_DIM)))
            _set_rows(o_ref, cur, d, jnp.concatenate(o_parts, axis=1))
            _set_rows(lse_ref, cur, d, jnp.concatenate(lse_parts, axis=1))

        _for_each_block(d, m, task, ATT_FWD_UNROLL)

    main = lambda c0: pl.BlockSpec((span, W), lambda hp, n: (n, c0 + hp))
    prev = lambda c0: pl.BlockSpec((group, W), lambda hp, n: (jnp.maximum(n * m - 1, 0), c0 + hp))
    out_blk = pl.BlockSpec((span, W), lambda hp, n: (n, hp))
    return pl.pallas_call(
        body, name=name, grid=(PW // W, T // span),
        in_specs=[main(c_q), main(c_k), main(c_v), prev(c_k), prev(c_v)],
        out_specs=[out_blk, out_blk],
        out_shape=[_sds((T, PW), F32), _sds((T, PW), F32)],
        compiler_params=_cparams("parallel", "parallel"),
    )(qkv, qkv, qkv, qkv, qkv)


def _attn_bwd(qkv, lse, do, corr, g, d, PW, name):
    T = qkv.shape[0]
    B, W = ATT_BLK, LANE
    m, span, group, n_blocks = _attn_geometry(T, d)
    c_q = g * PW // W
    c_k, c_v = c_q + 3 * PW // W, c_q + 6 * PW // W
    scale = HEAD_DIM ** -0.5
    nt = (((1,), (1,)), ((), ()))
    tn = (((0,), (0,)), ((), ()))

    def body(q_ref, k_ref, v_ref, do_ref, l_ref, c_ref, kp_ref, vp_ref, qx_ref, dox_ref, lx_ref, cx_ref,
             dq_ref, dk_ref, dv_ref):
        n = pl.program_id(1)
        i1 = lax.broadcasted_iota(jnp.int32, (B, B), 0)
        j1 = lax.broadcasted_iota(jnp.int32, (B, B), 1)
        i2 = lax.broadcasted_iota(jnp.int32, (2 * B, B), 0)
        j2 = lax.broadcasted_iota(jnp.int32, (2 * B, B), 1)

        def task(r, j):
            cur = j * group + r
            blk = n * m + j
            q_c, k_c, v_c = _rows(q_ref, cur, d), _rows(k_ref, cur, d), _rows(v_ref, cur, d)
            do_c, l_c, c_c = _rows(do_ref, cur, d), _rows(l_ref, cur, d), _rows(c_ref, cur, d)
            if j == 0:
                k_p, v_p = _rows(kp_ref, r, d), _rows(vp_ref, r, d)
            else:
                k_p, v_p = _rows(k_ref, cur - group, d), _rows(v_ref, cur - group, d)
            if j == m - 1:
                nxt = [_rows(ref, r, d) for ref in (qx_ref, dox_ref, lx_ref, cx_ref)]
            else:
                nxt = [_rows(ref, cur + group, d) for ref in (q_ref, do_ref, l_ref, c_ref)]
            q_x, do_x, l_x, c_x = nxt
            kn_c, kn_p, v_c, v_p = (a.astype(MXU_DTYPE) for a in (k_c, k_p, v_c, v_p))
            qn_c = q_c.astype(MXU_DTYPE)
            qn_cat = jnp.concatenate([qn_c, q_x.astype(MXU_DTYPE)], axis=0)
            do_cb = do_c.astype(MXU_DTYPE)
            do_cat = jnp.concatenate([do_cb, do_x.astype(MXU_DTYPE)], axis=0)
            l_cat = jnp.concatenate([l_c, l_x], axis=0)
            c_cat = jnp.concatenate([c_c, c_x], axis=0)
            mask_p = (j1 >= i1) & (blk > 0)
            mask_c = ((i2 < B) & (j2 <= i2)) | ((i2 >= B) & (j2 >= i2 - B) & (blk + 1 < n_blocks))
            dqn, dkn, dv = [], [], []
            for h, sl in enumerate(_head_slices()):
                lane = slice(h * HEAD_DIM, h * HEAD_DIM + 1)
                s_p = lax.dot_general(qn_c[:, sl], kn_p[:, sl], nt, preferred_element_type=F32) * scale
                pr_p = jnp.where(mask_p, jnp.exp(s_p - l_c[:, lane]), 0.0)
                dp_p = lax.dot_general(do_cb[:, sl], v_p[:, sl], nt, preferred_element_type=F32)
                ds_p = (pr_p * (dp_p + c_c[:, lane]) * scale).astype(MXU_DTYPE)
                s_c = lax.dot_general(qn_cat[:, sl], kn_c[:, sl], nt, preferred_element_type=F32) * scale
                pr_c = jnp.where(mask_c, jnp.exp(s_c - l_cat[:, lane]), 0.0)
                dp_c = lax.dot_general(do_cat[:, sl], v_c[:, sl], nt, preferred_element_type=F32)
                ds_c = (pr_c * (dp_c + c_cat[:, lane]) * scale).astype(MXU_DTYPE)
                dqn.append(jnp.dot(ds_p, kn_p[:, sl], preferred_element_type=F32)
                           + jnp.dot(ds_c[:B], kn_c[:, sl], preferred_element_type=F32))
                dkn.append(lax.dot_general(ds_c, qn_cat[:, sl], tn, preferred_element_type=F32))
                dv.append(lax.dot_general(pr_c.astype(MXU_DTYPE), do_cat[:, sl], tn, preferred_element_type=F32))
            _set_rows(dq_ref, cur, d, jnp.concatenate(dqn, axis=1))
            _set_rows(dk_ref, cur, d, jnp.concatenate(dkn, axis=1))
            _set_rows(dv_ref, cur, d, jnp.concatenate(dv, axis=1))

        _for_each_block(d, m, task, ATT_BWD_UNROLL)

    main = lambda c0: pl.BlockSpec((span, W), lambda hp, n: (n, c0 + hp))
    prev = lambda c0: pl.BlockSpec((group, W), lambda hp, n: (jnp.maximum(n * m - 1, 0), c0 + hp))
    nxt = lambda c0: pl.BlockSpec((group, W), lambda hp, n: (jnp.minimum((n + 1) * m, n_blocks - 1), c0 + hp))
    own = pl.BlockSpec((span, W), lambda hp, n: (n, hp))
    return pl.pallas_call(
        body, name=name, grid=(PW // W, T // span),
        in_specs=[main(c_q), main(c_k), main(c_v), main(0), main(0), main(0), prev(c_k), prev(c_v),
                  nxt(c_q), nxt(0), nxt(0), nxt(0)],
        out_specs=[own, own, own],
        out_shape=[_sds((T, PW), F32)] * 3,
        compiler_params=_cparams("parallel", "parallel"),
    )(qkv, qkv, qkv, do, lse, corr, qkv, qkv, qkv, do, lse, corr)


def _softmax3(lses):
    mx = jnp.maximum(jnp.maximum(lses[0], lses[1]), lses[2])
    ex = [jnp.exp(l - mx) for l in lses]
    inv = 1.0 / (ex[0] + ex[1] + ex[2])
    return [e * inv for e in ex]


def _mix_fwd(os_, lses, name, tr=1024):
    T, PW = os_[0].shape
    tr = _fit(T, tr)

    def body(o0, o1, o2, l0, l1, l2, y_ref):
        alpha = _softmax3([l0[...], l1[...], l2[...]])
        for g, o_ref in enumerate((o0, o1, o2)):
            y_ref[:, g * PW:(g + 1) * PW] = (o_ref[...] * alpha[g]).astype(y_ref.dtype)

    blk = pl.BlockSpec((tr, PW), lambda i: (i, 0))
    return pl.pallas_call(
        body, name=name, grid=(T // tr,),
        in_specs=[blk] * 6, out_specs=pl.BlockSpec((tr, 3 * PW), lambda i: (i, 0)),
        out_shape=_sds((T, 3 * PW), MXU_DTYPE),
        compiler_params=_cparams("parallel"),
    )(*os_, *lses)


def _mix_bwd(dymix, os_, lses, c_start, name, tr=1024):
    T, PW = os_[0].shape
    tr = _fit(T, tr)
    HP = PW // HEAD_DIM
    c0 = c_start // PW

    def body(d0, d1, d2, o0, o1, o2, l0, l1, l2, do0, do1, do2, dl0, dl1, dl2):
        col_head = lax.broadcasted_iota(jnp.int32, (tr, PW), 1) // HEAD_DIM
        alpha = _softmax3([l0[...], l1[...], l2[...]])
        dys = [d0[...].astype(F32), d1[...].astype(F32), d2[...].astype(F32)]
        dots = [_head_sum(dy * o_ref[...], col_head, HP) for dy, o_ref in zip(dys, (o0, o1, o2))]
        mean_dot = alpha[0] * dots[0] + alpha[1] * dots[1] + alpha[2] * dots[2]
        for g, (do_ref, dl_ref) in enumerate(((do0, dl0), (do1, dl1), (do2, dl2))):
            do_ref[...] = dys[g] * alpha[g]
            dl_ref[...] = -alpha[g] * mean_dot

    blk = pl.BlockSpec((tr, PW), lambda i: (i, 0))
    dy_specs = [pl.BlockSpec((tr, PW), lambda i, g=g: (i, c0 + g)) for g in range(3)]
    outs = pl.pallas_call(
        body, name=name, grid=(T // tr,),
        in_specs=dy_specs + [blk] * 6, out_specs=[blk] * 6,
        out_shape=[_sds((T, PW), F32)] * 6,
        compiler_params=_cparams("parallel"),
    )(dymix, dymix, dymix, *os_, *lses)
    return outs[:3], outs[3:]


def _adamw_math(w, g, m, v):
    m2 = ADAM_B1 * m + (1.0 - ADAM_B1) * g
    v2 = ADAM_B2 * v + (1.0 - ADAM_B2) * (g * g)
    m_hat = m2 / (1.0 - ADAM_B1 ** ADAM_STEP)
    v_hat = v2 / (1.0 - ADAM_B2 ** ADAM_STEP)
    delta = -ADAM_LR * (m_hat / (jnp.sqrt(v_hat) + ADAM_EPS) + ADAM_WD * w)
    return delta, m2, v2


def _adamw_layer(layer, w, m, v, own, landed, me, prev, name, tr=256):
    _, R, C = w.shape
    tr = next(t for t in range(min(tr, R) // 16 * 16, 0, -16) if R % t == 0)

    def body(me_ref, w_ref, m_ref, v_ref, own_ref, land_ref, *rest):
        g_ref, d_ref, m2_ref, v2_ref = rest[-4:]
        g = own_ref[...].astype(F32)
        for j in range(N_PEER):
            g = g + land_ref[j].astype(F32)
        delta, m2, v2 = _adamw_math(w_ref[...], g, m_ref[...], v_ref[...])
        g_ref[...] = g
        d_ref[...] = delta
        m2_ref[...] = m2
        v2_ref[...] = v2

    lay = pl.BlockSpec((None, tr, C), lambda i, me_ref: (layer, i, 0))
    in_specs = [lay, lay, lay, pl.BlockSpec((None, tr, C), lambda i, me_ref: (me_ref[0], i, 0)),
                pl.BlockSpec((N_PEER, tr, C), lambda i, me_ref: (0, i, 0))]
    args = [me, w, m, v, own, landed]
    aliases = {}
    if prev is not None:
        in_specs += [pl.BlockSpec(memory_space=pl.ANY)] * 4
        args += list(prev)
        aliases = {6 + i: i for i in range(4)}
    return pl.pallas_call(
        body, name=name,
        grid_spec=pltpu.PrefetchScalarGridSpec(num_scalar_prefetch=1, grid=(R // tr,), in_specs=in_specs, out_specs=[lay] * 4),
        out_shape=[_sds(w.shape, F32)] * 4,
        input_output_aliases=aliases,
        compiler_params=_cparams("parallel"),
    )(*args)


def _sum_parts(parts, name):
    _, R, C = parts.shape

    def body(p_ref, out_ref):
        g = p_ref[0]
        for j in range(1, N_DEV):
            g = g + p_ref[j]
        out_ref[...] = g

    return pl.pallas_call(
        body, name=name, grid=(1,),
        in_specs=[pl.BlockSpec((N_DEV, R, C), lambda i: (0, 0, 0))], out_specs=pl.BlockSpec((R, C), lambda i: (0, 0)),
        out_shape=_sds((R, C), F32), compiler_params=_cparams("arbitrary"),
    )(parts)


def _adamw_small(ws, gs, ms, vs, name):
    n = len(ws)

    def body(*refs):
        ins, outs = refs[:4 * n], refs[4 * n:]
        for i in range(n):
            delta, m2, v2 = _adamw_math(ins[i][...], ins[n + i][...], ins[2 * n + i][...], ins[3 * n + i][...])
            outs[i][...] = delta
            outs[n + i][...] = m2
            outs[2 * n + i][...] = v2

    outs = pl.pallas_call(
        body, name=name, out_shape=[_sds(w.shape, F32) for w in ws] * 3,
    )(*ws, *gs, *ms, *vs)
    return outs[:n], outs[n:2 * n], outs[2 * n:]


def _pack(arrays, rows_multiple=8):
    flat = []
    for a in arrays:
        a = a.reshape(-1).astype(F32)
        flat.append(jnp.pad(a, (0, (-a.shape[0]) % LANE)))
    flat = jnp.concatenate(flat)
    flat = jnp.pad(flat, (0, (-flat.shape[0]) % (LANE * rows_multiple)))
    return flat.reshape(-1, LANE)


def _unpack(packed, shapes):
    flat = packed.reshape(-1)
    out, off = [], 0
    for s in shapes:
        size = 1
        for dim in s:
            size *= dim
        out.append(flat[off:off + size].reshape(s))
        off += size + (-size) % LANE
    return out


def _layer_fwd(x, wts, getw, dims, dep=None, normed=None, next_gain=None, loss_target=None):
    AW, BW, PW, DP = dims["AW"], dims["BW"], dims["PW"], dims["DP"]
    q_start = 2 * AW + 3 * BW
    h, r1 = _rmsnorm_fwd(x, wts["attn_norm"], "rmsnorm_fwd", dep=dep) if normed is None else normed
    p = _mm_nt(h, getw("w_in", h), "proj_in", out_dtype=MXU_DTYPE, tm=1024, tn=1408)
    y_a = _sgu_fwd(p, wts["sgu_tril"], wts["sgu_bmat"], "sgu_fwd")
    y_b = _conv_fwd(p, getw("conv_w", y_a), AW, "conv_fwd")
    qkn = _qk_norm_fwd(p, wts["qk_gain"], q_start, PW, "qk_norm_fwd")
    os_, lses = [], []
    for g, d in enumerate(DILATIONS):
        o, lse = _attn_fwd(qkn, g, d, PW, "attn_fwd_%d" % d)
        os_.append(o)
        lses.append(lse)
    y_c = _mix_fwd(os_, lses, "mix_fwd")
    ymix = [y_a, y_b, y_c]
    x1, h2, r2 = _mm_residual_norm(ymix, getw("w_out", y_c), x, wts["mlp_norm"], "proj_out")
    a, hid = _mm_relu2(h2, getw("w_mlp_in", h2), "mlp_in", tm=2048)
    if loss_target is not None:
        x2 = _mm_residual_loss(hid, getw("w_mlp_out", hid), x1, loss_target, "mlp_out_loss")
    else:
        x2 = _mm_residual_norm([hid], getw("w_mlp_out", hid), x1, next_gain, "mlp_out", tm=256)
    saved = dict(x=x, h=h, r1=r1, p=p, qkn=qkn, os=os_, lses=lses, ymix=ymix, x1=x1, h2=h2, r2=r2, a=a, hid=hid)
    return x2, saved


def _layer_bwd(dx, dxb, wts, getw, scatter, saved, dims):
    AW, BW, PW, DP = dims["AW"], dims["BW"], dims["PW"], dims["DP"]
    q_start = 2 * AW + 3 * BW
    D = dx.shape[1]
    g_w2 = _wgrad_wide_a(saved["hid"], dxb, "mlp_out_wgrad")
    token = scatter("w_mlp_out", g_w2.reshape(N_DEV, -1, D))
    da = _mm_nt(dxb, getw("w_mlp_out", None), "mlp_out_dgrad", out_dtype=MXU_DTYPE, relu2_pre=saved["a"], dep=token,
                tm=1024, tn=1024)
    g_w1 = _wgrad_wide_b([saved["h2"]], da, "mlp_in_wgrad", groups=N_DEV)
    token = scatter("w_mlp_in", g_w1)
    dx1, dx1b, g_mlp_norm = _mm_norm_bwd(da, getw("w_mlp_in", None), saved["x1"], wts["mlp_norm"], saved["r2"], dx,
                                         "mlp_in_dgrad", dep=token)
    g_wout = _wgrad_wide_b(saved["ymix"], dx1b, "proj_out_wgrad")
    token = scatter("w_out", g_wout.reshape(N_DEV, -1, D))
    dymix = _mm_nt(dx1b, getw("w_out", None), "proj_out_dgrad", out_dtype=MXU_DTYPE, dep=token, tm=1024, tn=1024)
    p = saved["p"]
    du, dv, g_sgu_w, g_sgu_bmat = _sgu_bwd(dymix, p, wts["sgu_tril"], wts["sgu_tril_t"], wts["sgu_bmat"], "sgu_bwd")
    d_b, d_c, d_xb, g_conv = _conv_bwd(dymix, p, getw("conv_w", None), AW, "conv_bwd")
    dos, corrs = _mix_bwd(dymix, saved["os"], saved["lses"], AW + BW, "mix_bwd")
    dqns, dkns, dvs = [], [], []
    for g, d in enumerate(DILATIONS):
        dqn, dkn, dvv = _attn_bwd(saved["qkn"], saved["lses"][g], dos[g], corrs[g], g, d, PW, "attn_bwd_%d" % d)
        dqns.append(dqn)
        dkns.append(dkn)
        dvs.append(dvv.astype(MXU_DTYPE))
    dqk, g_qk = _qk_norm_bwd(dqns + dkns, p, wts["qk_gain"], q_start, PW, "qk_norm_bwd")
    g_q, g_k = (part.reshape(-1, HEAD_DIM).sum(0) for part in jnp.split(g_qk[0], 2))
    dp = jnp.concatenate([du, dv, d_b, d_c, d_xb, dqk] + dvs, axis=1)
    g_win_t = _wgrad_wide_a(dp, saved["h"], "proj_in_wgrad")
    token = scatter("w_in", g_win_t.reshape(N_DEV, DP // N_DEV, D))
    dx0, dx0b, g_attn_norm = _mm_norm_bwd(dp, getw("w_in", None), saved["x"], wts["attn_norm"], saved["r1"], dx1,
                                          "proj_in_dgrad", dep=token)
    H = AW // HEAD_DIM
    tril = jnp.tril(jnp.ones((CHUNK, CHUNK), F32))
    small = [g_attn_norm.reshape(-1), g_sgu_w * tril, g_sgu_bmat.reshape(CHUNK, H, HEAD_DIM).sum(-1).T,
             g_conv, g_q, g_k, g_mlp_norm.reshape(-1)]
    return dx0, dx0b, small


def kernel(x, attn_norm, w_in, sgu_w, sgu_b, conv_w, q_norm, k_norm, w_out, mlp_norm, w_mlp_in, w_mlp_out, loss_target, m_attn_norm, m_w_in, m_sgu_w, m_sgu_b, m_conv_w, m_q_norm, m_k_norm, m_w_out, m_mlp_norm, m_w_mlp_in, m_w_mlp_out, v_attn_norm, v_w_in, v_sgu_w, v_sgu_b, v_conv_w, v_q_norm, v_k_norm, v_w_out, v_mlp_norm, v_w_mlp_in, v_w_mlp_out):
    n_layers = attn_norm.shape[0]
    T, D = x.shape[1], x.shape[2]
    H = sgu_w.shape[1]
    AW = H * HEAD_DIM
    BW = conv_w.shape[2] * N_DEV
    DP = w_in.shape[2] * N_DEV
    DMIX = w_out.shape[1] * N_DEV
    DFF = w_mlp_in.shape[2] * N_DEV
    PW = (DMIX - AW - BW) // 3
    HP = PW // HEAD_DIM
    dims = dict(AW=AW, BW=BW, PW=PW, DP=DP)
    me = 4 * lax.axis_index("x") + 2 * lax.axis_index("y") + lax.axis_index("c")

    big_names = ("w_in", "w_out", "w_mlp_in", "w_mlp_out")
    tr_in = lambda a: jnp.swapaxes(a, 1, 2)
    big_w = dict(zip(big_names, (tr_in(w_in), w_out, w_mlp_in, w_mlp_out)))
    big_m = dict(zip(big_names, (tr_in(m_w_in), m_w_out, m_w_mlp_in, m_w_mlp_out)))
    big_v = dict(zip(big_names, (tr_in(v_w_in), v_w_out, v_w_mlp_in, v_w_mlp_out)))

    keys = []
    for l in range(n_layers):
        keys += [(l, nm) for nm in big_names]
    keys.insert(1, (0, "conv_w"))
    first_src = big_w[keys[0][1]][keys[0][0]].astype(MXU_DTYPE)
    first_flights, first_token = _exchange_start([first_src], [_own_in_place(first_src, me)], "gather", name="gather_start_first")
    zero = first_token[0, 0]
    srcs = [_pack([conv_w]) + zero if nm == "conv_w" else (big_w[nm][l] + zero).astype(MXU_DTYPE) for l, nm in keys[1:]]
    flights, gather_token = _exchange_start(srcs, [_own_in_place(s, me) for s in srcs], "gather", name="gather_start")
    arriving = dict(zip(keys, first_flights + flights))
    forwarding = {}
    relayout = dict(
        w_in=lambda g: g.reshape(DP, D), w_out=lambda g: g.reshape(DMIX, D),
        w_mlp_in=lambda g: g, w_mlp_out=lambda g: g.reshape(DFF, D),
        conv_w=lambda g: jnp.stack([_unpack(g[j], [conv_w.shape])[0] for j in range(N_DEV)], axis=2).reshape(
            n_layers, CONV_WIDTH, BW))
    gathered = {}

    def forward(key, after):
        _, land = _exchange_wait(arriving[key], after, "gather", name="gather_arrive_%d_%s" % key)
        fl, token = _exchange_start(None, [land], "forward", name="gather_forward_%d_%s" % key)
        forwarding[key] = fl[0]
        return token

    def weight_getter(l):
        def getw(nm, after):
            key = (0, nm) if nm == "conv_w" else (l, nm)
            if key not in gathered:
                ahead = keys[keys.index(key):][:2]
                for k in ahead:
                    if k not in forwarding:
                        after = forward(k, after)
                _, land = _exchange_wait(forwarding[key], after, "forward", name="gather_wait_%d_%s" % key)
                gathered[key] = relayout[nm](land)
            return gathered[key][l] if nm == "conv_w" else gathered[key]
        return getw

    tril = jnp.tril(jnp.ones((CHUNK, CHUNK), F32))
    layers = []
    for l in range(n_layers):
        w_tril = sgu_w[l] * tril
        layers.append(dict(
            attn_norm=attn_norm[l][None], mlp_norm=mlp_norm[l][None],
            sgu_tril=w_tril.astype(MXU_DTYPE), sgu_tril_t=w_tril.transpose(0, 2, 1).astype(MXU_DTYPE),
            sgu_bmat=jnp.repeat(sgu_b[l].T, HEAD_DIM, axis=1),
            qk_gain=jnp.concatenate([jnp.tile(q_norm[l], 3 * HP), jnp.tile(k_norm[l], 3 * HP)])[None]))

    xs, normed = x[0], None
    saved = []
    for l in range(n_layers):
        last = l == n_layers - 1
        out, sv = _layer_fwd(xs, layers[l], weight_getter(l), dims, dep=gather_token if l == 0 else None, normed=normed,
                             next_gain=None if last else layers[l + 1]["attn_norm"],
                             loss_target=loss_target[0] if last else None)
        saved.append(sv)
        if not last:
            xs, normed = out[0], (out[1], out[2])
    loss_blk, dx, dxb = out
    loss = lax.psum(loss_blk[0, 0], ("x", "y", "c"))

    scattering = {}

    def scatter_starter(l):
        def scatter(nm, partials):
            land = lax.empty((N_PEER,) + partials.shape[1:], partials.dtype)
            fl, tok = _exchange_start([partials], [land], "scatter", name="scatter_start_%d_%s" % (l, nm))
            scattering[(l, nm)] = fl[0]
            return tok
        return scatter

    small = [None] * n_layers
    for l in reversed(range(n_layers)):
        dx, dxb, small[l] = _layer_bwd(dx, dxb, layers[l], weight_getter(l), scatter_starter(l), saved[l], dims)

    small_shapes = [s.shape for s in small[0]]
    small_src = [_pack([s for l in range(n_layers) for s in small[l]])]
    small_flights, small_token = _exchange_start(small_src, [_own_in_place(s, me) for s in small_src], "gather_all",
                                                 name="small_start")
    grad_x = dx[None]

    me1 = me.astype(jnp.int32).reshape(1)
    res = {nm: None for nm in big_names}
    after = small_token
    for l in reversed(range(n_layers)):
        for nm in reversed(big_names):
            own, landed = _exchange_wait(scattering[(l, nm)], after, "scatter", name="scatter_wait_%d_%s" % (l, nm))
            res[nm] = _adamw_layer(l, big_w[nm], big_m[nm], big_v[nm], own, landed, me1, res[nm], "adamw_" + nm)
            after = res[nm][0]
    res["w_in"] = [tr_in(a) for a in res["w_in"]]
    big_out = [res[nm] for nm in big_names]

    _, gathered_small = _exchange_wait(small_flights[0], after, "gather_all", name="small_wait")
    summed = _unpack(_sum_parts(gathered_small, "sum_small"), small_shapes * n_layers)
    ns = len(small_shapes)
    g_small = [jnp.stack([summed[l * ns + i] for l in range(n_layers)]) for i in range(ns)]
    g_attn_norm, g_sgu_w, g_sgu_b, g_conv_full, g_q, g_k, g_mlp_norm = g_small
    cs = conv_w.shape[2]
    g_conv = lax.dynamic_slice_in_dim(g_conv_full, me * cs, cs, axis=2)
    sm_w = (attn_norm, sgu_w, sgu_b, conv_w, q_norm, k_norm, mlp_norm)
    sm_m = (m_attn_norm, m_sgu_w, m_sgu_b, m_conv_w, m_q_norm, m_k_norm, m_mlp_norm)
    sm_v = (v_attn_norm, v_sgu_w, v_sgu_b, v_conv_w, v_q_norm, v_k_norm, v_mlp_norm)
    sm_g = (g_attn_norm, g_sgu_w, g_sgu_b, g_conv, g_q, g_k, g_mlp_norm)
    sm_delta, sm_m2, sm_v2 = _adamw_small(sm_w, sm_g, sm_m, sm_v, "adamw_small")

    def ordered(small_list, big_kind):
        b = [big_out[i][big_kind] for i in range(4)]
        return [small_list[0], b[0], small_list[1], small_list[2], small_list[3], small_list[4], small_list[5],
                b[1], small_list[6], b[2], b[3]]

    return (loss, grad_x, *ordered(list(sm_g), 0), *ordered(sm_delta, 1), *ordered(sm_m2, 2), *ordered(sm_v2, 3))
```

```python
import jax
import jax.numpy as jnp
from jax import lax
from jax.experimental import pallas as pl
from jax.experimental.pallas import tpu as pltpu

N_DEV = 8
HEAD_DIM = 64
CHUNK = 128
ATT_BLK = 128
DILATIONS = (1, 4, 16)
CONV_WIDTH = 3
EPS = 1e-6
ADAM_LR = 0.001
ADAM_B1 = 0.9
ADAM_B2 = 0.999
ADAM_EPS = 1e-08
ADAM_WD = 0.01
ADAM_STEP = 10
MXU_DTYPE = jnp.bfloat16
F32 = jnp.float32
LANE = 128
VMEM_PHYSICAL_BYTES = 64 * 1024 * 1024
VMEM_LIMIT_BYTES = 56 * 1024 * 1024
NEG_INF = float("-inf")


def _cparams(*sem):
    return pltpu.CompilerParams(dimension_semantics=sem, vmem_limit_bytes=VMEM_LIMIT_BYTES)


def _sds(shape, dtype):
    return jax.ShapeDtypeStruct(shape, dtype)


def _fit(n, tile):
    for t in range(min(tile, n) // LANE * LANE, 0, -LANE):
        if n % t == 0:
            return t
    return n


_HBM = pl.BlockSpec(memory_space=pltpu.HBM)
_SEM = pl.BlockSpec(memory_space=pltpu.SEMAPHORE)
_DATAFLOW = pltpu.SideEffectType.DATAFLOW_SIDE_EFFECTING
N_PEER = N_DEV - 1


def _mesh_pos():
    x, y, c = lax.axis_index("x"), lax.axis_index("y"), lax.axis_index("c")
    return x, y, c, 4 * x + 2 * y + c


OTHER_CHIPS = (4, 2, 6)
EXCHANGE_PEERS = dict(
    scatter=tuple(range(1, N_DEV)),
    gather_all=tuple(range(1, N_DEV)),
    gather=(1,) + OTHER_CHIPS,
    forward=OTHER_CHIPS)


def _remote_copies(src, land, send_sems, recv_sems, mode):
    x, y, c, me = _mesh_pos()
    copies = []
    for i, k in enumerate(EXCHANGE_PEERS[mode]):
        px = (1 - x) if (k & 4) else x
        py = (1 - y) if (k & 2) else y
        pc = (1 - c) if (k & 1) else c
        if mode == "scatter":
            src_ref, dst_ref, dev = src.at[4 * px + 2 * py + pc], land.at[i], (px, py, pc)
        elif mode == "forward":
            slot = 4 * px + 2 * py + c
            src_ref, dst_ref, dev = land.at[slot], land.at[slot], (x, y, 1 - c)
        else:
            src_ref, dst_ref, dev = src, land.at[me], (px, py, pc)
        copies.append(pltpu.make_async_remote_copy(
            src_ref=src_ref, dst_ref=dst_ref, send_sem=send_sems.at[i], recv_sem=recv_sems.at[i],
            device_id=dev, device_id_type=pl.DeviceIdType.MESH))
    return copies


def _own_in_place(src, me):
    land = lax.empty((N_DEV,) + src.shape, src.dtype)
    return lax.dynamic_update_slice(land, src[None], (me,) + (0,) * src.ndim)


def _exchange_start(srcs, lands, mode, name):
    n = len(lands)
    has_src = srcs is not None
    arrays = (list(srcs) if has_src else []) + list(lands)
    n_arr = len(arrays)
    n_copies = len(EXCHANGE_PEERS[mode])

    def body(*refs):
        src = refs[:n] if has_src else [None] * n
        land = refs[n_arr - n:n_arr]
        send, recv = refs[n_arr:n_arr + n], refs[n_arr + n:n_arr + 2 * n]
        token = refs[2 * n_arr + 2 * n]
        for t in range(n):
            for cp in _remote_copies(src[t], land[t], send[t], recv[t], mode):
                cp.start()
        token[...] = jnp.zeros_like(token)

    outs = pl.pallas_call(
        body, name=name,
        out_shape=([pltpu.SemaphoreType.DMA((n_copies,))] * (2 * n) + [pltpu.HBM(a.shape, a.dtype) for a in arrays]
                   + [_sds((8, LANE), F32)]),
        in_specs=[_HBM] * n_arr,
        out_specs=[_SEM] * (2 * n) + [_HBM] * n_arr + [pl.BlockSpec(memory_space=pltpu.VMEM)],
        input_output_aliases={i: 2 * n + i for i in range(n_arr)},
        compiler_params=pltpu.CompilerParams(has_side_effects=_DATAFLOW),
    )(*[pltpu.with_memory_space_constraint(a, pltpu.HBM) for a in arrays])
    thru = outs[2 * n:2 * n + n_arr]
    flights = [(outs[t], outs[n + t], thru[t] if has_src else None, thru[n_arr - n + t]) for t in range(n)]
    return flights, outs[2 * n + n_arr]


def _exchange_wait(flight, after, mode, name):
    send, recv, src, land = flight
    arrays = [land] if src is None else [src, land]
    n_arr = len(arrays)

    def body(*refs):
        src_ref = refs[0] if n_arr == 2 else None
        land_ref, send_ref, recv_ref = refs[n_arr - 1], refs[n_arr], refs[n_arr + 1]
        for cp in _remote_copies(src_ref, land_ref, send_ref, recv_ref, mode):
            cp.wait_send()
            cp.wait_recv()

    outs = pl.pallas_call(
        body, name=name, out_shape=[pltpu.HBM(a.shape, a.dtype) for a in arrays],
        in_specs=[_HBM] * n_arr + [_SEM, _SEM, pl.BlockSpec(memory_space=pl.ANY)], out_specs=[_HBM] * n_arr,
        input_output_aliases={i: i for i in range(n_arr)},
        compiler_params=pltpu.CompilerParams(has_side_effects=_DATAFLOW),
    )(*arrays, send, recv, after)
    return (None, outs[0]) if src is None else (outs[0], outs[1])


def _rmsnorm_fwd(x, g, name, dep=None, tr=1024):
    T, D = x.shape
    tr = _fit(T, tr)

    def body(x_ref, g_ref, *rest):
        h_ref, r_ref = rest[-2:]
        xv = x_ref[...]
        r = lax.rsqrt(jnp.mean(xv * xv, axis=-1, keepdims=True) + EPS)
        h_ref[...] = (xv * r * g_ref[...]).astype(h_ref.dtype)
        r_ref[...] = r

    in_specs = [pl.BlockSpec((tr, D), lambda i: (i, 0)), pl.BlockSpec((1, D), lambda i: (0, 0))]
    args = [x, g]
    if dep is not None:
        in_specs.append(pl.BlockSpec(dep.shape, lambda i: (0, 0)))
        args.append(dep)
    return pl.pallas_call(
        body, name=name, grid=(T // tr,),
        in_specs=in_specs,
        out_specs=[pl.BlockSpec((tr, D), lambda i: (i, 0)), pl.BlockSpec((tr, 1), lambda i: (i, 0))],
        out_shape=[_sds((T, D), MXU_DTYPE), _sds((T, 1), F32)],
        compiler_params=_cparams("parallel"),
    )(*args)


def _mm_relu2(a, b, name, tm=1024, tn=1024):
    M, K = a.shape
    N = b.shape[0] * b.shape[2]
    tm, tn = _fit(M, tm), _fit(b.shape[2], tn)
    per = b.shape[2] // tn

    def body(a_ref, b_ref, y_ref, y2_ref):
        acc = jnp.dot(a_ref[...], b_ref[...], preferred_element_type=F32)
        y_ref[...] = acc.astype(y_ref.dtype)
        rl = jnp.maximum(acc, 0.0)
        y2_ref[...] = (rl * rl).astype(y2_ref.dtype)

    out_blk = pl.BlockSpec((tm, tn), lambda i, j: (i, j))
    return pl.pallas_call(
        body, name=name, grid=(M // tm, N // tn),
        in_specs=[pl.BlockSpec((tm, K), lambda i, j: (i, 0)), pl.BlockSpec((None, K, tn), lambda i, j: (j // per, 0, j % per))],
        out_specs=[out_blk] * 2, out_shape=[_sds((M, N), MXU_DTYPE)] * 2,
        compiler_params=_cparams("parallel", "parallel"),
    )(a, b)


def _column_offsets(pieces):
    offs = [0]
    for piece in pieces:
        offs.append(offs[-1] + piece.shape[1])
    return offs


def _mm_residual_norm(a_pieces, b, residual, g, name, tm=512):
    n = len(a_pieces)
    M = a_pieces[0].shape[0]
    K, N = b.shape
    offs = _column_offsets(a_pieces)
    assert offs[-1] == K
    tm = _fit(M, tm)
    f32_row, mxu_row = tm * N * 4, tm * N * jnp.dtype(MXU_DTYPE).itemsize
    need = b.size * b.dtype.itemsize + 2 * (tm * K * a_pieces[0].dtype.itemsize + 2 * f32_row + mxu_row) + 5 * f32_row
    vmem_limit = max(VMEM_LIMIT_BYTES, min(need, VMEM_PHYSICAL_BYTES - 2 * 1024 * 1024))

    def body(*refs):
        a_refs, b_ref, res_ref, g_ref, x_ref, h_ref, r_ref = refs[:n], *refs[n:]
        xv = res_ref[...]
        for i in range(n):
            xv = xv + jnp.dot(a_refs[i][...], b_ref[offs[i]:offs[i + 1], :], preferred_element_type=F32)
        x_ref[...] = xv
        r = lax.rsqrt(jnp.mean(xv * xv, axis=-1, keepdims=True) + EPS)
        h_ref[...] = (xv * r * g_ref[...]).astype(h_ref.dtype)
        r_ref[...] = r

    row = pl.BlockSpec((tm, N), lambda i: (i, 0))
    return pl.pallas_call(
        body, name=name, grid=(M // tm,),
        in_specs=[pl.BlockSpec((tm, piece.shape[1]), lambda i: (i, 0)) for piece in a_pieces]
        + [pl.BlockSpec((K, N), lambda i: (0, 0), pipeline_mode=pl.Buffered(1)), row, pl.BlockSpec((1, N), lambda i: (0, 0))],
        out_specs=[row, row, pl.BlockSpec((tm, 1), lambda i: (i, 0))],
        out_shape=[_sds((M, N), F32), _sds((M, N), MXU_DTYPE), _sds((M, 1), F32)],
        compiler_params=pltpu.CompilerParams(dimension_semantics=("parallel",), vmem_limit_bytes=vmem_limit),
    )(*a_pieces, b, residual, g)


def _mm_norm_bwd(a, b, x, g, r, dres, name, dep=None, tm=256):
    M, K = a.shape
    grouped = b.ndim == 3
    N = b.shape[1]
    tm = _fit(M, tm)
    nt = (((1,), (1,)), ((), ()))
    f32_row, mxu_row = tm * N * 4, tm * N * jnp.dtype(MXU_DTYPE).itemsize
    need = b.size * b.dtype.itemsize + 2 * (tm * K * a.dtype.itemsize + 3 * f32_row + mxu_row) + 5 * f32_row
    vmem_limit = max(VMEM_LIMIT_BYTES, min(need, VMEM_PHYSICAL_BYTES - 2 * 1024 * 1024))

    def body(*refs):
        a_ref, b_ref, x_ref, g_ref, r_ref, dres_ref = refs[:6]
        dx_ref, dxb_ref, dg_ref = refs[-3:]

        @pl.when(pl.program_id(0) == 0)
        def _():
            dg_ref[...] = jnp.zeros_like(dg_ref)

        if grouped:
            kg = b.shape[2]
            dh = lax.dot_general(a_ref[:, 0:kg], b_ref[0], nt, preferred_element_type=F32)
            for i in range(1, b.shape[0]):
                dh += lax.dot_general(a_ref[:, i * kg:(i + 1) * kg], b_ref[i], nt, preferred_element_type=F32)
        else:
            dh = jnp.dot(a_ref[...], b_ref[...], preferred_element_type=F32)
        xv, rv = x_ref[...], r_ref[...]
        gdy = dh * g_ref[...]
        mean_xg = jnp.mean(xv * gdy, axis=-1, keepdims=True)
        dx = dres_ref[...] + rv * gdy - xv * (rv * rv * rv) * mean_xg
        dx_ref[...] = dx
        dxb_ref[...] = dx.astype(dxb_ref.dtype)
        dg_ref[...] += jnp.sum(dh * xv * rv, axis=0, keepdims=True)

    row = pl.BlockSpec((tm, N), lambda i: (i, 0))
    in_specs = [pl.BlockSpec((tm, K), lambda i: (i, 0)),
                pl.BlockSpec(b.shape, lambda i: (0,) * b.ndim, pipeline_mode=pl.Buffered(1)),
                row, pl.BlockSpec((1, N), lambda i: (0, 0)), pl.BlockSpec((tm, 1), lambda i: (i, 0)), row]
    args = [a, b, x, g, r, dres]
    if dep is not None:
        in_specs.append(pl.BlockSpec(dep.shape, lambda i: (0, 0)))
        args.append(dep)
    return pl.pallas_call(
        body, name=name, grid=(M // tm,),
        in_specs=in_specs, out_specs=[row, row, pl.BlockSpec((1, N), lambda i: (0, 0))],
        out_shape=[_sds((M, N), F32), _sds((M, N), MXU_DTYPE), _sds((1, N), F32)],
        compiler_params=pltpu.CompilerParams(dimension_semantics=("arbitrary",), vmem_limit_bytes=vmem_limit),
    )(*args)


def _mm_residual_loss(a, b, residual, target, name, tm=512, tn=1024):
    M, K = a.shape
    N = b.shape[1]
    tm, tn = _fit(M, tm), _fit(N, tn)

    def body(a_ref, b_ref, res_ref, t_ref, loss_ref, dx_ref, dxb_ref):
        @pl.when((pl.program_id(0) == 0) & (pl.program_id(1) == 0))
        def _():
            loss_ref[...] = jnp.zeros_like(loss_ref)

        err = jnp.dot(a_ref[...], b_ref[...], preferred_element_type=F32) + res_ref[...] - t_ref[...]
        sq = jnp.sum(jnp.sum(err * err, axis=-1, keepdims=True), axis=0, keepdims=True)
        loss_ref[...] += (0.5 / N) * sq
        dx = err * (1.0 / N)
        dx_ref[...] = dx
        dxb_ref[...] = dx.astype(dxb_ref.dtype)

    blk = pl.BlockSpec((tm, tn), lambda j, i: (i, j))
    return pl.pallas_call(
        body, name=name, grid=(N // tn, M // tm),
        in_specs=[pl.BlockSpec((tm, K), lambda j, i: (i, 0)),
                  pl.BlockSpec((K, tn), lambda j, i: (0, j), pipeline_mode=pl.Buffered(1)), blk, blk],
        out_specs=[pl.BlockSpec((8, LANE), lambda j, i: (0, 0)), blk, blk],
        out_shape=[_sds((8, LANE), F32), _sds((M, N), F32), _sds((M, N), MXU_DTYPE)],
        compiler_params=_cparams("arbitrary", "arbitrary"),
    )(a, b, residual, target)


def _mm_nt(a, b, name, out_dtype=F32, relu2_pre=None, dep=None, tm=512, tn=512):
    M, K = a.shape
    N = b.shape[0]
    tm, tn = _fit(M, tm), _fit(N, tn)
    b_spec = pl.BlockSpec((tn, K), lambda i, j: (j, 0))

    def body(*refs):
        a_ref, b_ref = refs[0], refs[1]
        p_ref = refs[2] if relu2_pre is not None else None
        out_ref = refs[2 + (relu2_pre is not None) + (dep is not None)]
        acc = lax.dot_general(a_ref[...], b_ref[...], (((1,), (1,)), ((), ())), preferred_element_type=F32)
        if p_ref is not None:
            acc = acc * (2.0 * jnp.maximum(p_ref[...].astype(F32), 0.0))
        out_ref[...] = acc.astype(out_ref.dtype)

    out_blk = pl.BlockSpec((tm, tn), lambda i, j: (i, j))
    in_specs = [pl.BlockSpec((tm, K), lambda i, j: (i, 0)), b_spec]
    args = [a, b]
    if relu2_pre is not None:
        in_specs.append(out_blk)
        args.append(relu2_pre)
    if dep is not None:
        in_specs.append(pl.BlockSpec(dep.shape, lambda i, j: (0, 0)))
        args.append(dep)
    return pl.pallas_call(
        body, name=name, grid=(M // tm, N // tn),
        in_specs=in_specs, out_specs=out_blk, out_shape=_sds((M, N), out_dtype),
        compiler_params=_cparams("parallel", "parallel"),
    )(*args)


def _wgrad_wide_a(a, b, name, tm=512):
    T, M = a.shape
    N = b.shape[1]
    tm = _fit(M, tm)

    def body(a_ref, b_ref, out_ref):
        out_ref[...] = lax.dot_general(a_ref[...], b_ref[...], (((0,), (0,)), ((), ())),
                                       preferred_element_type=F32).astype(out_ref.dtype)

    return pl.pallas_call(
        body, name=name, grid=(M // tm,),
        in_specs=[pl.BlockSpec((T, tm), lambda i: (0, i)),
                  pl.BlockSpec((T, N), lambda i: (0, 0), pipeline_mode=pl.Buffered(1))],
        out_specs=pl.BlockSpec((tm, N), lambda i: (i, 0)), out_shape=_sds((M, N), MXU_DTYPE),
        compiler_params=_cparams("parallel"),
    )(a, b)


def _wgrad_wide_b(a_pieces, b, name, groups=None, tn=512, t_chunk=512):
    n = len(a_pieces)
    T = a_pieces[0].shape[0]
    offs = _column_offsets(a_pieces)
    M = offs[-1]
    N = b.shape[1]
    tn = _fit(N if groups is None else N // groups, tn)
    t_chunk = _fit(T, t_chunk)
    chunks = [(i, c // t_chunk) for i in range(n) for c in range(0, T, t_chunk)]

    def body(*refs):
        a_refs, (b_ref, out_ref, at_ref), stages, sems = refs[:n], refs[n:n + 3], refs[n + 3:2 * n + 3], refs[2 * n + 3]

        def copy(k):
            i, c = chunks[k]
            return pltpu.make_async_copy(a_refs[i].at[pl.ds(c * t_chunk, t_chunk), :], stages[i].at[c % 2], sems.at[i, c % 2])

        @pl.when(pl.program_id(0) == 0)
        def _():
            copy(0).start()
            for k, (i, c) in enumerate(chunks):
                if k + 1 < len(chunks):
                    copy(k + 1).start()
                copy(k).wait()
                at_ref[offs[i]:offs[i + 1], c * t_chunk:(c + 1) * t_chunk] = stages[i][c % 2].T

        out_ref[...] = jnp.dot(at_ref[...], b_ref[...], preferred_element_type=F32).astype(out_ref.dtype)

    if groups is None:
        out_spec = pl.BlockSpec((M, tn), lambda j: (0, j))
        out_shape = _sds((M, N), MXU_DTYPE)
    else:
        per = N // groups // tn
        out_spec = pl.BlockSpec((None, M, tn), lambda j: (j // per, 0, j % per))
        out_shape = _sds((groups, M, N // groups), MXU_DTYPE)
    return pl.pallas_call(
        body, name=name, grid=(N // tn,),
        in_specs=[pl.BlockSpec(memory_space=pl.ANY)] * n + [pl.BlockSpec((T, tn), lambda j: (0, j))],
        out_specs=out_spec, out_shape=out_shape,
        scratch_shapes=[pltpu.VMEM((M, T), MXU_DTYPE)]
        + [pltpu.VMEM((2, t_chunk, piece.shape[1]), piece.dtype) for piece in a_pieces]
        + [pltpu.SemaphoreType.DMA((n, 2))],
        compiler_params=_cparams("arbitrary"),
    )(*a_pieces, b)


SGU_ROWS = 512


def _sgu_mixed(v, w_ref, b_ref, n_heads):
    parts = [jnp.dot(w_ref[h], v[:, h * HEAD_DIM:(h + 1) * HEAD_DIM], preferred_element_type=F32) for h in range(n_heads)]
    return jnp.concatenate(parts, axis=1) + b_ref[...]


def _sgu_fwd(p, w_tril, bmat, name):
    T = p.shape[0]
    H = w_tril.shape[0]
    AW = H * HEAD_DIM
    rows = _fit(T, SGU_ROWS)

    def body(u_ref, v_ref, w_ref, b_ref, y_ref):
        for c in range(0, rows, CHUNK):
            ch = pl.ds(c, CHUNK)
            mixed = _sgu_mixed(v_ref[ch, :].astype(MXU_DTYPE), w_ref, b_ref, H)
            y_ref[ch, :] = (u_ref[ch, :].astype(F32) * mixed).astype(y_ref.dtype)

    const3 = lambda c: (0, 0, 0)
    return pl.pallas_call(
        body, name=name, grid=(T // rows,),
        in_specs=[pl.BlockSpec((rows, AW), lambda c: (c, 0)), pl.BlockSpec((rows, AW), lambda c: (c, 1)),
                  pl.BlockSpec((H, CHUNK, CHUNK), const3), pl.BlockSpec((CHUNK, AW), lambda c: (0, 0))],
        out_specs=pl.BlockSpec((rows, AW), lambda c: (c, 0)),
        out_shape=_sds((T, AW), MXU_DTYPE),
        compiler_params=_cparams("parallel"),
    )(p, p, w_tril, bmat)


def _sgu_bwd(dymix, p, w_tril, w_tril_t, bmat, name):
    T = p.shape[0]
    H = w_tril.shape[0]
    AW = H * HEAD_DIM
    rows = _fit(T, SGU_ROWS)

    def body(dy_ref, u_ref, v_ref, w_ref, wt_ref, b_ref, du_ref, dv_ref, dw_ref, db_ref):
        @pl.when(pl.program_id(0) == 0)
        def _():
            dw_ref[...] = jnp.zeros_like(dw_ref)
            db_ref[...] = jnp.zeros_like(db_ref)

        for c in range(0, rows, CHUNK):
            ch = pl.ds(c, CHUNK)
            v = v_ref[ch, :].astype(MXU_DTYPE)
            dy = dy_ref[ch, :].astype(F32)
            du_ref[ch, :] = (dy * _sgu_mixed(v, w_ref, b_ref, H)).astype(du_ref.dtype)
            dm = dy * u_ref[ch, :].astype(F32)
            db_ref[...] += dm
            dm_c = dm.astype(MXU_DTYPE)
            dv = []
            for h in range(H):
                sl = slice(h * HEAD_DIM, (h + 1) * HEAD_DIM)
                dv.append(jnp.dot(wt_ref[h], dm_c[:, sl], preferred_element_type=F32))
                dw_ref[h] += lax.dot_general(dm_c[:, sl], v[:, sl], (((1,), (1,)), ((), ())), preferred_element_type=F32)
            dv_ref[ch, :] = jnp.concatenate(dv, axis=1).astype(dv_ref.dtype)

    const3 = lambda c: (0, 0, 0)
    blk = pl.BlockSpec((rows, AW), lambda c: (c, 0))
    return pl.pallas_call(
        body, name=name, grid=(T // rows,),
        in_specs=[blk, blk, pl.BlockSpec((rows, AW), lambda c: (c, 1)),
                  pl.BlockSpec((H, CHUNK, CHUNK), const3), pl.BlockSpec((H, CHUNK, CHUNK), const3),
                  pl.BlockSpec((CHUNK, AW), lambda c: (0, 0))],
        out_specs=[blk, blk, pl.BlockSpec((H, CHUNK, CHUNK), const3), pl.BlockSpec((CHUNK, AW), lambda c: (0, 0))],
        out_shape=[_sds((T, AW), MXU_DTYPE), _sds((T, AW), MXU_DTYPE), _sds((H, CHUNK, CHUNK), F32), _sds((CHUNK, AW), F32)],
        compiler_params=_cparams("arbitrary"),
    )(dymix, p, p, w_tril, w_tril_t, bmat)


def _shift_down(z, s, row):
    return jnp.where(row >= s, pltpu.roll(z, s, 0), 0.0)


def _shift_up(z, s, row, T):
    return jnp.where(row < T - s, pltpu.roll(z, T - s, 0), 0.0)


def _conv_fwd(p, w_conv, AW, name):
    T = p.shape[0]
    BW = w_conv.shape[1]
    nb = BW // LANE
    b0 = 2 * AW // LANE

    def body(b_ref, c_ref, x_ref, w_ref, y_ref):
        row = lax.broadcasted_iota(jnp.int32, (T, LANE), 0)
        z = c_ref[...].astype(F32) * x_ref[...].astype(F32)
        w0, w1, w2 = w_ref[0:1, :], w_ref[1:2, :], w_ref[2:3, :]
        conv = w2 * z + w1 * _shift_down(z, 1, row) + w0 * _shift_down(z, 2, row)
        y_ref[...] = (b_ref[...].astype(F32) * conv).astype(y_ref.dtype)

    return pl.pallas_call(
        body, name=name, grid=(nb,),
        in_specs=[pl.BlockSpec((T, LANE), lambda j: (0, b0 + j)), pl.BlockSpec((T, LANE), lambda j: (0, b0 + nb + j)),
                  pl.BlockSpec((T, LANE), lambda j: (0, b0 + 2 * nb + j)), pl.BlockSpec((CONV_WIDTH, LANE), lambda j: (0, j))],
        out_specs=pl.BlockSpec((T, LANE), lambda j: (0, j)),
        out_shape=_sds((T, BW), MXU_DTYPE),
        compiler_params=_cparams("parallel"),
    )(p, p, p, w_conv)


def _conv_bwd(dymix, p, w_conv, AW, name):
    T = p.shape[0]
    BW = w_conv.shape[1]
    nb = BW // LANE
    b0 = 2 * AW // LANE
    y0 = AW // LANE

    def body(dy_ref, b_ref, c_ref, x_ref, w_ref, db_ref, dc_ref, dxb_ref, dw_ref):
        row = lax.broadcasted_iota(jnp.int32, (T, LANE), 0)
        cv, xv, dy = c_ref[...].astype(F32), x_ref[...].astype(F32), dy_ref[...].astype(F32)
        w0, w1, w2 = w_ref[0:1, :], w_ref[1:2, :], w_ref[2:3, :]
        z = cv * xv
        z1 = _shift_down(z, 1, row)
        z2 = _shift_down(z, 2, row)
        conv = w2 * z + w1 * z1 + w0 * z2
        db_ref[...] = (dy * conv).astype(db_ref.dtype)
        dconv = dy * b_ref[...].astype(F32)
        dz = w2 * dconv + w1 * _shift_up(dconv, 1, row, T) + w0 * _shift_up(dconv, 2, row, T)
        dc_ref[...] = (dz * xv).astype(dc_ref.dtype)
        dxb_ref[...] = (dz * cv).astype(dxb_ref.dtype)
        dw_ref[0:1, :] = jnp.sum(dconv * z2, axis=0, keepdims=True)
        dw_ref[1:2, :] = jnp.sum(dconv * z1, axis=0, keepdims=True)
        dw_ref[2:3, :] = jnp.sum(dconv * z, axis=0, keepdims=True)

    col = lambda j: (0, j)
    return pl.pallas_call(
        body, name=name, grid=(nb,),
        in_specs=[pl.BlockSpec((T, LANE), lambda j: (0, y0 + j)),
                  pl.BlockSpec((T, LANE), lambda j: (0, b0 + j)), pl.BlockSpec((T, LANE), lambda j: (0, b0 + nb + j)),
                  pl.BlockSpec((T, LANE), lambda j: (0, b0 + 2 * nb + j)), pl.BlockSpec((CONV_WIDTH, LANE), col)],
        out_specs=[pl.BlockSpec((T, LANE), col)] * 3 + [pl.BlockSpec((CONV_WIDTH, LANE), col)],
        out_shape=[_sds((T, BW), MXU_DTYPE)] * 3 + [_sds((CONV_WIDTH, BW), F32)],
        compiler_params=_cparams("parallel"),
    )(dymix, p, p, p, w_conv)


def _head_sum(x, col_head, n_heads):
    out = jnp.zeros_like(x)
    for h in range(n_heads):
        sel = col_head == h
        out = jnp.where(sel, jnp.sum(jnp.where(sel, x, 0.0), axis=-1, keepdims=True), out)
    return out


def _same_head(width):
    assert width == 2 * HEAD_DIM
    return lax.broadcasted_iota(jnp.int32, (1, width), 1) < HEAD_DIM


def _head_sum2(x, first):
    s0 = jnp.sum(jnp.where(first, x, 0.0), axis=-1, keepdims=True)
    s1 = jnp.sum(jnp.where(first, 0.0, x), axis=-1, keepdims=True)
    return jnp.where(first, s0, s1)


def _head_norm(x, g, first):
    r = lax.rsqrt(_head_sum2(x * x, first) * (1.0 / HEAD_DIM) + EPS)
    return x * r * g, r


def _head_norm_bwd(dy, x, g, r, first):
    gdy = dy * g
    mean_xg = _head_sum2(x * gdy, first) * (1.0 / HEAD_DIM)
    return r * gdy - x * (r * r * r) * mean_xg, dy * x * r


ATT_SPAN_MIN = 1024
ATT_FWD_UNROLL = 4
ATT_BWD_UNROLL = 4
HEADS_PER_LANES = LANE // HEAD_DIM


def _attn_geometry(T, d):
    m = max(1, min(ATT_SPAN_MIN, T) // (ATT_BLK * d))
    return m, ATT_BLK * d * m, ATT_BLK * d, T // (ATT_BLK * d)


def _rows(ref, start, d):
    return ref[pl.ds(start, ATT_BLK, stride=d), :] if d > 1 else ref[pl.ds(start, ATT_BLK), :]


def _set_rows(ref, start, d, value):
    if d > 1:
        ref[pl.ds(start, ATT_BLK, stride=d), :] = value
    else:
        ref[pl.ds(start, ATT_BLK), :] = value


def _for_each_block(d, m, task, unroll):
    for j in range(m):
        if d == 1:
            task(0, j)
        else:
            lax.fori_loop(0, d, lambda r, carry, j=j: (task(r, j), carry)[1], 0, unroll=min(unroll, d))


def _head_slices():
    return [slice(h * HEAD_DIM, (h + 1) * HEAD_DIM) for h in range(HEADS_PER_LANES)]


def _qk_norm_fwd(p, gains, q_start, PW, name, tr=1024):
    T = p.shape[0]
    tr = _fit(T, tr)
    n_norm = gains.shape[1] // PW
    n = n_norm * 3 // 2
    c0 = q_start // PW

    def body(*refs):
        x_refs, g_ref, out_ref = refs[:n], refs[n], refs[n + 1]
        first = _same_head(LANE)
        for i in range(n):
            for c in range(0, PW, LANE):
                lo = i * PW + c
                x = x_refs[i][:, c:c + LANE].astype(F32)
                out_ref[:, lo:lo + LANE] = _head_norm(x, g_ref[:, lo:lo + LANE], first)[0] if i < n_norm else x

    return pl.pallas_call(
        body, name=name, grid=(T // tr,),
        in_specs=[pl.BlockSpec((tr, PW), lambda i, j=j: (i, c0 + j)) for j in range(n)]
        + [pl.BlockSpec((1, n_norm * PW), lambda i: (0, 0))],
        out_specs=pl.BlockSpec((tr, n * PW), lambda i: (i, 0)), out_shape=_sds((T, n * PW), F32),
        compiler_params=_cparams("parallel"),
    )(*([p] * n), gains)


def _qk_norm_bwd(dns, p, gains, q_start, PW, name, tr=1024):
    T = p.shape[0]
    tr = _fit(T, tr)
    n = len(dns)
    c0 = q_start // PW

    def body(*refs):
        d_refs, x_refs, g_ref, out_ref, acc_ref = refs[:n], refs[n:2 * n], refs[2 * n], refs[2 * n + 1], refs[2 * n + 2]

        @pl.when(pl.program_id(0) == 0)
        def _():
            acc_ref[...] = jnp.zeros_like(acc_ref)

        first = _same_head(LANE)
        for i in range(n):
            for c in range(0, PW, LANE):
                lo = i * PW + c
                x, gv = x_refs[i][:, c:c + LANE].astype(F32), g_ref[:, lo:lo + LANE]
                _, r = _head_norm(x, gv, first)
                dx, g_part = _head_norm_bwd(d_refs[i][:, c:c + LANE], x, gv, r, first)
                out_ref[:, lo:lo + LANE] = dx.astype(out_ref.dtype)
                acc_ref[0:1, lo:lo + LANE] += jnp.sum(g_part, axis=0, keepdims=True)

    return pl.pallas_call(
        body, name=name, grid=(T // tr,),
        in_specs=[pl.BlockSpec((tr, PW), lambda i: (i, 0))] * n
        + [pl.BlockSpec((tr, PW), lambda i, j=j: (i, c0 + j)) for j in range(n)]
        + [pl.BlockSpec((1, n * PW), lambda i: (0, 0))],
        out_specs=[pl.BlockSpec((tr, n * PW), lambda i: (i, 0)), pl.BlockSpec((8, n * PW), lambda i: (0, 0))],
        out_shape=[_sds((T, n * PW), MXU_DTYPE), _sds((8, n * PW), F32)],
        compiler_params=_cparams("arbitrary"),
    )(*dns, *([p] * n), gains)


def _attn_fwd(qkv, g, d, PW, name):
    T = qkv.shape[0]
    B, W = ATT_BLK, LANE
    m, span, group, _ = _attn_geometry(T, d)
    c_q = g * PW // W
    c_k, c_v = c_q + 3 * PW // W, c_q + 6 * PW // W
    scale = HEAD_DIM ** -0.5

    def body(q_ref, k_ref, v_ref, kp_ref, vp_ref, o_ref, lse_ref):
        n = pl.program_id(1)
        qi = lax.broadcasted_iota(jnp.int32, (B, 2 * B), 0)
        kj = lax.broadcasted_iota(jnp.int32, (B, 2 * B), 1)
        band = (kj >= qi) & (kj <= qi + B)

        def task(r, j):
            cur = j * group + r
            if j == 0:
                kp, vp = _rows(kp_ref, r, d), _rows(vp_ref, r, d)
            else:
                kp, vp = _rows(k_ref, cur - group, d), _rows(v_ref, cur - group, d)
            mask = band & ((n * m + j > 0) | (kj >= B))
            qn = _rows(q_ref, cur, d).astype(MXU_DTYPE)
            kn = jnp.concatenate([kp, _rows(k_ref, cur, d)], axis=0).astype(MXU_DTYPE)
            vcat = jnp.concatenate([vp, _rows(v_ref, cur, d)], axis=0).astype(MXU_DTYPE)
            o_parts, lse_parts = [], []
            for sl in _head_slices():
                s = lax.dot_general(qn[:, sl], kn[:, sl], (((1,), (1,)), ((), ())), preferred_element_type=F32) * scale
                s = jnp.where(mask, s, NEG_INF)
                mx = jnp.max(s, axis=-1, keepdims=True)
                e = jnp.exp(s - mx)
                den = jnp.sum(e, axis=-1, keepdims=True)
                o_parts.append(jnp.dot(e.astype(MXU_DTYPE), vcat[:, sl], preferred_element_type=F32) / den)
                lse_parts.append(jnp.broadcast_to(mx + jnp.log(den), (B, HEAD_DIM)))
            _set_rows(o_ref, cur, d, jnp.concatenate(o_parts, axis=1))
            _set_rows(lse_ref, cur, d, jnp.concatenate(lse_parts, axis=1))

        _for_each_block(d, m, task, ATT_FWD_UNROLL)

    main = lambda c0: pl.BlockSpec((span, W), lambda hp, n: (n, c0 + hp))
    prev = lambda c0: pl.BlockSpec((group, W), lambda hp, n: (jnp.maximum(n * m - 1, 0), c0 + hp))
    out_blk = pl.BlockSpec((span, W), lambda hp, n: (n, hp))
    return pl.pallas_call(
        body, name=name, grid=(PW // W, T // span),
        in_specs=[main(c_q), main(c_k), main(c_v), prev(c_k), prev(c_v)],
        out_specs=[out_blk, out_blk],
        out_shape=[_sds((T, PW), F32), _sds((T, PW), F32)],
        compiler_params=_cparams("parallel", "parallel"),
    )(qkv, qkv, qkv, qkv, qkv)


def _attn_bwd(qkv, lse, do, corr, g, d, PW, name):
    T = qkv.shape[0]
    B, W = ATT_BLK, LANE
    m, span, group, n_blocks = _attn_geometry(T, d)
    c_q = g * PW // W
    c_k, c_v = c_q + 3 * PW // W, c_q + 6 * PW // W
    scale = HEAD_DIM ** -0.5
    nt = (((1,), (1,)), ((), ()))
    tn = (((0,), (0,)), ((), ()))

    def body(q_ref, k_ref, v_ref, do_ref, l_ref, c_ref, kp_ref, vp_ref, qx_ref, dox_ref, lx_ref, cx_ref,
             dq_ref, dk_ref, dv_ref):
        n = pl.program_id(1)
        i1 = lax.broadcasted_iota(jnp.int32, (B, B), 0)
        j1 = lax.broadcasted_iota(jnp.int32, (B, B), 1)
        i2 = lax.broadcasted_iota(jnp.int32, (2 * B, B), 0)
        j2 = lax.broadcasted_iota(jnp.int32, (2 * B, B), 1)

        def task(r, j):
            cur = j * group + r
            blk = n * m + j
            q_c, k_c, v_c = _rows(q_ref, cur, d), _rows(k_ref, cur, d), _rows(v_ref, cur, d)
            do_c, l_c, c_c = _rows(do_ref, cur, d), _rows(l_ref, cur, d), _rows(c_ref, cur, d)
            if j == 0:
                k_p, v_p = _rows(kp_ref, r, d), _rows(vp_ref, r, d)
            else:
                k_p, v_p = _rows(k_ref, cur - group, d), _rows(v_ref, cur - group, d)
            if j == m - 1:
                nxt = [_rows(ref, r, d) for ref in (qx_ref, dox_ref, lx_ref, cx_ref)]
            else:
                nxt = [_rows(ref, cur + group, d) for ref in (q_ref, do_ref, l_ref, c_ref)]
            q_x, do_x, l_x, c_x = nxt
            kn_c, kn_p, v_c, v_p = (a.astype(MXU_DTYPE) for a in (k_c, k_p, v_c, v_p))
            qn_c = q_c.astype(MXU_DTYPE)
            qn_cat = jnp.concatenate([qn_c, q_x.astype(MXU_DTYPE)], axis=0)
            do_cb = do_c.astype(MXU_DTYPE)
            do_cat = jnp.concatenate([do_cb, do_x.astype(MXU_DTYPE)], axis=0)
            l_cat = jnp.concatenate([l_c, l_x], axis=0)
            c_cat = jnp.concatenate([c_c, c_x], axis=0)
            mask_p = (j1 >= i1) & (blk > 0)
            mask_c = ((i2 < B) & (j2 <= i2)) | ((i2 >= B) & (j2 >= i2 - B) & (blk + 1 < n_blocks))
            dqn, dkn, dv = [], [], []
            for h, sl in enumerate(_head_slices()):
                lane = slice(h * HEAD_DIM, h * HEAD_DIM + 1)
                s_p = lax.dot_general(qn_c[:, sl], kn_p[:, sl], nt, preferred_element_type=F32) * scale
                pr_p = jnp.where(mask_p, jnp.exp(s_p - l_c[:, lane]), 0.0)
                dp_p = lax.dot_general(do_cb[:, sl], v_p[:, sl], nt, preferred_element_type=F32)
                ds_p = (pr_p * (dp_p + c_c[:, lane]) * scale).astype(MXU_DTYPE)
                s_c = lax.dot_general(qn_cat[:, sl], kn_c[:, sl], nt, preferred_element_type=F32) * scale
                pr_c = jnp.where(mask_c, jnp.exp(s_c - l_cat[:, lane]), 0.0)
                dp_c = lax.dot_general(do_cat[:, sl], v_c[:, sl], nt, preferred_element_type=F32)
                ds_c = (pr_c * (dp_c + c_cat[:, lane]) * scale).astype(MXU_DTYPE)
                dqn.append(jnp.dot(ds_p, kn_p[:, sl], preferred_element_type=F32)
                           + jnp.dot(ds_c[:B], kn_c[:, sl], preferred_element_type=F32))
                dkn.append(lax.dot_general(ds_c, qn_cat[:, sl], tn, preferred_element_type=F32))
                dv.append(lax.dot_general(pr_c.astype(MXU_DTYPE), do_cat[:, sl], tn, preferred_element_type=F32))
            _set_rows(dq_ref, cur, d, jnp.concatenate(dqn, axis=1))
            _set_rows(dk_ref, cur, d, jnp.concatenate(dkn, axis=1))
            _set_rows(dv_ref, cur, d, jnp.concatenate(dv, axis=1))

        _for_each_block(d, m, task, ATT_BWD_UNROLL)

    main = lambda c0: pl.BlockSpec((span, W), lambda hp, n: (n, c0 + hp))
    prev = lambda c0: pl.BlockSpec((group, W), lambda hp, n: (jnp.maximum(n * m - 1, 0), c0 + hp))
    nxt = lambda c0: pl.BlockSpec((group, W), lambda hp, n: (jnp.minimum((n + 1) * m, n_blocks - 1), c0 + hp))
    own = pl.BlockSpec((span, W), lambda hp, n: (n, hp))
    return pl.pallas_call(
        body, name=name, grid=(PW // W, T // span),
        in_specs=[main(c_q), main(c_k), main(c_v), main(0), main(0), main(0), prev(c_k), prev(c_v),
                  nxt(c_q), nxt(0), nxt(0), nxt(0)],
        out_specs=[own, own, own],
        out_shape=[_sds((T, PW), F32)] * 3,
        compiler_params=_cparams("parallel", "parallel"),
    )(qkv, qkv, qkv, do, lse, corr, qkv, qkv, qkv, do, lse, corr)


def _softmax3(lses):
    mx = jnp.maximum(jnp.maximum(lses[0], lses[1]), lses[2])
    ex = [jnp.exp(l - mx) for l in lses]
    inv = 1.0 / (ex[0] + ex[1] + ex[2])
    return [e * inv for e in ex]


def _mix_fwd(os_, lses, name, tr=1024):
    T, PW = os_[0].shape
    tr = _fit(T, tr)

    def body(o0, o1, o2, l0, l1, l2, y_ref):
        alpha = _softmax3([l0[...], l1[...], l2[...]])
        for g, o_ref in enumerate((o0, o1, o2)):
            y_ref[:, g * PW:(g + 1) * PW] = (o_ref[...] * alpha[g]).astype(y_ref.dtype)

    blk = pl.BlockSpec((tr, PW), lambda i: (i, 0))
    return pl.pallas_call(
        body, name=name, grid=(T // tr,),
        in_specs=[blk] * 6, out_specs=pl.BlockSpec((tr, 3 * PW), lambda i: (i, 0)),
        out_shape=_sds((T, 3 * PW), MXU_DTYPE),
        compiler_params=_cparams("parallel"),
    )(*os_, *lses)


def _mix_bwd(dymix, os_, lses, c_start, name, tr=1024):
    T, PW = os_[0].shape
    tr = _fit(T, tr)
    HP = PW // HEAD_DIM
    c0 = c_start // PW

    def body(d0, d1, d2, o0, o1, o2, l0, l1, l2, do0, do1, do2, dl0, dl1, dl2):
        col_head = lax.broadcasted_iota(jnp.int32, (tr, PW), 1) // HEAD_DIM
        alpha = _softmax3([l0[...], l1[...], l2[...]])
        dys = [d0[...].astype(F32), d1[...].astype(F32), d2[...].astype(F32)]
        dots = [_head_sum(dy * o_ref[...], col_head, HP) for dy, o_ref in zip(dys, (o0, o1, o2))]
        mean_dot = alpha[0] * dots[0] + alpha[1] * dots[1] + alpha[2] * dots[2]
        for g, (do_ref, dl_ref) in enumerate(((do0, dl0), (do1, dl1), (do2, dl2))):
            do_ref[...] = dys[g] * alpha[g]
            dl_ref[...] = -alpha[g] * mean_dot

    blk = pl.BlockSpec((tr, PW), lambda i: (i, 0))
    dy_specs = [pl.BlockSpec((tr, PW), lambda i, g=g: (i, c0 + g)) for g in range(3)]
    outs = pl.pallas_call(
        body, name=name, grid=(T // tr,),
        in_specs=dy_specs + [blk] * 6, out_specs=[blk] * 6,
        out_shape=[_sds((T, PW), F32)] * 6,
        compiler_params=_cparams("parallel"),
    )(dymix, dymix, dymix, *os_, *lses)
    return outs[:3], outs[3:]


def _adamw_math(w, g, m, v):
    m2 = ADAM_B1 * m + (1.0 - ADAM_B1) * g
    v2 = ADAM_B2 * v + (1.0 - ADAM_B2) * (g * g)
    m_hat = m2 / (1.0 - ADAM_B1 ** ADAM_STEP)
    v_hat = v2 / (1.0 - ADAM_B2 ** ADAM_STEP)
    delta = -ADAM_LR * (m_hat / (jnp.sqrt(v_hat) + ADAM_EPS) + ADAM_WD * w)
    return delta, m2, v2


def _adamw_layer(layer, w, m, v, own, landed, me, prev, name, tr=256):
    _, R, C = w.shape
    tr = next(t for t in range(min(tr, R) // 16 * 16, 0, -16) if R % t == 0)

    def body(me_ref, w_ref, m_ref, v_ref, own_ref, land_ref, *rest):
        g_ref, d_ref, m2_ref, v2_ref = rest[-4:]
        g = own_ref[...].astype(F32)
        for j in range(N_PEER):
            g = g + land_ref[j].astype(F32)
        delta, m2, v2 = _adamw_math(w_ref[...], g, m_ref[...], v_ref[...])
        g_ref[...] = g
        d_ref[...] = delta
        m2_ref[...] = m2
        v2_ref[...] = v2

    lay = pl.BlockSpec((None, tr, C), lambda i, me_ref: (layer, i, 0))
    in_specs = [lay, lay, lay, pl.BlockSpec((None, tr, C), lambda i, me_ref: (me_ref[0], i, 0)),
                pl.BlockSpec((N_PEER, tr, C), lambda i, me_ref: (0, i, 0))]
    args = [me, w, m, v, own, landed]
    aliases = {}
    if prev is not None:
        in_specs += [pl.BlockSpec(memory_space=pl.ANY)] * 4
        args += list(prev)
        aliases = {6 + i: i for i in range(4)}
    return pl.pallas_call(
        body, name=name,
        grid_spec=pltpu.PrefetchScalarGridSpec(num_scalar_prefetch=1, grid=(R // tr,), in_specs=in_specs, out_specs=[lay] * 4),
        out_shape=[_sds(w.shape, F32)] * 4,
        input_output_aliases=aliases,
        compiler_params=_cparams("parallel"),
    )(*args)


def _sum_parts(parts, name):
    _, R, C = parts.shape

    def body(p_ref, out_ref):
        g = p_ref[0]
        for j in range(1, N_DEV):
            g = g + p_ref[j]
        out_ref[...] = g

    return pl.pallas_call(
        body, name=name, grid=(1,),
        in_specs=[pl.BlockSpec((N_DEV, R, C), lambda i: (0, 0, 0))], out_specs=pl.BlockSpec((R, C), lambda i: (0, 0)),
        out_shape=_sds((R, C), F32), compiler_params=_cparams("arbitrary"),
    )(parts)


def _adamw_small(ws, gs, ms, vs, name):
    n = len(ws)

    def body(*refs):
        ins, outs = refs[:4 * n], refs[4 * n:]
        for i in range(n):
            delta, m2, v2 = _adamw_math(ins[i][...], ins[n + i][...], ins[2 * n + i][...], ins[3 * n + i][...])
            outs[i][...] = delta
            outs[n + i][...] = m2
            outs[2 * n + i][...] = v2

    outs = pl.pallas_call(
        body, name=name, out_shape=[_sds(w.shape, F32) for w in ws] * 3,
    )(*ws, *gs, *ms, *vs)
    return outs[:n], outs[n:2 * n], outs[2 * n:]


def _pack(arrays, rows_multiple=8):
    flat = []
    for a in arrays:
        a = a.reshape(-1).astype(F32)
        flat.append(jnp.pad(a, (0, (-a.shape[0]) % LANE)))
    flat = jnp.concatenate(flat)
    flat = jnp.pad(flat, (0, (-flat.shape[0]) % (LANE * rows_multiple)))
    return flat.reshape(-1, LANE)


def _unpack(packed, shapes):
    flat = packed.reshape(-1)
    out, off = [], 0
    for s in shapes:
        size = 1
        for dim in s:
            size *= dim
        out.append(flat[off:off + size].reshape(s))
        off += size + (-size) % LANE
    return out


def _layer_fwd(x, wts, getw, dims, dep=None, normed=None, next_gain=None, loss_target=None):
    AW, BW, PW, DP = dims["AW"], dims["BW"], dims["PW"], dims["DP"]
    q_start = 2 * AW + 3 * BW
    h, r1 = _rmsnorm_fwd(x, wts["attn_norm"], "rmsnorm_fwd", dep=dep) if normed is None else normed
    p = _mm_nt(h, getw("w_in", h), "proj_in", out_dtype=MXU_DTYPE, tm=1024, tn=1408)
    y_a = _sgu_fwd(p, wts["sgu_tril"], wts["sgu_bmat"], "sgu_fwd")
    y_b = _conv_fwd(p, getw("conv_w", y_a), AW, "conv_fwd")
    qkn = _qk_norm_fwd(p, wts["qk_gain"], q_start, PW, "qk_norm_fwd")
    os_, lses = [], []
    for g, d in enumerate(DILATIONS):
        o, lse = _attn_fwd(qkn, g, d, PW, "attn_fwd_%d" % d)
        os_.append(o)
        lses.append(lse)
    y_c = _mix_fwd(os_, lses, "mix_fwd")
    ymix = [y_a, y_b, y_c]
    x1, h2, r2 = _mm_residual_norm(ymix, getw("w_out", y_c), x, wts["mlp_norm"], "proj_out")
    a, hid = _mm_relu2(h2, getw("w_mlp_in", h2), "mlp_in", tm=2048)
    if loss_target is not None:
        x2 = _mm_residual_loss(hid, getw("w_mlp_out", hid), x1, loss_target, "mlp_out_loss")
    else:
        x2 = _mm_residual_norm([hid], getw("w_mlp_out", hid), x1, next_gain, "mlp_out", tm=256)
    saved = dict(x=x, h=h, r1=r1, p=p, qkn=qkn, os=os_, lses=lses, ymix=ymix, x1=x1, h2=h2, r2=r2, a=a, hid=hid)
    return x2, saved


def _layer_bwd(dx, dxb, wts, getw, scatter, saved, dims):
    AW, BW, PW, DP = dims["AW"], dims["BW"], dims["PW"], dims["DP"]
    q_start = 2 * AW + 3 * BW
    D = dx.shape[1]
    g_w2 = _wgrad_wide_a(saved["hid"], dxb, "mlp_out_wgrad")
    token = scatter("w_mlp_out", g_w2.reshape(N_DEV, -1, D))
    da = _mm_nt(dxb, getw("w_mlp_out", None), "mlp_out_dgrad", out_dtype=MXU_DTYPE, relu2_pre=saved["a"], dep=token,
                tm=1024, tn=1024)
    g_w1 = _wgrad_wide_b([saved["h2"]], da, "mlp_in_wgrad", groups=N_DEV)
    token = scatter("w_mlp_in", g_w1)
    dx1, dx1b, g_mlp_norm = _mm_norm_bwd(da, getw("w_mlp_in", None), saved["x1"], wts["mlp_norm"], saved["r2"], dx,
                                         "mlp_in_dgrad", dep=token)
    g_wout = _wgrad_wide_b(saved["ymix"], dx1b, "proj_out_wgrad")
    token = scatter("w_out", g_wout.reshape(N_DEV, -1, D))
    dymix = _mm_nt(dx1b, getw("w_out", None), "proj_out_dgrad", out_dtype=MXU_DTYPE, dep=token, tm=1024, tn=1024)
    p = saved["p"]
    du, dv, g_sgu_w, g_sgu_bmat = _sgu_bwd(dymix, p, wts["sgu_tril"], wts["sgu_tril_t"], wts["sgu_bmat"], "sgu_bwd")
    d_b, d_c, d_xb, g_conv = _conv_bwd(dymix, p, getw("conv_w", None), AW, "conv_bwd")
    dos, corrs = _mix_bwd(dymix, saved["os"], saved["lses"], AW + BW, "mix_bwd")
    dqns, dkns, dvs = [], [], []
    for g, d in enumerate(DILATIONS):
        dqn, dkn, dvv = _attn_bwd(saved["qkn"], saved["lses"][g], dos[g], corrs[g], g, d, PW, "attn_bwd_%d" % d)
        dqns.append(dqn)
        dkns.append(dkn)
        dvs.append(dvv.astype(MXU_DTYPE))
    dqk, g_qk = _qk_norm_bwd(dqns + dkns, p, wts["qk_gain"], q_start, PW, "qk_norm_bwd")
    g_q, g_k = (part.reshape(-1, HEAD_DIM).sum(0) for part in jnp.split(g_qk[0], 2))
    dp = jnp.concatenate([du, dv, d_b, d_c, d_xb, dqk] + dvs, axis=1)
    g_win_t = _wgrad_wide_a(dp, saved["h"], "proj_in_wgrad")
    token = scatter("w_in", g_win_t.reshape(N_DEV, DP // N_DEV, D))
    dx0, dx0b, g_attn_norm = _mm_norm_bwd(dp, getw("w_in", None), saved["x"], wts["attn_norm"], saved["r1"], dx1,
                                          "proj_in_dgrad", dep=token)
    H = AW // HEAD_DIM
    tril = jnp.tril(jnp.ones((CHUNK, CHUNK), F32))
    small = [g_attn_norm.reshape(-1), g_sgu_w * tril, g_sgu_bmat.reshape(CHUNK, H, HEAD_DIM).sum(-1).T,
             g_conv, g_q, g_k, g_mlp_norm.reshape(-1)]
    return dx0, dx0b, small


def kernel(x, attn_norm, w_in, sgu_w, sgu_b, conv_w, q_norm, k_norm, w_out, mlp_norm, w_mlp_in, w_mlp_out, loss_target, m_attn_norm, m_w_in, m_sgu_w, m_sgu_b, m_conv_w, m_q_norm, m_k_norm, m_w_out, m_mlp_norm, m_w_mlp_in, m_w_mlp_out, v_attn_norm, v_w_in, v_sgu_w, v_sgu_b, v_conv_w, v_q_norm, v_k_norm, v_w_out, v_mlp_norm, v_w_mlp_in, v_w_mlp_out):
    n_layers = attn_norm.shape[0]
    T, D = x.shape[1], x.shape[2]
    H = sgu_w.shape[1]
    AW = H * HEAD_DIM
    BW = conv_w.shape[2] * N_DEV
    DP = w_in.shape[2] * N_DEV
    DMIX = w_out.shape[1] * N_DEV
    DFF = w_mlp_in.shape[2] * N_DEV
    PW = (DMIX - AW - BW) // 3
    HP = PW // HEAD_DIM
    dims = dict(AW=AW, BW=BW, PW=PW, DP=DP)
    me = 4 * lax.axis_index("x") + 2 * lax.axis_index("y") + lax.axis_index("c")

    big_names = ("w_in", "w_out", "w_mlp_in", "w_mlp_out")
    tr_in = lambda a: jnp.swapaxes(a, 1, 2)
    big_w = dict(zip(big_names, (tr_in(w_in), w_out, w_mlp_in, w_mlp_out)))
    big_m = dict(zip(big_names, (tr_in(m_w_in), m_w_out, m_w_mlp_in, m_w_mlp_out)))
    big_v = dict(zip(big_names, (tr_in(v_w_in), v_w_out, v_w_mlp_in, v_w_mlp_out)))

    keys = []
    for l in range(n_layers):
        keys += [(l, nm) for nm in big_names]
    keys.insert(1, (0, "conv_w"))
    first_src = big_w[keys[0][1]][keys[0][0]].astype(MXU_DTYPE)
    first_flights, first_token = _exchange_start([first_src], [_own_in_place(first_src, me)], "gather", name="gather_start_first")
    zero = first_token[0, 0]
    srcs = [_pack([conv_w]) + zero if nm == "conv_w" else (big_w[nm][l] + zero).astype(MXU_DTYPE) for l, nm in keys[1:]]
    flights, gather_token = _exchange_start(srcs, [_own_in_place(s, me) for s in srcs], "gather", name="gather_start")
    arriving = dict(zip(keys, first_flights + flights))
    forwarding = {}
    relayout = dict(
        w_in=lambda g: g.reshape(DP, D), w_out=lambda g: g.reshape(DMIX, D),
        w_mlp_in=lambda g: g, w_mlp_out=lambda g: g.reshape(DFF, D),
        conv_w=lambda g: jnp.stack([_unpack(g[j], [conv_w.shape])[0] for j in range(N_DEV)], axis=2).reshape(
            n_layers, CONV_WIDTH, BW))
    gathered = {}

    def forward(key, after):
        _, land = _exchange_wait(arriving[key], after, "gather", name="gather_arrive_%d_%s" % key)
        fl, token = _exchange_start(None, [land], "forward", name="gather_forward_%d_%s" % key)
        forwarding[key] = fl[0]
        return token

    def weight_getter(l):
        def getw(nm, after):
            key = (0, nm) if nm == "conv_w" else (l, nm)
            if key not in gathered:
                ahead = keys[keys.index(key):][:2]
                for k in ahead:
                    if k not in forwarding:
                        after = forward(k, after)
                _, land = _exchange_wait(forwarding[key], after, "forward", name="gather_wait_%d_%s" % key)
                gathered[key] = relayout[nm](land)
            return gathered[key][l] if nm == "conv_w" else gathered[key]
        return getw

    tril = jnp.tril(jnp.ones((CHUNK, CHUNK), F32))
    layers = []
    for l in range(n_layers):
        w_tril = sgu_w[l] * tril
        layers.append(dict(
            attn_norm=attn_norm[l][None], mlp_norm=mlp_norm[l][None],
            sgu_tril=w_tril.astype(MXU_DTYPE), sgu_tril_t=w_tril.transpose(0, 2, 1).astype(MXU_DTYPE),
            sgu_bmat=jnp.repeat(sgu_b[l].T, HEAD_DIM, axis=1),
            qk_gain=jnp.concatenate([jnp.tile(q_norm[l], 3 * HP), jnp.tile(k_norm[l], 3 * HP)])[None]))

    xs, normed = x[0], None
    saved = []
    for l in range(n_layers):
        last = l == n_layers - 1
        out, sv = _layer_fwd(xs, layers[l], weight_getter(l), dims, dep=gather_token if l == 0 else None, normed=normed,
                             next_gain=None if last else layers[l + 1]["attn_norm"],
                             loss_target=loss_target[0] if last else None)
        saved.append(sv)
        if not last:
            xs, normed = out[0], (out[1], out[2])
    loss_blk, dx, dxb = out
    loss = lax.psum(loss_blk[0, 0], ("x", "y", "c"))

    scattering = {}

    def scatter_starter(l):
        def scatter(nm, partials):
            land = lax.empty((N_PEER,) + partials.shape[1:], partials.dtype)
            fl, tok = _exchange_start([partials], [land], "scatter", name="scatter_start_%d_%s" % (l, nm))
            scattering[(l, nm)] = fl[0]
            return tok
        return scatter

    small = [None] * n_layers
    for l in reversed(range(n_layers)):
        dx, dxb, small[l] = _layer_bwd(dx, dxb, layers[l], weight_getter(l), scatter_starter(l), saved[l], dims)

    small_shapes = [s.shape for s in small[0]]
    small_src = [_pack([s for l in range(n_layers) for s in small[l]])]
    small_flights, small_token = _exchange_start(small_src, [_own_in_place(s, me) for s in small_src], "gather_all",
                                                 name="small_start")
    grad_x = dx[None]

    me1 = me.astype(jnp.int32).reshape(1)
    res = {nm: None for nm in big_names}
    after = small_token
    for l in reversed(range(n_layers)):
        for nm in reversed(big_names):
            own, landed = _exchange_wait(scattering[(l, nm)], after, "scatter", name="scatter_wait_%d_%s" % (l, nm))
            res[nm] = _adamw_layer(l, big_w[nm], big_m[nm], big_v[nm], own, landed, me1, res[nm], "adamw_" + nm)
            after = res[nm][0]
    res["w_in"] = [tr_in(a) for a in res["w_in"]]
    big_out = [res[nm] for nm in big_names]

    _, gathered_small = _exchange_wait(small_flights[0], after, "gather_all", name="small_wait")
    summed = _unpack(_sum_parts(gathered_small, "sum_small"), small_shapes * n_layers)
    ns = len(small_shapes)
    g_small = [jnp.stack([summed[l * ns + i] for l in range(n_layers)]) for i in range(ns)]
    g_attn_norm, g_sgu_w, g_sgu_b, g_conv_full, g_q, g_k, g_mlp_norm = g_small
    cs = conv_w.shape[2]
    g_conv = lax.dynamic_slice_in_dim(g_conv_full, me * cs, cs, axis=2)
    sm_w = (attn_norm, sgu_w, sgu_b, conv_w, q_norm, k_norm, mlp_norm)
    sm_m = (m_attn_norm, m_sgu_w, m_sgu_b, m_conv_w, m_q_norm, m_k_norm, m_mlp_norm)
    sm_v = (v_attn_norm, v_sgu_w, v_sgu_b, v_conv_w, v_q_norm, v_k_norm, v_mlp_norm)
    sm_g = (g_attn_norm, g_sgu_w, g_sgu_b, g_conv, g_q, g_k, g_mlp_norm)
    sm_delta, sm_m2, sm_v2 = _adamw_small(sm_w, sm_g, sm_m, sm_v, "adamw_small")

    def ordered(small_list, big_kind):
        b = [big_out[i][big_kind] for i in range(4)]
        return [small_list[0], b[0], small_list[1], small_list[2], small_list[3], small_list[4], small_list[5],
                b[1], small_list[6], b[2], b[3]]

    return (loss, grad_x, *ordered(list(sm_g), 0), *ordered(sm_delta, 1), *ordered(sm_m2, 2), *ordered(sm_v2, 3))
```
